```python
import jax, jax.numpy as jnp
from jax import lax
import numpy as np

D_MODEL = 1024
BATCH = 8
SEQ = 4096
DEPTH = 1

PLE_DIM = 256
D_FF = 2816
D_MIX = D_MODEL
CHUNK = 128
GM_HEADS = 4
GM_HEAD_DIM = 128
GM_WIDTH = GM_HEADS * GM_HEAD_DIM
SB_HEADS = 8
SB_HEAD_DIM = 64
SB_WIDTH = SB_HEADS * SB_HEAD_DIM
SB_BLOCK = 128
MIX_IN_WIDTH = 2 * GM_WIDTH + 3 * SB_WIDTH
EPS = 1e-6

kernel_name = "hybrid_gmlp_stickbreaking_macaron_block"


def rms_norm(x, g):
    xf = x.astype(jnp.float32)
    y = xf * lax.rsqrt(jnp.mean(xf * xf, axis=-1, keepdims=True) + EPS)
    return (y * g.astype(jnp.float32)).astype(x.dtype)


def swiglu(x, w_in, w_out):
    gate, up = jnp.split(x @ w_in, 2, axis=-1)
    return (jax.nn.silu(gate) * up) @ w_out


def chunked_gmlp(u, v, v_gain, w_s, b_s):
    B, S, _ = u.shape
    nc = S // CHUNK
    vn = rms_norm(v, v_gain).reshape(B, nc, CHUNK, GM_HEADS, GM_HEAD_DIM)
    causal = jnp.tril(jnp.ones((CHUNK, CHUNK), dtype=bool))
    w = jnp.where(causal[None], w_s, jnp.zeros_like(w_s)).astype(vn.dtype)
    sv = jnp.einsum('hts,bcshd->bcthd', w, vn) + b_s.T.astype(vn.dtype)[None, None, :, :, None]
    return u * sv.reshape(B, S, GM_WIDTH)


def stick_breaking_attention(q, k, v):
    B, S, H, D = q.shape
    scale = D ** -0.5
    outs = []
    for i in range(S // SB_BLOCK):
        q0 = i * SB_BLOCK
        L = q0 + SB_BLOCK
        qb = q[:, q0:L]
        kp = k[:, :L]
        vp = v[:, :L]
        z = jnp.einsum('bthd,bshd->bhts', qb, kp).astype(jnp.float32) * scale
        t_idx = q0 + jnp.arange(SB_BLOCK)[:, None]
        s_idx = jnp.arange(L)[None, :]
        causal = s_idx < t_idx
        log_1m = jnp.where(causal, -jax.nn.softplus(z), 0.0)
        after = lax.cumsum(log_1m, axis=3, reverse=True) - log_1m
        a = jnp.where(causal, jnp.exp(jax.nn.log_sigmoid(z) + after), 0.0)
        outs.append(jnp.einsum('bhts,bshd->bthd', a.astype(vp.dtype), vp))
    return jnp.concatenate(outs, axis=1)


def _fwd_setup_inputs(seed: int = 0) -> dict:
    key = jax.random.key(seed)
    ks = jax.random.split(key, 18)
    f32 = jnp.float32

    def nrm(k, shape, fan_in):
        return jax.random.normal(k, shape, f32) * (fan_in ** -0.5)

    def gain(k, shape):
        return jnp.ones(shape, f32) + 0.05 * jax.random.normal(k, shape, f32)

    return {
        "x": jax.random.normal(ks[0], (BATCH, SEQ, D_MODEL), f32),
        "p": jax.random.normal(ks[1], (DEPTH, BATCH, SEQ, PLE_DIM), f32),
        "ffn1_norm": gain(ks[2], (DEPTH, D_MODEL)),
        "ffn1_w_in": nrm(ks[3], (DEPTH, D_MODEL, 2 * D_FF), D_MODEL),
        "ffn1_w_out": nrm(ks[4], (DEPTH, D_FF, D_MODEL), D_FF),
        "mix_norm": gain(ks[5], (DEPTH, D_MODEL)),
        "w_mix_in": nrm(ks[6], (DEPTH, D_MODEL, MIX_IN_WIDTH), D_MODEL),
        "gmlp_v_norm": gain(ks[7], (DEPTH, GM_WIDTH)),
        "gmlp_w_s": nrm(ks[8], (DEPTH, GM_HEADS, CHUNK, CHUNK), CHUNK),
        "gmlp_b": jnp.ones((DEPTH, GM_HEADS, CHUNK), f32) + 0.1 * jax.random.normal(ks[9], (DEPTH, GM_HEADS, CHUNK), f32),
        "w_mix_out": nrm(ks[10], (DEPTH, D_MIX, D_MODEL), D_MIX),
        "ffn2_norm": gain(ks[11], (DEPTH, D_MODEL)),
        "ffn2_w_in": nrm(ks[12], (DEPTH, D_MODEL, 2 * D_FF), D_MODEL),
        "ffn2_w_out": nrm(ks[13], (DEPTH, D_FF, D_MODEL), D_FF),
        "ple_norm": gain(ks[14], (DEPTH, D_MODEL)),
        "ple_w_gate": nrm(ks[15], (DEPTH, D_MODEL, D_MODEL), D_MODEL),
        "ple_w_proj": nrm(ks[16], (DEPTH, PLE_DIM, D_MODEL), PLE_DIM),
        "final_norm": gain(ks[17], (D_MODEL,)),
    }


def _fwd_reference(x, p, ffn1_norm, ffn1_w_in, ffn1_w_out, mix_norm, w_mix_in, gmlp_v_norm,
              gmlp_w_s, gmlp_b, w_mix_out, ffn2_norm, ffn2_w_in, ffn2_w_out,
              ple_norm, ple_w_gate, ple_w_proj, final_norm):
    B, S, _ = x.shape
    splits = [GM_WIDTH, 2 * GM_WIDTH, 2 * GM_WIDTH + SB_WIDTH, 2 * GM_WIDTH + 2 * SB_WIDTH]
    h = x
    for i in range(DEPTH):
        h = h + 0.5 * swiglu(rms_norm(h, ffn1_norm[i]), ffn1_w_in[i], ffn1_w_out[i])

        n = rms_norm(h, mix_norm[i])
        zmix = n @ w_mix_in[i]
        gm_u, gm_v, sb_q, sb_k, sb_v = jnp.split(zmix, splits, axis=-1)
        gm_out = chunked_gmlp(jax.nn.gelu(gm_u, approximate=False),
                              jax.nn.gelu(gm_v, approximate=False),
                              gmlp_v_norm[i], gmlp_w_s[i], gmlp_b[i])
        sb_out = stick_breaking_attention(sb_q.reshape(B, S, SB_HEADS, SB_HEAD_DIM),
                                          sb_k.reshape(B, S, SB_HEADS, SB_HEAD_DIM),
                                          sb_v.reshape(B, S, SB_HEADS, SB_HEAD_DIM))
        mixed = jnp.concatenate([gm_out, sb_out.reshape(B, S, SB_WIDTH)], axis=-1)
        h = h + mixed @ w_mix_out[i]

        h = h + 0.5 * swiglu(rms_norm(h, ffn2_norm[i]), ffn2_w_in[i], ffn2_w_out[i])

        gate = jax.nn.sigmoid(rms_norm(h, ple_norm[i]) @ ple_w_gate[i])
        h = h + gate * (p[i] @ ple_w_proj[i])
    return rms_norm(h, final_norm)


import jax as _jax
import jax.numpy as _jnp

TWIN_FORMAT = 'train_step'
FWD_PARAMS = ['x', 'p', 'ffn1_norm', 'ffn1_w_in', 'ffn1_w_out', 'mix_norm', 'w_mix_in', 'gmlp_v_norm', 'gmlp_w_s', 'gmlp_b', 'w_mix_out', 'ffn2_norm', 'ffn2_w_in', 'ffn2_w_out', 'ple_norm', 'ple_w_gate', 'ple_w_proj', 'final_norm']
TWIN_WEIGHTS = ['ffn1_norm', 'ffn1_w_in', 'ffn1_w_out', 'mix_norm', 'w_mix_in', 'gmlp_v_norm', 'gmlp_w_s', 'gmlp_b', 'w_mix_out', 'ffn2_norm', 'ffn2_w_in', 'ffn2_w_out', 'ple_norm', 'ple_w_gate', 'ple_w_proj', 'final_norm']
TWIN_DIFF_INPUT = 'x'
TWIN_INPUTS = ['x', 'p', 'ffn1_norm', 'ffn1_w_in', 'ffn1_w_out', 'mix_norm', 'w_mix_in', 'gmlp_v_norm', 'gmlp_w_s', 'gmlp_b', 'w_mix_out', 'ffn2_norm', 'ffn2_w_in', 'ffn2_w_out', 'ple_norm', 'ple_w_gate', 'ple_w_proj', 'final_norm', 'loss_target', 'm_ffn1_norm', 'm_ffn1_w_in', 'm_ffn1_w_out', 'm_mix_norm', 'm_w_mix_in', 'm_gmlp_v_norm', 'm_gmlp_w_s', 'm_gmlp_b', 'm_w_mix_out', 'm_ffn2_norm', 'm_ffn2_w_in', 'm_ffn2_w_out', 'm_ple_norm', 'm_ple_w_gate', 'm_ple_w_proj', 'm_final_norm', 'v_ffn1_norm', 'v_ffn1_w_in', 'v_ffn1_w_out', 'v_mix_norm', 'v_w_mix_in', 'v_gmlp_v_norm', 'v_gmlp_w_s', 'v_gmlp_b', 'v_w_mix_out', 'v_ffn2_norm', 'v_ffn2_w_in', 'v_ffn2_w_out', 'v_ple_norm', 'v_ple_w_gate', 'v_ple_w_proj', 'v_final_norm']
TWIN_OUTPUTS = ['loss', 'grad_x', 'grad_ffn1_norm', 'grad_ffn1_w_in', 'grad_ffn1_w_out', 'grad_mix_norm', 'grad_w_mix_in', 'grad_gmlp_v_norm', 'grad_gmlp_w_s', 'grad_gmlp_b', 'grad_w_mix_out', 'grad_ffn2_norm', 'grad_ffn2_w_in', 'grad_ffn2_w_out', 'grad_ple_norm', 'grad_ple_w_gate', 'grad_ple_w_proj', 'grad_final_norm', 'delta_ffn1_norm', 'delta_ffn1_w_in', 'delta_ffn1_w_out', 'delta_mix_norm', 'delta_w_mix_in', 'delta_gmlp_v_norm', 'delta_gmlp_w_s', 'delta_gmlp_b', 'delta_w_mix_out', 'delta_ffn2_norm', 'delta_ffn2_w_in', 'delta_ffn2_w_out', 'delta_ple_norm', 'delta_ple_w_gate', 'delta_ple_w_proj', 'delta_final_norm', 'new_m_ffn1_norm', 'new_m_ffn1_w_in', 'new_m_ffn1_w_out', 'new_m_mix_norm', 'new_m_w_mix_in', 'new_m_gmlp_v_norm', 'new_m_gmlp_w_s', 'new_m_gmlp_b', 'new_m_w_mix_out', 'new_m_ffn2_norm', 'new_m_ffn2_w_in', 'new_m_ffn2_w_out', 'new_m_ple_norm', 'new_m_ple_w_gate', 'new_m_ple_w_proj', 'new_m_final_norm', 'new_v_ffn1_norm', 'new_v_ffn1_w_in', 'new_v_ffn1_w_out', 'new_v_mix_norm', 'new_v_w_mix_in', 'new_v_gmlp_v_norm', 'new_v_gmlp_w_s', 'new_v_gmlp_b', 'new_v_w_mix_out', 'new_v_ffn2_norm', 'new_v_ffn2_w_in', 'new_v_ffn2_w_out', 'new_v_ple_norm', 'new_v_ple_w_gate', 'new_v_ple_w_proj', 'new_v_final_norm']
TWIN_LEAF_KINDS = {'loss': 'loss', 'grad_x': 'grad_x', 'grad_ffn1_norm': 'grad_w', 'grad_ffn1_w_in': 'grad_w', 'grad_ffn1_w_out': 'grad_w', 'grad_mix_norm': 'grad_w', 'grad_w_mix_in': 'grad_w', 'grad_gmlp_v_norm': 'grad_w', 'grad_gmlp_w_s': 'grad_w', 'grad_gmlp_b': 'grad_w', 'grad_w_mix_out': 'grad_w', 'grad_ffn2_norm': 'grad_w', 'grad_ffn2_w_in': 'grad_w', 'grad_ffn2_w_out': 'grad_w', 'grad_ple_norm': 'grad_w', 'grad_ple_w_gate': 'grad_w', 'grad_ple_w_proj': 'grad_w', 'grad_final_norm': 'grad_w', 'delta_ffn1_norm': 'delta_w', 'delta_ffn1_w_in': 'delta_w', 'delta_ffn1_w_out': 'delta_w', 'delta_mix_norm': 'delta_w', 'delta_w_mix_in': 'delta_w', 'delta_gmlp_v_norm': 'delta_w', 'delta_gmlp_w_s': 'delta_w', 'delta_gmlp_b': 'delta_w', 'delta_w_mix_out': 'delta_w', 'delta_ffn2_norm': 'delta_w', 'delta_ffn2_w_in': 'delta_w', 'delta_ffn2_w_out': 'delta_w', 'delta_ple_norm': 'delta_w', 'delta_ple_w_gate': 'delta_w', 'delta_ple_w_proj': 'delta_w', 'delta_final_norm': 'delta_w', 'new_m_ffn1_norm': 'new_m', 'new_m_ffn1_w_in': 'new_m', 'new_m_ffn1_w_out': 'new_m', 'new_m_mix_norm': 'new_m', 'new_m_w_mix_in': 'new_m', 'new_m_gmlp_v_norm': 'new_m', 'new_m_gmlp_w_s': 'new_m', 'new_m_gmlp_b': 'new_m', 'new_m_w_mix_out': 'new_m', 'new_m_ffn2_norm': 'new_m', 'new_m_ffn2_w_in': 'new_m', 'new_m_ffn2_w_out': 'new_m', 'new_m_ple_norm': 'new_m', 'new_m_ple_w_gate': 'new_m', 'new_m_ple_w_proj': 'new_m', 'new_m_final_norm': 'new_m', 'new_v_ffn1_norm': 'new_v', 'new_v_ffn1_w_in': 'new_v', 'new_v_ffn1_w_out': 'new_v', 'new_v_mix_norm': 'new_v', 'new_v_w_mix_in': 'new_v', 'new_v_gmlp_v_norm': 'new_v', 'new_v_gmlp_w_s': 'new_v', 'new_v_gmlp_b': 'new_v', 'new_v_w_mix_out': 'new_v', 'new_v_ffn2_norm': 'new_v', 'new_v_ffn2_w_in': 'new_v', 'new_v_ffn2_w_out': 'new_v', 'new_v_ple_norm': 'new_v', 'new_v_ple_w_gate': 'new_v', 'new_v_ple_w_proj': 'new_v', 'new_v_final_norm': 'new_v'}


def _forward(args):
    return _fwd_reference(*[args[k] for k in FWD_PARAMS])


def _output_shape():
    out = _jax.eval_shape(lambda: _forward(_fwd_setup_inputs(0)))
    return out.shape, out.dtype

N_MICROBATCH = 1
ADAM_LR = 0.001
ADAM_B1 = 0.9
ADAM_B2 = 0.999
ADAM_EPS = 1e-08
ADAM_WD = 0.01
ADAM_STEP = 10
PER_EXAMPLE_BATCH_AXIS = {'x': 0, 'p': 1, 'loss_target': 0}
SHARED_INPUTS = []
_WEIGHT_DTYPES = {'ffn1_norm': _jnp.float32, 'ffn1_w_in': _jnp.float32, 'ffn1_w_out': _jnp.float32, 'mix_norm': _jnp.float32, 'w_mix_in': _jnp.float32, 'gmlp_v_norm': _jnp.float32, 'gmlp_w_s': _jnp.float32, 'gmlp_b': _jnp.float32, 'w_mix_out': _jnp.float32, 'ffn2_norm': _jnp.float32, 'ffn2_w_in': _jnp.float32, 'ffn2_w_out': _jnp.float32, 'ple_norm': _jnp.float32, 'ple_w_gate': _jnp.float32, 'ple_w_proj': _jnp.float32, 'final_norm': _jnp.float32}
MOMENT_SCALE = {'ffn1_norm': 7.834267e-02, 'ffn1_w_in': 3.222374e-02, 'ffn1_w_out': 5.267445e-02, 'mix_norm': 1.219017e-01, 'w_mix_in': 7.448759e-02, 'gmlp_v_norm': 6.183799e-02, 'gmlp_w_s': 6.182761e-02, 'gmlp_b': 9.056598e-02, 'w_mix_out': 1.057620e-01, 'ffn2_norm': 6.043948e-02, 'ffn2_w_in': 2.362679e-02, 'ffn2_w_out': 3.864394e-02, 'ple_norm': 2.947740e-02, 'ple_w_gate': 2.956649e-02, 'ple_w_proj': 6.919639e-02, 'final_norm': 3.214503e+01}


def _to_microbatches(a, axis):
    t = _jnp.moveaxis(a, axis, 0)
    t = t.reshape((N_MICROBATCH, t.shape[0] // N_MICROBATCH) + t.shape[1:])
    return _jnp.moveaxis(t, 1, axis + 1)


def setup_inputs(seed: int = 0) -> dict:
    inp = _fwd_setup_inputs(seed)
    key = _jax.random.fold_in(_jax.random.key(seed), 7919)
    shape, _ = _output_shape()
    out = dict(inp)
    out["loss_target"] = _jax.random.normal(_jax.random.fold_in(key, 0), shape, _jnp.float32)
    for i, name in enumerate(TWIN_WEIGHTS):
        w = inp[name].astype(_jnp.float32)
        if MOMENT_SCALE is None:
            s = _jnp.sqrt(_jnp.mean(_jnp.square(w)) + 1e-30)
        else:
            s = MOMENT_SCALE[name]
        km, kv = _jax.random.split(_jax.random.fold_in(key, i + 1))
        out[name] = w
        out["m_" + name] = s * _jax.random.normal(km, w.shape, _jnp.float32)
        out["v_" + name] = (s * s) * _jax.random.uniform(kv, w.shape, _jnp.float32, 0.5, 1.5)
    if N_MICROBATCH > 1:
        for name, axis in PER_EXAMPLE_BATCH_AXIS.items():
            out[name] = _to_microbatches(out[name], axis)
    return {'x': out['x'], 'p': out['p'], 'ffn1_norm': out['ffn1_norm'], 'ffn1_w_in': out['ffn1_w_in'], 'ffn1_w_out': out['ffn1_w_out'], 'mix_norm': out['mix_norm'], 'w_mix_in': out['w_mix_in'], 'gmlp_v_norm': out['gmlp_v_norm'], 'gmlp_w_s': out['gmlp_w_s'], 'gmlp_b': out['gmlp_b'], 'w_mix_out': out['w_mix_out'], 'ffn2_norm': out['ffn2_norm'], 'ffn2_w_in': out['ffn2_w_in'], 'ffn2_w_out': out['ffn2_w_out'], 'ple_norm': out['ple_norm'], 'ple_w_gate': out['ple_w_gate'], 'ple_w_proj': out['ple_w_proj'], 'final_norm': out['final_norm'], 'loss_target': out['loss_target'], 'm_ffn1_norm': out['m_ffn1_norm'], 'm_ffn1_w_in': out['m_ffn1_w_in'], 'm_ffn1_w_out': out['m_ffn1_w_out'], 'm_mix_norm': out['m_mix_norm'], 'm_w_mix_in': out['m_w_mix_in'], 'm_gmlp_v_norm': out['m_gmlp_v_norm'], 'm_gmlp_w_s': out['m_gmlp_w_s'], 'm_gmlp_b': out['m_gmlp_b'], 'm_w_mix_out': out['m_w_mix_out'], 'm_ffn2_norm': out['m_ffn2_norm'], 'm_ffn2_w_in': out['m_ffn2_w_in'], 'm_ffn2_w_out': out['m_ffn2_w_out'], 'm_ple_norm': out['m_ple_norm'], 'm_ple_w_gate': out['m_ple_w_gate'], 'm_ple_w_proj': out['m_ple_w_proj'], 'm_final_norm': out['m_final_norm'], 'v_ffn1_norm': out['v_ffn1_norm'], 'v_ffn1_w_in': out['v_ffn1_w_in'], 'v_ffn1_w_out': out['v_ffn1_w_out'], 'v_mix_norm': out['v_mix_norm'], 'v_w_mix_in': out['v_w_mix_in'], 'v_gmlp_v_norm': out['v_gmlp_v_norm'], 'v_gmlp_w_s': out['v_gmlp_w_s'], 'v_gmlp_b': out['v_gmlp_b'], 'v_w_mix_out': out['v_w_mix_out'], 'v_ffn2_norm': out['v_ffn2_norm'], 'v_ffn2_w_in': out['v_ffn2_w_in'], 'v_ffn2_w_out': out['v_ffn2_w_out'], 'v_ple_norm': out['v_ple_norm'], 'v_ple_w_gate': out['v_ple_w_gate'], 'v_ple_w_proj': out['v_ple_w_proj'], 'v_final_norm': out['v_final_norm']}


def _loss(weights, diff, rest, loss_target):
    with _jax.named_scope("forward"):
        args = {**rest, TWIN_DIFF_INPUT: diff, **{k: w.astype(_WEIGHT_DTYPES[k]) for k, w in weights.items()}}
        y = _forward(args)
    with _jax.named_scope("loss_head"):
        err = _jnp.square(y.astype(_jnp.float32) - loss_target)
        return 0.5 * _jnp.sum(_jnp.mean(err, axis=-1)) if err.ndim else 0.5 * err


def _adamw(w, g, m, v):
    m = ADAM_B1 * m + (1.0 - ADAM_B1) * g
    v = ADAM_B2 * v + (1.0 - ADAM_B2) * _jnp.square(g)
    m_hat = m / (1.0 - ADAM_B1 ** ADAM_STEP)
    v_hat = v / (1.0 - ADAM_B2 ** ADAM_STEP)
    delta = -ADAM_LR * (m_hat / (_jnp.sqrt(v_hat) + ADAM_EPS) + ADAM_WD * w)
    return delta, m, v


def reference(x, p, ffn1_norm, ffn1_w_in, ffn1_w_out, mix_norm, w_mix_in, gmlp_v_norm, gmlp_w_s, gmlp_b, w_mix_out, ffn2_norm, ffn2_w_in, ffn2_w_out, ple_norm, ple_w_gate, ple_w_proj, final_norm, loss_target, m_ffn1_norm, m_ffn1_w_in, m_ffn1_w_out, m_mix_norm, m_w_mix_in, m_gmlp_v_norm, m_gmlp_w_s, m_gmlp_b, m_w_mix_out, m_ffn2_norm, m_ffn2_w_in, m_ffn2_w_out, m_ple_norm, m_ple_w_gate, m_ple_w_proj, m_final_norm, v_ffn1_norm, v_ffn1_w_in, v_ffn1_w_out, v_mix_norm, v_w_mix_in, v_gmlp_v_norm, v_gmlp_w_s, v_gmlp_b, v_w_mix_out, v_ffn2_norm, v_ffn2_w_in, v_ffn2_w_out, v_ple_norm, v_ple_w_gate, v_ple_w_proj, v_final_norm):
    given = dict(x=x, p=p, ffn1_norm=ffn1_norm, ffn1_w_in=ffn1_w_in, ffn1_w_out=ffn1_w_out, mix_norm=mix_norm, w_mix_in=w_mix_in, gmlp_v_norm=gmlp_v_norm, gmlp_w_s=gmlp_w_s, gmlp_b=gmlp_b, w_mix_out=w_mix_out, ffn2_norm=ffn2_norm, ffn2_w_in=ffn2_w_in, ffn2_w_out=ffn2_w_out, ple_norm=ple_norm, ple_w_gate=ple_w_gate, ple_w_proj=ple_w_proj, final_norm=final_norm, loss_target=loss_target, m_ffn1_norm=m_ffn1_norm, m_ffn1_w_in=m_ffn1_w_in, m_ffn1_w_out=m_ffn1_w_out, m_mix_norm=m_mix_norm, m_w_mix_in=m_w_mix_in, m_gmlp_v_norm=m_gmlp_v_norm, m_gmlp_w_s=m_gmlp_w_s, m_gmlp_b=m_gmlp_b, m_w_mix_out=m_w_mix_out, m_ffn2_norm=m_ffn2_norm, m_ffn2_w_in=m_ffn2_w_in, m_ffn2_w_out=m_ffn2_w_out, m_ple_norm=m_ple_norm, m_ple_w_gate=m_ple_w_gate, m_ple_w_proj=m_ple_w_proj, m_final_norm=m_final_norm, v_ffn1_norm=v_ffn1_norm, v_ffn1_w_in=v_ffn1_w_in, v_ffn1_w_out=v_ffn1_w_out, v_mix_norm=v_mix_norm, v_w_mix_in=v_w_mix_in, v_gmlp_v_norm=v_gmlp_v_norm, v_gmlp_w_s=v_gmlp_w_s, v_gmlp_b=v_gmlp_b, v_w_mix_out=v_w_mix_out, v_ffn2_norm=v_ffn2_norm, v_ffn2_w_in=v_ffn2_w_in, v_ffn2_w_out=v_ffn2_w_out, v_ple_norm=v_ple_norm, v_ple_w_gate=v_ple_w_gate, v_ple_w_proj=v_ple_w_proj, v_final_norm=v_final_norm)
    weights = {n: given[n] for n in TWIN_WEIGHTS}
    shared = {n: given[n] for n in SHARED_INPUTS}
    per_example = {n: given[n] for n in ['x', 'p']}
    grad_fn = _jax.value_and_grad(_loss, argnums=(0, 1))

    def one_microbatch(ex, loss_target):
        ex = dict(ex)
        diff = ex.pop(TWIN_DIFF_INPUT)
        return grad_fn(weights, diff, {**shared, **ex}, loss_target)

    if N_MICROBATCH == 1:
        loss, (grad_w, grad_x) = one_microbatch(per_example, given["loss_target"])
    else:
        def body(carry, xs):
            loss_sum, grad_sum = carry
            l_k, (gw_k, gx_k) = one_microbatch(xs[0], xs[1])
            with _jax.named_scope("update"):
                return (loss_sum + l_k, _jax.tree.map(_jnp.add, grad_sum, gw_k)), gx_k

        init = (_jnp.zeros((), _jnp.float32), _jax.tree.map(_jnp.zeros_like, weights))
        (loss, grad_w), grad_x = _jax.lax.scan(body, init, (per_example, given["loss_target"]))
    with _jax.named_scope("update"):
        delta_w, new_m, new_v = {}, {}, {}
        for n in TWIN_WEIGHTS:
            delta_w[n], new_m[n], new_v[n] = _adamw(weights[n], grad_w[n], given["m_" + n], given["v_" + n])
    return (loss, grad_x, *[grad_w[n] for n in TWIN_WEIGHTS], *[delta_w[n] for n in TWIN_WEIGHTS],
            *[new_m[n] for n in TWIN_WEIGHTS], *[new_v[n] for n in TWIN_WEIGHTS])
```

```python
import functools

import jax
import jax.numpy as jnp
from jax import lax
from jax.experimental import pallas as pl
from jax.experimental.pallas import tpu as pltpu

F32, BF16 = jnp.float32, jnp.bfloat16

D_MODEL = 1024
D_FF = 2816
FF_BLOCK = 2 * D_FF // 4
PLE_DIM = 256
CHUNK = 128
GM_HEADS = 4
GM_WIDTH = 512
SB_HEAD_DIM = 64
SB_WIDTH = 512
MIX_IN_WIDTH = 2 * GM_WIDTH + 3 * SB_WIDTH
MIX_BLOCK = MIX_IN_WIDTH // 4
EPS = 1e-6
N_CHIPS = 4
LANES = 128
ATT_BLOCK = 128
VMEM_LIMIT = 56 * 1024 * 1024

ADAM_LR, ADAM_B1, ADAM_B2, ADAM_EPS, ADAM_WD, ADAM_STEP = 0.001, 0.9, 0.999, 1e-08, 0.01, 10


def _dot(a, b):
    return jnp.dot(a, b, preferred_element_type=F32)


def _dot_nt(a, b):
    return lax.dot_general(a, b, (((1,), (1,)), ((), ())), preferred_element_type=F32)


def _dot_tn(a, b):
    return lax.dot_general(a, b, (((0,), (0,)), ((), ())), preferred_element_type=F32)


def _resident(shape):
    nd = len(shape)
    return pl.BlockSpec(shape, lambda *_: (0,) * nd, pipeline_mode=pl.Buffered(1))


def _rows(tm, width):
    return pl.BlockSpec((tm, width), lambda i: (i, 0))


def _params(n_axes=1):
    return pltpu.CompilerParams(dimension_semantics=("arbitrary",) * n_axes, vmem_limit_bytes=VMEM_LIMIT)


def _rstd(h):
    return lax.rsqrt(jnp.mean(h * h, axis=-1, keepdims=True) + EPS)


def _rms_bwd(dy, h, r, g):
    dyg = dy * g
    dh = r * dyg - h * (r * r * r) * jnp.mean(dyg * h, axis=-1, keepdims=True)
    return dh, dy * h * r


def _gelu(x):
    return 0.5 * x * (1.0 + lax.erf(x * (2.0 ** -0.5)))


def _gelu_grad(x):
    return 0.5 * (1.0 + lax.erf(x * (2.0 ** -0.5))) + x * jnp.exp(-0.5 * x * x) * ((2.0 * jnp.pi) ** -0.5)


def _token_tile(t):
    return min(256, t)


def _ffn_fwd(h, g, win, wout, name):
    t = h.shape[0]
    tm = _token_tile(t)

    def body(h_ref, g_ref, win_ref, wout_ref, ho_ref, gu_ref):
        hh = h_ref[...]
        n = (hh * _rstd(hh) * g_ref[...]).astype(BF16)
        acc = jnp.zeros((tm, D_MODEL), F32)
        for jb in range(2):
            gate = _dot(n, win_ref[jb])
            up = _dot(n, win_ref[2 + jb])
            gu_ref[:, jb * FF_BLOCK:(jb + 1) * FF_BLOCK] = gate.astype(BF16)
            gu_ref[:, D_FF + jb * FF_BLOCK:D_FF + (jb + 1) * FF_BLOCK] = up.astype(BF16)
            act = (gate * jax.nn.sigmoid(gate) * up).astype(BF16)
            acc = acc + _dot(act, wout_ref[jb * FF_BLOCK:(jb + 1) * FF_BLOCK, :])
        ho_ref[...] = hh + 0.5 * acc

    return pl.pallas_call(
        body, name=name, grid=(t // tm,),
        in_specs=[_rows(tm, D_MODEL), _resident((1, D_MODEL)), _resident(win.shape), _resident(wout.shape)],
        out_specs=[_rows(tm, D_MODEL), _rows(tm, 2 * D_FF)],
        out_shape=[jax.ShapeDtypeStruct((t, D_MODEL), F32), jax.ShapeDtypeStruct((t, 2 * D_FF), BF16)],
        compiler_params=_params(),
    )(h, g, win, wout)


def _ffn_bwd(dho, h, g, gu, win, wout, name):
    t = h.shape[0]
    tm = _token_tile(t)

    def body(dho_ref, h_ref, g_ref, gu_ref, win_ref, wout_ref, dh_ref, dgu_ref, n_ref, act_ref, dhh_ref, dg_ref):
        i = pl.program_id(0)
        hh = h_ref[...]
        gg = g_ref[...]
        r = _rstd(hh)
        n_ref[...] = (hh * r * gg).astype(BF16)
        dho = dho_ref[...]
        dhh = (0.5 * dho).astype(BF16)
        dhh_ref[...] = dhh
        dn = jnp.zeros((tm, D_MODEL), F32)
        for jb in range(2):
            cg = slice(jb * FF_BLOCK, (jb + 1) * FF_BLOCK)
            cu = slice(D_FF + jb * FF_BLOCK, D_FF + (jb + 1) * FF_BLOCK)
            dact = _dot_nt(dhh, wout_ref[cg, :])
            gate = gu_ref[:, cg].astype(F32)
            up = gu_ref[:, cu].astype(F32)
            sg = jax.nn.sigmoid(gate)
            silu = gate * sg
            act_ref[:, cg] = (silu * up).astype(BF16)
            dgate = (dact * up * (sg * (1.0 + gate * (1.0 - sg)))).astype(BF16)
            dup = (dact * silu).astype(BF16)
            dgu_ref[:, cg] = dgate
            dgu_ref[:, cu] = dup
            dn = dn + _dot_nt(dgate, win_ref[jb]) + _dot_nt(dup, win_ref[2 + jb])
        dh, dg_rows = _rms_bwd(dn, hh, r, gg)
        dh_ref[...] = dho + dh

        @pl.when(i == 0)
        def _():
            dg_ref[...] = jnp.zeros_like(dg_ref)

        dg_ref[...] += jnp.sum(dg_rows, axis=0, keepdims=True)

    return pl.pallas_call(
        body, name=name, grid=(t // tm,),
        in_specs=[_rows(tm, D_MODEL), _rows(tm, D_MODEL), _resident((1, D_MODEL)), _rows(tm, 2 * D_FF),
                  _resident(win.shape), _resident(wout.shape)],
        out_specs=[_rows(tm, D_MODEL), _rows(tm, 2 * D_FF), _rows(tm, D_MODEL), _rows(tm, D_FF), _rows(tm, D_MODEL),
                   pl.BlockSpec((1, D_MODEL), lambda i: (0, 0))],
        out_shape=[jax.ShapeDtypeStruct((t, D_MODEL), F32), jax.ShapeDtypeStruct((t, 2 * D_FF), BF16),
                   jax.ShapeDtypeStruct((t, D_MODEL), BF16), jax.ShapeDtypeStruct((t, D_FF), BF16),
                   jax.ShapeDtypeStruct((t, D_MODEL), BF16), jax.ShapeDtypeStruct((1, D_MODEL), F32)],
        compiler_params=_params(),
    )(dho, h, g, gu, win, wout)


def _wgrad(a, b, out_shape, out_block, out_index, a_width, b_width, grid_ij, name):
    t = a.shape[0]
    tk = min(512, t)

    def body(a_ref, b_ref, o_ref):
        k = pl.program_id(2)
        prod = _dot_tn(a_ref[...], b_ref[...]).reshape(o_ref.shape)

        @pl.when(k == 0)
        def _():
            o_ref[...] = prod

        @pl.when(k > 0)
        def _():
            o_ref[...] += prod

    return pl.pallas_call(
        body, name=name, grid=(*grid_ij, t // tk),
        in_specs=[pl.BlockSpec((tk, a_width), lambda i, j, k: (k, i)), pl.BlockSpec((tk, b_width), lambda i, j, k: (k, j))],
        out_specs=pl.BlockSpec(out_block, lambda i, j, k: out_index(i, j)),
        out_shape=jax.ShapeDtypeStruct(out_shape, F32),
        compiler_params=_params(3),
    )(a, b)


def _wgrad_cols(a, b, n_blocks, name):
    ka, nb = a.shape[1], b.shape[1] // n_blocks
    return _wgrad(a, b, (n_blocks, ka, nb), (1, ka, nb), lambda i, j: (j, 0, 0), ka, nb, (1, n_blocks), name)


def _wgrad_rows(a, b, n_blocks, name):
    ka, nb = a.shape[1] // n_blocks, b.shape[1]
    return _wgrad(a, b, (a.shape[1], nb), (ka, nb), lambda i, j: (i, 0), ka, nb, (n_blocks, 1), name)


def _mix_in_fwd(h, g, wmix):
    t = h.shape[0]
    tm = _token_tile(t)
    gw2 = 2 * GM_WIDTH

    def body(h_ref, g_ref, w_ref, zg_ref, qkv_ref):
        hh = h_ref[...]
        n = (hh * _rstd(hh) * g_ref[...]).astype(BF16)
        for b in range(N_CHIPS):
            z = _dot(n, w_ref[b])
            lo, hi = b * MIX_BLOCK, (b + 1) * MIX_BLOCK
            if hi <= gw2:
                zg_ref[:, lo:hi] = z
            elif lo >= gw2:
                qkv_ref[:, lo - gw2:hi - gw2] = z.astype(BF16)
            else:
                zg_ref[:, lo:gw2] = z[:, :gw2 - lo]
                qkv_ref[:, 0:hi - gw2] = z[:, gw2 - lo:].astype(BF16)

    return pl.pallas_call(
        body, name="mix_in_fwd", grid=(t // tm,),
        in_specs=[_rows(tm, D_MODEL), _resident((1, D_MODEL)), _resident(wmix.shape)],
        out_specs=[_rows(tm, gw2), _rows(tm, 3 * SB_WIDTH)],
        out_shape=[jax.ShapeDtypeStruct((t, gw2), F32), jax.ShapeDtypeStruct((t, 3 * SB_WIDTH), BF16)],
        compiler_params=_params(),
    )(h, g, wmix)


def _causal_chunk_mask():
    row = lax.broadcasted_iota(jnp.int32, (CHUNK, CHUNK), 0)
    col = lax.broadcasted_iota(jnp.int32, (CHUNK, CHUNK), 1)
    return row >= col


def _gmlp_tile(t):
    return min(512, t)


def _gmlp_fwd(zg, gv, ws, bt):
    t = zg.shape[0]
    tm = _gmlp_tile(t)

    def body(zg_ref, gv_ref, ws_ref, bt_ref, o_ref):
        u = _gelu(zg_ref[:, :GM_WIDTH])
        v = _gelu(zg_ref[:, GM_WIDTH:])
        vn = (v * _rstd(v) * gv_ref[...]).astype(BF16)
        mask = _causal_chunk_mask()
        for hd in range(GM_HEADS):
            wm = jnp.where(mask, ws_ref[hd], 0.0).astype(BF16)
            cols = slice(hd * CHUNK, (hd + 1) * CHUNK)
            for c in range(tm // CHUNK):
                rows = slice(c * CHUNK, (c + 1) * CHUNK)
                sv = _dot(wm, vn[rows, cols]) + bt_ref[:, hd:hd + 1]
                o_ref[rows, cols] = (u[rows, cols] * sv).astype(BF16)

    return pl.pallas_call(
        body, name="gmlp_fwd", grid=(t // tm,),
        in_specs=[_rows(tm, 2 * GM_WIDTH), _resident((1, GM_WIDTH)), _resident(ws.shape), _resident(bt.shape)],
        out_specs=_rows(tm, GM_WIDTH),
        out_shape=jax.ShapeDtypeStruct((t, GM_WIDTH), BF16),
        compiler_params=_params(),
    )(zg, gv, ws, bt)


def _gmlp_bwd(zg, dmixed, gv, ws, bt):
    t = zg.shape[0]
    tm = _gmlp_tile(t)

    def body(zg_ref, dgm_ref, gv_ref, ws_ref, bt_ref, dzg_ref, dws_ref, dbt_ref, dgv_ref):
        i = pl.program_id(0)

        @pl.when(i == 0)
        def _():
            dws_ref[...] = jnp.zeros_like(dws_ref)
            dbt_ref[...] = jnp.zeros_like(dbt_ref)
            dgv_ref[...] = jnp.zeros_like(dgv_ref)

        zu = zg_ref[:, :GM_WIDTH]
        zv = zg_ref[:, GM_WIDTH:]
        u = _gelu(zu)
        v = _gelu(zv)
        r = _rstd(v)
        gvv = gv_ref[...]
        vn = (v * r * gvv).astype(BF16)
        dgm = dgm_ref[...].astype(F32)
        dsv = (dgm * u).astype(BF16)
        mask = _causal_chunk_mask()
        du_cols, dvn_cols = [], []
        for hd in range(GM_HEADS):
            wm = jnp.where(mask, ws_ref[hd], 0.0).astype(BF16)
            cols = slice(hd * CHUNK, (hd + 1) * CHUNK)
            dw = jnp.zeros((CHUNK, CHUNK), F32)
            db = jnp.zeros((CHUNK, 1), F32)
            du_rows, dvn_rows = [], []
            for c in range(tm // CHUNK):
                rows = slice(c * CHUNK, (c + 1) * CHUNK)
                sv = _dot(wm, vn[rows, cols]) + bt_ref[:, hd:hd + 1]
                du_rows.append(dgm[rows, cols] * sv)
                dvn_rows.append(_dot_tn(wm, dsv[rows, cols]))
                dw = dw + _dot_nt(dsv[rows, cols], vn[rows, cols])
                db = db + jnp.sum(dsv[rows, cols].astype(F32), axis=1, keepdims=True)
            dws_ref[hd] += jnp.where(mask, dw, 0.0)
            dbt_ref[:, hd:hd + 1] += db
            du_cols.append(jnp.concatenate(du_rows, axis=0))
            dvn_cols.append(jnp.concatenate(dvn_rows, axis=0))
        du = jnp.concatenate(du_cols, axis=1)
        dvn = jnp.concatenate(dvn_cols, axis=1)
        dv, dgv_rows = _rms_bwd(dvn, v, r, gvv)
        dgv_ref[...] += jnp.sum(dgv_rows, axis=0, keepdims=True)
        dzg_ref[:, :GM_WIDTH] = (du * _gelu_grad(zu)).astype(BF16)
        dzg_ref[:, GM_WIDTH:] = (dv * _gelu_grad(zv)).astype(BF16)

    const = lambda nd: (lambda i: (0,) * nd)
    return pl.pallas_call(
        body, name="gmlp_bwd", grid=(t // tm,),
        in_specs=[_rows(tm, 2 * GM_WIDTH), _rows(tm, GM_WIDTH), _resident((1, GM_WIDTH)), _resident(ws.shape),
                  _resident(bt.shape)],
        out_specs=[_rows(tm, 2 * GM_WIDTH), pl.BlockSpec(ws.shape, const(3)), pl.BlockSpec(bt.shape, const(2)),
                   pl.BlockSpec((1, GM_WIDTH), const(2))],
        out_shape=[jax.ShapeDtypeStruct((t, 2 * GM_WIDTH), BF16), jax.ShapeDtypeStruct(ws.shape, F32),
                   jax.ShapeDtypeStruct(bt.shape, F32), jax.ShapeDtypeStruct((1, GM_WIDTH), F32)],
        compiler_params=_params(),
    )(zg, dmixed, gv, ws, bt)


def _att_masks():
    tb = ATT_BLOCK
    lane = lax.broadcasted_iota(jnp.int32, (1, LANES), 1)
    rj = lax.broadcasted_iota(jnp.int32, (2 * tb, 2 * tb), 0)
    cs = lax.broadcasted_iota(jnp.int32, (2 * tb, 2 * tb), 1)
    same_head = ((rj < tb) & (cs < tb)) | ((rj >= tb) & (cs >= tb))
    tq = lax.broadcasted_iota(jnp.int32, (tb, 2 * tb), 0)
    ts = lax.broadcasted_iota(jnp.int32, (tb, 2 * tb), 1)
    strict = jnp.where(ts < tb, ts, ts - tb) < tq
    return lane, rj, cs, same_head, strict, ts < tb


def _att_fill(k_ref, v_ref, kcat, vcat, n_blocks, lane):
    tb = ATT_BLOCK
    first = lane < SB_HEAD_DIM

    def fill(jb, carry):
        rows = pl.ds(pl.multiple_of(jb * tb, tb), tb)
        top = pl.ds(pl.multiple_of(jb * 2 * tb, tb), tb)
        bot = pl.ds(pl.multiple_of(jb * 2 * tb + tb, tb), tb)
        kb = k_ref[rows, :]
        vb = v_ref[rows, :]
        zero = jnp.zeros_like(kb)
        kcat[top, :] = jnp.where(first, kb, zero)
        kcat[bot, :] = jnp.where(first, zero, kb)
        vcat[top, :] = jnp.where(first, vb, zero)
        vcat[bot, :] = jnp.where(first, zero, vb)
        return carry

    lax.fori_loop(0, n_blocks, fill, 0)


def _split_dot(x, m):
    hi = x.astype(BF16)
    lo = (x - hi.astype(F32)).astype(BF16)
    return _dot(hi, m) + _dot(lo, m)


def _log1m(z):
    sp = jnp.maximum(z, 0.0) + jnp.log1p(jnp.exp(-jnp.abs(z)))
    return -sp, sp


def _att_specs(t):
    tb = ATT_BLOCK
    n_pairs = SB_WIDTH // LANES
    q_spec = pl.BlockSpec((tb, LANES), lambda p, i: (i, p))
    k_spec = pl.BlockSpec((t, LANES), lambda p, i: (0, n_pairs + p))
    v_spec = pl.BlockSpec((t, LANES), lambda p, i: (0, 2 * n_pairs + p))
    return n_pairs, q_spec, k_spec, v_spec


def _attn_fwd(qkv):
    t = qkv.shape[0]
    tb = ATT_BLOCK
    nkb = t // tb
    assert 2 * nkb <= LANES
    scale = SB_HEAD_DIM ** -0.5
    n_pairs, q_spec, k_spec, v_spec = _att_specs(t)

    def body(q_ref, k_ref, v_ref, o_ref, ct_ref, kcat, vcat):
        i = pl.program_id(1)
        lane, rj, cs, same_head, strict, left = _att_masks()

        @pl.when(i == 0)
        def _():
            _att_fill(k_ref, v_ref, kcat, vcat, nkb, lane)

        suffix = jnp.where(same_head & (rj >= cs), 1.0, 0.0).astype(BF16)
        q = q_ref[...]

        def blk(j, st, masked):
            ca, cb, acc, ct = st
            rows = pl.ds(pl.multiple_of(j * 2 * tb, 2 * tb), 2 * tb)
            z = _dot_nt(q, kcat[rows, :]) * scale
            l, _ = _log1m(z)
            if masked:
                l = jnp.where(strict, l, 0.0)
            r = _split_dot(l, suffix)
            a = jnp.exp(z + r + jnp.where(left, ca, cb))
            if masked:
                a = jnp.where(strict, a, 0.0)
            acc = acc + _dot(a.astype(BF16), vcat[rows, :])
            ct = jnp.where(lane == j, ca, jnp.where(lane == nkb + j, cb, ct))
            return ca + r[:, 0:1], cb + r[:, tb:tb + 1], acc, ct

        zero = jnp.zeros((tb, 1), F32)
        st = blk(i, (zero, zero, jnp.zeros((tb, LANES), F32), jnp.zeros((tb, LANES), F32)), True)
        st = lax.fori_loop(0, i, lambda k, s: blk(i - 1 - k, s, False), st)
        o_ref[...] = st[2].astype(BF16)
        ct_ref[0] = st[3]

    return pl.pallas_call(
        body, name="attn_fwd", grid=(n_pairs, nkb),
        in_specs=[q_spec, k_spec, v_spec],
        out_specs=[pl.BlockSpec((tb, LANES), lambda p, i: (i, p)), pl.BlockSpec((1, tb, LANES), lambda p, i: (p, i, 0))],
        out_shape=[jax.ShapeDtypeStruct((t, SB_WIDTH), BF16), jax.ShapeDtypeStruct((n_pairs, t, LANES), F32)],
        scratch_shapes=[pltpu.VMEM((2 * t, LANES), BF16), pltpu.VMEM((2 * t, LANES), BF16)],
        compiler_params=_params(2),
    )(qkv, qkv, qkv)


def _attn_bwd(qkv, dmixed, carries):
    t = qkv.shape[0]
    tb = ATT_BLOCK
    nkb = t // tb
    scale = SB_HEAD_DIM ** -0.5
    n_pairs, q_spec, k_spec, v_spec = _att_specs(t)
    gm_blocks = GM_WIDTH // LANES

    def body(q_ref, k_ref, v_ref, do_ref, ct_ref, dq_ref, dk_ref, dv_ref, kcat, vcat, dkacc, dvacc):
        i = pl.program_id(1)
        lane, rj, cs, same_head, strict, left = _att_masks()
        first = lane < SB_HEAD_DIM

        @pl.when(i == 0)
        def _():
            _att_fill(k_ref, v_ref, kcat, vcat, nkb, lane)
            dkacc[...] = jnp.zeros_like(dkacc)
            dvacc[...] = jnp.zeros_like(dvacc)

        suffix = jnp.where(same_head & (rj >= cs), 1.0, 0.0).astype(BF16)
        prefix = jnp.where(same_head & (rj <= cs), 1.0, 0.0).astype(BF16)
        q = q_ref[...]
        do = do_ref[...]
        ct = ct_ref[0]

        def blk(j, st, masked):
            pa, pb, dq = st
            rows = pl.ds(pl.multiple_of(j * 2 * tb, 2 * tb), 2 * tb)
            kc = kcat[rows, :]
            z = _dot_nt(q, kc) * scale
            l, sp = _log1m(z)
            if masked:
                l = jnp.where(strict, l, 0.0)
            r = _split_dot(l, suffix)
            ca = jnp.sum(jnp.where(lane == j, ct, 0.0), axis=1, keepdims=True)
            cb = jnp.sum(jnp.where(lane == nkb + j, ct, 0.0), axis=1, keepdims=True)
            a = jnp.exp(z + r + jnp.where(left, ca, cb))
            if masked:
                a = jnp.where(strict, a, 0.0)
            de = _dot_nt(do, vcat[rows, :]) * a
            cl = _split_dot(de, prefix)
            dz = de - jnp.exp(z - sp) * (cl + jnp.where(left, pa, pb))
            if masked:
                dz = jnp.where(strict, dz, 0.0)
            dzb = (dz * scale).astype(BF16)
            dq = dq + _dot(dzb, kc)
            dkc = _dot_tn(dzb, q)
            dvc = _dot_tn(a.astype(BF16), do)
            out_rows = pl.ds(pl.multiple_of(j * tb, tb), tb)
            dkacc[out_rows, :] += jnp.where(first, dkc[:tb], dkc[tb:])
            dvacc[out_rows, :] += jnp.where(first, dvc[:tb], dvc[tb:])
            return pa + cl[:, tb - 1:tb], pb + cl[:, 2 * tb - 1:2 * tb], dq

        zero = jnp.zeros((tb, 1), F32)
        st = lax.fori_loop(0, i, lambda j, s: blk(j, s, False), (zero, zero, jnp.zeros((tb, LANES), F32)))
        st = blk(i, st, True)
        dq_ref[...] = st[2].astype(BF16)

        @pl.when(i == nkb - 1)
        def _():
            dk_ref[...] = dkacc[...].astype(BF16)
            dv_ref[...] = dvacc[...].astype(BF16)

    col = pl.BlockSpec((t, LANES), lambda p, i: (0, p))
    out = jax.ShapeDtypeStruct((t, SB_WIDTH), BF16)
    return pl.pallas_call(
        body, name="attn_bwd", grid=(n_pairs, nkb),
        in_specs=[q_spec, k_spec, v_spec, pl.BlockSpec((tb, LANES), lambda p, i: (i, gm_blocks + p)),
                  pl.BlockSpec((1, tb, LANES), lambda p, i: (p, i, 0))],
        out_specs=[pl.BlockSpec((tb, LANES), lambda p, i: (i, p)), col, col],
        out_shape=[out, out, out],
        scratch_shapes=[pltpu.VMEM((2 * t, LANES), BF16), pltpu.VMEM((2 * t, LANES), BF16),
                        pltpu.VMEM((t, LANES), F32), pltpu.VMEM((t, LANES), F32)],
        compiler_params=_params(2),
    )(qkv, qkv, qkv, dmixed, carries)


def _matmul_residual(res, a, w, name):
    t = a.shape[0]
    tm = _token_tile(t)

    def body(res_ref, a_ref, w_ref, o_ref):
        o_ref[...] = res_ref[...] + _dot(a_ref[...], w_ref[...])

    return pl.pallas_call(
        body, name=name, grid=(t // tm,),
        in_specs=[_rows(tm, res.shape[1]), _rows(tm, a.shape[1]), _resident(w.shape)],
        out_specs=_rows(tm, res.shape[1]),
        out_shape=jax.ShapeDtypeStruct(res.shape, F32),
        compiler_params=_params(),
    )(res, a, w)


def _matmul_nt_cast(dy, w, name):
    t = dy.shape[0]
    tm = _token_tile(t)

    def body(dy_ref, w_ref, o_ref, dyb_ref):
        dyb = dy_ref[...].astype(BF16)
        dyb_ref[...] = dyb
        o_ref[...] = _dot_nt(dyb, w_ref[...]).astype(BF16)

    return pl.pallas_call(
        body, name=name, grid=(t // tm,),
        in_specs=[_rows(tm, dy.shape[1]), _resident(w.shape)],
        out_specs=[_rows(tm, w.shape[0]), _rows(tm, dy.shape[1])],
        out_shape=[jax.ShapeDtypeStruct((t, w.shape[0]), BF16), jax.ShapeDtypeStruct(dy.shape, BF16)],
        compiler_params=_params(),
    )(dy, w)


def _norm_input_bwd(dres, dz, w, h, g, name):
    t = h.shape[0]
    tm = _token_tile(t)
    nb, _, width = w.shape

    def body(dres_ref, dz_ref, w_ref, h_ref, g_ref, dh_ref, n_ref, dg_ref):
        i = pl.program_id(0)
        hh = h_ref[...]
        gg = g_ref[...]
        r = _rstd(hh)
        n_ref[...] = (hh * r * gg).astype(BF16)
        dn = jnp.zeros((tm, D_MODEL), F32)
        for b in range(nb):
            dn = dn + _dot_nt(dz_ref[:, b * width:(b + 1) * width], w_ref[b])
        dh, dg_rows = _rms_bwd(dn, hh, r, gg)
        dh_ref[...] = dres_ref[...] + dh

        @pl.when(i == 0)
        def _():
            dg_ref[...] = jnp.zeros_like(dg_ref)

        dg_ref[...] += jnp.sum(dg_rows, axis=0, keepdims=True)

    return pl.pallas_call(
        body, name=name, grid=(t // tm,),
        in_specs=[_rows(tm, D_MODEL), _rows(tm, nb * width), _resident(w.shape), _rows(tm, D_MODEL),
                  _resident((1, D_MODEL))],
        out_specs=[_rows(tm, D_MODEL), _rows(tm, D_MODEL), pl.BlockSpec((1, D_MODEL), lambda i: (0, 0))],
        out_shape=[jax.ShapeDtypeStruct((t, D_MODEL), F32), jax.ShapeDtypeStruct((t, D_MODEL), BF16),
                   jax.ShapeDtypeStruct((1, D_MODEL), F32)],
        compiler_params=_params(),
    )(dres, dz, w, h, g)


def _head(h, p, target, gple, gfin, wg, wproj):
    t = h.shape[0]
    tm = _token_tile(t)
    pw = D_MODEL // N_CHIPS

    def body(h_ref, p_ref, tgt_ref, gple_ref, gfin_ref, wg_ref, wproj_ref,
             loss_ref, dgf_ref, dh_ref, dgp_ref, dpp_ref, n_ref, pb_ref):
        i = pl.program_id(0)
        hh = h_ref[...]
        n = (hh * _rstd(hh) * gple_ref[...]).astype(BF16)
        n_ref[...] = n
        gate = jax.nn.sigmoid(_dot(n, wg_ref[...]))
        pb = p_ref[...].astype(BF16)
        pb_ref[...] = pb
        pp = jnp.concatenate([_dot(pb, wproj_ref[b]) for b in range(N_CHIPS)], axis=1)
        h4 = hh + gate * pp
        r = _rstd(h4)
        gf = gfin_ref[...]
        err = h4 * r * gf - tgt_ref[...]
        dy = err * (1.0 / D_MODEL)
        dh4, dgf_rows = _rms_bwd(dy, h4, r, gf)
        dh_ref[...] = dh4
        dgp_ref[...] = (dh4 * pp * gate * (1.0 - gate)).astype(BF16)
        dpp_ref[...] = (dh4 * gate).astype(BF16)

        @pl.when(i == 0)
        def _():
            loss_ref[...] = jnp.zeros_like(loss_ref)
            dgf_ref[...] = jnp.zeros_like(dgf_ref)

        loss_ref[...] += (0.5 / D_MODEL) * jnp.sum(err * err)
        dgf_ref[...] += jnp.sum(dgf_rows, axis=0, keepdims=True)

    bf = lambda w: jax.ShapeDtypeStruct((t, w), BF16)
    const = lambda i: (0, 0)
    return pl.pallas_call(
        body, name="head", grid=(t // tm,),
        in_specs=[_rows(tm, D_MODEL), _rows(tm, PLE_DIM), _rows(tm, D_MODEL), _resident((1, D_MODEL)),
                  _resident((1, D_MODEL)), _resident(wg.shape), _resident(wproj.shape)],
        out_specs=[pl.BlockSpec((1, LANES), const), pl.BlockSpec((1, D_MODEL), const), _rows(tm, D_MODEL),
                   _rows(tm, D_MODEL), _rows(tm, D_MODEL), _rows(tm, D_MODEL), _rows(tm, PLE_DIM)],
        out_shape=[jax.ShapeDtypeStruct((1, LANES), F32), jax.ShapeDtypeStruct((1, D_MODEL), F32),
                   jax.ShapeDtypeStruct((t, D_MODEL), F32), bf(D_MODEL), bf(D_MODEL), bf(D_MODEL), bf(PLE_DIM)],
        compiler_params=_params(),
    )(h, p, target, gple, gfin, wg, wproj)


def _device_step(x, p, target, small, big):
    w1in, w1out = big["ffn1_w_in"], big["ffn1_w_out"].reshape(D_FF, D_MODEL)
    w2in, w2out = big["ffn2_w_in"], big["ffn2_w_out"].reshape(D_FF, D_MODEL)
    wmix = big["w_mix_in"]
    wmo = big["w_mix_out"].reshape(D_MODEL, D_MODEL)
    wg = big["ple_w_gate"].reshape(D_MODEL, D_MODEL)
    wproj = big["ple_w_proj"]
    bt = small["gmlp_b"].T

    h1, gu1 = _ffn_fwd(x, small["ffn1_norm"], w1in, w1out, "ffn1_fwd")
    zg, qkv = _mix_in_fwd(h1, small["mix_norm"], wmix)
    gm = _gmlp_fwd(zg, small["gmlp_v_norm"], small["gmlp_w_s"], bt)
    att, carries = _attn_fwd(qkv)
    mixed = jnp.concatenate([gm, att], axis=1)
    h2 = _matmul_residual(h1, mixed, wmo, "mix_out_fwd")
    h3, gu2 = _ffn_fwd(h2, small["ffn2_norm"], w2in, w2out, "ffn2_fwd")
    loss, d_final, dh4, dgp, dpp, n4, pb = _head(h3, p, target, small["ple_norm"], small["final_norm"], wg, wproj)

    g_big, g_small = {}, {"final_norm": d_final}
    g_big["ple_w_gate"] = _wgrad_rows(n4, dgp, N_CHIPS, "wgrad_ple_gate")
    g_big["ple_w_proj"] = _wgrad_cols(pb, dpp, N_CHIPS, "wgrad_ple_proj")
    dh3, _, g_small["ple_norm"] = _norm_input_bwd(dh4, dgp, wg.reshape(1, D_MODEL, D_MODEL), h3, small["ple_norm"],
                                                  "ple_bwd")

    dh2, dgu2, n3, act2, dhh3, g_small["ffn2_norm"] = _ffn_bwd(dh3, h2, small["ffn2_norm"], gu2, w2in, w2out, "ffn2_bwd")
    g_big["ffn2_w_in"] = _wgrad_cols(n3, dgu2, N_CHIPS, "wgrad_ffn2_in")
    g_big["ffn2_w_out"] = _wgrad_rows(act2, dhh3, 2, "wgrad_ffn2_out")

    dmixed, dh2b = _matmul_nt_cast(dh2, wmo, "mix_out_bwd")
    g_big["w_mix_out"] = _wgrad_rows(mixed, dh2b, 2, "wgrad_mix_out")
    dzg, g_small["gmlp_w_s"], dbt, g_small["gmlp_v_norm"] = _gmlp_bwd(zg, dmixed, small["gmlp_v_norm"],
                                                                      small["gmlp_w_s"], bt)
    g_small["gmlp_b"] = dbt.T
    dq, dk, dv = _attn_bwd(qkv, dmixed, carries)
    dzmix = jnp.concatenate([dzg, dq, dk, dv], axis=1)
    dh1, n2, g_small["mix_norm"] = _norm_input_bwd(dh2, dzmix, wmix, h1, small["mix_norm"], "mix_in_bwd")
    g_big["w_mix_in"] = _wgrad_cols(n2, dzmix, N_CHIPS, "wgrad_mix_in")

    dx, dgu1, n1, act1, dhh1, g_small["ffn1_norm"] = _ffn_bwd(dh1, x, small["ffn1_norm"], gu1, w1in, w1out, "ffn1_bwd")
    g_big["ffn1_w_in"] = _wgrad_cols(n1, dgu1, N_CHIPS, "wgrad_ffn1_in")
    g_big["ffn1_w_out"] = _wgrad_rows(act1, dhh1, 2, "wgrad_ffn1_out")
    return loss, dx, g_big, g_small


_BIG = ("ffn1_w_in", "ffn1_w_out", "w_mix_in", "w_mix_out", "ffn2_w_in", "ffn2_w_out", "ple_w_gate", "ple_w_proj")
_SMALL = ("ffn1_norm", "mix_norm", "gmlp_v_norm", "gmlp_w_s", "gmlp_b", "ffn2_norm", "ple_norm", "final_norm")
_ALL = ("ffn1_norm", "ffn1_w_in", "ffn1_w_out", "mix_norm", "w_mix_in", "gmlp_v_norm", "gmlp_w_s", "gmlp_b", "w_mix_out",
        "ffn2_norm", "ffn2_w_in", "ffn2_w_out", "ple_norm", "ple_w_gate", "ple_w_proj", "final_norm")
_ANY = pl.BlockSpec(memory_space=pl.ANY)
_MESH = pl.DeviceIdType.MESH


def _mesh_pos():
    return lax.axis_index("x"), lax.axis_index("y"), lax.axis_index("c")


def _other_chips(x, y):
    return [((x, 1 - y), 2 * x + 1 - y), ((1 - x, y), 2 * (1 - x) + y), ((1 - x, 1 - y), 2 * (1 - x) + 1 - y)]


def _remote(src, dst, send_sem, recv_sem, device):
    return pltpu.make_async_remote_copy(src_ref=src, dst_ref=dst, send_sem=send_sem, recv_sem=recv_sem,
                                        device_id=device, device_id_type=_MESH)


def _all_gather(shards):
    n = len(shards)
    per = 2 * (N_CHIPS - 1)

    def body(*refs):
        ins, outs = refs[:n], refs[n:2 * n]
        send_sems, recv_sems, local_sems = refs[2 * n:]
        x, y, c = _mesh_pos()
        sibling = (x, y, 1 - c)
        chips = _other_chips(x, y)
        mine = 2 * x + y

        def half(w, blk, cc):
            hr = shards[w].shape[0] // 2
            return outs[w].at[blk, pl.ds(cc * hr, hr), :]

        local = []
        for w in range(n):
            cp = pltpu.make_async_copy(ins[w], outs[w].at[mine], local_sems.at[w])
            cp.start()
            local.append(cp)
        first = []
        for w in range(n):
            hr = shards[w].shape[0] // 2
            src = ins[w].at[pl.ds(c * hr, hr), :]
            for k, (chip, _) in enumerate(chips):
                cp = _remote(src, half(w, mine, c), send_sems.at[per * w + k], recv_sems.at[per * w + k], (*chip, c))
                cp.start()
                first.append(cp)
        passed = []
        for w in range(n):
            for k, (_, blk) in enumerate(chips):
                landed = half(w, blk, c)
                _remote(landed, landed, send_sems.at[per * w + k], recv_sems.at[per * w + k], sibling).wait_recv()
                cp = _remote(landed, landed, send_sems.at[per * w + 3 + k], recv_sems.at[per * w + 3 + k], sibling)
                cp.start()
                passed.append(cp)
        for w in range(n):
            for k, (_, blk) in enumerate(chips):
                other = half(w, blk, 1 - c)
                _remote(other, other, send_sems.at[per * w + 3 + k], recv_sems.at[per * w + 3 + k], sibling).wait_recv()
        for cp in first + passed:
            cp.wait_send()
        for cp in local:
            cp.wait()

    return pl.pallas_call(
        body, name="all_gather_weights",
        in_specs=[_ANY] * n, out_specs=[_ANY] * n,
        out_shape=[jax.ShapeDtypeStruct((N_CHIPS, *s.shape), s.dtype) for s in shards],
        scratch_shapes=[pltpu.SemaphoreType.DMA((per * n,)), pltpu.SemaphoreType.DMA((per * n,)),
                        pltpu.SemaphoreType.DMA((n,))],
    )(*shards)


def _pair_exchange(grads):
    n = len(grads)

    def body(*refs):
        ins, outs = refs[:n], refs[n:2 * n]
        send_sems, recv_sems = refs[2 * n:]
        x, y, c = _mesh_pos()
        cps = []
        for w in range(n):
            hr = grads[w].shape[1] // 2
            cp = _remote(ins[w].at[:, pl.ds((1 - c) * hr, hr), :], outs[w], send_sems.at[w], recv_sems.at[w], (x, y, 1 - c))
            cp.start()
            cps.append(cp)
        for cp in cps:
            cp.wait()

    return pl.pallas_call(
        body, name="grad_pair_exchange",
        in_specs=[_ANY] * n, out_specs=[_ANY] * n,
        out_shape=[jax.ShapeDtypeStruct((g.shape[0], g.shape[1] // 2, g.shape[2]), g.dtype) for g in grads],
        scratch_shapes=[pltpu.SemaphoreType.DMA((n,)), pltpu.SemaphoreType.DMA((n,))],
    )(*grads)


def _pair_sum(g, a, pos, name):
    nb, r, c = g.shape
    hr = r // 2

    def body(pos_ref, g_ref, a_ref, o_ref):
        o_ref[...] = (g_ref[...] + a_ref[...]).astype(BF16)

    return pl.pallas_call(
        body, name=name,
        grid_spec=pltpu.PrefetchScalarGridSpec(
            num_scalar_prefetch=1, grid=(nb,),
            in_specs=[pl.BlockSpec((1, hr, c), lambda k, pos: (k ^ pos[0], pos[1], 0)),
                      pl.BlockSpec((1, hr, c), lambda k, pos: (k ^ pos[0], 0, 0))],
            out_specs=pl.BlockSpec((1, hr, c), lambda k, pos: (k, 0, 0))),
        out_shape=jax.ShapeDtypeStruct((nb, hr, c), BF16),
        compiler_params=_params(),
    )(pos, g, a)


def _chip_exchange(sums):
    n = len(sums)
    per = N_CHIPS - 1

    def body(*refs):
        ins, outs = refs[:n], refs[n:2 * n]
        send_sems, recv_sems = refs[2 * n:]
        x, y, c = _mesh_pos()
        cps = []
        for w in range(n):
            for k, (chip, _) in enumerate(_other_chips(x, y)):
                cp = _remote(ins[w].at[k + 1], outs[w].at[k], send_sems.at[per * w + k], recv_sems.at[per * w + k],
                             (*chip, c))
                cp.start()
                cps.append(cp)
        for cp in cps:
            cp.wait()

    return pl.pallas_call(
        body, name="grad_chip_exchange",
        in_specs=[_ANY] * n, out_specs=[_ANY] * n,
        out_shape=[jax.ShapeDtypeStruct((per, *s.shape[1:]), s.dtype) for s in sums],
        scratch_shapes=[pltpu.SemaphoreType.DMA((per * n,)), pltpu.SemaphoreType.DMA((per * n,))],
    )(*sums)


def _chip_sum(s, b, pos, name):
    _, hr, c = s.shape

    def body(pos_ref, s_ref, b_ref, o_ref):
        o_ref[...] = (s_ref[0].astype(F32) + b_ref[0].astype(F32)) + (b_ref[1].astype(F32) + b_ref[2].astype(F32))

    return pl.pallas_call(
        body, name=name,
        grid_spec=pltpu.PrefetchScalarGridSpec(
            num_scalar_prefetch=1, grid=(1,),
            in_specs=[pl.BlockSpec((1, hr, c), lambda k, pos: (0, 0, 0)), pl.BlockSpec((N_CHIPS - 1, hr, c), lambda k, pos: (0, 0, 0))],
            out_specs=pl.BlockSpec((hr, c), lambda k, pos: (pos[1], 0))),
        out_shape=jax.ShapeDtypeStruct((2 * hr, c), F32),
        compiler_params=_params(),
    )(pos, s, b)


def _pair_share(grads):
    n = len(grads)

    def body(*refs):
        outs = refs[n:2 * n]
        send_sems, recv_sems = refs[2 * n:]
        x, y, c = _mesh_pos()
        cps = []
        for w in range(n):
            hr = grads[w].shape[0] // 2
            rows = outs[w].at[pl.ds(c * hr, hr), :]
            cp = _remote(rows, rows, send_sems.at[w], recv_sems.at[w], (x, y, 1 - c))
            cp.start()
            cps.append(cp)
        for w, cp in enumerate(cps):
            cp.wait_send()
            hr = grads[w].shape[0] // 2
            other = outs[w].at[pl.ds((1 - c) * hr, hr), :]
            _remote(other, other, send_sems.at[w], recv_sems.at[w], (x, y, 1 - c)).wait_recv()

    return pl.pallas_call(
        body, name="grad_pair_share",
        in_specs=[_ANY] * n, out_specs=[_ANY] * n,
        out_shape=[jax.ShapeDtypeStruct(g.shape, g.dtype) for g in grads],
        input_output_aliases={w: w for w in range(n)},
        scratch_shapes=[pltpu.SemaphoreType.DMA((n,)), pltpu.SemaphoreType.DMA((n,))],
    )(*grads)


def _all_reduce_small(buf):
    n_dev = 8

    def body(in_ref, o_ref, slots, send_sems, recv_sems):
        x, y, c = _mesh_pos()
        me = 4 * x + 2 * y + c
        slots[0] = in_ref[...]
        cps = []
        for q in range(1, n_dev):
            peer = (x ^ (q >> 2), y ^ ((q >> 1) & 1), c ^ (q & 1))
            cp = _remote(in_ref, slots.at[q], send_sems.at[q - 1], recv_sems.at[q - 1], peer)
            cp.start()
            cps.append(cp)
        for cp in cps:
            cp.wait()
        acc = slots[me]
        for d in range(1, n_dev):
            acc = acc + slots[d ^ me]
        o_ref[...] = acc

    return pl.pallas_call(
        body, name="all_reduce_small",
        in_specs=[pl.BlockSpec(memory_space=pltpu.VMEM)], out_specs=pl.BlockSpec(memory_space=pltpu.VMEM),
        out_shape=jax.ShapeDtypeStruct(buf.shape, buf.dtype),
        scratch_shapes=[pltpu.VMEM((n_dev, *buf.shape), buf.dtype), pltpu.SemaphoreType.DMA((n_dev - 1,)),
                        pltpu.SemaphoreType.DMA((n_dev - 1,))],
    )(buf)


def _adamw(w, g, m, v, name):
    r, c = w.shape
    tr = r if r * c * 4 <= (1 << 20) else 64
    bias1 = 1.0 - ADAM_B1 ** ADAM_STEP
    bias2 = 1.0 - ADAM_B2 ** ADAM_STEP

    def body(w_ref, g_ref, m_ref, v_ref, d_ref, mo_ref, vo_ref):
        gg = g_ref[...]
        m2 = ADAM_B1 * m_ref[...] + (1.0 - ADAM_B1) * gg
        v2 = ADAM_B2 * v_ref[...] + (1.0 - ADAM_B2) * (gg * gg)
        mo_ref[...] = m2
        vo_ref[...] = v2
        d_ref[...] = -ADAM_LR * ((m2 / bias1) / (jnp.sqrt(v2 / bias2) + ADAM_EPS) + ADAM_WD * w_ref[...])

    spec = pl.BlockSpec((tr, c), lambda i: (i, 0))
    out = jax.ShapeDtypeStruct((r, c), F32)
    return pl.pallas_call(
        body, name=name, grid=(r // tr,), in_specs=[spec] * 4, out_specs=[spec] * 3, out_shape=[out] * 3,
        compiler_params=_params(),
    )(w, g, m, v)


def _pack(parts):
    flat = jnp.concatenate([parts[n].reshape(-1) for n in _SMALL])
    return flat.reshape(-1, LANES)


def _unpack(buf, like):
    flat = buf.reshape(-1)
    out, at = {}, 0
    for n in _SMALL:
        size = like[n].size
        out[n] = flat[at:at + size].reshape(like[n].shape)
        at += size
    return out


def kernel(x, p, ffn1_norm, ffn1_w_in, ffn1_w_out, mix_norm, w_mix_in, gmlp_v_norm, gmlp_w_s, gmlp_b, w_mix_out, ffn2_norm, ffn2_w_in, ffn2_w_out, ple_norm, ple_w_gate, ple_w_proj, final_norm, loss_target, m_ffn1_norm, m_ffn1_w_in, m_ffn1_w_out, m_mix_norm, m_w_mix_in, m_gmlp_v_norm, m_gmlp_w_s, m_gmlp_b, m_w_mix_out, m_ffn2_norm, m_ffn2_w_in, m_ffn2_w_out, m_ple_norm, m_ple_w_gate, m_ple_w_proj, m_final_norm, v_ffn1_norm, v_ffn1_w_in, v_ffn1_w_out, v_mix_norm, v_w_mix_in, v_gmlp_v_norm, v_gmlp_w_s, v_gmlp_b, v_w_mix_out, v_ffn2_norm, v_ffn2_w_in, v_ffn2_w_out, v_ple_norm, v_ple_w_gate, v_ple_w_proj, v_final_norm):
    args = dict(locals())
    w = {n: args[n] for n in _ALL}
    m = {n: args["m_" + n] for n in _ALL}
    v = {n: args["v_" + n] for n in _ALL}
    axes = ("x", "y", "c")
    xi, yi, ci = _mesh_pos()
    pos = jnp.stack([2 * xi + yi, ci]).astype(jnp.int32)

    shard = {n: w[n][0] for n in _BIG}
    gathered = _all_gather([shard[n].astype(BF16) for n in _BIG])
    small = {n: (w[n][0] if w[n].ndim > 2 else w[n].reshape(1, -1)) for n in _SMALL}
    loss_part, dx, g_big, g_small = _device_step(x[0], p[0, 0], loss_target[0], small, dict(zip(_BIG, gathered)))
    loss = lax.psum(loss_part[0, 0], axes)

    partial = [g_big[n].reshape(N_CHIPS, *shard[n].shape) for n in _BIG]
    from_sibling = _pair_exchange(partial)
    pair = [_pair_sum(g, a, pos, "pair_sum_" + n) for n, g, a in zip(_BIG, partial, from_sibling)]
    from_chips = _chip_exchange(pair)
    halves = [_chip_sum(s, b, pos, "chip_sum_" + n) for n, s, b in zip(_BIG, pair, from_chips)]
    grads = dict(zip(_BIG, _pair_share(halves)))

    delta, new_m, new_v = {}, {}, {}
    for n in _BIG:
        d2, m2, v2 = _adamw(shard[n], grads[n], m[n][0], v[n][0], "adamw_" + n)
        grads[n], delta[n], new_m[n], new_v[n] = grads[n][None], d2[None], m2[None], v2[None]

    g_packed = _all_reduce_small(_pack(g_small))
    d_p, m_p, v_p = _adamw(_pack(w), g_packed, _pack(m), _pack(v), "adamw_small")
    for out, buf in ((grads, g_packed), (delta, d_p), (new_m, m_p), (new_v, v_p)):
        out.update(_unpack(buf, w))

    return (loss, dx[None], *[grads[n] for n in _ALL], *[delta[n] for n in _ALL], *[new_m[n] for n in _ALL],
            *[new_v[n] for n in _ALL])
```

```python
import functools

import jax
import jax.numpy as jnp
from jax import lax
from jax.experimental import pallas as pl
from jax.experimental.pallas import tpu as pltpu

F32, BF16 = jnp.float32, jnp.bfloat16

D_MODEL = 1024
D_FF = 2816
FF_BLOCK = 2 * D_FF // 4
PLE_DIM = 256
CHUNK = 128
GM_HEADS = 4
GM_WIDTH = 512
SB_HEAD_DIM = 64
SB_WIDTH = 512
MIX_IN_WIDTH = 2 * GM_WIDTH + 3 * SB_WIDTH
MIX_BLOCK = MIX_IN_WIDTH // 4
EPS = 1e-6
N_CHIPS = 4
LANES = 128
ATT_BLOCK = 128
ATT_Q = 256
VMEM_LIMIT = 56 * 1024 * 1024

ADAM_LR, ADAM_B1, ADAM_B2, ADAM_EPS, ADAM_WD, ADAM_STEP = 0.001, 0.9, 0.999, 1e-08, 0.01, 10


def _dot(a, b):
    return jnp.dot(a, b, preferred_element_type=F32)


def _dot_nt(a, b):
    return lax.dot_general(a, b, (((1,), (1,)), ((), ())), preferred_element_type=F32)


def _dot_tn(a, b):
    return lax.dot_general(a, b, (((0,), (0,)), ((), ())), preferred_element_type=F32)


def _resident(shape):
    nd = len(shape)
    return pl.BlockSpec(shape, lambda *_: (0,) * nd, pipeline_mode=pl.Buffered(1))


def _rows(tm, width):
    return pl.BlockSpec((tm, width), lambda i: (i, 0))


def _params(n_axes=1):
    return pltpu.CompilerParams(dimension_semantics=("arbitrary",) * n_axes, vmem_limit_bytes=VMEM_LIMIT)


def _rstd(h):
    return lax.rsqrt(jnp.mean(h * h, axis=-1, keepdims=True) + EPS)


def _rms_bwd(dy, h, r, g):
    dyg = dy * g
    dh = r * dyg - h * (r * r * r) * jnp.mean(dyg * h, axis=-1, keepdims=True)
    return dh, dy * h * r


def _gelu(x):
    return 0.5 * x * (1.0 + lax.erf(x * (2.0 ** -0.5)))


def _gelu_grad(x):
    return 0.5 * (1.0 + lax.erf(x * (2.0 ** -0.5))) + x * jnp.exp(-0.5 * x * x) * ((2.0 * jnp.pi) ** -0.5)


def _token_tile(t):
    return min(256, t)


def _ffn_fwd(h, g, win, wout, name):
    t = h.shape[0]
    tm = _token_tile(t)

    def body(h_ref, g_ref, win_ref, wout_ref, ho_ref, gu_ref):
        hh = h_ref[...]
        n = (hh * _rstd(hh) * g_ref[...]).astype(BF16)
        acc = jnp.zeros((tm, D_MODEL), F32)
        for jb in range(2):
            gate = _dot(n, win_ref[jb])
            up = _dot(n, win_ref[2 + jb])
            gu_ref[:, jb * FF_BLOCK:(jb + 1) * FF_BLOCK] = gate.astype(BF16)
            gu_ref[:, D_FF + jb * FF_BLOCK:D_FF + (jb + 1) * FF_BLOCK] = up.astype(BF16)
            act = (gate * jax.nn.sigmoid(gate) * up).astype(BF16)
            acc = acc + _dot(act, wout_ref[jb * FF_BLOCK:(jb + 1) * FF_BLOCK, :])
        ho_ref[...] = hh + 0.5 * acc

    return pl.pallas_call(
        body, name=name, grid=(t // tm,),
        in_specs=[_rows(tm, D_MODEL), _resident((1, D_MODEL)), _resident(win.shape), _resident(wout.shape)],
        out_specs=[_rows(tm, D_MODEL), _rows(tm, 2 * D_FF)],
        out_shape=[jax.ShapeDtypeStruct((t, D_MODEL), F32), jax.ShapeDtypeStruct((t, 2 * D_FF), BF16)],
        compiler_params=_params(),
    )(h, g, win, wout)


def _ffn_bwd(dho, h, g, gu, win, wout, name):
    t = h.shape[0]
    tm = _token_tile(t)

    def body(dho_ref, h_ref, g_ref, gu_ref, win_ref, wout_ref, dh_ref, dgu_ref, n_ref, act_ref, dhh_ref, dg_ref):
        i = pl.program_id(0)
        hh = h_ref[...]
        gg = g_ref[...]
        r = _rstd(hh)
        n_ref[...] = (hh * r * gg).astype(BF16)
        dho = dho_ref[...]
        dhh = (0.5 * dho).astype(BF16)
        dhh_ref[...] = dhh
        dn = jnp.zeros((tm, D_MODEL), F32)
        for jb in range(2):
            cg = slice(jb * FF_BLOCK, (jb + 1) * FF_BLOCK)
            cu = slice(D_FF + jb * FF_BLOCK, D_FF + (jb + 1) * FF_BLOCK)
            dact = _dot_nt(dhh, wout_ref[cg, :])
            gate = gu_ref[:, cg].astype(F32)
            up = gu_ref[:, cu].astype(F32)
            sg = jax.nn.sigmoid(gate)
            silu = gate * sg
            act_ref[:, cg] = (silu * up).astype(BF16)
            dgate = (dact * up * (sg * (1.0 + gate * (1.0 - sg)))).astype(BF16)
            dup = (dact * silu).astype(BF16)
            dgu_ref[:, cg] = dgate
            dgu_ref[:, cu] = dup
            dn = dn + _dot_nt(dgate, win_ref[jb]) + _dot_nt(dup, win_ref[2 + jb])
        dh, dg_rows = _rms_bwd(dn, hh, r, gg)
        dh_ref[...] = dho + dh

        @pl.when(i == 0)
        def _():
            dg_ref[...] = jnp.zeros_like(dg_ref)

        dg_ref[...] += jnp.sum(dg_rows, axis=0, keepdims=True)

    return pl.pallas_call(
        body, name=name, grid=(t // tm,),
        in_specs=[_rows(tm, D_MODEL), _rows(tm, D_MODEL), _resident((1, D_MODEL)), _rows(tm, 2 * D_FF),
                  _resident(win.shape), _resident(wout.shape)],
        out_specs=[_rows(tm, D_MODEL), _rows(tm, 2 * D_FF), _rows(tm, D_MODEL), _rows(tm, D_FF), _rows(tm, D_MODEL),
                   pl.BlockSpec((1, D_MODEL), lambda i: (0, 0))],
        out_shape=[jax.ShapeDtypeStruct((t, D_MODEL), F32), jax.ShapeDtypeStruct((t, 2 * D_FF), BF16),
                   jax.ShapeDtypeStruct((t, D_MODEL), BF16), jax.ShapeDtypeStruct((t, D_FF), BF16),
                   jax.ShapeDtypeStruct((t, D_MODEL), BF16), jax.ShapeDtypeStruct((1, D_MODEL), F32)],
        compiler_params=_params(),
    )(dho, h, g, gu, win, wout)


def _wgrad(a, b, out_shape, out_block, out_index, a_width, b_width, grid_ij, name):
    t = a.shape[0]
    tk = min(512, t)

    def body(a_ref, b_ref, o_ref):
        k = pl.program_id(2)
        prod = _dot_tn(a_ref[...], b_ref[...]).reshape(o_ref.shape)

        @pl.when(k == 0)
        def _():
            o_ref[...] = prod

        @pl.when(k > 0)
        def _():
            o_ref[...] += prod

    return pl.pallas_call(
        body, name=name, grid=(*grid_ij, t // tk),
        in_specs=[pl.BlockSpec((tk, a_width), lambda i, j, k: (k, i)), pl.BlockSpec((tk, b_width), lambda i, j, k: (k, j))],
        out_specs=pl.BlockSpec(out_block, lambda i, j, k: out_index(i, j)),
        out_shape=jax.ShapeDtypeStruct(out_shape, F32),
        compiler_params=_params(3),
    )(a, b)


def _wgrad_cols(a, b, n_blocks, name):
    ka, nb = a.shape[1], b.shape[1] // n_blocks
    return _wgrad(a, b, (n_blocks, ka, nb), (1, ka, nb), lambda i, j: (j, 0, 0), ka, nb, (1, n_blocks), name)


def _wgrad_rows(a, b, n_blocks, name):
    ka, nb = a.shape[1] // n_blocks, b.shape[1]
    return _wgrad(a, b, (a.shape[1], nb), (ka, nb), lambda i, j: (i, 0), ka, nb, (n_blocks, 1), name)


def _mix_in_fwd(h, g, wmix):
    t = h.shape[0]
    tm = _token_tile(t)
    gw2 = 2 * GM_WIDTH

    def body(h_ref, g_ref, w_ref, zg_ref, qkv_ref):
        hh = h_ref[...]
        n = (hh * _rstd(hh) * g_ref[...]).astype(BF16)
        for b in range(N_CHIPS):
            z = _dot(n, w_ref[b])
            lo, hi = b * MIX_BLOCK, (b + 1) * MIX_BLOCK
            if hi <= gw2:
                zg_ref[:, lo:hi] = z
            elif lo >= gw2:
                qkv_ref[:, lo - gw2:hi - gw2] = z.astype(BF16)
            else:
                zg_ref[:, lo:gw2] = z[:, :gw2 - lo]
                qkv_ref[:, 0:hi - gw2] = z[:, gw2 - lo:].astype(BF16)

    return pl.pallas_call(
        body, name="mix_in_fwd", grid=(t // tm,),
        in_specs=[_rows(tm, D_MODEL), _resident((1, D_MODEL)), _resident(wmix.shape)],
        out_specs=[_rows(tm, gw2), _rows(tm, 3 * SB_WIDTH)],
        out_shape=[jax.ShapeDtypeStruct((t, gw2), F32), jax.ShapeDtypeStruct((t, 3 * SB_WIDTH), BF16)],
        compiler_params=_params(),
    )(h, g, wmix)


def _causal_chunk_mask():
    row = lax.broadcasted_iota(jnp.int32, (CHUNK, CHUNK), 0)
    col = lax.broadcasted_iota(jnp.int32, (CHUNK, CHUNK), 1)
    return row >= col


def _gmlp_tile(t):
    return min(512, t)


def _gmlp_fwd(zg, gv, ws, bt):
    t = zg.shape[0]
    tm = _gmlp_tile(t)

    def body(zg_ref, gv_ref, ws_ref, bt_ref, o_ref):
        u = _gelu(zg_ref[:, :GM_WIDTH])
        v = _gelu(zg_ref[:, GM_WIDTH:])
        vn = (v * _rstd(v) * gv_ref[...]).astype(BF16)
        mask = _causal_chunk_mask()
        for hd in range(GM_HEADS):
            wm = jnp.where(mask, ws_ref[hd], 0.0).astype(BF16)
            cols = slice(hd * CHUNK, (hd + 1) * CHUNK)
            for c in range(tm // CHUNK):
                rows = slice(c * CHUNK, (c + 1) * CHUNK)
                sv = _dot(wm, vn[rows, cols]) + bt_ref[:, hd:hd + 1]
                o_ref[rows, cols] = (u[rows, cols] * sv).astype(BF16)

    return pl.pallas_call(
        body, name="gmlp_fwd", grid=(t // tm,),
        in_specs=[_rows(tm, 2 * GM_WIDTH), _resident((1, GM_WIDTH)), _resident(ws.shape), _resident(bt.shape)],
        out_specs=_rows(tm, GM_WIDTH),
        out_shape=jax.ShapeDtypeStruct((t, GM_WIDTH), BF16),
        compiler_params=_params(),
    )(zg, gv, ws, bt)


def _gmlp_bwd(zg, dmixed, gv, ws, bt):
    t = zg.shape[0]
    tm = _gmlp_tile(t)

    def body(zg_ref, dgm_ref, gv_ref, ws_ref, bt_ref, dzg_ref, dws_ref, dbt_ref, dgv_ref):
        i = pl.program_id(0)

        @pl.when(i == 0)
        def _():
            dws_ref[...] = jnp.zeros_like(dws_ref)
            dbt_ref[...] = jnp.zeros_like(dbt_ref)
            dgv_ref[...] = jnp.zeros_like(dgv_ref)

        zu = zg_ref[:, :GM_WIDTH]
        zv = zg_ref[:, GM_WIDTH:]
        u = _gelu(zu)
        v = _gelu(zv)
        r = _rstd(v)
        gvv = gv_ref[...]
        vn = (v * r * gvv).astype(BF16)
        dgm = dgm_ref[...].astype(F32)
        dsv = (dgm * u).astype(BF16)
        mask = _causal_chunk_mask()
        du_cols, dvn_cols = [], []
        for hd in range(GM_HEADS):
            wm = jnp.where(mask, ws_ref[hd], 0.0).astype(BF16)
            cols = slice(hd * CHUNK, (hd + 1) * CHUNK)
            dw = jnp.zeros((CHUNK, CHUNK), F32)
            db = jnp.zeros((CHUNK, 1), F32)
            du_rows, dvn_rows = [], []
            for c in range(tm // CHUNK):
                rows = slice(c * CHUNK, (c + 1) * CHUNK)
                sv = _dot(wm, vn[rows, cols]) + bt_ref[:, hd:hd + 1]
                du_rows.append(dgm[rows, cols] * sv)
                dvn_rows.append(_dot_tn(wm, dsv[rows, cols]))
                dw = dw + _dot_nt(dsv[rows, cols], vn[rows, cols])
                db = db + jnp.sum(dsv[rows, cols].astype(F32), axis=1, keepdims=True)
            dws_ref[hd] += jnp.where(mask, dw, 0.0)
            dbt_ref[:, hd:hd + 1] += db
            du_cols.append(jnp.concatenate(du_rows, axis=0))
            dvn_cols.append(jnp.concatenate(dvn_rows, axis=0))
        du = jnp.concatenate(du_cols, axis=1)
        dvn = jnp.concatenate(dvn_cols, axis=1)
        dv, dgv_rows = _rms_bwd(dvn, v, r, gvv)
        dgv_ref[...] += jnp.sum(dgv_rows, axis=0, keepdims=True)
        dzg_ref[:, :GM_WIDTH] = (du * _gelu_grad(zu)).astype(BF16)
        dzg_ref[:, GM_WIDTH:] = (dv * _gelu_grad(zv)).astype(BF16)

    const = lambda nd: (lambda i: (0,) * nd)
    return pl.pallas_call(
        body, name="gmlp_bwd", grid=(t // tm,),
        in_specs=[_rows(tm, 2 * GM_WIDTH), _rows(tm, GM_WIDTH), _resident((1, GM_WIDTH)), _resident(ws.shape),
                  _resident(bt.shape)],
        out_specs=[_rows(tm, 2 * GM_WIDTH), pl.BlockSpec(ws.shape, const(3)), pl.BlockSpec(bt.shape, const(2)),
                   pl.BlockSpec((1, GM_WIDTH), const(2))],
        out_shape=[jax.ShapeDtypeStruct((t, 2 * GM_WIDTH), BF16), jax.ShapeDtypeStruct(ws.shape, F32),
                   jax.ShapeDtypeStruct(bt.shape, F32), jax.ShapeDtypeStruct((1, GM_WIDTH), F32)],
        compiler_params=_params(),
    )(zg, dmixed, gv, ws, bt)


def _att_masks():
    tb = ATT_BLOCK
    lane = lax.broadcasted_iota(jnp.int32, (1, LANES), 1)
    rj = lax.broadcasted_iota(jnp.int32, (2 * tb, 2 * tb), 0)
    cs = lax.broadcasted_iota(jnp.int32, (2 * tb, 2 * tb), 1)
    same_head = ((rj < tb) & (cs < tb)) | ((rj >= tb) & (cs >= tb))
    suffix = jnp.where(same_head & (rj >= cs), 1.0, 0.0).astype(BF16)
    prefix = jnp.where(same_head & (rj <= cs), 1.0, 0.0).astype(BF16)
    left = lax.broadcasted_iota(jnp.int32, (1, 2 * tb), 1) < tb
    tq = lax.broadcasted_iota(jnp.int32, (ATT_Q, 4 * tb), 0)
    ts = lax.broadcasted_iota(jnp.int32, (ATT_Q, 4 * tb), 1)
    key = jnp.where(ts < 2 * tb, ts & (tb - 1), (ts & (tb - 1)) + tb)
    return lane, suffix, prefix, left, key < tq


def _att_fill(k_ref, v_ref, kcat, vcat, n_blocks, lane):
    tb = ATT_BLOCK
    first = lane < SB_HEAD_DIM

    def fill(jb, carry):
        rows = pl.ds(pl.multiple_of(jb * tb, tb), tb)
        top = pl.ds(pl.multiple_of(jb * 2 * tb, tb), tb)
        bot = pl.ds(pl.multiple_of(jb * 2 * tb + tb, tb), tb)
        kb = k_ref[rows, :]
        vb = v_ref[rows, :]
        zero = jnp.zeros_like(kb)
        kcat[top, :] = jnp.where(first, kb, zero)
        kcat[bot, :] = jnp.where(first, zero, kb)
        vcat[top, :] = jnp.where(first, vb, zero)
        vcat[bot, :] = jnp.where(first, zero, vb)
        return carry

    lax.fori_loop(0, n_blocks, fill, 0)


def _split_dot(x, m):
    hi = x.astype(BF16)
    lo = (x - hi.astype(F32)).astype(BF16)
    return _dot(hi, m) + _dot(lo, m)


def _log1m(z):
    sp = jnp.maximum(z, 0.0) + jnp.log1p(jnp.exp(-jnp.abs(z)))
    return -sp, sp


def _att_specs(t):
    n_pairs = SB_WIDTH // LANES
    q_spec = pl.BlockSpec((ATT_Q, LANES), lambda p, i: (i, p))
    k_spec = pl.BlockSpec((t, LANES), lambda p, i: (0, n_pairs + p))
    v_spec = pl.BlockSpec((t, LANES), lambda p, i: (0, 2 * n_pairs + p))
    return n_pairs, q_spec, k_spec, v_spec


def _attn_fwd(qkv):
    t = qkv.shape[0]
    tb = ATT_BLOCK
    nkb = t // tb
    assert 2 * nkb <= LANES and t % ATT_Q == 0 and ATT_Q == 2 * tb
    scale = SB_HEAD_DIM ** -0.5
    n_pairs, q_spec, k_spec, v_spec = _att_specs(t)

    def body(q_ref, k_ref, v_ref, o_ref, ct_ref, kcat, vcat, acc, carry):
        i = pl.program_id(1)
        lane, suffix, _, left, causal = _att_masks()

        @pl.when(i == 0)
        def _():
            _att_fill(k_ref, v_ref, kcat, vcat, nkb, lane)

        q = q_ref[...]
        acc[...] = jnp.zeros_like(acc)
        carry[...] = jnp.zeros_like(carry)
        ct_ref[0] = jnp.zeros((ATT_Q, LANES), F32)

        def step(m, masked):
            rows = pl.ds(pl.multiple_of(m * 4 * tb, 4 * tb), 4 * tb)
            z = _dot_nt(q, kcat[rows, :]) * scale
            l, _ = _log1m(z)
            if masked:
                l = jnp.where(causal, l, 0.0)
            probs = [None, None]
            for g in (1, 0):
                cols = slice(g * 2 * tb, (g + 1) * 2 * tb)
                j = 2 * m + g
                r = _split_dot(l[:, cols], suffix)
                c = carry[...]
                ct_ref[0] = jnp.where(lane == j, c[:, :tb], jnp.where(lane == nkb + j, c[:, tb:], ct_ref[0]))
                a = jnp.exp(z[:, cols] + r + c)
                if masked:
                    a = jnp.where(causal[:, cols], a, 0.0)
                probs[g] = a.astype(BF16)
                carry[...] = c + jnp.where(left, r[:, 0:1], r[:, tb:tb + 1])
            acc[...] += _dot(jnp.concatenate(probs, axis=1), vcat[rows, :])

        step(i, True)

        def loop(k, c):
            step(i - 1 - k, False)
            return c

        lax.fori_loop(0, i, loop, 0)
        o_ref[...] = acc[...].astype(BF16)

    return pl.pallas_call(
        body, name="attn_fwd", grid=(n_pairs, t // ATT_Q),
        in_specs=[q_spec, k_spec, v_spec],
        out_specs=[pl.BlockSpec((ATT_Q, LANES), lambda p, i: (i, p)), pl.BlockSpec((1, ATT_Q, LANES), lambda p, i: (p, i, 0))],
        out_shape=[jax.ShapeDtypeStruct((t, SB_WIDTH), BF16), jax.ShapeDtypeStruct((n_pairs, t, LANES), F32)],
        scratch_shapes=[pltpu.VMEM((2 * t, LANES), BF16), pltpu.VMEM((2 * t, LANES), BF16),
                        pltpu.VMEM((ATT_Q, LANES), F32), pltpu.VMEM((ATT_Q, 2 * tb), F32)],
        compiler_params=_params(2),
    )(qkv, qkv, qkv)


def _attn_bwd(qkv, dmixed, carries):
    t = qkv.shape[0]
    tb = ATT_BLOCK
    nkb = t // tb
    nq = t // ATT_Q
    scale = SB_HEAD_DIM ** -0.5
    n_pairs, q_spec, k_spec, v_spec = _att_specs(t)
    gm_blocks = GM_WIDTH // LANES

    def body(q_ref, k_ref, v_ref, do_ref, ct_ref, dq_ref, dk_ref, dv_ref, kcat, vcat, dkacc, dvacc, dqacc, carry):
        i = pl.program_id(1)
        lane, suffix, prefix, left, causal = _att_masks()
        first = lane < SB_HEAD_DIM

        @pl.when(i == 0)
        def _():
            _att_fill(k_ref, v_ref, kcat, vcat, nkb, lane)
            dkacc[...] = jnp.zeros_like(dkacc)
            dvacc[...] = jnp.zeros_like(dvacc)

        q = q_ref[...]
        do = do_ref[...]
        dqacc[...] = jnp.zeros_like(dqacc)
        carry[...] = jnp.zeros_like(carry)

        def step(m, masked):
            rows = pl.ds(pl.multiple_of(m * 4 * tb, 4 * tb), 4 * tb)
            kc = kcat[rows, :]
            z = _dot_nt(q, kc) * scale
            l, sp = _log1m(z)
            if masked:
                l = jnp.where(causal, l, 0.0)
            da = _dot_nt(do, vcat[rows, :])
            dzs, probs = [None, None], [None, None]
            for g in (0, 1):
                cols = slice(g * 2 * tb, (g + 1) * 2 * tb)
                j = 2 * m + g
                ct = ct_ref[0]
                ca = jnp.sum(jnp.where(lane == j, ct, 0.0), axis=1, keepdims=True)
                cb = jnp.sum(jnp.where(lane == nkb + j, ct, 0.0), axis=1, keepdims=True)
                r = _split_dot(l[:, cols], suffix)
                a = jnp.exp(z[:, cols] + r + jnp.where(left, ca, cb))
                if masked:
                    a = jnp.where(causal[:, cols], a, 0.0)
                de = da[:, cols] * a
                cl = _split_dot(de, prefix)
                pre = carry[...]
                dz = de - jnp.exp(z[:, cols] - sp[:, cols]) * (cl + pre)
                if masked:
                    dz = jnp.where(causal[:, cols], dz, 0.0)
                carry[...] = pre + jnp.where(left, cl[:, tb - 1:tb], cl[:, 2 * tb - 1:2 * tb])
                dzs[g] = (dz * scale).astype(BF16)
                probs[g] = a.astype(BF16)
            dzb = jnp.concatenate(dzs, axis=1)
            dqacc[...] += _dot(dzb, kc)
            dkc = _dot_tn(dzb, q)
            dvc = _dot_tn(jnp.concatenate(probs, axis=1), do)
            out_rows = pl.ds(pl.multiple_of(m * 2 * tb, 2 * tb), 2 * tb)
            pick = lambda x: jnp.concatenate([jnp.where(first, x[0:tb], x[tb:2 * tb]),
                                              jnp.where(first, x[2 * tb:3 * tb], x[3 * tb:4 * tb])], axis=0)
            dkacc[out_rows, :] += pick(dkc)
            dvacc[out_rows, :] += pick(dvc)

        def loop(m, c):
            step(m, False)
            return c

        lax.fori_loop(0, i, loop, 0)
        step(i, True)
        dq_ref[...] = dqacc[...].astype(BF16)

        @pl.when(i == nq - 1)
        def _():
            dk_ref[...] = dkacc[...].astype(BF16)
            dv_ref[...] = dvacc[...].astype(BF16)

    col = pl.BlockSpec((t, LANES), lambda p, i: (0, p))
    out = jax.ShapeDtypeStruct((t, SB_WIDTH), BF16)
    return pl.pallas_call(
        body, name="attn_bwd", grid=(n_pairs, nq),
        in_specs=[q_spec, k_spec, v_spec, pl.BlockSpec((ATT_Q, LANES), lambda p, i: (i, gm_blocks + p)),
                  pl.BlockSpec((1, ATT_Q, LANES), lambda p, i: (p, i, 0))],
        out_specs=[pl.BlockSpec((ATT_Q, LANES), lambda p, i: (i, p)), col, col],
        out_shape=[out, out, out],
        scratch_shapes=[pltpu.VMEM((2 * t, LANES), BF16), pltpu.VMEM((2 * t, LANES), BF16),
                        pltpu.VMEM((t, LANES), F32), pltpu.VMEM((t, LANES), F32),
                        pltpu.VMEM((ATT_Q, LANES), F32), pltpu.VMEM((ATT_Q, 2 * tb), F32)],
        compiler_params=_params(2),
    )(qkv, qkv, qkv, dmixed, carries)


def _matmul_residual(res, a, w, name):
    t = a.shape[0]
    tm = _token_tile(t)

    def body(res_ref, a_ref, w_ref, o_ref):
        o_ref[...] = res_ref[...] + _dot(a_ref[...], w_ref[...])

    return pl.pallas_call(
        body, name=name, grid=(t // tm,),
        in_specs=[_rows(tm, res.shape[1]), _rows(tm, a.shape[1]), _resident(w.shape)],
        out_specs=_rows(tm, res.shape[1]),
        out_shape=jax.ShapeDtypeStruct(res.shape, F32),
        compiler_params=_params(),
    )(res, a, w)


def _matmul_nt_cast(dy, w, name):
    t = dy.shape[0]
    tm = _token_tile(t)

    def body(dy_ref, w_ref, o_ref, dyb_ref):
        dyb = dy_ref[...].astype(BF16)
        dyb_ref[...] = dyb
        o_ref[...] = _dot_nt(dyb, w_ref[...]).astype(BF16)

    return pl.pallas_call(
        body, name=name, grid=(t // tm,),
        in_specs=[_rows(tm, dy.shape[1]), _resident(w.shape)],
        out_specs=[_rows(tm, w.shape[0]), _rows(tm, dy.shape[1])],
        out_shape=[jax.ShapeDtypeStruct((t, w.shape[0]), BF16), jax.ShapeDtypeStruct(dy.shape, BF16)],
        compiler_params=_params(),
    )(dy, w)


def _norm_input_bwd(dres, dz, w, h, g, name):
    t = h.shape[0]
    tm = _token_tile(t)
    nb, _, width = w.shape

    def body(dres_ref, dz_ref, w_ref, h_ref, g_ref, dh_ref, n_ref, dg_ref):
        i = pl.program_id(0)
        hh = h_ref[...]
        gg = g_ref[...]
        r = _rstd(hh)
        n_ref[...] = (hh * r * gg).astype(BF16)
        dn = jnp.zeros((tm, D_MODEL), F32)
        for b in range(nb):
            dn = dn + _dot_nt(dz_ref[:, b * width:(b + 1) * width], w_ref[b])
        dh, dg_rows = _rms_bwd(dn, hh, r, gg)
        dh_ref[...] = dres_ref[...] + dh

        @pl.when(i == 0)
        def _():
            dg_ref[...] = jnp.zeros_like(dg_ref)

        dg_ref[...] += jnp.sum(dg_rows, axis=0, keepdims=True)

    return pl.pallas_call(
        body, name=name, grid=(t // tm,),
        in_specs=[_rows(tm, D_MODEL), _rows(tm, nb * width), _resident(w.shape), _rows(tm, D_MODEL),
                  _resident((1, D_MODEL))],
        out_specs=[_rows(tm, D_MODEL), _rows(tm, D_MODEL), pl.BlockSpec((1, D_MODEL), lambda i: (0, 0))],
        out_shape=[jax.ShapeDtypeStruct((t, D_MODEL), F32), jax.ShapeDtypeStruct((t, D_MODEL), BF16),
                   jax.ShapeDtypeStruct((1, D_MODEL), F32)],
        compiler_params=_params(),
    )(dres, dz, w, h, g)


def _head(h, p, target, gple, gfin, wg, wproj):
    t = h.shape[0]
    tm = _token_tile(t)
    pw = D_MODEL // N_CHIPS

    def body(h_ref, p_ref, tgt_ref, gple_ref, gfin_ref, wg_ref, wproj_ref,
             loss_ref, dgf_ref, dh_ref, dgp_ref, dpp_ref, n_ref, pb_ref):
        i = pl.program_id(0)
        hh = h_ref[...]
        n = (hh * _rstd(hh) * gple_ref[...]).astype(BF16)
        n_ref[...] = n
        gate = jax.nn.sigmoid(_dot(n, wg_ref[...]))
        pb = p_ref[...].astype(BF16)
        pb_ref[...] = pb
        pp = jnp.concatenate([_dot(pb, wproj_ref[b]) for b in range(N_CHIPS)], axis=1)
        h4 = hh + gate * pp
        r = _rstd(h4)
        gf = gfin_ref[...]
        err = h4 * r * gf - tgt_ref[...]
        dy = err * (1.0 / D_MODEL)
        dh4, dgf_rows = _rms_bwd(dy, h4, r, gf)
        dh_ref[...] = dh4
        dgp_ref[...] = (dh4 * pp * gate * (1.0 - gate)).astype(BF16)
        dpp_ref[...] = (dh4 * gate).astype(BF16)

        @pl.when(i == 0)
        def _():
            loss_ref[...] = jnp.zeros_like(loss_ref)
            dgf_ref[...] = jnp.zeros_like(dgf_ref)

        loss_ref[...] += (0.5 / D_MODEL) * jnp.sum(err * err)
        dgf_ref[...] += jnp.sum(dgf_rows, axis=0, keepdims=True)

    bf = lambda w: jax.ShapeDtypeStruct((t, w), BF16)
    const = lambda i: (0, 0)
    return pl.pallas_call(
        body, name="head", grid=(t // tm,),
        in_specs=[_rows(tm, D_MODEL), _rows(tm, PLE_DIM), _rows(tm, D_MODEL), _resident((1, D_MODEL)),
                  _resident((1, D_MODEL)), _resident(wg.shape), _resident(wproj.shape)],
        out_specs=[pl.BlockSpec((1, LANES), const), pl.BlockSpec((1, D_MODEL), const), _rows(tm, D_MODEL),
                   _rows(tm, D_MODEL), _rows(tm, D_MODEL), _rows(tm, D_MODEL), _rows(tm, PLE_DIM)],
        out_shape=[jax.ShapeDtypeStruct((1, LANES), F32), jax.ShapeDtypeStruct((1, D_MODEL), F32),
                   jax.ShapeDtypeStruct((t, D_MODEL), F32), bf(D_MODEL), bf(D_MODEL), bf(D_MODEL), bf(PLE_DIM)],
        compiler_params=_params(),
    )(h, p, target, gple, gfin, wg, wproj)


def _device_step(x, p, target, small, big):
    w1in, w1out = big["ffn1_w_in"], big["ffn1_w_out"].reshape(D_FF, D_MODEL)
    w2in, w2out = big["ffn2_w_in"], big["ffn2_w_out"].reshape(D_FF, D_MODEL)
    wmix = big["w_mix_in"]
    wmo = big["w_mix_out"].reshape(D_MODEL, D_MODEL)
    wg = big["ple_w_gate"].reshape(D_MODEL, D_MODEL)
    wproj = big["ple_w_proj"]
    bt = small["gmlp_b"].T

    h1, gu1 = _ffn_fwd(x, small["ffn1_norm"], w1in, w1out, "ffn1_fwd")
    zg, qkv = _mix_in_fwd(h1, small["mix_norm"], wmix)
    gm = _gmlp_fwd(zg, small["gmlp_v_norm"], small["gmlp_w_s"], bt)
    att, carries = _attn_fwd(qkv)
    mixed = jnp.concatenate([gm, att], axis=1)
    h2 = _matmul_residual(h1, mixed, wmo, "mix_out_fwd")
    h3, gu2 = _ffn_fwd(h2, small["ffn2_norm"], w2in, w2out, "ffn2_fwd")
    loss, d_final, dh4, dgp, dpp, n4, pb = _head(h3, p, target, small["ple_norm"], small["final_norm"], wg, wproj)

    g_big, g_small = {}, {"final_norm": d_final}
    g_big["ple_w_gate"] = _wgrad_rows(n4, dgp, N_CHIPS, "wgrad_ple_gate")
    g_big["ple_w_proj"] = _wgrad_cols(pb, dpp, N_CHIPS, "wgrad_ple_proj")
    dh3, _, g_small["ple_norm"] = _norm_input_bwd(dh4, dgp, wg.reshape(1, D_MODEL, D_MODEL), h3, small["ple_norm"],
                                                  "ple_bwd")

    dh2, dgu2, n3, act2, dhh3, g_small["ffn2_norm"] = _ffn_bwd(dh3, h2, small["ffn2_norm"], gu2, w2in, w2out, "ffn2_bwd")
    g_big["ffn2_w_in"] = _wgrad_cols(n3, dgu2, N_CHIPS, "wgrad_ffn2_in")
    g_big["ffn2_w_out"] = _wgrad_rows(act2, dhh3, 2, "wgrad_ffn2_out")

    dmixed, dh2b = _matmul_nt_cast(dh2, wmo, "mix_out_bwd")
    g_big["w_mix_out"] = _wgrad_rows(mixed, dh2b, 2, "wgrad_mix_out")
    dzg, g_small["gmlp_w_s"], dbt, g_small["gmlp_v_norm"] = _gmlp_bwd(zg, dmixed, small["gmlp_v_norm"],
                                                                      small["gmlp_w_s"], bt)
    g_small["gmlp_b"] = dbt.T
    dq, dk, dv = _attn_bwd(qkv, dmixed, carries)
    dzmix = jnp.concatenate([dzg, dq, dk, dv], axis=1)
    dh1, n2, g_small["mix_norm"] = _norm_input_bwd(dh2, dzmix, wmix, h1, small["mix_norm"], "mix_in_bwd")
    g_big["w_mix_in"] = _wgrad_cols(n2, dzmix, N_CHIPS, "wgrad_mix_in")

    dx, dgu1, n1, act1, dhh1, g_small["ffn1_norm"] = _ffn_bwd(dh1, x, small["ffn1_norm"], gu1, w1in, w1out, "ffn1_bwd")
    g_big["ffn1_w_in"] = _wgrad_cols(n1, dgu1, N_CHIPS, "wgrad_ffn1_in")
    g_big["ffn1_w_out"] = _wgrad_rows(act1, dhh1, 2, "wgrad_ffn1_out")
    return loss, dx, g_big, g_small


_BIG = ("ffn1_w_in", "ffn1_w_out", "w_mix_in", "w_mix_out", "ffn2_w_in", "ffn2_w_out", "ple_w_gate", "ple_w_proj")
_SMALL = ("ffn1_norm", "mix_norm", "gmlp_v_norm", "gmlp_w_s", "gmlp_b", "ffn2_norm", "ple_norm", "final_norm")
_ALL = ("ffn1_norm", "ffn1_w_in", "ffn1_w_out", "mix_norm", "w_mix_in", "gmlp_v_norm", "gmlp_w_s", "gmlp_b", "w_mix_out",
        "ffn2_norm", "ffn2_w_in", "ffn2_w_out", "ple_norm", "ple_w_gate", "ple_w_proj", "final_norm")
_ANY = pl.BlockSpec(memory_space=pl.ANY)
_MESH = pl.DeviceIdType.MESH


def _mesh_pos():
    return lax.axis_index("x"), lax.axis_index("y"), lax.axis_index("c")


def _other_chips(x, y):
    return [((x, 1 - y), 2 * x + 1 - y), ((1 - x, y), 2 * (1 - x) + y), ((1 - x, 1 - y), 2 * (1 - x) + 1 - y)]


def _remote(src, dst, send_sem, recv_sem, device):
    return pltpu.make_async_remote_copy(src_ref=src, dst_ref=dst, send_sem=send_sem, recv_sem=recv_sem,
                                        device_id=device, device_id_type=_MESH)


def _all_gather(shards):
    n = len(shards)
    per = 2 * (N_CHIPS - 1)

    def body(*refs):
        ins, outs = refs[:n], refs[n:2 * n]
        send_sems, recv_sems, local_sems = refs[2 * n:]
        x, y, c = _mesh_pos()
        sibling = (x, y, 1 - c)
        chips = _other_chips(x, y)
        mine = 2 * x + y

        def half(w, blk, cc):
            hr = shards[w].shape[0] // 2
            return outs[w].at[blk, pl.ds(cc * hr, hr), :]

        local = []
        for w in range(n):
            cp = pltpu.make_async_copy(ins[w], outs[w].at[mine], local_sems.at[w])
            cp.start()
            local.append(cp)
        first = []
        for w in range(n):
            hr = shards[w].shape[0] // 2
            src = ins[w].at[pl.ds(c * hr, hr), :]
            for k, (chip, _) in enumerate(chips):
                cp = _remote(src, half(w, mine, c), send_sems.at[per * w + k], recv_sems.at[per * w + k], (*chip, c))
                cp.start()
                first.append(cp)
        passed = []
        for w in range(n):
            for k, (_, blk) in enumerate(chips):
                landed = half(w, blk, c)
                _remote(landed, landed, send_sems.at[per * w + k], recv_sems.at[per * w + k], sibling).wait_recv()
                cp = _remote(landed, landed, send_sems.at[per * w + 3 + k], recv_sems.at[per * w + 3 + k], sibling)
                cp.start()
                passed.append(cp)
        for w in range(n):
            for k, (_, blk) in enumerate(chips):
                other = half(w, blk, 1 - c)
                _remote(other, other, send_sems.at[per * w + 3 + k], recv_sems.at[per * w + 3 + k], sibling).wait_recv()
        for cp in first + passed:
            cp.wait_send()
        for cp in local:
            cp.wait()

    return pl.pallas_call(
        body, name="all_gather_weights",
        in_specs=[_ANY] * n, out_specs=[_ANY] * n,
        out_shape=[jax.ShapeDtypeStruct((N_CHIPS, *s.shape), s.dtype) for s in shards],
        scratch_shapes=[pltpu.SemaphoreType.DMA((per * n,)), pltpu.SemaphoreType.DMA((per * n,)),
                        pltpu.SemaphoreType.DMA((n,))],
    )(*shards)


def _pair_exchange(grads):
    n = len(grads)

    def body(*refs):
        ins, outs = refs[:n], refs[n:2 * n]
        send_sems, recv_sems = refs[2 * n:]
        x, y, c = _mesh_pos()
        cps = []
        for w in range(n):
            hr = grads[w].shape[1] // 2
            cp = _remote(ins[w].at[:, pl.ds((1 - c) * hr, hr), :], outs[w], send_sems.at[w], recv_sems.at[w], (x, y, 1 - c))
            cp.start()
            cps.append(cp)
        for cp in cps:
            cp.wait()

    return pl.pallas_call(
        body, name="grad_pair_exchange",
        in_specs=[_ANY] * n, out_specs=[_ANY] * n,
        out_shape=[jax.ShapeDtypeStruct((g.shape[0], g.shape[1] // 2, g.shape[2]), g.dtype) for g in grads],
        scratch_shapes=[pltpu.SemaphoreType.DMA((n,)), pltpu.SemaphoreType.DMA((n,))],
    )(*grads)


def _pair_sum(g, a, pos, name):
    nb, r, c = g.shape
    hr = r // 2

    def body(pos_ref, g_ref, a_ref, o_ref):
        o_ref[...] = (g_ref[...] + a_ref[...]).astype(BF16)

    return pl.pallas_call(
        body, name=name,
        grid_spec=pltpu.PrefetchScalarGridSpec(
            num_scalar_prefetch=1, grid=(nb,),
            in_specs=[pl.BlockSpec((1, hr, c), lambda k, pos: (k ^ pos[0], pos[1], 0)),
                      pl.BlockSpec((1, hr, c), lambda k, pos: (k ^ pos[0], 0, 0))],
            out_specs=pl.BlockSpec((1, hr, c), lambda k, pos: (k, 0, 0))),
        out_shape=jax.ShapeDtypeStruct((nb, hr, c), BF16),
        compiler_params=_params(),
    )(pos, g, a)


def _chip_exchange(sums):
    n = len(sums)
    per = N_CHIPS - 1

    def body(*refs):
        ins, outs = refs[:n], refs[n:2 * n]
        send_sems, recv_sems = refs[2 * n:]
        x, y, c = _mesh_pos()
        cps = []
        for w in range(n):
            for k, (chip, _) in enumerate(_other_chips(x, y)):
                cp = _remote(ins[w].at[k + 1], outs[w].at[k], send_sems.at[per * w + k], recv_sems.at[per * w + k],
                             (*chip, c))
                cp.start()
                cps.append(cp)
        for cp in cps:
            cp.wait()

    return pl.pallas_call(
        body, name="grad_chip_exchange",
        in_specs=[_ANY] * n, out_specs=[_ANY] * n,
        out_shape=[jax.ShapeDtypeStruct((per, *s.shape[1:]), s.dtype) for s in sums],
        scratch_shapes=[pltpu.SemaphoreType.DMA((per * n,)), pltpu.SemaphoreType.DMA((per * n,))],
    )(*sums)


def _chip_sum(s, b, pos, name):
    _, hr, c = s.shape

    def body(pos_ref, s_ref, b_ref, o_ref):
        o_ref[...] = (s_ref[0].astype(F32) + b_ref[0].astype(F32)) + (b_ref[1].astype(F32) + b_ref[2].astype(F32))

    return pl.pallas_call(
        body, name=name,
        grid_spec=pltpu.PrefetchScalarGridSpec(
            num_scalar_prefetch=1, grid=(1,),
            in_specs=[pl.BlockSpec((1, hr, c), lambda k, pos: (0, 0, 0)), pl.BlockSpec((N_CHIPS - 1, hr, c), lambda k, pos: (0, 0, 0))],
            out_specs=pl.BlockSpec((hr, c), lambda k, pos: (pos[1], 0))),
        out_shape=jax.ShapeDtypeStruct((2 * hr, c), F32),
        compiler_params=_params(),
    )(pos, s, b)


def _pair_share(grads):
    n = len(grads)

    def body(*refs):
        outs = refs[n:2 * n]
        send_sems, recv_sems = refs[2 * n:]
        x, y, c = _mesh_pos()
        cps = []
        for w in range(n):
            hr = grads[w].shape[0] // 2
            rows = outs[w].at[pl.ds(c * hr, hr), :]
            cp = _remote(rows, rows, send_sems.at[w], recv_sems.at[w], (x, y, 1 - c))
            cp.start()
            cps.append(cp)
        for w, cp in enumerate(cps):
            cp.wait_send()
            hr = grads[w].shape[0] // 2
            other = outs[w].at[pl.ds((1 - c) * hr, hr), :]
            _remote(other, other, send_sems.at[w], recv_sems.at[w], (x, y, 1 - c)).wait_recv()

    return pl.pallas_call(
        body, name="grad_pair_share",
        in_specs=[_ANY] * n, out_specs=[_ANY] * n,
        out_shape=[jax.ShapeDtypeStruct(g.shape, g.dtype) for g in grads],
        input_output_aliases={w: w for w in range(n)},
        scratch_shapes=[pltpu.SemaphoreType.DMA((n,)), pltpu.SemaphoreType.DMA((n,))],
    )(*grads)


def _all_reduce_small(buf):
    n_dev = 8

    def body(in_ref, o_ref, slots, send_sems, recv_sems):
        x, y, c = _mesh_pos()
        me = 4 * x + 2 * y + c
        slots[0] = in_ref[...]
        cps = []
        for q in range(1, n_dev):
            peer = (x ^ (q >> 2), y ^ ((q >> 1) & 1), c ^ (q & 1))
            cp = _remote(in_ref, slots.at[q], send_sems.at[q - 1], recv_sems.at[q - 1], peer)
            cp.start()
            cps.append(cp)
        for cp in cps:
            cp.wait()
        acc = slots[me]
        for d in range(1, n_dev):
            acc = acc + slots[d ^ me]
        o_ref[...] = acc

    return pl.pallas_call(
        body, name="all_reduce_small",
        in_specs=[pl.BlockSpec(memory_space=pltpu.VMEM)], out_specs=pl.BlockSpec(memory_space=pltpu.VMEM),
        out_shape=jax.ShapeDtypeStruct(buf.shape, buf.dtype),
        scratch_shapes=[pltpu.VMEM((n_dev, *buf.shape), buf.dtype), pltpu.SemaphoreType.DMA((n_dev - 1,)),
                        pltpu.SemaphoreType.DMA((n_dev - 1,))],
    )(buf)


def _adamw(w, g, m, v, name):
    r, c = w.shape
    tr = r if r * c * 4 <= (1 << 20) else 64
    bias1 = 1.0 - ADAM_B1 ** ADAM_STEP
    bias2 = 1.0 - ADAM_B2 ** ADAM_STEP

    def body(w_ref, g_ref, m_ref, v_ref, d_ref, mo_ref, vo_ref):
        gg = g_ref[...]
        m2 = ADAM_B1 * m_ref[...] + (1.0 - ADAM_B1) * gg
        v2 = ADAM_B2 * v_ref[...] + (1.0 - ADAM_B2) * (gg * gg)
        mo_ref[...] = m2
        vo_ref[...] = v2
        d_ref[...] = -ADAM_LR * ((m2 / bias1) / (jnp.sqrt(v2 / bias2) + ADAM_EPS) + ADAM_WD * w_ref[...])

    spec = pl.BlockSpec((tr, c), lambda i: (i, 0))
    out = jax.ShapeDtypeStruct((r, c), F32)
    return pl.pallas_call(
        body, name=name, grid=(r // tr,), in_specs=[spec] * 4, out_specs=[spec] * 3, out_shape=[out] * 3,
        compiler_params=_params(),
    )(w, g, m, v)


def _pack(parts):
    flat = jnp.concatenate([parts[n].reshape(-1) for n in _SMALL])
    return flat.reshape(-1, LANES)


def _unpack(buf, like):
    flat = buf.reshape(-1)
    out, at = {}, 0
    for n in _SMALL:
        size = like[n].size
        out[n] = flat[at:at + size].reshape(like[n].shape)
        at += size
    return out


def kernel(x, p, ffn1_norm, ffn1_w_in, ffn1_w_out, mix_norm, w_mix_in, gmlp_v_norm, gmlp_w_s, gmlp_b, w_mix_out, ffn2_norm, ffn2_w_in, ffn2_w_out, ple_norm, ple_w_gate, ple_w_proj, final_norm, loss_target, m_ffn1_norm, m_ffn1_w_in, m_ffn1_w_out, m_mix_norm, m_w_mix_in, m_gmlp_v_norm, m_gmlp_w_s, m_gmlp_b, m_w_mix_out, m_ffn2_norm, m_ffn2_w_in, m_ffn2_w_out, m_ple_norm, m_ple_w_gate, m_ple_w_proj, m_final_norm, v_ffn1_norm, v_ffn1_w_in, v_ffn1_w_out, v_mix_norm, v_w_mix_in, v_gmlp_v_norm, v_gmlp_w_s, v_gmlp_b, v_w_mix_out, v_ffn2_norm, v_ffn2_w_in, v_ffn2_w_out, v_ple_norm, v_ple_w_gate, v_ple_w_proj, v_final_norm):
    args = dict(locals())
    w = {n: args[n] for n in _ALL}
    m = {n: args["m_" + n] for n in _ALL}
    v = {n: args["v_" + n] for n in _ALL}
    axes = ("x", "y", "c")
    xi, yi, ci = _mesh_pos()
    pos = jnp.stack([2 * xi + yi, ci]).astype(jnp.int32)

    shard = {n: w[n][0] for n in _BIG}
    gathered = _all_gather([shard[n].astype(BF16) for n in _BIG])
    small = {n: (w[n][0] if w[n].ndim > 2 else w[n].reshape(1, -1)) for n in _SMALL}
    loss_part, dx, g_big, g_small = _device_step(x[0], p[0, 0], loss_target[0], small, dict(zip(_BIG, gathered)))
    loss = lax.psum(loss_part[0, 0], axes)

    partial = [g_big[n].reshape(N_CHIPS, *shard[n].shape) for n in _BIG]
    from_sibling = _pair_exchange(partial)
    pair = [_pair_sum(g, a, pos, "pair_sum_" + n) for n, g, a in zip(_BIG, partial, from_sibling)]
    from_chips = _chip_exchange(pair)
    halves = [_chip_sum(s, b, pos, "chip_sum_" + n) for n, s, b in zip(_BIG, pair, from_chips)]
    grads = dict(zip(_BIG, _pair_share(halves)))

    delta, new_m, new_v = {}, {}, {}
    for n in _BIG:
        d2, m2, v2 = _adamw(shard[n], grads[n], m[n][0], v[n][0], "adamw_" + n)
        grads[n], delta[n], new_m[n], new_v[n] = grads[n][None], d2[None], m2[None], v2[None]

    g_packed = _all_reduce_small(_pack(g_small))
    d_p, m_p, v_p = _adamw(_pack(w), g_packed, _pack(m), _pack(v), "adamw_small")
    for out, buf in ((grads, g_packed), (delta, d_p), (new_m, m_p), (new_v, v_p)):
        out.update(_unpack(buf, w))

    return (loss, dx[None], *[grads[n] for n in _ALL], *[delta[n] for n in _ALL], *[new_m[n] for n in _ALL],
            *[new_v[n] for n in _ALL])
```

```python
import functools

import jax
import jax.numpy as jnp
from jax import lax
from jax.experimental import pallas as pl
from jax.experimental.pallas import tpu as pltpu

F32, BF16 = jnp.float32, jnp.bfloat16

D_MODEL = 1024
D_FF = 2816
FF_BLOCK = 2 * D_FF // 4
PLE_DIM = 256
CHUNK = 128
GM_HEADS = 4
GM_WIDTH = 512
SB_HEAD_DIM = 64
SB_WIDTH = 512
MIX_IN_WIDTH = 2 * GM_WIDTH + 3 * SB_WIDTH
MIX_BLOCK = MIX_IN_WIDTH // 4
EPS = 1e-6
N_CHIPS = 4
LANES = 128
ATT_BLOCK = 128
ATT_Q = 512
VMEM_LIMIT = 56 * 1024 * 1024

ADAM_LR, ADAM_B1, ADAM_B2, ADAM_EPS, ADAM_WD, ADAM_STEP = 0.001, 0.9, 0.999, 1e-08, 0.01, 10


def _dot(a, b):
    return jnp.dot(a, b, preferred_element_type=F32)


def _dot_nt(a, b):
    return lax.dot_general(a, b, (((1,), (1,)), ((), ())), preferred_element_type=F32)


def _dot_tn(a, b):
    return lax.dot_general(a, b, (((0,), (0,)), ((), ())), preferred_element_type=F32)


def _resident(shape):
    nd = len(shape)
    return pl.BlockSpec(shape, lambda *_: (0,) * nd, pipeline_mode=pl.Buffered(1))


def _rows(tm, width):
    return pl.BlockSpec((tm, width), lambda i: (i, 0))


def _params(n_axes=1):
    return pltpu.CompilerParams(dimension_semantics=("arbitrary",) * n_axes, vmem_limit_bytes=VMEM_LIMIT)


def _rstd(h):
    return lax.rsqrt(jnp.mean(h * h, axis=-1, keepdims=True) + EPS)


def _rms_bwd(dy, h, r, g):
    dyg = dy * g
    dh = r * dyg - h * (r * r * r) * jnp.mean(dyg * h, axis=-1, keepdims=True)
    return dh, dy * h * r


def _gelu(x):
    return 0.5 * x * (1.0 + lax.erf(x * (2.0 ** -0.5)))


def _gelu_grad(x):
    return 0.5 * (1.0 + lax.erf(x * (2.0 ** -0.5))) + x * jnp.exp(-0.5 * x * x) * ((2.0 * jnp.pi) ** -0.5)


def _token_tile(t):
    return min(256, t)


def _ffn_fwd(h, g, win, wout, name):
    t = h.shape[0]
    tm = _token_tile(t)

    def body(h_ref, g_ref, win_ref, wout_ref, ho_ref, gu_ref):
        hh = h_ref[...]
        n = (hh * _rstd(hh) * g_ref[...]).astype(BF16)
        acc = jnp.zeros((tm, D_MODEL), F32)
        for jb in range(2):
            gate = _dot(n, win_ref[jb])
            up = _dot(n, win_ref[2 + jb])
            gu_ref[:, jb * FF_BLOCK:(jb + 1) * FF_BLOCK] = gate.astype(BF16)
            gu_ref[:, D_FF + jb * FF_BLOCK:D_FF + (jb + 1) * FF_BLOCK] = up.astype(BF16)
            act = (gate * jax.nn.sigmoid(gate) * up).astype(BF16)
            acc = acc + _dot(act, wout_ref[jb * FF_BLOCK:(jb + 1) * FF_BLOCK, :])
        ho_ref[...] = hh + 0.5 * acc

    return pl.pallas_call(
        body, name=name, grid=(t // tm,),
        in_specs=[_rows(tm, D_MODEL), _resident((1, D_MODEL)), _resident(win.shape), _resident(wout.shape)],
        out_specs=[_rows(tm, D_MODEL), _rows(tm, 2 * D_FF)],
        out_shape=[jax.ShapeDtypeStruct((t, D_MODEL), F32), jax.ShapeDtypeStruct((t, 2 * D_FF), BF16)],
        compiler_params=_params(),
    )(h, g, win, wout)


def _ffn_bwd(dho, h, g, gu, win, wout, name):
    t = h.shape[0]
    tm = _token_tile(t)

    def body(dho_ref, h_ref, g_ref, gu_ref, win_ref, wout_ref, dh_ref, dgu_ref, n_ref, act_ref, dhh_ref, dg_ref):
        i = pl.program_id(0)
        hh = h_ref[...]
        gg = g_ref[...]
        r = _rstd(hh)
        n_ref[...] = (hh * r * gg).astype(BF16)
        dho = dho_ref[...]
        dhh = (0.5 * dho).astype(BF16)
        dhh_ref[...] = dhh
        dn = jnp.zeros((tm, D_MODEL), F32)
        for jb in range(2):
            cg = slice(jb * FF_BLOCK, (jb + 1) * FF_BLOCK)
            cu = slice(D_FF + jb * FF_BLOCK, D_FF + (jb + 1) * FF_BLOCK)
            dact = _dot_nt(dhh, wout_ref[cg, :])
            gate = gu_ref[:, cg].astype(F32)
            up = gu_ref[:, cu].astype(F32)
            sg = jax.nn.sigmoid(gate)
            silu = gate * sg
            act_ref[:, cg] = (silu * up).astype(BF16)
            dgate = (dact * up * (sg * (1.0 + gate * (1.0 - sg)))).astype(BF16)
            dup = (dact * silu).astype(BF16)
            dgu_ref[:, cg] = dgate
            dgu_ref[:, cu] = dup
            dn = dn + _dot_nt(dgate, win_ref[jb]) + _dot_nt(dup, win_ref[2 + jb])
        dh, dg_rows = _rms_bwd(dn, hh, r, gg)
        dh_ref[...] = dho + dh

        @pl.when(i == 0)
        def _():
            dg_ref[...] = jnp.zeros_like(dg_ref)

        dg_ref[...] += jnp.sum(dg_rows, axis=0, keepdims=True)

    return pl.pallas_call(
        body, name=name, grid=(t // tm,),
        in_specs=[_rows(tm, D_MODEL), _rows(tm, D_MODEL), _resident((1, D_MODEL)), _rows(tm, 2 * D_FF),
                  _resident(win.shape), _resident(wout.shape)],
        out_specs=[_rows(tm, D_MODEL), _rows(tm, 2 * D_FF), _rows(tm, D_MODEL), _rows(tm, D_FF), _rows(tm, D_MODEL),
                   pl.BlockSpec((1, D_MODEL), lambda i: (0, 0))],
        out_shape=[jax.ShapeDtypeStruct((t, D_MODEL), F32), jax.ShapeDtypeStruct((t, 2 * D_FF), BF16),
                   jax.ShapeDtypeStruct((t, D_MODEL), BF16), jax.ShapeDtypeStruct((t, D_FF), BF16),
                   jax.ShapeDtypeStruct((t, D_MODEL), BF16), jax.ShapeDtypeStruct((1, D_MODEL), F32)],
        compiler_params=_params(),
    )(dho, h, g, gu, win, wout)


def _wgrad(a, b, out_shape, out_block, out_index, a_width, b_width, grid_ij, name):
    t = a.shape[0]
    tk = min(512, t)

    def body(a_ref, b_ref, o_ref):
        k = pl.program_id(2)
        prod = _dot_tn(a_ref[...], b_ref[...]).reshape(o_ref.shape)

        @pl.when(k == 0)
        def _():
            o_ref[...] = prod

        @pl.when(k > 0)
        def _():
            o_ref[...] += prod

    return pl.pallas_call(
        body, name=name, grid=(*grid_ij, t // tk),
        in_specs=[pl.BlockSpec((tk, a_width), lambda i, j, k: (k, i)), pl.BlockSpec((tk, b_width), lambda i, j, k: (k, j))],
        out_specs=pl.BlockSpec(out_block, lambda i, j, k: out_index(i, j)),
        out_shape=jax.ShapeDtypeStruct(out_shape, F32),
        compiler_params=_params(3),
    )(a, b)


def _wgrad_cols(a, b, n_blocks, name):
    ka, nb = a.shape[1], b.shape[1] // n_blocks
    return _wgrad(a, b, (n_blocks, ka, nb), (1, ka, nb), lambda i, j: (j, 0, 0), ka, nb, (1, n_blocks), name)


def _wgrad_rows(a, b, n_blocks, name):
    ka, nb = a.shape[1] // n_blocks, b.shape[1]
    return _wgrad(a, b, (a.shape[1], nb), (ka, nb), lambda i, j: (i, 0), ka, nb, (n_blocks, 1), name)


def _mix_in_fwd(h, g, wmix):
    t = h.shape[0]
    tm = _token_tile(t)
    gw2 = 2 * GM_WIDTH

    def body(h_ref, g_ref, w_ref, zg_ref, qkv_ref):
        hh = h_ref[...]
        n = (hh * _rstd(hh) * g_ref[...]).astype(BF16)
        for b in range(N_CHIPS):
            z = _dot(n, w_ref[b])
            lo, hi = b * MIX_BLOCK, (b + 1) * MIX_BLOCK
            if hi <= gw2:
                zg_ref[:, lo:hi] = z
            elif lo >= gw2:
                qkv_ref[:, lo - gw2:hi - gw2] = z.astype(BF16)
            else:
                zg_ref[:, lo:gw2] = z[:, :gw2 - lo]
                qkv_ref[:, 0:hi - gw2] = z[:, gw2 - lo:].astype(BF16)

    return pl.pallas_call(
        body, name="mix_in_fwd", grid=(t // tm,),
        in_specs=[_rows(tm, D_MODEL), _resident((1, D_MODEL)), _resident(wmix.shape)],
        out_specs=[_rows(tm, gw2), _rows(tm, 3 * SB_WIDTH)],
        out_shape=[jax.ShapeDtypeStruct((t, gw2), F32), jax.ShapeDtypeStruct((t, 3 * SB_WIDTH), BF16)],
        compiler_params=_params(),
    )(h, g, wmix)


def _causal_chunk_mask():
    row = lax.broadcasted_iota(jnp.int32, (CHUNK, CHUNK), 0)
    col = lax.broadcasted_iota(jnp.int32, (CHUNK, CHUNK), 1)
    return row >= col


def _gmlp_tile(t):
    return min(512, t)


def _gmlp_fwd(zg, gv, ws, bt):
    t = zg.shape[0]
    tm = _gmlp_tile(t)

    def body(zg_ref, gv_ref, ws_ref, bt_ref, o_ref):
        u = _gelu(zg_ref[:, :GM_WIDTH])
        v = _gelu(zg_ref[:, GM_WIDTH:])
        vn = (v * _rstd(v) * gv_ref[...]).astype(BF16)
        mask = _causal_chunk_mask()
        for hd in range(GM_HEADS):
            wm = jnp.where(mask, ws_ref[hd], 0.0).astype(BF16)
            cols = slice(hd * CHUNK, (hd + 1) * CHUNK)
            for c in range(tm // CHUNK):
                rows = slice(c * CHUNK, (c + 1) * CHUNK)
                sv = _dot(wm, vn[rows, cols]) + bt_ref[:, hd:hd + 1]
                o_ref[rows, cols] = (u[rows, cols] * sv).astype(BF16)

    return pl.pallas_call(
        body, name="gmlp_fwd", grid=(t // tm,),
        in_specs=[_rows(tm, 2 * GM_WIDTH), _resident((1, GM_WIDTH)), _resident(ws.shape), _resident(bt.shape)],
        out_specs=_rows(tm, GM_WIDTH),
        out_shape=jax.ShapeDtypeStruct((t, GM_WIDTH), BF16),
        compiler_params=_params(),
    )(zg, gv, ws, bt)


def _gmlp_bwd(zg, dmixed, gv, ws, bt):
    t = zg.shape[0]
    tm = _gmlp_tile(t)

    def body(zg_ref, dgm_ref, gv_ref, ws_ref, bt_ref, dzg_ref, dws_ref, dbt_ref, dgv_ref):
        i = pl.program_id(0)

        @pl.when(i == 0)
        def _():
            dws_ref[...] = jnp.zeros_like(dws_ref)
            dbt_ref[...] = jnp.zeros_like(dbt_ref)
            dgv_ref[...] = jnp.zeros_like(dgv_ref)

        zu = zg_ref[:, :GM_WIDTH]
        zv = zg_ref[:, GM_WIDTH:]
        u = _gelu(zu)
        v = _gelu(zv)
        r = _rstd(v)
        gvv = gv_ref[...]
        vn = (v * r * gvv).astype(BF16)
        dgm = dgm_ref[...].astype(F32)
        dsv = (dgm * u).astype(BF16)
        mask = _causal_chunk_mask()
        du_cols, dvn_cols = [], []
        for hd in range(GM_HEADS):
            wm = jnp.where(mask, ws_ref[hd], 0.0).astype(BF16)
            cols = slice(hd * CHUNK, (hd + 1) * CHUNK)
            dw = jnp.zeros((CHUNK, CHUNK), F32)
            db = jnp.zeros((CHUNK, 1), F32)
            du_rows, dvn_rows = [], []
            for c in range(tm // CHUNK):
                rows = slice(c * CHUNK, (c + 1) * CHUNK)
                sv = _dot(wm, vn[rows, cols]) + bt_ref[:, hd:hd + 1]
                du_rows.append(dgm[rows, cols] * sv)
                dvn_rows.append(_dot_tn(wm, dsv[rows, cols]))
                dw = dw + _dot_nt(dsv[rows, cols], vn[rows, cols])
                db = db + jnp.sum(dsv[rows, cols].astype(F32), axis=1, keepdims=True)
            dws_ref[hd] += jnp.where(mask, dw, 0.0)
            dbt_ref[:, hd:hd + 1] += db
            du_cols.append(jnp.concatenate(du_rows, axis=0))
            dvn_cols.append(jnp.concatenate(dvn_rows, axis=0))
        du = jnp.concatenate(du_cols, axis=1)
        dvn = jnp.concatenate(dvn_cols, axis=1)
        dv, dgv_rows = _rms_bwd(dvn, v, r, gvv)
        dgv_ref[...] += jnp.sum(dgv_rows, axis=0, keepdims=True)
        dzg_ref[:, :GM_WIDTH] = (du * _gelu_grad(zu)).astype(BF16)
        dzg_ref[:, GM_WIDTH:] = (dv * _gelu_grad(zv)).astype(BF16)

    const = lambda nd: (lambda i: (0,) * nd)
    return pl.pallas_call(
        body, name="gmlp_bwd", grid=(t // tm,),
        in_specs=[_rows(tm, 2 * GM_WIDTH), _rows(tm, GM_WIDTH), _resident((1, GM_WIDTH)), _resident(ws.shape),
                  _resident(bt.shape)],
        out_specs=[_rows(tm, 2 * GM_WIDTH), pl.BlockSpec(ws.shape, const(3)), pl.BlockSpec(bt.shape, const(2)),
                   pl.BlockSpec((1, GM_WIDTH), const(2))],
        out_shape=[jax.ShapeDtypeStruct((t, 2 * GM_WIDTH), BF16), jax.ShapeDtypeStruct(ws.shape, F32),
                   jax.ShapeDtypeStruct(bt.shape, F32), jax.ShapeDtypeStruct((1, GM_WIDTH), F32)],
        compiler_params=_params(),
    )(zg, dmixed, gv, ws, bt)


def _att_masks():
    tb = ATT_BLOCK
    lane = lax.broadcasted_iota(jnp.int32, (1, LANES), 1)
    rj = lax.broadcasted_iota(jnp.int32, (2 * tb, 2 * tb), 0)
    cs = lax.broadcasted_iota(jnp.int32, (2 * tb, 2 * tb), 1)
    same_head = ((rj < tb) & (cs < tb)) | ((rj >= tb) & (cs >= tb))
    suffix = jnp.where(same_head & (rj >= cs), 1.0, 0.0).astype(BF16)
    prefix = jnp.where(same_head & (rj <= cs), 1.0, 0.0).astype(BF16)
    suffix, prefix = jnp.concatenate([suffix, suffix], axis=0), jnp.concatenate([prefix, prefix], axis=0)
    left = lax.broadcasted_iota(jnp.int32, (1, 2 * tb), 1) < tb
    tq = lax.broadcasted_iota(jnp.int32, (ATT_Q, 4 * tb), 0)
    ts = lax.broadcasted_iota(jnp.int32, (ATT_Q, 4 * tb), 1)
    key = jnp.where(ts < 2 * tb, ts & (tb - 1), (ts & (tb - 1)) + tb)
    return lane, suffix, prefix, left, key, tq


def _att_fill(k_ref, v_ref, kcat, vcat, n_blocks, lane):
    tb = ATT_BLOCK
    first = lane < SB_HEAD_DIM

    def fill(jb, carry):
        rows = pl.ds(pl.multiple_of(jb * tb, tb), tb)
        top = pl.ds(pl.multiple_of(jb * 2 * tb, tb), tb)
        bot = pl.ds(pl.multiple_of(jb * 2 * tb + tb, tb), tb)
        kb = k_ref[rows, :]
        vb = v_ref[rows, :]
        zero = jnp.zeros_like(kb)
        kcat[top, :] = jnp.where(first, kb, zero)
        kcat[bot, :] = jnp.where(first, zero, kb)
        vcat[top, :] = jnp.where(first, vb, zero)
        vcat[bot, :] = jnp.where(first, zero, vb)
        return carry

    lax.fori_loop(0, n_blocks, fill, 0)


def _split_dot(x, m2):
    hi = x.astype(BF16)
    lo = (x - hi.astype(F32)).astype(BF16)
    return _dot(jnp.concatenate([hi, lo], axis=1), m2)


def _log1m(z):
    return jnp.minimum(-z, 0.0) - jnp.log(1.0 + jnp.exp(-jnp.abs(z)))


def _scaled_queries(q_ref):
    return (q_ref[...].astype(F32) * (SB_HEAD_DIM ** -0.5)).astype(BF16)


def _att_specs(t):
    n_pairs = SB_WIDTH // LANES
    q_spec = pl.BlockSpec((ATT_Q, LANES), lambda p, i: (i, p))
    k_spec = pl.BlockSpec((t, LANES), lambda p, i: (0, n_pairs + p))
    v_spec = pl.BlockSpec((t, LANES), lambda p, i: (0, 2 * n_pairs + p))
    return n_pairs, q_spec, k_spec, v_spec


def _attn_fwd(qkv):
    t = qkv.shape[0]
    tb = ATT_BLOCK
    nkb = t // tb
    assert 2 * nkb <= LANES and t % ATT_Q == 0 and ATT_Q == 4 * tb
    n_pairs, q_spec, k_spec, v_spec = _att_specs(t)

    def body(q_ref, k_ref, v_ref, o_ref, ct_ref, kcat, vcat, acc, carry, z0, r0, z1, r1):
        i = pl.program_id(1)
        lane, suffix, _, left, key, tq = _att_masks()

        @pl.when(i == 0)
        def _():
            _att_fill(k_ref, v_ref, kcat, vcat, nkb, lane)

        q = _scaled_queries(q_ref)
        acc[...] = jnp.zeros_like(acc)
        carry[...] = jnp.zeros_like(carry)
        ct_ref[0] = jnp.zeros((ATT_Q, LANES), F32)

        def key_rows(m):
            return pl.ds(pl.multiple_of(m * 4 * tb, 4 * tb), 4 * tb)

        def scores(m, zb, rb, causal=None):
            z = _dot_nt(q, kcat[key_rows(m), :])
            zb[...] = z
            l = _log1m(z)
            if causal is not None:
                l = jnp.where(causal, l, 0.0)
            for g in (1, 0):
                cols = slice(g * 2 * tb, (g + 1) * 2 * tb)
                rb[:, cols] = _split_dot(l[:, cols], suffix)

        def weigh(m, zb, rb, causal=None):
            probs = [None, None]
            for g in (1, 0):
                cols = slice(g * 2 * tb, (g + 1) * 2 * tb)
                j = 2 * m + g
                r = rb[:, cols]
                c = carry[...]
                ct_ref[0] = jnp.where(lane == j, c[:, :tb], jnp.where(lane == nkb + j, c[:, tb:], ct_ref[0]))
                a = jnp.exp(zb[:, cols] + r + c)
                if causal is not None:
                    a = jnp.where(causal[:, cols], a, 0.0)
                probs[g] = a.astype(BF16)
                carry[...] = c + jnp.where(left, r[:, 0:1], r[:, tb:tb + 1])
            acc[...] += _dot(jnp.concatenate(probs, axis=1), vcat[key_rows(m), :])

        for e in (1, 0):
            causal = key + e * 2 * tb < tq
            scores(2 * i + e, z0, r0, causal)
            weigh(2 * i + e, z0, r0, causal)

        @pl.when(i > 0)
        def _():
            scores(2 * i - 1, z1, r1)

            def loop(k, c):
                u = i - 1 - k
                scores(2 * u, z0, r0)
                weigh(2 * u + 1, z1, r1)
                scores(jnp.maximum(2 * u - 1, 0), z1, r1)
                weigh(2 * u, z0, r0)
                return c

            lax.fori_loop(0, i, loop, 0)

        o_ref[...] = acc[...].astype(BF16)

    tile = pltpu.VMEM((ATT_Q, 4 * tb), F32)

    return pl.pallas_call(
        body, name="attn_fwd", grid=(n_pairs, t // ATT_Q),
        in_specs=[q_spec, k_spec, v_spec],
        out_specs=[pl.BlockSpec((ATT_Q, LANES), lambda p, i: (i, p)), pl.BlockSpec((1, ATT_Q, LANES), lambda p, i: (p, i, 0))],
        out_shape=[jax.ShapeDtypeStruct((t, SB_WIDTH), BF16), jax.ShapeDtypeStruct((n_pairs, t, LANES), F32)],
        scratch_shapes=[pltpu.VMEM((2 * t, LANES), BF16), pltpu.VMEM((2 * t, LANES), BF16),
                        pltpu.VMEM((ATT_Q, LANES), F32), pltpu.VMEM((ATT_Q, 2 * tb), F32), tile, tile, tile, tile],
        compiler_params=_params(2),
    )(qkv, qkv, qkv)


def _attn_bwd(qkv, dmixed, carries):
    t = qkv.shape[0]
    tb = ATT_BLOCK
    nkb = t // tb
    nq = t // ATT_Q
    scale = SB_HEAD_DIM ** -0.5
    n_pairs, q_spec, k_spec, v_spec = _att_specs(t)
    gm_blocks = GM_WIDTH // LANES

    def body(q_ref, k_ref, v_ref, do_ref, ct_ref, dq_ref, dk_ref, dv_ref, kcat, vcat, dkacc, dvacc, dqacc, carry,
             z0, r0, s0, a0, z1, r1, s1, a1):
        i = pl.program_id(1)
        lane, suffix, prefix, left, key, tq = _att_masks()
        first = lane < SB_HEAD_DIM

        @pl.when(i == 0)
        def _():
            _att_fill(k_ref, v_ref, kcat, vcat, nkb, lane)
            dkacc[...] = jnp.zeros_like(dkacc)
            dvacc[...] = jnp.zeros_like(dvacc)

        q = _scaled_queries(q_ref)
        do = do_ref[...]
        dqacc[...] = jnp.zeros_like(dqacc)
        carry[...] = jnp.zeros_like(carry)

        def key_rows(m):
            return pl.ds(pl.multiple_of(m * 4 * tb, 4 * tb), 4 * tb)

        def front(m, bufs, causal=None):
            zb, rb, sb, ab = bufs
            z = _dot_nt(q, kcat[key_rows(m), :])
            zb[...] = z
            l = _log1m(z)
            sb[...] = jnp.exp(z + l)
            if causal is not None:
                l = jnp.where(causal, l, 0.0)
            for g in (0, 1):
                cols = slice(g * 2 * tb, (g + 1) * 2 * tb)
                rb[:, cols] = _split_dot(l[:, cols], suffix)
            ab[...] = _dot_nt(do, vcat[key_rows(m), :])

        def back(m, bufs, causal=None):
            zb, rb, sb, ab = bufs
            dzs, probs = [None, None], [None, None]
            for g in (0, 1):
                cols = slice(g * 2 * tb, (g + 1) * 2 * tb)
                j = 2 * m + g
                ct = ct_ref[0]
                ca = jnp.sum(jnp.where(lane == j, ct, 0.0), axis=1, keepdims=True)
                cb = jnp.sum(jnp.where(lane == nkb + j, ct, 0.0), axis=1, keepdims=True)
                a = jnp.exp(zb[:, cols] + rb[:, cols] + jnp.where(left, ca, cb))
                if causal is not None:
                    a = jnp.where(causal[:, cols], a, 0.0)
                de = ab[:, cols] * a
                cl = _split_dot(de, prefix)
                pre = carry[...]
                dz = de - sb[:, cols] * (cl + pre)
                if causal is not None:
                    dz = jnp.where(causal[:, cols], dz, 0.0)
                carry[...] = pre + jnp.where(left, cl[:, tb - 1:tb], cl[:, 2 * tb - 1:2 * tb])
                dzs[g] = dz.astype(BF16)
                probs[g] = a.astype(BF16)
            dzb = jnp.concatenate(dzs, axis=1)
            dqacc[...] += _dot(dzb, kcat[key_rows(m), :])
            dkc = _dot_tn(dzb, q)
            dvc = _dot_tn(jnp.concatenate(probs, axis=1), do)
            out_rows = pl.ds(pl.multiple_of(m * 2 * tb, 2 * tb), 2 * tb)
            pick = lambda x: jnp.concatenate([jnp.where(first, x[0:tb], x[tb:2 * tb]),
                                              jnp.where(first, x[2 * tb:3 * tb], x[3 * tb:4 * tb])], axis=0)
            dkacc[out_rows, :] += pick(dkc)
            dvacc[out_rows, :] += pick(dvc)

        b0, b1 = (z0, r0, s0, a0), (z1, r1, s1, a1)

        @pl.when(i > 0)
        def _():
            front(0, b0)

            def loop(u, c):
                front(2 * u + 1, b1)
                back(2 * u, b0)
                front(jnp.minimum(2 * u + 2, 2 * i - 1), b0)
                back(2 * u + 1, b1)
                return c

            lax.fori_loop(0, i, loop, 0)

        for e in (0, 1):
            causal = key + e * 2 * tb < tq
            front(2 * i + e, b0, causal)
            back(2 * i + e, b0, causal)

        dq_ref[...] = (dqacc[...] * scale).astype(BF16)

        @pl.when(i == nq - 1)
        def _():
            dk_ref[...] = dkacc[...].astype(BF16)
            dv_ref[...] = dvacc[...].astype(BF16)

    col = pl.BlockSpec((t, LANES), lambda p, i: (0, p))
    out = jax.ShapeDtypeStruct((t, SB_WIDTH), BF16)
    tile = pltpu.VMEM((ATT_Q, 4 * tb), F32)
    return pl.pallas_call(
        body, name="attn_bwd", grid=(n_pairs, nq),
        in_specs=[q_spec, k_spec, v_spec, pl.BlockSpec((ATT_Q, LANES), lambda p, i: (i, gm_blocks + p)),
                  pl.BlockSpec((1, ATT_Q, LANES), lambda p, i: (p, i, 0))],
        out_specs=[pl.BlockSpec((ATT_Q, LANES), lambda p, i: (i, p)), col, col],
        out_shape=[out, out, out],
        scratch_shapes=[pltpu.VMEM((2 * t, LANES), BF16), pltpu.VMEM((2 * t, LANES), BF16),
                        pltpu.VMEM((t, LANES), F32), pltpu.VMEM((t, LANES), F32),
                        pltpu.VMEM((ATT_Q, LANES), F32), pltpu.VMEM((ATT_Q, 2 * tb), F32)] + [tile] * 8,
        compiler_params=_params(2),
    )(qkv, qkv, qkv, dmixed, carries)


def _matmul_residual(res, a, w, name):
    t = a.shape[0]
    tm = _token_tile(t)

    def body(res_ref, a_ref, w_ref, o_ref):
        o_ref[...] = res_ref[...] + _dot(a_ref[...], w_ref[...])

    return pl.pallas_call(
        body, name=name, grid=(t // tm,),
        in_specs=[_rows(tm, res.shape[1]), _rows(tm, a.shape[1]), _resident(w.shape)],
        out_specs=_rows(tm, res.shape[1]),
        out_shape=jax.ShapeDtypeStruct(res.shape, F32),
        compiler_params=_params(),
    )(res, a, w)


def _matmul_nt_cast(dy, w, name):
    t = dy.shape[0]
    tm = _token_tile(t)

    def body(dy_ref, w_ref, o_ref, dyb_ref):
        dyb = dy_ref[...].astype(BF16)
        dyb_ref[...] = dyb
        o_ref[...] = _dot_nt(dyb, w_ref[...]).astype(BF16)

    return pl.pallas_call(
        body, name=name, grid=(t // tm,),
        in_specs=[_rows(tm, dy.shape[1]), _resident(w.shape)],
        out_specs=[_rows(tm, w.shape[0]), _rows(tm, dy.shape[1])],
        out_shape=[jax.ShapeDtypeStruct((t, w.shape[0]), BF16), jax.ShapeDtypeStruct(dy.shape, BF16)],
        compiler_params=_params(),
    )(dy, w)


def _norm_input_bwd(dres, dz, w, h, g, name):
    t = h.shape[0]
    tm = _token_tile(t)
    nb, _, width = w.shape

    def body(dres_ref, dz_ref, w_ref, h_ref, g_ref, dh_ref, n_ref, dg_ref):
        i = pl.program_id(0)
        hh = h_ref[...]
        gg = g_ref[...]
        r = _rstd(hh)
        n_ref[...] = (hh * r * gg).astype(BF16)
        dn = jnp.zeros((tm, D_MODEL), F32)
        for b in range(nb):
            dn = dn + _dot_nt(dz_ref[:, b * width:(b + 1) * width], w_ref[b])
        dh, dg_rows = _rms_bwd(dn, hh, r, gg)
        dh_ref[...] = dres_ref[...] + dh

        @pl.when(i == 0)
        def _():
            dg_ref[...] = jnp.zeros_like(dg_ref)

        dg_ref[...] += jnp.sum(dg_rows, axis=0, keepdims=True)

    return pl.pallas_call(
        body, name=name, grid=(t // tm,),
        in_specs=[_rows(tm, D_MODEL), _rows(tm, nb * width), _resident(w.shape), _rows(tm, D_MODEL),
                  _resident((1, D_MODEL))],
        out_specs=[_rows(tm, D_MODEL), _rows(tm, D_MODEL), pl.BlockSpec((1, D_MODEL), lambda i: (0, 0))],
        out_shape=[jax.ShapeDtypeStruct((t, D_MODEL), F32), jax.ShapeDtypeStruct((t, D_MODEL), BF16),
                   jax.ShapeDtypeStruct((1, D_MODEL), F32)],
        compiler_params=_params(),
    )(dres, dz, w, h, g)


def _head(h, p, target, gple, gfin, wg, wproj):
    t = h.shape[0]
    tm = _token_tile(t)
    pw = D_MODEL // N_CHIPS

    def body(h_ref, p_ref, tgt_ref, gple_ref, gfin_ref, wg_ref, wproj_ref,
             loss_ref, dgf_ref, dh_ref, dgp_ref, dpp_ref, n_ref, pb_ref):
        i = pl.program_id(0)
        hh = h_ref[...]
        n = (hh * _rstd(hh) * gple_ref[...]).astype(BF16)
        n_ref[...] = n
        gate = jax.nn.sigmoid(_dot(n, wg_ref[...]))
        pb = p_ref[...].astype(BF16)
        pb_ref[...] = pb
        pp = jnp.concatenate([_dot(pb, wproj_ref[b]) for b in range(N_CHIPS)], axis=1)
        h4 = hh + gate * pp
        r = _rstd(h4)
        gf = gfin_ref[...]
        err = h4 * r * gf - tgt_ref[...]
        dy = err * (1.0 / D_MODEL)
        dh4, dgf_rows = _rms_bwd(dy, h4, r, gf)
        dh_ref[...] = dh4
        dgp_ref[...] = (dh4 * pp * gate * (1.0 - gate)).astype(BF16)
        dpp_ref[...] = (dh4 * gate).astype(BF16)

        @pl.when(i == 0)
        def _():
            loss_ref[...] = jnp.zeros_like(loss_ref)
            dgf_ref[...] = jnp.zeros_like(dgf_ref)

        loss_ref[...] += (0.5 / D_MODEL) * jnp.sum(err * err)
        dgf_ref[...] += jnp.sum(dgf_rows, axis=0, keepdims=True)

    bf = lambda w: jax.ShapeDtypeStruct((t, w), BF16)
    const = lambda i: (0, 0)
    return pl.pallas_call(
        body, name="head", grid=(t // tm,),
        in_specs=[_rows(tm, D_MODEL), _rows(tm, PLE_DIM), _rows(tm, D_MODEL), _resident((1, D_MODEL)),
                  _resident((1, D_MODEL)), _resident(wg.shape), _resident(wproj.shape)],
        out_specs=[pl.BlockSpec((1, LANES), const), pl.BlockSpec((1, D_MODEL), const), _rows(tm, D_MODEL),
                   _rows(tm, D_MODEL), _rows(tm, D_MODEL), _rows(tm, D_MODEL), _rows(tm, PLE_DIM)],
        out_shape=[jax.ShapeDtypeStruct((1, LANES), F32), jax.ShapeDtypeStruct((1, D_MODEL), F32),
                   jax.ShapeDtypeStruct((t, D_MODEL), F32), bf(D_MODEL), bf(D_MODEL), bf(D_MODEL), bf(PLE_DIM)],
        compiler_params=_params(),
    )(h, p, target, gple, gfin, wg, wproj)


def _device_step(x, p, target, small, big):
    w1in, w1out = big["ffn1_w_in"], big["ffn1_w_out"].reshape(D_FF, D_MODEL)
    w2in, w2out = big["ffn2_w_in"], big["ffn2_w_out"].reshape(D_FF, D_MODEL)
    wmix = big["w_mix_in"]
    wmo = big["w_mix_out"].reshape(D_MODEL, D_MODEL)
    wg = big["ple_w_gate"].reshape(D_MODEL, D_MODEL)
    wproj = big["ple_w_proj"]
    bt = small["gmlp_b"].T

    h1, gu1 = _ffn_fwd(x, small["ffn1_norm"], w1in, w1out, "ffn1_fwd")
    zg, qkv = _mix_in_fwd(h1, small["mix_norm"], wmix)
    gm = _gmlp_fwd(zg, small["gmlp_v_norm"], small["gmlp_w_s"], bt)
    att, carries = _attn_fwd(qkv)
    mixed = jnp.concatenate([gm, att], axis=1)
    h2 = _matmul_residual(h1, mixed, wmo, "mix_out_fwd")
    h3, gu2 = _ffn_fwd(h2, small["ffn2_norm"], w2in, w2out, "ffn2_fwd")
    loss, d_final, dh4, dgp, dpp, n4, pb = _head(h3, p, target, small["ple_norm"], small["final_norm"], wg, wproj)

    g_big, g_small = {}, {"final_norm": d_final}
    g_big["ple_w_gate"] = _wgrad_rows(n4, dgp, N_CHIPS, "wgrad_ple_gate")
    g_big["ple_w_proj"] = _wgrad_cols(pb, dpp, N_CHIPS, "wgrad_ple_proj")
    dh3, _, g_small["ple_norm"] = _norm_input_bwd(dh4, dgp, wg.reshape(1, D_MODEL, D_MODEL), h3, small["ple_norm"],
                                                  "ple_bwd")

    dh2, dgu2, n3, act2, dhh3, g_small["ffn2_norm"] = _ffn_bwd(dh3, h2, small["ffn2_norm"], gu2, w2in, w2out, "ffn2_bwd")
    g_big["ffn2_w_in"] = _wgrad_cols(n3, dgu2, N_CHIPS, "wgrad_ffn2_in")
    g_big["ffn2_w_out"] = _wgrad_rows(act2, dhh3, 2, "wgrad_ffn2_out")

    dmixed, dh2b = _matmul_nt_cast(dh2, wmo, "mix_out_bwd")
    g_big["w_mix_out"] = _wgrad_rows(mixed, dh2b, 2, "wgrad_mix_out")
    dzg, g_small["gmlp_w_s"], dbt, g_small["gmlp_v_norm"] = _gmlp_bwd(zg, dmixed, small["gmlp_v_norm"],
                                                                      small["gmlp_w_s"], bt)
    g_small["gmlp_b"] = dbt.T
    dq, dk, dv = _attn_bwd(qkv, dmixed, carries)
    dzmix = jnp.concatenate([dzg, dq, dk, dv], axis=1)
    dh1, n2, g_small["mix_norm"] = _norm_input_bwd(dh2, dzmix, wmix, h1, small["mix_norm"], "mix_in_bwd")
    g_big["w_mix_in"] = _wgrad_cols(n2, dzmix, N_CHIPS, "wgrad_mix_in")

    dx, dgu1, n1, act1, dhh1, g_small["ffn1_norm"] = _ffn_bwd(dh1, x, small["ffn1_norm"], gu1, w1in, w1out, "ffn1_bwd")
    g_big["ffn1_w_in"] = _wgrad_cols(n1, dgu1, N_CHIPS, "wgrad_ffn1_in")
    g_big["ffn1_w_out"] = _wgrad_rows(act1, dhh1, 2, "wgrad_ffn1_out")
    return loss, dx, g_big, g_small


_BIG = ("ffn1_w_in", "ffn1_w_out", "w_mix_in", "w_mix_out", "ffn2_w_in", "ffn2_w_out", "ple_w_gate", "ple_w_proj")
_SMALL = ("ffn1_norm", "mix_norm", "gmlp_v_norm", "gmlp_w_s", "gmlp_b", "ffn2_norm", "ple_norm", "final_norm")
_ALL = ("ffn1_norm", "ffn1_w_in", "ffn1_w_out", "mix_norm", "w_mix_in", "gmlp_v_norm", "gmlp_w_s", "gmlp_b", "w_mix_out",
        "ffn2_norm", "ffn2_w_in", "ffn2_w_out", "ple_norm", "ple_w_gate", "ple_w_proj", "final_norm")
_ANY = pl.BlockSpec(memory_space=pl.ANY)
_MESH = pl.DeviceIdType.MESH


def _mesh_pos():
    return lax.axis_index("x"), lax.axis_index("y"), lax.axis_index("c")


def _other_chips(x, y):
    return [((x, 1 - y), 2 * x + 1 - y), ((1 - x, y), 2 * (1 - x) + y), ((1 - x, 1 - y), 2 * (1 - x) + 1 - y)]


def _remote(src, dst, send_sem, recv_sem, device):
    return pltpu.make_async_remote_copy(src_ref=src, dst_ref=dst, send_sem=send_sem, recv_sem=recv_sem,
                                        device_id=device, device_id_type=_MESH)


def _all_gather(shards):
    n = len(shards)
    per = 2 * (N_CHIPS - 1)

    def body(*refs):
        ins, outs = refs[:n], refs[n:2 * n]
        send_sems, recv_sems, local_sems = refs[2 * n:]
        x, y, c = _mesh_pos()
        sibling = (x, y, 1 - c)
        chips = _other_chips(x, y)
        mine = 2 * x + y

        def half(w, blk, cc):
            hr = shards[w].shape[0] // 2
            return outs[w].at[blk, pl.ds(cc * hr, hr), :]

        local = []
        for w in range(n):
            cp = pltpu.make_async_copy(ins[w], outs[w].at[mine], local_sems.at[w])
            cp.start()
            local.append(cp)
        first = []
        for w in range(n):
            hr = shards[w].shape[0] // 2
            src = ins[w].at[pl.ds(c * hr, hr), :]
            for k, (chip, _) in enumerate(chips):
                cp = _remote(src, half(w, mine, c), send_sems.at[per * w + k], recv_sems.at[per * w + k], (*chip, c))
                cp.start()
                first.append(cp)
        passed = []
        for w in range(n):
            for k, (_, blk) in enumerate(chips):
                landed = half(w, blk, c)
                _remote(landed, landed, send_sems.at[per * w + k], recv_sems.at[per * w + k], sibling).wait_recv()
                cp = _remote(landed, landed, send_sems.at[per * w + 3 + k], recv_sems.at[per * w + 3 + k], sibling)
                cp.start()
                passed.append(cp)
        for w in range(n):
            for k, (_, blk) in enumerate(chips):
                other = half(w, blk, 1 - c)
                _remote(other, other, send_sems.at[per * w + 3 + k], recv_sems.at[per * w + 3 + k], sibling).wait_recv()
        for cp in first + passed:
            cp.wait_send()
        for cp in local:
            cp.wait()

    return pl.pallas_call(
        body, name="all_gather_weights",
        in_specs=[_ANY] * n, out_specs=[_ANY] * n,
        out_shape=[jax.ShapeDtypeStruct((N_CHIPS, *s.shape), s.dtype) for s in shards],
        scratch_shapes=[pltpu.SemaphoreType.DMA((per * n,)), pltpu.SemaphoreType.DMA((per * n,)),
                        pltpu.SemaphoreType.DMA((n,))],
    )(*shards)


def _pair_exchange(grads):
    n = len(grads)

    def body(*refs):
        ins, outs = refs[:n], refs[n:2 * n]
        send_sems, recv_sems = refs[2 * n:]
        x, y, c = _mesh_pos()
        cps = []
        for w in range(n):
            hr = grads[w].shape[1] // 2
            cp = _remote(ins[w].at[:, pl.ds((1 - c) * hr, hr), :], outs[w], send_sems.at[w], recv_sems.at[w], (x, y, 1 - c))
            cp.start()
            cps.append(cp)
        for cp in cps:
            cp.wait()

    return pl.pallas_call(
        body, name="grad_pair_exchange",
        in_specs=[_ANY] * n, out_specs=[_ANY] * n,
        out_shape=[jax.ShapeDtypeStruct((g.shape[0], g.shape[1] // 2, g.shape[2]), g.dtype) for g in grads],
        scratch_shapes=[pltpu.SemaphoreType.DMA((n,)), pltpu.SemaphoreType.DMA((n,))],
    )(*grads)


def _pair_sum(g, a, pos, name):
    nb, r, c = g.shape
    hr = r // 2

    def body(pos_ref, g_ref, a_ref, o_ref):
        o_ref[...] = (g_ref[...] + a_ref[...]).astype(BF16)

    return pl.pallas_call(
        body, name=name,
        grid_spec=pltpu.PrefetchScalarGridSpec(
            num_scalar_prefetch=1, grid=(nb,),
            in_specs=[pl.BlockSpec((1, hr, c), lambda k, pos: (k ^ pos[0], pos[1], 0)),
                      pl.BlockSpec((1, hr, c), lambda k, pos: (k ^ pos[0], 0, 0))],
            out_specs=pl.BlockSpec((1, hr, c), lambda k, pos: (k, 0, 0))),
        out_shape=jax.ShapeDtypeStruct((nb, hr, c), BF16),
        compiler_params=_params(),
    )(pos, g, a)


def _chip_exchange(sums):
    n = len(sums)
    per = N_CHIPS - 1

    def body(*refs):
        ins, outs = refs[:n], refs[n:2 * n]
        send_sems, recv_sems = refs[2 * n:]
        x, y, c = _mesh_pos()
        cps = []
        for w in range(n):
            for k, (chip, _) in enumerate(_other_chips(x, y)):
                cp = _remote(ins[w].at[k + 1], outs[w].at[k], send_sems.at[per * w + k], recv_sems.at[per * w + k],
                             (*chip, c))
                cp.start()
                cps.append(cp)
        for cp in cps:
            cp.wait()

    return pl.pallas_call(
        body, name="grad_chip_exchange",
        in_specs=[_ANY] * n, out_specs=[_ANY] * n,
        out_shape=[jax.ShapeDtypeStruct((per, *s.shape[1:]), s.dtype) for s in sums],
        scratch_shapes=[pltpu.SemaphoreType.DMA((per * n,)), pltpu.SemaphoreType.DMA((per * n,))],
    )(*sums)


def _chip_sum(s, b, pos, name):
    _, hr, c = s.shape

    def body(pos_ref, s_ref, b_ref, o_ref):
        o_ref[...] = (s_ref[0].astype(F32) + b_ref[0].astype(F32)) + (b_ref[1].astype(F32) + b_ref[2].astype(F32))

    return pl.pallas_call(
        body, name=name,
        grid_spec=pltpu.PrefetchScalarGridSpec(
            num_scalar_prefetch=1, grid=(1,),
            in_specs=[pl.BlockSpec((1, hr, c), lambda k, pos: (0, 0, 0)), pl.BlockSpec((N_CHIPS - 1, hr, c), lambda k, pos: (0, 0, 0))],
            out_specs=pl.BlockSpec((hr, c), lambda k, pos: (pos[1], 0))),
        out_shape=jax.ShapeDtypeStruct((2 * hr, c), F32),
        compiler_params=_params(),
    )(pos, s, b)


def _pair_share(grads):
    n = len(grads)

    def body(*refs):
        outs = refs[n:2 * n]
        send_sems, recv_sems = refs[2 * n:]
        x, y, c = _mesh_pos()
        cps = []
        for w in range(n):
            hr = grads[w].shape[0] // 2
            rows = outs[w].at[pl.ds(c * hr, hr), :]
            cp = _remote(rows, rows, send_sems.at[w], recv_sems.at[w], (x, y, 1 - c))
            cp.start()
            cps.append(cp)
        for w, cp in enumerate(cps):
            cp.wait_send()
            hr = grads[w].shape[0] // 2
            other = outs[w].at[pl.ds((1 - c) * hr, hr), :]
            _remote(other, other, send_sems.at[w], recv_sems.at[w], (x, y, 1 - c)).wait_recv()

    return pl.pallas_call(
        body, name="grad_pair_share",
        in_specs=[_ANY] * n, out_specs=[_ANY] * n,
        out_shape=[jax.ShapeDtypeStruct(g.shape, g.dtype) for g in grads],
        input_output_aliases={w: w for w in range(n)},
        scratch_shapes=[pltpu.SemaphoreType.DMA((n,)), pltpu.SemaphoreType.DMA((n,))],
    )(*grads)


def _all_reduce_small(buf):
    n_dev = 8

    def body(in_ref, o_ref, slots, send_sems, recv_sems):
        x, y, c = _mesh_pos()
        me = 4 * x + 2 * y + c
        slots[0] = in_ref[...]
        cps = []
        for q in range(1, n_dev):
            peer = (x ^ (q >> 2), y ^ ((q >> 1) & 1), c ^ (q & 1))
            cp = _remote(in_ref, slots.at[q], send_sems.at[q - 1], recv_sems.at[q - 1], peer)
            cp.start()
            cps.append(cp)
        for cp in cps:
            cp.wait()
        acc = slots[me]
        for d in range(1, n_dev):
            acc = acc + slots[d ^ me]
        o_ref[...] = acc

    return pl.pallas_call(
        body, name="all_reduce_small",
        in_specs=[pl.BlockSpec(memory_space=pltpu.VMEM)], out_specs=pl.BlockSpec(memory_space=pltpu.VMEM),
        out_shape=jax.ShapeDtypeStruct(buf.shape, buf.dtype),
        scratch_shapes=[pltpu.VMEM((n_dev, *buf.shape), buf.dtype), pltpu.SemaphoreType.DMA((n_dev - 1,)),
                        pltpu.SemaphoreType.DMA((n_dev - 1,))],
    )(buf)


def _adamw(w, g, m, v, name):
    r, c = w.shape
    tr = r if r * c * 4 <= (1 << 20) else 64
    bias1 = 1.0 - ADAM_B1 ** ADAM_STEP
    bias2 = 1.0 - ADAM_B2 ** ADAM_STEP

    def body(w_ref, g_ref, m_ref, v_ref, d_ref, mo_ref, vo_ref):
        gg = g_ref[...]
        m2 = ADAM_B1 * m_ref[...] + (1.0 - ADAM_B1) * gg
        v2 = ADAM_B2 * v_ref[...] + (1.0 - ADAM_B2) * (gg * gg)
        mo_ref[...] = m2
        vo_ref[...] = v2
        d_ref[...] = -ADAM_LR * ((m2 / bias1) / (jnp.sqrt(v2 / bias2) + ADAM_EPS) + ADAM_WD * w_ref[...])

    spec = pl.BlockSpec((tr, c), lambda i: (i, 0))
    out = jax.ShapeDtypeStruct((r, c), F32)
    return pl.pallas_call(
        body, name=name, grid=(r // tr,), in_specs=[spec] * 4, out_specs=[spec] * 3, out_shape=[out] * 3,
        compiler_params=_params(),
    )(w, g, m, v)


def _pack(parts):
    flat = jnp.concatenate([parts[n].reshape(-1) for n in _SMALL])
    return flat.reshape(-1, LANES)


def _unpack(buf, like):
    flat = buf.reshape(-1)
    out, at = {}, 0
    for n in _SMALL:
        size = like[n].size
        out[n] = flat[at:at + size].reshape(like[n].shape)
        at += size
    return out


def kernel(x, p, ffn1_norm, ffn1_w_in, ffn1_w_out, mix_norm, w_mix_in, gmlp_v_norm, gmlp_w_s, gmlp_b, w_mix_out, ffn2_norm, ffn2_w_in, ffn2_w_out, ple_norm, ple_w_gate, ple_w_proj, final_norm, loss_target, m_ffn1_norm, m_ffn1_w_in, m_ffn1_w_out, m_mix_norm, m_w_mix_in, m_gmlp_v_norm, m_gmlp_w_s, m_gmlp_b, m_w_mix_out, m_ffn2_norm, m_ffn2_w_in, m_ffn2_w_out, m_ple_norm, m_ple_w_gate, m_ple_w_proj, m_final_norm, v_ffn1_norm, v_ffn1_w_in, v_ffn1_w_out, v_mix_norm, v_w_mix_in, v_gmlp_v_norm, v_gmlp_w_s, v_gmlp_b, v_w_mix_out, v_ffn2_norm, v_ffn2_w_in, v_ffn2_w_out, v_ple_norm, v_ple_w_gate, v_ple_w_proj, v_final_norm):
    args = dict(locals())
    w = {n: args[n] for n in _ALL}
    m = {n: args["m_" + n] for n in _ALL}
    v = {n: args["v_" + n] for n in _ALL}
    axes = ("x", "y", "c")
    xi, yi, ci = _mesh_pos()
    pos = jnp.stack([2 * xi + yi, ci]).astype(jnp.int32)

    shard = {n: w[n][0] for n in _BIG}
    gathered = _all_gather([shard[n].astype(BF16) for n in _BIG])
    small = {n: (w[n][0] if w[n].ndim > 2 else w[n].reshape(1, -1)) for n in _SMALL}
    loss_part, dx, g_big, g_small = _device_step(x[0], p[0, 0], loss_target[0], small, dict(zip(_BIG, gathered)))
    loss = lax.psum(loss_part[0, 0], axes)

    partial = [g_big[n].reshape(N_CHIPS, *shard[n].shape) for n in _BIG]
    from_sibling = _pair_exchange(partial)
    pair = [_pair_sum(g, a, pos, "pair_sum_" + n) for n, g, a in zip(_BIG, partial, from_sibling)]
    from_chips = _chip_exchange(pair)
    halves = [_chip_sum(s, b, pos, "chip_sum_" + n) for n, s, b in zip(_BIG, pair, from_chips)]
    grads = dict(zip(_BIG, _pair_share(halves)))

    delta, new_m, new_v = {}, {}, {}
    for n in _BIG:
        d2, m2, v2 = _adamw(shard[n], grads[n], m[n][0], v[n][0], "adamw_" + n)
        grads[n], delta[n], new_m[n], new_v[n] = grads[n][None], d2[None], m2[None], v2[None]

    g_packed = _all_reduce_small(_pack(g_small))
    d_p, m_p, v_p = _adamw(_pack(w), g_packed, _pack(m), _pack(v), "adamw_small")
    for out, buf in ((grads, g_packed), (delta, d_p), (new_m, m_p), (new_v, v_p)):
        out.update(_unpack(buf, w))

    return (loss, dx[None], *[grads[n] for n in _ALL], *[delta[n] for n in _ALL], *[new_m[n] for n in _ALL],
            *[new_v[n] for n in _ALL])
```

```python
import functools

import jax
import jax.numpy as jnp
from jax import lax
from jax.experimental import pallas as pl
from jax.experimental.pallas import tpu as pltpu

F32, BF16 = jnp.float32, jnp.bfloat16

D_MODEL = 1024
D_FF = 2816
FF_BLOCK = 2 * D_FF // 4
PLE_DIM = 256
CHUNK = 128
GM_HEADS = 4
GM_WIDTH = 512
SB_HEAD_DIM = 64
SB_WIDTH = 512
MIX_IN_WIDTH = 2 * GM_WIDTH + 3 * SB_WIDTH
MIX_BLOCK = MIX_IN_WIDTH // 4
EPS = 1e-6
N_CHIPS = 4
LANES = 128
ATT_BLOCK = 128
ATT_Q = 512
VMEM_LIMIT = 56 * 1024 * 1024

ADAM_LR, ADAM_B1, ADAM_B2, ADAM_EPS, ADAM_WD, ADAM_STEP = 0.001, 0.9, 0.999, 1e-08, 0.01, 10


def _dot(a, b):
    return jnp.dot(a, b, preferred_element_type=F32)


def _dot_nt(a, b):
    return lax.dot_general(a, b, (((1,), (1,)), ((), ())), preferred_element_type=F32)


def _dot_tn(a, b):
    return lax.dot_general(a, b, (((0,), (0,)), ((), ())), preferred_element_type=F32)


def _resident(shape):
    nd = len(shape)
    return pl.BlockSpec(shape, lambda *_: (0,) * nd, pipeline_mode=pl.Buffered(1))


def _rows(tm, width):
    return pl.BlockSpec((tm, width), lambda i: (i, 0))


def _params(n_axes=1):
    return pltpu.CompilerParams(dimension_semantics=("arbitrary",) * n_axes, vmem_limit_bytes=VMEM_LIMIT)


def _rstd(h):
    return lax.rsqrt(jnp.mean(h * h, axis=-1, keepdims=True) + EPS)


def _rms_bwd(dy, h, r, g):
    dyg = dy * g
    dh = r * dyg - h * (r * r * r) * jnp.mean(dyg * h, axis=-1, keepdims=True)
    return dh, dy * h * r


def _gelu(x):
    return 0.5 * x * (1.0 + lax.erf(x * (2.0 ** -0.5)))


def _gelu_grad(x):
    return 0.5 * (1.0 + lax.erf(x * (2.0 ** -0.5))) + x * jnp.exp(-0.5 * x * x) * ((2.0 * jnp.pi) ** -0.5)


def _token_tile(t):
    return min(256, t)


def _ffn_fwd(h, g, win, wout, name, exchange=None):
    t = h.shape[0]
    tm = _token_tile(t)

    def body(h_ref, g_ref, win_ref, wout_ref, ho_ref, gu_ref):
        hh = h_ref[...]
        n = (hh * _rstd(hh) * g_ref[...]).astype(BF16)
        acc = jnp.zeros((tm, D_MODEL), F32)
        for jb in range(2):
            gate = _dot(n, win_ref[jb])
            up = _dot(n, win_ref[2 + jb])
            gu_ref[:, jb * FF_BLOCK:(jb + 1) * FF_BLOCK] = gate.astype(BF16)
            gu_ref[:, D_FF + jb * FF_BLOCK:D_FF + (jb + 1) * FF_BLOCK] = up.astype(BF16)
            act = (gate * jax.nn.sigmoid(gate) * up).astype(BF16)
            acc = acc + _dot(act, wout_ref[jb * FF_BLOCK:(jb + 1) * FF_BLOCK, :])
        ho_ref[...] = hh + 0.5 * acc

    n = t // tm
    return _call(
        body, (h, g, win, wout), name=name, grid=(n,),
        in_specs=[_rows(tm, D_MODEL), _resident((1, D_MODEL)), _resident(win.shape), _resident(wout.shape)],
        out_specs=[_rows(tm, D_MODEL), _rows(tm, 2 * D_FF)],
        out_shape=[jax.ShapeDtypeStruct((t, D_MODEL), F32), jax.ShapeDtypeStruct((t, 2 * D_FF), BF16)],
        exchange=exchange, steps=(0, (2 * n) // 3, n - 1))


def _ffn_bwd(dho, h, g, gu, win, wout, name, exchange=None):
    t = h.shape[0]
    tm = _token_tile(t)

    def body(dho_ref, h_ref, g_ref, gu_ref, win_ref, wout_ref, dh_ref, dgu_ref, n_ref, act_ref, dhh_ref, dg_ref):
        i = pl.program_id(0)
        hh = h_ref[...]
        gg = g_ref[...]
        r = _rstd(hh)
        n_ref[...] = (hh * r * gg).astype(BF16)
        dho = dho_ref[...]
        dhh = (0.5 * dho).astype(BF16)
        dhh_ref[...] = dhh
        dn = jnp.zeros((tm, D_MODEL), F32)
        for jb in range(2):
            cg = slice(jb * FF_BLOCK, (jb + 1) * FF_BLOCK)
            cu = slice(D_FF + jb * FF_BLOCK, D_FF + (jb + 1) * FF_BLOCK)
            dact = _dot_nt(dhh, wout_ref[cg, :])
            gate = gu_ref[:, cg].astype(F32)
            up = gu_ref[:, cu].astype(F32)
            sg = jax.nn.sigmoid(gate)
            silu = gate * sg
            act_ref[:, cg] = (silu * up).astype(BF16)
            dgate = (dact * up * (sg * (1.0 + gate * (1.0 - sg)))).astype(BF16)
            dup = (dact * silu).astype(BF16)
            dgu_ref[:, cg] = dgate
            dgu_ref[:, cu] = dup
            dn = dn + _dot_nt(dgate, win_ref[jb]) + _dot_nt(dup, win_ref[2 + jb])
        dh, dg_rows = _rms_bwd(dn, hh, r, gg)
        dh_ref[...] = dho + dh

        @pl.when(i == 0)
        def _():
            dg_ref[...] = jnp.zeros_like(dg_ref)

        dg_ref[...] += jnp.sum(dg_rows, axis=0, keepdims=True)

    n = t // tm
    return _call(
        body, (dho, h, g, gu, win, wout), name=name, grid=(n,),
        in_specs=[_rows(tm, D_MODEL), _rows(tm, D_MODEL), _resident((1, D_MODEL)), _rows(tm, 2 * D_FF),
                  _resident(win.shape), _resident(wout.shape)],
        out_specs=[_rows(tm, D_MODEL), _rows(tm, 2 * D_FF), _rows(tm, D_MODEL), _rows(tm, D_FF), _rows(tm, D_MODEL),
                   pl.BlockSpec((1, D_MODEL), lambda i: (0, 0))],
        out_shape=[jax.ShapeDtypeStruct((t, D_MODEL), F32), jax.ShapeDtypeStruct((t, 2 * D_FF), BF16),
                   jax.ShapeDtypeStruct((t, D_MODEL), BF16), jax.ShapeDtypeStruct((t, D_FF), BF16),
                   jax.ShapeDtypeStruct((t, D_MODEL), BF16), jax.ShapeDtypeStruct((1, D_MODEL), F32)],
        exchange=exchange, steps=(0, n - 1))


def _wgrad(a, b, out_shape, out_block, out_index, a_width, b_width, grid_ij, name):
    t = a.shape[0]
    tk = min(512, t)

    def body(a_ref, b_ref, o_ref):
        k = pl.program_id(2)
        prod = _dot_tn(a_ref[...], b_ref[...]).reshape(o_ref.shape)

        @pl.when(k == 0)
        def _():
            o_ref[...] = prod

        @pl.when(k > 0)
        def _():
            o_ref[...] += prod

    return pl.pallas_call(
        body, name=name, grid=(*grid_ij, t // tk),
        in_specs=[pl.BlockSpec((tk, a_width), lambda i, j, k: (k, i)), pl.BlockSpec((tk, b_width), lambda i, j, k: (k, j))],
        out_specs=pl.BlockSpec(out_block, lambda i, j, k: out_index(i, j)),
        out_shape=jax.ShapeDtypeStruct(out_shape, F32),
        compiler_params=_params(3),
    )(a, b)


def _wgrad_cols(a, b, n_blocks, name):
    ka, nb = a.shape[1], b.shape[1] // n_blocks
    return _wgrad(a, b, (n_blocks, ka, nb), (1, ka, nb), lambda i, j: (j, 0, 0), ka, nb, (1, n_blocks), name)


def _wgrad_rows(a, b, n_blocks, name):
    ka, nb = a.shape[1] // n_blocks, b.shape[1]
    return _wgrad(a, b, (a.shape[1], nb), (ka, nb), lambda i, j: (i, 0), ka, nb, (n_blocks, 1), name)


def _mix_in_fwd(h, g, wmix):
    t = h.shape[0]
    tm = _token_tile(t)
    gw2 = 2 * GM_WIDTH

    def body(h_ref, g_ref, w_ref, zg_ref, qkv_ref):
        hh = h_ref[...]
        n = (hh * _rstd(hh) * g_ref[...]).astype(BF16)
        for b in range(N_CHIPS):
            z = _dot(n, w_ref[b])
            lo, hi = b * MIX_BLOCK, (b + 1) * MIX_BLOCK
            if hi <= gw2:
                zg_ref[:, lo:hi] = z
            elif lo >= gw2:
                qkv_ref[:, lo - gw2:hi - gw2] = z.astype(BF16)
            else:
                zg_ref[:, lo:gw2] = z[:, :gw2 - lo]
                qkv_ref[:, 0:hi - gw2] = z[:, gw2 - lo:].astype(BF16)

    return pl.pallas_call(
        body, name="mix_in_fwd", grid=(t // tm,),
        in_specs=[_rows(tm, D_MODEL), _resident((1, D_MODEL)), _resident(wmix.shape)],
        out_specs=[_rows(tm, gw2), _rows(tm, 3 * SB_WIDTH)],
        out_shape=[jax.ShapeDtypeStruct((t, gw2), F32), jax.ShapeDtypeStruct((t, 3 * SB_WIDTH), BF16)],
        compiler_params=_params(),
    )(h, g, wmix)


def _causal_chunk_mask():
    row = lax.broadcasted_iota(jnp.int32, (CHUNK, CHUNK), 0)
    col = lax.broadcasted_iota(jnp.int32, (CHUNK, CHUNK), 1)
    return row >= col


def _gmlp_tile(t):
    return min(512, t)


def _gmlp_fwd(zg, gv, ws, bt):
    t = zg.shape[0]
    tm = _gmlp_tile(t)

    def body(zg_ref, gv_ref, ws_ref, bt_ref, o_ref):
        u = _gelu(zg_ref[:, :GM_WIDTH])
        v = _gelu(zg_ref[:, GM_WIDTH:])
        vn = (v * _rstd(v) * gv_ref[...]).astype(BF16)
        mask = _causal_chunk_mask()
        for hd in range(GM_HEADS):
            wm = jnp.where(mask, ws_ref[hd], 0.0).astype(BF16)
            cols = slice(hd * CHUNK, (hd + 1) * CHUNK)
            for c in range(tm // CHUNK):
                rows = slice(c * CHUNK, (c + 1) * CHUNK)
                sv = _dot(wm, vn[rows, cols]) + bt_ref[:, hd:hd + 1]
                o_ref[rows, cols] = (u[rows, cols] * sv).astype(BF16)

    return pl.pallas_call(
        body, name="gmlp_fwd", grid=(t // tm,),
        in_specs=[_rows(tm, 2 * GM_WIDTH), _resident((1, GM_WIDTH)), _resident(ws.shape), _resident(bt.shape)],
        out_specs=_rows(tm, GM_WIDTH),
        out_shape=jax.ShapeDtypeStruct((t, GM_WIDTH), BF16),
        compiler_params=_params(),
    )(zg, gv, ws, bt)


def _gmlp_bwd(zg, dmixed, gv, ws, bt):
    t = zg.shape[0]
    tm = _gmlp_tile(t)

    def body(zg_ref, dgm_ref, gv_ref, ws_ref, bt_ref, dzg_ref, dws_ref, dbt_ref, dgv_ref):
        i = pl.program_id(0)

        @pl.when(i == 0)
        def _():
            dws_ref[...] = jnp.zeros_like(dws_ref)
            dbt_ref[...] = jnp.zeros_like(dbt_ref)
            dgv_ref[...] = jnp.zeros_like(dgv_ref)

        zu = zg_ref[:, :GM_WIDTH]
        zv = zg_ref[:, GM_WIDTH:]
        u = _gelu(zu)
        v = _gelu(zv)
        r = _rstd(v)
        gvv = gv_ref[...]
        vn = (v * r * gvv).astype(BF16)
        dgm = dgm_ref[...].astype(F32)
        dsv = (dgm * u).astype(BF16)
        mask = _causal_chunk_mask()
        du_cols, dvn_cols = [], []
        for hd in range(GM_HEADS):
            wm = jnp.where(mask, ws_ref[hd], 0.0).astype(BF16)
            cols = slice(hd * CHUNK, (hd + 1) * CHUNK)
            dw = jnp.zeros((CHUNK, CHUNK), F32)
            db = jnp.zeros((CHUNK, 1), F32)
            du_rows, dvn_rows = [], []
            for c in range(tm // CHUNK):
                rows = slice(c * CHUNK, (c + 1) * CHUNK)
                sv = _dot(wm, vn[rows, cols]) + bt_ref[:, hd:hd + 1]
                du_rows.append(dgm[rows, cols] * sv)
                dvn_rows.append(_dot_tn(wm, dsv[rows, cols]))
                dw = dw + _dot_nt(dsv[rows, cols], vn[rows, cols])
                db = db + jnp.sum(dsv[rows, cols].astype(F32), axis=1, keepdims=True)
            dws_ref[hd] += jnp.where(mask, dw, 0.0)
            dbt_ref[:, hd:hd + 1] += db
            du_cols.append(jnp.concatenate(du_rows, axis=0))
            dvn_cols.append(jnp.concatenate(dvn_rows, axis=0))
        du = jnp.concatenate(du_cols, axis=1)
        dvn = jnp.concatenate(dvn_cols, axis=1)
        dv, dgv_rows = _rms_bwd(dvn, v, r, gvv)
        dgv_ref[...] += jnp.sum(dgv_rows, axis=0, keepdims=True)
        dzg_ref[:, :GM_WIDTH] = (du * _gelu_grad(zu)).astype(BF16)
        dzg_ref[:, GM_WIDTH:] = (dv * _gelu_grad(zv)).astype(BF16)

    const = lambda nd: (lambda i: (0,) * nd)
    return pl.pallas_call(
        body, name="gmlp_bwd", grid=(t // tm,),
        in_specs=[_rows(tm, 2 * GM_WIDTH), _rows(tm, GM_WIDTH), _resident((1, GM_WIDTH)), _resident(ws.shape),
                  _resident(bt.shape)],
        out_specs=[_rows(tm, 2 * GM_WIDTH), pl.BlockSpec(ws.shape, const(3)), pl.BlockSpec(bt.shape, const(2)),
                   pl.BlockSpec((1, GM_WIDTH), const(2))],
        out_shape=[jax.ShapeDtypeStruct((t, 2 * GM_WIDTH), BF16), jax.ShapeDtypeStruct(ws.shape, F32),
                   jax.ShapeDtypeStruct(bt.shape, F32), jax.ShapeDtypeStruct((1, GM_WIDTH), F32)],
        compiler_params=_params(),
    )(zg, dmixed, gv, ws, bt)


def _att_masks():
    tb = ATT_BLOCK
    lane = lax.broadcasted_iota(jnp.int32, (1, LANES), 1)
    rj = lax.broadcasted_iota(jnp.int32, (2 * tb, 2 * tb), 0)
    cs = lax.broadcasted_iota(jnp.int32, (2 * tb, 2 * tb), 1)
    same_head = ((rj < tb) & (cs < tb)) | ((rj >= tb) & (cs >= tb))
    suffix = jnp.where(same_head & (rj >= cs), 1.0, 0.0).astype(BF16)
    prefix = jnp.where(same_head & (rj <= cs), 1.0, 0.0).astype(BF16)
    suffix, prefix = jnp.concatenate([suffix, suffix], axis=0), jnp.concatenate([prefix, prefix], axis=0)
    left = lax.broadcasted_iota(jnp.int32, (1, 2 * tb), 1) < tb
    tq = lax.broadcasted_iota(jnp.int32, (ATT_Q, 4 * tb), 0)
    ts = lax.broadcasted_iota(jnp.int32, (ATT_Q, 4 * tb), 1)
    key = jnp.where(ts < 2 * tb, ts & (tb - 1), (ts & (tb - 1)) + tb)
    return lane, suffix, prefix, left, key, tq


def _att_fill(k_ref, v_ref, kcat, vcat, n_blocks, lane):
    tb = ATT_BLOCK
    first = lane < SB_HEAD_DIM

    def fill(jb, carry):
        rows = pl.ds(pl.multiple_of(jb * tb, tb), tb)
        top = pl.ds(pl.multiple_of(jb * 2 * tb, tb), tb)
        bot = pl.ds(pl.multiple_of(jb * 2 * tb + tb, tb), tb)
        kb = k_ref[rows, :]
        vb = v_ref[rows, :]
        zero = jnp.zeros_like(kb)
        kcat[top, :] = jnp.where(first, kb, zero)
        kcat[bot, :] = jnp.where(first, zero, kb)
        vcat[top, :] = jnp.where(first, vb, zero)
        vcat[bot, :] = jnp.where(first, zero, vb)
        return carry

    lax.fori_loop(0, n_blocks, fill, 0)


def _split_dot(x, m2):
    hi = x.astype(BF16)
    lo = (x - hi.astype(F32)).astype(BF16)
    return _dot(jnp.concatenate([hi, lo], axis=1), m2)


def _log1m(z):
    return jnp.minimum(-z, 0.0) - jnp.log(1.0 + jnp.exp(-jnp.abs(z)))


def _scaled_queries(q_ref):
    return (q_ref[...].astype(F32) * (SB_HEAD_DIM ** -0.5)).astype(BF16)


def _att_specs(t):
    n_pairs = SB_WIDTH // LANES
    q_spec = pl.BlockSpec((ATT_Q, LANES), lambda p, i: (i, p))
    k_spec = pl.BlockSpec((t, LANES), lambda p, i: (0, n_pairs + p))
    v_spec = pl.BlockSpec((t, LANES), lambda p, i: (0, 2 * n_pairs + p))
    return n_pairs, q_spec, k_spec, v_spec


def _attn_fwd(qkv, exchange=None):
    t = qkv.shape[0]
    tb = ATT_BLOCK
    nkb = t // tb
    assert 2 * nkb <= LANES and t % ATT_Q == 0 and ATT_Q == 4 * tb
    n_pairs, q_spec, k_spec, v_spec = _att_specs(t)

    def body(q_ref, k_ref, v_ref, o_ref, ct_ref, kcat, vcat, acc, carry, z0, r0, z1, r1):
        i = pl.program_id(1)
        lane, suffix, _, left, key, tq = _att_masks()

        @pl.when(i == 0)
        def _():
            _att_fill(k_ref, v_ref, kcat, vcat, nkb, lane)

        q = _scaled_queries(q_ref)
        acc[...] = jnp.zeros_like(acc)
        carry[...] = jnp.zeros_like(carry)
        ct_ref[0] = jnp.zeros((ATT_Q, LANES), F32)

        def key_rows(m):
            return pl.ds(pl.multiple_of(m * 4 * tb, 4 * tb), 4 * tb)

        def scores(m, zb, rb, causal=None):
            z = _dot_nt(q, kcat[key_rows(m), :])
            zb[...] = z
            l = _log1m(z)
            if causal is not None:
                l = jnp.where(causal, l, 0.0)
            for g in (1, 0):
                cols = slice(g * 2 * tb, (g + 1) * 2 * tb)
                rb[:, cols] = _split_dot(l[:, cols], suffix)

        def weigh(m, zb, rb, causal=None):
            probs = [None, None]
            for g in (1, 0):
                cols = slice(g * 2 * tb, (g + 1) * 2 * tb)
                j = 2 * m + g
                r = rb[:, cols]
                c = carry[...]
                ct_ref[0] = jnp.where(lane == j, c[:, :tb], jnp.where(lane == nkb + j, c[:, tb:], ct_ref[0]))
                a = jnp.exp(zb[:, cols] + r + c)
                if causal is not None:
                    a = jnp.where(causal[:, cols], a, 0.0)
                probs[g] = a.astype(BF16)
                carry[...] = c + jnp.where(left, r[:, 0:1], r[:, tb:tb + 1])
            acc[...] += _dot(jnp.concatenate(probs, axis=1), vcat[key_rows(m), :])

        for e in (1, 0):
            causal = key + e * 2 * tb < tq
            scores(2 * i + e, z0, r0, causal)
            weigh(2 * i + e, z0, r0, causal)

        @pl.when(i > 0)
        def _():
            scores(2 * i - 1, z1, r1)

            def loop(k, c):
                u = i - 1 - k
                scores(2 * u, z0, r0)
                weigh(2 * u + 1, z1, r1)
                scores(jnp.maximum(2 * u - 1, 0), z1, r1)
                weigh(2 * u, z0, r0)
                return c

            lax.fori_loop(0, i, loop, 0)

        o_ref[...] = acc[...].astype(BF16)

    tile = pltpu.VMEM((ATT_Q, 4 * tb), F32)

    nq = t // ATT_Q
    return _call(
        body, (qkv, qkv, qkv), name="attn_fwd", grid=(n_pairs, nq),
        in_specs=[q_spec, k_spec, v_spec],
        out_specs=[pl.BlockSpec((ATT_Q, LANES), lambda p, i: (i, p)), pl.BlockSpec((1, ATT_Q, LANES), lambda p, i: (p, i, 0))],
        out_shape=[jax.ShapeDtypeStruct((t, SB_WIDTH), BF16), jax.ShapeDtypeStruct((n_pairs, t, LANES), F32)],
        scratch_shapes=[pltpu.VMEM((2 * t, LANES), BF16), pltpu.VMEM((2 * t, LANES), BF16),
                        pltpu.VMEM((ATT_Q, LANES), F32), pltpu.VMEM((ATT_Q, 2 * tb), F32), tile, tile, tile, tile],
        exchange=exchange, steps=(0, (n_pairs - 1) * nq - 1, n_pairs * nq - 1))


def _attn_bwd(qkv, dmixed, carries, exchange=None):
    t = qkv.shape[0]
    tb = ATT_BLOCK
    nkb = t // tb
    nq = t // ATT_Q
    scale = SB_HEAD_DIM ** -0.5
    n_pairs, q_spec, k_spec, v_spec = _att_specs(t)
    gm_blocks = GM_WIDTH // LANES

    def body(q_ref, k_ref, v_ref, do_ref, ct_ref, dq_ref, dk_ref, dv_ref, kcat, vcat, dkacc, dvacc, dqacc, carry,
             z0, r0, s0, a0, z1, r1, s1, a1):
        i = pl.program_id(1)
        lane, suffix, prefix, left, key, tq = _att_masks()
        first = lane < SB_HEAD_DIM

        @pl.when(i == 0)
        def _():
            _att_fill(k_ref, v_ref, kcat, vcat, nkb, lane)
            dkacc[...] = jnp.zeros_like(dkacc)
            dvacc[...] = jnp.zeros_like(dvacc)

        q = _scaled_queries(q_ref)
        do = do_ref[...]
        dqacc[...] = jnp.zeros_like(dqacc)
        carry[...] = jnp.zeros_like(carry)

        def key_rows(m):
            return pl.ds(pl.multiple_of(m * 4 * tb, 4 * tb), 4 * tb)

        def front(m, bufs, causal=None):
            zb, rb, sb, ab = bufs
            z = _dot_nt(q, kcat[key_rows(m), :])
            zb[...] = z
            l = _log1m(z)
            sb[...] = jnp.exp(z + l)
            if causal is not None:
                l = jnp.where(causal, l, 0.0)
            for g in (0, 1):
                cols = slice(g * 2 * tb, (g + 1) * 2 * tb)
                rb[:, cols] = _split_dot(l[:, cols], suffix)
            ab[...] = _dot_nt(do, vcat[key_rows(m), :])

        def back(m, bufs, causal=None):
            zb, rb, sb, ab = bufs
            dzs, probs = [None, None], [None, None]
            for g in (0, 1):
                cols = slice(g * 2 * tb, (g + 1) * 2 * tb)
                j = 2 * m + g
                ct = ct_ref[0]
                ca = jnp.sum(jnp.where(lane == j, ct, 0.0), axis=1, keepdims=True)
                cb = jnp.sum(jnp.where(lane == nkb + j, ct, 0.0), axis=1, keepdims=True)
                a = jnp.exp(zb[:, cols] + rb[:, cols] + jnp.where(left, ca, cb))
                if causal is not None:
                    a = jnp.where(causal[:, cols], a, 0.0)
                de = ab[:, cols] * a
                cl = _split_dot(de, prefix)
                pre = carry[...]
                dz = de - sb[:, cols] * (cl + pre)
                if causal is not None:
                    dz = jnp.where(causal[:, cols], dz, 0.0)
                carry[...] = pre + jnp.where(left, cl[:, tb - 1:tb], cl[:, 2 * tb - 1:2 * tb])
                dzs[g] = dz.astype(BF16)
                probs[g] = a.astype(BF16)
            dzb = jnp.concatenate(dzs, axis=1)
            dqacc[...] += _dot(dzb, kcat[key_rows(m), :])
            dkc = _dot_tn(dzb, q)
            dvc = _dot_tn(jnp.concatenate(probs, axis=1), do)
            out_rows = pl.ds(pl.multiple_of(m * 2 * tb, 2 * tb), 2 * tb)
            pick = lambda x: jnp.concatenate([jnp.where(first, x[0:tb], x[tb:2 * tb]),
                                              jnp.where(first, x[2 * tb:3 * tb], x[3 * tb:4 * tb])], axis=0)
            dkacc[out_rows, :] += pick(dkc)
            dvacc[out_rows, :] += pick(dvc)

        b0, b1 = (z0, r0, s0, a0), (z1, r1, s1, a1)

        @pl.when(i > 0)
        def _():
            front(0, b0)

            def loop(u, c):
                front(2 * u + 1, b1)
                back(2 * u, b0)
                front(jnp.minimum(2 * u + 2, 2 * i - 1), b0)
                back(2 * u + 1, b1)
                return c

            lax.fori_loop(0, i, loop, 0)

        for e in (0, 1):
            causal = key + e * 2 * tb < tq
            front(2 * i + e, b0, causal)
            back(2 * i + e, b0, causal)

        dq_ref[...] = (dqacc[...] * scale).astype(BF16)

        @pl.when(i == nq - 1)
        def _():
            dk_ref[...] = dkacc[...].astype(BF16)
            dv_ref[...] = dvacc[...].astype(BF16)

    col = pl.BlockSpec((t, LANES), lambda p, i: (0, p))
    out = jax.ShapeDtypeStruct((t, SB_WIDTH), BF16)
    tile = pltpu.VMEM((ATT_Q, 4 * tb), F32)
    return _call(
        body, (qkv, qkv, qkv, dmixed, carries), name="attn_bwd", grid=(n_pairs, nq),
        in_specs=[q_spec, k_spec, v_spec, pl.BlockSpec((ATT_Q, LANES), lambda p, i: (i, gm_blocks + p)),
                  pl.BlockSpec((1, ATT_Q, LANES), lambda p, i: (p, i, 0))],
        out_specs=[pl.BlockSpec((ATT_Q, LANES), lambda p, i: (i, p)), col, col],
        out_shape=[out, out, out],
        scratch_shapes=[pltpu.VMEM((2 * t, LANES), BF16), pltpu.VMEM((2 * t, LANES), BF16),
                        pltpu.VMEM((t, LANES), F32), pltpu.VMEM((t, LANES), F32),
                        pltpu.VMEM((ATT_Q, LANES), F32), pltpu.VMEM((ATT_Q, 2 * tb), F32)] + [tile] * 8,
        exchange=exchange, steps=(0, n_pairs * nq - 1))


def _matmul_residual(res, a, w, name):
    t = a.shape[0]
    tm = _token_tile(t)

    def body(res_ref, a_ref, w_ref, o_ref):
        o_ref[...] = res_ref[...] + _dot(a_ref[...], w_ref[...])

    return pl.pallas_call(
        body, name=name, grid=(t // tm,),
        in_specs=[_rows(tm, res.shape[1]), _rows(tm, a.shape[1]), _resident(w.shape)],
        out_specs=_rows(tm, res.shape[1]),
        out_shape=jax.ShapeDtypeStruct(res.shape, F32),
        compiler_params=_params(),
    )(res, a, w)


def _matmul_nt_cast(dy, w, name):
    t = dy.shape[0]
    tm = _token_tile(t)

    def body(dy_ref, w_ref, o_ref, dyb_ref):
        dyb = dy_ref[...].astype(BF16)
        dyb_ref[...] = dyb
        o_ref[...] = _dot_nt(dyb, w_ref[...]).astype(BF16)

    return pl.pallas_call(
        body, name=name, grid=(t // tm,),
        in_specs=[_rows(tm, dy.shape[1]), _resident(w.shape)],
        out_specs=[_rows(tm, w.shape[0]), _rows(tm, dy.shape[1])],
        out_shape=[jax.ShapeDtypeStruct((t, w.shape[0]), BF16), jax.ShapeDtypeStruct(dy.shape, BF16)],
        compiler_params=_params(),
    )(dy, w)


def _norm_input_bwd(dres, dz, w, h, g, name):
    t = h.shape[0]
    tm = _token_tile(t)
    nb, _, width = w.shape

    def body(dres_ref, dz_ref, w_ref, h_ref, g_ref, dh_ref, n_ref, dg_ref):
        i = pl.program_id(0)
        hh = h_ref[...]
        gg = g_ref[...]
        r = _rstd(hh)
        n_ref[...] = (hh * r * gg).astype(BF16)
        dn = jnp.zeros((tm, D_MODEL), F32)
        for b in range(nb):
            dn = dn + _dot_nt(dz_ref[:, b * width:(b + 1) * width], w_ref[b])
        dh, dg_rows = _rms_bwd(dn, hh, r, gg)
        dh_ref[...] = dres_ref[...] + dh

        @pl.when(i == 0)
        def _():
            dg_ref[...] = jnp.zeros_like(dg_ref)

        dg_ref[...] += jnp.sum(dg_rows, axis=0, keepdims=True)

    return pl.pallas_call(
        body, name=name, grid=(t // tm,),
        in_specs=[_rows(tm, D_MODEL), _rows(tm, nb * width), _resident(w.shape), _rows(tm, D_MODEL),
                  _resident((1, D_MODEL))],
        out_specs=[_rows(tm, D_MODEL), _rows(tm, D_MODEL), pl.BlockSpec((1, D_MODEL), lambda i: (0, 0))],
        out_shape=[jax.ShapeDtypeStruct((t, D_MODEL), F32), jax.ShapeDtypeStruct((t, D_MODEL), BF16),
                   jax.ShapeDtypeStruct((1, D_MODEL), F32)],
        compiler_params=_params(),
    )(dres, dz, w, h, g)


def _head(h, p, target, gple, gfin, wg, wproj):
    t = h.shape[0]
    tm = _token_tile(t)
    pw = D_MODEL // N_CHIPS

    def body(h_ref, p_ref, tgt_ref, gple_ref, gfin_ref, wg_ref, wproj_ref,
             loss_ref, dgf_ref, dh_ref, dgp_ref, dpp_ref, n_ref, pb_ref):
        i = pl.program_id(0)
        hh = h_ref[...]
        n = (hh * _rstd(hh) * gple_ref[...]).astype(BF16)
        n_ref[...] = n
        gate = jax.nn.sigmoid(_dot(n, wg_ref[...]))
        pb = p_ref[...].astype(BF16)
        pb_ref[...] = pb
        pp = jnp.concatenate([_dot(pb, wproj_ref[b]) for b in range(N_CHIPS)], axis=1)
        h4 = hh + gate * pp
        r = _rstd(h4)
        gf = gfin_ref[...]
        err = h4 * r * gf - tgt_ref[...]
        dy = err * (1.0 / D_MODEL)
        dh4, dgf_rows = _rms_bwd(dy, h4, r, gf)
        dh_ref[...] = dh4
        dgp_ref[...] = (dh4 * pp * gate * (1.0 - gate)).astype(BF16)
        dpp_ref[...] = (dh4 * gate).astype(BF16)

        @pl.when(i == 0)
        def _():
            loss_ref[...] = jnp.zeros_like(loss_ref)
            dgf_ref[...] = jnp.zeros_like(dgf_ref)

        loss_ref[...] += (0.5 / D_MODEL) * jnp.sum(err * err)
        dgf_ref[...] += jnp.sum(dgf_rows, axis=0, keepdims=True)

    bf = lambda w: jax.ShapeDtypeStruct((t, w), BF16)
    const = lambda i: (0, 0)
    return pl.pallas_call(
        body, name="head", grid=(t // tm,),
        in_specs=[_rows(tm, D_MODEL), _rows(tm, PLE_DIM), _rows(tm, D_MODEL), _resident((1, D_MODEL)),
                  _resident((1, D_MODEL)), _resident(wg.shape), _resident(wproj.shape)],
        out_specs=[pl.BlockSpec((1, LANES), const), pl.BlockSpec((1, D_MODEL), const), _rows(tm, D_MODEL),
                   _rows(tm, D_MODEL), _rows(tm, D_MODEL), _rows(tm, D_MODEL), _rows(tm, PLE_DIM)],
        out_shape=[jax.ShapeDtypeStruct((1, LANES), F32), jax.ShapeDtypeStruct((1, D_MODEL), F32),
                   jax.ShapeDtypeStruct((t, D_MODEL), F32), bf(D_MODEL), bf(D_MODEL), bf(D_MODEL), bf(PLE_DIM)],
        compiler_params=_params(),
    )(h, p, target, gple, gfin, wg, wproj)


_BIG = ("ffn1_w_in", "ffn1_w_out", "w_mix_in", "w_mix_out", "ffn2_w_in", "ffn2_w_out", "ple_w_gate", "ple_w_proj")
_SMALL = ("ffn1_norm", "mix_norm", "gmlp_v_norm", "gmlp_w_s", "gmlp_b", "ffn2_norm", "ple_norm", "final_norm")
_ALL = ("ffn1_norm", "ffn1_w_in", "ffn1_w_out", "mix_norm", "w_mix_in", "gmlp_v_norm", "gmlp_w_s", "gmlp_b", "w_mix_out",
        "ffn2_norm", "ffn2_w_in", "ffn2_w_out", "ple_norm", "ple_w_gate", "ple_w_proj", "final_norm")
_ANY = pl.BlockSpec(memory_space=pl.ANY)
_MESH = pl.DeviceIdType.MESH


def _mesh_pos():
    return lax.axis_index("x"), lax.axis_index("y"), lax.axis_index("c")


def _other_chips(x, y):
    return [((x, 1 - y), 2 * x + 1 - y), ((1 - x, y), 2 * (1 - x) + y), ((1 - x, 1 - y), 2 * (1 - x) + 1 - y)]


def _remote(src, dst, send_sem, recv_sem, device):
    return pltpu.make_async_remote_copy(src_ref=src, dst_ref=dst, send_sem=send_sem, recv_sem=recv_sem,
                                        device_id=device, device_id_type=_MESH)


class _WeightGather:
    def __init__(self, shards):
        self.shapes = [s.shape for s in shards]
        self.operands = list(shards)
        self.out_shape = [jax.ShapeDtypeStruct((N_CHIPS, *s.shape), s.dtype) for s in shards]
        n = len(shards)
        self.per = 2 * (N_CHIPS - 1)
        self.scratch = [pltpu.SemaphoreType.DMA((self.per * n,)), pltpu.SemaphoreType.DMA((self.per * n,)),
                        pltpu.SemaphoreType.DMA((n,))]
        self.phases = [self.send, self.forward, self.finish]

    def _copies(self, ins, outs, sems):
        send_sems, recv_sems, local_sems = sems
        x, y, c = _mesh_pos()
        sibling = (x, y, 1 - c)
        mine = 2 * x + y
        local, first, landing, passed, arriving = [], [], [], [], []
        for w, shape in enumerate(self.shapes):
            hr = shape[0] // 2
            half = lambda blk, cc, w=w, hr=hr: outs[w].at[blk, pl.ds(cc * hr, hr), :]
            local.append(pltpu.make_async_copy(ins[w], outs[w].at[mine], local_sems.at[w]))
            for k, (chip, blk) in enumerate(_other_chips(x, y)):
                s = self.per * w + k
                first.append(_remote(ins[w].at[pl.ds(c * hr, hr), :], half(mine, c), send_sems.at[s], recv_sems.at[s],
                                     (*chip, c)))
                landing.append(_remote(half(blk, c), half(blk, c), send_sems.at[s], recv_sems.at[s], sibling))
                s = self.per * w + N_CHIPS - 1 + k
                passed.append(_remote(half(blk, c), half(blk, c), send_sems.at[s], recv_sems.at[s], sibling))
                arriving.append(_remote(half(blk, 1 - c), half(blk, 1 - c), send_sems.at[s], recv_sems.at[s], sibling))
        return local, first, landing, passed, arriving

    def send(self, ins, outs, sems):
        local, first, _, _, _ = self._copies(ins, outs, sems)
        for cp in local + first:
            cp.start()

    def forward(self, ins, outs, sems):
        _, _, landing, passed, _ = self._copies(ins, outs, sems)
        for landed, cp in zip(landing, passed):
            landed.wait_recv()
            cp.start()

    def finish(self, ins, outs, sems):
        local, first, _, passed, arriving = self._copies(ins, outs, sems)
        for cp in arriving:
            cp.wait_recv()
        for cp in first + passed:
            cp.wait_send()
        for cp in local:
            cp.wait()


class _ChipExchange:
    def __init__(self, sums):
        n = len(sums)
        self.n = n
        self.per = N_CHIPS - 1
        self.operands = list(sums)
        self.out_shape = [jax.ShapeDtypeStruct((self.per, *s.shape[1:]), s.dtype) for s in sums]
        self.scratch = [pltpu.SemaphoreType.DMA((self.per * n,)), pltpu.SemaphoreType.DMA((self.per * n,))]
        self.phases = [self.send, self.finish]

    def _copies(self, ins, outs, sems):
        send_sems, recv_sems = sems
        x, y, c = _mesh_pos()
        cps = []
        for w in range(self.n):
            for k, (chip, _) in enumerate(_other_chips(x, y)):
                s = self.per * w + k
                cps.append(_remote(ins[w].at[k + 1], outs[w].at[k], send_sems.at[s], recv_sems.at[s], (*chip, c)))
        return cps

    def send(self, ins, outs, sems):
        for cp in self._copies(ins, outs, sems):
            cp.start()

    def finish(self, ins, outs, sems):
        for cp in self._copies(ins, outs, sems):
            cp.wait()


def _run_exchange(ex, name):
    n_in, n_out = len(ex.operands), len(ex.out_shape)

    def body(*refs):
        ins, outs, sems = refs[:n_in], refs[n_in:n_in + n_out], refs[n_in + n_out:]
        for phase in ex.phases:
            phase(ins, outs, sems)

    return pl.pallas_call(body, name=name, in_specs=[_ANY] * n_in, out_specs=[_ANY] * n_out, out_shape=ex.out_shape,
                          scratch_shapes=ex.scratch)(*ex.operands)


def _call(body, args, *, name, grid, in_specs, out_specs, out_shape, scratch_shapes=(), exchange=None, steps=None):
    params = _params(len(grid))
    if exchange is None:
        out = pl.pallas_call(body, name=name, grid=grid, in_specs=in_specs, out_specs=out_specs, out_shape=out_shape,
                             scratch_shapes=list(scratch_shapes), compiler_params=params)(*args)
        return out, None
    n_in, n_out, n_scr = len(in_specs), len(out_specs), len(scratch_shapes)
    n_xin, n_xout = len(exchange.operands), len(exchange.out_shape)
    assert len(steps) == len(exchange.phases)

    def hosting(*refs):
        cuts = [n_in, n_xin, n_out, n_xout, n_scr]
        parts, at = [], 0
        for size in cuts:
            parts.append(refs[at:at + size])
            at += size
        ins, xins, outs, xouts, scr = parts
        sems = refs[at:]
        step = 0
        for axis, size in enumerate(grid):
            step = step * size + pl.program_id(axis)
        pl.when(step == steps[0])(lambda: exchange.phases[0](xins, xouts, sems))
        body(*ins, *outs, *scr)
        for at_step, phase in zip(steps[1:], exchange.phases[1:]):
            pl.when(step == at_step)(functools.partial(phase, xins, xouts, sems))

    out = pl.pallas_call(
        hosting, name=name, grid=grid,
        in_specs=list(in_specs) + [_ANY] * n_xin, out_specs=list(out_specs) + [_ANY] * n_xout,
        out_shape=list(out_shape) + list(exchange.out_shape),
        scratch_shapes=list(scratch_shapes) + list(exchange.scratch), compiler_params=params,
    )(*args, *exchange.operands)
    return out[:n_out], out[n_out:]


def _pair_exchange(grads, name):
    n = len(grads)

    def body(*refs):
        ins, outs = refs[:n], refs[n:2 * n]
        send_sems, recv_sems = refs[2 * n:]
        x, y, c = _mesh_pos()
        cps = []
        for w in range(n):
            hr = grads[w].shape[1] // 2
            cp = _remote(ins[w].at[:, pl.ds((1 - c) * hr, hr), :], outs[w], send_sems.at[w], recv_sems.at[w], (x, y, 1 - c))
            cp.start()
            cps.append(cp)
        for cp in cps:
            cp.wait()

    return pl.pallas_call(
        body, name=name,
        in_specs=[_ANY] * n, out_specs=[_ANY] * n,
        out_shape=[jax.ShapeDtypeStruct((g.shape[0], g.shape[1] // 2, g.shape[2]), g.dtype) for g in grads],
        scratch_shapes=[pltpu.SemaphoreType.DMA((n,)), pltpu.SemaphoreType.DMA((n,))],
    )(*grads)


def _pair_sum(g, a, pos, name):
    nb, r, c = g.shape
    hr = r // 2

    def body(pos_ref, g_ref, a_ref, o_ref):
        o_ref[...] = (g_ref[...] + a_ref[...]).astype(BF16)

    return pl.pallas_call(
        body, name=name,
        grid_spec=pltpu.PrefetchScalarGridSpec(
            num_scalar_prefetch=1, grid=(nb,),
            in_specs=[pl.BlockSpec((1, hr, c), lambda k, pos: (k ^ pos[0], pos[1], 0)),
                      pl.BlockSpec((1, hr, c), lambda k, pos: (k ^ pos[0], 0, 0))],
            out_specs=pl.BlockSpec((1, hr, c), lambda k, pos: (k, 0, 0))),
        out_shape=jax.ShapeDtypeStruct((nb, hr, c), BF16),
        compiler_params=_params(),
    )(pos, g, a)


def _chip_sum(s, b, pos, name):
    _, hr, c = s.shape

    def body(pos_ref, s_ref, b_ref, o_ref):
        o_ref[...] = (s_ref[0].astype(F32) + b_ref[0].astype(F32)) + (b_ref[1].astype(F32) + b_ref[2].astype(F32))

    return pl.pallas_call(
        body, name=name,
        grid_spec=pltpu.PrefetchScalarGridSpec(
            num_scalar_prefetch=1, grid=(1,),
            in_specs=[pl.BlockSpec((1, hr, c), lambda k, pos: (0, 0, 0)), pl.BlockSpec((N_CHIPS - 1, hr, c), lambda k, pos: (0, 0, 0))],
            out_specs=pl.BlockSpec((hr, c), lambda k, pos: (pos[1], 0))),
        out_shape=jax.ShapeDtypeStruct((2 * hr, c), F32),
        compiler_params=_params(),
    )(pos, s, b)


def _pair_share(grads):
    n = len(grads)

    def body(*refs):
        outs = refs[n:2 * n]
        send_sems, recv_sems = refs[2 * n:]
        x, y, c = _mesh_pos()
        cps = []
        for w in range(n):
            hr = grads[w].shape[0] // 2
            rows = outs[w].at[pl.ds(c * hr, hr), :]
            cp = _remote(rows, rows, send_sems.at[w], recv_sems.at[w], (x, y, 1 - c))
            cp.start()
            cps.append(cp)
        for w, cp in enumerate(cps):
            cp.wait_send()
            hr = grads[w].shape[0] // 2
            other = outs[w].at[pl.ds((1 - c) * hr, hr), :]
            _remote(other, other, send_sems.at[w], recv_sems.at[w], (x, y, 1 - c)).wait_recv()

    return pl.pallas_call(
        body, name="grad_pair_share",
        in_specs=[_ANY] * n, out_specs=[_ANY] * n,
        out_shape=[jax.ShapeDtypeStruct(g.shape, g.dtype) for g in grads],
        input_output_aliases={w: w for w in range(n)},
        scratch_shapes=[pltpu.SemaphoreType.DMA((n,)), pltpu.SemaphoreType.DMA((n,))],
    )(*grads)


def _all_reduce_small(buf):
    n_dev = 8

    def body(in_ref, o_ref, slots, send_sems, recv_sems):
        x, y, c = _mesh_pos()
        me = 4 * x + 2 * y + c
        slots[0] = in_ref[...]
        cps = []
        for q in range(1, n_dev):
            peer = (x ^ (q >> 2), y ^ ((q >> 1) & 1), c ^ (q & 1))
            cp = _remote(in_ref, slots.at[q], send_sems.at[q - 1], recv_sems.at[q - 1], peer)
            cp.start()
            cps.append(cp)
        for cp in cps:
            cp.wait()
        acc = slots[me]
        for d in range(1, n_dev):
            acc = acc + slots[d ^ me]
        o_ref[...] = acc

    return pl.pallas_call(
        body, name="all_reduce_small",
        in_specs=[pl.BlockSpec(memory_space=pltpu.VMEM)], out_specs=pl.BlockSpec(memory_space=pltpu.VMEM),
        out_shape=jax.ShapeDtypeStruct(buf.shape, buf.dtype),
        scratch_shapes=[pltpu.VMEM((n_dev, *buf.shape), buf.dtype), pltpu.SemaphoreType.DMA((n_dev - 1,)),
                        pltpu.SemaphoreType.DMA((n_dev - 1,))],
    )(buf)


def _adamw(w, g, m, v, name):
    r, c = w.shape
    tr = r if r * c * 4 <= (1 << 20) else 64
    bias1 = 1.0 - ADAM_B1 ** ADAM_STEP
    bias2 = 1.0 - ADAM_B2 ** ADAM_STEP

    def body(w_ref, g_ref, m_ref, v_ref, d_ref, mo_ref, vo_ref):
        gg = g_ref[...]
        m2 = ADAM_B1 * m_ref[...] + (1.0 - ADAM_B1) * gg
        v2 = ADAM_B2 * v_ref[...] + (1.0 - ADAM_B2) * (gg * gg)
        mo_ref[...] = m2
        vo_ref[...] = v2
        d_ref[...] = -ADAM_LR * ((m2 / bias1) / (jnp.sqrt(v2 / bias2) + ADAM_EPS) + ADAM_WD * w_ref[...])

    spec = pl.BlockSpec((tr, c), lambda i: (i, 0))
    out = jax.ShapeDtypeStruct((r, c), F32)
    return pl.pallas_call(
        body, name=name, grid=(r // tr,), in_specs=[spec] * 4, out_specs=[spec] * 3, out_shape=[out] * 3,
        compiler_params=_params(),
    )(w, g, m, v)


def _pack(parts):
    flat = jnp.concatenate([parts[n].reshape(-1) for n in _SMALL])
    return flat.reshape(-1, LANES)


def _unpack(buf, like):
    flat = buf.reshape(-1)
    out, at = {}, 0
    for n in _SMALL:
        size = like[n].size
        out[n] = flat[at:at + size].reshape(like[n].shape)
        at += size
    return out


def kernel(x, p, ffn1_norm, ffn1_w_in, ffn1_w_out, mix_norm, w_mix_in, gmlp_v_norm, gmlp_w_s, gmlp_b, w_mix_out, ffn2_norm, ffn2_w_in, ffn2_w_out, ple_norm, ple_w_gate, ple_w_proj, final_norm, loss_target, m_ffn1_norm, m_ffn1_w_in, m_ffn1_w_out, m_mix_norm, m_w_mix_in, m_gmlp_v_norm, m_gmlp_w_s, m_gmlp_b, m_w_mix_out, m_ffn2_norm, m_ffn2_w_in, m_ffn2_w_out, m_ple_norm, m_ple_w_gate, m_ple_w_proj, m_final_norm, v_ffn1_norm, v_ffn1_w_in, v_ffn1_w_out, v_mix_norm, v_w_mix_in, v_gmlp_v_norm, v_gmlp_w_s, v_gmlp_b, v_w_mix_out, v_ffn2_norm, v_ffn2_w_in, v_ffn2_w_out, v_ple_norm, v_ple_w_gate, v_ple_w_proj, v_final_norm):
    args = dict(locals())
    w = {n: args[n] for n in _ALL}
    m = {n: args["m_" + n] for n in _ALL}
    v = {n: args["v_" + n] for n in _ALL}
    xi, yi, ci = _mesh_pos()
    pos = jnp.stack([2 * xi + yi, ci]).astype(jnp.int32)
    shard = {n: w[n][0] for n in _BIG}
    cast = {n: shard[n].astype(BF16) for n in _BIG}
    small = {n: (w[n][0] if w[n].ndim > 2 else w[n].reshape(1, -1)) for n in _SMALL}
    bt = small["gmlp_b"].T
    g_small, pair, from_chips = {}, {}, {}

    def pair_reduce(partials, tag):
        names = list(partials)
        parts = [partials[n].reshape(N_CHIPS, *shard[n].shape) for n in names]
        got = _pair_exchange(parts, "grad_pair_exchange_" + tag)
        for n, g, a in zip(names, parts, got):
            pair[n] = _pair_sum(g, a, pos, "pair_sum_" + n)
        return names

    w1in, w1out = _run_exchange(_WeightGather([cast["ffn1_w_in"], cast["ffn1_w_out"]]), "gather_ffn1")
    w1out = w1out.reshape(D_FF, D_MODEL)
    (h1, gu1), (wmix, wmo) = _ffn_fwd(x[0], small["ffn1_norm"], w1in, w1out, "ffn1_fwd",
                                      _WeightGather([cast["w_mix_in"], cast["w_mix_out"]]))
    wmo = wmo.reshape(D_MODEL, D_MODEL)
    zg, qkv = _mix_in_fwd(h1, small["mix_norm"], wmix)
    gm = _gmlp_fwd(zg, small["gmlp_v_norm"], small["gmlp_w_s"], bt)
    (att, carries), (w2in, w2out, wg, wproj) = _attn_fwd(
        qkv, _WeightGather([cast["ffn2_w_in"], cast["ffn2_w_out"], cast["ple_w_gate"], cast["ple_w_proj"]]))
    w2out = w2out.reshape(D_FF, D_MODEL)
    wg = wg.reshape(D_MODEL, D_MODEL)
    mixed = jnp.concatenate([gm, att], axis=1)
    h2 = _matmul_residual(h1, mixed, wmo, "mix_out_fwd")
    (h3, gu2), _ = _ffn_fwd(h2, small["ffn2_norm"], w2in, w2out, "ffn2_fwd")
    loss_part, g_small["final_norm"], dh4, dgp, dpp, n4, pb = _head(
        h3, p[0, 0], loss_target[0], small["ple_norm"], small["final_norm"], wg, wproj)
    loss = lax.psum(loss_part[0, 0], ("x", "y", "c"))

    part = {"ple_w_gate": _wgrad_rows(n4, dgp, N_CHIPS, "wgrad_ple_gate"),
            "ple_w_proj": _wgrad_cols(pb, dpp, N_CHIPS, "wgrad_ple_proj")}
    dh3, _, g_small["ple_norm"] = _norm_input_bwd(dh4, dgp, wg.reshape(1, D_MODEL, D_MODEL), h3, small["ple_norm"],
                                                  "ple_bwd")
    (dh2, dgu2, n3, act2, dhh3, g_small["ffn2_norm"]), _ = _ffn_bwd(dh3, h2, small["ffn2_norm"], gu2, w2in, w2out,
                                                                   "ffn2_bwd")
    part["ffn2_w_in"] = _wgrad_cols(n3, dgu2, N_CHIPS, "wgrad_ffn2_in")
    part["ffn2_w_out"] = _wgrad_rows(act2, dhh3, 2, "wgrad_ffn2_out")
    dmixed, dh2b = _matmul_nt_cast(dh2, wmo, "mix_out_bwd")
    part["w_mix_out"] = _wgrad_rows(mixed, dh2b, 2, "wgrad_mix_out")
    group = pair_reduce(part, "late")
    dzg, g_small["gmlp_w_s"], dbt, g_small["gmlp_v_norm"] = _gmlp_bwd(zg, dmixed, small["gmlp_v_norm"],
                                                                      small["gmlp_w_s"], bt)
    g_small["gmlp_b"] = dbt.T
    (dq, dk, dv), got = _attn_bwd(qkv, dmixed, carries, _ChipExchange([pair[n] for n in group]))
    from_chips.update(zip(group, got))

    dzmix = jnp.concatenate([dzg, dq, dk, dv], axis=1)
    dh1, n2, g_small["mix_norm"] = _norm_input_bwd(dh2, dzmix, wmix, h1, small["mix_norm"], "mix_in_bwd")
    group = pair_reduce({"w_mix_in": _wgrad_cols(n2, dzmix, N_CHIPS, "wgrad_mix_in")}, "mix")
    (dx, dgu1, n1, act1, dhh1, g_small["ffn1_norm"]), got = _ffn_bwd(
        dh1, x[0], small["ffn1_norm"], gu1, w1in, w1out, "ffn1_bwd", _ChipExchange([pair[n] for n in group]))
    from_chips.update(zip(group, got))

    group = pair_reduce({"ffn1_w_in": _wgrad_cols(n1, dgu1, N_CHIPS, "wgrad_ffn1_in"),
                         "ffn1_w_out": _wgrad_rows(act1, dhh1, 2, "wgrad_ffn1_out")}, "early")
    from_chips.update(zip(group, _run_exchange(_ChipExchange([pair[n] for n in group]), "grad_chip_exchange_early")))

    halves = [_chip_sum(pair[n], from_chips[n], pos, "chip_sum_" + n) for n in _BIG]
    grads = dict(zip(_BIG, _pair_share(halves)))

    delta, new_m, new_v = {}, {}, {}
    for n in _BIG:
        d2, m2, v2 = _adamw(shard[n], grads[n], m[n][0], v[n][0], "adamw_" + n)
        grads[n], delta[n], new_m[n], new_v[n] = grads[n][None], d2[None], m2[None], v2[None]

    g_packed = _all_reduce_small(_pack(g_small))
    d_p, m_p, v_p = _adamw(_pack(w), g_packed, _pack(m), _pack(v), "adamw_small")
    for out, buf in ((grads, g_packed), (delta, d_p), (new_m, m_p), (new_v, v_p)):
        out.update(_unpack(buf, w))

    return (loss, dx[None], *[grads[n] for n in _ALL], *[delta[n] for n in _ALL], *[new_m[n] for n in _ALL],
            *[new_v[n] for n in _ALL])
```

```python
import functools
import math

import jax
import jax.numpy as jnp
from jax import lax
from jax.experimental import pallas as pl
from jax.experimental.pallas import tpu as pltpu

F32, BF16 = jnp.float32, jnp.bfloat16

D_MODEL = 1024
D_FF = 2816
FF_BLOCK = 2 * D_FF // 4
PLE_DIM = 256
CHUNK = 128
GM_HEADS = 4
GM_WIDTH = 512
SB_HEAD_DIM = 64
SB_WIDTH = 512
MIX_IN_WIDTH = 2 * GM_WIDTH + 3 * SB_WIDTH
MIX_BLOCK = MIX_IN_WIDTH // 4
EPS = 1e-6
N_CHIPS = 4
LANES = 128
ATT_BLOCK = 128
ATT_Q = 512
VMEM_LIMIT = 56 * 1024 * 1024

ADAM_LR, ADAM_B1, ADAM_B2, ADAM_EPS, ADAM_WD, ADAM_STEP = 0.001, 0.9, 0.999, 1e-08, 0.01, 10


def _dot(a, b):
    return jnp.dot(a, b, preferred_element_type=F32)


def _dot_nt(a, b):
    return lax.dot_general(a, b, (((1,), (1,)), ((), ())), preferred_element_type=F32)


def _dot_tn(a, b):
    return lax.dot_general(a, b, (((0,), (0,)), ((), ())), preferred_element_type=F32)


def _resident(shape):
    nd = len(shape)
    return pl.BlockSpec(shape, lambda *_: (0,) * nd, pipeline_mode=pl.Buffered(1))


def _rows(tm, width):
    return pl.BlockSpec((tm, width), lambda i: (i, 0))


def _params(n_axes=1):
    return pltpu.CompilerParams(dimension_semantics=("arbitrary",) * n_axes, vmem_limit_bytes=VMEM_LIMIT)


def _rstd(h):
    return lax.rsqrt(jnp.mean(h * h, axis=-1, keepdims=True) + EPS)


def _rms_bwd(dy, h, r, g):
    dyg = dy * g
    dh = r * dyg - h * (r * r * r) * jnp.mean(dyg * h, axis=-1, keepdims=True)
    return dh, dy * h * r


def _gelu(x):
    return 0.5 * x * (1.0 + lax.erf(x * (2.0 ** -0.5)))


def _gelu_grad(x):
    return 0.5 * (1.0 + lax.erf(x * (2.0 ** -0.5))) + x * jnp.exp(-0.5 * x * x) * ((2.0 * jnp.pi) ** -0.5)


def _token_tile(t):
    return min(256, t)


def _ffn_fwd(h, g, win, wout, name, exchange=None):
    t = h.shape[0]
    tm = _token_tile(t)

    def body(h_ref, g_ref, win_ref, wout_ref, ho_ref, gu_ref):
        hh = h_ref[...]
        n = (hh * _rstd(hh) * g_ref[...]).astype(BF16)
        acc = jnp.zeros((tm, D_MODEL), F32)
        for jb in range(2):
            gate = _dot(n, win_ref[jb])
            up = _dot(n, win_ref[2 + jb])
            gu_ref[:, jb * FF_BLOCK:(jb + 1) * FF_BLOCK] = gate.astype(BF16)
            gu_ref[:, D_FF + jb * FF_BLOCK:D_FF + (jb + 1) * FF_BLOCK] = up.astype(BF16)
            act = (gate * jax.nn.sigmoid(gate) * up).astype(BF16)
            acc = acc + _dot(act, wout_ref[jb * FF_BLOCK:(jb + 1) * FF_BLOCK, :])
        ho_ref[...] = hh + 0.5 * acc

    n = t // tm
    return _call(
        body, (h, g, win, wout), name=name, grid=(n,),
        in_specs=[_rows(tm, D_MODEL), _resident((1, D_MODEL)), _resident(win.shape), _resident(wout.shape)],
        out_specs=[_rows(tm, D_MODEL), _rows(tm, 2 * D_FF)],
        out_shape=[jax.ShapeDtypeStruct((t, D_MODEL), F32), jax.ShapeDtypeStruct((t, 2 * D_FF), BF16)],
        exchange=exchange, steps=(0, (2 * n) // 3, n - 1))


def _ffn_bwd(dho, h, g, gu, win, wout, name, exchange=None):
    t = h.shape[0]
    tm = _token_tile(t)

    def body(dho_ref, h_ref, g_ref, gu_ref, win_ref, wout_ref, dh_ref, dgu_ref, n_ref, act_ref, dhh_ref, dg_ref):
        i = pl.program_id(0)
        hh = h_ref[...]
        gg = g_ref[...]
        r = _rstd(hh)
        n_ref[...] = (hh * r * gg).astype(BF16)
        dho = dho_ref[...]
        dhh = (0.5 * dho).astype(BF16)
        dhh_ref[...] = dhh
        dn = jnp.zeros((tm, D_MODEL), F32)
        for jb in range(2):
            cg = slice(jb * FF_BLOCK, (jb + 1) * FF_BLOCK)
            cu = slice(D_FF + jb * FF_BLOCK, D_FF + (jb + 1) * FF_BLOCK)
            dact = _dot_nt(dhh, wout_ref[cg, :])
            gate = gu_ref[:, cg].astype(F32)
            up = gu_ref[:, cu].astype(F32)
            sg = jax.nn.sigmoid(gate)
            silu = gate * sg
            act_ref[:, cg] = (silu * up).astype(BF16)
            dgate = (dact * up * (sg * (1.0 + gate * (1.0 - sg)))).astype(BF16)
            dup = (dact * silu).astype(BF16)
            dgu_ref[:, cg] = dgate
            dgu_ref[:, cu] = dup
            dn = dn + _dot_nt(dgate, win_ref[jb]) + _dot_nt(dup, win_ref[2 + jb])
        dh, dg_rows = _rms_bwd(dn, hh, r, gg)
        dh_ref[...] = dho + dh

        @pl.when(i == 0)
        def _():
            dg_ref[...] = jnp.zeros_like(dg_ref)

        dg_ref[...] += jnp.sum(dg_rows, axis=0, keepdims=True)

    n = t // tm
    return _call(
        body, (dho, h, g, gu, win, wout), name=name, grid=(n,),
        in_specs=[_rows(tm, D_MODEL), _rows(tm, D_MODEL), _resident((1, D_MODEL)), _rows(tm, 2 * D_FF),
                  _resident(win.shape), _resident(wout.shape)],
        out_specs=[_rows(tm, D_MODEL), _rows(tm, 2 * D_FF), _rows(tm, D_MODEL), _rows(tm, D_FF), _rows(tm, D_MODEL),
                   pl.BlockSpec((1, D_MODEL), lambda i: (0, 0))],
        out_shape=[jax.ShapeDtypeStruct((t, D_MODEL), F32), jax.ShapeDtypeStruct((t, 2 * D_FF), BF16),
                   jax.ShapeDtypeStruct((t, D_MODEL), BF16), jax.ShapeDtypeStruct((t, D_FF), BF16),
                   jax.ShapeDtypeStruct((t, D_MODEL), BF16), jax.ShapeDtypeStruct((1, D_MODEL), F32)],
        exchange=exchange, steps=(0, n - 1))


def _wgrad(a, b, out_shape, out_block, out_index, a_width, b_width, grid_ij, name):
    t = a.shape[0]
    tk = min(2048, t)

    def body(a_ref, b_ref, o_ref):
        k = pl.program_id(2)
        prod = _dot_tn(a_ref[...], b_ref[...]).reshape(o_ref.shape)

        @pl.when(k == 0)
        def _():
            o_ref[...] = prod

        @pl.when(k > 0)
        def _():
            o_ref[...] += prod

    return pl.pallas_call(
        body, name=name, grid=(*grid_ij, t // tk),
        in_specs=[pl.BlockSpec((tk, a_width), lambda i, j, k: (k, i)), pl.BlockSpec((tk, b_width), lambda i, j, k: (k, j))],
        out_specs=pl.BlockSpec(out_block, lambda i, j, k: out_index(i, j)),
        out_shape=jax.ShapeDtypeStruct(out_shape, F32),
        compiler_params=_params(3),
    )(a, b)


def _wgrad_cols(a, b, n_blocks, name):
    ka, nb = a.shape[1], b.shape[1] // n_blocks
    return _wgrad(a, b, (n_blocks, ka, nb), (1, ka, nb), lambda i, j: (j, 0, 0), ka, nb, (1, n_blocks), name)


def _wgrad_rows(a, b, n_blocks, name):
    ka, nb = a.shape[1] // n_blocks, b.shape[1]
    return _wgrad(a, b, (a.shape[1], nb), (ka, nb), lambda i, j: (i, 0), ka, nb, (n_blocks, 1), name)


def _mix_in_fwd(h, g, wmix):
    t = h.shape[0]
    tm = _token_tile(t)
    gw2 = 2 * GM_WIDTH

    def body(h_ref, g_ref, w_ref, zg_ref, qkv_ref):
        hh = h_ref[...]
        n = (hh * _rstd(hh) * g_ref[...]).astype(BF16)
        for b in range(N_CHIPS):
            z = _dot(n, w_ref[b])
            lo, hi = b * MIX_BLOCK, (b + 1) * MIX_BLOCK
            if hi <= gw2:
                zg_ref[:, lo:hi] = z
            elif lo >= gw2:
                qkv_ref[:, lo - gw2:hi - gw2] = z.astype(BF16)
            else:
                zg_ref[:, lo:gw2] = z[:, :gw2 - lo]
                qkv_ref[:, 0:hi - gw2] = z[:, gw2 - lo:].astype(BF16)

    return pl.pallas_call(
        body, name="mix_in_fwd", grid=(t // tm,),
        in_specs=[_rows(tm, D_MODEL), _resident((1, D_MODEL)), _resident(wmix.shape)],
        out_specs=[_rows(tm, gw2), _rows(tm, 3 * SB_WIDTH)],
        out_shape=[jax.ShapeDtypeStruct((t, gw2), F32), jax.ShapeDtypeStruct((t, 3 * SB_WIDTH), BF16)],
        compiler_params=_params(),
    )(h, g, wmix)


def _causal_chunk_mask():
    row = lax.broadcasted_iota(jnp.int32, (CHUNK, CHUNK), 0)
    col = lax.broadcasted_iota(jnp.int32, (CHUNK, CHUNK), 1)
    return row >= col


def _gmlp_tile(t):
    return min(512, t)


def _gmlp_fwd(zg, gv, ws, bt):
    t = zg.shape[0]
    tm = _gmlp_tile(t)

    def body(zg_ref, gv_ref, ws_ref, bt_ref, o_ref):
        u = _gelu(zg_ref[:, :GM_WIDTH])
        v = _gelu(zg_ref[:, GM_WIDTH:])
        vn = (v * _rstd(v) * gv_ref[...]).astype(BF16)
        mask = _causal_chunk_mask()
        for hd in range(GM_HEADS):
            wm = jnp.where(mask, ws_ref[hd], 0.0).astype(BF16)
            cols = slice(hd * CHUNK, (hd + 1) * CHUNK)
            for c in range(tm // CHUNK):
                rows = slice(c * CHUNK, (c + 1) * CHUNK)
                sv = _dot(wm, vn[rows, cols]) + bt_ref[:, hd:hd + 1]
                o_ref[rows, cols] = (u[rows, cols] * sv).astype(BF16)

    return pl.pallas_call(
        body, name="gmlp_fwd", grid=(t // tm,),
        in_specs=[_rows(tm, 2 * GM_WIDTH), _resident((1, GM_WIDTH)), _resident(ws.shape), _resident(bt.shape)],
        out_specs=_rows(tm, GM_WIDTH),
        out_shape=jax.ShapeDtypeStruct((t, GM_WIDTH), BF16),
        compiler_params=_params(),
    )(zg, gv, ws, bt)


def _gmlp_bwd(zg, dmixed, gv, ws, bt):
    t = zg.shape[0]
    tm = _gmlp_tile(t)

    def body(zg_ref, dgm_ref, gv_ref, ws_ref, bt_ref, dzg_ref, dws_ref, dbt_ref, dgv_ref):
        i = pl.program_id(0)

        @pl.when(i == 0)
        def _():
            dws_ref[...] = jnp.zeros_like(dws_ref)
            dbt_ref[...] = jnp.zeros_like(dbt_ref)
            dgv_ref[...] = jnp.zeros_like(dgv_ref)

        zu = zg_ref[:, :GM_WIDTH]
        zv = zg_ref[:, GM_WIDTH:]
        u = _gelu(zu)
        v = _gelu(zv)
        r = _rstd(v)
        gvv = gv_ref[...]
        vn = (v * r * gvv).astype(BF16)
        dgm = dgm_ref[...].astype(F32)
        dsv = (dgm * u).astype(BF16)
        mask = _causal_chunk_mask()
        du_cols, dvn_cols = [], []
        for hd in range(GM_HEADS):
            wm = jnp.where(mask, ws_ref[hd], 0.0).astype(BF16)
            cols = slice(hd * CHUNK, (hd + 1) * CHUNK)
            dw = jnp.zeros((CHUNK, CHUNK), F32)
            db = jnp.zeros((CHUNK, 1), F32)
            du_rows, dvn_rows = [], []
            for c in range(tm // CHUNK):
                rows = slice(c * CHUNK, (c + 1) * CHUNK)
                sv = _dot(wm, vn[rows, cols]) + bt_ref[:, hd:hd + 1]
                du_rows.append(dgm[rows, cols] * sv)
                dvn_rows.append(_dot_tn(wm, dsv[rows, cols]))
                dw = dw + _dot_nt(dsv[rows, cols], vn[rows, cols])
                db = db + jnp.sum(dsv[rows, cols].astype(F32), axis=1, keepdims=True)
            dws_ref[hd] += jnp.where(mask, dw, 0.0)
            dbt_ref[:, hd:hd + 1] += db
            du_cols.append(jnp.concatenate(du_rows, axis=0))
            dvn_cols.append(jnp.concatenate(dvn_rows, axis=0))
        du = jnp.concatenate(du_cols, axis=1)
        dvn = jnp.concatenate(dvn_cols, axis=1)
        dv, dgv_rows = _rms_bwd(dvn, v, r, gvv)
        dgv_ref[...] += jnp.sum(dgv_rows, axis=0, keepdims=True)
        dzg_ref[:, :GM_WIDTH] = (du * _gelu_grad(zu)).astype(BF16)
        dzg_ref[:, GM_WIDTH:] = (dv * _gelu_grad(zv)).astype(BF16)

    const = lambda nd: (lambda i: (0,) * nd)
    return pl.pallas_call(
        body, name="gmlp_bwd", grid=(t // tm,),
        in_specs=[_rows(tm, 2 * GM_WIDTH), _rows(tm, GM_WIDTH), _resident((1, GM_WIDTH)), _resident(ws.shape),
                  _resident(bt.shape)],
        out_specs=[_rows(tm, 2 * GM_WIDTH), pl.BlockSpec(ws.shape, const(3)), pl.BlockSpec(bt.shape, const(2)),
                   pl.BlockSpec((1, GM_WIDTH), const(2))],
        out_shape=[jax.ShapeDtypeStruct((t, 2 * GM_WIDTH), BF16), jax.ShapeDtypeStruct(ws.shape, F32),
                   jax.ShapeDtypeStruct(bt.shape, F32), jax.ShapeDtypeStruct((1, GM_WIDTH), F32)],
        compiler_params=_params(),
    )(zg, dmixed, gv, ws, bt)


def _att_masks():
    tb = ATT_BLOCK
    lane = lax.broadcasted_iota(jnp.int32, (1, LANES), 1)
    rj = lax.broadcasted_iota(jnp.int32, (2 * tb, 2 * tb), 0)
    cs = lax.broadcasted_iota(jnp.int32, (2 * tb, 2 * tb), 1)
    same_head = ((rj < tb) & (cs < tb)) | ((rj >= tb) & (cs >= tb))
    suffix = jnp.where(same_head & (rj >= cs), 1.0, 0.0).astype(BF16)
    prefix = jnp.where(same_head & (rj <= cs), 1.0, 0.0).astype(BF16)
    left = lax.broadcasted_iota(jnp.int32, (1, 2 * tb), 1) < tb
    tq = lax.broadcasted_iota(jnp.int32, (ATT_Q, 4 * tb), 0)
    ts = lax.broadcasted_iota(jnp.int32, (ATT_Q, 4 * tb), 1)
    key = jnp.where(ts < 2 * tb, ts & (tb - 1), (ts & (tb - 1)) + tb)
    return lane, suffix, prefix, left, key, tq


def _att_fill(k_ref, v_ref, kcat, vcat, n_blocks, lane):
    tb = ATT_BLOCK
    first = lane < SB_HEAD_DIM

    def fill(jb, carry):
        rows = pl.ds(pl.multiple_of(jb * tb, tb), tb)
        top = pl.ds(pl.multiple_of(jb * 2 * tb, tb), tb)
        bot = pl.ds(pl.multiple_of(jb * 2 * tb + tb, tb), tb)
        kb = k_ref[rows, :]
        vb = v_ref[rows, :]
        zero = jnp.zeros_like(kb)
        kcat[top, :] = jnp.where(first, kb, zero)
        kcat[bot, :] = jnp.where(first, zero, kb)
        vcat[top, :] = jnp.where(first, vb, zero)
        vcat[bot, :] = jnp.where(first, zero, vb)
        return carry

    lax.fori_loop(0, n_blocks, fill, 0)


def _block_sums(x, m):
    return _dot(x.astype(BF16), m)


def _softplus(z):
    return jnp.maximum(z, 0.0) + jnp.log(1.0 + jnp.exp2(jnp.abs(z) * -math.log2(math.e)))


def _scaled_queries(q_ref):
    return (q_ref[...].astype(F32) * (SB_HEAD_DIM ** -0.5)).astype(BF16)


def _att_specs(t):
    n_pairs = SB_WIDTH // LANES
    q_spec = pl.BlockSpec((ATT_Q, LANES), lambda p, i: (i, p))
    k_spec = pl.BlockSpec((t, LANES), lambda p, i: (0, n_pairs + p))
    v_spec = pl.BlockSpec((t, LANES), lambda p, i: (0, 2 * n_pairs + p))
    return n_pairs, q_spec, k_spec, v_spec


def _attn_fwd(qkv, exchange=None):
    t = qkv.shape[0]
    tb = ATT_BLOCK
    nkb = t // tb
    assert 2 * nkb <= LANES and t % ATT_Q == 0 and ATT_Q == 4 * tb
    n_pairs, q_spec, k_spec, v_spec = _att_specs(t)

    def body(q_ref, k_ref, v_ref, o_ref, ct_ref, kcat, vcat, acc, carry, z0, r0, z1, r1):
        i = pl.program_id(1)
        lane, suffix, _, left, key, tq = _att_masks()

        @pl.when(i == 0)
        def _():
            _att_fill(k_ref, v_ref, kcat, vcat, nkb, lane)

        q = _scaled_queries(q_ref)
        acc[...] = jnp.zeros_like(acc)
        carry[...] = jnp.zeros_like(carry)
        ct_ref[0] = jnp.zeros((ATT_Q, LANES), F32)

        def key_rows(m):
            return pl.ds(pl.multiple_of(m * 4 * tb, 4 * tb), 4 * tb)

        def scores(m, zb, rb, causal=None):
            z = _dot_nt(q, kcat[key_rows(m), :])
            zb[...] = z
            sp = _softplus(z)
            if causal is not None:
                sp = jnp.where(causal, sp, 0.0)
            for g in (1, 0):
                cols = slice(g * 2 * tb, (g + 1) * 2 * tb)
                rb[:, cols] = _block_sums(sp[:, cols], suffix)

        def weigh(m, zb, rb, causal=None):
            probs = [None, None]
            for g in (1, 0):
                cols = slice(g * 2 * tb, (g + 1) * 2 * tb)
                j = 2 * m + g
                r = rb[:, cols]
                c = carry[...]
                ct_ref[0] = jnp.where(lane == j, c[:, :tb], jnp.where(lane == nkb + j, c[:, tb:], ct_ref[0]))
                a = jnp.exp(zb[:, cols] - (r + c))
                if causal is not None:
                    a = jnp.where(causal[:, cols], a, 0.0)
                probs[g] = a.astype(BF16)
                carry[...] = c + jnp.where(left, r[:, 0:1], r[:, tb:tb + 1])
            acc[...] += _dot(jnp.concatenate(probs, axis=1), vcat[key_rows(m), :])

        for e in (1, 0):
            causal = key + e * 2 * tb < tq
            scores(2 * i + e, z0, r0, causal)
            weigh(2 * i + e, z0, r0, causal)

        @pl.when(i > 0)
        def _():
            scores(2 * i - 1, z1, r1)

            def loop(k, c):
                u = i - 1 - k
                scores(2 * u, z0, r0)
                weigh(2 * u + 1, z1, r1)
                scores(jnp.maximum(2 * u - 1, 0), z1, r1)
                weigh(2 * u, z0, r0)
                return c

            lax.fori_loop(0, i, loop, 0)

        o_ref[...] = acc[...].astype(BF16)

    tile = pltpu.VMEM((ATT_Q, 4 * tb), F32)

    nq = t // ATT_Q
    return _call(
        body, (qkv, qkv, qkv), name="attn_fwd", grid=(n_pairs, nq),
        in_specs=[q_spec, k_spec, v_spec],
        out_specs=[pl.BlockSpec((ATT_Q, LANES), lambda p, i: (i, p)), pl.BlockSpec((1, ATT_Q, LANES), lambda p, i: (p, i, 0))],
        out_shape=[jax.ShapeDtypeStruct((t, SB_WIDTH), BF16), jax.ShapeDtypeStruct((n_pairs, t, LANES), F32)],
        scratch_shapes=[pltpu.VMEM((2 * t, LANES), BF16), pltpu.VMEM((2 * t, LANES), BF16),
                        pltpu.VMEM((ATT_Q, LANES), F32), pltpu.VMEM((ATT_Q, 2 * tb), F32), tile, tile, tile, tile],
        exchange=exchange, steps=(0, (n_pairs - 1) * nq - 1, n_pairs * nq - 1))


def _attn_bwd(qkv, dmixed, carries, exchange=None):
    t = qkv.shape[0]
    tb = ATT_BLOCK
    nkb = t // tb
    nq = t // ATT_Q
    scale = SB_HEAD_DIM ** -0.5
    n_pairs, q_spec, k_spec, v_spec = _att_specs(t)
    gm_blocks = GM_WIDTH // LANES

    def body(q_ref, k_ref, v_ref, do_ref, ct_ref, dq_ref, dk_ref, dv_ref, kcat, vcat, dkacc, dvacc, dqacc, carry,
             z0, r0, s0, a0, z1, r1, s1, a1):
        i = pl.program_id(1)
        lane, suffix, prefix, left, key, tq = _att_masks()
        first = lane < SB_HEAD_DIM

        @pl.when(i == 0)
        def _():
            _att_fill(k_ref, v_ref, kcat, vcat, nkb, lane)
            dkacc[...] = jnp.zeros_like(dkacc)
            dvacc[...] = jnp.zeros_like(dvacc)

        q = _scaled_queries(q_ref)
        do = do_ref[...]
        dqacc[...] = jnp.zeros_like(dqacc)
        carry[...] = jnp.zeros_like(carry)

        def key_rows(m):
            return pl.ds(pl.multiple_of(m * 4 * tb, 4 * tb), 4 * tb)

        def front(m, bufs, causal=None):
            zb, rb, sb, ab = bufs
            z = _dot_nt(q, kcat[key_rows(m), :])
            zb[...] = z
            sp = _softplus(z)
            sb[...] = jnp.exp(z - sp)
            if causal is not None:
                sp = jnp.where(causal, sp, 0.0)
            for g in (0, 1):
                cols = slice(g * 2 * tb, (g + 1) * 2 * tb)
                rb[:, cols] = _block_sums(sp[:, cols], suffix)
            ab[...] = _dot_nt(do, vcat[key_rows(m), :])

        def back(m, bufs, causal=None):
            zb, rb, sb, ab = bufs
            dzs, probs = [None, None], [None, None]
            for g in (0, 1):
                cols = slice(g * 2 * tb, (g + 1) * 2 * tb)
                j = 2 * m + g
                ct = ct_ref[0]
                ca = jnp.sum(jnp.where(lane == j, ct, 0.0), axis=1, keepdims=True)
                cb = jnp.sum(jnp.where(lane == nkb + j, ct, 0.0), axis=1, keepdims=True)
                a = jnp.exp(zb[:, cols] - (rb[:, cols] + jnp.where(left, ca, cb)))
                if causal is not None:
                    a = jnp.where(causal[:, cols], a, 0.0)
                de = ab[:, cols] * a
                cl = _block_sums(de, prefix)
                pre = carry[...]
                dz = de - sb[:, cols] * (cl + pre)
                if causal is not None:
                    dz = jnp.where(causal[:, cols], dz, 0.0)
                carry[...] = pre + jnp.where(left, cl[:, tb - 1:tb], cl[:, 2 * tb - 1:2 * tb])
                dzs[g] = dz.astype(BF16)
                probs[g] = a.astype(BF16)
            dzb = jnp.concatenate(dzs, axis=1)
            dqacc[...] += _dot(dzb, kcat[key_rows(m), :])
            dkc = _dot_tn(dzb, q)
            dvc = _dot_tn(jnp.concatenate(probs, axis=1), do)
            out_rows = pl.ds(pl.multiple_of(m * 2 * tb, 2 * tb), 2 * tb)
            pick = lambda x: jnp.concatenate([jnp.where(first, x[0:tb], x[tb:2 * tb]),
                                              jnp.where(first, x[2 * tb:3 * tb], x[3 * tb:4 * tb])], axis=0)
            dkacc[out_rows, :] += pick(dkc)
            dvacc[out_rows, :] += pick(dvc)

        b0, b1 = (z0, r0, s0, a0), (z1, r1, s1, a1)

        @pl.when(i > 0)
        def _():
            front(0, b0)

            def loop(u, c):
                front(2 * u + 1, b1)
                back(2 * u, b0)
                front(jnp.minimum(2 * u + 2, 2 * i - 1), b0)
                back(2 * u + 1, b1)
                return c

            lax.fori_loop(0, i, loop, 0)

        for e in (0, 1):
            causal = key + e * 2 * tb < tq
            front(2 * i + e, b0, causal)
            back(2 * i + e, b0, causal)

        dq_ref[...] = (dqacc[...] * scale).astype(BF16)

        @pl.when(i == nq - 1)
        def _():
            dk_ref[...] = dkacc[...].astype(BF16)
            dv_ref[...] = dvacc[...].astype(BF16)

    col = pl.BlockSpec((t, LANES), lambda p, i: (0, p))
    out = jax.ShapeDtypeStruct((t, SB_WIDTH), BF16)
    tile = pltpu.VMEM((ATT_Q, 4 * tb), F32)
    return _call(
        body, (qkv, qkv, qkv, dmixed, carries), name="attn_bwd", grid=(n_pairs, nq),
        in_specs=[q_spec, k_spec, v_spec, pl.BlockSpec((ATT_Q, LANES), lambda p, i: (i, gm_blocks + p)),
                  pl.BlockSpec((1, ATT_Q, LANES), lambda p, i: (p, i, 0))],
        out_specs=[pl.BlockSpec((ATT_Q, LANES), lambda p, i: (i, p)), col, col],
        out_shape=[out, out, out],
        scratch_shapes=[pltpu.VMEM((2 * t, LANES), BF16), pltpu.VMEM((2 * t, LANES), BF16),
                        pltpu.VMEM((t, LANES), F32), pltpu.VMEM((t, LANES), F32),
                        pltpu.VMEM((ATT_Q, LANES), F32), pltpu.VMEM((ATT_Q, 2 * tb), F32)] + [tile] * 8,
        exchange=exchange, steps=(0, n_pairs * nq - 1))


def _matmul_residual(res, a, w, name):
    t = a.shape[0]
    tm = _token_tile(t)

    def body(res_ref, a_ref, w_ref, o_ref):
        o_ref[...] = res_ref[...] + _dot(a_ref[...], w_ref[...])

    return pl.pallas_call(
        body, name=name, grid=(t // tm,),
        in_specs=[_rows(tm, res.shape[1]), _rows(tm, a.shape[1]), _resident(w.shape)],
        out_specs=_rows(tm, res.shape[1]),
        out_shape=jax.ShapeDtypeStruct(res.shape, F32),
        compiler_params=_params(),
    )(res, a, w)


def _matmul_nt_cast(dy, w, name):
    t = dy.shape[0]
    tm = _token_tile(t)

    def body(dy_ref, w_ref, o_ref, dyb_ref):
        dyb = dy_ref[...].astype(BF16)
        dyb_ref[...] = dyb
        o_ref[...] = _dot_nt(dyb, w_ref[...]).astype(BF16)

    return pl.pallas_call(
        body, name=name, grid=(t // tm,),
        in_specs=[_rows(tm, dy.shape[1]), _resident(w.shape)],
        out_specs=[_rows(tm, w.shape[0]), _rows(tm, dy.shape[1])],
        out_shape=[jax.ShapeDtypeStruct((t, w.shape[0]), BF16), jax.ShapeDtypeStruct(dy.shape, BF16)],
        compiler_params=_params(),
    )(dy, w)


def _norm_input_bwd(dres, dz, w, h, g, name):
    t = h.shape[0]
    tm = _token_tile(t)
    nb, _, width = w.shape

    def body(dres_ref, dz_ref, w_ref, h_ref, g_ref, dh_ref, n_ref, dg_ref):
        i = pl.program_id(0)
        hh = h_ref[...]
        gg = g_ref[...]
        r = _rstd(hh)
        n_ref[...] = (hh * r * gg).astype(BF16)
        dn = jnp.zeros((tm, D_MODEL), F32)
        for b in range(nb):
            dn = dn + _dot_nt(dz_ref[:, b * width:(b + 1) * width], w_ref[b])
        dh, dg_rows = _rms_bwd(dn, hh, r, gg)
        dh_ref[...] = dres_ref[...] + dh

        @pl.when(i == 0)
        def _():
            dg_ref[...] = jnp.zeros_like(dg_ref)

        dg_ref[...] += jnp.sum(dg_rows, axis=0, keepdims=True)

    return pl.pallas_call(
        body, name=name, grid=(t // tm,),
        in_specs=[_rows(tm, D_MODEL), _rows(tm, nb * width), _resident(w.shape), _rows(tm, D_MODEL),
                  _resident((1, D_MODEL))],
        out_specs=[_rows(tm, D_MODEL), _rows(tm, D_MODEL), pl.BlockSpec((1, D_MODEL), lambda i: (0, 0))],
        out_shape=[jax.ShapeDtypeStruct((t, D_MODEL), F32), jax.ShapeDtypeStruct((t, D_MODEL), BF16),
                   jax.ShapeDtypeStruct((1, D_MODEL), F32)],
        compiler_params=_params(),
    )(dres, dz, w, h, g)


def _head(h, p, target, gple, gfin, wg, wproj):
    t = h.shape[0]
    tm = _token_tile(t)
    pw = D_MODEL // N_CHIPS

    def body(h_ref, p_ref, tgt_ref, gple_ref, gfin_ref, wg_ref, wproj_ref,
             loss_ref, dgf_ref, dh_ref, dgp_ref, dpp_ref, n_ref, pb_ref):
        i = pl.program_id(0)
        hh = h_ref[...]
        n = (hh * _rstd(hh) * gple_ref[...]).astype(BF16)
        n_ref[...] = n
        gate = jax.nn.sigmoid(_dot(n, wg_ref[...]))
        pb = p_ref[...].astype(BF16)
        pb_ref[...] = pb
        pp = jnp.concatenate([_dot(pb, wproj_ref[b]) for b in range(N_CHIPS)], axis=1)
        h4 = hh + gate * pp
        r = _rstd(h4)
        gf = gfin_ref[...]
        err = h4 * r * gf - tgt_ref[...]
        dy = err * (1.0 / D_MODEL)
        dh4, dgf_rows = _rms_bwd(dy, h4, r, gf)
        dh_ref[...] = dh4
        dgp_ref[...] = (dh4 * pp * gate * (1.0 - gate)).astype(BF16)
        dpp_ref[...] = (dh4 * gate).astype(BF16)

        @pl.when(i == 0)
        def _():
            loss_ref[...] = jnp.zeros_like(loss_ref)
            dgf_ref[...] = jnp.zeros_like(dgf_ref)

        loss_ref[...] += (0.5 / D_MODEL) * jnp.sum(err * err)
        dgf_ref[...] += jnp.sum(dgf_rows, axis=0, keepdims=True)

    bf = lambda w: jax.ShapeDtypeStruct((t, w), BF16)
    const = lambda i: (0, 0)
    return pl.pallas_call(
        body, name="head", grid=(t // tm,),
        in_specs=[_rows(tm, D_MODEL), _rows(tm, PLE_DIM), _rows(tm, D_MODEL), _resident((1, D_MODEL)),
                  _resident((1, D_MODEL)), _resident(wg.shape), _resident(wproj.shape)],
        out_specs=[pl.BlockSpec((1, LANES), const), pl.BlockSpec((1, D_MODEL), const), _rows(tm, D_MODEL),
                   _rows(tm, D_MODEL), _rows(tm, D_MODEL), _rows(tm, D_MODEL), _rows(tm, PLE_DIM)],
        out_shape=[jax.ShapeDtypeStruct((1, LANES), F32), jax.ShapeDtypeStruct((1, D_MODEL), F32),
                   jax.ShapeDtypeStruct((t, D_MODEL), F32), bf(D_MODEL), bf(D_MODEL), bf(D_MODEL), bf(PLE_DIM)],
        compiler_params=_params(),
    )(h, p, target, gple, gfin, wg, wproj)


_BIG = ("ffn1_w_in", "ffn1_w_out", "w_mix_in", "w_mix_out", "ffn2_w_in", "ffn2_w_out", "ple_w_gate", "ple_w_proj")
_SMALL = ("ffn1_norm", "mix_norm", "gmlp_v_norm", "gmlp_w_s", "gmlp_b", "ffn2_norm", "ple_norm", "final_norm")
_ALL = ("ffn1_norm", "ffn1_w_in", "ffn1_w_out", "mix_norm", "w_mix_in", "gmlp_v_norm", "gmlp_w_s", "gmlp_b", "w_mix_out",
        "ffn2_norm", "ffn2_w_in", "ffn2_w_out", "ple_norm", "ple_w_gate", "ple_w_proj", "final_norm")
_ANY = pl.BlockSpec(memory_space=pl.ANY)
_MESH = pl.DeviceIdType.MESH


def _mesh_pos():
    return lax.axis_index("x"), lax.axis_index("y"), lax.axis_index("c")


def _other_chips(x, y):
    return [((x, 1 - y), 2 * x + 1 - y), ((1 - x, y), 2 * (1 - x) + y), ((1 - x, 1 - y), 2 * (1 - x) + 1 - y)]


def _remote(src, dst, send_sem, recv_sem, device):
    return pltpu.make_async_remote_copy(src_ref=src, dst_ref=dst, send_sem=send_sem, recv_sem=recv_sem,
                                        device_id=device, device_id_type=_MESH)


class _WeightGather:
    def __init__(self, shards):
        self.shapes = [s.shape for s in shards]
        self.operands = list(shards)
        self.out_shape = [jax.ShapeDtypeStruct((N_CHIPS, *s.shape), s.dtype) for s in shards]
        n = len(shards)
        self.per = 2 * (N_CHIPS - 1)
        self.scratch = [pltpu.SemaphoreType.DMA((self.per * n,)), pltpu.SemaphoreType.DMA((self.per * n,)),
                        pltpu.SemaphoreType.DMA((n,))]
        self.phases = [self.send, self.forward, self.finish]

    def _copies(self, ins, outs, sems):
        send_sems, recv_sems, local_sems = sems
        x, y, c = _mesh_pos()
        sibling = (x, y, 1 - c)
        mine = 2 * x + y
        local, first, landing, passed, arriving = [], [], [], [], []
        for w, shape in enumerate(self.shapes):
            hr = shape[0] // 2
            half = lambda blk, cc, w=w, hr=hr: outs[w].at[blk, pl.ds(cc * hr, hr), :]
            local.append(pltpu.make_async_copy(ins[w], outs[w].at[mine], local_sems.at[w]))
            for k, (chip, blk) in enumerate(_other_chips(x, y)):
                s = self.per * w + k
                first.append(_remote(ins[w].at[pl.ds(c * hr, hr), :], half(mine, c), send_sems.at[s], recv_sems.at[s],
                                     (*chip, c)))
                landing.append(_remote(half(blk, c), half(blk, c), send_sems.at[s], recv_sems.at[s], sibling))
                s = self.per * w + N_CHIPS - 1 + k
                passed.append(_remote(half(blk, c), half(blk, c), send_sems.at[s], recv_sems.at[s], sibling))
                arriving.append(_remote(half(blk, 1 - c), half(blk, 1 - c), send_sems.at[s], recv_sems.at[s], sibling))
        return local, first, landing, passed, arriving

    def send(self, ins, outs, sems):
        local, first, _, _, _ = self._copies(ins, outs, sems)
        for cp in local + first:
            cp.start()

    def forward(self, ins, outs, sems):
        _, _, landing, passed, _ = self._copies(ins, outs, sems)
        for landed, cp in zip(landing, passed):
            landed.wait_recv()
            cp.start()

    def finish(self, ins, outs, sems):
        local, first, _, passed, arriving = self._copies(ins, outs, sems)
        for cp in arriving:
            cp.wait_recv()
        for cp in first + passed:
            cp.wait_send()
        for cp in local:
            cp.wait()


class _ChipExchange:
    def __init__(self, sums):
        n = len(sums)
        self.n = n
        self.per = N_CHIPS - 1
        self.operands = list(sums)
        self.out_shape = [jax.ShapeDtypeStruct((self.per, *s.shape[1:]), s.dtype) for s in sums]
        self.scratch = [pltpu.SemaphoreType.DMA((self.per * n,)), pltpu.SemaphoreType.DMA((self.per * n,))]
        self.phases = [self.send, self.finish]

    def _copies(self, ins, outs, sems):
        send_sems, recv_sems = sems
        x, y, c = _mesh_pos()
        cps = []
        for w in range(self.n):
            for k, (chip, _) in enumerate(_other_chips(x, y)):
                s = self.per * w + k
                cps.append(_remote(ins[w].at[k + 1], outs[w].at[k], send_sems.at[s], recv_sems.at[s], (*chip, c)))
        return cps

    def send(self, ins, outs, sems):
        for cp in self._copies(ins, outs, sems):
            cp.start()

    def finish(self, ins, outs, sems):
        for cp in self._copies(ins, outs, sems):
            cp.wait()


def _run_exchange(ex, name):
    n_in, n_out = len(ex.operands), len(ex.out_shape)

    def body(*refs):
        ins, outs, sems = refs[:n_in], refs[n_in:n_in + n_out], refs[n_in + n_out:]
        for phase in ex.phases:
            phase(ins, outs, sems)

    return pl.pallas_call(body, name=name, in_specs=[_ANY] * n_in, out_specs=[_ANY] * n_out, out_shape=ex.out_shape,
                          scratch_shapes=ex.scratch)(*ex.operands)


def _call(body, args, *, name, grid, in_specs, out_specs, out_shape, scratch_shapes=(), exchange=None, steps=None):
    params = _params(len(grid))
    if exchange is None:
        out = pl.pallas_call(body, name=name, grid=grid, in_specs=in_specs, out_specs=out_specs, out_shape=out_shape,
                             scratch_shapes=list(scratch_shapes), compiler_params=params)(*args)
        return out, None
    n_in, n_out, n_scr = len(in_specs), len(out_specs), len(scratch_shapes)
    n_xin, n_xout = len(exchange.operands), len(exchange.out_shape)
    assert len(steps) == len(exchange.phases)

    def hosting(*refs):
        cuts = [n_in, n_xin, n_out, n_xout, n_scr]
        parts, at = [], 0
        for size in cuts:
            parts.append(refs[at:at + size])
            at += size
        ins, xins, outs, xouts, scr = parts
        sems = refs[at:]
        step = 0
        for axis, size in enumerate(grid):
            step = step * size + pl.program_id(axis)
        pl.when(step == steps[0])(lambda: exchange.phases[0](xins, xouts, sems))
        body(*ins, *outs, *scr)
        for at_step, phase in zip(steps[1:], exchange.phases[1:]):
            pl.when(step == at_step)(functools.partial(phase, xins, xouts, sems))

    out = pl.pallas_call(
        hosting, name=name, grid=grid,
        in_specs=list(in_specs) + [_ANY] * n_xin, out_specs=list(out_specs) + [_ANY] * n_xout,
        out_shape=list(out_shape) + list(exchange.out_shape),
        scratch_shapes=list(scratch_shapes) + list(exchange.scratch), compiler_params=params,
    )(*args, *exchange.operands)
    return out[:n_out], out[n_out:]


def _pair_exchange(grads, name):
    n = len(grads)

    def body(*refs):
        ins, outs = refs[:n], refs[n:2 * n]
        send_sems, recv_sems = refs[2 * n:]
        x, y, c = _mesh_pos()
        cps = []
        for w in range(n):
            hr = grads[w].shape[1] // 2
            cp = _remote(ins[w].at[:, pl.ds((1 - c) * hr, hr), :], outs[w], send_sems.at[w], recv_sems.at[w], (x, y, 1 - c))
            cp.start()
            cps.append(cp)
        for cp in cps:
            cp.wait()

    return pl.pallas_call(
        body, name=name,
        in_specs=[_ANY] * n, out_specs=[_ANY] * n,
        out_shape=[jax.ShapeDtypeStruct((g.shape[0], g.shape[1] // 2, g.shape[2]), g.dtype) for g in grads],
        scratch_shapes=[pltpu.SemaphoreType.DMA((n,)), pltpu.SemaphoreType.DMA((n,))],
    )(*grads)


def _pair_sum(g, a, pos, name):
    nb, r, c = g.shape
    hr = r // 2

    def body(pos_ref, g_ref, a_ref, o_ref):
        o_ref[...] = (g_ref[...] + a_ref[...]).astype(BF16)

    return pl.pallas_call(
        body, name=name,
        grid_spec=pltpu.PrefetchScalarGridSpec(
            num_scalar_prefetch=1, grid=(nb,),
            in_specs=[pl.BlockSpec((1, hr, c), lambda k, pos: (k ^ pos[0], pos[1], 0)),
                      pl.BlockSpec((1, hr, c), lambda k, pos: (k ^ pos[0], 0, 0))],
            out_specs=pl.BlockSpec((1, hr, c), lambda k, pos: (k, 0, 0))),
        out_shape=jax.ShapeDtypeStruct((nb, hr, c), BF16),
        compiler_params=_params(),
    )(pos, g, a)


def _chip_sum(s, b, pos, name):
    _, hr, c = s.shape

    def body(pos_ref, s_ref, b_ref, o_ref):
        o_ref[...] = (s_ref[0].astype(F32) + b_ref[0].astype(F32)) + (b_ref[1].astype(F32) + b_ref[2].astype(F32))

    return pl.pallas_call(
        body, name=name,
        grid_spec=pltpu.PrefetchScalarGridSpec(
            num_scalar_prefetch=1, grid=(1,),
            in_specs=[pl.BlockSpec((1, hr, c), lambda k, pos: (0, 0, 0)), pl.BlockSpec((N_CHIPS - 1, hr, c), lambda k, pos: (0, 0, 0))],
            out_specs=pl.BlockSpec((hr, c), lambda k, pos: (pos[1], 0))),
        out_shape=jax.ShapeDtypeStruct((2 * hr, c), F32),
        compiler_params=_params(),
    )(pos, s, b)


def _pair_share(grads):
    n = len(grads)

    def body(*refs):
        outs = refs[n:2 * n]
        send_sems, recv_sems = refs[2 * n:]
        x, y, c = _mesh_pos()
        cps = []
        for w in range(n):
            hr = grads[w].shape[0] // 2
            rows = outs[w].at[pl.ds(c * hr, hr), :]
            cp = _remote(rows, rows, send_sems.at[w], recv_sems.at[w], (x, y, 1 - c))
            cp.start()
            cps.append(cp)
        for w, cp in enumerate(cps):
            cp.wait_send()
            hr = grads[w].shape[0] // 2
            other = outs[w].at[pl.ds((1 - c) * hr, hr), :]
            _remote(other, other, send_sems.at[w], recv_sems.at[w], (x, y, 1 - c)).wait_recv()

    return pl.pallas_call(
        body, name="grad_pair_share",
        in_specs=[_ANY] * n, out_specs=[_ANY] * n,
        out_shape=[jax.ShapeDtypeStruct(g.shape, g.dtype) for g in grads],
        input_output_aliases={w: w for w in range(n)},
        scratch_shapes=[pltpu.SemaphoreType.DMA((n,)), pltpu.SemaphoreType.DMA((n,))],
    )(*grads)


def _all_reduce_small(buf):
    n_dev = 8

    def body(in_ref, o_ref, slots, send_sems, recv_sems):
        x, y, c = _mesh_pos()
        me = 4 * x + 2 * y + c
        slots[0] = in_ref[...]
        cps = []
        for q in range(1, n_dev):
            peer = (x ^ (q >> 2), y ^ ((q >> 1) & 1), c ^ (q & 1))
            cp = _remote(in_ref, slots.at[q], send_sems.at[q - 1], recv_sems.at[q - 1], peer)
            cp.start()
            cps.append(cp)
        for cp in cps:
            cp.wait()
        acc = slots[me]
        for d in range(1, n_dev):
            acc = acc + slots[d ^ me]
        o_ref[...] = acc

    return pl.pallas_call(
        body, name="all_reduce_small",
        in_specs=[pl.BlockSpec(memory_space=pltpu.VMEM)], out_specs=pl.BlockSpec(memory_space=pltpu.VMEM),
        out_shape=jax.ShapeDtypeStruct(buf.shape, buf.dtype),
        scratch_shapes=[pltpu.VMEM((n_dev, *buf.shape), buf.dtype), pltpu.SemaphoreType.DMA((n_dev - 1,)),
                        pltpu.SemaphoreType.DMA((n_dev - 1,))],
    )(buf)


def _adamw(w, g, m, v, name):
    r, c = w.shape
    tr = r if r * c * 4 <= (1 << 20) else 64
    bias1 = 1.0 - ADAM_B1 ** ADAM_STEP
    bias2 = 1.0 - ADAM_B2 ** ADAM_STEP

    def body(w_ref, g_ref, m_ref, v_ref, d_ref, mo_ref, vo_ref):
        gg = g_ref[...]
        m2 = ADAM_B1 * m_ref[...] + (1.0 - ADAM_B1) * gg
        v2 = ADAM_B2 * v_ref[...] + (1.0 - ADAM_B2) * (gg * gg)
        mo_ref[...] = m2
        vo_ref[...] = v2
        d_ref[...] = -ADAM_LR * ((m2 / bias1) / (jnp.sqrt(v2 / bias2) + ADAM_EPS) + ADAM_WD * w_ref[...])

    spec = pl.BlockSpec((tr, c), lambda i: (i, 0))
    out = jax.ShapeDtypeStruct((r, c), F32)
    return pl.pallas_call(
        body, name=name, grid=(r // tr,), in_specs=[spec] * 4, out_specs=[spec] * 3, out_shape=[out] * 3,
        compiler_params=_params(),
    )(w, g, m, v)


def _pack(parts):
    flat = jnp.concatenate([parts[n].reshape(-1) for n in _SMALL])
    return flat.reshape(-1, LANES)


def _unpack(buf, like):
    flat = buf.reshape(-1)
    out, at = {}, 0
    for n in _SMALL:
        size = like[n].size
        out[n] = flat[at:at + size].reshape(like[n].shape)
        at += size
    return out


def kernel(x, p, ffn1_norm, ffn1_w_in, ffn1_w_out, mix_norm, w_mix_in, gmlp_v_norm, gmlp_w_s, gmlp_b, w_mix_out, ffn2_norm, ffn2_w_in, ffn2_w_out, ple_norm, ple_w_gate, ple_w_proj, final_norm, loss_target, m_ffn1_norm, m_ffn1_w_in, m_ffn1_w_out, m_mix_norm, m_w_mix_in, m_gmlp_v_norm, m_gmlp_w_s, m_gmlp_b, m_w_mix_out, m_ffn2_norm, m_ffn2_w_in, m_ffn2_w_out, m_ple_norm, m_ple_w_gate, m_ple_w_proj, m_final_norm, v_ffn1_norm, v_ffn1_w_in, v_ffn1_w_out, v_mix_norm, v_w_mix_in, v_gmlp_v_norm, v_gmlp_w_s, v_gmlp_b, v_w_mix_out, v_ffn2_norm, v_ffn2_w_in, v_ffn2_w_out, v_ple_norm, v_ple_w_gate, v_ple_w_proj, v_final_norm):
    args = dict(locals())
    w = {n: args[n] for n in _ALL}
    m = {n: args["m_" + n] for n in _ALL}
    v = {n: args["v_" + n] for n in _ALL}
    xi, yi, ci = _mesh_pos()
    pos = jnp.stack([2 * xi + yi, ci]).astype(jnp.int32)
    shard = {n: w[n][0] for n in _BIG}
    cast = {n: shard[n].astype(BF16) for n in _BIG}
    small = {n: (w[n][0] if w[n].ndim > 2 else w[n].reshape(1, -1)) for n in _SMALL}
    bt = small["gmlp_b"].T
    g_small, pair, from_chips = {}, {}, {}

    def pair_reduce(partials, tag):
        names = list(partials)
        parts = [partials[n].reshape(N_CHIPS, *shard[n].shape) for n in names]
        got = _pair_exchange(parts, "grad_pair_exchange_" + tag)
        for n, g, a in zip(names, parts, got):
            pair[n] = _pair_sum(g, a, pos, "pair_sum_" + n)
        return names

    w1in, w1out = _run_exchange(_WeightGather([cast["ffn1_w_in"], cast["ffn1_w_out"]]), "gather_ffn1")
    w1out = w1out.reshape(D_FF, D_MODEL)
    (h1, gu1), (wmix, wmo) = _ffn_fwd(x[0], small["ffn1_norm"], w1in, w1out, "ffn1_fwd",
                                      _WeightGather([cast["w_mix_in"], cast["w_mix_out"]]))
    wmo = wmo.reshape(D_MODEL, D_MODEL)
    zg, qkv = _mix_in_fwd(h1, small["mix_norm"], wmix)
    gm = _gmlp_fwd(zg, small["gmlp_v_norm"], small["gmlp_w_s"], bt)
    (att, carries), (w2in, w2out, wg, wproj) = _attn_fwd(
        qkv, _WeightGather([cast["ffn2_w_in"], cast["ffn2_w_out"], cast["ple_w_gate"], cast["ple_w_proj"]]))
    w2out = w2out.reshape(D_FF, D_MODEL)
    wg = wg.reshape(D_MODEL, D_MODEL)
    mixed = jnp.concatenate([gm, att], axis=1)
    h2 = _matmul_residual(h1, mixed, wmo, "mix_out_fwd")
    (h3, gu2), _ = _ffn_fwd(h2, small["ffn2_norm"], w2in, w2out, "ffn2_fwd")
    loss_part, g_small["final_norm"], dh4, dgp, dpp, n4, pb = _head(
        h3, p[0, 0], loss_target[0], small["ple_norm"], small["final_norm"], wg, wproj)

    part = {"ple_w_gate": _wgrad_rows(n4, dgp, N_CHIPS, "wgrad_ple_gate"),
            "ple_w_proj": _wgrad_cols(pb, dpp, N_CHIPS, "wgrad_ple_proj")}
    dh3, _, g_small["ple_norm"] = _norm_input_bwd(dh4, dgp, wg.reshape(1, D_MODEL, D_MODEL), h3, small["ple_norm"],
                                                  "ple_bwd")
    (dh2, dgu2, n3, act2, dhh3, g_small["ffn2_norm"]), _ = _ffn_bwd(dh3, h2, small["ffn2_norm"], gu2, w2in, w2out,
                                                                   "ffn2_bwd")
    part["ffn2_w_in"] = _wgrad_cols(n3, dgu2, N_CHIPS, "wgrad_ffn2_in")
    part["ffn2_w_out"] = _wgrad_rows(act2, dhh3, 2, "wgrad_ffn2_out")
    dmixed, dh2b = _matmul_nt_cast(dh2, wmo, "mix_out_bwd")
    part["w_mix_out"] = _wgrad_rows(mixed, dh2b, 2, "wgrad_mix_out")
    group = pair_reduce(part, "late")
    dzg, g_small["gmlp_w_s"], dbt, g_small["gmlp_v_norm"] = _gmlp_bwd(zg, dmixed, small["gmlp_v_norm"],
                                                                      small["gmlp_w_s"], bt)
    g_small["gmlp_b"] = dbt.T
    (dq, dk, dv), got = _attn_bwd(qkv, dmixed, carries, _ChipExchange([pair[n] for n in group]))
    from_chips.update(zip(group, got))

    dzmix = jnp.concatenate([dzg, dq, dk, dv], axis=1)
    dh1, n2, g_small["mix_norm"] = _norm_input_bwd(dh2, dzmix, wmix, h1, small["mix_norm"], "mix_in_bwd")
    group = pair_reduce({"w_mix_in": _wgrad_cols(n2, dzmix, N_CHIPS, "wgrad_mix_in")}, "mix")
    (dx, dgu1, n1, act1, dhh1, g_small["ffn1_norm"]), got = _ffn_bwd(
        dh1, x[0], small["ffn1_norm"], gu1, w1in, w1out, "ffn1_bwd", _ChipExchange([pair[n] for n in group]))
    from_chips.update(zip(group, got))

    group = pair_reduce({"ffn1_w_in": _wgrad_cols(n1, dgu1, N_CHIPS, "wgrad_ffn1_in"),
                         "ffn1_w_out": _wgrad_rows(act1, dhh1, 2, "wgrad_ffn1_out")}, "early")
    from_chips.update(zip(group, _run_exchange(_ChipExchange([pair[n] for n in group]), "grad_chip_exchange_early")))

    halves = [_chip_sum(pair[n], from_chips[n], pos, "chip_sum_" + n) for n in _BIG]
    grads = dict(zip(_BIG, _pair_share(halves)))

    delta, new_m, new_v = {}, {}, {}
    for n in _BIG:
        d2, m2, v2 = _adamw(shard[n], grads[n], m[n][0], v[n][0], "adamw_" + n)
        grads[n], delta[n], new_m[n], new_v[n] = grads[n][None], d2[None], m2[None], v2[None]

    packed = _pack(g_small)
    summed = _all_reduce_small(jnp.concatenate([packed, jnp.broadcast_to(loss_part, (8, LANES))], axis=0))
    g_packed, loss = summed[:packed.shape[0]], summed[packed.shape[0], 0]
    d_p, m_p, v_p = _adamw(_pack(w), g_packed, _pack(m), _pack(v), "adamw_small")
    for out, buf in ((grads, g_packed), (delta, d_p), (new_m, m_p), (new_v, v_p)):
        out.update(_unpack(buf, w))

    return (loss, dx[None], *[grads[n] for n in _ALL], *[delta[n] for n in _ALL], *[new_m[n] for n in _ALL],
            *[new_v[n] for n in _ALL])
```

```python
import functools
import math

import jax
import jax.numpy as jnp
from jax import lax
from jax.experimental import pallas as pl
from jax.experimental.pallas import tpu as pltpu

F32, BF16 = jnp.float32, jnp.bfloat16

D_MODEL = 1024
D_FF = 2816
FF_BLOCK = 2 * D_FF // 4
PLE_DIM = 256
CHUNK = 128
GM_HEADS = 4
GM_WIDTH = 512
SB_HEAD_DIM = 64
SB_WIDTH = 512
MIX_IN_WIDTH = 2 * GM_WIDTH + 3 * SB_WIDTH
MIX_BLOCK = MIX_IN_WIDTH // 4
EPS = 1e-6
N_CHIPS = 4
LANES = 128
ATT_BLOCK = 128
ATT_Q = 512
VMEM_LIMIT = 56 * 1024 * 1024

ADAM_LR, ADAM_B1, ADAM_B2, ADAM_EPS, ADAM_WD, ADAM_STEP = 0.001, 0.9, 0.999, 1e-08, 0.01, 10


def _dot(a, b):
    return jnp.dot(a, b, preferred_element_type=F32)


def _dot_nt(a, b):
    return lax.dot_general(a, b, (((1,), (1,)), ((), ())), preferred_element_type=F32)


def _dot_tn(a, b):
    return lax.dot_general(a, b, (((0,), (0,)), ((), ())), preferred_element_type=F32)


def _resident(shape):
    nd = len(shape)
    return pl.BlockSpec(shape, lambda *_: (0,) * nd, pipeline_mode=pl.Buffered(1))


def _rows(tm, width):
    return pl.BlockSpec((tm, width), lambda i: (i, 0))


def _params(n_axes=1):
    return pltpu.CompilerParams(dimension_semantics=("arbitrary",) * n_axes, vmem_limit_bytes=VMEM_LIMIT)


def _rstd(h):
    return lax.rsqrt(jnp.mean(h * h, axis=-1, keepdims=True) + EPS)


def _rms_bwd(dy, h, r, g):
    dyg = dy * g
    dh = r * dyg - h * (r * r * r) * jnp.mean(dyg * h, axis=-1, keepdims=True)
    return dh, dy * h * r


def _gelu(x):
    return 0.5 * x * (1.0 + lax.erf(x * (2.0 ** -0.5)))


def _gelu_grad(x):
    return 0.5 * (1.0 + lax.erf(x * (2.0 ** -0.5))) + x * jnp.exp(-0.5 * x * x) * ((2.0 * jnp.pi) ** -0.5)


def _token_tile(t):
    return min(256, t)


def _ffn_fwd(h, g, win, wout, name, exchange=None):
    t = h.shape[0]
    tm = _token_tile(t)

    def body(h_ref, g_ref, win_ref, wout_ref, ho_ref, gu_ref):
        hh = h_ref[...]
        n = (hh * _rstd(hh) * g_ref[...]).astype(BF16)
        acc = jnp.zeros((tm, D_MODEL), F32)
        for jb in range(2):
            gate = _dot(n, win_ref[jb])
            up = _dot(n, win_ref[2 + jb])
            gu_ref[:, jb * FF_BLOCK:(jb + 1) * FF_BLOCK] = gate.astype(BF16)
            gu_ref[:, D_FF + jb * FF_BLOCK:D_FF + (jb + 1) * FF_BLOCK] = up.astype(BF16)
            act = (gate * jax.nn.sigmoid(gate) * up).astype(BF16)
            acc = acc + _dot(act, wout_ref[jb * FF_BLOCK:(jb + 1) * FF_BLOCK, :])
        ho_ref[...] = hh + 0.5 * acc

    n = t // tm
    return _call(
        body, (h, g, win, wout), name=name, grid=(n,),
        in_specs=[_rows(tm, D_MODEL), _resident((1, D_MODEL)), _resident(win.shape), _resident(wout.shape)],
        out_specs=[_rows(tm, D_MODEL), _rows(tm, 2 * D_FF)],
        out_shape=[jax.ShapeDtypeStruct((t, D_MODEL), F32), jax.ShapeDtypeStruct((t, 2 * D_FF), BF16)],
        exchange=exchange, steps=(0, (2 * n) // 3, n - 1))


def _ffn_bwd(dho, h, g, gu, win, wout, name, exchange=None):
    t = h.shape[0]
    tm = _token_tile(t)

    def body(dho_ref, h_ref, g_ref, gu_ref, win_ref, wout_ref, dh_ref, dgu_ref, n_ref, act_ref, dhh_ref, dg_ref):
        i = pl.program_id(0)
        hh = h_ref[...]
        gg = g_ref[...]
        r = _rstd(hh)
        n_ref[...] = (hh * r * gg).astype(BF16)
        dho = dho_ref[...]
        dhh = (0.5 * dho).astype(BF16)
        dhh_ref[...] = dhh
        dn = jnp.zeros((tm, D_MODEL), F32)
        for jb in range(2):
            cg = slice(jb * FF_BLOCK, (jb + 1) * FF_BLOCK)
            cu = slice(D_FF + jb * FF_BLOCK, D_FF + (jb + 1) * FF_BLOCK)
            dact = _dot_nt(dhh, wout_ref[cg, :])
            gate = gu_ref[:, cg].astype(F32)
            up = gu_ref[:, cu].astype(F32)
            sg = jax.nn.sigmoid(gate)
            silu = gate * sg
            act_ref[:, cg] = (silu * up).astype(BF16)
            dgate = (dact * up * (sg * (1.0 + gate * (1.0 - sg)))).astype(BF16)
            dup = (dact * silu).astype(BF16)
            dgu_ref[:, cg] = dgate
            dgu_ref[:, cu] = dup
            dn = dn + _dot_nt(dgate, win_ref[jb]) + _dot_nt(dup, win_ref[2 + jb])
        dh, dg_rows = _rms_bwd(dn, hh, r, gg)
        dh_ref[...] = dho + dh

        @pl.when(i == 0)
        def _():
            dg_ref[...] = jnp.zeros_like(dg_ref)

        dg_ref[...] += jnp.sum(dg_rows, axis=0, keepdims=True)

    n = t // tm
    return _call(
        body, (dho, h, g, gu, win, wout), name=name, grid=(n,),
        in_specs=[_rows(tm, D_MODEL), _rows(tm, D_MODEL), _resident((1, D_MODEL)), _rows(tm, 2 * D_FF),
                  _resident(win.shape), _resident(wout.shape)],
        out_specs=[_rows(tm, D_MODEL), _rows(tm, 2 * D_FF), _rows(tm, D_MODEL), _rows(tm, D_FF), _rows(tm, D_MODEL),
                   pl.BlockSpec((1, D_MODEL), lambda i: (0, 0))],
        out_shape=[jax.ShapeDtypeStruct((t, D_MODEL), F32), jax.ShapeDtypeStruct((t, 2 * D_FF), BF16),
                   jax.ShapeDtypeStruct((t, D_MODEL), BF16), jax.ShapeDtypeStruct((t, D_FF), BF16),
                   jax.ShapeDtypeStruct((t, D_MODEL), BF16), jax.ShapeDtypeStruct((1, D_MODEL), F32)],
        exchange=exchange, steps=(0, n - 1))


def _wgrad(a, b, out_shape, out_block, out_index, a_width, b_width, grid_ij, name, exchange=None):
    t = a.shape[0]
    tk = min(2048, t)

    def body(a_ref, b_ref, o_ref):
        k = pl.program_id(2)
        prod = _dot_tn(a_ref[...], b_ref[...]).reshape(o_ref.shape)

        @pl.when(k == 0)
        def _():
            o_ref[...] = prod

        @pl.when(k > 0)
        def _():
            o_ref[...] += prod

    grid = (*grid_ij, t // tk)
    (out,), got = _call(
        body, (a, b), name=name, grid=grid,
        in_specs=[pl.BlockSpec((tk, a_width), lambda i, j, k: (k, i)), pl.BlockSpec((tk, b_width), lambda i, j, k: (k, j))],
        out_specs=[pl.BlockSpec(out_block, lambda i, j, k: out_index(i, j))],
        out_shape=[jax.ShapeDtypeStruct(out_shape, F32)],
        exchange=exchange, steps=(0, grid[0] * grid[1] * grid[2] - 1))
    return out if exchange is None else (out, got)


def _wgrad_cols(a, b, n_blocks, name, exchange=None):
    ka, nb = a.shape[1], b.shape[1] // n_blocks
    return _wgrad(a, b, (n_blocks, ka, nb), (1, ka, nb), lambda i, j: (j, 0, 0), ka, nb, (1, n_blocks), name, exchange)


def _wgrad_rows(a, b, n_blocks, name):
    ka, nb = a.shape[1] // n_blocks, b.shape[1]
    return _wgrad(a, b, (a.shape[1], nb), (ka, nb), lambda i, j: (i, 0), ka, nb, (n_blocks, 1), name)


def _mix_in_fwd(h, g, wmix):
    t = h.shape[0]
    tm = _token_tile(t)
    gw2 = 2 * GM_WIDTH

    def body(h_ref, g_ref, w_ref, zg_ref, qkv_ref):
        hh = h_ref[...]
        n = (hh * _rstd(hh) * g_ref[...]).astype(BF16)
        for b in range(N_CHIPS):
            z = _dot(n, w_ref[b])
            lo, hi = b * MIX_BLOCK, (b + 1) * MIX_BLOCK
            if hi <= gw2:
                zg_ref[:, lo:hi] = z
            elif lo >= gw2:
                qkv_ref[:, lo - gw2:hi - gw2] = z.astype(BF16)
            else:
                zg_ref[:, lo:gw2] = z[:, :gw2 - lo]
                qkv_ref[:, 0:hi - gw2] = z[:, gw2 - lo:].astype(BF16)

    return pl.pallas_call(
        body, name="mix_in_fwd", grid=(t // tm,),
        in_specs=[_rows(tm, D_MODEL), _resident((1, D_MODEL)), _resident(wmix.shape)],
        out_specs=[_rows(tm, gw2), _rows(tm, 3 * SB_WIDTH)],
        out_shape=[jax.ShapeDtypeStruct((t, gw2), F32), jax.ShapeDtypeStruct((t, 3 * SB_WIDTH), BF16)],
        compiler_params=_params(),
    )(h, g, wmix)


def _causal_chunk_mask():
    row = lax.broadcasted_iota(jnp.int32, (CHUNK, CHUNK), 0)
    col = lax.broadcasted_iota(jnp.int32, (CHUNK, CHUNK), 1)
    return row >= col


def _gmlp_tile(t):
    return min(512, t)


def _gmlp_fwd(zg, gv, ws, bt):
    t = zg.shape[0]
    tm = _gmlp_tile(t)

    def body(zg_ref, gv_ref, ws_ref, bt_ref, o_ref):
        u = _gelu(zg_ref[:, :GM_WIDTH])
        v = _gelu(zg_ref[:, GM_WIDTH:])
        vn = (v * _rstd(v) * gv_ref[...]).astype(BF16)
        mask = _causal_chunk_mask()
        for hd in range(GM_HEADS):
            wm = jnp.where(mask, ws_ref[hd], 0.0).astype(BF16)
            cols = slice(hd * CHUNK, (hd + 1) * CHUNK)
            for c in range(tm // CHUNK):
                rows = slice(c * CHUNK, (c + 1) * CHUNK)
                sv = _dot(wm, vn[rows, cols]) + bt_ref[:, hd:hd + 1]
                o_ref[rows, cols] = (u[rows, cols] * sv).astype(BF16)

    return pl.pallas_call(
        body, name="gmlp_fwd", grid=(t // tm,),
        in_specs=[_rows(tm, 2 * GM_WIDTH), _resident((1, GM_WIDTH)), _resident(ws.shape), _resident(bt.shape)],
        out_specs=_rows(tm, GM_WIDTH),
        out_shape=jax.ShapeDtypeStruct((t, GM_WIDTH), BF16),
        compiler_params=_params(),
    )(zg, gv, ws, bt)


def _gmlp_bwd(zg, dmixed, gv, ws, bt):
    t = zg.shape[0]
    tm = _gmlp_tile(t)

    def body(zg_ref, dgm_ref, gv_ref, ws_ref, bt_ref, dzg_ref, dws_ref, dbt_ref, dgv_ref):
        i = pl.program_id(0)

        @pl.when(i == 0)
        def _():
            dws_ref[...] = jnp.zeros_like(dws_ref)
            dbt_ref[...] = jnp.zeros_like(dbt_ref)
            dgv_ref[...] = jnp.zeros_like(dgv_ref)

        zu = zg_ref[:, :GM_WIDTH]
        zv = zg_ref[:, GM_WIDTH:]
        u = _gelu(zu)
        v = _gelu(zv)
        r = _rstd(v)
        gvv = gv_ref[...]
        vn = (v * r * gvv).astype(BF16)
        dgm = dgm_ref[...].astype(F32)
        dsv = (dgm * u).astype(BF16)
        mask = _causal_chunk_mask()
        du_cols, dvn_cols = [], []
        for hd in range(GM_HEADS):
            wm = jnp.where(mask, ws_ref[hd], 0.0).astype(BF16)
            cols = slice(hd * CHUNK, (hd + 1) * CHUNK)
            dw = jnp.zeros((CHUNK, CHUNK), F32)
            db = jnp.zeros((CHUNK, 1), F32)
            du_rows, dvn_rows = [], []
            for c in range(tm // CHUNK):
                rows = slice(c * CHUNK, (c + 1) * CHUNK)
                sv = _dot(wm, vn[rows, cols]) + bt_ref[:, hd:hd + 1]
                du_rows.append(dgm[rows, cols] * sv)
                dvn_rows.append(_dot_tn(wm, dsv[rows, cols]))
                dw = dw + _dot_nt(dsv[rows, cols], vn[rows, cols])
                db = db + jnp.sum(dsv[rows, cols].astype(F32), axis=1, keepdims=True)
            dws_ref[hd] += jnp.where(mask, dw, 0.0)
            dbt_ref[:, hd:hd + 1] += db
            du_cols.append(jnp.concatenate(du_rows, axis=0))
            dvn_cols.append(jnp.concatenate(dvn_rows, axis=0))
        du = jnp.concatenate(du_cols, axis=1)
        dvn = jnp.concatenate(dvn_cols, axis=1)
        dv, dgv_rows = _rms_bwd(dvn, v, r, gvv)
        dgv_ref[...] += jnp.sum(dgv_rows, axis=0, keepdims=True)
        dzg_ref[:, :GM_WIDTH] = (du * _gelu_grad(zu)).astype(BF16)
        dzg_ref[:, GM_WIDTH:] = (dv * _gelu_grad(zv)).astype(BF16)

    const = lambda nd: (lambda i: (0,) * nd)
    return pl.pallas_call(
        body, name="gmlp_bwd", grid=(t // tm,),
        in_specs=[_rows(tm, 2 * GM_WIDTH), _rows(tm, GM_WIDTH), _resident((1, GM_WIDTH)), _resident(ws.shape),
                  _resident(bt.shape)],
        out_specs=[_rows(tm, 2 * GM_WIDTH), pl.BlockSpec(ws.shape, const(3)), pl.BlockSpec(bt.shape, const(2)),
                   pl.BlockSpec((1, GM_WIDTH), const(2))],
        out_shape=[jax.ShapeDtypeStruct((t, 2 * GM_WIDTH), BF16), jax.ShapeDtypeStruct(ws.shape, F32),
                   jax.ShapeDtypeStruct(bt.shape, F32), jax.ShapeDtypeStruct((1, GM_WIDTH), F32)],
        compiler_params=_params(),
    )(zg, dmixed, gv, ws, bt)


def _att_masks():
    tb = ATT_BLOCK
    lane = lax.broadcasted_iota(jnp.int32, (1, LANES), 1)
    rj = lax.broadcasted_iota(jnp.int32, (2 * tb, 2 * tb), 0)
    cs = lax.broadcasted_iota(jnp.int32, (2 * tb, 2 * tb), 1)
    same_head = ((rj < tb) & (cs < tb)) | ((rj >= tb) & (cs >= tb))
    suffix = jnp.where(same_head & (rj >= cs), 1.0, 0.0).astype(BF16)
    prefix = jnp.where(same_head & (rj <= cs), 1.0, 0.0).astype(BF16)
    left = lax.broadcasted_iota(jnp.int32, (1, 2 * tb), 1) < tb
    tq = lax.broadcasted_iota(jnp.int32, (ATT_Q, 4 * tb), 0)
    ts = lax.broadcasted_iota(jnp.int32, (ATT_Q, 4 * tb), 1)
    key = jnp.where(ts < 2 * tb, ts & (tb - 1), (ts & (tb - 1)) + tb)
    return lane, suffix, prefix, left, key, tq


def _att_fill(k_ref, v_ref, kcat, vcat, n_blocks, lane):
    tb = ATT_BLOCK
    first = lane < SB_HEAD_DIM

    def fill(jb, carry):
        rows = pl.ds(pl.multiple_of(jb * tb, tb), tb)
        top = pl.ds(pl.multiple_of(jb * 2 * tb, tb), tb)
        bot = pl.ds(pl.multiple_of(jb * 2 * tb + tb, tb), tb)
        kb = k_ref[rows, :]
        vb = v_ref[rows, :]
        zero = jnp.zeros_like(kb)
        kcat[top, :] = jnp.where(first, kb, zero)
        kcat[bot, :] = jnp.where(first, zero, kb)
        vcat[top, :] = jnp.where(first, vb, zero)
        vcat[bot, :] = jnp.where(first, zero, vb)
        return carry

    lax.fori_loop(0, n_blocks, fill, 0)


def _block_sums(x, m):
    return _dot(x.astype(BF16), m)


def _softplus(z):
    return jnp.maximum(z, 0.0) + jnp.log(1.0 + jnp.exp2(jnp.abs(z) * -math.log2(math.e)))


def _scaled_queries(q_ref):
    return (q_ref[...].astype(F32) * (SB_HEAD_DIM ** -0.5)).astype(BF16)


def _att_specs(t):
    n_pairs = SB_WIDTH // LANES
    q_spec = pl.BlockSpec((ATT_Q, LANES), lambda p, i: (i, p))
    k_spec = pl.BlockSpec((t, LANES), lambda p, i: (0, n_pairs + p))
    v_spec = pl.BlockSpec((t, LANES), lambda p, i: (0, 2 * n_pairs + p))
    return n_pairs, q_spec, k_spec, v_spec


def _attn_fwd(qkv, exchange=None):
    t = qkv.shape[0]
    tb = ATT_BLOCK
    nkb = t // tb
    assert 2 * nkb <= LANES and t % ATT_Q == 0 and ATT_Q == 4 * tb
    n_pairs, q_spec, k_spec, v_spec = _att_specs(t)

    def body(q_ref, k_ref, v_ref, o_ref, ct_ref, kcat, vcat, acc, carry, z0, r0, z1, r1):
        i = pl.program_id(1)
        lane, suffix, _, left, key, tq = _att_masks()

        @pl.when(i == 0)
        def _():
            _att_fill(k_ref, v_ref, kcat, vcat, nkb, lane)

        q = _scaled_queries(q_ref)
        acc[...] = jnp.zeros_like(acc)
        carry[...] = jnp.zeros_like(carry)
        ct_ref[0] = jnp.zeros((ATT_Q, LANES), F32)

        def key_rows(m):
            return pl.ds(pl.multiple_of(m * 4 * tb, 4 * tb), 4 * tb)

        def scores(m, zb, rb, causal=None):
            z = _dot_nt(q, kcat[key_rows(m), :])
            zb[...] = z
            sp = _softplus(z)
            if causal is not None:
                sp = jnp.where(causal, sp, 0.0)
            for g in (1, 0):
                cols = slice(g * 2 * tb, (g + 1) * 2 * tb)
                rb[:, cols] = _block_sums(sp[:, cols], suffix)

        def weigh(m, zb, rb, causal=None):
            probs = [None, None]
            for g in (1, 0):
                cols = slice(g * 2 * tb, (g + 1) * 2 * tb)
                j = 2 * m + g
                r = rb[:, cols]
                c = carry[...]
                ct_ref[0] = jnp.where(lane == j, c[:, :tb], jnp.where(lane == nkb + j, c[:, tb:], ct_ref[0]))
                a = jnp.exp(zb[:, cols] - (r + c))
                if causal is not None:
                    a = jnp.where(causal[:, cols], a, 0.0)
                probs[g] = a.astype(BF16)
                carry[...] = c + jnp.where(left, r[:, 0:1], r[:, tb:tb + 1])
            acc[...] += _dot(jnp.concatenate(probs, axis=1), vcat[key_rows(m), :])

        for e in (1, 0):
            causal = key + e * 2 * tb < tq
            scores(2 * i + e, z0, r0, causal)
            weigh(2 * i + e, z0, r0, causal)

        @pl.when(i > 0)
        def _():
            scores(2 * i - 1, z1, r1)

            def loop(k, c):
                u = i - 1 - k
                scores(2 * u, z0, r0)
                weigh(2 * u + 1, z1, r1)
                scores(jnp.maximum(2 * u - 1, 0), z1, r1)
                weigh(2 * u, z0, r0)
                return c

            lax.fori_loop(0, i, loop, 0)

        o_ref[...] = acc[...].astype(BF16)

    tile = pltpu.VMEM((ATT_Q, 4 * tb), F32)

    nq = t // ATT_Q
    return _call(
        body, (qkv, qkv, qkv), name="attn_fwd", grid=(n_pairs, nq),
        in_specs=[q_spec, k_spec, v_spec],
        out_specs=[pl.BlockSpec((ATT_Q, LANES), lambda p, i: (i, p)), pl.BlockSpec((1, ATT_Q, LANES), lambda p, i: (p, i, 0))],
        out_shape=[jax.ShapeDtypeStruct((t, SB_WIDTH), BF16), jax.ShapeDtypeStruct((n_pairs, t, LANES), F32)],
        scratch_shapes=[pltpu.VMEM((2 * t, LANES), BF16), pltpu.VMEM((2 * t, LANES), BF16),
                        pltpu.VMEM((ATT_Q, LANES), F32), pltpu.VMEM((ATT_Q, 2 * tb), F32), tile, tile, tile, tile],
        exchange=exchange, steps=(0, (n_pairs - 1) * nq - 1, n_pairs * nq - 1))


def _attn_bwd(qkv, dmixed, carries, exchange=None):
    t = qkv.shape[0]
    tb = ATT_BLOCK
    nkb = t // tb
    nq = t // ATT_Q
    scale = SB_HEAD_DIM ** -0.5
    n_pairs, q_spec, k_spec, v_spec = _att_specs(t)
    gm_blocks = GM_WIDTH // LANES

    def body(q_ref, k_ref, v_ref, do_ref, ct_ref, dq_ref, dk_ref, dv_ref, kcat, vcat, dkacc, dvacc, dqacc, carry,
             z0, r0, s0, a0, z1, r1, s1, a1):
        i = pl.program_id(1)
        lane, suffix, prefix, left, key, tq = _att_masks()
        first = lane < SB_HEAD_DIM

        @pl.when(i == 0)
        def _():
            _att_fill(k_ref, v_ref, kcat, vcat, nkb, lane)
            dkacc[...] = jnp.zeros_like(dkacc)
            dvacc[...] = jnp.zeros_like(dvacc)

        q = _scaled_queries(q_ref)
        do = do_ref[...]
        dqacc[...] = jnp.zeros_like(dqacc)
        carry[...] = jnp.zeros_like(carry)

        def key_rows(m):
            return pl.ds(pl.multiple_of(m * 4 * tb, 4 * tb), 4 * tb)

        def front(m, bufs, causal=None):
            zb, rb, sb, ab = bufs
            z = _dot_nt(q, kcat[key_rows(m), :])
            zb[...] = z
            sp = _softplus(z)
            sb[...] = jnp.exp(z - sp)
            if causal is not None:
                sp = jnp.where(causal, sp, 0.0)
            for g in (0, 1):
                cols = slice(g * 2 * tb, (g + 1) * 2 * tb)
                rb[:, cols] = _block_sums(sp[:, cols], suffix)
            ab[...] = _dot_nt(do, vcat[key_rows(m), :])

        def back(m, bufs, causal=None):
            zb, rb, sb, ab = bufs
            dzs, probs = [None, None], [None, None]
            for g in (0, 1):
                cols = slice(g * 2 * tb, (g + 1) * 2 * tb)
                j = 2 * m + g
                ct = ct_ref[0]
                ca = jnp.sum(jnp.where(lane == j, ct, 0.0), axis=1, keepdims=True)
                cb = jnp.sum(jnp.where(lane == nkb + j, ct, 0.0), axis=1, keepdims=True)
                a = jnp.exp(zb[:, cols] - (rb[:, cols] + jnp.where(left, ca, cb)))
                if causal is not None:
                    a = jnp.where(causal[:, cols], a, 0.0)
                de = ab[:, cols] * a
                cl = _block_sums(de, prefix)
                pre = carry[...]
                dz = de - sb[:, cols] * (cl + pre)
                if causal is not None:
                    dz = jnp.where(causal[:, cols], dz, 0.0)
                carry[...] = pre + jnp.where(left, cl[:, tb - 1:tb], cl[:, 2 * tb - 1:2 * tb])
                dzs[g] = dz.astype(BF16)
                probs[g] = a.astype(BF16)
            dzb = jnp.concatenate(dzs, axis=1)
            dqacc[...] += _dot(dzb, kcat[key_rows(m), :])
            dkc = _dot_tn(dzb, q)
            dvc = _dot_tn(jnp.concatenate(probs, axis=1), do)
            out_rows = pl.ds(pl.multiple_of(m * 2 * tb, 2 * tb), 2 * tb)
            pick = lambda x: jnp.concatenate([jnp.where(first, x[0:tb], x[tb:2 * tb]),
                                              jnp.where(first, x[2 * tb:3 * tb], x[3 * tb:4 * tb])], axis=0)
            dkacc[out_rows, :] += pick(dkc)
            dvacc[out_rows, :] += pick(dvc)

        b0, b1 = (z0, r0, s0, a0), (z1, r1, s1, a1)

        @pl.when(i > 0)
        def _():
            front(0, b0)

            def loop(u, c):
                front(2 * u + 1, b1)
                back(2 * u, b0)
                front(jnp.minimum(2 * u + 2, 2 * i - 1), b0)
                back(2 * u + 1, b1)
                return c

            lax.fori_loop(0, i, loop, 0)

        for e in (0, 1):
            causal = key + e * 2 * tb < tq
            front(2 * i + e, b0, causal)
            back(2 * i + e, b0, causal)

        dq_ref[...] = (dqacc[...] * scale).astype(BF16)

        @pl.when(i == nq - 1)
        def _():
            dk_ref[...] = dkacc[...].astype(BF16)
            dv_ref[...] = dvacc[...].astype(BF16)

    col = pl.BlockSpec((t, LANES), lambda p, i: (0, p))
    out = jax.ShapeDtypeStruct((t, SB_WIDTH), BF16)
    tile = pltpu.VMEM((ATT_Q, 4 * tb), F32)
    return _call(
        body, (qkv, qkv, qkv, dmixed, carries), name="attn_bwd", grid=(n_pairs, nq),
        in_specs=[q_spec, k_spec, v_spec, pl.BlockSpec((ATT_Q, LANES), lambda p, i: (i, gm_blocks + p)),
                  pl.BlockSpec((1, ATT_Q, LANES), lambda p, i: (p, i, 0))],
        out_specs=[pl.BlockSpec((ATT_Q, LANES), lambda p, i: (i, p)), col, col],
        out_shape=[out, out, out],
        scratch_shapes=[pltpu.VMEM((2 * t, LANES), BF16), pltpu.VMEM((2 * t, LANES), BF16),
                        pltpu.VMEM((t, LANES), F32), pltpu.VMEM((t, LANES), F32),
                        pltpu.VMEM((ATT_Q, LANES), F32), pltpu.VMEM((ATT_Q, 2 * tb), F32)] + [tile] * 8,
        exchange=exchange, steps=(0, n_pairs * nq - 1))


def _matmul_residual(res, a, w, name):
    t = a.shape[0]
    tm = _token_tile(t)

    def body(res_ref, a_ref, w_ref, o_ref):
        o_ref[...] = res_ref[...] + _dot(a_ref[...], w_ref[...])

    return pl.pallas_call(
        body, name=name, grid=(t // tm,),
        in_specs=[_rows(tm, res.shape[1]), _rows(tm, a.shape[1]), _resident(w.shape)],
        out_specs=_rows(tm, res.shape[1]),
        out_shape=jax.ShapeDtypeStruct(res.shape, F32),
        compiler_params=_params(),
    )(res, a, w)


def _matmul_nt_cast(dy, w, name):
    t = dy.shape[0]
    tm = _token_tile(t)

    def body(dy_ref, w_ref, o_ref, dyb_ref):
        dyb = dy_ref[...].astype(BF16)
        dyb_ref[...] = dyb
        o_ref[...] = _dot_nt(dyb, w_ref[...]).astype(BF16)

    return pl.pallas_call(
        body, name=name, grid=(t // tm,),
        in_specs=[_rows(tm, dy.shape[1]), _resident(w.shape)],
        out_specs=[_rows(tm, w.shape[0]), _rows(tm, dy.shape[1])],
        out_shape=[jax.ShapeDtypeStruct((t, w.shape[0]), BF16), jax.ShapeDtypeStruct(dy.shape, BF16)],
        compiler_params=_params(),
    )(dy, w)


def _norm_input_bwd(dres, dz, w, h, g, name):
    t = h.shape[0]
    tm = _token_tile(t)
    nb, _, width = w.shape

    def body(dres_ref, dz_ref, w_ref, h_ref, g_ref, dh_ref, n_ref, dg_ref):
        i = pl.program_id(0)
        hh = h_ref[...]
        gg = g_ref[...]
        r = _rstd(hh)
        n_ref[...] = (hh * r * gg).astype(BF16)
        dn = jnp.zeros((tm, D_MODEL), F32)
        for b in range(nb):
            dn = dn + _dot_nt(dz_ref[:, b * width:(b + 1) * width], w_ref[b])
        dh, dg_rows = _rms_bwd(dn, hh, r, gg)
        dh_ref[...] = dres_ref[...] + dh

        @pl.when(i == 0)
        def _():
            dg_ref[...] = jnp.zeros_like(dg_ref)

        dg_ref[...] += jnp.sum(dg_rows, axis=0, keepdims=True)

    return pl.pallas_call(
        body, name=name, grid=(t // tm,),
        in_specs=[_rows(tm, D_MODEL), _rows(tm, nb * width), _resident(w.shape), _rows(tm, D_MODEL),
                  _resident((1, D_MODEL))],
        out_specs=[_rows(tm, D_MODEL), _rows(tm, D_MODEL), pl.BlockSpec((1, D_MODEL), lambda i: (0, 0))],
        out_shape=[jax.ShapeDtypeStruct((t, D_MODEL), F32), jax.ShapeDtypeStruct((t, D_MODEL), BF16),
                   jax.ShapeDtypeStruct((1, D_MODEL), F32)],
        compiler_params=_params(),
    )(dres, dz, w, h, g)


def _head(h, p, target, gple, gfin, wg, wproj):
    t = h.shape[0]
    tm = _token_tile(t)
    pw = D_MODEL // N_CHIPS

    def body(h_ref, p_ref, tgt_ref, gple_ref, gfin_ref, wg_ref, wproj_ref,
             loss_ref, dgf_ref, dh_ref, dgp_ref, dpp_ref, n_ref, pb_ref):
        i = pl.program_id(0)
        hh = h_ref[...]
        n = (hh * _rstd(hh) * gple_ref[...]).astype(BF16)
        n_ref[...] = n
        gate = jax.nn.sigmoid(_dot(n, wg_ref[...]))
        pb = p_ref[...].astype(BF16)
        pb_ref[...] = pb
        pp = jnp.concatenate([_dot(pb, wproj_ref[b]) for b in range(N_CHIPS)], axis=1)
        h4 = hh + gate * pp
        r = _rstd(h4)
        gf = gfin_ref[...]
        err = h4 * r * gf - tgt_ref[...]
        dy = err * (1.0 / D_MODEL)
        dh4, dgf_rows = _rms_bwd(dy, h4, r, gf)
        dh_ref[...] = dh4
        dgp_ref[...] = (dh4 * pp * gate * (1.0 - gate)).astype(BF16)
        dpp_ref[...] = (dh4 * gate).astype(BF16)

        @pl.when(i == 0)
        def _():
            loss_ref[...] = jnp.zeros_like(loss_ref)
            dgf_ref[...] = jnp.zeros_like(dgf_ref)

        loss_ref[...] += (0.5 / D_MODEL) * jnp.sum(err * err)
        dgf_ref[...] += jnp.sum(dgf_rows, axis=0, keepdims=True)

    bf = lambda w: jax.ShapeDtypeStruct((t, w), BF16)
    const = lambda i: (0, 0)
    return pl.pallas_call(
        body, name="head", grid=(t // tm,),
        in_specs=[_rows(tm, D_MODEL), _rows(tm, PLE_DIM), _rows(tm, D_MODEL), _resident((1, D_MODEL)),
                  _resident((1, D_MODEL)), _resident(wg.shape), _resident(wproj.shape)],
        out_specs=[pl.BlockSpec((1, LANES), const), pl.BlockSpec((1, D_MODEL), const), _rows(tm, D_MODEL),
                   _rows(tm, D_MODEL), _rows(tm, D_MODEL), _rows(tm, D_MODEL), _rows(tm, PLE_DIM)],
        out_shape=[jax.ShapeDtypeStruct((1, LANES), F32), jax.ShapeDtypeStruct((1, D_MODEL), F32),
                   jax.ShapeDtypeStruct((t, D_MODEL), F32), bf(D_MODEL), bf(D_MODEL), bf(D_MODEL), bf(PLE_DIM)],
        compiler_params=_params(),
    )(h, p, target, gple, gfin, wg, wproj)


_BIG = ("ffn1_w_in", "ffn1_w_out", "w_mix_in", "w_mix_out", "ffn2_w_in", "ffn2_w_out", "ple_w_gate", "ple_w_proj")
_SMALL = ("ffn1_norm", "mix_norm", "gmlp_v_norm", "gmlp_w_s", "gmlp_b", "ffn2_norm", "ple_norm", "final_norm")
_ALL = ("ffn1_norm", "ffn1_w_in", "ffn1_w_out", "mix_norm", "w_mix_in", "gmlp_v_norm", "gmlp_w_s", "gmlp_b", "w_mix_out",
        "ffn2_norm", "ffn2_w_in", "ffn2_w_out", "ple_norm", "ple_w_gate", "ple_w_proj", "final_norm")
_ANY = pl.BlockSpec(memory_space=pl.ANY)
_MESH = pl.DeviceIdType.MESH


def _mesh_pos():
    return lax.axis_index("x"), lax.axis_index("y"), lax.axis_index("c")


def _other_chips(x, y):
    return [((x, 1 - y), 2 * x + 1 - y), ((1 - x, y), 2 * (1 - x) + y), ((1 - x, 1 - y), 2 * (1 - x) + 1 - y)]


def _remote(src, dst, send_sem, recv_sem, device):
    return pltpu.make_async_remote_copy(src_ref=src, dst_ref=dst, send_sem=send_sem, recv_sem=recv_sem,
                                        device_id=device, device_id_type=_MESH)


class _WeightGather:
    def __init__(self, shards):
        self.shapes = [s.shape for s in shards]
        self.operands = list(shards)
        self.out_shape = [jax.ShapeDtypeStruct((N_CHIPS, *s.shape), s.dtype) for s in shards]
        n = len(shards)
        self.per = 2 * (N_CHIPS - 1)
        self.scratch = [pltpu.SemaphoreType.DMA((self.per * n,)), pltpu.SemaphoreType.DMA((self.per * n,)),
                        pltpu.SemaphoreType.DMA((n,))]
        self.phases = [self.send, self.forward, self.finish]

    def _copies(self, ins, outs, sems):
        send_sems, recv_sems, local_sems = sems
        x, y, c = _mesh_pos()
        sibling = (x, y, 1 - c)
        mine = 2 * x + y
        local, first, landing, passed, arriving = [], [], [], [], []
        for w, shape in enumerate(self.shapes):
            hr = shape[0] // 2
            half = lambda blk, cc, w=w, hr=hr: outs[w].at[blk, pl.ds(cc * hr, hr), :]
            local.append(pltpu.make_async_copy(ins[w], outs[w].at[mine], local_sems.at[w]))
            for k, (chip, blk) in enumerate(_other_chips(x, y)):
                s = self.per * w + k
                first.append(_remote(ins[w].at[pl.ds(c * hr, hr), :], half(mine, c), send_sems.at[s], recv_sems.at[s],
                                     (*chip, c)))
                landing.append(_remote(half(blk, c), half(blk, c), send_sems.at[s], recv_sems.at[s], sibling))
                s = self.per * w + N_CHIPS - 1 + k
                passed.append(_remote(half(blk, c), half(blk, c), send_sems.at[s], recv_sems.at[s], sibling))
                arriving.append(_remote(half(blk, 1 - c), half(blk, 1 - c), send_sems.at[s], recv_sems.at[s], sibling))
        return local, first, landing, passed, arriving

    def send(self, ins, outs, sems):
        local, first, _, _, _ = self._copies(ins, outs, sems)
        for cp in local + first:
            cp.start()

    def forward(self, ins, outs, sems):
        _, _, landing, passed, _ = self._copies(ins, outs, sems)
        for landed, cp in zip(landing, passed):
            landed.wait_recv()
            cp.start()

    def finish(self, ins, outs, sems):
        local, first, _, passed, arriving = self._copies(ins, outs, sems)
        for cp in arriving:
            cp.wait_recv()
        for cp in first + passed:
            cp.wait_send()
        for cp in local:
            cp.wait()


class _ChipExchange:
    def __init__(self, sums):
        n = len(sums)
        self.n = n
        self.per = N_CHIPS - 1
        self.operands = list(sums)
        self.out_shape = [jax.ShapeDtypeStruct((self.per, *s.shape[1:]), s.dtype) for s in sums]
        self.scratch = [pltpu.SemaphoreType.DMA((self.per * n,)), pltpu.SemaphoreType.DMA((self.per * n,))]
        self.phases = [self.send, self.finish]

    def _copies(self, ins, outs, sems):
        send_sems, recv_sems = sems
        x, y, c = _mesh_pos()
        cps = []
        for w in range(self.n):
            for k, (chip, _) in enumerate(_other_chips(x, y)):
                s = self.per * w + k
                cps.append(_remote(ins[w].at[k + 1], outs[w].at[k], send_sems.at[s], recv_sems.at[s], (*chip, c)))
        return cps

    def send(self, ins, outs, sems):
        for cp in self._copies(ins, outs, sems):
            cp.start()

    def finish(self, ins, outs, sems):
        for cp in self._copies(ins, outs, sems):
            cp.wait()


def _run_exchange(ex, name):
    n_in, n_out = len(ex.operands), len(ex.out_shape)

    def body(*refs):
        ins, outs, sems = refs[:n_in], refs[n_in:n_in + n_out], refs[n_in + n_out:]
        for phase in ex.phases:
            phase(ins, outs, sems)

    return pl.pallas_call(body, name=name, in_specs=[_ANY] * n_in, out_specs=[_ANY] * n_out, out_shape=ex.out_shape,
                          scratch_shapes=ex.scratch)(*ex.operands)


def _call(body, args, *, name, grid, in_specs, out_specs, out_shape, scratch_shapes=(), exchange=None, steps=None):
    params = _params(len(grid))
    if exchange is None:
        out = pl.pallas_call(body, name=name, grid=grid, in_specs=in_specs, out_specs=out_specs, out_shape=out_shape,
                             scratch_shapes=list(scratch_shapes), compiler_params=params)(*args)
        return out, None
    n_in, n_out, n_scr = len(in_specs), len(out_specs), len(scratch_shapes)
    n_xin, n_xout = len(exchange.operands), len(exchange.out_shape)
    assert len(steps) == len(exchange.phases)

    def hosting(*refs):
        cuts = [n_in, n_xin, n_out, n_xout, n_scr]
        parts, at = [], 0
        for size in cuts:
            parts.append(refs[at:at + size])
            at += size
        ins, xins, outs, xouts, scr = parts
        sems = refs[at:]
        step = 0
        for axis, size in enumerate(grid):
            step = step * size + pl.program_id(axis)
        pl.when(step == steps[0])(lambda: exchange.phases[0](xins, xouts, sems))
        body(*ins, *outs, *scr)
        for at_step, phase in zip(steps[1:], exchange.phases[1:]):
            pl.when(step == at_step)(functools.partial(phase, xins, xouts, sems))

    out = pl.pallas_call(
        hosting, name=name, grid=grid,
        in_specs=list(in_specs) + [_ANY] * n_xin, out_specs=list(out_specs) + [_ANY] * n_xout,
        out_shape=list(out_shape) + list(exchange.out_shape),
        scratch_shapes=list(scratch_shapes) + list(exchange.scratch), compiler_params=params,
    )(*args, *exchange.operands)
    return out[:n_out], out[n_out:]


def _pair_exchange(grads, name):
    n = len(grads)

    def body(*refs):
        ins, outs = refs[:n], refs[n:2 * n]
        send_sems, recv_sems = refs[2 * n:]
        x, y, c = _mesh_pos()
        cps = []
        for w in range(n):
            hr = grads[w].shape[1] // 2
            cp = _remote(ins[w].at[:, pl.ds((1 - c) * hr, hr), :], outs[w], send_sems.at[w], recv_sems.at[w], (x, y, 1 - c))
            cp.start()
            cps.append(cp)
        for cp in cps:
            cp.wait()

    return pl.pallas_call(
        body, name=name,
        in_specs=[_ANY] * n, out_specs=[_ANY] * n,
        out_shape=[jax.ShapeDtypeStruct((g.shape[0], g.shape[1] // 2, g.shape[2]), g.dtype) for g in grads],
        scratch_shapes=[pltpu.SemaphoreType.DMA((n,)), pltpu.SemaphoreType.DMA((n,))],
    )(*grads)


def _pair_sum(g, a, pos, name):
    nb, r, c = g.shape
    hr = r // 2

    def body(pos_ref, g_ref, a_ref, o_ref):
        o_ref[...] = (g_ref[...] + a_ref[...]).astype(BF16)

    return pl.pallas_call(
        body, name=name,
        grid_spec=pltpu.PrefetchScalarGridSpec(
            num_scalar_prefetch=1, grid=(nb,),
            in_specs=[pl.BlockSpec((1, hr, c), lambda k, pos: (k ^ pos[0], pos[1], 0)),
                      pl.BlockSpec((1, hr, c), lambda k, pos: (k ^ pos[0], 0, 0))],
            out_specs=pl.BlockSpec((1, hr, c), lambda k, pos: (k, 0, 0))),
        out_shape=jax.ShapeDtypeStruct((nb, hr, c), BF16),
        compiler_params=_params(),
    )(pos, g, a)


def _chip_sum(s, b, pos, name):
    _, hr, c = s.shape

    def body(pos_ref, s_ref, b_ref, o_ref):
        o_ref[...] = (s_ref[0].astype(F32) + b_ref[0].astype(F32)) + (b_ref[1].astype(F32) + b_ref[2].astype(F32))

    return pl.pallas_call(
        body, name=name,
        grid_spec=pltpu.PrefetchScalarGridSpec(
            num_scalar_prefetch=1, grid=(1,),
            in_specs=[pl.BlockSpec((1, hr, c), lambda k, pos: (0, 0, 0)), pl.BlockSpec((N_CHIPS - 1, hr, c), lambda k, pos: (0, 0, 0))],
            out_specs=pl.BlockSpec((hr, c), lambda k, pos: (pos[1], 0))),
        out_shape=jax.ShapeDtypeStruct((2 * hr, c), F32),
        compiler_params=_params(),
    )(pos, s, b)


def _pair_share(grads, name):
    n = len(grads)

    def body(*refs):
        outs = refs[n:2 * n]
        send_sems, recv_sems = refs[2 * n:]
        x, y, c = _mesh_pos()
        cps = []
        for w in range(n):
            hr = grads[w].shape[0] // 2
            rows = outs[w].at[pl.ds(c * hr, hr), :]
            cp = _remote(rows, rows, send_sems.at[w], recv_sems.at[w], (x, y, 1 - c))
            cp.start()
            cps.append(cp)
        for w, cp in enumerate(cps):
            cp.wait_send()
            hr = grads[w].shape[0] // 2
            other = outs[w].at[pl.ds((1 - c) * hr, hr), :]
            _remote(other, other, send_sems.at[w], recv_sems.at[w], (x, y, 1 - c)).wait_recv()

    return pl.pallas_call(
        body, name=name,
        in_specs=[_ANY] * n, out_specs=[_ANY] * n,
        out_shape=[jax.ShapeDtypeStruct(g.shape, g.dtype) for g in grads],
        input_output_aliases={w: w for w in range(n)},
        scratch_shapes=[pltpu.SemaphoreType.DMA((n,)), pltpu.SemaphoreType.DMA((n,))],
    )(*grads)


def _all_reduce_small(buf):
    n_dev = 8

    def body(in_ref, o_ref, slots, send_sems, recv_sems):
        x, y, c = _mesh_pos()
        me = 4 * x + 2 * y + c
        slots[0] = in_ref[...]
        cps = []
        for q in range(1, n_dev):
            peer = (x ^ (q >> 2), y ^ ((q >> 1) & 1), c ^ (q & 1))
            cp = _remote(in_ref, slots.at[q], send_sems.at[q - 1], recv_sems.at[q - 1], peer)
            cp.start()
            cps.append(cp)
        for cp in cps:
            cp.wait()
        acc = slots[me]
        for d in range(1, n_dev):
            acc = acc + slots[d ^ me]
        o_ref[...] = acc

    return pl.pallas_call(
        body, name="all_reduce_small",
        in_specs=[pl.BlockSpec(memory_space=pltpu.VMEM)], out_specs=pl.BlockSpec(memory_space=pltpu.VMEM),
        out_shape=jax.ShapeDtypeStruct(buf.shape, buf.dtype),
        scratch_shapes=[pltpu.VMEM((n_dev, *buf.shape), buf.dtype), pltpu.SemaphoreType.DMA((n_dev - 1,)),
                        pltpu.SemaphoreType.DMA((n_dev - 1,))],
    )(buf)


ADAMW_STEPS = 8


def _adamw(items, name, exchange=None):
    n = len(items)
    bias1 = 1.0 - ADAM_B1 ** ADAM_STEP
    bias2 = 1.0 - ADAM_B2 ** ADAM_STEP

    def body(*refs):
        ins, outs = refs[:4 * n], refs[4 * n:]
        for k in range(n):
            w_ref, g_ref, m_ref, v_ref = ins[4 * k:4 * k + 4]
            d_ref, mo_ref, vo_ref = outs[3 * k:3 * k + 3]
            gg = g_ref[...]
            m2 = ADAM_B1 * m_ref[...] + (1.0 - ADAM_B1) * gg
            v2 = ADAM_B2 * v_ref[...] + (1.0 - ADAM_B2) * (gg * gg)
            mo_ref[...] = m2
            vo_ref[...] = v2
            d_ref[...] = -ADAM_LR * ((m2 / bias1) / (jnp.sqrt(v2 / bias2) + ADAM_EPS) + ADAM_WD * w_ref[...])

    in_specs, out_specs, out_shape, args = [], [], [], []
    for w, g, m, v in items:
        r, c = w.shape
        steps = ADAMW_STEPS if r % (8 * ADAMW_STEPS) == 0 else 1
        assert steps == ADAMW_STEPS or n == 1
        spec = pl.BlockSpec((r // steps, c), lambda i: (i, 0))
        in_specs += [spec] * 4
        out_specs += [spec] * 3
        out_shape += [jax.ShapeDtypeStruct((r, c), F32)] * 3
        args += [w, g, m, v]
    out, got = _call(body, args, name=name, grid=(steps,), in_specs=in_specs, out_specs=out_specs, out_shape=out_shape,
                     exchange=exchange, steps=(0, steps - 1))
    return [tuple(out[3 * k:3 * k + 3]) for k in range(n)], got


def _pack(parts):
    flat = jnp.concatenate([parts[n].reshape(-1) for n in _SMALL])
    return flat.reshape(-1, LANES)


def _unpack(buf, like):
    flat = buf.reshape(-1)
    out, at = {}, 0
    for n in _SMALL:
        size = like[n].size
        out[n] = flat[at:at + size].reshape(like[n].shape)
        at += size
    return out


def kernel(x, p, ffn1_norm, ffn1_w_in, ffn1_w_out, mix_norm, w_mix_in, gmlp_v_norm, gmlp_w_s, gmlp_b, w_mix_out, ffn2_norm, ffn2_w_in, ffn2_w_out, ple_norm, ple_w_gate, ple_w_proj, final_norm, loss_target, m_ffn1_norm, m_ffn1_w_in, m_ffn1_w_out, m_mix_norm, m_w_mix_in, m_gmlp_v_norm, m_gmlp_w_s, m_gmlp_b, m_w_mix_out, m_ffn2_norm, m_ffn2_w_in, m_ffn2_w_out, m_ple_norm, m_ple_w_gate, m_ple_w_proj, m_final_norm, v_ffn1_norm, v_ffn1_w_in, v_ffn1_w_out, v_mix_norm, v_w_mix_in, v_gmlp_v_norm, v_gmlp_w_s, v_gmlp_b, v_w_mix_out, v_ffn2_norm, v_ffn2_w_in, v_ffn2_w_out, v_ple_norm, v_ple_w_gate, v_ple_w_proj, v_final_norm):
    args = dict(locals())
    w = {n: args[n] for n in _ALL}
    m = {n: args["m_" + n] for n in _ALL}
    v = {n: args["v_" + n] for n in _ALL}
    xi, yi, ci = _mesh_pos()
    pos = jnp.stack([2 * xi + yi, ci]).astype(jnp.int32)
    shard = {n: w[n][0] for n in _BIG}
    cast = {n: shard[n].astype(BF16) for n in _BIG}
    small = {n: (w[n][0] if w[n].ndim > 2 else w[n].reshape(1, -1)) for n in _SMALL}
    bt = small["gmlp_b"].T
    g_small, pair, from_chips = {}, {}, {}

    def pair_reduce(partials, tag):
        names = list(partials)
        parts = [partials[n].reshape(N_CHIPS, *shard[n].shape) for n in names]
        got = _pair_exchange(parts, "grad_pair_exchange_" + tag)
        for n, g, a in zip(names, parts, got):
            pair[n] = _pair_sum(g, a, pos, "pair_sum_" + n)
        return names

    w1in, w1out = _run_exchange(_WeightGather([cast["ffn1_w_in"], cast["ffn1_w_out"]]), "gather_ffn1")
    w1out = w1out.reshape(D_FF, D_MODEL)
    (h1, gu1), (wmix, wmo) = _ffn_fwd(x[0], small["ffn1_norm"], w1in, w1out, "ffn1_fwd",
                                      _WeightGather([cast["w_mix_in"], cast["w_mix_out"]]))
    wmo = wmo.reshape(D_MODEL, D_MODEL)
    zg, qkv = _mix_in_fwd(h1, small["mix_norm"], wmix)
    gm = _gmlp_fwd(zg, small["gmlp_v_norm"], small["gmlp_w_s"], bt)
    (att, carries), (w2in, w2out, wg, wproj) = _attn_fwd(
        qkv, _WeightGather([cast["ffn2_w_in"], cast["ffn2_w_out"], cast["ple_w_gate"], cast["ple_w_proj"]]))
    w2out = w2out.reshape(D_FF, D_MODEL)
    wg = wg.reshape(D_MODEL, D_MODEL)
    mixed = jnp.concatenate([gm, att], axis=1)
    h2 = _matmul_residual(h1, mixed, wmo, "mix_out_fwd")
    (h3, gu2), _ = _ffn_fwd(h2, small["ffn2_norm"], w2in, w2out, "ffn2_fwd")
    loss_part, g_small["final_norm"], dh4, dgp, dpp, n4, pb = _head(
        h3, p[0, 0], loss_target[0], small["ple_norm"], small["final_norm"], wg, wproj)

    part = {"ple_w_gate": _wgrad_rows(n4, dgp, N_CHIPS, "wgrad_ple_gate"),
            "ple_w_proj": _wgrad_cols(pb, dpp, N_CHIPS, "wgrad_ple_proj")}
    dh3, _, g_small["ple_norm"] = _norm_input_bwd(dh4, dgp, wg.reshape(1, D_MODEL, D_MODEL), h3, small["ple_norm"],
                                                  "ple_bwd")
    (dh2, dgu2, n3, act2, dhh3, g_small["ffn2_norm"]), _ = _ffn_bwd(dh3, h2, small["ffn2_norm"], gu2, w2in, w2out,
                                                                   "ffn2_bwd")
    part["ffn2_w_in"] = _wgrad_cols(n3, dgu2, N_CHIPS, "wgrad_ffn2_in")
    part["ffn2_w_out"] = _wgrad_rows(act2, dhh3, 2, "wgrad_ffn2_out")
    dmixed, dh2b = _matmul_nt_cast(dh2, wmo, "mix_out_bwd")
    part["w_mix_out"] = _wgrad_rows(mixed, dh2b, 2, "wgrad_mix_out")
    group = pair_reduce(part, "late")
    dzg, g_small["gmlp_w_s"], dbt, g_small["gmlp_v_norm"] = _gmlp_bwd(zg, dmixed, small["gmlp_v_norm"],
                                                                      small["gmlp_w_s"], bt)
    g_small["gmlp_b"] = dbt.T
    (dq, dk, dv), got = _attn_bwd(qkv, dmixed, carries, _ChipExchange([pair[n] for n in group]))
    from_chips.update(zip(group, got))

    dzmix = jnp.concatenate([dzg, dq, dk, dv], axis=1)
    dh1, n2, g_small["mix_norm"] = _norm_input_bwd(dh2, dzmix, wmix, h1, small["mix_norm"], "mix_in_bwd")
    group = pair_reduce({"w_mix_in": _wgrad_cols(n2, dzmix, N_CHIPS, "wgrad_mix_in")}, "mix")
    (dx, dgu1, n1, act1, dhh1, g_small["ffn1_norm"]), got = _ffn_bwd(
        dh1, x[0], small["ffn1_norm"], gu1, w1in, w1out, "ffn1_bwd", _ChipExchange([pair[n] for n in group]))
    from_chips.update(zip(group, got))

    pair_reduce({"ffn1_w_out": _wgrad_rows(act1, dhh1, 2, "wgrad_ffn1_out")}, "out")
    g_in, got = _wgrad_cols(n1, dgu1, N_CHIPS, "wgrad_ffn1_in", _ChipExchange([pair["ffn1_w_out"]]))
    from_chips["ffn1_w_out"] = got[0]
    pair_reduce({"ffn1_w_in": g_in}, "in")

    def finish(names, tag, exchange=None):
        halves = [_chip_sum(pair[n], from_chips[n], pos, "chip_sum_" + n) for n in names]
        full = _pair_share(halves, "grad_pair_share_" + tag)
        out, got = _adamw([(shard[n], g, m[n][0], v[n][0]) for n, g in zip(names, full)], "adamw_" + tag, exchange)
        for n, g, (d2, m2, v2) in zip(names, full, out):
            grads[n], delta[n], new_m[n], new_v[n] = g[None], d2[None], m2[None], v2[None]
        return got

    grads, delta, new_m, new_v = {}, {}, {}, {}
    got = finish([n for n in _BIG if n != "ffn1_w_in"], "most", _ChipExchange([pair["ffn1_w_in"]]))
    from_chips["ffn1_w_in"] = got[0]
    finish(["ffn1_w_in"], "last")

    packed = _pack(g_small)
    summed = _all_reduce_small(jnp.concatenate([packed, jnp.broadcast_to(loss_part, (8, LANES))], axis=0))
    g_packed, loss = summed[:packed.shape[0]], summed[packed.shape[0], 0]
    ((d_p, m_p, v_p),), _ = _adamw([(_pack(w), g_packed, _pack(m), _pack(v))], "adamw_small")
    for out, buf in ((grads, g_packed), (delta, d_p), (new_m, m_p), (new_v, v_p)):
        out.update(_unpack(buf, w))

    return (loss, dx[None], *[grads[n] for n in _ALL], *[delta[n] for n in _ALL], *[new_m[n] for n in _ALL],
            *[new_v[n] for n in _ALL])
```

```python
import functools
import math

import jax
import jax.numpy as jnp
from jax import lax
from jax.experimental import pallas as pl
from jax.experimental.pallas import tpu as pltpu

F32, BF16 = jnp.float32, jnp.bfloat16

D_MODEL = 1024
D_FF = 2816
FF_BLOCK = 2 * D_FF // 4
PLE_DIM = 256
CHUNK = 128
GM_HEADS = 4
GM_WIDTH = 512
SB_HEAD_DIM = 64
SB_WIDTH = 512
MIX_IN_WIDTH = 2 * GM_WIDTH + 3 * SB_WIDTH
MIX_BLOCK = MIX_IN_WIDTH // 4
EPS = 1e-6
N_CHIPS = 4
LANES = 128
ATT_BLOCK = 128
ATT_Q = 512
VMEM_LIMIT = 56 * 1024 * 1024

ADAM_LR, ADAM_B1, ADAM_B2, ADAM_EPS, ADAM_WD, ADAM_STEP = 0.001, 0.9, 0.999, 1e-08, 0.01, 10


def _dot(a, b):
    return jnp.dot(a, b, preferred_element_type=F32)


def _dot_nt(a, b):
    return lax.dot_general(a, b, (((1,), (1,)), ((), ())), preferred_element_type=F32)


def _dot_tn(a, b):
    return lax.dot_general(a, b, (((0,), (0,)), ((), ())), preferred_element_type=F32)


def _resident(shape):
    nd = len(shape)
    return pl.BlockSpec(shape, lambda *_: (0,) * nd, pipeline_mode=pl.Buffered(1))


def _rows(tm, width):
    return pl.BlockSpec((tm, width), lambda i: (i, 0))


def _params(n_axes=1):
    return pltpu.CompilerParams(dimension_semantics=("arbitrary",) * n_axes, vmem_limit_bytes=VMEM_LIMIT)


def _rstd(h):
    return lax.rsqrt(jnp.mean(h * h, axis=-1, keepdims=True) + EPS)


def _rms_bwd(dy, h, r, g):
    dyg = dy * g
    dh = r * dyg - h * (r * r * r) * jnp.mean(dyg * h, axis=-1, keepdims=True)
    return dh, dy * h * r


def _gelu(x):
    return 0.5 * x * (1.0 + lax.erf(x * (2.0 ** -0.5)))


def _gelu_grad(x):
    return 0.5 * (1.0 + lax.erf(x * (2.0 ** -0.5))) + x * jnp.exp(-0.5 * x * x) * ((2.0 * jnp.pi) ** -0.5)


def _token_tile(t):
    return min(256, t)


def _ffn_fwd(h, g, win, wout, name, exchange=None):
    t = h.shape[0]
    tm = _token_tile(t)

    def body(h_ref, g_ref, win_ref, wout_ref, ho_ref, gu_ref):
        hh = h_ref[...]
        n = (hh * _rstd(hh) * g_ref[...]).astype(BF16)
        acc = jnp.zeros((tm, D_MODEL), F32)
        for jb in range(2):
            gate = _dot(n, win_ref[jb])
            up = _dot(n, win_ref[2 + jb])
            gu_ref[:, jb * FF_BLOCK:(jb + 1) * FF_BLOCK] = gate.astype(BF16)
            gu_ref[:, D_FF + jb * FF_BLOCK:D_FF + (jb + 1) * FF_BLOCK] = up.astype(BF16)
            act = (gate * jax.nn.sigmoid(gate) * up).astype(BF16)
            acc = acc + _dot(act, wout_ref[jb * FF_BLOCK:(jb + 1) * FF_BLOCK, :])
        ho_ref[...] = hh + 0.5 * acc

    n = t // tm
    return _call(
        body, (h, g, win, wout), name=name, grid=(n,),
        in_specs=[_rows(tm, D_MODEL), _resident((1, D_MODEL)), _resident(win.shape), _resident(wout.shape)],
        out_specs=[_rows(tm, D_MODEL), _rows(tm, 2 * D_FF)],
        out_shape=[jax.ShapeDtypeStruct((t, D_MODEL), F32), jax.ShapeDtypeStruct((t, 2 * D_FF), BF16)],
        exchange=exchange, steps=(0, (2 * n) // 3, n - 1))


def _ffn_bwd(dho, h, g, gu, win, wout, name, exchange=None):
    t = h.shape[0]
    tm = _token_tile(t)

    def body(dho_ref, h_ref, g_ref, gu_ref, win_ref, wout_ref, dh_ref, dgu_ref, n_ref, act_ref, dhh_ref, dg_ref):
        i = pl.program_id(0)
        hh = h_ref[...]
        gg = g_ref[...]
        r = _rstd(hh)
        n_ref[...] = (hh * r * gg).astype(BF16)
        dho = dho_ref[...]
        dhh = (0.5 * dho).astype(BF16)
        dhh_ref[...] = dhh
        dn = jnp.zeros((tm, D_MODEL), F32)
        for jb in range(2):
            cg = slice(jb * FF_BLOCK, (jb + 1) * FF_BLOCK)
            cu = slice(D_FF + jb * FF_BLOCK, D_FF + (jb + 1) * FF_BLOCK)
            dact = _dot_nt(dhh, wout_ref[cg, :])
            gate = gu_ref[:, cg].astype(F32)
            up = gu_ref[:, cu].astype(F32)
            sg = jax.nn.sigmoid(gate)
            silu = gate * sg
            act_ref[:, cg] = (silu * up).astype(BF16)
            dgate = (dact * up * (sg * (1.0 + gate * (1.0 - sg)))).astype(BF16)
            dup = (dact * silu).astype(BF16)
            dgu_ref[:, cg] = dgate
            dgu_ref[:, cu] = dup
            dn = dn + _dot_nt(dgate, win_ref[jb]) + _dot_nt(dup, win_ref[2 + jb])
        dh, dg_rows = _rms_bwd(dn, hh, r, gg)
        dh_ref[...] = dho + dh

        @pl.when(i == 0)
        def _():
            dg_ref[...] = jnp.zeros_like(dg_ref)

        dg_ref[...] += jnp.sum(dg_rows, axis=0, keepdims=True)

    n = t // tm
    return _call(
        body, (dho, h, g, gu, win, wout), name=name, grid=(n,),
        in_specs=[_rows(tm, D_MODEL), _rows(tm, D_MODEL), _resident((1, D_MODEL)), _rows(tm, 2 * D_FF),
                  _resident(win.shape), _resident(wout.shape)],
        out_specs=[_rows(tm, D_MODEL), _rows(tm, 2 * D_FF), _rows(tm, D_MODEL), _rows(tm, D_FF), _rows(tm, D_MODEL),
                   pl.BlockSpec((1, D_MODEL), lambda i: (0, 0))],
        out_shape=[jax.ShapeDtypeStruct((t, D_MODEL), F32), jax.ShapeDtypeStruct((t, 2 * D_FF), BF16),
                   jax.ShapeDtypeStruct((t, D_MODEL), BF16), jax.ShapeDtypeStruct((t, D_FF), BF16),
                   jax.ShapeDtypeStruct((t, D_MODEL), BF16), jax.ShapeDtypeStruct((1, D_MODEL), F32)],
        exchange=exchange, steps=(0, n - 1))


def _wgrad(a, b, out_shape, out_block, out_index, a_width, b_width, grid_ij, name, exchange=None):
    t = a.shape[0]
    tk = min(2048, t)

    def body(a_ref, b_ref, o_ref):
        k = pl.program_id(2)
        prod = _dot_tn(a_ref[...], b_ref[...]).reshape(o_ref.shape)

        @pl.when(k == 0)
        def _():
            o_ref[...] = prod

        @pl.when(k > 0)
        def _():
            o_ref[...] += prod

    grid = (*grid_ij, t // tk)
    (out,), got = _call(
        body, (a, b), name=name, grid=grid,
        in_specs=[pl.BlockSpec((tk, a_width), lambda i, j, k: (k, i)), pl.BlockSpec((tk, b_width), lambda i, j, k: (k, j))],
        out_specs=[pl.BlockSpec(out_block, lambda i, j, k: out_index(i, j))],
        out_shape=[jax.ShapeDtypeStruct(out_shape, F32)],
        exchange=exchange, steps=(0, grid[0] * grid[1] * grid[2] - 1))
    return out if exchange is None else (out, got)


def _wgrad_cols(a, b, n_blocks, name, exchange=None):
    ka, nb = a.shape[1], b.shape[1] // n_blocks
    return _wgrad(a, b, (n_blocks, ka, nb), (1, ka, nb), lambda i, j: (j, 0, 0), ka, nb, (1, n_blocks), name, exchange)


def _wgrad_rows(a, b, n_blocks, name):
    ka, nb = a.shape[1] // n_blocks, b.shape[1]
    return _wgrad(a, b, (a.shape[1], nb), (ka, nb), lambda i, j: (i, 0), ka, nb, (n_blocks, 1), name)


def _mix_in_fwd(h, g, wmix):
    t = h.shape[0]
    tm = _token_tile(t)
    gw2 = 2 * GM_WIDTH

    def body(h_ref, g_ref, w_ref, zg_ref, qkv_ref):
        hh = h_ref[...]
        n = (hh * _rstd(hh) * g_ref[...]).astype(BF16)
        for b in range(N_CHIPS):
            z = _dot(n, w_ref[b])
            lo, hi = b * MIX_BLOCK, (b + 1) * MIX_BLOCK
            if hi <= gw2:
                zg_ref[:, lo:hi] = z
            elif lo >= gw2:
                qkv_ref[:, lo - gw2:hi - gw2] = z.astype(BF16)
            else:
                zg_ref[:, lo:gw2] = z[:, :gw2 - lo]
                qkv_ref[:, 0:hi - gw2] = z[:, gw2 - lo:].astype(BF16)

    return pl.pallas_call(
        body, name="mix_in_fwd", grid=(t // tm,),
        in_specs=[_rows(tm, D_MODEL), _resident((1, D_MODEL)), _resident(wmix.shape)],
        out_specs=[_rows(tm, gw2), _rows(tm, 3 * SB_WIDTH)],
        out_shape=[jax.ShapeDtypeStruct((t, gw2), F32), jax.ShapeDtypeStruct((t, 3 * SB_WIDTH), BF16)],
        compiler_params=_params(),
    )(h, g, wmix)


def _causal_chunk_mask():
    row = lax.broadcasted_iota(jnp.int32, (CHUNK, CHUNK), 0)
    col = lax.broadcasted_iota(jnp.int32, (CHUNK, CHUNK), 1)
    return row >= col


def _gmlp_tile(t):
    return min(512, t)


def _gmlp_fwd(zg, gv, ws, bt):
    t = zg.shape[0]
    tm = _gmlp_tile(t)

    def body(zg_ref, gv_ref, ws_ref, bt_ref, o_ref):
        u = _gelu(zg_ref[:, :GM_WIDTH])
        v = _gelu(zg_ref[:, GM_WIDTH:])
        vn = (v * _rstd(v) * gv_ref[...]).astype(BF16)
        mask = _causal_chunk_mask()
        for hd in range(GM_HEADS):
            wm = jnp.where(mask, ws_ref[hd], 0.0).astype(BF16)
            cols = slice(hd * CHUNK, (hd + 1) * CHUNK)
            for c in range(tm // CHUNK):
                rows = slice(c * CHUNK, (c + 1) * CHUNK)
                sv = _dot(wm, vn[rows, cols]) + bt_ref[:, hd:hd + 1]
                o_ref[rows, cols] = (u[rows, cols] * sv).astype(BF16)

    return pl.pallas_call(
        body, name="gmlp_fwd", grid=(t // tm,),
        in_specs=[_rows(tm, 2 * GM_WIDTH), _resident((1, GM_WIDTH)), _resident(ws.shape), _resident(bt.shape)],
        out_specs=_rows(tm, GM_WIDTH),
        out_shape=jax.ShapeDtypeStruct((t, GM_WIDTH), BF16),
        compiler_params=_params(),
    )(zg, gv, ws, bt)


def _gmlp_bwd(zg, dmixed, gv, ws, bt):
    t = zg.shape[0]
    tm = _gmlp_tile(t)

    def body(zg_ref, dgm_ref, gv_ref, ws_ref, bt_ref, dzg_ref, dws_ref, dbt_ref, dgv_ref):
        i = pl.program_id(0)

        @pl.when(i == 0)
        def _():
            dws_ref[...] = jnp.zeros_like(dws_ref)
            dbt_ref[...] = jnp.zeros_like(dbt_ref)
            dgv_ref[...] = jnp.zeros_like(dgv_ref)

        zu = zg_ref[:, :GM_WIDTH]
        zv = zg_ref[:, GM_WIDTH:]
        u = _gelu(zu)
        v = _gelu(zv)
        r = _rstd(v)
        gvv = gv_ref[...]
        vn = (v * r * gvv).astype(BF16)
        dgm = dgm_ref[...].astype(F32)
        dsv = (dgm * u).astype(BF16)
        mask = _causal_chunk_mask()
        du_cols, dvn_cols = [], []
        for hd in range(GM_HEADS):
            wm = jnp.where(mask, ws_ref[hd], 0.0).astype(BF16)
            cols = slice(hd * CHUNK, (hd + 1) * CHUNK)
            dw = jnp.zeros((CHUNK, CHUNK), F32)
            db = jnp.zeros((CHUNK, 1), F32)
            du_rows, dvn_rows = [], []
            for c in range(tm // CHUNK):
                rows = slice(c * CHUNK, (c + 1) * CHUNK)
                sv = _dot(wm, vn[rows, cols]) + bt_ref[:, hd:hd + 1]
                du_rows.append(dgm[rows, cols] * sv)
                dvn_rows.append(_dot_tn(wm, dsv[rows, cols]))
                dw = dw + _dot_nt(dsv[rows, cols], vn[rows, cols])
                db = db + jnp.sum(dsv[rows, cols].astype(F32), axis=1, keepdims=True)
            dws_ref[hd] += jnp.where(mask, dw, 0.0)
            dbt_ref[:, hd:hd + 1] += db
            du_cols.append(jnp.concatenate(du_rows, axis=0))
            dvn_cols.append(jnp.concatenate(dvn_rows, axis=0))
        du = jnp.concatenate(du_cols, axis=1)
        dvn = jnp.concatenate(dvn_cols, axis=1)
        dv, dgv_rows = _rms_bwd(dvn, v, r, gvv)
        dgv_ref[...] += jnp.sum(dgv_rows, axis=0, keepdims=True)
        dzg_ref[:, :GM_WIDTH] = (du * _gelu_grad(zu)).astype(BF16)
        dzg_ref[:, GM_WIDTH:] = (dv * _gelu_grad(zv)).astype(BF16)

    const = lambda nd: (lambda i: (0,) * nd)
    return pl.pallas_call(
        body, name="gmlp_bwd", grid=(t // tm,),
        in_specs=[_rows(tm, 2 * GM_WIDTH), _rows(tm, GM_WIDTH), _resident((1, GM_WIDTH)), _resident(ws.shape),
                  _resident(bt.shape)],
        out_specs=[_rows(tm, 2 * GM_WIDTH), pl.BlockSpec(ws.shape, const(3)), pl.BlockSpec(bt.shape, const(2)),
                   pl.BlockSpec((1, GM_WIDTH), const(2))],
        out_shape=[jax.ShapeDtypeStruct((t, 2 * GM_WIDTH), BF16), jax.ShapeDtypeStruct(ws.shape, F32),
                   jax.ShapeDtypeStruct(bt.shape, F32), jax.ShapeDtypeStruct((1, GM_WIDTH), F32)],
        compiler_params=_params(),
    )(zg, dmixed, gv, ws, bt)


def _att_masks():
    tb = ATT_BLOCK
    lane = lax.broadcasted_iota(jnp.int32, (1, LANES), 1)
    rj = lax.broadcasted_iota(jnp.int32, (2 * tb, 2 * tb), 0)
    cs = lax.broadcasted_iota(jnp.int32, (2 * tb, 2 * tb), 1)
    same_head = ((rj < tb) & (cs < tb)) | ((rj >= tb) & (cs >= tb))
    suffix = jnp.where(same_head & (rj >= cs), 1.0, 0.0).astype(BF16)
    prefix = jnp.where(same_head & (rj <= cs), 1.0, 0.0).astype(BF16)
    left = lax.broadcasted_iota(jnp.int32, (1, 2 * tb), 1) < tb
    tq = lax.broadcasted_iota(jnp.int32, (ATT_Q, 4 * tb), 0)
    ts = lax.broadcasted_iota(jnp.int32, (ATT_Q, 4 * tb), 1)
    key = jnp.where(ts < 2 * tb, ts & (tb - 1), (ts & (tb - 1)) + tb)
    return lane, suffix, prefix, left, key, tq


def _att_fill(k_ref, v_ref, kcat, vcat, n_blocks, lane):
    tb = ATT_BLOCK
    first = lane < SB_HEAD_DIM

    def fill(jb, carry):
        rows = pl.ds(pl.multiple_of(jb * tb, tb), tb)
        top = pl.ds(pl.multiple_of(jb * 2 * tb, tb), tb)
        bot = pl.ds(pl.multiple_of(jb * 2 * tb + tb, tb), tb)
        kb = k_ref[rows, :]
        vb = v_ref[rows, :]
        zero = jnp.zeros_like(kb)
        kcat[top, :] = jnp.where(first, kb, zero)
        kcat[bot, :] = jnp.where(first, zero, kb)
        vcat[top, :] = jnp.where(first, vb, zero)
        vcat[bot, :] = jnp.where(first, zero, vb)
        return carry

    lax.fori_loop(0, n_blocks, fill, 0)


def _block_sums(x, m):
    return _dot(x.astype(BF16), m)


def _softplus(z):
    return jnp.maximum(z, 0.0) + jnp.log(1.0 + jnp.exp2(jnp.abs(z) * -math.log2(math.e)))


def _scaled_queries(q_ref):
    return (q_ref[...].astype(F32) * (SB_HEAD_DIM ** -0.5)).astype(BF16)


def _att_specs(t):
    n_pairs = SB_WIDTH // LANES
    q_spec = pl.BlockSpec((ATT_Q, LANES), lambda p, i: (i, p))
    k_spec = pl.BlockSpec((t, LANES), lambda p, i: (0, n_pairs + p))
    v_spec = pl.BlockSpec((t, LANES), lambda p, i: (0, 2 * n_pairs + p))
    return n_pairs, q_spec, k_spec, v_spec


def _attn_fwd(qkv, exchange=None):
    t = qkv.shape[0]
    tb = ATT_BLOCK
    nkb = t // tb
    assert 2 * nkb <= LANES and t % ATT_Q == 0 and ATT_Q == 4 * tb
    n_pairs, q_spec, k_spec, v_spec = _att_specs(t)

    def body(q_ref, k_ref, v_ref, o_ref, ct_ref, kcat, vcat, acc, carry, z0, r0, z1, r1):
        i = pl.program_id(1)
        lane, suffix, _, left, key, tq = _att_masks()

        @pl.when(i == 0)
        def _():
            _att_fill(k_ref, v_ref, kcat, vcat, nkb, lane)

        q = _scaled_queries(q_ref)
        acc[...] = jnp.zeros_like(acc)
        carry[...] = jnp.zeros_like(carry)
        ct_ref[0] = jnp.zeros((ATT_Q, LANES), F32)

        def key_rows(m):
            return pl.ds(pl.multiple_of(m * 4 * tb, 4 * tb), 4 * tb)

        def scores(m, zb, rb, causal=None):
            z = _dot_nt(q, kcat[key_rows(m), :])
            zb[...] = z
            sp = _softplus(z)
            if causal is not None:
                sp = jnp.where(causal, sp, 0.0)
            for g in (1, 0):
                cols = slice(g * 2 * tb, (g + 1) * 2 * tb)
                rb[:, cols] = _block_sums(sp[:, cols], suffix)

        def weigh(m, zb, rb, causal=None):
            probs = [None, None]
            for g in (1, 0):
                cols = slice(g * 2 * tb, (g + 1) * 2 * tb)
                j = 2 * m + g
                r = rb[:, cols]
                c = carry[...]
                ct_ref[0] = jnp.where(lane == j, c[:, :tb], jnp.where(lane == nkb + j, c[:, tb:], ct_ref[0]))
                a = jnp.exp(zb[:, cols] - (r + c))
                if causal is not None:
                    a = jnp.where(causal[:, cols], a, 0.0)
                probs[g] = a.astype(BF16)
                carry[...] = c + jnp.where(left, r[:, 0:1], r[:, tb:tb + 1])
            acc[...] += _dot(jnp.concatenate(probs, axis=1), vcat[key_rows(m), :])

        sooner, later = key < tq, key + 2 * tb < tq
        scores(2 * i + 1, z1, r1, later)
        scores(2 * i, z0, r0, sooner)
        weigh(2 * i + 1, z1, r1, later)
        weigh(2 * i, z0, r0, sooner)

        @pl.when(i > 0)
        def _():
            scores(2 * i - 1, z1, r1)

            def loop(k, c):
                u = i - 1 - k
                scores(2 * u, z0, r0)
                weigh(2 * u + 1, z1, r1)
                scores(jnp.maximum(2 * u - 1, 0), z1, r1)
                weigh(2 * u, z0, r0)
                return c

            lax.fori_loop(0, i, loop, 0)

        o_ref[...] = acc[...].astype(BF16)

    tile = pltpu.VMEM((ATT_Q, 4 * tb), F32)

    nq = t // ATT_Q
    return _call(
        body, (qkv, qkv, qkv), name="attn_fwd", grid=(n_pairs, nq),
        in_specs=[q_spec, k_spec, v_spec],
        out_specs=[pl.BlockSpec((ATT_Q, LANES), lambda p, i: (i, p)), pl.BlockSpec((1, ATT_Q, LANES), lambda p, i: (p, i, 0))],
        out_shape=[jax.ShapeDtypeStruct((t, SB_WIDTH), BF16), jax.ShapeDtypeStruct((n_pairs, t, LANES), F32)],
        scratch_shapes=[pltpu.VMEM((2 * t, LANES), BF16), pltpu.VMEM((2 * t, LANES), BF16),
                        pltpu.VMEM((ATT_Q, LANES), F32), pltpu.VMEM((ATT_Q, 2 * tb), F32), tile, tile, tile, tile],
        exchange=exchange, steps=(0, (n_pairs - 1) * nq - 1, n_pairs * nq - 1))


def _attn_bwd(qkv, dmixed, carries, exchange=None):
    t = qkv.shape[0]
    tb = ATT_BLOCK
    nkb = t // tb
    nq = t // ATT_Q
    scale = SB_HEAD_DIM ** -0.5
    n_pairs, q_spec, k_spec, v_spec = _att_specs(t)
    gm_blocks = GM_WIDTH // LANES

    def body(q_ref, k_ref, v_ref, do_ref, ct_ref, dq_ref, dk_ref, dv_ref, kcat, vcat, dkacc, dvacc, dqacc, carry,
             z0, r0, s0, a0, z1, r1, s1, a1):
        i = pl.program_id(1)
        lane, suffix, prefix, left, key, tq = _att_masks()
        first = lane < SB_HEAD_DIM

        @pl.when(i == 0)
        def _():
            _att_fill(k_ref, v_ref, kcat, vcat, nkb, lane)
            dkacc[...] = jnp.zeros_like(dkacc)
            dvacc[...] = jnp.zeros_like(dvacc)

        q = _scaled_queries(q_ref)
        do = do_ref[...]
        dqacc[...] = jnp.zeros_like(dqacc)
        carry[...] = jnp.zeros_like(carry)

        def key_rows(m):
            return pl.ds(pl.multiple_of(m * 4 * tb, 4 * tb), 4 * tb)

        def front(m, bufs, causal=None):
            zb, rb, sb, ab = bufs
            z = _dot_nt(q, kcat[key_rows(m), :])
            zb[...] = z
            sp = _softplus(z)
            sb[...] = jnp.exp(z - sp)
            if causal is not None:
                sp = jnp.where(causal, sp, 0.0)
            for g in (0, 1):
                cols = slice(g * 2 * tb, (g + 1) * 2 * tb)
                rb[:, cols] = _block_sums(sp[:, cols], suffix)
            ab[...] = _dot_nt(do, vcat[key_rows(m), :])

        def back(m, bufs, causal=None):
            zb, rb, sb, ab = bufs
            dzs, probs = [None, None], [None, None]
            for g in (0, 1):
                cols = slice(g * 2 * tb, (g + 1) * 2 * tb)
                j = 2 * m + g
                ct = ct_ref[0]
                ca = jnp.sum(jnp.where(lane == j, ct, 0.0), axis=1, keepdims=True)
                cb = jnp.sum(jnp.where(lane == nkb + j, ct, 0.0), axis=1, keepdims=True)
                a = jnp.exp(zb[:, cols] - (rb[:, cols] + jnp.where(left, ca, cb)))
                if causal is not None:
                    a = jnp.where(causal[:, cols], a, 0.0)
                de = ab[:, cols] * a
                cl = _block_sums(de, prefix)
                pre = carry[...]
                dz = de - sb[:, cols] * (cl + pre)
                if causal is not None:
                    dz = jnp.where(causal[:, cols], dz, 0.0)
                carry[...] = pre + jnp.where(left, cl[:, tb - 1:tb], cl[:, 2 * tb - 1:2 * tb])
                dzs[g] = dz.astype(BF16)
                probs[g] = a.astype(BF16)
            dzb = jnp.concatenate(dzs, axis=1)
            dqacc[...] += _dot(dzb, kcat[key_rows(m), :])
            dkc = _dot_tn(dzb, q)
            dvc = _dot_tn(jnp.concatenate(probs, axis=1), do)
            out_rows = pl.ds(pl.multiple_of(m * 2 * tb, 2 * tb), 2 * tb)
            pick = lambda x: jnp.concatenate([jnp.where(first, x[0:tb], x[tb:2 * tb]),
                                              jnp.where(first, x[2 * tb:3 * tb], x[3 * tb:4 * tb])], axis=0)
            dkacc[out_rows, :] += pick(dkc)
            dvacc[out_rows, :] += pick(dvc)

        b0, b1 = (z0, r0, s0, a0), (z1, r1, s1, a1)

        @pl.when(i > 0)
        def _():
            front(0, b0)

            def loop(u, c):
                front(2 * u + 1, b1)
                back(2 * u, b0)
                front(jnp.minimum(2 * u + 2, 2 * i - 1), b0)
                back(2 * u + 1, b1)
                return c

            lax.fori_loop(0, i, loop, 0)

        sooner, later = key < tq, key + 2 * tb < tq
        front(2 * i, b0, sooner)
        front(2 * i + 1, b1, later)
        back(2 * i, b0, sooner)
        back(2 * i + 1, b1, later)

        dq_ref[...] = (dqacc[...] * scale).astype(BF16)

        @pl.when(i == nq - 1)
        def _():
            dk_ref[...] = dkacc[...].astype(BF16)
            dv_ref[...] = dvacc[...].astype(BF16)

    col = pl.BlockSpec((t, LANES), lambda p, i: (0, p))
    out = jax.ShapeDtypeStruct((t, SB_WIDTH), BF16)
    tile = pltpu.VMEM((ATT_Q, 4 * tb), F32)
    return _call(
        body, (qkv, qkv, qkv, dmixed, carries), name="attn_bwd", grid=(n_pairs, nq),
        in_specs=[q_spec, k_spec, v_spec, pl.BlockSpec((ATT_Q, LANES), lambda p, i: (i, gm_blocks + p)),
                  pl.BlockSpec((1, ATT_Q, LANES), lambda p, i: (p, i, 0))],
        out_specs=[pl.BlockSpec((ATT_Q, LANES), lambda p, i: (i, p)), col, col],
        out_shape=[out, out, out],
        scratch_shapes=[pltpu.VMEM((2 * t, LANES), BF16), pltpu.VMEM((2 * t, LANES), BF16),
                        pltpu.VMEM((t, LANES), F32), pltpu.VMEM((t, LANES), F32),
                        pltpu.VMEM((ATT_Q, LANES), F32), pltpu.VMEM((ATT_Q, 2 * tb), F32)] + [tile] * 8,
        exchange=exchange, steps=(0, n_pairs * nq - 1))


def _matmul_residual(res, a, w, name):
    t = a.shape[0]
    tm = _token_tile(t)

    def body(res_ref, a_ref, w_ref, o_ref):
        o_ref[...] = res_ref[...] + _dot(a_ref[...], w_ref[...])

    return pl.pallas_call(
        body, name=name, grid=(t // tm,),
        in_specs=[_rows(tm, res.shape[1]), _rows(tm, a.shape[1]), _resident(w.shape)],
        out_specs=_rows(tm, res.shape[1]),
        out_shape=jax.ShapeDtypeStruct(res.shape, F32),
        compiler_params=_params(),
    )(res, a, w)


def _matmul_nt_cast(dy, w, name):
    t = dy.shape[0]
    tm = _token_tile(t)

    def body(dy_ref, w_ref, o_ref, dyb_ref):
        dyb = dy_ref[...].astype(BF16)
        dyb_ref[...] = dyb
        o_ref[...] = _dot_nt(dyb, w_ref[...]).astype(BF16)

    return pl.pallas_call(
        body, name=name, grid=(t // tm,),
        in_specs=[_rows(tm, dy.shape[1]), _resident(w.shape)],
        out_specs=[_rows(tm, w.shape[0]), _rows(tm, dy.shape[1])],
        out_shape=[jax.ShapeDtypeStruct((t, w.shape[0]), BF16), jax.ShapeDtypeStruct(dy.shape, BF16)],
        compiler_params=_params(),
    )(dy, w)


def _norm_input_bwd(dres, dz, w, h, g, name):
    t = h.shape[0]
    tm = _token_tile(t)
    nb, _, width = w.shape

    def body(dres_ref, dz_ref, w_ref, h_ref, g_ref, dh_ref, n_ref, dg_ref):
        i = pl.program_id(0)
        hh = h_ref[...]
        gg = g_ref[...]
        r = _rstd(hh)
        n_ref[...] = (hh * r * gg).astype(BF16)
        dn = jnp.zeros((tm, D_MODEL), F32)
        for b in range(nb):
            dn = dn + _dot_nt(dz_ref[:, b * width:(b + 1) * width], w_ref[b])
        dh, dg_rows = _rms_bwd(dn, hh, r, gg)
        dh_ref[...] = dres_ref[...] + dh

        @pl.when(i == 0)
        def _():
            dg_ref[...] = jnp.zeros_like(dg_ref)

        dg_ref[...] += jnp.sum(dg_rows, axis=0, keepdims=True)

    return pl.pallas_call(
        body, name=name, grid=(t // tm,),
        in_specs=[_rows(tm, D_MODEL), _rows(tm, nb * width), _resident(w.shape), _rows(tm, D_MODEL),
                  _resident((1, D_MODEL))],
        out_specs=[_rows(tm, D_MODEL), _rows(tm, D_MODEL), pl.BlockSpec((1, D_MODEL), lambda i: (0, 0))],
        out_shape=[jax.ShapeDtypeStruct((t, D_MODEL), F32), jax.ShapeDtypeStruct((t, D_MODEL), BF16),
                   jax.ShapeDtypeStruct((1, D_MODEL), F32)],
        compiler_params=_params(),
    )(dres, dz, w, h, g)


def _head(h, p, target, gple, gfin, wg, wproj):
    t = h.shape[0]
    tm = _token_tile(t)
    pw = D_MODEL // N_CHIPS

    def body(h_ref, p_ref, tgt_ref, gple_ref, gfin_ref, wg_ref, wproj_ref,
             loss_ref, dgf_ref, dgple_ref, dh_ref, dgp_ref, dpp_ref, n_ref, pb_ref):
        i = pl.program_id(0)
        hh = h_ref[...]
        r_in = _rstd(hh)
        gp = gple_ref[...]
        n = (hh * r_in * gp).astype(BF16)
        n_ref[...] = n
        gate = jax.nn.sigmoid(_dot(n, wg_ref[...]))
        pb = p_ref[...].astype(BF16)
        pb_ref[...] = pb
        pp = jnp.concatenate([_dot(pb, wproj_ref[b]) for b in range(N_CHIPS)], axis=1)
        h4 = hh + gate * pp
        r = _rstd(h4)
        gf = gfin_ref[...]
        err = h4 * r * gf - tgt_ref[...]
        dy = err * (1.0 / D_MODEL)
        dh4, dgf_rows = _rms_bwd(dy, h4, r, gf)
        dgp = (dh4 * pp * gate * (1.0 - gate)).astype(BF16)
        dgp_ref[...] = dgp
        dpp_ref[...] = (dh4 * gate).astype(BF16)
        dh, dgple_rows = _rms_bwd(_dot_nt(dgp, wg_ref[...]), hh, r_in, gp)
        dh_ref[...] = dh4 + dh

        @pl.when(i == 0)
        def _():
            loss_ref[...] = jnp.zeros_like(loss_ref)
            dgf_ref[...] = jnp.zeros_like(dgf_ref)
            dgple_ref[...] = jnp.zeros_like(dgple_ref)

        loss_ref[...] += (0.5 / D_MODEL) * jnp.sum(err * err)
        dgf_ref[...] += jnp.sum(dgf_rows, axis=0, keepdims=True)
        dgple_ref[...] += jnp.sum(dgple_rows, axis=0, keepdims=True)

    bf = lambda w: jax.ShapeDtypeStruct((t, w), BF16)
    const = lambda i: (0, 0)
    return pl.pallas_call(
        body, name="head", grid=(t // tm,),
        in_specs=[_rows(tm, D_MODEL), _rows(tm, PLE_DIM), _rows(tm, D_MODEL), _resident((1, D_MODEL)),
                  _resident((1, D_MODEL)), _resident(wg.shape), _resident(wproj.shape)],
        out_specs=[pl.BlockSpec((1, LANES), const), pl.BlockSpec((1, D_MODEL), const), pl.BlockSpec((1, D_MODEL), const),
                   _rows(tm, D_MODEL), _rows(tm, D_MODEL), _rows(tm, D_MODEL), _rows(tm, D_MODEL), _rows(tm, PLE_DIM)],
        out_shape=[jax.ShapeDtypeStruct((1, LANES), F32), jax.ShapeDtypeStruct((1, D_MODEL), F32),
                   jax.ShapeDtypeStruct((1, D_MODEL), F32), jax.ShapeDtypeStruct((t, D_MODEL), F32), bf(D_MODEL),
                   bf(D_MODEL), bf(D_MODEL), bf(PLE_DIM)],
        compiler_params=_params(),
    )(h, p, target, gple, gfin, wg, wproj)


_BIG = ("ffn1_w_in", "ffn1_w_out", "w_mix_in", "w_mix_out", "ffn2_w_in", "ffn2_w_out", "ple_w_gate", "ple_w_proj")
_SMALL = ("ffn1_norm", "mix_norm", "gmlp_v_norm", "gmlp_w_s", "gmlp_b", "ffn2_norm", "ple_norm", "final_norm")
_ALL = ("ffn1_norm", "ffn1_w_in", "ffn1_w_out", "mix_norm", "w_mix_in", "gmlp_v_norm", "gmlp_w_s", "gmlp_b", "w_mix_out",
        "ffn2_norm", "ffn2_w_in", "ffn2_w_out", "ple_norm", "ple_w_gate", "ple_w_proj", "final_norm")
_ANY = pl.BlockSpec(memory_space=pl.ANY)
_MESH = pl.DeviceIdType.MESH


def _mesh_pos():
    return lax.axis_index("x"), lax.axis_index("y"), lax.axis_index("c")


def _other_chips(x, y):
    return [((x, 1 - y), 2 * x + 1 - y), ((1 - x, y), 2 * (1 - x) + y), ((1 - x, 1 - y), 2 * (1 - x) + 1 - y)]


def _remote(src, dst, send_sem, recv_sem, device):
    return pltpu.make_async_remote_copy(src_ref=src, dst_ref=dst, send_sem=send_sem, recv_sem=recv_sem,
                                        device_id=device, device_id_type=_MESH)


class _WeightGather:
    def __init__(self, shards):
        self.shapes = [s.shape for s in shards]
        self.operands = list(shards)
        self.out_shape = [jax.ShapeDtypeStruct((N_CHIPS, *s.shape), s.dtype) for s in shards]
        n = len(shards)
        self.per = 2 * (N_CHIPS - 1)
        self.scratch = [pltpu.SemaphoreType.DMA((self.per * n,)), pltpu.SemaphoreType.DMA((self.per * n,)),
                        pltpu.SemaphoreType.DMA((n,))]
        self.phases = [self.send, self.forward, self.finish]

    def _copies(self, ins, outs, sems):
        send_sems, recv_sems, local_sems = sems
        x, y, c = _mesh_pos()
        sibling = (x, y, 1 - c)
        mine = 2 * x + y
        local, first, landing, passed, arriving = [], [], [], [], []
        for w, shape in enumerate(self.shapes):
            hr = shape[0] // 2
            half = lambda blk, cc, w=w, hr=hr: outs[w].at[blk, pl.ds(cc * hr, hr), :]
            local.append(pltpu.make_async_copy(ins[w], outs[w].at[mine], local_sems.at[w]))
            for k, (chip, blk) in enumerate(_other_chips(x, y)):
                s = self.per * w + k
                first.append(_remote(ins[w].at[pl.ds(c * hr, hr), :], half(mine, c), send_sems.at[s], recv_sems.at[s],
                                     (*chip, c)))
                landing.append(_remote(half(blk, c), half(blk, c), send_sems.at[s], recv_sems.at[s], sibling))
                s = self.per * w + N_CHIPS - 1 + k
                passed.append(_remote(half(blk, c), half(blk, c), send_sems.at[s], recv_sems.at[s], sibling))
                arriving.append(_remote(half(blk, 1 - c), half(blk, 1 - c), send_sems.at[s], recv_sems.at[s], sibling))
        return local, first, landing, passed, arriving

    def send(self, ins, outs, sems):
        local, first, _, _, _ = self._copies(ins, outs, sems)
        for cp in local + first:
            cp.start()

    def forward(self, ins, outs, sems):
        _, _, landing, passed, _ = self._copies(ins, outs, sems)
        for landed, cp in zip(landing, passed):
            landed.wait_recv()
            cp.start()

    def finish(self, ins, outs, sems):
        local, first, _, passed, arriving = self._copies(ins, outs, sems)
        for cp in arriving:
            cp.wait_recv()
        for cp in first + passed:
            cp.wait_send()
        for cp in local:
            cp.wait()


class _ChipExchange:
    def __init__(self, sums):
        n = len(sums)
        self.n = n
        self.per = N_CHIPS - 1
        self.operands = list(sums)
        self.out_shape = [jax.ShapeDtypeStruct((self.per, *s.shape[1:]), s.dtype) for s in sums]
        self.scratch = [pltpu.SemaphoreType.DMA((self.per * n,)), pltpu.SemaphoreType.DMA((self.per * n,))]
        self.phases = [self.send, self.finish]

    def _copies(self, ins, outs, sems):
        send_sems, recv_sems = sems
        x, y, c = _mesh_pos()
        cps = []
        for w in range(self.n):
            for k, (chip, _) in enumerate(_other_chips(x, y)):
                s = self.per * w + k
                cps.append(_remote(ins[w].at[k + 1], outs[w].at[k], send_sems.at[s], recv_sems.at[s], (*chip, c)))
        return cps

    def send(self, ins, outs, sems):
        for cp in self._copies(ins, outs, sems):
            cp.start()

    def finish(self, ins, outs, sems):
        for cp in self._copies(ins, outs, sems):
            cp.wait()


def _run_exchange(ex, name):
    n_in, n_out = len(ex.operands), len(ex.out_shape)

    def body(*refs):
        ins, outs, sems = refs[:n_in], refs[n_in:n_in + n_out], refs[n_in + n_out:]
        for phase in ex.phases:
            phase(ins, outs, sems)

    return pl.pallas_call(body, name=name, in_specs=[_ANY] * n_in, out_specs=[_ANY] * n_out, out_shape=ex.out_shape,
                          scratch_shapes=ex.scratch)(*ex.operands)


def _call(body, args, *, name, grid, in_specs, out_specs, out_shape, scratch_shapes=(), exchange=None, steps=None):
    params = _params(len(grid))
    if exchange is None:
        out = pl.pallas_call(body, name=name, grid=grid, in_specs=in_specs, out_specs=out_specs, out_shape=out_shape,
                             scratch_shapes=list(scratch_shapes), compiler_params=params)(*args)
        return out, None
    n_in, n_out, n_scr = len(in_specs), len(out_specs), len(scratch_shapes)
    n_xin, n_xout = len(exchange.operands), len(exchange.out_shape)
    assert len(steps) == len(exchange.phases)

    def hosting(*refs):
        cuts = [n_in, n_xin, n_out, n_xout, n_scr]
        parts, at = [], 0
        for size in cuts:
            parts.append(refs[at:at + size])
            at += size
        ins, xins, outs, xouts, scr = parts
        sems = refs[at:]
        step = 0
        for axis, size in enumerate(grid):
            step = step * size + pl.program_id(axis)
        pl.when(step == steps[0])(lambda: exchange.phases[0](xins, xouts, sems))
        body(*ins, *outs, *scr)
        for at_step, phase in zip(steps[1:], exchange.phases[1:]):
            pl.when(step == at_step)(functools.partial(phase, xins, xouts, sems))

    out = pl.pallas_call(
        hosting, name=name, grid=grid,
        in_specs=list(in_specs) + [_ANY] * n_xin, out_specs=list(out_specs) + [_ANY] * n_xout,
        out_shape=list(out_shape) + list(exchange.out_shape),
        scratch_shapes=list(scratch_shapes) + list(exchange.scratch), compiler_params=params,
    )(*args, *exchange.operands)
    return out[:n_out], out[n_out:]


def _pair_exchange(grads, name):
    n = len(grads)

    def body(*refs):
        ins, outs = refs[:n], refs[n:2 * n]
        send_sems, recv_sems = refs[2 * n:]
        x, y, c = _mesh_pos()
        cps = []
        for w in range(n):
            hr = grads[w].shape[1] // 2
            cp = _remote(ins[w].at[:, pl.ds((1 - c) * hr, hr), :], outs[w], send_sems.at[w], recv_sems.at[w], (x, y, 1 - c))
            cp.start()
            cps.append(cp)
        for cp in cps:
            cp.wait()

    return pl.pallas_call(
        body, name=name,
        in_specs=[_ANY] * n, out_specs=[_ANY] * n,
        out_shape=[jax.ShapeDtypeStruct((g.shape[0], g.shape[1] // 2, g.shape[2]), g.dtype) for g in grads],
        scratch_shapes=[pltpu.SemaphoreType.DMA((n,)), pltpu.SemaphoreType.DMA((n,))],
    )(*grads)


def _pair_sum(g, a, pos, name):
    nb, r, c = g.shape
    hr = r // 2

    def body(pos_ref, g_ref, a_ref, o_ref):
        o_ref[...] = (g_ref[...] + a_ref[...]).astype(BF16)

    return pl.pallas_call(
        body, name=name,
        grid_spec=pltpu.PrefetchScalarGridSpec(
            num_scalar_prefetch=1, grid=(nb,),
            in_specs=[pl.BlockSpec((1, hr, c), lambda k, pos: (k ^ pos[0], pos[1], 0)),
                      pl.BlockSpec((1, hr, c), lambda k, pos: (k ^ pos[0], 0, 0))],
            out_specs=pl.BlockSpec((1, hr, c), lambda k, pos: (k, 0, 0))),
        out_shape=jax.ShapeDtypeStruct((nb, hr, c), BF16),
        compiler_params=_params(),
    )(pos, g, a)


def _chip_sum(s, b, pos, name):
    _, hr, c = s.shape

    def body(pos_ref, s_ref, b_ref, o_ref):
        o_ref[...] = (s_ref[0].astype(F32) + b_ref[0].astype(F32)) + (b_ref[1].astype(F32) + b_ref[2].astype(F32))

    return pl.pallas_call(
        body, name=name,
        grid_spec=pltpu.PrefetchScalarGridSpec(
            num_scalar_prefetch=1, grid=(1,),
            in_specs=[pl.BlockSpec((1, hr, c), lambda k, pos: (0, 0, 0)), pl.BlockSpec((N_CHIPS - 1, hr, c), lambda k, pos: (0, 0, 0))],
            out_specs=pl.BlockSpec((hr, c), lambda k, pos: (pos[1], 0))),
        out_shape=jax.ShapeDtypeStruct((2 * hr, c), F32),
        compiler_params=_params(),
    )(pos, s, b)


def _pair_share(grads, name):
    n = len(grads)

    def body(*refs):
        outs = refs[n:2 * n]
        send_sems, recv_sems = refs[2 * n:]
        x, y, c = _mesh_pos()
        cps = []
        for w in range(n):
            hr = grads[w].shape[0] // 2
            rows = outs[w].at[pl.ds(c * hr, hr), :]
            cp = _remote(rows, rows, send_sems.at[w], recv_sems.at[w], (x, y, 1 - c))
            cp.start()
            cps.append(cp)
        for w, cp in enumerate(cps):
            cp.wait_send()
            hr = grads[w].shape[0] // 2
            other = outs[w].at[pl.ds((1 - c) * hr, hr), :]
            _remote(other, other, send_sems.at[w], recv_sems.at[w], (x, y, 1 - c)).wait_recv()

    return pl.pallas_call(
        body, name=name,
        in_specs=[_ANY] * n, out_specs=[_ANY] * n,
        out_shape=[jax.ShapeDtypeStruct(g.shape, g.dtype) for g in grads],
        input_output_aliases={w: w for w in range(n)},
        scratch_shapes=[pltpu.SemaphoreType.DMA((n,)), pltpu.SemaphoreType.DMA((n,))],
    )(*grads)


def _all_reduce_small(rows, mats):
    n_dev = 8
    n_rows = -(-len(rows) // 8) * 8
    heights = [math.prod(a.shape[:-1]) for a in mats]
    n_tall = -(-sum(heights) // 8) * 8
    arrays = list(rows) + list(mats)

    def body(*refs):
        ins, outs = refs[:len(arrays)], refs[len(arrays):2 * len(arrays)]
        wide, tall, wide_slots, tall_slots, send_sems, recv_sems = refs[2 * len(arrays):]
        x, y, c = _mesh_pos()
        me = 4 * x + 2 * y + c
        wide[...] = jnp.zeros_like(wide)
        tall[...] = jnp.zeros_like(tall)
        for k, a in enumerate(rows):
            wide[k:k + 1, 0:a.shape[1]] = ins[k][...]
        at = 0
        for k, h in enumerate(heights):
            tall[at:at + h, :] = ins[len(rows) + k][...].reshape(h, LANES)
            at += h
        wide_slots[0] = wide[...]
        tall_slots[0] = tall[...]
        cps = []
        for q in range(1, n_dev):
            peer = (x ^ (q >> 2), y ^ ((q >> 1) & 1), c ^ (q & 1))
            for j, (buf, slots) in enumerate(((wide, wide_slots), (tall, tall_slots))):
                s = 2 * (q - 1) + j
                cp = _remote(buf, slots.at[q], send_sems.at[s], recv_sems.at[s], peer)
                cp.start()
                cps.append(cp)
        for cp in cps:
            cp.wait()
        wide_sum, tall_sum = wide_slots[me], tall_slots[me]
        for d in range(1, n_dev):
            wide_sum = wide_sum + wide_slots[d ^ me]
            tall_sum = tall_sum + tall_slots[d ^ me]
        for k, a in enumerate(rows):
            outs[k][...] = wide_sum[k:k + 1, 0:a.shape[1]]
        at = 0
        for k, h in enumerate(heights):
            outs[len(rows) + k][...] = tall_sum[at:at + h, :].reshape(mats[k].shape)
            at += h

    vmem = pl.BlockSpec(memory_space=pltpu.VMEM)
    return pl.pallas_call(
        body, name="all_reduce_small",
        in_specs=[vmem] * len(arrays), out_specs=[vmem] * len(arrays),
        out_shape=[jax.ShapeDtypeStruct(a.shape, F32) for a in arrays],
        scratch_shapes=[pltpu.VMEM((n_rows, D_MODEL), F32), pltpu.VMEM((n_tall, LANES), F32),
                        pltpu.VMEM((n_dev, n_rows, D_MODEL), F32), pltpu.VMEM((n_dev, n_tall, LANES), F32),
                        pltpu.SemaphoreType.DMA((2 * (n_dev - 1),)), pltpu.SemaphoreType.DMA((2 * (n_dev - 1),))],
    )(*arrays)


def _adamw_small(items):
    n = len(items)
    bias1 = 1.0 - ADAM_B1 ** ADAM_STEP
    bias2 = 1.0 - ADAM_B2 ** ADAM_STEP

    def body(*refs):
        ins, outs = refs[:4 * n], refs[4 * n:]
        for k in range(n):
            w_ref, g_ref, m_ref, v_ref = ins[4 * k:4 * k + 4]
            gg = g_ref[...]
            m2 = ADAM_B1 * m_ref[...] + (1.0 - ADAM_B1) * gg
            v2 = ADAM_B2 * v_ref[...] + (1.0 - ADAM_B2) * (gg * gg)
            outs[3 * k + 1][...] = m2
            outs[3 * k + 2][...] = v2
            outs[3 * k][...] = -ADAM_LR * ((m2 / bias1) / (jnp.sqrt(v2 / bias2) + ADAM_EPS) + ADAM_WD * w_ref[...])

    vmem = pl.BlockSpec(memory_space=pltpu.VMEM)
    out = pl.pallas_call(
        body, name="adamw_small", in_specs=[vmem] * (4 * n), out_specs=[vmem] * (3 * n),
        out_shape=[jax.ShapeDtypeStruct(w.shape, F32) for w, _, _, _ in items for _ in range(3)],
    )(*[a for item in items for a in item])
    return [tuple(out[3 * k:3 * k + 3]) for k in range(n)]


ADAMW_STEPS = 8


def _adamw(items, name, exchange=None):
    n = len(items)
    bias1 = 1.0 - ADAM_B1 ** ADAM_STEP
    bias2 = 1.0 - ADAM_B2 ** ADAM_STEP

    def body(*refs):
        ins, outs = refs[:4 * n], refs[4 * n:]
        for k in range(n):
            w_ref, g_ref, m_ref, v_ref = ins[4 * k:4 * k + 4]
            d_ref, mo_ref, vo_ref = outs[3 * k:3 * k + 3]
            gg = g_ref[...]
            m2 = ADAM_B1 * m_ref[...] + (1.0 - ADAM_B1) * gg
            v2 = ADAM_B2 * v_ref[...] + (1.0 - ADAM_B2) * (gg * gg)
            mo_ref[...] = m2
            vo_ref[...] = v2
            d_ref[...] = -ADAM_LR * ((m2 / bias1) / (jnp.sqrt(v2 / bias2) + ADAM_EPS) + ADAM_WD * w_ref[...])

    in_specs, out_specs, out_shape, args = [], [], [], []
    for w, g, m, v in items:
        r, c = w.shape
        steps = ADAMW_STEPS if r % (8 * ADAMW_STEPS) == 0 else 1
        assert steps == ADAMW_STEPS or n == 1
        spec = pl.BlockSpec((r // steps, c), lambda i: (i, 0))
        in_specs += [spec] * 4
        out_specs += [spec] * 3
        out_shape += [jax.ShapeDtypeStruct((r, c), F32)] * 3
        args += [w, g, m, v]
    out, got = _call(body, args, name=name, grid=(steps,), in_specs=in_specs, out_specs=out_specs, out_shape=out_shape,
                     exchange=exchange, steps=(0, steps - 1))
    return [tuple(out[3 * k:3 * k + 3]) for k in range(n)], got


def kernel(x, p, ffn1_norm, ffn1_w_in, ffn1_w_out, mix_norm, w_mix_in, gmlp_v_norm, gmlp_w_s, gmlp_b, w_mix_out, ffn2_norm, ffn2_w_in, ffn2_w_out, ple_norm, ple_w_gate, ple_w_proj, final_norm, loss_target, m_ffn1_norm, m_ffn1_w_in, m_ffn1_w_out, m_mix_norm, m_w_mix_in, m_gmlp_v_norm, m_gmlp_w_s, m_gmlp_b, m_w_mix_out, m_ffn2_norm, m_ffn2_w_in, m_ffn2_w_out, m_ple_norm, m_ple_w_gate, m_ple_w_proj, m_final_norm, v_ffn1_norm, v_ffn1_w_in, v_ffn1_w_out, v_mix_norm, v_w_mix_in, v_gmlp_v_norm, v_gmlp_w_s, v_gmlp_b, v_w_mix_out, v_ffn2_norm, v_ffn2_w_in, v_ffn2_w_out, v_ple_norm, v_ple_w_gate, v_ple_w_proj, v_final_norm):
    args = dict(locals())
    w = {n: args[n] for n in _ALL}
    m = {n: args["m_" + n] for n in _ALL}
    v = {n: args["v_" + n] for n in _ALL}
    xi, yi, ci = _mesh_pos()
    pos = jnp.stack([2 * xi + yi, ci]).astype(jnp.int32)
    shard = {n: w[n][0] for n in _BIG}
    cast = {n: shard[n].astype(BF16) for n in _BIG}
    small = {n: (w[n][0] if w[n].ndim > 2 else w[n].reshape(1, -1)) for n in _SMALL}
    bt = small["gmlp_b"].T
    g_small, pair, from_chips = {}, {}, {}

    def pair_reduce(partials, tag):
        names = list(partials)
        parts = [partials[n].reshape(N_CHIPS, *shard[n].shape) for n in names]
        got = _pair_exchange(parts, "grad_pair_exchange_" + tag)
        for n, g, a in zip(names, parts, got):
            pair[n] = _pair_sum(g, a, pos, "pair_sum_" + n)
        return names

    w1in, w1out = _run_exchange(_WeightGather([cast["ffn1_w_in"], cast["ffn1_w_out"]]), "gather_ffn1")
    w1out = w1out.reshape(D_FF, D_MODEL)
    (h1, gu1), (wmix, wmo) = _ffn_fwd(x[0], small["ffn1_norm"], w1in, w1out, "ffn1_fwd",
                                      _WeightGather([cast["w_mix_in"], cast["w_mix_out"]]))
    wmo = wmo.reshape(D_MODEL, D_MODEL)
    zg, qkv = _mix_in_fwd(h1, small["mix_norm"], wmix)
    gm = _gmlp_fwd(zg, small["gmlp_v_norm"], small["gmlp_w_s"], bt)
    (att, carries), (w2in, w2out, wg, wproj) = _attn_fwd(
        qkv, _WeightGather([cast["ffn2_w_in"], cast["ffn2_w_out"], cast["ple_w_gate"], cast["ple_w_proj"]]))
    w2out = w2out.reshape(D_FF, D_MODEL)
    wg = wg.reshape(D_MODEL, D_MODEL)
    mixed = jnp.concatenate([gm, att], axis=1)
    h2 = _matmul_residual(h1, mixed, wmo, "mix_out_fwd")
    (h3, gu2), _ = _ffn_fwd(h2, small["ffn2_norm"], w2in, w2out, "ffn2_fwd")
    loss_part, g_small["final_norm"], g_small["ple_norm"], dh3, dgp, dpp, n4, pb = _head(
        h3, p[0, 0], loss_target[0], small["ple_norm"], small["final_norm"], wg, wproj)

    part = {"ple_w_gate": _wgrad_rows(n4, dgp, N_CHIPS, "wgrad_ple_gate"),
            "ple_w_proj": _wgrad_cols(pb, dpp, N_CHIPS, "wgrad_ple_proj")}
    (dh2, dgu2, n3, act2, dhh3, g_small["ffn2_norm"]), _ = _ffn_bwd(dh3, h2, small["ffn2_norm"], gu2, w2in, w2out,
                                                                   "ffn2_bwd")
    part["ffn2_w_in"] = _wgrad_cols(n3, dgu2, N_CHIPS, "wgrad_ffn2_in")
    part["ffn2_w_out"] = _wgrad_rows(act2, dhh3, 2, "wgrad_ffn2_out")
    dmixed, dh2b = _matmul_nt_cast(dh2, wmo, "mix_out_bwd")
    part["w_mix_out"] = _wgrad_rows(mixed, dh2b, 2, "wgrad_mix_out")
    group = pair_reduce(part, "late")
    dzg, g_small["gmlp_w_s"], dbt, g_small["gmlp_v_norm"] = _gmlp_bwd(zg, dmixed, small["gmlp_v_norm"],
                                                                      small["gmlp_w_s"], bt)
    g_small["gmlp_b"] = dbt.T
    (dq, dk, dv), got = _attn_bwd(qkv, dmixed, carries, _ChipExchange([pair[n] for n in group]))
    from_chips.update(zip(group, got))

    dzmix = jnp.concatenate([dzg, dq, dk, dv], axis=1)
    dh1, n2, g_small["mix_norm"] = _norm_input_bwd(dh2, dzmix, wmix, h1, small["mix_norm"], "mix_in_bwd")
    group = pair_reduce({"w_mix_in": _wgrad_cols(n2, dzmix, N_CHIPS, "wgrad_mix_in")}, "mix")
    (dx, dgu1, n1, act1, dhh1, g_small["ffn1_norm"]), got = _ffn_bwd(
        dh1, x[0], small["ffn1_norm"], gu1, w1in, w1out, "ffn1_bwd", _ChipExchange([pair[n] for n in group]))
    from_chips.update(zip(group, got))

    pair_reduce({"ffn1_w_out": _wgrad_rows(act1, dhh1, 2, "wgrad_ffn1_out")}, "out")
    g_in, got = _wgrad_cols(n1, dgu1, N_CHIPS, "wgrad_ffn1_in", _ChipExchange([pair["ffn1_w_out"]]))
    from_chips["ffn1_w_out"] = got[0]
    pair_reduce({"ffn1_w_in": g_in}, "in")

    def finish(names, tag, exchange=None):
        halves = [_chip_sum(pair[n], from_chips[n], pos, "chip_sum_" + n) for n in names]
        full = _pair_share(halves, "grad_pair_share_" + tag)
        out, got = _adamw([(shard[n], g, m[n][0], v[n][0]) for n, g in zip(names, full)], "adamw_" + tag, exchange)
        for n, g, (d2, m2, v2) in zip(names, full, out):
            grads[n], delta[n], new_m[n], new_v[n] = g[None], d2[None], m2[None], v2[None]
        return got

    grads, delta, new_m, new_v = {}, {}, {}, {}
    got = finish([n for n in _BIG if n != "ffn1_w_in"], "most", _ChipExchange([pair["ffn1_w_in"]]))
    from_chips["ffn1_w_in"] = got[0]
    finish(["ffn1_w_in"], "last")

    rows = [n for n in _SMALL if g_small[n].shape[0] == 1]
    mats = [n for n in _SMALL if n not in rows]
    summed = _all_reduce_small([g_small[n] for n in rows] + [loss_part], [g_small[n] for n in mats])
    loss = summed[len(rows)][0, 0]
    g_sum = dict(zip(rows + mats, summed[:len(rows)] + summed[len(rows) + 1:]))
    like = lambda a, n: a[n].reshape(small[n].shape)
    out = _adamw_small([(small[n], g_sum[n], like(m, n), like(v, n)) for n in _SMALL])
    for n, (d2, m2, v2) in zip(_SMALL, out):
        grads[n], delta[n], new_m[n], new_v[n] = (a.reshape(w[n].shape) for a in (g_sum[n], d2, m2, v2))

    return (loss, dx[None], *[grads[n] for n in _ALL], *[delta[n] for n in _ALL], *[new_m[n] for n in _ALL],
            *[new_v[n] for n in _ALL])
```

```python
import functools
import math

import jax
import jax.numpy as jnp
from jax import lax
from jax.experimental import pallas as pl
from jax.experimental.pallas import tpu as pltpu

F32, BF16 = jnp.float32, jnp.bfloat16

D_MODEL = 1024
D_FF = 2816
FF_BLOCK = 2 * D_FF // 4
PLE_DIM = 256
CHUNK = 128
GM_HEADS = 4
GM_WIDTH = 512
SB_HEAD_DIM = 64
SB_WIDTH = 512
MIX_IN_WIDTH = 2 * GM_WIDTH + 3 * SB_WIDTH
MIX_BLOCK = MIX_IN_WIDTH // 4
EPS = 1e-6
N_CHIPS = 4
LANES = 128
ATT_BLOCK = 128
ATT_Q = 512
VMEM_LIMIT = 56 * 1024 * 1024

ADAM_LR, ADAM_B1, ADAM_B2, ADAM_EPS, ADAM_WD, ADAM_STEP = 0.001, 0.9, 0.999, 1e-08, 0.01, 10


def _dot(a, b):
    return jnp.dot(a, b, preferred_element_type=F32)


def _dot_nt(a, b):
    return lax.dot_general(a, b, (((1,), (1,)), ((), ())), preferred_element_type=F32)


def _dot_tn(a, b):
    return lax.dot_general(a, b, (((0,), (0,)), ((), ())), preferred_element_type=F32)


def _resident(shape):
    nd = len(shape)
    return pl.BlockSpec(shape, lambda *_: (0,) * nd, pipeline_mode=pl.Buffered(1))


def _rows(tm, width):
    return pl.BlockSpec((tm, width), lambda i: (i, 0))


def _params(n_axes=1):
    return pltpu.CompilerParams(dimension_semantics=("arbitrary",) * n_axes, vmem_limit_bytes=VMEM_LIMIT)


def _rstd(h):
    return lax.rsqrt(jnp.mean(h * h, axis=-1, keepdims=True) + EPS)


def _rms_bwd(dy, h, r, g):
    dyg = dy * g
    dh = r * dyg - h * (r * r * r) * jnp.mean(dyg * h, axis=-1, keepdims=True)
    return dh, dy * h * r


def _gelu(x):
    return 0.5 * x * (1.0 + lax.erf(x * (2.0 ** -0.5)))


def _gelu_grad(x):
    return 0.5 * (1.0 + lax.erf(x * (2.0 ** -0.5))) + x * jnp.exp(-0.5 * x * x) * ((2.0 * jnp.pi) ** -0.5)


def _token_tile(t):
    return min(256, t)


def _ffn_fwd(h, g, win, wout, name, exchange=None):
    t = h.shape[0]
    tm = _token_tile(t)

    def body(h_ref, g_ref, win_ref, wout_ref, ho_ref, gu_ref):
        hh = h_ref[...]
        n = (hh * _rstd(hh) * g_ref[...]).astype(BF16)
        acc = jnp.zeros((tm, D_MODEL), F32)
        for jb in range(2):
            gate = _dot(n, win_ref[jb])
            up = _dot(n, win_ref[2 + jb])
            gu_ref[:, jb * FF_BLOCK:(jb + 1) * FF_BLOCK] = gate.astype(BF16)
            gu_ref[:, D_FF + jb * FF_BLOCK:D_FF + (jb + 1) * FF_BLOCK] = up.astype(BF16)
            act = (gate * jax.nn.sigmoid(gate) * up).astype(BF16)
            acc = acc + _dot(act, wout_ref[jb * FF_BLOCK:(jb + 1) * FF_BLOCK, :])
        ho_ref[...] = hh + 0.5 * acc

    n = t // tm
    return _call(
        body, (h, g, win, wout), name=name, grid=(n,),
        in_specs=[_rows(tm, D_MODEL), _resident((1, D_MODEL)), _resident(win.shape), _resident(wout.shape)],
        out_specs=[_rows(tm, D_MODEL), _rows(tm, 2 * D_FF)],
        out_shape=[jax.ShapeDtypeStruct((t, D_MODEL), F32), jax.ShapeDtypeStruct((t, 2 * D_FF), BF16)],
        exchange=exchange, steps=(0, (2 * n) // 3, n - 1))


def _ffn_bwd(dho, h, g, gu, win, wout, name, exchange=None):
    t = h.shape[0]
    tm = _token_tile(t)

    def body(dho_ref, h_ref, g_ref, gu_ref, win_ref, wout_ref, dh_ref, dgu_ref, n_ref, act_ref, dhh_ref, dg_ref):
        i = pl.program_id(0)
        hh = h_ref[...]
        gg = g_ref[...]
        r = _rstd(hh)
        n_ref[...] = (hh * r * gg).astype(BF16)
        dho = dho_ref[...]
        dhh = (0.5 * dho).astype(BF16)
        dhh_ref[...] = dhh
        dn = jnp.zeros((tm, D_MODEL), F32)
        for jb in range(2):
            cg = slice(jb * FF_BLOCK, (jb + 1) * FF_BLOCK)
            cu = slice(D_FF + jb * FF_BLOCK, D_FF + (jb + 1) * FF_BLOCK)
            dact = _dot_nt(dhh, wout_ref[cg, :])
            gate = gu_ref[:, cg].astype(F32)
            up = gu_ref[:, cu].astype(F32)
            sg = jax.nn.sigmoid(gate)
            silu = gate * sg
            act_ref[:, cg] = (silu * up).astype(BF16)
            dgate = (dact * up * (sg * (1.0 + gate * (1.0 - sg)))).astype(BF16)
            dup = (dact * silu).astype(BF16)
            dgu_ref[:, cg] = dgate
            dgu_ref[:, cu] = dup
            dn = dn + _dot_nt(dgate, win_ref[jb]) + _dot_nt(dup, win_ref[2 + jb])
        dh, dg_rows = _rms_bwd(dn, hh, r, gg)
        dh_ref[...] = dho + dh

        @pl.when(i == 0)
        def _():
            dg_ref[...] = jnp.zeros_like(dg_ref)

        dg_ref[...] += jnp.sum(dg_rows, axis=0, keepdims=True)

    n = t // tm
    return _call(
        body, (dho, h, g, gu, win, wout), name=name, grid=(n,),
        in_specs=[_rows(tm, D_MODEL), _rows(tm, D_MODEL), _resident((1, D_MODEL)), _rows(tm, 2 * D_FF),
                  _resident(win.shape), _resident(wout.shape)],
        out_specs=[_rows(tm, D_MODEL), _rows(tm, 2 * D_FF), _rows(tm, D_MODEL), _rows(tm, D_FF), _rows(tm, D_MODEL),
                   pl.BlockSpec((1, D_MODEL), lambda i: (0, 0))],
        out_shape=[jax.ShapeDtypeStruct((t, D_MODEL), F32), jax.ShapeDtypeStruct((t, 2 * D_FF), BF16),
                   jax.ShapeDtypeStruct((t, D_MODEL), BF16), jax.ShapeDtypeStruct((t, D_FF), BF16),
                   jax.ShapeDtypeStruct((t, D_MODEL), BF16), jax.ShapeDtypeStruct((1, D_MODEL), F32)],
        exchange=exchange, steps=(0, n - 1))


def _wgrad(a, b, out_shape, out_block, out_index, a_width, b_width, grid_ij, name, exchange=None):
    t = a.shape[0]
    tk = min(2048, t)

    def body(a_ref, b_ref, o_ref):
        k = pl.program_id(2)
        prod = _dot_tn(a_ref[...], b_ref[...]).reshape(o_ref.shape)

        @pl.when(k == 0)
        def _():
            o_ref[...] = prod

        @pl.when(k > 0)
        def _():
            o_ref[...] += prod

    grid = (*grid_ij, t // tk)
    (out,), got = _call(
        body, (a, b), name=name, grid=grid,
        in_specs=[pl.BlockSpec((tk, a_width), lambda i, j, k: (k, i)), pl.BlockSpec((tk, b_width), lambda i, j, k: (k, j))],
        out_specs=[pl.BlockSpec(out_block, lambda i, j, k: out_index(i, j))],
        out_shape=[jax.ShapeDtypeStruct(out_shape, F32)],
        exchange=exchange, steps=(0, grid[0] * grid[1] * grid[2] - 1))
    return out if exchange is None else (out, got)


def _wgrad_cols(a, b, n_blocks, name, exchange=None):
    ka, nb = a.shape[1], b.shape[1] // n_blocks
    return _wgrad(a, b, (n_blocks, ka, nb), (1, ka, nb), lambda i, j: (j, 0, 0), ka, nb, (1, n_blocks), name, exchange)


def _wgrad_rows(a, b, n_blocks, name, exchange=None):
    ka, nb = a.shape[1] // n_blocks, b.shape[1]
    return _wgrad(a, b, (a.shape[1], nb), (ka, nb), lambda i, j: (i, 0), ka, nb, (n_blocks, 1), name, exchange)


def _mix_in_fwd(h, g, wmix):
    t = h.shape[0]
    tm = _token_tile(t)
    gw2 = 2 * GM_WIDTH

    def body(h_ref, g_ref, w_ref, zg_ref, qkv_ref):
        hh = h_ref[...]
        n = (hh * _rstd(hh) * g_ref[...]).astype(BF16)
        for b in range(N_CHIPS):
            z = _dot(n, w_ref[b])
            lo, hi = b * MIX_BLOCK, (b + 1) * MIX_BLOCK
            if hi <= gw2:
                zg_ref[:, lo:hi] = z
            elif lo >= gw2:
                qkv_ref[:, lo - gw2:hi - gw2] = z.astype(BF16)
            else:
                zg_ref[:, lo:gw2] = z[:, :gw2 - lo]
                qkv_ref[:, 0:hi - gw2] = z[:, gw2 - lo:].astype(BF16)

    return pl.pallas_call(
        body, name="mix_in_fwd", grid=(t // tm,),
        in_specs=[_rows(tm, D_MODEL), _resident((1, D_MODEL)), _resident(wmix.shape)],
        out_specs=[_rows(tm, gw2), _rows(tm, 3 * SB_WIDTH)],
        out_shape=[jax.ShapeDtypeStruct((t, gw2), F32), jax.ShapeDtypeStruct((t, 3 * SB_WIDTH), BF16)],
        compiler_params=_params(),
    )(h, g, wmix)


def _causal_chunk_mask():
    row = lax.broadcasted_iota(jnp.int32, (CHUNK, CHUNK), 0)
    col = lax.broadcasted_iota(jnp.int32, (CHUNK, CHUNK), 1)
    return row >= col


def _gmlp_tile(t):
    return min(512, t)


def _gmlp_fwd(zg, gv, ws, bt):
    t = zg.shape[0]
    tm = _gmlp_tile(t)

    def body(zg_ref, gv_ref, ws_ref, bt_ref, o_ref):
        u = _gelu(zg_ref[:, :GM_WIDTH])
        v = _gelu(zg_ref[:, GM_WIDTH:])
        vn = (v * _rstd(v) * gv_ref[...]).astype(BF16)
        mask = _causal_chunk_mask()
        for hd in range(GM_HEADS):
            wm = jnp.where(mask, ws_ref[hd], 0.0).astype(BF16)
            cols = slice(hd * CHUNK, (hd + 1) * CHUNK)
            for c in range(tm // CHUNK):
                rows = slice(c * CHUNK, (c + 1) * CHUNK)
                sv = _dot(wm, vn[rows, cols]) + bt_ref[:, hd:hd + 1]
                o_ref[rows, cols] = (u[rows, cols] * sv).astype(BF16)

    return pl.pallas_call(
        body, name="gmlp_fwd", grid=(t // tm,),
        in_specs=[_rows(tm, 2 * GM_WIDTH), _resident((1, GM_WIDTH)), _resident(ws.shape), _resident(bt.shape)],
        out_specs=_rows(tm, GM_WIDTH),
        out_shape=jax.ShapeDtypeStruct((t, GM_WIDTH), BF16),
        compiler_params=_params(),
    )(zg, gv, ws, bt)


def _gmlp_bwd(zg, dmixed, gv, ws, bt):
    t = zg.shape[0]
    tm = _gmlp_tile(t)

    def body(zg_ref, dgm_ref, gv_ref, ws_ref, bt_ref, dzg_ref, dws_ref, dbt_ref, dgv_ref):
        i = pl.program_id(0)

        @pl.when(i == 0)
        def _():
            dws_ref[...] = jnp.zeros_like(dws_ref)
            dbt_ref[...] = jnp.zeros_like(dbt_ref)
            dgv_ref[...] = jnp.zeros_like(dgv_ref)

        zu = zg_ref[:, :GM_WIDTH]
        zv = zg_ref[:, GM_WIDTH:]
        u = _gelu(zu)
        v = _gelu(zv)
        r = _rstd(v)
        gvv = gv_ref[...]
        vn = (v * r * gvv).astype(BF16)
        dgm = dgm_ref[...].astype(F32)
        dsv = (dgm * u).astype(BF16)
        mask = _causal_chunk_mask()
        du_cols, dvn_cols = [], []
        for hd in range(GM_HEADS):
            wm = jnp.where(mask, ws_ref[hd], 0.0).astype(BF16)
            cols = slice(hd * CHUNK, (hd + 1) * CHUNK)
            dw = jnp.zeros((CHUNK, CHUNK), F32)
            db = jnp.zeros((CHUNK, 1), F32)
            du_rows, dvn_rows = [], []
            for c in range(tm // CHUNK):
                rows = slice(c * CHUNK, (c + 1) * CHUNK)
                sv = _dot(wm, vn[rows, cols]) + bt_ref[:, hd:hd + 1]
                du_rows.append(dgm[rows, cols] * sv)
                dvn_rows.append(_dot_tn(wm, dsv[rows, cols]))
                dw = dw + _dot_nt(dsv[rows, cols], vn[rows, cols])
                db = db + jnp.sum(dsv[rows, cols].astype(F32), axis=1, keepdims=True)
            dws_ref[hd] += jnp.where(mask, dw, 0.0)
            dbt_ref[:, hd:hd + 1] += db
            du_cols.append(jnp.concatenate(du_rows, axis=0))
            dvn_cols.append(jnp.concatenate(dvn_rows, axis=0))
        du = jnp.concatenate(du_cols, axis=1)
        dvn = jnp.concatenate(dvn_cols, axis=1)
        dv, dgv_rows = _rms_bwd(dvn, v, r, gvv)
        dgv_ref[...] += jnp.sum(dgv_rows, axis=0, keepdims=True)
        dzg_ref[:, :GM_WIDTH] = (du * _gelu_grad(zu)).astype(BF16)
        dzg_ref[:, GM_WIDTH:] = (dv * _gelu_grad(zv)).astype(BF16)

    const = lambda nd: (lambda i: (0,) * nd)
    return pl.pallas_call(
        body, name="gmlp_bwd", grid=(t // tm,),
        in_specs=[_rows(tm, 2 * GM_WIDTH), _rows(tm, GM_WIDTH), _resident((1, GM_WIDTH)), _resident(ws.shape),
                  _resident(bt.shape)],
        out_specs=[_rows(tm, 2 * GM_WIDTH), pl.BlockSpec(ws.shape, const(3)), pl.BlockSpec(bt.shape, const(2)),
                   pl.BlockSpec((1, GM_WIDTH), const(2))],
        out_shape=[jax.ShapeDtypeStruct((t, 2 * GM_WIDTH), BF16), jax.ShapeDtypeStruct(ws.shape, F32),
                   jax.ShapeDtypeStruct(bt.shape, F32), jax.ShapeDtypeStruct((1, GM_WIDTH), F32)],
        compiler_params=_params(),
    )(zg, dmixed, gv, ws, bt)


def _att_masks():
    tb = ATT_BLOCK
    lane = lax.broadcasted_iota(jnp.int32, (1, LANES), 1)
    rj = lax.broadcasted_iota(jnp.int32, (2 * tb, 2 * tb), 0)
    cs = lax.broadcasted_iota(jnp.int32, (2 * tb, 2 * tb), 1)
    same_head = ((rj < tb) & (cs < tb)) | ((rj >= tb) & (cs >= tb))
    suffix = jnp.where(same_head & (rj >= cs), 1.0, 0.0).astype(BF16)
    prefix = jnp.where(same_head & (rj <= cs), 1.0, 0.0).astype(BF16)
    left = lax.broadcasted_iota(jnp.int32, (1, 2 * tb), 1) < tb
    tq = lax.broadcasted_iota(jnp.int32, (ATT_Q, 4 * tb), 0)
    ts = lax.broadcasted_iota(jnp.int32, (ATT_Q, 4 * tb), 1)
    key = jnp.where(ts < 2 * tb, ts & (tb - 1), (ts & (tb - 1)) + tb)
    return lane, suffix, prefix, left, key, tq


def _att_fill(k_ref, v_ref, kcat, vcat, n_blocks, lane):
    tb = ATT_BLOCK
    first = lane < SB_HEAD_DIM

    def fill(jb, carry):
        rows = pl.ds(pl.multiple_of(jb * tb, tb), tb)
        top = pl.ds(pl.multiple_of(jb * 2 * tb, tb), tb)
        bot = pl.ds(pl.multiple_of(jb * 2 * tb + tb, tb), tb)
        kb = k_ref[rows, :]
        vb = v_ref[rows, :]
        zero = jnp.zeros_like(kb)
        kcat[top, :] = jnp.where(first, kb, zero)
        kcat[bot, :] = jnp.where(first, zero, kb)
        vcat[top, :] = jnp.where(first, vb, zero)
        vcat[bot, :] = jnp.where(first, zero, vb)
        return carry

    lax.fori_loop(0, n_blocks, fill, 0)


def _block_sums(x, m):
    return _dot(x.astype(BF16), m)


def _softplus2(z2):
    return jnp.maximum(z2, 0.0) + jnp.log2(1.0 + jnp.exp2(-jnp.abs(z2)))


def _scaled_queries(q_ref, base2):
    scale = SB_HEAD_DIM ** -0.5
    return (q_ref[...].astype(F32) * (scale * math.log2(math.e) if base2 else scale)).astype(BF16)


def _att_specs(t):
    n_pairs = SB_WIDTH // LANES
    q_spec = pl.BlockSpec((ATT_Q, LANES), lambda p, i: (i, p))
    k_spec = pl.BlockSpec((t, LANES), lambda p, i: (0, n_pairs + p))
    v_spec = pl.BlockSpec((t, LANES), lambda p, i: (0, 2 * n_pairs + p))
    return n_pairs, q_spec, k_spec, v_spec


def _attn_fwd(qkv, exchange=None):
    t = qkv.shape[0]
    tb = ATT_BLOCK
    nkb = t // tb
    assert 2 * nkb <= LANES and t % ATT_Q == 0 and ATT_Q == 4 * tb
    n_pairs, q_spec, k_spec, v_spec = _att_specs(t)

    def body(q_ref, k_ref, v_ref, o_ref, ct_ref, kcat, vcat, acc, carry, z0, r0, z1, r1):
        i = pl.program_id(1)
        lane, suffix, _, left, key, tq = _att_masks()

        @pl.when(i == 0)
        def _():
            _att_fill(k_ref, v_ref, kcat, vcat, nkb, lane)

        q = _scaled_queries(q_ref, True)
        acc[...] = jnp.zeros_like(acc)
        carry[...] = jnp.zeros_like(carry)
        ct_ref[0] = jnp.zeros((ATT_Q, LANES), F32)

        def key_rows(m):
            return pl.ds(pl.multiple_of(m * 4 * tb, 4 * tb), 4 * tb)

        def scores(m, zb, rb, causal=None):
            z = _dot_nt(q, kcat[key_rows(m), :])
            zb[...] = z
            sp = _softplus2(z)
            if causal is not None:
                sp = jnp.where(causal, sp, 0.0)
            for g in (1, 0):
                cols = slice(g * 2 * tb, (g + 1) * 2 * tb)
                rb[:, cols] = _block_sums(sp[:, cols], suffix)

        def weigh(m, zb, rb, causal=None):
            probs = [None, None]
            for g in (1, 0):
                cols = slice(g * 2 * tb, (g + 1) * 2 * tb)
                j = 2 * m + g
                r = rb[:, cols]
                c = carry[...]
                ct_ref[0] = jnp.where(lane == j, c[:, :tb], jnp.where(lane == nkb + j, c[:, tb:], ct_ref[0]))
                a = jnp.exp2(zb[:, cols] - (r + c))
                if causal is not None:
                    a = jnp.where(causal[:, cols], a, 0.0)
                probs[g] = a.astype(BF16)
                carry[...] = c + jnp.where(left, r[:, 0:1], r[:, tb:tb + 1])
            acc[...] += _dot(jnp.concatenate(probs, axis=1), vcat[key_rows(m), :])

        sooner, later = key < tq, key + 2 * tb < tq
        scores(2 * i + 1, z1, r1, later)
        scores(2 * i, z0, r0, sooner)
        weigh(2 * i + 1, z1, r1, later)
        weigh(2 * i, z0, r0, sooner)

        @pl.when(i > 0)
        def _():
            scores(2 * i - 1, z1, r1)

            def loop(k, c):
                u = i - 1 - k
                scores(2 * u, z0, r0)
                weigh(2 * u + 1, z1, r1)
                scores(jnp.maximum(2 * u - 1, 0), z1, r1)
                weigh(2 * u, z0, r0)
                return c

            lax.fori_loop(0, i, loop, 0)

        o_ref[...] = acc[...].astype(BF16)

    tile = pltpu.VMEM((ATT_Q, 4 * tb), F32)

    nq = t // ATT_Q
    return _call(
        body, (qkv, qkv, qkv), name="attn_fwd", grid=(n_pairs, nq),
        in_specs=[q_spec, k_spec, v_spec],
        out_specs=[pl.BlockSpec((ATT_Q, LANES), lambda p, i: (i, p)), pl.BlockSpec((1, ATT_Q, LANES), lambda p, i: (p, i, 0))],
        out_shape=[jax.ShapeDtypeStruct((t, SB_WIDTH), BF16), jax.ShapeDtypeStruct((n_pairs, t, LANES), F32)],
        scratch_shapes=[pltpu.VMEM((2 * t, LANES), BF16), pltpu.VMEM((2 * t, LANES), BF16),
                        pltpu.VMEM((ATT_Q, LANES), F32), pltpu.VMEM((ATT_Q, 2 * tb), F32), tile, tile, tile, tile],
        exchange=exchange, steps=(0, (n_pairs - 1) * nq - 1, n_pairs * nq - 1))


def _attn_bwd(qkv, dmixed, carries, exchange=None):
    t = qkv.shape[0]
    tb = ATT_BLOCK
    nkb = t // tb
    nq = t // ATT_Q
    scale = SB_HEAD_DIM ** -0.5
    n_pairs, q_spec, k_spec, v_spec = _att_specs(t)
    gm_blocks = GM_WIDTH // LANES

    def body(q_ref, k_ref, v_ref, do_ref, ct_ref, dq_ref, dk_ref, dv_ref, kcat, vcat, dkacc, dvacc, dqacc, carry,
             z0, r0, s0, a0, z1, r1, s1, a1):
        i = pl.program_id(1)
        lane, suffix, prefix, left, key, tq = _att_masks()
        first = lane < SB_HEAD_DIM

        @pl.when(i == 0)
        def _():
            _att_fill(k_ref, v_ref, kcat, vcat, nkb, lane)
            dkacc[...] = jnp.zeros_like(dkacc)
            dvacc[...] = jnp.zeros_like(dvacc)

        q2 = _scaled_queries(q_ref, True)
        q = _scaled_queries(q_ref, False)
        do = do_ref[...]
        dqacc[...] = jnp.zeros_like(dqacc)
        carry[...] = jnp.zeros_like(carry)

        def key_rows(m):
            return pl.ds(pl.multiple_of(m * 4 * tb, 4 * tb), 4 * tb)

        def front(m, bufs, causal=None):
            zb, rb, sb, ab = bufs
            z = _dot_nt(q2, kcat[key_rows(m), :])
            zb[...] = z
            sp = _softplus2(z)
            sb[...] = jnp.exp2(z - sp)
            if causal is not None:
                sp = jnp.where(causal, sp, 0.0)
            for g in (0, 1):
                cols = slice(g * 2 * tb, (g + 1) * 2 * tb)
                rb[:, cols] = _block_sums(sp[:, cols], suffix)
            ab[...] = _dot_nt(do, vcat[key_rows(m), :])

        def back(m, bufs, causal=None):
            zb, rb, sb, ab = bufs
            dzs, probs = [None, None], [None, None]
            for g in (0, 1):
                cols = slice(g * 2 * tb, (g + 1) * 2 * tb)
                j = 2 * m + g
                ct = ct_ref[0]
                ca = jnp.sum(jnp.where(lane == j, ct, 0.0), axis=1, keepdims=True)
                cb = jnp.sum(jnp.where(lane == nkb + j, ct, 0.0), axis=1, keepdims=True)
                a = jnp.exp2(zb[:, cols] - (rb[:, cols] + jnp.where(left, ca, cb)))
                if causal is not None:
                    a = jnp.where(causal[:, cols], a, 0.0)
                de = ab[:, cols] * a
                cl = _block_sums(de, prefix)
                pre = carry[...]
                dz = de - sb[:, cols] * (cl + pre)
                if causal is not None:
                    dz = jnp.where(causal[:, cols], dz, 0.0)
                carry[...] = pre + jnp.where(left, cl[:, tb - 1:tb], cl[:, 2 * tb - 1:2 * tb])
                dzs[g] = dz.astype(BF16)
                probs[g] = a.astype(BF16)
            dzb = jnp.concatenate(dzs, axis=1)
            dqacc[...] += _dot(dzb, kcat[key_rows(m), :])
            dkc = _dot_tn(dzb, q)
            dvc = _dot_tn(jnp.concatenate(probs, axis=1), do)
            out_rows = pl.ds(pl.multiple_of(m * 2 * tb, 2 * tb), 2 * tb)
            pick = lambda x: jnp.concatenate([jnp.where(first, x[0:tb], x[tb:2 * tb]),
                                              jnp.where(first, x[2 * tb:3 * tb], x[3 * tb:4 * tb])], axis=0)
            dkacc[out_rows, :] += pick(dkc)
            dvacc[out_rows, :] += pick(dvc)

        b0, b1 = (z0, r0, s0, a0), (z1, r1, s1, a1)

        @pl.when(i > 0)
        def _():
            front(0, b0)

            def loop(u, c):
                front(2 * u + 1, b1)
                back(2 * u, b0)
                front(jnp.minimum(2 * u + 2, 2 * i - 1), b0)
                back(2 * u + 1, b1)
                return c

            lax.fori_loop(0, i, loop, 0)

        sooner, later = key < tq, key + 2 * tb < tq
        front(2 * i, b0, sooner)
        front(2 * i + 1, b1, later)
        back(2 * i, b0, sooner)
        back(2 * i + 1, b1, later)

        dq_ref[...] = (dqacc[...] * scale).astype(BF16)

        @pl.when(i == nq - 1)
        def _():
            dk_ref[...] = dkacc[...].astype(BF16)
            dv_ref[...] = dvacc[...].astype(BF16)

    col = pl.BlockSpec((t, LANES), lambda p, i: (0, p))
    out = jax.ShapeDtypeStruct((t, SB_WIDTH), BF16)
    tile = pltpu.VMEM((ATT_Q, 4 * tb), F32)
    return _call(
        body, (qkv, qkv, qkv, dmixed, carries), name="attn_bwd", grid=(n_pairs, nq),
        in_specs=[q_spec, k_spec, v_spec, pl.BlockSpec((ATT_Q, LANES), lambda p, i: (i, gm_blocks + p)),
                  pl.BlockSpec((1, ATT_Q, LANES), lambda p, i: (p, i, 0))],
        out_specs=[pl.BlockSpec((ATT_Q, LANES), lambda p, i: (i, p)), col, col],
        out_shape=[out, out, out],
        scratch_shapes=[pltpu.VMEM((2 * t, LANES), BF16), pltpu.VMEM((2 * t, LANES), BF16),
                        pltpu.VMEM((t, LANES), F32), pltpu.VMEM((t, LANES), F32),
                        pltpu.VMEM((ATT_Q, LANES), F32), pltpu.VMEM((ATT_Q, 2 * tb), F32)] + [tile] * 8,
        exchange=exchange, steps=(0, n_pairs * nq - 1))


def _matmul_residual(res, a, w, name):
    t = a.shape[0]
    tm = _token_tile(t)

    def body(res_ref, a_ref, w_ref, o_ref):
        o_ref[...] = res_ref[...] + _dot(a_ref[...], w_ref[...])

    return pl.pallas_call(
        body, name=name, grid=(t // tm,),
        in_specs=[_rows(tm, res.shape[1]), _rows(tm, a.shape[1]), _resident(w.shape)],
        out_specs=_rows(tm, res.shape[1]),
        out_shape=jax.ShapeDtypeStruct(res.shape, F32),
        compiler_params=_params(),
    )(res, a, w)


def _matmul_nt_cast(dy, w, name):
    t = dy.shape[0]
    tm = _token_tile(t)

    def body(dy_ref, w_ref, o_ref, dyb_ref):
        dyb = dy_ref[...].astype(BF16)
        dyb_ref[...] = dyb
        o_ref[...] = _dot_nt(dyb, w_ref[...]).astype(BF16)

    return pl.pallas_call(
        body, name=name, grid=(t // tm,),
        in_specs=[_rows(tm, dy.shape[1]), _resident(w.shape)],
        out_specs=[_rows(tm, w.shape[0]), _rows(tm, dy.shape[1])],
        out_shape=[jax.ShapeDtypeStruct((t, w.shape[0]), BF16), jax.ShapeDtypeStruct(dy.shape, BF16)],
        compiler_params=_params(),
    )(dy, w)


def _norm_input_bwd(dres, dz, w, h, g, name):
    t = h.shape[0]
    tm = _token_tile(t)
    nb, _, width = w.shape

    def body(dres_ref, dz_ref, w_ref, h_ref, g_ref, dh_ref, n_ref, dg_ref):
        i = pl.program_id(0)
        hh = h_ref[...]
        gg = g_ref[...]
        r = _rstd(hh)
        n_ref[...] = (hh * r * gg).astype(BF16)
        dn = jnp.zeros((tm, D_MODEL), F32)
        for b in range(nb):
            dn = dn + _dot_nt(dz_ref[:, b * width:(b + 1) * width], w_ref[b])
        dh, dg_rows = _rms_bwd(dn, hh, r, gg)
        dh_ref[...] = dres_ref[...] + dh

        @pl.when(i == 0)
        def _():
            dg_ref[...] = jnp.zeros_like(dg_ref)

        dg_ref[...] += jnp.sum(dg_rows, axis=0, keepdims=True)

    return pl.pallas_call(
        body, name=name, grid=(t // tm,),
        in_specs=[_rows(tm, D_MODEL), _rows(tm, nb * width), _resident(w.shape), _rows(tm, D_MODEL),
                  _resident((1, D_MODEL))],
        out_specs=[_rows(tm, D_MODEL), _rows(tm, D_MODEL), pl.BlockSpec((1, D_MODEL), lambda i: (0, 0))],
        out_shape=[jax.ShapeDtypeStruct((t, D_MODEL), F32), jax.ShapeDtypeStruct((t, D_MODEL), BF16),
                   jax.ShapeDtypeStruct((1, D_MODEL), F32)],
        compiler_params=_params(),
    )(dres, dz, w, h, g)


def _head(h, p, target, gple, gfin, wg, wproj):
    t = h.shape[0]
    tm = _token_tile(t)
    pw = D_MODEL // N_CHIPS

    def body(h_ref, p_ref, tgt_ref, gple_ref, gfin_ref, wg_ref, wproj_ref,
             loss_ref, dgf_ref, dgple_ref, dh_ref, dgp_ref, dpp_ref, n_ref, pb_ref):
        i = pl.program_id(0)
        hh = h_ref[...]
        r_in = _rstd(hh)
        gp = gple_ref[...]
        n = (hh * r_in * gp).astype(BF16)
        n_ref[...] = n
        gate = jax.nn.sigmoid(_dot(n, wg_ref[...]))
        pb = p_ref[...].astype(BF16)
        pb_ref[...] = pb
        pp = jnp.concatenate([_dot(pb, wproj_ref[b]) for b in range(N_CHIPS)], axis=1)
        h4 = hh + gate * pp
        r = _rstd(h4)
        gf = gfin_ref[...]
        err = h4 * r * gf - tgt_ref[...]
        dy = err * (1.0 / D_MODEL)
        dh4, dgf_rows = _rms_bwd(dy, h4, r, gf)
        dgp = (dh4 * pp * gate * (1.0 - gate)).astype(BF16)
        dgp_ref[...] = dgp
        dpp_ref[...] = (dh4 * gate).astype(BF16)
        dh, dgple_rows = _rms_bwd(_dot_nt(dgp, wg_ref[...]), hh, r_in, gp)
        dh_ref[...] = dh4 + dh

        @pl.when(i == 0)
        def _():
            loss_ref[...] = jnp.zeros_like(loss_ref)
            dgf_ref[...] = jnp.zeros_like(dgf_ref)
            dgple_ref[...] = jnp.zeros_like(dgple_ref)

        loss_ref[...] += (0.5 / D_MODEL) * jnp.sum(err * err)
        dgf_ref[...] += jnp.sum(dgf_rows, axis=0, keepdims=True)
        dgple_ref[...] += jnp.sum(dgple_rows, axis=0, keepdims=True)

    bf = lambda w: jax.ShapeDtypeStruct((t, w), BF16)
    const = lambda i: (0, 0)
    return pl.pallas_call(
        body, name="head", grid=(t // tm,),
        in_specs=[_rows(tm, D_MODEL), _rows(tm, PLE_DIM), _rows(tm, D_MODEL), _resident((1, D_MODEL)),
                  _resident((1, D_MODEL)), _resident(wg.shape), _resident(wproj.shape)],
        out_specs=[pl.BlockSpec((1, LANES), const), pl.BlockSpec((1, D_MODEL), const), pl.BlockSpec((1, D_MODEL), const),
                   _rows(tm, D_MODEL), _rows(tm, D_MODEL), _rows(tm, D_MODEL), _rows(tm, D_MODEL), _rows(tm, PLE_DIM)],
        out_shape=[jax.ShapeDtypeStruct((1, LANES), F32), jax.ShapeDtypeStruct((1, D_MODEL), F32),
                   jax.ShapeDtypeStruct((1, D_MODEL), F32), jax.ShapeDtypeStruct((t, D_MODEL), F32), bf(D_MODEL),
                   bf(D_MODEL), bf(D_MODEL), bf(PLE_DIM)],
        compiler_params=_params(),
    )(h, p, target, gple, gfin, wg, wproj)


_BIG = ("ffn1_w_in", "ffn1_w_out", "w_mix_in", "w_mix_out", "ffn2_w_in", "ffn2_w_out", "ple_w_gate", "ple_w_proj")
_SMALL = ("ffn1_norm", "mix_norm", "gmlp_v_norm", "gmlp_w_s", "gmlp_b", "ffn2_norm", "ple_norm", "final_norm")
_ALL = ("ffn1_norm", "ffn1_w_in", "ffn1_w_out", "mix_norm", "w_mix_in", "gmlp_v_norm", "gmlp_w_s", "gmlp_b", "w_mix_out",
        "ffn2_norm", "ffn2_w_in", "ffn2_w_out", "ple_norm", "ple_w_gate", "ple_w_proj", "final_norm")
_ANY = pl.BlockSpec(memory_space=pl.ANY)
_MESH = pl.DeviceIdType.MESH


def _mesh_pos():
    return lax.axis_index("x"), lax.axis_index("y"), lax.axis_index("c")


def _other_chips(x, y):
    return [((x, 1 - y), 2 * x + 1 - y), ((1 - x, y), 2 * (1 - x) + y), ((1 - x, 1 - y), 2 * (1 - x) + 1 - y)]


def _remote(src, dst, send_sem, recv_sem, device):
    return pltpu.make_async_remote_copy(src_ref=src, dst_ref=dst, send_sem=send_sem, recv_sem=recv_sem,
                                        device_id=device, device_id_type=_MESH)


class _WeightGather:
    def __init__(self, shards):
        self.shapes = [s.shape for s in shards]
        self.operands = list(shards)
        self.out_shape = [jax.ShapeDtypeStruct((N_CHIPS, *s.shape), s.dtype) for s in shards]
        n = len(shards)
        self.per = 2 * (N_CHIPS - 1)
        self.scratch = [pltpu.SemaphoreType.DMA((self.per * n,)), pltpu.SemaphoreType.DMA((self.per * n,)),
                        pltpu.SemaphoreType.DMA((n,))]
        self.phases = [self.send, self.forward, self.finish]

    def _copies(self, ins, outs, sems):
        send_sems, recv_sems, local_sems = sems
        x, y, c = _mesh_pos()
        sibling = (x, y, 1 - c)
        mine = 2 * x + y
        local, first, landing, passed, arriving = [], [], [], [], []
        for w, shape in enumerate(self.shapes):
            hr = shape[0] // 2
            half = lambda blk, cc, w=w, hr=hr: outs[w].at[blk, pl.ds(cc * hr, hr), :]
            local.append(pltpu.make_async_copy(ins[w], outs[w].at[mine], local_sems.at[w]))
            for k, (chip, blk) in enumerate(_other_chips(x, y)):
                s = self.per * w + k
                first.append(_remote(ins[w].at[pl.ds(c * hr, hr), :], half(mine, c), send_sems.at[s], recv_sems.at[s],
                                     (*chip, c)))
                landing.append(_remote(half(blk, c), half(blk, c), send_sems.at[s], recv_sems.at[s], sibling))
                s = self.per * w + N_CHIPS - 1 + k
                passed.append(_remote(half(blk, c), half(blk, c), send_sems.at[s], recv_sems.at[s], sibling))
                arriving.append(_remote(half(blk, 1 - c), half(blk, 1 - c), send_sems.at[s], recv_sems.at[s], sibling))
        return local, first, landing, passed, arriving

    def send(self, ins, outs, sems):
        local, first, _, _, _ = self._copies(ins, outs, sems)
        for cp in local + first:
            cp.start()

    def forward(self, ins, outs, sems):
        _, _, landing, passed, _ = self._copies(ins, outs, sems)
        for landed, cp in zip(landing, passed):
            landed.wait_recv()
            cp.start()

    def finish(self, ins, outs, sems):
        local, first, _, passed, arriving = self._copies(ins, outs, sems)
        for cp in arriving:
            cp.wait_recv()
        for cp in first + passed:
            cp.wait_send()
        for cp in local:
            cp.wait()


class _ChipExchange:
    def __init__(self, sums):
        n = len(sums)
        self.n = n
        self.per = N_CHIPS - 1
        self.operands = list(sums)
        self.out_shape = [jax.ShapeDtypeStruct((self.per, *s.shape[1:]), s.dtype) for s in sums]
        self.scratch = [pltpu.SemaphoreType.DMA((self.per * n,)), pltpu.SemaphoreType.DMA((self.per * n,))]
        self.phases = [self.send, self.finish]

    def _copies(self, ins, outs, sems):
        send_sems, recv_sems = sems
        x, y, c = _mesh_pos()
        cps = []
        for w in range(self.n):
            for k, (chip, _) in enumerate(_other_chips(x, y)):
                s = self.per * w + k
                cps.append(_remote(ins[w].at[k + 1], outs[w].at[k], send_sems.at[s], recv_sems.at[s], (*chip, c)))
        return cps

    def send(self, ins, outs, sems):
        for cp in self._copies(ins, outs, sems):
            cp.start()

    def finish(self, ins, outs, sems):
        for cp in self._copies(ins, outs, sems):
            cp.wait()


def _run_exchange(ex, name):
    n_in, n_out = len(ex.operands), len(ex.out_shape)

    def body(*refs):
        ins, outs, sems = refs[:n_in], refs[n_in:n_in + n_out], refs[n_in + n_out:]
        for phase in ex.phases:
            phase(ins, outs, sems)

    return pl.pallas_call(body, name=name, in_specs=[_ANY] * n_in, out_specs=[_ANY] * n_out, out_shape=ex.out_shape,
                          scratch_shapes=ex.scratch)(*ex.operands)


def _call(body, args, *, name, grid, in_specs, out_specs, out_shape, scratch_shapes=(), exchange=None, steps=None):
    params = _params(len(grid))
    if exchange is None:
        out = pl.pallas_call(body, name=name, grid=grid, in_specs=in_specs, out_specs=out_specs, out_shape=out_shape,
                             scratch_shapes=list(scratch_shapes), compiler_params=params)(*args)
        return out, None
    n_in, n_out, n_scr = len(in_specs), len(out_specs), len(scratch_shapes)
    n_xin, n_xout = len(exchange.operands), len(exchange.out_shape)
    assert len(steps) == len(exchange.phases)

    def hosting(*refs):
        cuts = [n_in, n_xin, n_out, n_xout, n_scr]
        parts, at = [], 0
        for size in cuts:
            parts.append(refs[at:at + size])
            at += size
        ins, xins, outs, xouts, scr = parts
        sems = refs[at:]
        step = 0
        for axis, size in enumerate(grid):
            step = step * size + pl.program_id(axis)
        pl.when(step == steps[0])(lambda: exchange.phases[0](xins, xouts, sems))
        body(*ins, *outs, *scr)
        for at_step, phase in zip(steps[1:], exchange.phases[1:]):
            pl.when(step == at_step)(functools.partial(phase, xins, xouts, sems))

    out = pl.pallas_call(
        hosting, name=name, grid=grid,
        in_specs=list(in_specs) + [_ANY] * n_xin, out_specs=list(out_specs) + [_ANY] * n_xout,
        out_shape=list(out_shape) + list(exchange.out_shape),
        scratch_shapes=list(scratch_shapes) + list(exchange.scratch), compiler_params=params,
    )(*args, *exchange.operands)
    return out[:n_out], out[n_out:]


def _pair_exchange(grads, name):
    n = len(grads)

    def body(*refs):
        ins, outs = refs[:n], refs[n:2 * n]
        send_sems, recv_sems = refs[2 * n:]
        x, y, c = _mesh_pos()
        cps = []
        for w in range(n):
            hr = grads[w].shape[1] // 2
            cp = _remote(ins[w].at[:, pl.ds((1 - c) * hr, hr), :], outs[w], send_sems.at[w], recv_sems.at[w], (x, y, 1 - c))
            cp.start()
            cps.append(cp)
        for cp in cps:
            cp.wait()

    return pl.pallas_call(
        body, name=name,
        in_specs=[_ANY] * n, out_specs=[_ANY] * n,
        out_shape=[jax.ShapeDtypeStruct((g.shape[0], g.shape[1] // 2, g.shape[2]), g.dtype) for g in grads],
        scratch_shapes=[pltpu.SemaphoreType.DMA((n,)), pltpu.SemaphoreType.DMA((n,))],
    )(*grads)


def _pair_sum(g, a, pos, name):
    nb, r, c = g.shape
    hr = r // 2

    def body(pos_ref, g_ref, a_ref, o_ref):
        o_ref[...] = (g_ref[...] + a_ref[...]).astype(BF16)

    return pl.pallas_call(
        body, name=name,
        grid_spec=pltpu.PrefetchScalarGridSpec(
            num_scalar_prefetch=1, grid=(nb,),
            in_specs=[pl.BlockSpec((1, hr, c), lambda k, pos: (k ^ pos[0], pos[1], 0)),
                      pl.BlockSpec((1, hr, c), lambda k, pos: (k ^ pos[0], 0, 0))],
            out_specs=pl.BlockSpec((1, hr, c), lambda k, pos: (k, 0, 0))),
        out_shape=jax.ShapeDtypeStruct((nb, hr, c), BF16),
        compiler_params=_params(),
    )(pos, g, a)


def _chip_sum(s, b, pos, name):
    _, hr, c = s.shape

    def body(pos_ref, s_ref, b_ref, o_ref):
        o_ref[...] = (s_ref[0].astype(F32) + b_ref[0].astype(F32)) + (b_ref[1].astype(F32) + b_ref[2].astype(F32))

    return pl.pallas_call(
        body, name=name,
        grid_spec=pltpu.PrefetchScalarGridSpec(
            num_scalar_prefetch=1, grid=(1,),
            in_specs=[pl.BlockSpec((1, hr, c), lambda k, pos: (0, 0, 0)), pl.BlockSpec((N_CHIPS - 1, hr, c), lambda k, pos: (0, 0, 0))],
            out_specs=pl.BlockSpec((hr, c), lambda k, pos: (pos[1], 0))),
        out_shape=jax.ShapeDtypeStruct((2 * hr, c), F32),
        compiler_params=_params(),
    )(pos, s, b)


def _pair_share(grads, name):
    n = len(grads)

    def body(*refs):
        outs = refs[n:2 * n]
        send_sems, recv_sems = refs[2 * n:]
        x, y, c = _mesh_pos()
        cps = []
        for w in range(n):
            hr = grads[w].shape[0] // 2
            rows = outs[w].at[pl.ds(c * hr, hr), :]
            cp = _remote(rows, rows, send_sems.at[w], recv_sems.at[w], (x, y, 1 - c))
            cp.start()
            cps.append(cp)
        for w, cp in enumerate(cps):
            cp.wait_send()
            hr = grads[w].shape[0] // 2
            other = outs[w].at[pl.ds((1 - c) * hr, hr), :]
            _remote(other, other, send_sems.at[w], recv_sems.at[w], (x, y, 1 - c)).wait_recv()

    return pl.pallas_call(
        body, name=name,
        in_specs=[_ANY] * n, out_specs=[_ANY] * n,
        out_shape=[jax.ShapeDtypeStruct(g.shape, g.dtype) for g in grads],
        input_output_aliases={w: w for w in range(n)},
        scratch_shapes=[pltpu.SemaphoreType.DMA((n,)), pltpu.SemaphoreType.DMA((n,))],
    )(*grads)


def _all_reduce_small(rows, mats):
    n_dev = 8
    n_rows = -(-len(rows) // 8) * 8
    heights = [math.prod(a.shape[:-1]) for a in mats]
    n_tall = -(-sum(heights) // 8) * 8
    arrays = list(rows) + list(mats)

    def body(*refs):
        ins, outs = refs[:len(arrays)], refs[len(arrays):2 * len(arrays)]
        wide, tall, wide_slots, tall_slots, send_sems, recv_sems = refs[2 * len(arrays):]
        x, y, c = _mesh_pos()
        me = 4 * x + 2 * y + c
        wide[...] = jnp.zeros_like(wide)
        tall[...] = jnp.zeros_like(tall)
        for k, a in enumerate(rows):
            wide[k:k + 1, 0:a.shape[1]] = ins[k][...]
        at = 0
        for k, h in enumerate(heights):
            tall[at:at + h, :] = ins[len(rows) + k][...].reshape(h, LANES)
            at += h
        wide_slots[0] = wide[...]
        tall_slots[0] = tall[...]
        cps = []
        for q in range(1, n_dev):
            peer = (x ^ (q >> 2), y ^ ((q >> 1) & 1), c ^ (q & 1))
            for j, (buf, slots) in enumerate(((wide, wide_slots), (tall, tall_slots))):
                s = 2 * (q - 1) + j
                cp = _remote(buf, slots.at[q], send_sems.at[s], recv_sems.at[s], peer)
                cp.start()
                cps.append(cp)
        for cp in cps:
            cp.wait()
        wide_sum, tall_sum = wide_slots[me], tall_slots[me]
        for d in range(1, n_dev):
            wide_sum = wide_sum + wide_slots[d ^ me]
            tall_sum = tall_sum + tall_slots[d ^ me]
        for k, a in enumerate(rows):
            outs[k][...] = wide_sum[k:k + 1, 0:a.shape[1]]
        at = 0
        for k, h in enumerate(heights):
            outs[len(rows) + k][...] = tall_sum[at:at + h, :].reshape(mats[k].shape)
            at += h

    vmem = pl.BlockSpec(memory_space=pltpu.VMEM)
    return pl.pallas_call(
        body, name="all_reduce_small",
        in_specs=[vmem] * len(arrays), out_specs=[vmem] * len(arrays),
        out_shape=[jax.ShapeDtypeStruct(a.shape, F32) for a in arrays],
        scratch_shapes=[pltpu.VMEM((n_rows, D_MODEL), F32), pltpu.VMEM((n_tall, LANES), F32),
                        pltpu.VMEM((n_dev, n_rows, D_MODEL), F32), pltpu.VMEM((n_dev, n_tall, LANES), F32),
                        pltpu.SemaphoreType.DMA((2 * (n_dev - 1),)), pltpu.SemaphoreType.DMA((2 * (n_dev - 1),))],
    )(*arrays)


def _adamw_small(items):
    n = len(items)
    bias1 = 1.0 - ADAM_B1 ** ADAM_STEP
    bias2 = 1.0 - ADAM_B2 ** ADAM_STEP

    def body(*refs):
        ins, outs = refs[:4 * n], refs[4 * n:]
        for k in range(n):
            w_ref, g_ref, m_ref, v_ref = ins[4 * k:4 * k + 4]
            gg = g_ref[...]
            m2 = ADAM_B1 * m_ref[...] + (1.0 - ADAM_B1) * gg
            v2 = ADAM_B2 * v_ref[...] + (1.0 - ADAM_B2) * (gg * gg)
            outs[3 * k + 1][...] = m2
            outs[3 * k + 2][...] = v2
            outs[3 * k][...] = -ADAM_LR * ((m2 / bias1) / (jnp.sqrt(v2 / bias2) + ADAM_EPS) + ADAM_WD * w_ref[...])

    vmem = pl.BlockSpec(memory_space=pltpu.VMEM)
    out = pl.pallas_call(
        body, name="adamw_small", in_specs=[vmem] * (4 * n), out_specs=[vmem] * (3 * n),
        out_shape=[jax.ShapeDtypeStruct(w.shape, F32) for w, _, _, _ in items for _ in range(3)],
    )(*[a for item in items for a in item])
    return [tuple(out[3 * k:3 * k + 3]) for k in range(n)]


ADAMW_STEPS = 8


def _adamw(items, name, exchange=None):
    n = len(items)
    bias1 = 1.0 - ADAM_B1 ** ADAM_STEP
    bias2 = 1.0 - ADAM_B2 ** ADAM_STEP

    def body(*refs):
        ins, outs = refs[:4 * n], refs[4 * n:]
        for k in range(n):
            w_ref, g_ref, m_ref, v_ref = ins[4 * k:4 * k + 4]
            d_ref, mo_ref, vo_ref = outs[3 * k:3 * k + 3]
            gg = g_ref[...]
            m2 = ADAM_B1 * m_ref[...] + (1.0 - ADAM_B1) * gg
            v2 = ADAM_B2 * v_ref[...] + (1.0 - ADAM_B2) * (gg * gg)
            mo_ref[...] = m2
            vo_ref[...] = v2
            d_ref[...] = -ADAM_LR * ((m2 / bias1) / (jnp.sqrt(v2 / bias2) + ADAM_EPS) + ADAM_WD * w_ref[...])

    in_specs, out_specs, out_shape, args = [], [], [], []
    for w, g, m, v in items:
        r, c = w.shape
        steps = ADAMW_STEPS if r % (8 * ADAMW_STEPS) == 0 else 1
        assert steps == ADAMW_STEPS or n == 1
        spec = pl.BlockSpec((r // steps, c), lambda i: (i, 0))
        in_specs += [spec] * 4
        out_specs += [spec] * 3
        out_shape += [jax.ShapeDtypeStruct((r, c), F32)] * 3
        args += [w, g, m, v]
    out, got = _call(body, args, name=name, grid=(steps,), in_specs=in_specs, out_specs=out_specs, out_shape=out_shape,
                     exchange=exchange, steps=(0, steps - 1))
    return [tuple(out[3 * k:3 * k + 3]) for k in range(n)], got


def kernel(x, p, ffn1_norm, ffn1_w_in, ffn1_w_out, mix_norm, w_mix_in, gmlp_v_norm, gmlp_w_s, gmlp_b, w_mix_out, ffn2_norm, ffn2_w_in, ffn2_w_out, ple_norm, ple_w_gate, ple_w_proj, final_norm, loss_target, m_ffn1_norm, m_ffn1_w_in, m_ffn1_w_out, m_mix_norm, m_w_mix_in, m_gmlp_v_norm, m_gmlp_w_s, m_gmlp_b, m_w_mix_out, m_ffn2_norm, m_ffn2_w_in, m_ffn2_w_out, m_ple_norm, m_ple_w_gate, m_ple_w_proj, m_final_norm, v_ffn1_norm, v_ffn1_w_in, v_ffn1_w_out, v_mix_norm, v_w_mix_in, v_gmlp_v_norm, v_gmlp_w_s, v_gmlp_b, v_w_mix_out, v_ffn2_norm, v_ffn2_w_in, v_ffn2_w_out, v_ple_norm, v_ple_w_gate, v_ple_w_proj, v_final_norm):
    args = dict(locals())
    w = {n: args[n] for n in _ALL}
    m = {n: args["m_" + n] for n in _ALL}
    v = {n: args["v_" + n] for n in _ALL}
    xi, yi, ci = _mesh_pos()
    pos = jnp.stack([2 * xi + yi, ci]).astype(jnp.int32)
    shard = {n: w[n][0] for n in _BIG}
    cast = {n: shard[n].astype(BF16) for n in _BIG}
    small = {n: (w[n][0] if w[n].ndim > 2 else w[n].reshape(1, -1)) for n in _SMALL}
    bt = small["gmlp_b"].T
    g_small, pair, from_chips = {}, {}, {}

    def pair_reduce(partials, tag):
        names = list(partials)
        parts = [partials[n].reshape(N_CHIPS, *shard[n].shape) for n in names]
        got = _pair_exchange(parts, "grad_pair_exchange_" + tag)
        for n, g, a in zip(names, parts, got):
            pair[n] = _pair_sum(g, a, pos, "pair_sum_" + n)
        return names

    w1in, w1out = _run_exchange(_WeightGather([cast["ffn1_w_in"], cast["ffn1_w_out"]]), "gather_ffn1")
    w1out = w1out.reshape(D_FF, D_MODEL)
    (h1, gu1), (wmix, wmo) = _ffn_fwd(x[0], small["ffn1_norm"], w1in, w1out, "ffn1_fwd",
                                      _WeightGather([cast["w_mix_in"], cast["w_mix_out"]]))
    wmo = wmo.reshape(D_MODEL, D_MODEL)
    zg, qkv = _mix_in_fwd(h1, small["mix_norm"], wmix)
    gm = _gmlp_fwd(zg, small["gmlp_v_norm"], small["gmlp_w_s"], bt)
    (att, carries), (w2in, w2out, wg, wproj) = _attn_fwd(
        qkv, _WeightGather([cast["ffn2_w_in"], cast["ffn2_w_out"], cast["ple_w_gate"], cast["ple_w_proj"]]))
    w2out = w2out.reshape(D_FF, D_MODEL)
    wg = wg.reshape(D_MODEL, D_MODEL)
    mixed = jnp.concatenate([gm, att], axis=1)
    h2 = _matmul_residual(h1, mixed, wmo, "mix_out_fwd")
    (h3, gu2), _ = _ffn_fwd(h2, small["ffn2_norm"], w2in, w2out, "ffn2_fwd")
    loss_part, g_small["final_norm"], g_small["ple_norm"], dh3, dgp, dpp, n4, pb = _head(
        h3, p[0, 0], loss_target[0], small["ple_norm"], small["final_norm"], wg, wproj)

    part = {"ple_w_gate": _wgrad_rows(n4, dgp, N_CHIPS, "wgrad_ple_gate"),
            "ple_w_proj": _wgrad_cols(pb, dpp, N_CHIPS, "wgrad_ple_proj")}
    (dh2, dgu2, n3, act2, dhh3, g_small["ffn2_norm"]), _ = _ffn_bwd(dh3, h2, small["ffn2_norm"], gu2, w2in, w2out,
                                                                   "ffn2_bwd")
    part["ffn2_w_in"] = _wgrad_cols(n3, dgu2, N_CHIPS, "wgrad_ffn2_in")
    part["ffn2_w_out"] = _wgrad_rows(act2, dhh3, 2, "wgrad_ffn2_out")
    dmixed, dh2b = _matmul_nt_cast(dh2, wmo, "mix_out_bwd")
    part["w_mix_out"] = _wgrad_rows(mixed, dh2b, 2, "wgrad_mix_out")
    group = pair_reduce(part, "late")
    dzg, g_small["gmlp_w_s"], dbt, g_small["gmlp_v_norm"] = _gmlp_bwd(zg, dmixed, small["gmlp_v_norm"],
                                                                      small["gmlp_w_s"], bt)
    g_small["gmlp_b"] = dbt.T
    (dq, dk, dv), got = _attn_bwd(qkv, dmixed, carries, _ChipExchange([pair[n] for n in group]))
    from_chips.update(zip(group, got))

    dzmix = jnp.concatenate([dzg, dq, dk, dv], axis=1)
    dh1, n2, g_small["mix_norm"] = _norm_input_bwd(dh2, dzmix, wmix, h1, small["mix_norm"], "mix_in_bwd")
    group = pair_reduce({"w_mix_in": _wgrad_cols(n2, dzmix, N_CHIPS, "wgrad_mix_in")}, "mix")
    (dx, dgu1, n1, act1, dhh1, g_small["ffn1_norm"]), _ = _ffn_bwd(dh1, x[0], small["ffn1_norm"], gu1, w1in, w1out,
                                                                   "ffn1_bwd")

    g_out, got = _wgrad_rows(act1, dhh1, 2, "wgrad_ffn1_out", _ChipExchange([pair[n] for n in group]))
    from_chips.update(zip(group, got))
    pair_reduce({"ffn1_w_out": g_out}, "out")
    g_in, got = _wgrad_cols(n1, dgu1, N_CHIPS, "wgrad_ffn1_in", _ChipExchange([pair["ffn1_w_out"]]))
    from_chips["ffn1_w_out"] = got[0]
    pair_reduce({"ffn1_w_in": g_in}, "in")

    def finish(names, tag, exchange=None):
        halves = [_chip_sum(pair[n], from_chips[n], pos, "chip_sum_" + n) for n in names]
        full = _pair_share(halves, "grad_pair_share_" + tag)
        out, got = _adamw([(shard[n], g, m[n][0], v[n][0]) for n, g in zip(names, full)], "adamw_" + tag, exchange)
        for n, g, (d2, m2, v2) in zip(names, full, out):
            grads[n], delta[n], new_m[n], new_v[n] = g[None], d2[None], m2[None], v2[None]
        return got

    grads, delta, new_m, new_v = {}, {}, {}, {}
    got = finish([n for n in _BIG if n != "ffn1_w_in"], "most", _ChipExchange([pair["ffn1_w_in"]]))
    from_chips["ffn1_w_in"] = got[0]
    finish(["ffn1_w_in"], "last")

    rows = [n for n in _SMALL if g_small[n].shape[0] == 1]
    mats = [n for n in _SMALL if n not in rows]
    summed = _all_reduce_small([g_small[n] for n in rows] + [loss_part], [g_small[n] for n in mats])
    loss = summed[len(rows)][0, 0]
    g_sum = dict(zip(rows + mats, summed[:len(rows)] + summed[len(rows) + 1:]))
    like = lambda a, n: a[n].reshape(small[n].shape)
    out = _adamw_small([(small[n], g_sum[n], like(m, n), like(v, n)) for n in _SMALL])
    for n, (d2, m2, v2) in zip(_SMALL, out):
        grads[n], delta[n], new_m[n], new_v[n] = (a.reshape(w[n].shape) for a in (g_sum[n], d2, m2, v2))

    return (loss, dx[None], *[grads[n] for n in _ALL], *[delta[n] for n in _ALL], *[new_m[n] for n in _ALL],
            *[new_v[n] for n in _ALL])
```

```python
import functools
import math

import jax
import jax.numpy as jnp
from jax import lax
from jax.experimental import pallas as pl
from jax.experimental.pallas import tpu as pltpu

F32, BF16 = jnp.float32, jnp.bfloat16

D_MODEL = 1024
D_FF = 2816
FF_BLOCK = 2 * D_FF // 4
PLE_DIM = 256
CHUNK = 128
GM_HEADS = 4
GM_WIDTH = 512
SB_HEAD_DIM = 64
SB_WIDTH = 512
MIX_IN_WIDTH = 2 * GM_WIDTH + 3 * SB_WIDTH
MIX_BLOCK = MIX_IN_WIDTH // 4
EPS = 1e-6
N_CHIPS = 4
LANES = 128
ATT_BLOCK = 128
ATT_Q = 512
VMEM_LIMIT = 56 * 1024 * 1024

ADAM_LR, ADAM_B1, ADAM_B2, ADAM_EPS, ADAM_WD, ADAM_STEP = 0.001, 0.9, 0.999, 1e-08, 0.01, 10


def _dot(a, b):
    return jnp.dot(a, b, preferred_element_type=F32)


def _dot_nt(a, b):
    return lax.dot_general(a, b, (((1,), (1,)), ((), ())), preferred_element_type=F32)


def _dot_tn(a, b):
    return lax.dot_general(a, b, (((0,), (0,)), ((), ())), preferred_element_type=F32)


def _resident(shape):
    nd = len(shape)
    return pl.BlockSpec(shape, lambda *_: (0,) * nd, pipeline_mode=pl.Buffered(1))


def _rows(tm, width):
    return pl.BlockSpec((tm, width), lambda i: (i, 0))


def _params(n_axes=1):
    return pltpu.CompilerParams(dimension_semantics=("arbitrary",) * n_axes, vmem_limit_bytes=VMEM_LIMIT)


def _rstd(h):
    return lax.rsqrt(jnp.mean(h * h, axis=-1, keepdims=True) + EPS)


def _rms_bwd(dy, h, r, g):
    dyg = dy * g
    dh = r * dyg - h * (r * r * r) * jnp.mean(dyg * h, axis=-1, keepdims=True)
    return dh, dy * h * r


def _gelu(x):
    return 0.5 * x * (1.0 + lax.erf(x * (2.0 ** -0.5)))


def _gelu_grad(x):
    return 0.5 * (1.0 + lax.erf(x * (2.0 ** -0.5))) + x * jnp.exp(-0.5 * x * x) * ((2.0 * jnp.pi) ** -0.5)


def _token_tile(t):
    return min(256, t)


def _ffn_fwd(h, g, win, wout, name, exchange=None):
    t = h.shape[0]
    tm = _token_tile(t)

    def body(h_ref, g_ref, win_ref, wout_ref, ho_ref, gu_ref):
        hh = h_ref[...]
        n = (hh * _rstd(hh) * g_ref[...]).astype(BF16)
        acc = jnp.zeros((tm, D_MODEL), F32)
        for jb in range(2):
            gate = _dot(n, win_ref[jb])
            up = _dot(n, win_ref[2 + jb])
            gu_ref[:, jb * FF_BLOCK:(jb + 1) * FF_BLOCK] = gate.astype(BF16)
            gu_ref[:, D_FF + jb * FF_BLOCK:D_FF + (jb + 1) * FF_BLOCK] = up.astype(BF16)
            act = (gate * jax.nn.sigmoid(gate) * up).astype(BF16)
            acc = acc + _dot(act, wout_ref[jb * FF_BLOCK:(jb + 1) * FF_BLOCK, :])
        ho_ref[...] = hh + 0.5 * acc

    n = t // tm
    return _call(
        body, (h, g, win, wout), name=name, grid=(n,),
        in_specs=[_rows(tm, D_MODEL), _resident((1, D_MODEL)), _resident(win.shape), _resident(wout.shape)],
        out_specs=[_rows(tm, D_MODEL), _rows(tm, 2 * D_FF)],
        out_shape=[jax.ShapeDtypeStruct((t, D_MODEL), F32), jax.ShapeDtypeStruct((t, 2 * D_FF), BF16)],
        exchange=exchange, steps=(0, (2 * n) // 3, n - 1))


def _ffn_bwd(dho, h, g, gu, win, wout, name, exchange=None):
    t = h.shape[0]
    tm = _token_tile(t)

    def body(dho_ref, h_ref, g_ref, gu_ref, win_ref, wout_ref, dh_ref, dgu_ref, n_ref, act_ref, dhh_ref, dg_ref):
        i = pl.program_id(0)
        hh = h_ref[...]
        gg = g_ref[...]
        r = _rstd(hh)
        n_ref[...] = (hh * r * gg).astype(BF16)
        dho = dho_ref[...]
        dhh = (0.5 * dho).astype(BF16)
        dhh_ref[...] = dhh
        dn = jnp.zeros((tm, D_MODEL), F32)
        for jb in range(2):
            cg = slice(jb * FF_BLOCK, (jb + 1) * FF_BLOCK)
            cu = slice(D_FF + jb * FF_BLOCK, D_FF + (jb + 1) * FF_BLOCK)
            dact = _dot_nt(dhh, wout_ref[cg, :])
            gate = gu_ref[:, cg].astype(F32)
            up = gu_ref[:, cu].astype(F32)
            sg = jax.nn.sigmoid(gate)
            silu = gate * sg
            act_ref[:, cg] = (silu * up).astype(BF16)
            dgate = (dact * up * (sg * (1.0 + gate * (1.0 - sg)))).astype(BF16)
            dup = (dact * silu).astype(BF16)
            dgu_ref[:, cg] = dgate
            dgu_ref[:, cu] = dup
            dn = dn + _dot_nt(dgate, win_ref[jb]) + _dot_nt(dup, win_ref[2 + jb])
        dh, dg_rows = _rms_bwd(dn, hh, r, gg)
        dh_ref[...] = dho + dh

        @pl.when(i == 0)
        def _():
            dg_ref[...] = jnp.zeros_like(dg_ref)

        dg_ref[...] += jnp.sum(dg_rows, axis=0, keepdims=True)

    n = t // tm
    return _call(
        body, (dho, h, g, gu, win, wout), name=name, grid=(n,),
        in_specs=[_rows(tm, D_MODEL), _rows(tm, D_MODEL), _resident((1, D_MODEL)), _rows(tm, 2 * D_FF),
                  _resident(win.shape), _resident(wout.shape)],
        out_specs=[_rows(tm, D_MODEL), _rows(tm, 2 * D_FF), _rows(tm, D_MODEL), _rows(tm, D_FF), _rows(tm, D_MODEL),
                   pl.BlockSpec((1, D_MODEL), lambda i: (0, 0))],
        out_shape=[jax.ShapeDtypeStruct((t, D_MODEL), F32), jax.ShapeDtypeStruct((t, 2 * D_FF), BF16),
                   jax.ShapeDtypeStruct((t, D_MODEL), BF16), jax.ShapeDtypeStruct((t, D_FF), BF16),
                   jax.ShapeDtypeStruct((t, D_MODEL), BF16), jax.ShapeDtypeStruct((1, D_MODEL), F32)],
        exchange=exchange, steps=(0, n - 1))


def _wgrad(a, b, out_shape, out_block, out_index, a_width, b_width, grid_ij, name, exchange=None):
    t = a.shape[0]
    tk = min(2048, t)
    nk = t // tk

    def body(a_ref, b_ref, o_ref, ob_ref):
        k = pl.program_id(2)
        prod = _dot_tn(a_ref[...], b_ref[...]).reshape(o_ref.shape)

        @pl.when(k == 0)
        def _():
            o_ref[...] = prod

        @pl.when(k > 0)
        def _():
            o_ref[...] += prod

        @pl.when(k == nk - 1)
        def _():
            ob_ref[...] = o_ref[...].astype(BF16)

    grid = (*grid_ij, nk)
    out_spec = pl.BlockSpec(out_block, lambda i, j, k: out_index(i, j))
    out, got = _call(
        body, (a, b), name=name, grid=grid,
        in_specs=[pl.BlockSpec((tk, a_width), lambda i, j, k: (k, i)), pl.BlockSpec((tk, b_width), lambda i, j, k: (k, j))],
        out_specs=[out_spec, out_spec],
        out_shape=[jax.ShapeDtypeStruct(out_shape, F32), jax.ShapeDtypeStruct(out_shape, BF16)],
        exchange=exchange, steps=(0, grid[0] * grid[1] * grid[2] - 1))
    return tuple(out) if exchange is None else (tuple(out), got)


def _wgrad_cols(a, b, n_blocks, name, exchange=None):
    ka, nb = a.shape[1], b.shape[1] // n_blocks
    return _wgrad(a, b, (n_blocks, ka, nb), (1, ka, nb), lambda i, j: (j, 0, 0), ka, nb, (1, n_blocks), name, exchange)


def _wgrad_rows(a, b, n_blocks, name, exchange=None):
    ka, nb = a.shape[1] // n_blocks, b.shape[1]
    return _wgrad(a, b, (a.shape[1], nb), (ka, nb), lambda i, j: (i, 0), ka, nb, (n_blocks, 1), name, exchange)


def _mix_in_fwd(h, g, wmix):
    t = h.shape[0]
    tm = _token_tile(t)
    gw2 = 2 * GM_WIDTH

    def body(h_ref, g_ref, w_ref, zg_ref, qkv_ref):
        hh = h_ref[...]
        n = (hh * _rstd(hh) * g_ref[...]).astype(BF16)
        for b in range(N_CHIPS):
            z = _dot(n, w_ref[b])
            lo, hi = b * MIX_BLOCK, (b + 1) * MIX_BLOCK
            if hi <= gw2:
                zg_ref[:, lo:hi] = z
            elif lo >= gw2:
                qkv_ref[:, lo - gw2:hi - gw2] = z.astype(BF16)
            else:
                zg_ref[:, lo:gw2] = z[:, :gw2 - lo]
                qkv_ref[:, 0:hi - gw2] = z[:, gw2 - lo:].astype(BF16)

    return pl.pallas_call(
        body, name="mix_in_fwd", grid=(t // tm,),
        in_specs=[_rows(tm, D_MODEL), _resident((1, D_MODEL)), _resident(wmix.shape)],
        out_specs=[_rows(tm, gw2), _rows(tm, 3 * SB_WIDTH)],
        out_shape=[jax.ShapeDtypeStruct((t, gw2), F32), jax.ShapeDtypeStruct((t, 3 * SB_WIDTH), BF16)],
        compiler_params=_params(),
    )(h, g, wmix)


def _causal_chunk_mask():
    row = lax.broadcasted_iota(jnp.int32, (CHUNK, CHUNK), 0)
    col = lax.broadcasted_iota(jnp.int32, (CHUNK, CHUNK), 1)
    return row >= col


def _gmlp_tile(t):
    return min(512, t)


def _gmlp_fwd(zg, gv, ws, bt):
    t = zg.shape[0]
    tm = _gmlp_tile(t)

    def body(zg_ref, gv_ref, ws_ref, bt_ref, o_ref):
        u = _gelu(zg_ref[:, :GM_WIDTH])
        v = _gelu(zg_ref[:, GM_WIDTH:])
        vn = (v * _rstd(v) * gv_ref[...]).astype(BF16)
        mask = _causal_chunk_mask()
        for hd in range(GM_HEADS):
            wm = jnp.where(mask, ws_ref[hd], 0.0).astype(BF16)
            cols = slice(hd * CHUNK, (hd + 1) * CHUNK)
            for c in range(tm // CHUNK):
                rows = slice(c * CHUNK, (c + 1) * CHUNK)
                sv = _dot(wm, vn[rows, cols]) + bt_ref[:, hd:hd + 1]
                o_ref[rows, cols] = (u[rows, cols] * sv).astype(BF16)

    return pl.pallas_call(
        body, name="gmlp_fwd", grid=(t // tm,),
        in_specs=[_rows(tm, 2 * GM_WIDTH), _resident((1, GM_WIDTH)), _resident(ws.shape), _resident(bt.shape)],
        out_specs=_rows(tm, GM_WIDTH),
        out_shape=jax.ShapeDtypeStruct((t, GM_WIDTH), BF16),
        compiler_params=_params(),
    )(zg, gv, ws, bt)


def _gmlp_bwd(zg, dmixed, gv, ws, bt):
    t = zg.shape[0]
    tm = _gmlp_tile(t)

    def body(zg_ref, dgm_ref, gv_ref, ws_ref, bt_ref, dzg_ref, dws_ref, dbt_ref, dgv_ref):
        i = pl.program_id(0)

        @pl.when(i == 0)
        def _():
            dws_ref[...] = jnp.zeros_like(dws_ref)
            dbt_ref[...] = jnp.zeros_like(dbt_ref)
            dgv_ref[...] = jnp.zeros_like(dgv_ref)

        zu = zg_ref[:, :GM_WIDTH]
        zv = zg_ref[:, GM_WIDTH:]
        u = _gelu(zu)
        v = _gelu(zv)
        r = _rstd(v)
        gvv = gv_ref[...]
        vn = (v * r * gvv).astype(BF16)
        dgm = dgm_ref[...].astype(F32)
        dsv = (dgm * u).astype(BF16)
        mask = _causal_chunk_mask()
        du_cols, dvn_cols = [], []
        for hd in range(GM_HEADS):
            wm = jnp.where(mask, ws_ref[hd], 0.0).astype(BF16)
            cols = slice(hd * CHUNK, (hd + 1) * CHUNK)
            dw = jnp.zeros((CHUNK, CHUNK), F32)
            db = jnp.zeros((CHUNK, 1), F32)
            du_rows, dvn_rows = [], []
            for c in range(tm // CHUNK):
                rows = slice(c * CHUNK, (c + 1) * CHUNK)
                sv = _dot(wm, vn[rows, cols]) + bt_ref[:, hd:hd + 1]
                du_rows.append(dgm[rows, cols] * sv)
                dvn_rows.append(_dot_tn(wm, dsv[rows, cols]))
                dw = dw + _dot_nt(dsv[rows, cols], vn[rows, cols])
                db = db + jnp.sum(dsv[rows, cols].astype(F32), axis=1, keepdims=True)
            dws_ref[hd] += jnp.where(mask, dw, 0.0)
            dbt_ref[:, hd:hd + 1] += db
            du_cols.append(jnp.concatenate(du_rows, axis=0))
            dvn_cols.append(jnp.concatenate(dvn_rows, axis=0))
        du = jnp.concatenate(du_cols, axis=1)
        dvn = jnp.concatenate(dvn_cols, axis=1)
        dv, dgv_rows = _rms_bwd(dvn, v, r, gvv)
        dgv_ref[...] += jnp.sum(dgv_rows, axis=0, keepdims=True)
        dzg_ref[:, :GM_WIDTH] = (du * _gelu_grad(zu)).astype(BF16)
        dzg_ref[:, GM_WIDTH:] = (dv * _gelu_grad(zv)).astype(BF16)

    const = lambda nd: (lambda i: (0,) * nd)
    return pl.pallas_call(
        body, name="gmlp_bwd", grid=(t // tm,),
        in_specs=[_rows(tm, 2 * GM_WIDTH), _rows(tm, GM_WIDTH), _resident((1, GM_WIDTH)), _resident(ws.shape),
                  _resident(bt.shape)],
        out_specs=[_rows(tm, 2 * GM_WIDTH), pl.BlockSpec(ws.shape, const(3)), pl.BlockSpec(bt.shape, const(2)),
                   pl.BlockSpec((1, GM_WIDTH), const(2))],
        out_shape=[jax.ShapeDtypeStruct((t, 2 * GM_WIDTH), BF16), jax.ShapeDtypeStruct(ws.shape, F32),
                   jax.ShapeDtypeStruct(bt.shape, F32), jax.ShapeDtypeStruct((1, GM_WIDTH), F32)],
        compiler_params=_params(),
    )(zg, dmixed, gv, ws, bt)


def _att_masks():
    tb = ATT_BLOCK
    lane = lax.broadcasted_iota(jnp.int32, (1, LANES), 1)
    rj = lax.broadcasted_iota(jnp.int32, (2 * tb, 2 * tb), 0)
    cs = lax.broadcasted_iota(jnp.int32, (2 * tb, 2 * tb), 1)
    same_head = ((rj < tb) & (cs < tb)) | ((rj >= tb) & (cs >= tb))
    suffix = jnp.where(same_head & (rj >= cs), 1.0, 0.0).astype(BF16)
    prefix = jnp.where(same_head & (rj <= cs), 1.0, 0.0).astype(BF16)
    left = lax.broadcasted_iota(jnp.int32, (1, 2 * tb), 1) < tb
    tq = lax.broadcasted_iota(jnp.int32, (ATT_Q, 4 * tb), 0)
    ts = lax.broadcasted_iota(jnp.int32, (ATT_Q, 4 * tb), 1)
    key = jnp.where(ts < 2 * tb, ts & (tb - 1), (ts & (tb - 1)) + tb)
    return lane, suffix, prefix, left, key, tq


def _att_fill(k_ref, v_ref, kcat, vcat, n_blocks, lane):
    tb = ATT_BLOCK
    first = lane < SB_HEAD_DIM

    def fill(jb, carry):
        rows = pl.ds(pl.multiple_of(jb * tb, tb), tb)
        top = pl.ds(pl.multiple_of(jb * 2 * tb, tb), tb)
        bot = pl.ds(pl.multiple_of(jb * 2 * tb + tb, tb), tb)
        kb = k_ref[rows, :]
        vb = v_ref[rows, :]
        zero = jnp.zeros_like(kb)
        kcat[top, :] = jnp.where(first, kb, zero)
        kcat[bot, :] = jnp.where(first, zero, kb)
        vcat[top, :] = jnp.where(first, vb, zero)
        vcat[bot, :] = jnp.where(first, zero, vb)
        return carry

    lax.fori_loop(0, n_blocks, fill, 0)


def _block_sums(x, m):
    return _dot(x.astype(BF16), m)


def _softplus2(z2):
    return jnp.maximum(z2, 0.0) + jnp.log2(1.0 + jnp.exp2(-jnp.abs(z2)))


def _scaled_queries(q_ref, base2):
    scale = SB_HEAD_DIM ** -0.5
    return (q_ref[...].astype(F32) * (scale * math.log2(math.e) if base2 else scale)).astype(BF16)


def _att_specs(t):
    n_pairs = SB_WIDTH // LANES
    q_spec = pl.BlockSpec((ATT_Q, LANES), lambda p, i: (i, p))
    k_spec = pl.BlockSpec((t, LANES), lambda p, i: (0, n_pairs + p))
    v_spec = pl.BlockSpec((t, LANES), lambda p, i: (0, 2 * n_pairs + p))
    return n_pairs, q_spec, k_spec, v_spec


def _attn_fwd(qkv, exchange=None):
    t = qkv.shape[0]
    tb = ATT_BLOCK
    nkb = t // tb
    assert 2 * nkb <= LANES and t % ATT_Q == 0 and ATT_Q == 4 * tb
    n_pairs, q_spec, k_spec, v_spec = _att_specs(t)

    def body(q_ref, k_ref, v_ref, o_ref, ct_ref, kcat, vcat, acc, carry, z0, r0, z1, r1):
        i = pl.program_id(1)
        lane, suffix, _, left, key, tq = _att_masks()

        @pl.when(i == 0)
        def _():
            _att_fill(k_ref, v_ref, kcat, vcat, nkb, lane)

        q = _scaled_queries(q_ref, True)
        acc[...] = jnp.zeros_like(acc)
        carry[...] = jnp.zeros_like(carry)
        ct_ref[0] = jnp.zeros((ATT_Q, LANES), F32)

        def key_rows(m):
            return pl.ds(pl.multiple_of(m * 4 * tb, 4 * tb), 4 * tb)

        def scores(m, zb, rb, causal=None, rs=slice(None)):
            z = _dot_nt(q[rs], kcat[key_rows(m), :])
            zb[rs, :] = z
            sp = _softplus2(z)
            if causal is not None:
                sp = jnp.where(causal[rs], sp, 0.0)
            for g in (1, 0):
                cols = slice(g * 2 * tb, (g + 1) * 2 * tb)
                rb[rs, cols] = _block_sums(sp[:, cols], suffix)

        def weigh(m, zb, rb, causal=None, rs=slice(None)):
            probs = [None, None]
            for g in (1, 0):
                cols = slice(g * 2 * tb, (g + 1) * 2 * tb)
                j = 2 * m + g
                r = rb[rs, cols]
                c = carry[rs, :]
                ct_ref[0, rs, :] = jnp.where(lane == j, c[:, :tb], jnp.where(lane == nkb + j, c[:, tb:], ct_ref[0, rs, :]))
                a = jnp.exp2(zb[rs, cols] - (r + c))
                if causal is not None:
                    a = jnp.where(causal[rs, cols], a, 0.0)
                probs[g] = a.astype(BF16)
                carry[rs, :] = c + jnp.where(left, r[:, 0:1], r[:, tb:tb + 1])
            acc[rs, :] += _dot(jnp.concatenate(probs, axis=1), vcat[key_rows(m), :])

        sooner, later, late_rows = key < tq, key + 2 * tb < tq, slice(2 * tb, 4 * tb)
        scores(2 * i + 1, z1, r1, later, late_rows)
        scores(2 * i, z0, r0, sooner)
        weigh(2 * i + 1, z1, r1, later, late_rows)
        weigh(2 * i, z0, r0, sooner)

        @pl.when(i > 0)
        def _():
            scores(2 * i - 1, z1, r1)

            def loop(k, c):
                u = i - 1 - k
                scores(2 * u, z0, r0)
                weigh(2 * u + 1, z1, r1)
                scores(jnp.maximum(2 * u - 1, 0), z1, r1)
                weigh(2 * u, z0, r0)
                return c

            lax.fori_loop(0, i, loop, 0)

        o_ref[...] = acc[...].astype(BF16)

    tile = pltpu.VMEM((ATT_Q, 4 * tb), F32)

    nq = t // ATT_Q
    return _call(
        body, (qkv, qkv, qkv), name="attn_fwd", grid=(n_pairs, nq),
        in_specs=[q_spec, k_spec, v_spec],
        out_specs=[pl.BlockSpec((ATT_Q, LANES), lambda p, i: (i, p)), pl.BlockSpec((1, ATT_Q, LANES), lambda p, i: (p, i, 0))],
        out_shape=[jax.ShapeDtypeStruct((t, SB_WIDTH), BF16), jax.ShapeDtypeStruct((n_pairs, t, LANES), F32)],
        scratch_shapes=[pltpu.VMEM((2 * t, LANES), BF16), pltpu.VMEM((2 * t, LANES), BF16),
                        pltpu.VMEM((ATT_Q, LANES), F32), pltpu.VMEM((ATT_Q, 2 * tb), F32), tile, tile, tile, tile],
        exchange=exchange, steps=(0, (n_pairs - 1) * nq - 1, n_pairs * nq - 1))


def _attn_bwd(qkv, dmixed, carries, exchange=None):
    t = qkv.shape[0]
    tb = ATT_BLOCK
    nkb = t // tb
    nq = t // ATT_Q
    scale = SB_HEAD_DIM ** -0.5
    n_pairs, q_spec, k_spec, v_spec = _att_specs(t)
    gm_blocks = GM_WIDTH // LANES

    def body(q_ref, k_ref, v_ref, do_ref, ct_ref, dq_ref, dk_ref, dv_ref, kcat, vcat, dkacc, dvacc, dqacc, carry,
             z0, r0, s0, a0, z1, r1, s1, a1):
        i = pl.program_id(1)
        lane, suffix, prefix, left, key, tq = _att_masks()
        first = lane < SB_HEAD_DIM

        @pl.when(i == 0)
        def _():
            _att_fill(k_ref, v_ref, kcat, vcat, nkb, lane)
            dkacc[...] = jnp.zeros_like(dkacc)
            dvacc[...] = jnp.zeros_like(dvacc)

        q2 = _scaled_queries(q_ref, True)
        q = _scaled_queries(q_ref, False)
        do = do_ref[...]
        dqacc[...] = jnp.zeros_like(dqacc)
        carry[...] = jnp.zeros_like(carry)

        def key_rows(m):
            return pl.ds(pl.multiple_of(m * 4 * tb, 4 * tb), 4 * tb)

        def front(m, bufs, causal=None, rs=slice(None)):
            zb, rb, sb, ab = bufs
            z = _dot_nt(q2[rs], kcat[key_rows(m), :])
            zb[rs, :] = z
            sp = _softplus2(z)
            sb[rs, :] = jnp.exp2(z - sp)
            if causal is not None:
                sp = jnp.where(causal[rs], sp, 0.0)
            for g in (0, 1):
                cols = slice(g * 2 * tb, (g + 1) * 2 * tb)
                rb[rs, cols] = _block_sums(sp[:, cols], suffix)
            ab[rs, :] = _dot_nt(do[rs], vcat[key_rows(m), :])

        def back(m, bufs, causal=None, rs=slice(None)):
            zb, rb, sb, ab = bufs
            dzs, probs = [None, None], [None, None]
            for g in (0, 1):
                cols = slice(g * 2 * tb, (g + 1) * 2 * tb)
                j = 2 * m + g
                ct = ct_ref[0, rs, :]
                ca = jnp.sum(jnp.where(lane == j, ct, 0.0), axis=1, keepdims=True)
                cb = jnp.sum(jnp.where(lane == nkb + j, ct, 0.0), axis=1, keepdims=True)
                a = jnp.exp2(zb[rs, cols] - (rb[rs, cols] + jnp.where(left, ca, cb)))
                if causal is not None:
                    a = jnp.where(causal[rs, cols], a, 0.0)
                de = ab[rs, cols] * a
                cl = _block_sums(de, prefix)
                pre = carry[rs, :]
                dz = de - sb[rs, cols] * (cl + pre)
                if causal is not None:
                    dz = jnp.where(causal[rs, cols], dz, 0.0)
                carry[rs, :] = pre + jnp.where(left, cl[:, tb - 1:tb], cl[:, 2 * tb - 1:2 * tb])
                dzs[g] = dz.astype(BF16)
                probs[g] = a.astype(BF16)
            dzb = jnp.concatenate(dzs, axis=1)
            dqacc[rs, :] += _dot(dzb, kcat[key_rows(m), :])
            dkc = _dot_tn(dzb, q[rs])
            dvc = _dot_tn(jnp.concatenate(probs, axis=1), do[rs])
            out_rows = pl.ds(pl.multiple_of(m * 2 * tb, 2 * tb), 2 * tb)
            pick = lambda x: jnp.concatenate([jnp.where(first, x[0:tb], x[tb:2 * tb]),
                                              jnp.where(first, x[2 * tb:3 * tb], x[3 * tb:4 * tb])], axis=0)
            dkacc[out_rows, :] += pick(dkc)
            dvacc[out_rows, :] += pick(dvc)

        b0, b1 = (z0, r0, s0, a0), (z1, r1, s1, a1)

        @pl.when(i > 0)
        def _():
            front(0, b0)

            def loop(u, c):
                front(2 * u + 1, b1)
                back(2 * u, b0)
                front(jnp.minimum(2 * u + 2, 2 * i - 1), b0)
                back(2 * u + 1, b1)
                return c

            lax.fori_loop(0, i, loop, 0)

        sooner, later, late_rows = key < tq, key + 2 * tb < tq, slice(2 * tb, 4 * tb)
        front(2 * i, b0, sooner)
        front(2 * i + 1, b1, later, late_rows)
        back(2 * i, b0, sooner)
        back(2 * i + 1, b1, later, late_rows)

        dq_ref[...] = (dqacc[...] * scale).astype(BF16)

        @pl.when(i == nq - 1)
        def _():
            dk_ref[...] = dkacc[...].astype(BF16)
            dv_ref[...] = dvacc[...].astype(BF16)

    col = pl.BlockSpec((t, LANES), lambda p, i: (0, p))
    out = jax.ShapeDtypeStruct((t, SB_WIDTH), BF16)
    tile = pltpu.VMEM((ATT_Q, 4 * tb), F32)
    return _call(
        body, (qkv, qkv, qkv, dmixed, carries), name="attn_bwd", grid=(n_pairs, nq),
        in_specs=[q_spec, k_spec, v_spec, pl.BlockSpec((ATT_Q, LANES), lambda p, i: (i, gm_blocks + p)),
                  pl.BlockSpec((1, ATT_Q, LANES), lambda p, i: (p, i, 0))],
        out_specs=[pl.BlockSpec((ATT_Q, LANES), lambda p, i: (i, p)), col, col],
        out_shape=[out, out, out],
        scratch_shapes=[pltpu.VMEM((2 * t, LANES), BF16), pltpu.VMEM((2 * t, LANES), BF16),
                        pltpu.VMEM((t, LANES), F32), pltpu.VMEM((t, LANES), F32),
                        pltpu.VMEM((ATT_Q, LANES), F32), pltpu.VMEM((ATT_Q, 2 * tb), F32)] + [tile] * 8,
        exchange=exchange, steps=(0, n_pairs * nq - 1))


def _matmul_residual(res, a, w, name):
    t = a.shape[0]
    tm = _token_tile(t)

    def body(res_ref, a_ref, w_ref, o_ref):
        o_ref[...] = res_ref[...] + _dot(a_ref[...], w_ref[...])

    return pl.pallas_call(
        body, name=name, grid=(t // tm,),
        in_specs=[_rows(tm, res.shape[1]), _rows(tm, a.shape[1]), _resident(w.shape)],
        out_specs=_rows(tm, res.shape[1]),
        out_shape=jax.ShapeDtypeStruct(res.shape, F32),
        compiler_params=_params(),
    )(res, a, w)


def _matmul_nt_cast(dy, w, name):
    t = dy.shape[0]
    tm = _token_tile(t)

    def body(dy_ref, w_ref, o_ref, dyb_ref):
        dyb = dy_ref[...].astype(BF16)
        dyb_ref[...] = dyb
        o_ref[...] = _dot_nt(dyb, w_ref[...]).astype(BF16)

    return pl.pallas_call(
        body, name=name, grid=(t // tm,),
        in_specs=[_rows(tm, dy.shape[1]), _resident(w.shape)],
        out_specs=[_rows(tm, w.shape[0]), _rows(tm, dy.shape[1])],
        out_shape=[jax.ShapeDtypeStruct((t, w.shape[0]), BF16), jax.ShapeDtypeStruct(dy.shape, BF16)],
        compiler_params=_params(),
    )(dy, w)


def _norm_input_bwd(dres, dz, w, h, g, name):
    t = h.shape[0]
    tm = _token_tile(t)
    nb, _, width = w.shape

    def body(dres_ref, dz_ref, w_ref, h_ref, g_ref, dh_ref, n_ref, dg_ref):
        i = pl.program_id(0)
        hh = h_ref[...]
        gg = g_ref[...]
        r = _rstd(hh)
        n_ref[...] = (hh * r * gg).astype(BF16)
        dn = jnp.zeros((tm, D_MODEL), F32)
        for b in range(nb):
            dn = dn + _dot_nt(dz_ref[:, b * width:(b + 1) * width], w_ref[b])
        dh, dg_rows = _rms_bwd(dn, hh, r, gg)
        dh_ref[...] = dres_ref[...] + dh

        @pl.when(i == 0)
        def _():
            dg_ref[...] = jnp.zeros_like(dg_ref)

        dg_ref[...] += jnp.sum(dg_rows, axis=0, keepdims=True)

    return pl.pallas_call(
        body, name=name, grid=(t // tm,),
        in_specs=[_rows(tm, D_MODEL), _rows(tm, nb * width), _resident(w.shape), _rows(tm, D_MODEL),
                  _resident((1, D_MODEL))],
        out_specs=[_rows(tm, D_MODEL), _rows(tm, D_MODEL), pl.BlockSpec((1, D_MODEL), lambda i: (0, 0))],
        out_shape=[jax.ShapeDtypeStruct((t, D_MODEL), F32), jax.ShapeDtypeStruct((t, D_MODEL), BF16),
                   jax.ShapeDtypeStruct((1, D_MODEL), F32)],
        compiler_params=_params(),
    )(dres, dz, w, h, g)


def _head(h, p, target, gple, gfin, wg, wproj):
    t = h.shape[0]
    tm = _token_tile(t)
    pw = D_MODEL // N_CHIPS

    def body(h_ref, p_ref, tgt_ref, gple_ref, gfin_ref, wg_ref, wproj_ref,
             loss_ref, dgf_ref, dgple_ref, dh_ref, dgp_ref, dpp_ref, n_ref, pb_ref):
        i = pl.program_id(0)
        hh = h_ref[...]
        r_in = _rstd(hh)
        gp = gple_ref[...]
        n = (hh * r_in * gp).astype(BF16)
        n_ref[...] = n
        gate = jax.nn.sigmoid(_dot(n, wg_ref[...]))
        pb = p_ref[...].astype(BF16)
        pb_ref[...] = pb
        pp = jnp.concatenate([_dot(pb, wproj_ref[b]) for b in range(N_CHIPS)], axis=1)
        h4 = hh + gate * pp
        r = _rstd(h4)
        gf = gfin_ref[...]
        err = h4 * r * gf - tgt_ref[...]
        dy = err * (1.0 / D_MODEL)
        dh4, dgf_rows = _rms_bwd(dy, h4, r, gf)
        dgp = (dh4 * pp * gate * (1.0 - gate)).astype(BF16)
        dgp_ref[...] = dgp
        dpp_ref[...] = (dh4 * gate).astype(BF16)
        dh, dgple_rows = _rms_bwd(_dot_nt(dgp, wg_ref[...]), hh, r_in, gp)
        dh_ref[...] = dh4 + dh

        @pl.when(i == 0)
        def _():
            loss_ref[...] = jnp.zeros_like(loss_ref)
            dgf_ref[...] = jnp.zeros_like(dgf_ref)
            dgple_ref[...] = jnp.zeros_like(dgple_ref)

        loss_ref[...] += (0.5 / D_MODEL) * jnp.sum(err * err)
        dgf_ref[...] += jnp.sum(dgf_rows, axis=0, keepdims=True)
        dgple_ref[...] += jnp.sum(dgple_rows, axis=0, keepdims=True)

    bf = lambda w: jax.ShapeDtypeStruct((t, w), BF16)
    const = lambda i: (0, 0)
    return pl.pallas_call(
        body, name="head", grid=(t // tm,),
        in_specs=[_rows(tm, D_MODEL), _rows(tm, PLE_DIM), _rows(tm, D_MODEL), _resident((1, D_MODEL)),
                  _resident((1, D_MODEL)), _resident(wg.shape), _resident(wproj.shape)],
        out_specs=[pl.BlockSpec((1, LANES), const), pl.BlockSpec((1, D_MODEL), const), pl.BlockSpec((1, D_MODEL), const),
                   _rows(tm, D_MODEL), _rows(tm, D_MODEL), _rows(tm, D_MODEL), _rows(tm, D_MODEL), _rows(tm, PLE_DIM)],
        out_shape=[jax.ShapeDtypeStruct((1, LANES), F32), jax.ShapeDtypeStruct((1, D_MODEL), F32),
                   jax.ShapeDtypeStruct((1, D_MODEL), F32), jax.ShapeDtypeStruct((t, D_MODEL), F32), bf(D_MODEL),
                   bf(D_MODEL), bf(D_MODEL), bf(PLE_DIM)],
        compiler_params=_params(),
    )(h, p, target, gple, gfin, wg, wproj)


_BIG = ("ffn1_w_in", "ffn1_w_out", "w_mix_in", "w_mix_out", "ffn2_w_in", "ffn2_w_out", "ple_w_gate", "ple_w_proj")
_SMALL = ("ffn1_norm", "mix_norm", "gmlp_v_norm", "gmlp_w_s", "gmlp_b", "ffn2_norm", "ple_norm", "final_norm")
_ALL = ("ffn1_norm", "ffn1_w_in", "ffn1_w_out", "mix_norm", "w_mix_in", "gmlp_v_norm", "gmlp_w_s", "gmlp_b", "w_mix_out",
        "ffn2_norm", "ffn2_w_in", "ffn2_w_out", "ple_norm", "ple_w_gate", "ple_w_proj", "final_norm")
_ANY = pl.BlockSpec(memory_space=pl.ANY)
_MESH = pl.DeviceIdType.MESH


def _mesh_pos():
    return lax.axis_index("x"), lax.axis_index("y"), lax.axis_index("c")


def _other_chips(x, y):
    return [((x, 1 - y), 2 * x + 1 - y), ((1 - x, y), 2 * (1 - x) + y), ((1 - x, 1 - y), 2 * (1 - x) + 1 - y)]


def _remote(src, dst, send_sem, recv_sem, device):
    return pltpu.make_async_remote_copy(src_ref=src, dst_ref=dst, send_sem=send_sem, recv_sem=recv_sem,
                                        device_id=device, device_id_type=_MESH)


class _WeightGather:
    def __init__(self, shards):
        self.shapes = [s.shape for s in shards]
        self.operands = list(shards)
        self.out_shape = [jax.ShapeDtypeStruct((N_CHIPS, *s.shape), s.dtype) for s in shards]
        n = len(shards)
        self.per = 2 * (N_CHIPS - 1)
        self.scratch = [pltpu.SemaphoreType.DMA((self.per * n,)), pltpu.SemaphoreType.DMA((self.per * n,)),
                        pltpu.SemaphoreType.DMA((n,))]
        self.phases = [self.send, self.forward, self.finish]

    def _copies(self, ins, outs, sems):
        send_sems, recv_sems, local_sems = sems
        x, y, c = _mesh_pos()
        sibling = (x, y, 1 - c)
        mine = 2 * x + y
        local, first, landing, passed, arriving = [], [], [], [], []
        for w, shape in enumerate(self.shapes):
            hr = shape[0] // 2
            half = lambda blk, cc, w=w, hr=hr: outs[w].at[blk, pl.ds(cc * hr, hr), :]
            local.append(pltpu.make_async_copy(ins[w], outs[w].at[mine], local_sems.at[w]))
            for k, (chip, blk) in enumerate(_other_chips(x, y)):
                s = self.per * w + k
                first.append(_remote(ins[w].at[pl.ds(c * hr, hr), :], half(mine, c), send_sems.at[s], recv_sems.at[s],
                                     (*chip, c)))
                landing.append(_remote(half(blk, c), half(blk, c), send_sems.at[s], recv_sems.at[s], sibling))
                s = self.per * w + N_CHIPS - 1 + k
                passed.append(_remote(half(blk, c), half(blk, c), send_sems.at[s], recv_sems.at[s], sibling))
                arriving.append(_remote(half(blk, 1 - c), half(blk, 1 - c), send_sems.at[s], recv_sems.at[s], sibling))
        return local, first, landing, passed, arriving

    def send(self, ins, outs, sems):
        local, first, _, _, _ = self._copies(ins, outs, sems)
        for cp in local + first:
            cp.start()

    def forward(self, ins, outs, sems):
        _, _, landing, passed, _ = self._copies(ins, outs, sems)
        for landed, cp in zip(landing, passed):
            landed.wait_recv()
            cp.start()

    def finish(self, ins, outs, sems):
        local, first, _, passed, arriving = self._copies(ins, outs, sems)
        for cp in arriving:
            cp.wait_recv()
        for cp in first + passed:
            cp.wait_send()
        for cp in local:
            cp.wait()


class _ChipExchange:
    def __init__(self, sums):
        n = len(sums)
        self.n = n
        self.per = N_CHIPS - 1
        self.operands = list(sums)
        self.out_shape = [jax.ShapeDtypeStruct((self.per, *s.shape[1:]), s.dtype) for s in sums]
        self.scratch = [pltpu.SemaphoreType.DMA((self.per * n,)), pltpu.SemaphoreType.DMA((self.per * n,))]
        self.phases = [self.send, self.finish]

    def _copies(self, ins, outs, sems):
        send_sems, recv_sems = sems
        x, y, c = _mesh_pos()
        cps = []
        for w in range(self.n):
            for k, (chip, _) in enumerate(_other_chips(x, y)):
                s = self.per * w + k
                cps.append(_remote(ins[w].at[k + 1], outs[w].at[k], send_sems.at[s], recv_sems.at[s], (*chip, c)))
        return cps

    def send(self, ins, outs, sems):
        for cp in self._copies(ins, outs, sems):
            cp.start()

    def finish(self, ins, outs, sems):
        for cp in self._copies(ins, outs, sems):
            cp.wait()


def _run_exchange(ex, name):
    n_in, n_out = len(ex.operands), len(ex.out_shape)

    def body(*refs):
        ins, outs, sems = refs[:n_in], refs[n_in:n_in + n_out], refs[n_in + n_out:]
        for phase in ex.phases:
            phase(ins, outs, sems)

    return pl.pallas_call(body, name=name, in_specs=[_ANY] * n_in, out_specs=[_ANY] * n_out, out_shape=ex.out_shape,
                          scratch_shapes=ex.scratch)(*ex.operands)


def _call(body, args, *, name, grid, in_specs, out_specs, out_shape, scratch_shapes=(), exchange=None, steps=None):
    params = _params(len(grid))
    if exchange is None:
        out = pl.pallas_call(body, name=name, grid=grid, in_specs=in_specs, out_specs=out_specs, out_shape=out_shape,
                             scratch_shapes=list(scratch_shapes), compiler_params=params)(*args)
        return out, None
    n_in, n_out, n_scr = len(in_specs), len(out_specs), len(scratch_shapes)
    n_xin, n_xout = len(exchange.operands), len(exchange.out_shape)
    assert len(steps) == len(exchange.phases)

    def hosting(*refs):
        cuts = [n_in, n_xin, n_out, n_xout, n_scr]
        parts, at = [], 0
        for size in cuts:
            parts.append(refs[at:at + size])
            at += size
        ins, xins, outs, xouts, scr = parts
        sems = refs[at:]
        step = 0
        for axis, size in enumerate(grid):
            step = step * size + pl.program_id(axis)
        pl.when(step == steps[0])(lambda: exchange.phases[0](xins, xouts, sems))
        body(*ins, *outs, *scr)
        for at_step, phase in zip(steps[1:], exchange.phases[1:]):
            pl.when(step == at_step)(functools.partial(phase, xins, xouts, sems))

    out = pl.pallas_call(
        hosting, name=name, grid=grid,
        in_specs=list(in_specs) + [_ANY] * n_xin, out_specs=list(out_specs) + [_ANY] * n_xout,
        out_shape=list(out_shape) + list(exchange.out_shape),
        scratch_shapes=list(scratch_shapes) + list(exchange.scratch), compiler_params=params,
    )(*args, *exchange.operands)
    return out[:n_out], out[n_out:]


def _pair_exchange(grads, name):
    n = len(grads)

    def body(*refs):
        ins, outs = refs[:n], refs[n:2 * n]
        send_sems, recv_sems = refs[2 * n:]
        x, y, c = _mesh_pos()
        cps = []
        for w in range(n):
            hr = grads[w].shape[1] // 2
            cp = _remote(ins[w].at[:, pl.ds((1 - c) * hr, hr), :], outs[w], send_sems.at[w], recv_sems.at[w], (x, y, 1 - c))
            cp.start()
            cps.append(cp)
        for cp in cps:
            cp.wait()

    return pl.pallas_call(
        body, name=name,
        in_specs=[_ANY] * n, out_specs=[_ANY] * n,
        out_shape=[jax.ShapeDtypeStruct((g.shape[0], g.shape[1] // 2, g.shape[2]), g.dtype) for g in grads],
        scratch_shapes=[pltpu.SemaphoreType.DMA((n,)), pltpu.SemaphoreType.DMA((n,))],
    )(*grads)


def _pair_sum(g, a, pos, name):
    nb, r, c = g.shape
    hr = r // 2

    def body(pos_ref, g_ref, a_ref, o_ref):
        o_ref[...] = (g_ref[...] + a_ref[...].astype(F32)).astype(BF16)

    return pl.pallas_call(
        body, name=name,
        grid_spec=pltpu.PrefetchScalarGridSpec(
            num_scalar_prefetch=1, grid=(nb,),
            in_specs=[pl.BlockSpec((1, hr, c), lambda k, pos: (k ^ pos[0], pos[1], 0)),
                      pl.BlockSpec((1, hr, c), lambda k, pos: (k ^ pos[0], 0, 0))],
            out_specs=pl.BlockSpec((1, hr, c), lambda k, pos: (k, 0, 0))),
        out_shape=jax.ShapeDtypeStruct((nb, hr, c), BF16),
        compiler_params=_params(),
    )(pos, g, a)


def _chip_sum(s, b, pos, name):
    _, hr, c = s.shape

    def body(pos_ref, s_ref, b_ref, o_ref):
        o_ref[...] = (s_ref[0].astype(F32) + b_ref[0].astype(F32)) + (b_ref[1].astype(F32) + b_ref[2].astype(F32))

    return pl.pallas_call(
        body, name=name,
        grid_spec=pltpu.PrefetchScalarGridSpec(
            num_scalar_prefetch=1, grid=(1,),
            in_specs=[pl.BlockSpec((1, hr, c), lambda k, pos: (0, 0, 0)), pl.BlockSpec((N_CHIPS - 1, hr, c), lambda k, pos: (0, 0, 0))],
            out_specs=pl.BlockSpec((hr, c), lambda k, pos: (pos[1], 0))),
        out_shape=jax.ShapeDtypeStruct((2 * hr, c), F32),
        compiler_params=_params(),
    )(pos, s, b)


def _pair_share(grads, name):
    n = len(grads)

    def body(*refs):
        outs = refs[n:2 * n]
        send_sems, recv_sems = refs[2 * n:]
        x, y, c = _mesh_pos()
        cps = []
        for w in range(n):
            hr = grads[w].shape[0] // 2
            rows = outs[w].at[pl.ds(c * hr, hr), :]
            cp = _remote(rows, rows, send_sems.at[w], recv_sems.at[w], (x, y, 1 - c))
            cp.start()
            cps.append(cp)
        for w, cp in enumerate(cps):
            cp.wait_send()
            hr = grads[w].shape[0] // 2
            other = outs[w].at[pl.ds((1 - c) * hr, hr), :]
            _remote(other, other, send_sems.at[w], recv_sems.at[w], (x, y, 1 - c)).wait_recv()

    return pl.pallas_call(
        body, name=name,
        in_specs=[_ANY] * n, out_specs=[_ANY] * n,
        out_shape=[jax.ShapeDtypeStruct(g.shape, g.dtype) for g in grads],
        input_output_aliases={w: w for w in range(n)},
        scratch_shapes=[pltpu.SemaphoreType.DMA((n,)), pltpu.SemaphoreType.DMA((n,))],
    )(*grads)


def _all_reduce_small(rows, mats):
    n_dev = 8
    n_rows = -(-len(rows) // 8) * 8
    heights = [math.prod(a.shape[:-1]) for a in mats]
    n_tall = -(-sum(heights) // 8) * 8
    arrays = list(rows) + list(mats)

    def body(*refs):
        ins, outs = refs[:len(arrays)], refs[len(arrays):2 * len(arrays)]
        wide, tall, wide_slots, tall_slots, send_sems, recv_sems = refs[2 * len(arrays):]
        x, y, c = _mesh_pos()
        me = 4 * x + 2 * y + c
        wide[...] = jnp.zeros_like(wide)
        tall[...] = jnp.zeros_like(tall)
        for k, a in enumerate(rows):
            wide[k:k + 1, 0:a.shape[1]] = ins[k][...]
        at = 0
        for k, h in enumerate(heights):
            tall[at:at + h, :] = ins[len(rows) + k][...].reshape(h, LANES)
            at += h
        wide_slots[0] = wide[...]
        tall_slots[0] = tall[...]
        cps = []
        for q in range(1, n_dev):
            peer = (x ^ (q >> 2), y ^ ((q >> 1) & 1), c ^ (q & 1))
            for j, (buf, slots) in enumerate(((wide, wide_slots), (tall, tall_slots))):
                s = 2 * (q - 1) + j
                cp = _remote(buf, slots.at[q], send_sems.at[s], recv_sems.at[s], peer)
                cp.start()
                cps.append(cp)
        for cp in cps:
            cp.wait()
        wide_sum, tall_sum = wide_slots[me], tall_slots[me]
        for d in range(1, n_dev):
            wide_sum = wide_sum + wide_slots[d ^ me]
            tall_sum = tall_sum + tall_slots[d ^ me]
        for k, a in enumerate(rows):
            outs[k][...] = wide_sum[k:k + 1, 0:a.shape[1]]
        at = 0
        for k, h in enumerate(heights):
            outs[len(rows) + k][...] = tall_sum[at:at + h, :].reshape(mats[k].shape)
            at += h

    vmem = pl.BlockSpec(memory_space=pltpu.VMEM)
    return pl.pallas_call(
        body, name="all_reduce_small",
        in_specs=[vmem] * len(arrays), out_specs=[vmem] * len(arrays),
        out_shape=[jax.ShapeDtypeStruct(a.shape, F32) for a in arrays],
        scratch_shapes=[pltpu.VMEM((n_rows, D_MODEL), F32), pltpu.VMEM((n_tall, LANES), F32),
                        pltpu.VMEM((n_dev, n_rows, D_MODEL), F32), pltpu.VMEM((n_dev, n_tall, LANES), F32),
                        pltpu.SemaphoreType.DMA((2 * (n_dev - 1),)), pltpu.SemaphoreType.DMA((2 * (n_dev - 1),))],
    )(*arrays)


def _adamw_small(items):
    n = len(items)
    bias1 = 1.0 - ADAM_B1 ** ADAM_STEP
    bias2 = 1.0 - ADAM_B2 ** ADAM_STEP

    def body(*refs):
        ins, outs = refs[:4 * n], refs[4 * n:]
        for k in range(n):
            w_ref, g_ref, m_ref, v_ref = ins[4 * k:4 * k + 4]
            gg = g_ref[...]
            m2 = ADAM_B1 * m_ref[...] + (1.0 - ADAM_B1) * gg
            v2 = ADAM_B2 * v_ref[...] + (1.0 - ADAM_B2) * (gg * gg)
            outs[3 * k + 1][...] = m2
            outs[3 * k + 2][...] = v2
            outs[3 * k][...] = -ADAM_LR * ((m2 / bias1) / (jnp.sqrt(v2 / bias2) + ADAM_EPS) + ADAM_WD * w_ref[...])

    vmem = pl.BlockSpec(memory_space=pltpu.VMEM)
    out = pl.pallas_call(
        body, name="adamw_small", in_specs=[vmem] * (4 * n), out_specs=[vmem] * (3 * n),
        out_shape=[jax.ShapeDtypeStruct(w.shape, F32) for w, _, _, _ in items for _ in range(3)],
    )(*[a for item in items for a in item])
    return [tuple(out[3 * k:3 * k + 3]) for k in range(n)]


ADAMW_STEPS = 8


def _adamw(items, name, exchange=None):
    n = len(items)
    bias1 = 1.0 - ADAM_B1 ** ADAM_STEP
    bias2 = 1.0 - ADAM_B2 ** ADAM_STEP

    def body(*refs):
        ins, outs = refs[:4 * n], refs[4 * n:]
        for k in range(n):
            w_ref, g_ref, m_ref, v_ref = ins[4 * k:4 * k + 4]
            d_ref, mo_ref, vo_ref = outs[3 * k:3 * k + 3]
            gg = g_ref[...]
            m2 = ADAM_B1 * m_ref[...] + (1.0 - ADAM_B1) * gg
            v2 = ADAM_B2 * v_ref[...] + (1.0 - ADAM_B2) * (gg * gg)
            mo_ref[...] = m2
            vo_ref[...] = v2
            d_ref[...] = -ADAM_LR * ((m2 / bias1) / (jnp.sqrt(v2 / bias2) + ADAM_EPS) + ADAM_WD * w_ref[...])

    in_specs, out_specs, out_shape, args = [], [], [], []
    for w, g, m, v in items:
        r, c = w.shape
        steps = ADAMW_STEPS if r % (8 * ADAMW_STEPS) == 0 else 1
        assert steps == ADAMW_STEPS or n == 1
        spec = pl.BlockSpec((r // steps, c), lambda i: (i, 0))
        in_specs += [spec] * 4
        out_specs += [spec] * 3
        out_shape += [jax.ShapeDtypeStruct((r, c), F32)] * 3
        args += [w, g, m, v]
    out, got = _call(body, args, name=name, grid=(steps,), in_specs=in_specs, out_specs=out_specs, out_shape=out_shape,
                     exchange=exchange, steps=(0, steps - 1))
    return [tuple(out[3 * k:3 * k + 3]) for k in range(n)], got


def kernel(x, p, ffn1_norm, ffn1_w_in, ffn1_w_out, mix_norm, w_mix_in, gmlp_v_norm, gmlp_w_s, gmlp_b, w_mix_out, ffn2_norm, ffn2_w_in, ffn2_w_out, ple_norm, ple_w_gate, ple_w_proj, final_norm, loss_target, m_ffn1_norm, m_ffn1_w_in, m_ffn1_w_out, m_mix_norm, m_w_mix_in, m_gmlp_v_norm, m_gmlp_w_s, m_gmlp_b, m_w_mix_out, m_ffn2_norm, m_ffn2_w_in, m_ffn2_w_out, m_ple_norm, m_ple_w_gate, m_ple_w_proj, m_final_norm, v_ffn1_norm, v_ffn1_w_in, v_ffn1_w_out, v_mix_norm, v_w_mix_in, v_gmlp_v_norm, v_gmlp_w_s, v_gmlp_b, v_w_mix_out, v_ffn2_norm, v_ffn2_w_in, v_ffn2_w_out, v_ple_norm, v_ple_w_gate, v_ple_w_proj, v_final_norm):
    args = dict(locals())
    w = {n: args[n] for n in _ALL}
    m = {n: args["m_" + n] for n in _ALL}
    v = {n: args["v_" + n] for n in _ALL}
    xi, yi, ci = _mesh_pos()
    pos = jnp.stack([2 * xi + yi, ci]).astype(jnp.int32)
    shard = {n: w[n][0] for n in _BIG}
    cast = {n: shard[n].astype(BF16) for n in _BIG}
    small = {n: (w[n][0] if w[n].ndim > 2 else w[n].reshape(1, -1)) for n in _SMALL}
    bt = small["gmlp_b"].T
    g_small, pair, from_chips = {}, {}, {}

    def pair_reduce(partials, tag):
        names = list(partials)
        blocks = lambda a, n: a.reshape(N_CHIPS, *shard[n].shape)
        got = _pair_exchange([blocks(partials[n][1], n) for n in names], "grad_pair_exchange_" + tag)
        for n, a in zip(names, got):
            pair[n] = _pair_sum(blocks(partials[n][0], n), a, pos, "pair_sum_" + n)
        return names

    w1in, w1out = _run_exchange(_WeightGather([cast["ffn1_w_in"], cast["ffn1_w_out"]]), "gather_ffn1")
    w1out = w1out.reshape(D_FF, D_MODEL)
    (h1, gu1), (wmix, wmo) = _ffn_fwd(x[0], small["ffn1_norm"], w1in, w1out, "ffn1_fwd",
                                      _WeightGather([cast["w_mix_in"], cast["w_mix_out"]]))
    wmo = wmo.reshape(D_MODEL, D_MODEL)
    zg, qkv = _mix_in_fwd(h1, small["mix_norm"], wmix)
    gm = _gmlp_fwd(zg, small["gmlp_v_norm"], small["gmlp_w_s"], bt)
    (att, carries), (w2in, w2out, wg, wproj) = _attn_fwd(
        qkv, _WeightGather([cast["ffn2_w_in"], cast["ffn2_w_out"], cast["ple_w_gate"], cast["ple_w_proj"]]))
    w2out = w2out.reshape(D_FF, D_MODEL)
    wg = wg.reshape(D_MODEL, D_MODEL)
    mixed = jnp.concatenate([gm, att], axis=1)
    h2 = _matmul_residual(h1, mixed, wmo, "mix_out_fwd")
    (h3, gu2), _ = _ffn_fwd(h2, small["ffn2_norm"], w2in, w2out, "ffn2_fwd")
    loss_part, g_small["final_norm"], g_small["ple_norm"], dh3, dgp, dpp, n4, pb = _head(
        h3, p[0, 0], loss_target[0], small["ple_norm"], small["final_norm"], wg, wproj)

    part = {"ple_w_gate": _wgrad_rows(n4, dgp, N_CHIPS, "wgrad_ple_gate"),
            "ple_w_proj": _wgrad_cols(pb, dpp, N_CHIPS, "wgrad_ple_proj")}
    (dh2, dgu2, n3, act2, dhh3, g_small["ffn2_norm"]), _ = _ffn_bwd(dh3, h2, small["ffn2_norm"], gu2, w2in, w2out,
                                                                   "ffn2_bwd")
    part["ffn2_w_in"] = _wgrad_cols(n3, dgu2, N_CHIPS, "wgrad_ffn2_in")
    part["ffn2_w_out"] = _wgrad_rows(act2, dhh3, 2, "wgrad_ffn2_out")
    dmixed, dh2b = _matmul_nt_cast(dh2, wmo, "mix_out_bwd")
    part["w_mix_out"] = _wgrad_rows(mixed, dh2b, 2, "wgrad_mix_out")
    group = pair_reduce(part, "late")
    dzg, g_small["gmlp_w_s"], dbt, g_small["gmlp_v_norm"] = _gmlp_bwd(zg, dmixed, small["gmlp_v_norm"],
                                                                      small["gmlp_w_s"], bt)
    g_small["gmlp_b"] = dbt.T
    (dq, dk, dv), got = _attn_bwd(qkv, dmixed, carries, _ChipExchange([pair[n] for n in group]))
    from_chips.update(zip(group, got))

    dzmix = jnp.concatenate([dzg, dq, dk, dv], axis=1)
    dh1, n2, g_small["mix_norm"] = _norm_input_bwd(dh2, dzmix, wmix, h1, small["mix_norm"], "mix_in_bwd")
    group = pair_reduce({"w_mix_in": _wgrad_cols(n2, dzmix, N_CHIPS, "wgrad_mix_in")}, "mix")
    (dx, dgu1, n1, act1, dhh1, g_small["ffn1_norm"]), _ = _ffn_bwd(dh1, x[0], small["ffn1_norm"], gu1, w1in, w1out,
                                                                   "ffn1_bwd")

    g_out, got = _wgrad_rows(act1, dhh1, 2, "wgrad_ffn1_out", _ChipExchange([pair[n] for n in group]))
    from_chips.update(zip(group, got))
    pair_reduce({"ffn1_w_out": g_out}, "out")
    g_in, got = _wgrad_cols(n1, dgu1, N_CHIPS, "wgrad_ffn1_in", _ChipExchange([pair["ffn1_w_out"]]))
    from_chips["ffn1_w_out"] = got[0]
    pair_reduce({"ffn1_w_in": g_in}, "in")

    def finish(names, tag, exchange=None):
        halves = [_chip_sum(pair[n], from_chips[n], pos, "chip_sum_" + n) for n in names]
        full = _pair_share(halves, "grad_pair_share_" + tag)
        out, got = _adamw([(shard[n], g, m[n][0], v[n][0]) for n, g in zip(names, full)], "adamw_" + tag, exchange)
        for n, g, (d2, m2, v2) in zip(names, full, out):
            grads[n], delta[n], new_m[n], new_v[n] = g[None], d2[None], m2[None], v2[None]
        return got

    grads, delta, new_m, new_v = {}, {}, {}, {}
    got = finish([n for n in _BIG if n != "ffn1_w_in"], "most", _ChipExchange([pair["ffn1_w_in"]]))
    from_chips["ffn1_w_in"] = got[0]
    finish(["ffn1_w_in"], "last")

    rows = [n for n in _SMALL if g_small[n].shape[0] == 1]
    mats = [n for n in _SMALL if n not in rows]
    summed = _all_reduce_small([g_small[n] for n in rows] + [loss_part], [g_small[n] for n in mats])
    loss = summed[len(rows)][0, 0]
    g_sum = dict(zip(rows + mats, summed[:len(rows)] + summed[len(rows) + 1:]))
    like = lambda a, n: a[n].reshape(small[n].shape)
    out = _adamw_small([(small[n], g_sum[n], like(m, n), like(v, n)) for n in _SMALL])
    for n, (d2, m2, v2) in zip(_SMALL, out):
        grads[n], delta[n], new_m[n], new_v[n] = (a.reshape(w[n].shape) for a in (g_sum[n], d2, m2, v2))

    return (loss, dx[None], *[grads[n] for n in _ALL], *[delta[n] for n in _ALL], *[new_m[n] for n in _ALL],
            *[new_v[n] for n in _ALL])
```

```python
import functools
import math

import jax
import jax.numpy as jnp
from jax import lax
from jax.experimental import pallas as pl
from jax.experimental.pallas import tpu as pltpu

F32, BF16 = jnp.float32, jnp.bfloat16

D_MODEL = 1024
D_FF = 2816
FF_BLOCK = 2 * D_FF // 4
PLE_DIM = 256
CHUNK = 128
GM_HEADS = 4
GM_WIDTH = 512
SB_HEAD_DIM = 64
SB_WIDTH = 512
MIX_IN_WIDTH = 2 * GM_WIDTH + 3 * SB_WIDTH
MIX_BLOCK = MIX_IN_WIDTH // 4
EPS = 1e-6
N_CHIPS = 4
LANES = 128
ATT_BLOCK = 128
ATT_Q = 512
VMEM_LIMIT = 56 * 1024 * 1024

ADAM_LR, ADAM_B1, ADAM_B2, ADAM_EPS, ADAM_WD, ADAM_STEP = 0.001, 0.9, 0.999, 1e-08, 0.01, 10


def _dot(a, b):
    return jnp.dot(a, b, preferred_element_type=F32)


def _dot_nt(a, b):
    return lax.dot_general(a, b, (((1,), (1,)), ((), ())), preferred_element_type=F32)


def _dot_tn(a, b):
    return lax.dot_general(a, b, (((0,), (0,)), ((), ())), preferred_element_type=F32)


def _resident(shape):
    nd = len(shape)
    return pl.BlockSpec(shape, lambda *_: (0,) * nd, pipeline_mode=pl.Buffered(1))


def _rows(tm, width):
    return pl.BlockSpec((tm, width), lambda i: (i, 0))


def _params(n_axes=1):
    return pltpu.CompilerParams(dimension_semantics=("arbitrary",) * n_axes, vmem_limit_bytes=VMEM_LIMIT)


def _rstd(h):
    return lax.rsqrt(jnp.mean(h * h, axis=-1, keepdims=True) + EPS)


def _rms_bwd(dy, h, r, g):
    dyg = dy * g
    dh = r * dyg - h * (r * r * r) * jnp.mean(dyg * h, axis=-1, keepdims=True)
    return dh, dy * h * r


def _gelu(x):
    return 0.5 * x * (1.0 + lax.erf(x * (2.0 ** -0.5)))


def _gelu_grad(x):
    return 0.5 * (1.0 + lax.erf(x * (2.0 ** -0.5))) + x * jnp.exp(-0.5 * x * x) * ((2.0 * jnp.pi) ** -0.5)


def _token_tile(t):
    return min(256, t)


def _ffn_fwd(h, g, win, wout, name, exchange=None):
    t = h.shape[0]
    tm = _token_tile(t)

    def body(h_ref, g_ref, win_ref, wout_ref, ho_ref, gu_ref):
        hh = h_ref[...]
        n = (hh * _rstd(hh) * g_ref[...]).astype(BF16)
        acc = jnp.zeros((tm, D_MODEL), F32)
        for jb in range(2):
            gate = _dot(n, win_ref[jb])
            up = _dot(n, win_ref[2 + jb])
            gu_ref[:, jb * FF_BLOCK:(jb + 1) * FF_BLOCK] = gate.astype(BF16)
            gu_ref[:, D_FF + jb * FF_BLOCK:D_FF + (jb + 1) * FF_BLOCK] = up.astype(BF16)
            act = (gate * jax.nn.sigmoid(gate) * up).astype(BF16)
            acc = acc + _dot(act, wout_ref[jb * FF_BLOCK:(jb + 1) * FF_BLOCK, :])
        ho_ref[...] = hh + 0.5 * acc

    n = t // tm
    return _call(
        body, (h, g, win, wout), name=name, grid=(n,),
        in_specs=[_rows(tm, D_MODEL), _resident((1, D_MODEL)), _resident(win.shape), _resident(wout.shape)],
        out_specs=[_rows(tm, D_MODEL), _rows(tm, 2 * D_FF)],
        out_shape=[jax.ShapeDtypeStruct((t, D_MODEL), F32), jax.ShapeDtypeStruct((t, 2 * D_FF), BF16)],
        exchange=exchange, steps=(0, (2 * n) // 3, n - 1))


def _ffn_bwd(dho, h, g, gu, win, wout, name, exchange=None):
    t = h.shape[0]
    tm = _token_tile(t)

    def body(dho_ref, h_ref, g_ref, gu_ref, win_ref, wout_ref, dh_ref, dgu_ref, n_ref, act_ref, dhh_ref, dg_ref):
        i = pl.program_id(0)
        hh = h_ref[...]
        gg = g_ref[...]
        r = _rstd(hh)
        n_ref[...] = (hh * r * gg).astype(BF16)
        dho = dho_ref[...]
        dhh = (0.5 * dho).astype(BF16)
        dhh_ref[...] = dhh
        dn = jnp.zeros((tm, D_MODEL), F32)
        for jb in range(2):
            cg = slice(jb * FF_BLOCK, (jb + 1) * FF_BLOCK)
            cu = slice(D_FF + jb * FF_BLOCK, D_FF + (jb + 1) * FF_BLOCK)
            dact = _dot_nt(dhh, wout_ref[cg, :])
            gate = gu_ref[:, cg].astype(F32)
            up = gu_ref[:, cu].astype(F32)
            sg = jax.nn.sigmoid(gate)
            silu = gate * sg
            act_ref[:, cg] = (silu * up).astype(BF16)
            dgate = (dact * up * (sg * (1.0 + gate * (1.0 - sg)))).astype(BF16)
            dup = (dact * silu).astype(BF16)
            dgu_ref[:, cg] = dgate
            dgu_ref[:, cu] = dup
            dn = dn + _dot_nt(dgate, win_ref[jb]) + _dot_nt(dup, win_ref[2 + jb])
        dh, dg_rows = _rms_bwd(dn, hh, r, gg)
        dh_ref[...] = dho + dh

        @pl.when(i == 0)
        def _():
            dg_ref[...] = jnp.zeros_like(dg_ref)

        dg_ref[...] += jnp.sum(dg_rows, axis=0, keepdims=True)

    n = t // tm
    return _call(
        body, (dho, h, g, gu, win, wout), name=name, grid=(n,),
        in_specs=[_rows(tm, D_MODEL), _rows(tm, D_MODEL), _resident((1, D_MODEL)), _rows(tm, 2 * D_FF),
                  _resident(win.shape), _resident(wout.shape)],
        out_specs=[_rows(tm, D_MODEL), _rows(tm, 2 * D_FF), _rows(tm, D_MODEL), _rows(tm, D_FF), _rows(tm, D_MODEL),
                   pl.BlockSpec((1, D_MODEL), lambda i: (0, 0))],
        out_shape=[jax.ShapeDtypeStruct((t, D_MODEL), F32), jax.ShapeDtypeStruct((t, 2 * D_FF), BF16),
                   jax.ShapeDtypeStruct((t, D_MODEL), BF16), jax.ShapeDtypeStruct((t, D_FF), BF16),
                   jax.ShapeDtypeStruct((t, D_MODEL), BF16), jax.ShapeDtypeStruct((1, D_MODEL), F32)],
        exchange=exchange, steps=(0, n - 1))


def _wgrad(a, b, out_shape, out_block, out_index, a_width, b_width, grid_ij, name, exchange=None):
    t = a.shape[0]
    tk = min(2048, t)
    nk = t // tk

    def body(a_ref, b_ref, o_ref, ob_ref):
        k = pl.program_id(2)
        prod = _dot_tn(a_ref[...], b_ref[...]).reshape(o_ref.shape)

        @pl.when(k == 0)
        def _():
            o_ref[...] = prod

        @pl.when(k > 0)
        def _():
            o_ref[...] += prod

        @pl.when(k == nk - 1)
        def _():
            ob_ref[...] = o_ref[...].astype(BF16)

    grid = (*grid_ij, nk)
    out_spec = pl.BlockSpec(out_block, lambda i, j, k: out_index(i, j))
    out, got = _call(
        body, (a, b), name=name, grid=grid,
        in_specs=[pl.BlockSpec((tk, a_width), lambda i, j, k: (k, i)), pl.BlockSpec((tk, b_width), lambda i, j, k: (k, j))],
        out_specs=[out_spec, out_spec],
        out_shape=[jax.ShapeDtypeStruct(out_shape, F32), jax.ShapeDtypeStruct(out_shape, BF16)],
        exchange=exchange, steps=(0, grid[0] * grid[1] * grid[2] - 1))
    return tuple(out) if exchange is None else (tuple(out), got)


def _wgrad_cols(a, b, n_blocks, name, exchange=None):
    ka, nb = a.shape[1], b.shape[1] // n_blocks
    return _wgrad(a, b, (n_blocks, ka, nb), (1, ka, nb), lambda i, j: (j, 0, 0), ka, nb, (1, n_blocks), name, exchange)


def _wgrad_rows(a, b, n_blocks, name, exchange=None):
    ka, nb = a.shape[1] // n_blocks, b.shape[1]
    return _wgrad(a, b, (a.shape[1], nb), (ka, nb), lambda i, j: (i, 0), ka, nb, (n_blocks, 1), name, exchange)


def _mix_in_fwd(h, g, wmix):
    t = h.shape[0]
    tm = _token_tile(t)
    gw2 = 2 * GM_WIDTH

    def body(h_ref, g_ref, w_ref, zg_ref, qkv_ref):
        hh = h_ref[...]
        n = (hh * _rstd(hh) * g_ref[...]).astype(BF16)
        for b in range(N_CHIPS):
            z = _dot(n, w_ref[b])
            lo, hi = b * MIX_BLOCK, (b + 1) * MIX_BLOCK
            if hi <= gw2:
                zg_ref[:, lo:hi] = z
            elif lo >= gw2:
                qkv_ref[:, lo - gw2:hi - gw2] = z.astype(BF16)
            else:
                zg_ref[:, lo:gw2] = z[:, :gw2 - lo]
                qkv_ref[:, 0:hi - gw2] = z[:, gw2 - lo:].astype(BF16)

    return pl.pallas_call(
        body, name="mix_in_fwd", grid=(t // tm,),
        in_specs=[_rows(tm, D_MODEL), _resident((1, D_MODEL)), _resident(wmix.shape)],
        out_specs=[_rows(tm, gw2), _rows(tm, 3 * SB_WIDTH)],
        out_shape=[jax.ShapeDtypeStruct((t, gw2), F32), jax.ShapeDtypeStruct((t, 3 * SB_WIDTH), BF16)],
        compiler_params=_params(),
    )(h, g, wmix)


def _causal_chunk_mask():
    row = lax.broadcasted_iota(jnp.int32, (CHUNK, CHUNK), 0)
    col = lax.broadcasted_iota(jnp.int32, (CHUNK, CHUNK), 1)
    return row >= col


def _gmlp_tile(t):
    return min(512, t)


def _gmlp_fwd(zg, gv, ws, bt):
    t = zg.shape[0]
    tm = _gmlp_tile(t)

    def body(zg_ref, gv_ref, ws_ref, bt_ref, o_ref):
        u = _gelu(zg_ref[:, :GM_WIDTH])
        v = _gelu(zg_ref[:, GM_WIDTH:])
        vn = (v * _rstd(v) * gv_ref[...]).astype(BF16)
        mask = _causal_chunk_mask()
        for hd in range(GM_HEADS):
            wm = jnp.where(mask, ws_ref[hd], 0.0).astype(BF16)
            cols = slice(hd * CHUNK, (hd + 1) * CHUNK)
            for c in range(tm // CHUNK):
                rows = slice(c * CHUNK, (c + 1) * CHUNK)
                sv = _dot(wm, vn[rows, cols]) + bt_ref[:, hd:hd + 1]
                o_ref[rows, cols] = (u[rows, cols] * sv).astype(BF16)

    return pl.pallas_call(
        body, name="gmlp_fwd", grid=(t // tm,),
        in_specs=[_rows(tm, 2 * GM_WIDTH), _resident((1, GM_WIDTH)), _resident(ws.shape), _resident(bt.shape)],
        out_specs=_rows(tm, GM_WIDTH),
        out_shape=jax.ShapeDtypeStruct((t, GM_WIDTH), BF16),
        compiler_params=_params(),
    )(zg, gv, ws, bt)


def _gmlp_bwd(zg, dmixed, gv, ws, bt):
    t = zg.shape[0]
    tm = _gmlp_tile(t)

    def body(zg_ref, dgm_ref, gv_ref, ws_ref, bt_ref, dzg_ref, dws_ref, dbt_ref, dgv_ref):
        i = pl.program_id(0)

        @pl.when(i == 0)
        def _():
            dws_ref[...] = jnp.zeros_like(dws_ref)
            dbt_ref[...] = jnp.zeros_like(dbt_ref)
            dgv_ref[...] = jnp.zeros_like(dgv_ref)

        zu = zg_ref[:, :GM_WIDTH]
        zv = zg_ref[:, GM_WIDTH:]
        u = _gelu(zu)
        v = _gelu(zv)
        r = _rstd(v)
        gvv = gv_ref[...]
        vn = (v * r * gvv).astype(BF16)
        dgm = dgm_ref[...].astype(F32)
        dsv = (dgm * u).astype(BF16)
        mask = _causal_chunk_mask()
        du_cols, dvn_cols = [], []
        for hd in range(GM_HEADS):
            wm = jnp.where(mask, ws_ref[hd], 0.0).astype(BF16)
            cols = slice(hd * CHUNK, (hd + 1) * CHUNK)
            dw = jnp.zeros((CHUNK, CHUNK), F32)
            db = jnp.zeros((CHUNK, 1), F32)
            du_rows, dvn_rows = [], []
            for c in range(tm // CHUNK):
                rows = slice(c * CHUNK, (c + 1) * CHUNK)
                sv = _dot(wm, vn[rows, cols]) + bt_ref[:, hd:hd + 1]
                du_rows.append(dgm[rows, cols] * sv)
                dvn_rows.append(_dot_tn(wm, dsv[rows, cols]))
                dw = dw + _dot_nt(dsv[rows, cols], vn[rows, cols])
                db = db + jnp.sum(dsv[rows, cols].astype(F32), axis=1, keepdims=True)
            dws_ref[hd] += jnp.where(mask, dw, 0.0)
            dbt_ref[:, hd:hd + 1] += db
            du_cols.append(jnp.concatenate(du_rows, axis=0))
            dvn_cols.append(jnp.concatenate(dvn_rows, axis=0))
        du = jnp.concatenate(du_cols, axis=1)
        dvn = jnp.concatenate(dvn_cols, axis=1)
        dv, dgv_rows = _rms_bwd(dvn, v, r, gvv)
        dgv_ref[...] += jnp.sum(dgv_rows, axis=0, keepdims=True)
        dzg_ref[:, :GM_WIDTH] = (du * _gelu_grad(zu)).astype(BF16)
        dzg_ref[:, GM_WIDTH:] = (dv * _gelu_grad(zv)).astype(BF16)

    const = lambda nd: (lambda i: (0,) * nd)
    return pl.pallas_call(
        body, name="gmlp_bwd", grid=(t // tm,),
        in_specs=[_rows(tm, 2 * GM_WIDTH), _rows(tm, GM_WIDTH), _resident((1, GM_WIDTH)), _resident(ws.shape),
                  _resident(bt.shape)],
        out_specs=[_rows(tm, 2 * GM_WIDTH), pl.BlockSpec(ws.shape, const(3)), pl.BlockSpec(bt.shape, const(2)),
                   pl.BlockSpec((1, GM_WIDTH), const(2))],
        out_shape=[jax.ShapeDtypeStruct((t, 2 * GM_WIDTH), BF16), jax.ShapeDtypeStruct(ws.shape, F32),
                   jax.ShapeDtypeStruct(bt.shape, F32), jax.ShapeDtypeStruct((1, GM_WIDTH), F32)],
        compiler_params=_params(),
    )(zg, dmixed, gv, ws, bt)


def _att_masks():
    tb = ATT_BLOCK
    lane = lax.broadcasted_iota(jnp.int32, (1, LANES), 1)
    rj = lax.broadcasted_iota(jnp.int32, (2 * tb, 2 * tb), 0)
    cs = lax.broadcasted_iota(jnp.int32, (2 * tb, 2 * tb), 1)
    same_head = ((rj < tb) & (cs < tb)) | ((rj >= tb) & (cs >= tb))
    suffix = jnp.where(same_head & (rj >= cs), 1.0, 0.0).astype(BF16)
    prefix = jnp.where(same_head & (rj <= cs), 1.0, 0.0).astype(BF16)
    left = lax.broadcasted_iota(jnp.int32, (1, 2 * tb), 1) < tb
    tq = lax.broadcasted_iota(jnp.int32, (ATT_Q, 4 * tb), 0)
    ts = lax.broadcasted_iota(jnp.int32, (ATT_Q, 4 * tb), 1)
    key = jnp.where(ts < 2 * tb, ts & (tb - 1), (ts & (tb - 1)) + tb)
    return lane, suffix, prefix, left, key, tq


def _att_fill(k_ref, v_ref, kcat, vcat, n_blocks, lane):
    tb = ATT_BLOCK
    first = lane < SB_HEAD_DIM

    def fill(jb, carry):
        rows = pl.ds(pl.multiple_of(jb * tb, tb), tb)
        top = pl.ds(pl.multiple_of(jb * 2 * tb, tb), tb)
        bot = pl.ds(pl.multiple_of(jb * 2 * tb + tb, tb), tb)
        kb = k_ref[rows, :]
        vb = v_ref[rows, :]
        zero = jnp.zeros_like(kb)
        kcat[top, :] = jnp.where(first, kb, zero)
        kcat[bot, :] = jnp.where(first, zero, kb)
        vcat[top, :] = jnp.where(first, vb, zero)
        vcat[bot, :] = jnp.where(first, zero, vb)
        return carry

    lax.fori_loop(0, n_blocks, fill, 0)


def _block_sums(x, m):
    return _dot(x.astype(BF16), m)


def _softplus2(z2):
    return jnp.maximum(z2, 0.0) + jnp.log2(1.0 + jnp.exp2(-jnp.abs(z2)))


def _scaled_queries(q_ref, base2):
    scale = SB_HEAD_DIM ** -0.5
    return (q_ref[...].astype(F32) * (scale * math.log2(math.e) if base2 else scale)).astype(BF16)


def _att_specs(t):
    n_pairs = SB_WIDTH // LANES
    q_spec = pl.BlockSpec((ATT_Q, LANES), lambda p, i: (i, p))
    k_spec = pl.BlockSpec((t, LANES), lambda p, i: (0, n_pairs + p))
    v_spec = pl.BlockSpec((t, LANES), lambda p, i: (0, 2 * n_pairs + p))
    return n_pairs, q_spec, k_spec, v_spec


def _attn_fwd(qkv, exchange=None):
    t = qkv.shape[0]
    tb = ATT_BLOCK
    nkb = t // tb
    assert 2 * nkb <= LANES and t % ATT_Q == 0 and ATT_Q == 4 * tb
    n_pairs, q_spec, k_spec, v_spec = _att_specs(t)

    def body(q_ref, k_ref, v_ref, o_ref, ct_ref, kcat, vcat, acc, carry, z0, r0, z1, r1):
        i = pl.program_id(1)
        lane, suffix, _, left, key, tq = _att_masks()

        @pl.when(i == 0)
        def _():
            _att_fill(k_ref, v_ref, kcat, vcat, nkb, lane)

        q = _scaled_queries(q_ref, True)
        acc[...] = jnp.zeros_like(acc)
        carry[...] = jnp.zeros_like(carry)
        ct_ref[0] = jnp.zeros((ATT_Q, LANES), F32)

        def key_rows(m):
            return pl.ds(pl.multiple_of(m * 4 * tb, 4 * tb), 4 * tb)

        def scores(m, zb, rb, causal=None, rs=slice(None)):
            z = _dot_nt(q[rs], kcat[key_rows(m), :])
            zb[rs, :] = z
            sp = _softplus2(z)
            if causal is not None:
                sp = jnp.where(causal[rs], sp, 0.0)
            for g in (1, 0):
                cols = slice(g * 2 * tb, (g + 1) * 2 * tb)
                rb[rs, cols] = _block_sums(sp[:, cols], suffix)

        def weigh(m, zb, rb, causal=None, rs=slice(None)):
            probs = [None, None]
            for g in (1, 0):
                cols = slice(g * 2 * tb, (g + 1) * 2 * tb)
                j = 2 * m + g
                r = rb[rs, cols]
                c = carry[rs, :]
                ct_ref[0, rs, :] = jnp.where(lane == j, c[:, :tb], jnp.where(lane == nkb + j, c[:, tb:], ct_ref[0, rs, :]))
                a = jnp.exp2(zb[rs, cols] - (r + c))
                if causal is not None:
                    a = jnp.where(causal[rs, cols], a, 0.0)
                probs[g] = a.astype(BF16)
                carry[rs, :] = c + jnp.where(left, r[:, 0:1], r[:, tb:tb + 1])
            acc[rs, :] += _dot(jnp.concatenate(probs, axis=1), vcat[key_rows(m), :])

        sooner, later, late_rows = key < tq, key + 2 * tb < tq, slice(2 * tb, 4 * tb)
        scores(2 * i + 1, z1, r1, later, late_rows)
        scores(2 * i, z0, r0, sooner)
        weigh(2 * i + 1, z1, r1, later, late_rows)
        weigh(2 * i, z0, r0, sooner)

        @pl.when(i > 0)
        def _():
            scores(2 * i - 1, z1, r1)

            def loop(k, c):
                u = i - 1 - k
                scores(2 * u, z0, r0)
                weigh(2 * u + 1, z1, r1)
                scores(2 * u - 1, z1, r1)
                weigh(2 * u, z0, r0)
                return c

            lax.fori_loop(0, i - 1, loop, 0)
            scores(0, z0, r0)
            weigh(1, z1, r1)
            weigh(0, z0, r0)

        o_ref[...] = acc[...].astype(BF16)

    tile = pltpu.VMEM((ATT_Q, 4 * tb), F32)

    nq = t // ATT_Q
    return _call(
        body, (qkv, qkv, qkv), name="attn_fwd", grid=(n_pairs, nq),
        in_specs=[q_spec, k_spec, v_spec],
        out_specs=[pl.BlockSpec((ATT_Q, LANES), lambda p, i: (i, p)), pl.BlockSpec((1, ATT_Q, LANES), lambda p, i: (p, i, 0))],
        out_shape=[jax.ShapeDtypeStruct((t, SB_WIDTH), BF16), jax.ShapeDtypeStruct((n_pairs, t, LANES), F32)],
        scratch_shapes=[pltpu.VMEM((2 * t, LANES), BF16), pltpu.VMEM((2 * t, LANES), BF16),
                        pltpu.VMEM((ATT_Q, LANES), F32), pltpu.VMEM((ATT_Q, 2 * tb), F32), tile, tile, tile, tile],
        exchange=exchange, steps=(0, (n_pairs - 1) * nq - 1, n_pairs * nq - 1))


def _attn_bwd(qkv, dmixed, carries, exchange=None):
    t = qkv.shape[0]
    tb = ATT_BLOCK
    nkb = t // tb
    nq = t // ATT_Q
    scale = SB_HEAD_DIM ** -0.5
    n_pairs, q_spec, k_spec, v_spec = _att_specs(t)
    gm_blocks = GM_WIDTH // LANES

    def body(q_ref, k_ref, v_ref, do_ref, ct_ref, dq_ref, dk_ref, dv_ref, kcat, vcat, dkacc, dvacc, dqacc, carry,
             z0, r0, s0, a0, z1, r1, s1, a1):
        i = pl.program_id(1)
        lane, suffix, prefix, left, key, tq = _att_masks()
        first = lane < SB_HEAD_DIM

        @pl.when(i == 0)
        def _():
            _att_fill(k_ref, v_ref, kcat, vcat, nkb, lane)
            dkacc[...] = jnp.zeros_like(dkacc)
            dvacc[...] = jnp.zeros_like(dvacc)

        q2 = _scaled_queries(q_ref, True)
        q = _scaled_queries(q_ref, False)
        do = do_ref[...]
        dqacc[...] = jnp.zeros_like(dqacc)
        carry[...] = jnp.zeros_like(carry)

        def key_rows(m):
            return pl.ds(pl.multiple_of(m * 4 * tb, 4 * tb), 4 * tb)

        def front(m, bufs, causal=None, rs=slice(None)):
            zb, rb, sb, ab = bufs
            z = _dot_nt(q2[rs], kcat[key_rows(m), :])
            zb[rs, :] = z
            sp = _softplus2(z)
            sb[rs, :] = jnp.exp2(z - sp)
            if causal is not None:
                sp = jnp.where(causal[rs], sp, 0.0)
            for g in (0, 1):
                cols = slice(g * 2 * tb, (g + 1) * 2 * tb)
                rb[rs, cols] = _block_sums(sp[:, cols], suffix)
            ab[rs, :] = _dot_nt(do[rs], vcat[key_rows(m), :])

        def back(m, bufs, causal=None, rs=slice(None)):
            zb, rb, sb, ab = bufs
            dzs, probs = [None, None], [None, None]
            for g in (0, 1):
                cols = slice(g * 2 * tb, (g + 1) * 2 * tb)
                j = 2 * m + g
                ct = ct_ref[0, rs, :]
                ca = jnp.sum(jnp.where(lane == j, ct, 0.0), axis=1, keepdims=True)
                cb = jnp.sum(jnp.where(lane == nkb + j, ct, 0.0), axis=1, keepdims=True)
                a = jnp.exp2(zb[rs, cols] - (rb[rs, cols] + jnp.where(left, ca, cb)))
                if causal is not None:
                    a = jnp.where(causal[rs, cols], a, 0.0)
                de = ab[rs, cols] * a
                cl = _block_sums(de, prefix)
                pre = carry[rs, :]
                dz = de - sb[rs, cols] * (cl + pre)
                if causal is not None:
                    dz = jnp.where(causal[rs, cols], dz, 0.0)
                carry[rs, :] = pre + jnp.where(left, cl[:, tb - 1:tb], cl[:, 2 * tb - 1:2 * tb])
                dzs[g] = dz.astype(BF16)
                probs[g] = a.astype(BF16)
            dzb = jnp.concatenate(dzs, axis=1)
            dqacc[rs, :] += _dot(dzb, kcat[key_rows(m), :])
            dkc = _dot_tn(dzb, q[rs])
            dvc = _dot_tn(jnp.concatenate(probs, axis=1), do[rs])
            out_rows = pl.ds(pl.multiple_of(m * 2 * tb, 2 * tb), 2 * tb)
            pick = lambda x: jnp.concatenate([jnp.where(first, x[0:tb], x[tb:2 * tb]),
                                              jnp.where(first, x[2 * tb:3 * tb], x[3 * tb:4 * tb])], axis=0)
            dkacc[out_rows, :] += pick(dkc)
            dvacc[out_rows, :] += pick(dvc)

        b0, b1 = (z0, r0, s0, a0), (z1, r1, s1, a1)

        @pl.when(i > 0)
        def _():
            front(0, b0)

            def loop(u, c):
                front(2 * u + 1, b1)
                back(2 * u, b0)
                front(2 * u + 2, b0)
                back(2 * u + 1, b1)
                return c

            lax.fori_loop(0, i - 1, loop, 0)
            front(2 * i - 1, b1)
            back(2 * i - 2, b0)
            back(2 * i - 1, b1)

        sooner, later, late_rows = key < tq, key + 2 * tb < tq, slice(2 * tb, 4 * tb)
        front(2 * i, b0, sooner)
        front(2 * i + 1, b1, later, late_rows)
        back(2 * i, b0, sooner)
        back(2 * i + 1, b1, later, late_rows)

        dq_ref[...] = (dqacc[...] * scale).astype(BF16)

        @pl.when(i == nq - 1)
        def _():
            dk_ref[...] = dkacc[...].astype(BF16)
            dv_ref[...] = dvacc[...].astype(BF16)

    col = pl.BlockSpec((t, LANES), lambda p, i: (0, p))
    out = jax.ShapeDtypeStruct((t, SB_WIDTH), BF16)
    tile = pltpu.VMEM((ATT_Q, 4 * tb), F32)
    return _call(
        body, (qkv, qkv, qkv, dmixed, carries), name="attn_bwd", grid=(n_pairs, nq),
        in_specs=[q_spec, k_spec, v_spec, pl.BlockSpec((ATT_Q, LANES), lambda p, i: (i, gm_blocks + p)),
                  pl.BlockSpec((1, ATT_Q, LANES), lambda p, i: (p, i, 0))],
        out_specs=[pl.BlockSpec((ATT_Q, LANES), lambda p, i: (i, p)), col, col],
        out_shape=[out, out, out],
        scratch_shapes=[pltpu.VMEM((2 * t, LANES), BF16), pltpu.VMEM((2 * t, LANES), BF16),
                        pltpu.VMEM((t, LANES), F32), pltpu.VMEM((t, LANES), F32),
                        pltpu.VMEM((ATT_Q, LANES), F32), pltpu.VMEM((ATT_Q, 2 * tb), F32)] + [tile] * 8,
        exchange=exchange, steps=(0, n_pairs * nq - 1))


def _matmul_residual(res, a, w, name):
    t = a.shape[0]
    tm = _token_tile(t)

    def body(res_ref, a_ref, w_ref, o_ref):
        o_ref[...] = res_ref[...] + _dot(a_ref[...], w_ref[...])

    return pl.pallas_call(
        body, name=name, grid=(t // tm,),
        in_specs=[_rows(tm, res.shape[1]), _rows(tm, a.shape[1]), _resident(w.shape)],
        out_specs=_rows(tm, res.shape[1]),
        out_shape=jax.ShapeDtypeStruct(res.shape, F32),
        compiler_params=_params(),
    )(res, a, w)


def _matmul_nt_cast(dy, w, name):
    t = dy.shape[0]
    tm = _token_tile(t)

    def body(dy_ref, w_ref, o_ref, dyb_ref):
        dyb = dy_ref[...].astype(BF16)
        dyb_ref[...] = dyb
        o_ref[...] = _dot_nt(dyb, w_ref[...]).astype(BF16)

    return pl.pallas_call(
        body, name=name, grid=(t // tm,),
        in_specs=[_rows(tm, dy.shape[1]), _resident(w.shape)],
        out_specs=[_rows(tm, w.shape[0]), _rows(tm, dy.shape[1])],
        out_shape=[jax.ShapeDtypeStruct((t, w.shape[0]), BF16), jax.ShapeDtypeStruct(dy.shape, BF16)],
        compiler_params=_params(),
    )(dy, w)


def _norm_input_bwd(dres, dz, w, h, g, name):
    t = h.shape[0]
    tm = _token_tile(t)
    nb, _, width = w.shape

    def body(dres_ref, dz_ref, w_ref, h_ref, g_ref, dh_ref, n_ref, dg_ref):
        i = pl.program_id(0)
        hh = h_ref[...]
        gg = g_ref[...]
        r = _rstd(hh)
        n_ref[...] = (hh * r * gg).astype(BF16)
        dn = jnp.zeros((tm, D_MODEL), F32)
        for b in range(nb):
            dn = dn + _dot_nt(dz_ref[:, b * width:(b + 1) * width], w_ref[b])
        dh, dg_rows = _rms_bwd(dn, hh, r, gg)
        dh_ref[...] = dres_ref[...] + dh

        @pl.when(i == 0)
        def _():
            dg_ref[...] = jnp.zeros_like(dg_ref)

        dg_ref[...] += jnp.sum(dg_rows, axis=0, keepdims=True)

    return pl.pallas_call(
        body, name=name, grid=(t // tm,),
        in_specs=[_rows(tm, D_MODEL), _rows(tm, nb * width), _resident(w.shape), _rows(tm, D_MODEL),
                  _resident((1, D_MODEL))],
        out_specs=[_rows(tm, D_MODEL), _rows(tm, D_MODEL), pl.BlockSpec((1, D_MODEL), lambda i: (0, 0))],
        out_shape=[jax.ShapeDtypeStruct((t, D_MODEL), F32), jax.ShapeDtypeStruct((t, D_MODEL), BF16),
                   jax.ShapeDtypeStruct((1, D_MODEL), F32)],
        compiler_params=_params(),
    )(dres, dz, w, h, g)


def _head(h, p, target, gple, gfin, wg, wproj):
    t = h.shape[0]
    tm = _token_tile(t)
    pw = D_MODEL // N_CHIPS

    def body(h_ref, p_ref, tgt_ref, gple_ref, gfin_ref, wg_ref, wproj_ref,
             loss_ref, dgf_ref, dgple_ref, dh_ref, dgp_ref, dpp_ref, n_ref, pb_ref):
        i = pl.program_id(0)
        hh = h_ref[...]
        r_in = _rstd(hh)
        gp = gple_ref[...]
        n = (hh * r_in * gp).astype(BF16)
        n_ref[...] = n
        gate = jax.nn.sigmoid(_dot(n, wg_ref[...]))
        pb = p_ref[...].astype(BF16)
        pb_ref[...] = pb
        pp = jnp.concatenate([_dot(pb, wproj_ref[b]) for b in range(N_CHIPS)], axis=1)
        h4 = hh + gate * pp
        r = _rstd(h4)
        gf = gfin_ref[...]
        err = h4 * r * gf - tgt_ref[...]
        dy = err * (1.0 / D_MODEL)
        dh4, dgf_rows = _rms_bwd(dy, h4, r, gf)
        dgp = (dh4 * pp * gate * (1.0 - gate)).astype(BF16)
        dgp_ref[...] = dgp
        dpp_ref[...] = (dh4 * gate).astype(BF16)
        dh, dgple_rows = _rms_bwd(_dot_nt(dgp, wg_ref[...]), hh, r_in, gp)
        dh_ref[...] = dh4 + dh

        @pl.when(i == 0)
        def _():
            loss_ref[...] = jnp.zeros_like(loss_ref)
            dgf_ref[...] = jnp.zeros_like(dgf_ref)
            dgple_ref[...] = jnp.zeros_like(dgple_ref)

        loss_ref[...] += (0.5 / D_MODEL) * jnp.sum(err * err)
        dgf_ref[...] += jnp.sum(dgf_rows, axis=0, keepdims=True)
        dgple_ref[...] += jnp.sum(dgple_rows, axis=0, keepdims=True)

    bf = lambda w: jax.ShapeDtypeStruct((t, w), BF16)
    const = lambda i: (0, 0)
    return pl.pallas_call(
        body, name="head", grid=(t // tm,),
        in_specs=[_rows(tm, D_MODEL), _rows(tm, PLE_DIM), _rows(tm, D_MODEL), _resident((1, D_MODEL)),
                  _resident((1, D_MODEL)), _resident(wg.shape), _resident(wproj.shape)],
        out_specs=[pl.BlockSpec((1, LANES), const), pl.BlockSpec((1, D_MODEL), const), pl.BlockSpec((1, D_MODEL), const),
                   _rows(tm, D_MODEL), _rows(tm, D_MODEL), _rows(tm, D_MODEL), _rows(tm, D_MODEL), _rows(tm, PLE_DIM)],
        out_shape=[jax.ShapeDtypeStruct((1, LANES), F32), jax.ShapeDtypeStruct((1, D_MODEL), F32),
                   jax.ShapeDtypeStruct((1, D_MODEL), F32), jax.ShapeDtypeStruct((t, D_MODEL), F32), bf(D_MODEL),
                   bf(D_MODEL), bf(D_MODEL), bf(PLE_DIM)],
        compiler_params=_params(),
    )(h, p, target, gple, gfin, wg, wproj)


_BIG = ("ffn1_w_in", "ffn1_w_out", "w_mix_in", "w_mix_out", "ffn2_w_in", "ffn2_w_out", "ple_w_gate", "ple_w_proj")
_SMALL = ("ffn1_norm", "mix_norm", "gmlp_v_norm", "gmlp_w_s", "gmlp_b", "ffn2_norm", "ple_norm", "final_norm")
_ALL = ("ffn1_norm", "ffn1_w_in", "ffn1_w_out", "mix_norm", "w_mix_in", "gmlp_v_norm", "gmlp_w_s", "gmlp_b", "w_mix_out",
        "ffn2_norm", "ffn2_w_in", "ffn2_w_out", "ple_norm", "ple_w_gate", "ple_w_proj", "final_norm")
_ANY = pl.BlockSpec(memory_space=pl.ANY)
_MESH = pl.DeviceIdType.MESH


def _mesh_pos():
    return lax.axis_index("x"), lax.axis_index("y"), lax.axis_index("c")


def _other_chips(x, y):
    return [((x, 1 - y), 2 * x + 1 - y), ((1 - x, y), 2 * (1 - x) + y), ((1 - x, 1 - y), 2 * (1 - x) + 1 - y)]


def _remote(src, dst, send_sem, recv_sem, device):
    return pltpu.make_async_remote_copy(src_ref=src, dst_ref=dst, send_sem=send_sem, recv_sem=recv_sem,
                                        device_id=device, device_id_type=_MESH)


class _WeightGather:
    def __init__(self, shards):
        self.shapes = [s.shape for s in shards]
        self.operands = list(shards)
        self.out_shape = [jax.ShapeDtypeStruct((N_CHIPS, *s.shape), s.dtype) for s in shards]
        n = len(shards)
        self.per = 2 * (N_CHIPS - 1)
        self.scratch = [pltpu.SemaphoreType.DMA((self.per * n,)), pltpu.SemaphoreType.DMA((self.per * n,)),
                        pltpu.SemaphoreType.DMA((n,))]
        self.phases = [self.send, self.forward, self.finish]

    def _copies(self, ins, outs, sems):
        send_sems, recv_sems, local_sems = sems
        x, y, c = _mesh_pos()
        sibling = (x, y, 1 - c)
        mine = 2 * x + y
        local, first, landing, passed, arriving = [], [], [], [], []
        for w, shape in enumerate(self.shapes):
            hr = shape[0] // 2
            half = lambda blk, cc, w=w, hr=hr: outs[w].at[blk, pl.ds(cc * hr, hr), :]
            local.append(pltpu.make_async_copy(ins[w], outs[w].at[mine], local_sems.at[w]))
            for k, (chip, blk) in enumerate(_other_chips(x, y)):
                s = self.per * w + k
                first.append(_remote(ins[w].at[pl.ds(c * hr, hr), :], half(mine, c), send_sems.at[s], recv_sems.at[s],
                                     (*chip, c)))
                landing.append(_remote(half(blk, c), half(blk, c), send_sems.at[s], recv_sems.at[s], sibling))
                s = self.per * w + N_CHIPS - 1 + k
                passed.append(_remote(half(blk, c), half(blk, c), send_sems.at[s], recv_sems.at[s], sibling))
                arriving.append(_remote(half(blk, 1 - c), half(blk, 1 - c), send_sems.at[s], recv_sems.at[s], sibling))
        return local, first, landing, passed, arriving

    def send(self, ins, outs, sems):
        local, first, _, _, _ = self._copies(ins, outs, sems)
        for cp in local + first:
            cp.start()

    def forward(self, ins, outs, sems):
        _, _, landing, passed, _ = self._copies(ins, outs, sems)
        for landed, cp in zip(landing, passed):
            landed.wait_recv()
            cp.start()

    def finish(self, ins, outs, sems):
        local, first, _, passed, arriving = self._copies(ins, outs, sems)
        for cp in arriving:
            cp.wait_recv()
        for cp in first + passed:
            cp.wait_send()
        for cp in local:
            cp.wait()


class _ChipExchange:
    def __init__(self, sums):
        n = len(sums)
        self.n = n
        self.per = N_CHIPS - 1
        self.operands = list(sums)
        self.out_shape = [jax.ShapeDtypeStruct((self.per, *s.shape[1:]), s.dtype) for s in sums]
        self.scratch = [pltpu.SemaphoreType.DMA((self.per * n,)), pltpu.SemaphoreType.DMA((self.per * n,))]
        self.phases = [self.send, self.finish]

    def _copies(self, ins, outs, sems):
        send_sems, recv_sems = sems
        x, y, c = _mesh_pos()
        cps = []
        for w in range(self.n):
            for k, (chip, _) in enumerate(_other_chips(x, y)):
                s = self.per * w + k
                cps.append(_remote(ins[w].at[k + 1], outs[w].at[k], send_sems.at[s], recv_sems.at[s], (*chip, c)))
        return cps

    def send(self, ins, outs, sems):
        for cp in self._copies(ins, outs, sems):
            cp.start()

    def finish(self, ins, outs, sems):
        for cp in self._copies(ins, outs, sems):
            cp.wait()


def _run_exchange(ex, name):
    n_in, n_out = len(ex.operands), len(ex.out_shape)

    def body(*refs):
        ins, outs, sems = refs[:n_in], refs[n_in:n_in + n_out], refs[n_in + n_out:]
        for phase in ex.phases:
            phase(ins, outs, sems)

    return pl.pallas_call(body, name=name, in_specs=[_ANY] * n_in, out_specs=[_ANY] * n_out, out_shape=ex.out_shape,
                          scratch_shapes=ex.scratch)(*ex.operands)


def _call(body, args, *, name, grid, in_specs, out_specs, out_shape, scratch_shapes=(), exchange=None, steps=None):
    params = _params(len(grid))
    if exchange is None:
        out = pl.pallas_call(body, name=name, grid=grid, in_specs=in_specs, out_specs=out_specs, out_shape=out_shape,
                             scratch_shapes=list(scratch_shapes), compiler_params=params)(*args)
        return out, None
    n_in, n_out, n_scr = len(in_specs), len(out_specs), len(scratch_shapes)
    n_xin, n_xout = len(exchange.operands), len(exchange.out_shape)
    assert len(steps) == len(exchange.phases)

    def hosting(*refs):
        cuts = [n_in, n_xin, n_out, n_xout, n_scr]
        parts, at = [], 0
        for size in cuts:
            parts.append(refs[at:at + size])
            at += size
        ins, xins, outs, xouts, scr = parts
        sems = refs[at:]
        step = 0
        for axis, size in enumerate(grid):
            step = step * size + pl.program_id(axis)
        pl.when(step == steps[0])(lambda: exchange.phases[0](xins, xouts, sems))
        body(*ins, *outs, *scr)
        for at_step, phase in zip(steps[1:], exchange.phases[1:]):
            pl.when(step == at_step)(functools.partial(phase, xins, xouts, sems))

    out = pl.pallas_call(
        hosting, name=name, grid=grid,
        in_specs=list(in_specs) + [_ANY] * n_xin, out_specs=list(out_specs) + [_ANY] * n_xout,
        out_shape=list(out_shape) + list(exchange.out_shape),
        scratch_shapes=list(scratch_shapes) + list(exchange.scratch), compiler_params=params,
    )(*args, *exchange.operands)
    return out[:n_out], out[n_out:]


def _pair_exchange(grads, name):
    n = len(grads)

    def body(*refs):
        ins, outs = refs[:n], refs[n:2 * n]
        send_sems, recv_sems = refs[2 * n:]
        x, y, c = _mesh_pos()
        cps = []
        for w in range(n):
            hr = grads[w].shape[1] // 2
            cp = _remote(ins[w].at[:, pl.ds((1 - c) * hr, hr), :], outs[w], send_sems.at[w], recv_sems.at[w], (x, y, 1 - c))
            cp.start()
            cps.append(cp)
        for cp in cps:
            cp.wait()

    return pl.pallas_call(
        body, name=name,
        in_specs=[_ANY] * n, out_specs=[_ANY] * n,
        out_shape=[jax.ShapeDtypeStruct((g.shape[0], g.shape[1] // 2, g.shape[2]), g.dtype) for g in grads],
        scratch_shapes=[pltpu.SemaphoreType.DMA((n,)), pltpu.SemaphoreType.DMA((n,))],
    )(*grads)


def _pair_sum(g, a, pos, name):
    nb, r, c = g.shape
    hr = r // 2

    def body(pos_ref, g_ref, a_ref, o_ref):
        o_ref[...] = (g_ref[...] + a_ref[...].astype(F32)).astype(BF16)

    return pl.pallas_call(
        body, name=name,
        grid_spec=pltpu.PrefetchScalarGridSpec(
            num_scalar_prefetch=1, grid=(nb,),
            in_specs=[pl.BlockSpec((1, hr, c), lambda k, pos: (k ^ pos[0], pos[1], 0)),
                      pl.BlockSpec((1, hr, c), lambda k, pos: (k ^ pos[0], 0, 0))],
            out_specs=pl.BlockSpec((1, hr, c), lambda k, pos: (k, 0, 0))),
        out_shape=jax.ShapeDtypeStruct((nb, hr, c), BF16),
        compiler_params=_params(),
    )(pos, g, a)


def _chip_sum(s, b, pos, name):
    _, hr, c = s.shape

    def body(pos_ref, s_ref, b_ref, o_ref):
        o_ref[...] = (s_ref[0].astype(F32) + b_ref[0].astype(F32)) + (b_ref[1].astype(F32) + b_ref[2].astype(F32))

    return pl.pallas_call(
        body, name=name,
        grid_spec=pltpu.PrefetchScalarGridSpec(
            num_scalar_prefetch=1, grid=(1,),
            in_specs=[pl.BlockSpec((1, hr, c), lambda k, pos: (0, 0, 0)), pl.BlockSpec((N_CHIPS - 1, hr, c), lambda k, pos: (0, 0, 0))],
            out_specs=pl.BlockSpec((hr, c), lambda k, pos: (pos[1], 0))),
        out_shape=jax.ShapeDtypeStruct((2 * hr, c), F32),
        compiler_params=_params(),
    )(pos, s, b)


def _pair_share(grads, name):
    n = len(grads)

    def body(*refs):
        outs = refs[n:2 * n]
        send_sems, recv_sems = refs[2 * n:]
        x, y, c = _mesh_pos()
        cps = []
        for w in range(n):
            hr = grads[w].shape[0] // 2
            rows = outs[w].at[pl.ds(c * hr, hr), :]
            cp = _remote(rows, rows, send_sems.at[w], recv_sems.at[w], (x, y, 1 - c))
            cp.start()
            cps.append(cp)
        for w, cp in enumerate(cps):
            cp.wait_send()
            hr = grads[w].shape[0] // 2
            other = outs[w].at[pl.ds((1 - c) * hr, hr), :]
            _remote(other, other, send_sems.at[w], recv_sems.at[w], (x, y, 1 - c)).wait_recv()

    return pl.pallas_call(
        body, name=name,
        in_specs=[_ANY] * n, out_specs=[_ANY] * n,
        out_shape=[jax.ShapeDtypeStruct(g.shape, g.dtype) for g in grads],
        input_output_aliases={w: w for w in range(n)},
        scratch_shapes=[pltpu.SemaphoreType.DMA((n,)), pltpu.SemaphoreType.DMA((n,))],
    )(*grads)


def _all_reduce_small(rows, mats):
    n_sems = 2 * N_CHIPS
    n_rows = -(-len(rows) // 8) * 8
    heights = [math.prod(a.shape[:-1]) for a in mats]
    n_tall = -(-sum(heights) // 8) * 8
    arrays = list(rows) + list(mats)

    def body(*refs):
        ins, outs = refs[:len(arrays)], refs[len(arrays):2 * len(arrays)]
        wide, tall, wide_pair, tall_pair, wide_slots, tall_slots, send_sems, recv_sems = refs[2 * len(arrays):]
        x, y, c = _mesh_pos()
        wide[...] = jnp.zeros_like(wide)
        tall[...] = jnp.zeros_like(tall)
        for k, a in enumerate(rows):
            wide[k:k + 1, 0:a.shape[1]] = ins[k][...]
        at = 0
        for k, h in enumerate(heights):
            tall[at:at + h, :] = ins[len(rows) + k][...].reshape(h, LANES)
            at += h
        wide_pair[c] = wide[...]
        tall_pair[c] = tall[...]
        cps = [_remote(wide, wide_pair.at[c], send_sems.at[0], recv_sems.at[0], (x, y, 1 - c)),
               _remote(tall, tall_pair.at[c], send_sems.at[1], recv_sems.at[1], (x, y, 1 - c))]
        for cp in cps:
            cp.start()
        for cp in cps:
            cp.wait()
        wide[...] = wide_pair[0] + wide_pair[1]
        tall[...] = tall_pair[0] + tall_pair[1]
        mine = 2 * x + y
        wide_slots[mine] = wide[...]
        tall_slots[mine] = tall[...]
        cps = []
        for k, (chip, _) in enumerate(_other_chips(x, y)):
            for j, (buf, slots) in enumerate(((wide, wide_slots), (tall, tall_slots))):
                s = 2 + 2 * k + j
                cps.append(_remote(buf, slots.at[mine], send_sems.at[s], recv_sems.at[s], (*chip, c)))
        for cp in cps:
            cp.start()
        for cp in cps:
            cp.wait()
        wide_sum, tall_sum = wide_slots[0], tall_slots[0]
        for d in range(1, N_CHIPS):
            wide_sum = wide_sum + wide_slots[d]
            tall_sum = tall_sum + tall_slots[d]
        for k, a in enumerate(rows):
            outs[k][...] = wide_sum[k:k + 1, 0:a.shape[1]]
        at = 0
        for k, h in enumerate(heights):
            outs[len(rows) + k][...] = tall_sum[at:at + h, :].reshape(mats[k].shape)
            at += h

    vmem = pl.BlockSpec(memory_space=pltpu.VMEM)
    return pl.pallas_call(
        body, name="all_reduce_small",
        in_specs=[vmem] * len(arrays), out_specs=[vmem] * len(arrays),
        out_shape=[jax.ShapeDtypeStruct(a.shape, F32) for a in arrays],
        scratch_shapes=[pltpu.VMEM((n_rows, D_MODEL), F32), pltpu.VMEM((n_tall, LANES), F32),
                        pltpu.VMEM((2, n_rows, D_MODEL), F32), pltpu.VMEM((2, n_tall, LANES), F32),
                        pltpu.VMEM((N_CHIPS, n_rows, D_MODEL), F32), pltpu.VMEM((N_CHIPS, n_tall, LANES), F32),
                        pltpu.SemaphoreType.DMA((n_sems,)), pltpu.SemaphoreType.DMA((n_sems,))],
    )(*arrays)


def _adamw_small(items):
    n = len(items)
    bias1 = 1.0 - ADAM_B1 ** ADAM_STEP
    bias2 = 1.0 - ADAM_B2 ** ADAM_STEP

    def body(*refs):
        ins, outs = refs[:4 * n], refs[4 * n:]
        for k in range(n):
            w_ref, g_ref, m_ref, v_ref = ins[4 * k:4 * k + 4]
            gg = g_ref[...]
            m2 = ADAM_B1 * m_ref[...] + (1.0 - ADAM_B1) * gg
            v2 = ADAM_B2 * v_ref[...] + (1.0 - ADAM_B2) * (gg * gg)
            outs[3 * k + 1][...] = m2
            outs[3 * k + 2][...] = v2
            outs[3 * k][...] = -ADAM_LR * ((m2 / bias1) / (jnp.sqrt(v2 / bias2) + ADAM_EPS) + ADAM_WD * w_ref[...])

    vmem = pl.BlockSpec(memory_space=pltpu.VMEM)
    out = pl.pallas_call(
        body, name="adamw_small", in_specs=[vmem] * (4 * n), out_specs=[vmem] * (3 * n),
        out_shape=[jax.ShapeDtypeStruct(w.shape, F32) for w, _, _, _ in items for _ in range(3)],
    )(*[a for item in items for a in item])
    return [tuple(out[3 * k:3 * k + 3]) for k in range(n)]


ADAMW_STEPS = 8


def _adamw(items, name, exchange=None):
    n = len(items)
    bias1 = 1.0 - ADAM_B1 ** ADAM_STEP
    bias2 = 1.0 - ADAM_B2 ** ADAM_STEP

    def body(*refs):
        ins, outs = refs[:4 * n], refs[4 * n:]
        for k in range(n):
            w_ref, g_ref, m_ref, v_ref = ins[4 * k:4 * k + 4]
            d_ref, mo_ref, vo_ref = outs[3 * k:3 * k + 3]
            gg = g_ref[...]
            m2 = ADAM_B1 * m_ref[...] + (1.0 - ADAM_B1) * gg
            v2 = ADAM_B2 * v_ref[...] + (1.0 - ADAM_B2) * (gg * gg)
            mo_ref[...] = m2
            vo_ref[...] = v2
            d_ref[...] = -ADAM_LR * ((m2 / bias1) / (jnp.sqrt(v2 / bias2) + ADAM_EPS) + ADAM_WD * w_ref[...])

    in_specs, out_specs, out_shape, args = [], [], [], []
    for w, g, m, v in items:
        r, c = w.shape
        steps = ADAMW_STEPS if r % (8 * ADAMW_STEPS) == 0 else 1
        assert steps == ADAMW_STEPS or n == 1
        spec = pl.BlockSpec((r // steps, c), lambda i: (i, 0))
        in_specs += [spec] * 4
        out_specs += [spec] * 3
        out_shape += [jax.ShapeDtypeStruct((r, c), F32)] * 3
        args += [w, g, m, v]
    out, got = _call(body, args, name=name, grid=(steps,), in_specs=in_specs, out_specs=out_specs, out_shape=out_shape,
                     exchange=exchange, steps=(0, steps - 1))
    return [tuple(out[3 * k:3 * k + 3]) for k in range(n)], got


def kernel(x, p, ffn1_norm, ffn1_w_in, ffn1_w_out, mix_norm, w_mix_in, gmlp_v_norm, gmlp_w_s, gmlp_b, w_mix_out, ffn2_norm, ffn2_w_in, ffn2_w_out, ple_norm, ple_w_gate, ple_w_proj, final_norm, loss_target, m_ffn1_norm, m_ffn1_w_in, m_ffn1_w_out, m_mix_norm, m_w_mix_in, m_gmlp_v_norm, m_gmlp_w_s, m_gmlp_b, m_w_mix_out, m_ffn2_norm, m_ffn2_w_in, m_ffn2_w_out, m_ple_norm, m_ple_w_gate, m_ple_w_proj, m_final_norm, v_ffn1_norm, v_ffn1_w_in, v_ffn1_w_out, v_mix_norm, v_w_mix_in, v_gmlp_v_norm, v_gmlp_w_s, v_gmlp_b, v_w_mix_out, v_ffn2_norm, v_ffn2_w_in, v_ffn2_w_out, v_ple_norm, v_ple_w_gate, v_ple_w_proj, v_final_norm):
    args = dict(locals())
    w = {n: args[n] for n in _ALL}
    m = {n: args["m_" + n] for n in _ALL}
    v = {n: args["v_" + n] for n in _ALL}
    xi, yi, ci = _mesh_pos()
    pos = jnp.stack([2 * xi + yi, ci]).astype(jnp.int32)
    shard = {n: w[n][0] for n in _BIG}
    cast = {n: shard[n].astype(BF16) for n in _BIG}
    small = {n: (w[n][0] if w[n].ndim > 2 else w[n].reshape(1, -1)) for n in _SMALL}
    bt = small["gmlp_b"].T
    g_small, pair, from_chips = {}, {}, {}

    def pair_reduce(partials, tag):
        names = list(partials)
        blocks = lambda a, n: a.reshape(N_CHIPS, *shard[n].shape)
        got = _pair_exchange([blocks(partials[n][1], n) for n in names], "grad_pair_exchange_" + tag)
        for n, a in zip(names, got):
            pair[n] = _pair_sum(blocks(partials[n][0], n), a, pos, "pair_sum_" + n)
        return names

    w1in, w1out = _run_exchange(_WeightGather([cast["ffn1_w_in"], cast["ffn1_w_out"]]), "gather_ffn1")
    w1out = w1out.reshape(D_FF, D_MODEL)
    (h1, gu1), (wmix, wmo) = _ffn_fwd(x[0], small["ffn1_norm"], w1in, w1out, "ffn1_fwd",
                                      _WeightGather([cast["w_mix_in"], cast["w_mix_out"]]))
    wmo = wmo.reshape(D_MODEL, D_MODEL)
    zg, qkv = _mix_in_fwd(h1, small["mix_norm"], wmix)
    gm = _gmlp_fwd(zg, small["gmlp_v_norm"], small["gmlp_w_s"], bt)
    (att, carries), (w2in, w2out, wg, wproj) = _attn_fwd(
        qkv, _WeightGather([cast["ffn2_w_in"], cast["ffn2_w_out"], cast["ple_w_gate"], cast["ple_w_proj"]]))
    w2out = w2out.reshape(D_FF, D_MODEL)
    wg = wg.reshape(D_MODEL, D_MODEL)
    mixed = jnp.concatenate([gm, att], axis=1)
    h2 = _matmul_residual(h1, mixed, wmo, "mix_out_fwd")
    (h3, gu2), _ = _ffn_fwd(h2, small["ffn2_norm"], w2in, w2out, "ffn2_fwd")
    loss_part, g_small["final_norm"], g_small["ple_norm"], dh3, dgp, dpp, n4, pb = _head(
        h3, p[0, 0], loss_target[0], small["ple_norm"], small["final_norm"], wg, wproj)

    part = {"ple_w_gate": _wgrad_rows(n4, dgp, N_CHIPS, "wgrad_ple_gate"),
            "ple_w_proj": _wgrad_cols(pb, dpp, N_CHIPS, "wgrad_ple_proj")}
    (dh2, dgu2, n3, act2, dhh3, g_small["ffn2_norm"]), _ = _ffn_bwd(dh3, h2, small["ffn2_norm"], gu2, w2in, w2out,
                                                                   "ffn2_bwd")
    part["ffn2_w_in"] = _wgrad_cols(n3, dgu2, N_CHIPS, "wgrad_ffn2_in")
    part["ffn2_w_out"] = _wgrad_rows(act2, dhh3, 2, "wgrad_ffn2_out")
    dmixed, dh2b = _matmul_nt_cast(dh2, wmo, "mix_out_bwd")
    part["w_mix_out"] = _wgrad_rows(mixed, dh2b, 2, "wgrad_mix_out")
    group = pair_reduce(part, "late")
    dzg, g_small["gmlp_w_s"], dbt, g_small["gmlp_v_norm"] = _gmlp_bwd(zg, dmixed, small["gmlp_v_norm"],
                                                                      small["gmlp_w_s"], bt)
    g_small["gmlp_b"] = dbt.T
    (dq, dk, dv), got = _attn_bwd(qkv, dmixed, carries, _ChipExchange([pair[n] for n in group]))
    from_chips.update(zip(group, got))

    dzmix = jnp.concatenate([dzg, dq, dk, dv], axis=1)
    dh1, n2, g_small["mix_norm"] = _norm_input_bwd(dh2, dzmix, wmix, h1, small["mix_norm"], "mix_in_bwd")
    group = pair_reduce({"w_mix_in": _wgrad_cols(n2, dzmix, N_CHIPS, "wgrad_mix_in")}, "mix")
    (dx, dgu1, n1, act1, dhh1, g_small["ffn1_norm"]), _ = _ffn_bwd(dh1, x[0], small["ffn1_norm"], gu1, w1in, w1out,
                                                                   "ffn1_bwd")

    g_out, got = _wgrad_rows(act1, dhh1, 2, "wgrad_ffn1_out", _ChipExchange([pair[n] for n in group]))
    from_chips.update(zip(group, got))
    pair_reduce({"ffn1_w_out": g_out}, "out")
    g_in, got = _wgrad_cols(n1, dgu1, N_CHIPS, "wgrad_ffn1_in", _ChipExchange([pair["ffn1_w_out"]]))
    from_chips["ffn1_w_out"] = got[0]
    pair_reduce({"ffn1_w_in": g_in}, "in")

    def finish(names, tag, exchange=None):
        halves = [_chip_sum(pair[n], from_chips[n], pos, "chip_sum_" + n) for n in names]
        full = _pair_share(halves, "grad_pair_share_" + tag)
        out, got = _adamw([(shard[n], g, m[n][0], v[n][0]) for n, g in zip(names, full)], "adamw_" + tag, exchange)
        for n, g, (d2, m2, v2) in zip(names, full, out):
            grads[n], delta[n], new_m[n], new_v[n] = g[None], d2[None], m2[None], v2[None]
        return got

    grads, delta, new_m, new_v = {}, {}, {}, {}
    got = finish([n for n in _BIG if n != "ffn1_w_in"], "most", _ChipExchange([pair["ffn1_w_in"]]))
    from_chips["ffn1_w_in"] = got[0]
    finish(["ffn1_w_in"], "last")

    rows = [n for n in _SMALL if g_small[n].shape[0] == 1]
    mats = [n for n in _SMALL if n not in rows]
    summed = _all_reduce_small([g_small[n] for n in rows] + [loss_part], [g_small[n] for n in mats])
    loss = summed[len(rows)][0, 0]
    g_sum = dict(zip(rows + mats, summed[:len(rows)] + summed[len(rows) + 1:]))
    like = lambda a, n: a[n].reshape(small[n].shape)
    out = _adamw_small([(small[n], g_sum[n], like(m, n), like(v, n)) for n in _SMALL])
    for n, (d2, m2, v2) in zip(_SMALL, out):
        grads[n], delta[n], new_m[n], new_v[n] = (a.reshape(w[n].shape) for a in (g_sum[n], d2, m2, v2))

    return (loss, dx[None], *[grads[n] for n in _ALL], *[delta[n] for n in _ALL], *[new_m[n] for n in _ALL],
            *[new_v[n] for n in _ALL])
```

```python
import functools
import math

import jax
import jax.numpy as jnp
from jax import lax
from jax.experimental import pallas as pl
from jax.experimental.pallas import tpu as pltpu

F32, BF16 = jnp.float32, jnp.bfloat16

D_MODEL = 1024
D_FF = 2816
FF_BLOCK = 2 * D_FF // 4
PLE_DIM = 256
CHUNK = 128
GM_HEADS = 4
GM_WIDTH = 512
SB_HEAD_DIM = 64
SB_WIDTH = 512
MIX_IN_WIDTH = 2 * GM_WIDTH + 3 * SB_WIDTH
MIX_BLOCK = MIX_IN_WIDTH // 4
EPS = 1e-6
N_CHIPS = 4
LANES = 128
ATT_BLOCK = 128
ATT_Q = 512
VMEM_LIMIT = 56 * 1024 * 1024

ADAM_LR, ADAM_B1, ADAM_B2, ADAM_EPS, ADAM_WD, ADAM_STEP = 0.001, 0.9, 0.999, 1e-08, 0.01, 10


def _dot(a, b):
    return jnp.dot(a, b, preferred_element_type=F32)


def _dot_nt(a, b):
    return lax.dot_general(a, b, (((1,), (1,)), ((), ())), preferred_element_type=F32)


def _dot_tn(a, b):
    return lax.dot_general(a, b, (((0,), (0,)), ((), ())), preferred_element_type=F32)


def _resident(shape):
    nd = len(shape)
    return pl.BlockSpec(shape, lambda *_: (0,) * nd, pipeline_mode=pl.Buffered(1))


def _rows(tm, width):
    return pl.BlockSpec((tm, width), lambda i: (i, 0))


def _params(n_axes=1):
    return pltpu.CompilerParams(dimension_semantics=("arbitrary",) * n_axes, vmem_limit_bytes=VMEM_LIMIT)


def _rstd(h):
    return lax.rsqrt(jnp.mean(h * h, axis=-1, keepdims=True) + EPS)


def _rms_bwd(dy, h, r, g):
    dyg = dy * g
    dh = r * dyg - h * (r * r * r) * jnp.mean(dyg * h, axis=-1, keepdims=True)
    return dh, dy * h * r


def _gelu(x):
    return 0.5 * x * (1.0 + lax.erf(x * (2.0 ** -0.5)))


def _gelu_grad(x):
    return 0.5 * (1.0 + lax.erf(x * (2.0 ** -0.5))) + x * jnp.exp(-0.5 * x * x) * ((2.0 * jnp.pi) ** -0.5)


def _token_tile(t):
    return min(256, t)


def _ffn_fwd(h, g, win, wout, name, exchange=None):
    t = h.shape[0]
    tm = _token_tile(t)

    def body(h_ref, g_ref, win_ref, wout_ref, ho_ref, gu_ref):
        hh = h_ref[...]
        n = (hh * _rstd(hh) * g_ref[...]).astype(BF16)
        acc = jnp.zeros((tm, D_MODEL), F32)
        for jb in range(2):
            gate = _dot(n, win_ref[jb])
            up = _dot(n, win_ref[2 + jb])
            gu_ref[:, jb * FF_BLOCK:(jb + 1) * FF_BLOCK] = gate.astype(BF16)
            gu_ref[:, D_FF + jb * FF_BLOCK:D_FF + (jb + 1) * FF_BLOCK] = up.astype(BF16)
            act = (gate * jax.nn.sigmoid(gate) * up).astype(BF16)
            acc = acc + _dot(act, wout_ref[jb * FF_BLOCK:(jb + 1) * FF_BLOCK, :])
        ho_ref[...] = hh + 0.5 * acc

    n = t // tm
    return _call(
        body, (h, g, win, wout), name=name, grid=(n,),
        in_specs=[_rows(tm, D_MODEL), _resident((1, D_MODEL)), _resident(win.shape), _resident(wout.shape)],
        out_specs=[_rows(tm, D_MODEL), _rows(tm, 2 * D_FF)],
        out_shape=[jax.ShapeDtypeStruct((t, D_MODEL), F32), jax.ShapeDtypeStruct((t, 2 * D_FF), BF16)],
        exchange=exchange, steps=(0, (2 * n) // 3, n - 1))


def _ffn_bwd(dho, h, g, gu, win, wout, name, exchange=None):
    t = h.shape[0]
    tm = _token_tile(t)

    def body(dho_ref, h_ref, g_ref, gu_ref, win_ref, wout_ref, dh_ref, dgu_ref, n_ref, act_ref, dhh_ref, dg_ref):
        i = pl.program_id(0)
        hh = h_ref[...]
        gg = g_ref[...]
        r = _rstd(hh)
        n_ref[...] = (hh * r * gg).astype(BF16)
        dho = dho_ref[...]
        dhh = (0.5 * dho).astype(BF16)
        dhh_ref[...] = dhh
        dn = jnp.zeros((tm, D_MODEL), F32)
        for jb in range(2):
            cg = slice(jb * FF_BLOCK, (jb + 1) * FF_BLOCK)
            cu = slice(D_FF + jb * FF_BLOCK, D_FF + (jb + 1) * FF_BLOCK)
            dact = _dot_nt(dhh, wout_ref[cg, :])
            gate = gu_ref[:, cg].astype(F32)
            up = gu_ref[:, cu].astype(F32)
            sg = jax.nn.sigmoid(gate)
            silu = gate * sg
            act_ref[:, cg] = (silu * up).astype(BF16)
            dgate = (dact * up * (sg * (1.0 + gate * (1.0 - sg)))).astype(BF16)
            dup = (dact * silu).astype(BF16)
            dgu_ref[:, cg] = dgate
            dgu_ref[:, cu] = dup
            dn = dn + _dot_nt(dgate, win_ref[jb]) + _dot_nt(dup, win_ref[2 + jb])
        dh, dg_rows = _rms_bwd(dn, hh, r, gg)
        dh_ref[...] = dho + dh

        @pl.when(i == 0)
        def _():
            dg_ref[...] = jnp.zeros_like(dg_ref)

        dg_ref[...] += jnp.sum(dg_rows, axis=0, keepdims=True)

    n = t // tm
    return _call(
        body, (dho, h, g, gu, win, wout), name=name, grid=(n,),
        in_specs=[_rows(tm, D_MODEL), _rows(tm, D_MODEL), _resident((1, D_MODEL)), _rows(tm, 2 * D_FF),
                  _resident(win.shape), _resident(wout.shape)],
        out_specs=[_rows(tm, D_MODEL), _rows(tm, 2 * D_FF), _rows(tm, D_MODEL), _rows(tm, D_FF), _rows(tm, D_MODEL),
                   pl.BlockSpec((1, D_MODEL), lambda i: (0, 0))],
        out_shape=[jax.ShapeDtypeStruct((t, D_MODEL), F32), jax.ShapeDtypeStruct((t, 2 * D_FF), BF16),
                   jax.ShapeDtypeStruct((t, D_MODEL), BF16), jax.ShapeDtypeStruct((t, D_FF), BF16),
                   jax.ShapeDtypeStruct((t, D_MODEL), BF16), jax.ShapeDtypeStruct((1, D_MODEL), F32)],
        exchange=exchange, steps=(0, n - 1))


def _wgrad(a, b, out_shape, out_block, out_index, a_width, b_width, grid_ij, name, exchange=None):
    t = a.shape[0]
    tk = min(2048, t)
    nk = t // tk

    def body(a_ref, b_ref, o_ref, ob_ref):
        k = pl.program_id(2)
        prod = _dot_tn(a_ref[...], b_ref[...]).reshape(o_ref.shape)

        @pl.when(k == 0)
        def _():
            o_ref[...] = prod

        @pl.when(k > 0)
        def _():
            o_ref[...] += prod

        @pl.when(k == nk - 1)
        def _():
            ob_ref[...] = o_ref[...].astype(BF16)

    grid = (*grid_ij, nk)
    out_spec = pl.BlockSpec(out_block, lambda i, j, k: out_index(i, j))
    out, got = _call(
        body, (a, b), name=name, grid=grid,
        in_specs=[pl.BlockSpec((tk, a_width), lambda i, j, k: (k, i)), pl.BlockSpec((tk, b_width), lambda i, j, k: (k, j))],
        out_specs=[out_spec, out_spec],
        out_shape=[jax.ShapeDtypeStruct(out_shape, F32), jax.ShapeDtypeStruct(out_shape, BF16)],
        exchange=exchange, steps=(0, grid[0] * grid[1] * grid[2] - 1))
    return tuple(out) if exchange is None else (tuple(out), got)


def _wgrad_cols(a, b, n_blocks, name, exchange=None):
    ka, nb = a.shape[1], b.shape[1] // n_blocks
    return _wgrad(a, b, (n_blocks, ka, nb), (1, ka, nb), lambda i, j: (j, 0, 0), ka, nb, (1, n_blocks), name, exchange)


def _wgrad_rows(a, b, n_blocks, name, exchange=None):
    ka, nb = a.shape[1] // n_blocks, b.shape[1]
    return _wgrad(a, b, (a.shape[1], nb), (ka, nb), lambda i, j: (i, 0), ka, nb, (n_blocks, 1), name, exchange)


def _mix_in_fwd(h, g, wmix):
    t = h.shape[0]
    tm = _token_tile(t)
    gw2 = 2 * GM_WIDTH

    def body(h_ref, g_ref, w_ref, zg_ref, qkv_ref):
        hh = h_ref[...]
        n = (hh * _rstd(hh) * g_ref[...]).astype(BF16)
        for b in range(N_CHIPS):
            z = _dot(n, w_ref[b])
            lo, hi = b * MIX_BLOCK, (b + 1) * MIX_BLOCK
            if hi <= gw2:
                zg_ref[:, lo:hi] = z
            elif lo >= gw2:
                qkv_ref[:, lo - gw2:hi - gw2] = z.astype(BF16)
            else:
                zg_ref[:, lo:gw2] = z[:, :gw2 - lo]
                qkv_ref[:, 0:hi - gw2] = z[:, gw2 - lo:].astype(BF16)

    return pl.pallas_call(
        body, name="mix_in_fwd", grid=(t // tm,),
        in_specs=[_rows(tm, D_MODEL), _resident((1, D_MODEL)), _resident(wmix.shape)],
        out_specs=[_rows(tm, gw2), _rows(tm, 3 * SB_WIDTH)],
        out_shape=[jax.ShapeDtypeStruct((t, gw2), F32), jax.ShapeDtypeStruct((t, 3 * SB_WIDTH), BF16)],
        compiler_params=_params(),
    )(h, g, wmix)


def _causal_chunk_mask():
    row = lax.broadcasted_iota(jnp.int32, (CHUNK, CHUNK), 0)
    col = lax.broadcasted_iota(jnp.int32, (CHUNK, CHUNK), 1)
    return row >= col


def _gmlp_tile(t):
    return min(512, t)


def _gmlp_fwd(zg, gv, ws, bt):
    t = zg.shape[0]
    tm = _gmlp_tile(t)

    def body(zg_ref, gv_ref, ws_ref, bt_ref, o_ref):
        u = _gelu(zg_ref[:, :GM_WIDTH])
        v = _gelu(zg_ref[:, GM_WIDTH:])
        vn = (v * _rstd(v) * gv_ref[...]).astype(BF16)
        mask = _causal_chunk_mask()
        for hd in range(GM_HEADS):
            wm = jnp.where(mask, ws_ref[hd], 0.0).astype(BF16)
            cols = slice(hd * CHUNK, (hd + 1) * CHUNK)
            for c in range(tm // CHUNK):
                rows = slice(c * CHUNK, (c + 1) * CHUNK)
                sv = _dot(wm, vn[rows, cols]) + bt_ref[:, hd:hd + 1]
                o_ref[rows, cols] = (u[rows, cols] * sv).astype(BF16)

    return pl.pallas_call(
        body, name="gmlp_fwd", grid=(t // tm,),
        in_specs=[_rows(tm, 2 * GM_WIDTH), _resident((1, GM_WIDTH)), _resident(ws.shape), _resident(bt.shape)],
        out_specs=_rows(tm, GM_WIDTH),
        out_shape=jax.ShapeDtypeStruct((t, GM_WIDTH), BF16),
        compiler_params=_params(),
    )(zg, gv, ws, bt)


def _gmlp_bwd(zg, dmixed, gv, ws, bt):
    t = zg.shape[0]
    tm = _gmlp_tile(t)

    def body(zg_ref, dgm_ref, gv_ref, ws_ref, bt_ref, dzg_ref, dws_ref, dbt_ref, dgv_ref):
        i = pl.program_id(0)

        @pl.when(i == 0)
        def _():
            dws_ref[...] = jnp.zeros_like(dws_ref)
            dbt_ref[...] = jnp.zeros_like(dbt_ref)
            dgv_ref[...] = jnp.zeros_like(dgv_ref)

        zu = zg_ref[:, :GM_WIDTH]
        zv = zg_ref[:, GM_WIDTH:]
        u = _gelu(zu)
        v = _gelu(zv)
        r = _rstd(v)
        gvv = gv_ref[...]
        vn = (v * r * gvv).astype(BF16)
        dgm = dgm_ref[...].astype(F32)
        dsv = (dgm * u).astype(BF16)
        mask = _causal_chunk_mask()
        du_cols, dvn_cols = [], []
        for hd in range(GM_HEADS):
            wm = jnp.where(mask, ws_ref[hd], 0.0).astype(BF16)
            cols = slice(hd * CHUNK, (hd + 1) * CHUNK)
            dw = jnp.zeros((CHUNK, CHUNK), F32)
            db = jnp.zeros((CHUNK, 1), F32)
            du_rows, dvn_rows = [], []
            for c in range(tm // CHUNK):
                rows = slice(c * CHUNK, (c + 1) * CHUNK)
                sv = _dot(wm, vn[rows, cols]) + bt_ref[:, hd:hd + 1]
                du_rows.append(dgm[rows, cols] * sv)
                dvn_rows.append(_dot_tn(wm, dsv[rows, cols]))
                dw = dw + _dot_nt(dsv[rows, cols], vn[rows, cols])
                db = db + jnp.sum(dsv[rows, cols].astype(F32), axis=1, keepdims=True)
            dws_ref[hd] += jnp.where(mask, dw, 0.0)
            dbt_ref[:, hd:hd + 1] += db
            du_cols.append(jnp.concatenate(du_rows, axis=0))
            dvn_cols.append(jnp.concatenate(dvn_rows, axis=0))
        du = jnp.concatenate(du_cols, axis=1)
        dvn = jnp.concatenate(dvn_cols, axis=1)
        dv, dgv_rows = _rms_bwd(dvn, v, r, gvv)
        dgv_ref[...] += jnp.sum(dgv_rows, axis=0, keepdims=True)
        dzg_ref[:, :GM_WIDTH] = (du * _gelu_grad(zu)).astype(BF16)
        dzg_ref[:, GM_WIDTH:] = (dv * _gelu_grad(zv)).astype(BF16)

    const = lambda nd: (lambda i: (0,) * nd)
    return pl.pallas_call(
        body, name="gmlp_bwd", grid=(t // tm,),
        in_specs=[_rows(tm, 2 * GM_WIDTH), _rows(tm, GM_WIDTH), _resident((1, GM_WIDTH)), _resident(ws.shape),
                  _resident(bt.shape)],
        out_specs=[_rows(tm, 2 * GM_WIDTH), pl.BlockSpec(ws.shape, const(3)), pl.BlockSpec(bt.shape, const(2)),
                   pl.BlockSpec((1, GM_WIDTH), const(2))],
        out_shape=[jax.ShapeDtypeStruct((t, 2 * GM_WIDTH), BF16), jax.ShapeDtypeStruct(ws.shape, F32),
                   jax.ShapeDtypeStruct(bt.shape, F32), jax.ShapeDtypeStruct((1, GM_WIDTH), F32)],
        compiler_params=_params(),
    )(zg, dmixed, gv, ws, bt)


def _att_masks():
    tb = ATT_BLOCK
    lane = lax.broadcasted_iota(jnp.int32, (1, LANES), 1)
    rj = lax.broadcasted_iota(jnp.int32, (2 * tb, 2 * tb), 0)
    cs = lax.broadcasted_iota(jnp.int32, (2 * tb, 2 * tb), 1)
    same_head = ((rj < tb) & (cs < tb)) | ((rj >= tb) & (cs >= tb))
    suffix = jnp.where(same_head & (rj >= cs), 1.0, 0.0).astype(BF16)
    prefix = jnp.where(same_head & (rj <= cs), 1.0, 0.0).astype(BF16)
    left = lax.broadcasted_iota(jnp.int32, (1, 2 * tb), 1) < tb
    tq = lax.broadcasted_iota(jnp.int32, (ATT_Q, 4 * tb), 0)
    ts = lax.broadcasted_iota(jnp.int32, (ATT_Q, 4 * tb), 1)
    key = jnp.where(ts < 2 * tb, ts & (tb - 1), (ts & (tb - 1)) + tb)
    return lane, suffix, prefix, left, key, tq


def _att_fill(k_ref, v_ref, kcat, vcat, n_blocks, lane):
    tb = ATT_BLOCK
    first = lane < SB_HEAD_DIM

    def fill(jb, carry):
        rows = pl.ds(pl.multiple_of(jb * tb, tb), tb)
        top = pl.ds(pl.multiple_of(jb * 2 * tb, tb), tb)
        bot = pl.ds(pl.multiple_of(jb * 2 * tb + tb, tb), tb)
        kb = k_ref[rows, :]
        vb = v_ref[rows, :]
        zero = jnp.zeros_like(kb)
        kcat[top, :] = jnp.where(first, kb, zero)
        kcat[bot, :] = jnp.where(first, zero, kb)
        vcat[top, :] = jnp.where(first, vb, zero)
        vcat[bot, :] = jnp.where(first, zero, vb)
        return carry

    lax.fori_loop(0, n_blocks, fill, 0)


def _block_sums(x, m):
    return _dot(x.astype(BF16), m)


def _softplus2(z2):
    return jnp.maximum(z2, 0.0) + jnp.log2(1.0 + jnp.exp2(-jnp.abs(z2)))


def _scaled_queries(q_ref, base2):
    scale = SB_HEAD_DIM ** -0.5
    return (q_ref[...].astype(F32) * (scale * math.log2(math.e) if base2 else scale)).astype(BF16)


def _att_specs(t):
    n_pairs = SB_WIDTH // LANES
    q_spec = pl.BlockSpec((ATT_Q, LANES), lambda p, i: (i, p))
    k_spec = pl.BlockSpec((t, LANES), lambda p, i: (0, n_pairs + p))
    v_spec = pl.BlockSpec((t, LANES), lambda p, i: (0, 2 * n_pairs + p))
    return n_pairs, q_spec, k_spec, v_spec


def _attn_fwd(qkv, exchange=None):
    t = qkv.shape[0]
    tb = ATT_BLOCK
    nkb = t // tb
    assert 2 * nkb <= LANES and t % ATT_Q == 0 and ATT_Q == 4 * tb
    n_pairs, q_spec, k_spec, v_spec = _att_specs(t)

    def body(q_ref, k_ref, v_ref, o_ref, ct_ref, kcat, vcat, acc, carry, z0, r0, z1, r1):
        i = pl.program_id(1)
        lane, suffix, _, left, key, tq = _att_masks()

        @pl.when(i == 0)
        def _():
            _att_fill(k_ref, v_ref, kcat, vcat, nkb, lane)

        q = _scaled_queries(q_ref, True)
        acc[...] = jnp.zeros_like(acc)
        carry[...] = jnp.zeros_like(carry)
        ct_ref[0] = jnp.zeros((ATT_Q, LANES), F32)

        def key_rows(m):
            return pl.ds(pl.multiple_of(m * 4 * tb, 4 * tb), 4 * tb)

        def scores(m, zb, rb, causal=None, rs=slice(None)):
            z = _dot_nt(q[rs], kcat[key_rows(m), :])
            zb[rs, :] = z
            sp = _softplus2(z)
            if causal is not None:
                sp = jnp.where(causal[rs], sp, 0.0)
            for g in (1, 0):
                cols = slice(g * 2 * tb, (g + 1) * 2 * tb)
                rb[rs, cols] = _block_sums(sp[:, cols], suffix)

        def weigh(m, zb, rb, causal=None, rs=slice(None)):
            probs = [None, None]
            for g in (1, 0):
                cols = slice(g * 2 * tb, (g + 1) * 2 * tb)
                j = 2 * m + g
                r = rb[rs, cols]
                c = carry[rs, :]
                ct_ref[0, rs, :] = jnp.where(lane == j, c[:, :tb], jnp.where(lane == nkb + j, c[:, tb:], ct_ref[0, rs, :]))
                a = jnp.exp2(zb[rs, cols] - (r + c))
                if causal is not None:
                    a = jnp.where(causal[rs, cols], a, 0.0)
                probs[g] = a.astype(BF16)
                carry[rs, :] = c + jnp.where(left, r[:, 0:1], r[:, tb:tb + 1])
            acc[rs, :] += _dot(jnp.concatenate(probs, axis=1), vcat[key_rows(m), :])

        sooner, later, late_rows = key < tq, key + 2 * tb < tq, slice(2 * tb, 4 * tb)
        scores(2 * i + 1, z1, r1, later, late_rows)
        scores(2 * i, z0, r0, sooner)
        weigh(2 * i + 1, z1, r1, later, late_rows)
        weigh(2 * i, z0, r0, sooner)

        @pl.when(i > 0)
        def _():
            scores(2 * i - 1, z1, r1)

            def loop(k, c):
                u = i - 1 - k
                scores(2 * u, z0, r0)
                weigh(2 * u + 1, z1, r1)
                scores(2 * u - 1, z1, r1)
                weigh(2 * u, z0, r0)
                return c

            lax.fori_loop(0, i - 1, loop, 0)
            scores(0, z0, r0)
            weigh(1, z1, r1)
            weigh(0, z0, r0)

        o_ref[...] = acc[...].astype(BF16)

    tile = pltpu.VMEM((ATT_Q, 4 * tb), F32)

    nq = t // ATT_Q
    return _call(
        body, (qkv, qkv, qkv), name="attn_fwd", grid=(n_pairs, nq),
        in_specs=[q_spec, k_spec, v_spec],
        out_specs=[pl.BlockSpec((ATT_Q, LANES), lambda p, i: (i, p)), pl.BlockSpec((1, ATT_Q, LANES), lambda p, i: (p, i, 0))],
        out_shape=[jax.ShapeDtypeStruct((t, SB_WIDTH), BF16), jax.ShapeDtypeStruct((n_pairs, t, LANES), F32)],
        scratch_shapes=[pltpu.VMEM((2 * t, LANES), BF16), pltpu.VMEM((2 * t, LANES), BF16),
                        pltpu.VMEM((ATT_Q, LANES), F32), pltpu.VMEM((ATT_Q, 2 * tb), F32), tile, tile, tile, tile],
        exchange=exchange, steps=(0, (n_pairs - 1) * nq - 1, n_pairs * nq - 1))


def _attn_bwd(qkv, dmixed, carries, exchange=None):
    t = qkv.shape[0]
    tb = ATT_BLOCK
    nkb = t // tb
    nq = t // ATT_Q
    scale = SB_HEAD_DIM ** -0.5
    n_pairs, q_spec, k_spec, v_spec = _att_specs(t)
    gm_blocks = GM_WIDTH // LANES

    def body(q_ref, k_ref, v_ref, do_ref, ct_ref, dq_ref, dk_ref, dv_ref, kcat, vcat, dkacc, dvacc, dqacc, carry,
             z0, r0, s0, a0, z1, r1, s1, a1):
        i = pl.program_id(1)
        lane, suffix, prefix, left, key, tq = _att_masks()
        first = lane < SB_HEAD_DIM

        @pl.when(i == 0)
        def _():
            _att_fill(k_ref, v_ref, kcat, vcat, nkb, lane)
            dkacc[...] = jnp.zeros_like(dkacc)
            dvacc[...] = jnp.zeros_like(dvacc)

        q2 = _scaled_queries(q_ref, True)
        q = _scaled_queries(q_ref, False)
        do = do_ref[...]
        dqacc[...] = jnp.zeros_like(dqacc)
        carry[...] = jnp.zeros_like(carry)

        def key_rows(m):
            return pl.ds(pl.multiple_of(m * 4 * tb, 4 * tb), 4 * tb)

        def front(m, bufs, causal=None, rs=slice(None)):
            zb, rb, sb, ab = bufs
            z = _dot_nt(q2[rs], kcat[key_rows(m), :])
            zb[rs, :] = z
            sp = _softplus2(z)
            sb[rs, :] = jnp.exp2(z - sp)
            if causal is not None:
                sp = jnp.where(causal[rs], sp, 0.0)
            for g in (0, 1):
                cols = slice(g * 2 * tb, (g + 1) * 2 * tb)
                rb[rs, cols] = _block_sums(sp[:, cols], suffix)
            ab[rs, :] = _dot_nt(do[rs], vcat[key_rows(m), :])

        def back(m, bufs, causal=None, rs=slice(None)):
            zb, rb, sb, ab = bufs
            dzs, probs = [None, None], [None, None]
            for g in (0, 1):
                cols = slice(g * 2 * tb, (g + 1) * 2 * tb)
                j = 2 * m + g
                ct = ct_ref[0, rs, :]
                ca = jnp.sum(jnp.where(lane == j, ct, 0.0), axis=1, keepdims=True)
                cb = jnp.sum(jnp.where(lane == nkb + j, ct, 0.0), axis=1, keepdims=True)
                a = jnp.exp2(zb[rs, cols] - (rb[rs, cols] + jnp.where(left, ca, cb)))
                if causal is not None:
                    a = jnp.where(causal[rs, cols], a, 0.0)
                de = ab[rs, cols] * a
                cl = _block_sums(de, prefix)
                pre = carry[rs, :]
                dz = de - sb[rs, cols] * (cl + pre)
                if causal is not None:
                    dz = jnp.where(causal[rs, cols], dz, 0.0)
                carry[rs, :] = pre + jnp.where(left, cl[:, tb - 1:tb], cl[:, 2 * tb - 1:2 * tb])
                dzs[g] = dz.astype(BF16)
                probs[g] = a.astype(BF16)
            dzb = jnp.concatenate(dzs, axis=1)
            dqacc[rs, :] += _dot(dzb, kcat[key_rows(m), :])
            dkc = _dot_tn(dzb, q[rs])
            dvc = _dot_tn(jnp.concatenate(probs, axis=1), do[rs])
            out_rows = pl.ds(pl.multiple_of(m * 2 * tb, 2 * tb), 2 * tb)
            pick = lambda x: jnp.concatenate([jnp.where(first, x[0:tb], x[tb:2 * tb]),
                                              jnp.where(first, x[2 * tb:3 * tb], x[3 * tb:4 * tb])], axis=0)
            dkacc[out_rows, :] += pick(dkc)
            dvacc[out_rows, :] += pick(dvc)

        b0, b1 = (z0, r0, s0, a0), (z1, r1, s1, a1)

        @pl.when(i > 0)
        def _():
            front(0, b0)

            def loop(u, c):
                front(2 * u + 1, b1)
                back(2 * u, b0)
                front(2 * u + 2, b0)
                back(2 * u + 1, b1)
                return c

            lax.fori_loop(0, i - 1, loop, 0)
            front(2 * i - 1, b1)
            back(2 * i - 2, b0)
            back(2 * i - 1, b1)

        sooner, later, late_rows = key < tq, key + 2 * tb < tq, slice(2 * tb, 4 * tb)
        front(2 * i, b0, sooner)
        front(2 * i + 1, b1, later, late_rows)
        back(2 * i, b0, sooner)
        back(2 * i + 1, b1, later, late_rows)

        dq_ref[...] = (dqacc[...] * scale).astype(BF16)

        @pl.when(i == nq - 1)
        def _():
            dk_ref[...] = dkacc[...].astype(BF16)
            dv_ref[...] = dvacc[...].astype(BF16)

    col = pl.BlockSpec((t, LANES), lambda p, i: (0, p))
    out = jax.ShapeDtypeStruct((t, SB_WIDTH), BF16)
    tile = pltpu.VMEM((ATT_Q, 4 * tb), F32)
    return _call(
        body, (qkv, qkv, qkv, dmixed, carries), name="attn_bwd", grid=(n_pairs, nq),
        in_specs=[q_spec, k_spec, v_spec, pl.BlockSpec((ATT_Q, LANES), lambda p, i: (i, gm_blocks + p)),
                  pl.BlockSpec((1, ATT_Q, LANES), lambda p, i: (p, i, 0))],
        out_specs=[pl.BlockSpec((ATT_Q, LANES), lambda p, i: (i, p)), col, col],
        out_shape=[out, out, out],
        scratch_shapes=[pltpu.VMEM((2 * t, LANES), BF16), pltpu.VMEM((2 * t, LANES), BF16),
                        pltpu.VMEM((t, LANES), F32), pltpu.VMEM((t, LANES), F32),
                        pltpu.VMEM((ATT_Q, LANES), F32), pltpu.VMEM((ATT_Q, 2 * tb), F32)] + [tile] * 8,
        exchange=exchange, steps=(0, n_pairs * nq - 1))


def _matmul_residual(res, a, w, name):
    t = a.shape[0]
    tm = _token_tile(t)

    def body(res_ref, a_ref, w_ref, o_ref):
        o_ref[...] = res_ref[...] + _dot(a_ref[...], w_ref[...])

    return pl.pallas_call(
        body, name=name, grid=(t // tm,),
        in_specs=[_rows(tm, res.shape[1]), _rows(tm, a.shape[1]), _resident(w.shape)],
        out_specs=_rows(tm, res.shape[1]),
        out_shape=jax.ShapeDtypeStruct(res.shape, F32),
        compiler_params=_params(),
    )(res, a, w)


def _matmul_nt_cast(dy, w, name):
    t = dy.shape[0]
    tm = _token_tile(t)

    def body(dy_ref, w_ref, o_ref, dyb_ref):
        dyb = dy_ref[...].astype(BF16)
        dyb_ref[...] = dyb
        o_ref[...] = _dot_nt(dyb, w_ref[...]).astype(BF16)

    return pl.pallas_call(
        body, name=name, grid=(t // tm,),
        in_specs=[_rows(tm, dy.shape[1]), _resident(w.shape)],
        out_specs=[_rows(tm, w.shape[0]), _rows(tm, dy.shape[1])],
        out_shape=[jax.ShapeDtypeStruct((t, w.shape[0]), BF16), jax.ShapeDtypeStruct(dy.shape, BF16)],
        compiler_params=_params(),
    )(dy, w)


def _norm_input_bwd(dres, dz, w, h, g, name):
    t = h.shape[0]
    tm = _token_tile(t)
    nb, _, width = w.shape

    def body(dres_ref, dz_ref, w_ref, h_ref, g_ref, dh_ref, n_ref, dg_ref):
        i = pl.program_id(0)
        hh = h_ref[...]
        gg = g_ref[...]
        r = _rstd(hh)
        n_ref[...] = (hh * r * gg).astype(BF16)
        dn = jnp.zeros((tm, D_MODEL), F32)
        for b in range(nb):
            dn = dn + _dot_nt(dz_ref[:, b * width:(b + 1) * width], w_ref[b])
        dh, dg_rows = _rms_bwd(dn, hh, r, gg)
        dh_ref[...] = dres_ref[...] + dh

        @pl.when(i == 0)
        def _():
            dg_ref[...] = jnp.zeros_like(dg_ref)

        dg_ref[...] += jnp.sum(dg_rows, axis=0, keepdims=True)

    return pl.pallas_call(
        body, name=name, grid=(t // tm,),
        in_specs=[_rows(tm, D_MODEL), _rows(tm, nb * width), _resident(w.shape), _rows(tm, D_MODEL),
                  _resident((1, D_MODEL))],
        out_specs=[_rows(tm, D_MODEL), _rows(tm, D_MODEL), pl.BlockSpec((1, D_MODEL), lambda i: (0, 0))],
        out_shape=[jax.ShapeDtypeStruct((t, D_MODEL), F32), jax.ShapeDtypeStruct((t, D_MODEL), BF16),
                   jax.ShapeDtypeStruct((1, D_MODEL), F32)],
        compiler_params=_params(),
    )(dres, dz, w, h, g)


def _head(h, p, target, gple, gfin, wg, wproj):
    t = h.shape[0]
    tm = _token_tile(t)
    pw = D_MODEL // N_CHIPS

    def body(h_ref, p_ref, tgt_ref, gple_ref, gfin_ref, wg_ref, wproj_ref,
             loss_ref, dgf_ref, dgple_ref, dh_ref, dgp_ref, dpp_ref, n_ref, pb_ref):
        i = pl.program_id(0)
        hh = h_ref[...]
        r_in = _rstd(hh)
        gp = gple_ref[...]
        n = (hh * r_in * gp).astype(BF16)
        n_ref[...] = n
        gate = jax.nn.sigmoid(_dot(n, wg_ref[...]))
        pb = p_ref[...].astype(BF16)
        pb_ref[...] = pb
        pp = jnp.concatenate([_dot(pb, wproj_ref[b]) for b in range(N_CHIPS)], axis=1)
        h4 = hh + gate * pp
        r = _rstd(h4)
        gf = gfin_ref[...]
        err = h4 * r * gf - tgt_ref[...]
        dy = err * (1.0 / D_MODEL)
        dh4, dgf_rows = _rms_bwd(dy, h4, r, gf)
        dgp = (dh4 * pp * gate * (1.0 - gate)).astype(BF16)
        dgp_ref[...] = dgp
        dpp_ref[...] = (dh4 * gate).astype(BF16)
        dh, dgple_rows = _rms_bwd(_dot_nt(dgp, wg_ref[...]), hh, r_in, gp)
        dh_ref[...] = dh4 + dh

        @pl.when(i == 0)
        def _():
            loss_ref[...] = jnp.zeros_like(loss_ref)
            dgf_ref[...] = jnp.zeros_like(dgf_ref)
            dgple_ref[...] = jnp.zeros_like(dgple_ref)

        loss_ref[...] += (0.5 / D_MODEL) * jnp.sum(err * err)
        dgf_ref[...] += jnp.sum(dgf_rows, axis=0, keepdims=True)
        dgple_ref[...] += jnp.sum(dgple_rows, axis=0, keepdims=True)

    bf = lambda w: jax.ShapeDtypeStruct((t, w), BF16)
    const = lambda i: (0, 0)
    return pl.pallas_call(
        body, name="head", grid=(t // tm,),
        in_specs=[_rows(tm, D_MODEL), _rows(tm, PLE_DIM), _rows(tm, D_MODEL), _resident((1, D_MODEL)),
                  _resident((1, D_MODEL)), _resident(wg.shape), _resident(wproj.shape)],
        out_specs=[pl.BlockSpec((1, LANES), const), pl.BlockSpec((1, D_MODEL), const), pl.BlockSpec((1, D_MODEL), const),
                   _rows(tm, D_MODEL), _rows(tm, D_MODEL), _rows(tm, D_MODEL), _rows(tm, D_MODEL), _rows(tm, PLE_DIM)],
        out_shape=[jax.ShapeDtypeStruct((1, LANES), F32), jax.ShapeDtypeStruct((1, D_MODEL), F32),
                   jax.ShapeDtypeStruct((1, D_MODEL), F32), jax.ShapeDtypeStruct((t, D_MODEL), F32), bf(D_MODEL),
                   bf(D_MODEL), bf(D_MODEL), bf(PLE_DIM)],
        compiler_params=_params(),
    )(h, p, target, gple, gfin, wg, wproj)


_BIG = ("ffn1_w_in", "ffn1_w_out", "w_mix_in", "w_mix_out", "ffn2_w_in", "ffn2_w_out", "ple_w_gate", "ple_w_proj")
_SMALL = ("ffn1_norm", "mix_norm", "gmlp_v_norm", "gmlp_w_s", "gmlp_b", "ffn2_norm", "ple_norm", "final_norm")
_ALL = ("ffn1_norm", "ffn1_w_in", "ffn1_w_out", "mix_norm", "w_mix_in", "gmlp_v_norm", "gmlp_w_s", "gmlp_b", "w_mix_out",
        "ffn2_norm", "ffn2_w_in", "ffn2_w_out", "ple_norm", "ple_w_gate", "ple_w_proj", "final_norm")
_ANY = pl.BlockSpec(memory_space=pl.ANY)
_MESH = pl.DeviceIdType.MESH


def _mesh_pos():
    return lax.axis_index("x"), lax.axis_index("y"), lax.axis_index("c")


def _other_chips(x, y):
    return [((x, 1 - y), 2 * x + 1 - y), ((1 - x, y), 2 * (1 - x) + y), ((1 - x, 1 - y), 2 * (1 - x) + 1 - y)]


def _remote(src, dst, send_sem, recv_sem, device):
    return pltpu.make_async_remote_copy(src_ref=src, dst_ref=dst, send_sem=send_sem, recv_sem=recv_sem,
                                        device_id=device, device_id_type=_MESH)


class _WeightGather:
    def __init__(self, shards):
        self.shapes = [s.shape for s in shards]
        self.operands = list(shards)
        self.out_shape = [jax.ShapeDtypeStruct((N_CHIPS, *s.shape), s.dtype) for s in shards]
        n = len(shards)
        self.per = 2 * (N_CHIPS - 1)
        self.scratch = [pltpu.SemaphoreType.DMA((self.per * n,)), pltpu.SemaphoreType.DMA((self.per * n,)),
                        pltpu.SemaphoreType.DMA((n,))]
        self.phases = [self.send, self.forward, self.finish]

    def _copies(self, ins, outs, sems):
        send_sems, recv_sems, local_sems = sems
        x, y, c = _mesh_pos()
        sibling = (x, y, 1 - c)
        mine = 2 * x + y
        local, first, landing, passed, arriving = [], [], [], [], []
        for w, shape in enumerate(self.shapes):
            hr = shape[0] // 2
            half = lambda blk, cc, w=w, hr=hr: outs[w].at[blk, pl.ds(cc * hr, hr), :]
            local.append(pltpu.make_async_copy(ins[w], outs[w].at[mine], local_sems.at[w]))
            for k, (chip, blk) in enumerate(_other_chips(x, y)):
                s = self.per * w + k
                first.append(_remote(ins[w].at[pl.ds(c * hr, hr), :], half(mine, c), send_sems.at[s], recv_sems.at[s],
                                     (*chip, c)))
                landing.append(_remote(half(blk, c), half(blk, c), send_sems.at[s], recv_sems.at[s], sibling))
                s = self.per * w + N_CHIPS - 1 + k
                passed.append(_remote(half(blk, c), half(blk, c), send_sems.at[s], recv_sems.at[s], sibling))
                arriving.append(_remote(half(blk, 1 - c), half(blk, 1 - c), send_sems.at[s], recv_sems.at[s], sibling))
        return local, first, landing, passed, arriving

    def send(self, ins, outs, sems):
        local, first, _, _, _ = self._copies(ins, outs, sems)
        for cp in local + first:
            cp.start()

    def forward(self, ins, outs, sems):
        _, _, landing, passed, _ = self._copies(ins, outs, sems)
        for landed, cp in zip(landing, passed):
            landed.wait_recv()
            cp.start()

    def finish(self, ins, outs, sems):
        local, first, _, passed, arriving = self._copies(ins, outs, sems)
        for cp in arriving:
            cp.wait_recv()
        for cp in first + passed:
            cp.wait_send()
        for cp in local:
            cp.wait()


class _ChipExchange:
    def __init__(self, sums):
        n = len(sums)
        self.n = n
        self.per = N_CHIPS - 1
        self.operands = list(sums)
        self.out_shape = [jax.ShapeDtypeStruct((self.per, *s.shape[1:]), s.dtype) for s in sums]
        self.scratch = [pltpu.SemaphoreType.DMA((self.per * n,)), pltpu.SemaphoreType.DMA((self.per * n,))]
        self.phases = [self.send, self.finish]

    def _copies(self, ins, outs, sems):
        send_sems, recv_sems = sems
        x, y, c = _mesh_pos()
        cps = []
        for w in range(self.n):
            for k, (chip, _) in enumerate(_other_chips(x, y)):
                s = self.per * w + k
                cps.append(_remote(ins[w].at[k + 1], outs[w].at[k], send_sems.at[s], recv_sems.at[s], (*chip, c)))
        return cps

    def send(self, ins, outs, sems):
        for cp in self._copies(ins, outs, sems):
            cp.start()

    def finish(self, ins, outs, sems):
        for cp in self._copies(ins, outs, sems):
            cp.wait()


def _run_exchange(ex, name):
    n_in, n_out = len(ex.operands), len(ex.out_shape)

    def body(*refs):
        ins, outs, sems = refs[:n_in], refs[n_in:n_in + n_out], refs[n_in + n_out:]
        for phase in ex.phases:
            phase(ins, outs, sems)

    return pl.pallas_call(body, name=name, in_specs=[_ANY] * n_in, out_specs=[_ANY] * n_out, out_shape=ex.out_shape,
                          scratch_shapes=ex.scratch)(*ex.operands)


def _call(body, args, *, name, grid, in_specs, out_specs, out_shape, scratch_shapes=(), exchange=None, steps=None):
    params = _params(len(grid))
    if exchange is None:
        out = pl.pallas_call(body, name=name, grid=grid, in_specs=in_specs, out_specs=out_specs, out_shape=out_shape,
                             scratch_shapes=list(scratch_shapes), compiler_params=params)(*args)
        return out, None
    n_in, n_out, n_scr = len(in_specs), len(out_specs), len(scratch_shapes)
    n_xin, n_xout = len(exchange.operands), len(exchange.out_shape)
    assert len(steps) == len(exchange.phases)

    def hosting(*refs):
        cuts = [n_in, n_xin, n_out, n_xout, n_scr]
        parts, at = [], 0
        for size in cuts:
            parts.append(refs[at:at + size])
            at += size
        ins, xins, outs, xouts, scr = parts
        sems = refs[at:]
        step = 0
        for axis, size in enumerate(grid):
            step = step * size + pl.program_id(axis)
        pl.when(step == steps[0])(lambda: exchange.phases[0](xins, xouts, sems))
        body(*ins, *outs, *scr)
        for at_step, phase in zip(steps[1:], exchange.phases[1:]):
            pl.when(step == at_step)(functools.partial(phase, xins, xouts, sems))

    out = pl.pallas_call(
        hosting, name=name, grid=grid,
        in_specs=list(in_specs) + [_ANY] * n_xin, out_specs=list(out_specs) + [_ANY] * n_xout,
        out_shape=list(out_shape) + list(exchange.out_shape),
        scratch_shapes=list(scratch_shapes) + list(exchange.scratch), compiler_params=params,
    )(*args, *exchange.operands)
    return out[:n_out], out[n_out:]


def _pair_exchange(grads, name):
    n = len(grads)

    def body(*refs):
        ins, outs = refs[:n], refs[n:2 * n]
        send_sems, recv_sems = refs[2 * n:]
        x, y, c = _mesh_pos()
        cps = []
        for w in range(n):
            hr = grads[w].shape[1] // 2
            cp = _remote(ins[w].at[:, pl.ds((1 - c) * hr, hr), :], outs[w], send_sems.at[w], recv_sems.at[w], (x, y, 1 - c))
            cp.start()
            cps.append(cp)
        for cp in cps:
            cp.wait()

    return pl.pallas_call(
        body, name=name,
        in_specs=[_ANY] * n, out_specs=[_ANY] * n,
        out_shape=[jax.ShapeDtypeStruct((g.shape[0], g.shape[1] // 2, g.shape[2]), g.dtype) for g in grads],
        scratch_shapes=[pltpu.SemaphoreType.DMA((n,)), pltpu.SemaphoreType.DMA((n,))],
    )(*grads)


SUM_STEPS = 2


def _pair_sums(gs, sibling, pos, name):
    n = len(gs)

    def body(pos_ref, *refs):
        for k in range(n):
            refs[2 * n + k][...] = (refs[k][...] + refs[n + k][...].astype(F32)).astype(BF16)

    g_specs, a_specs, o_specs, out_shape = [], [], [], []
    for g in gs:
        nb, r, c = g.shape
        tr = r // 2 // SUM_STEPS
        g_specs.append(pl.BlockSpec((1, tr, c), lambda k, i, pos: (k ^ pos[0], pos[1] * SUM_STEPS + i, 0)))
        a_specs.append(pl.BlockSpec((1, tr, c), lambda k, i, pos: (k ^ pos[0], i, 0)))
        o_specs.append(pl.BlockSpec((1, tr, c), lambda k, i, pos: (k, i, 0)))
        out_shape.append(jax.ShapeDtypeStruct((nb, r // 2, c), BF16))
    return pl.pallas_call(
        body, name=name,
        grid_spec=pltpu.PrefetchScalarGridSpec(num_scalar_prefetch=1, grid=(N_CHIPS, SUM_STEPS),
                                               in_specs=g_specs + a_specs, out_specs=o_specs),
        out_shape=out_shape, compiler_params=_params(2),
    )(pos, *gs, *sibling)


def _chip_sums(pairs, others, pos, name):
    n = len(pairs)

    def body(pos_ref, *refs):
        for k in range(n):
            s_ref, b_ref = refs[k], refs[n + k]
            refs[2 * n + k][...] = ((s_ref[0].astype(F32) + b_ref[0].astype(F32))
                                    + (b_ref[1].astype(F32) + b_ref[2].astype(F32)))

    s_specs, b_specs, o_specs, out_shape = [], [], [], []
    for s in pairs:
        _, hr, c = s.shape
        tr = hr // SUM_STEPS
        s_specs.append(pl.BlockSpec((1, tr, c), lambda i, pos: (0, i, 0)))
        b_specs.append(pl.BlockSpec((N_CHIPS - 1, tr, c), lambda i, pos: (0, i, 0)))
        o_specs.append(pl.BlockSpec((tr, c), lambda i, pos: (pos[1] * SUM_STEPS + i, 0)))
        out_shape.append(jax.ShapeDtypeStruct((2 * hr, c), F32))
    return pl.pallas_call(
        body, name=name,
        grid_spec=pltpu.PrefetchScalarGridSpec(num_scalar_prefetch=1, grid=(SUM_STEPS,),
                                               in_specs=s_specs + b_specs, out_specs=o_specs),
        out_shape=out_shape, compiler_params=_params(),
    )(pos, *pairs, *others)


def _pair_share(grads, name):
    n = len(grads)

    def body(*refs):
        outs = refs[n:2 * n]
        send_sems, recv_sems = refs[2 * n:]
        x, y, c = _mesh_pos()
        cps = []
        for w in range(n):
            hr = grads[w].shape[0] // 2
            rows = outs[w].at[pl.ds(c * hr, hr), :]
            cp = _remote(rows, rows, send_sems.at[w], recv_sems.at[w], (x, y, 1 - c))
            cp.start()
            cps.append(cp)
        for w, cp in enumerate(cps):
            cp.wait_send()
            hr = grads[w].shape[0] // 2
            other = outs[w].at[pl.ds((1 - c) * hr, hr), :]
            _remote(other, other, send_sems.at[w], recv_sems.at[w], (x, y, 1 - c)).wait_recv()

    return pl.pallas_call(
        body, name=name,
        in_specs=[_ANY] * n, out_specs=[_ANY] * n,
        out_shape=[jax.ShapeDtypeStruct(g.shape, g.dtype) for g in grads],
        input_output_aliases={w: w for w in range(n)},
        scratch_shapes=[pltpu.SemaphoreType.DMA((n,)), pltpu.SemaphoreType.DMA((n,))],
    )(*grads)


def _all_reduce_small(rows, mats):
    n_sems = 2 * N_CHIPS
    n_rows = -(-len(rows) // 8) * 8
    heights = [math.prod(a.shape[:-1]) for a in mats]
    n_tall = -(-sum(heights) // 8) * 8
    arrays = list(rows) + list(mats)

    def body(*refs):
        ins, outs = refs[:len(arrays)], refs[len(arrays):2 * len(arrays)]
        wide, tall, wide_pair, tall_pair, wide_slots, tall_slots, send_sems, recv_sems = refs[2 * len(arrays):]
        x, y, c = _mesh_pos()
        wide[...] = jnp.zeros_like(wide)
        tall[...] = jnp.zeros_like(tall)
        for k, a in enumerate(rows):
            wide[k:k + 1, 0:a.shape[1]] = ins[k][...]
        at = 0
        for k, h in enumerate(heights):
            tall[at:at + h, :] = ins[len(rows) + k][...].reshape(h, LANES)
            at += h
        wide_pair[c] = wide[...]
        tall_pair[c] = tall[...]
        cps = [_remote(wide, wide_pair.at[c], send_sems.at[0], recv_sems.at[0], (x, y, 1 - c)),
               _remote(tall, tall_pair.at[c], send_sems.at[1], recv_sems.at[1], (x, y, 1 - c))]
        for cp in cps:
            cp.start()
        for cp in cps:
            cp.wait()
        wide[...] = wide_pair[0] + wide_pair[1]
        tall[...] = tall_pair[0] + tall_pair[1]
        mine = 2 * x + y
        wide_slots[mine] = wide[...]
        tall_slots[mine] = tall[...]
        cps = []
        for k, (chip, _) in enumerate(_other_chips(x, y)):
            for j, (buf, slots) in enumerate(((wide, wide_slots), (tall, tall_slots))):
                s = 2 + 2 * k + j
                cps.append(_remote(buf, slots.at[mine], send_sems.at[s], recv_sems.at[s], (*chip, c)))
        for cp in cps:
            cp.start()
        for cp in cps:
            cp.wait()
        wide_sum, tall_sum = wide_slots[0], tall_slots[0]
        for d in range(1, N_CHIPS):
            wide_sum = wide_sum + wide_slots[d]
            tall_sum = tall_sum + tall_slots[d]
        for k, a in enumerate(rows):
            outs[k][...] = wide_sum[k:k + 1, 0:a.shape[1]]
        at = 0
        for k, h in enumerate(heights):
            outs[len(rows) + k][...] = tall_sum[at:at + h, :].reshape(mats[k].shape)
            at += h

    vmem = pl.BlockSpec(memory_space=pltpu.VMEM)
    return pl.pallas_call(
        body, name="all_reduce_small",
        in_specs=[vmem] * len(arrays), out_specs=[vmem] * len(arrays),
        out_shape=[jax.ShapeDtypeStruct(a.shape, F32) for a in arrays],
        scratch_shapes=[pltpu.VMEM((n_rows, D_MODEL), F32), pltpu.VMEM((n_tall, LANES), F32),
                        pltpu.VMEM((2, n_rows, D_MODEL), F32), pltpu.VMEM((2, n_tall, LANES), F32),
                        pltpu.VMEM((N_CHIPS, n_rows, D_MODEL), F32), pltpu.VMEM((N_CHIPS, n_tall, LANES), F32),
                        pltpu.SemaphoreType.DMA((n_sems,)), pltpu.SemaphoreType.DMA((n_sems,))],
    )(*arrays)


def _adamw_small(items):
    n = len(items)
    bias1 = 1.0 - ADAM_B1 ** ADAM_STEP
    bias2 = 1.0 - ADAM_B2 ** ADAM_STEP

    def body(*refs):
        ins, outs = refs[:4 * n], refs[4 * n:]
        for k in range(n):
            w_ref, g_ref, m_ref, v_ref = ins[4 * k:4 * k + 4]
            gg = g_ref[...]
            m2 = ADAM_B1 * m_ref[...] + (1.0 - ADAM_B1) * gg
            v2 = ADAM_B2 * v_ref[...] + (1.0 - ADAM_B2) * (gg * gg)
            outs[3 * k + 1][...] = m2
            outs[3 * k + 2][...] = v2
            outs[3 * k][...] = -ADAM_LR * ((m2 / bias1) / (jnp.sqrt(v2 / bias2) + ADAM_EPS) + ADAM_WD * w_ref[...])

    vmem = pl.BlockSpec(memory_space=pltpu.VMEM)
    out = pl.pallas_call(
        body, name="adamw_small", in_specs=[vmem] * (4 * n), out_specs=[vmem] * (3 * n),
        out_shape=[jax.ShapeDtypeStruct(w.shape, F32) for w, _, _, _ in items for _ in range(3)],
    )(*[a for item in items for a in item])
    return [tuple(out[3 * k:3 * k + 3]) for k in range(n)]


ADAMW_STEPS = 8


def _adamw(items, name, exchange=None):
    n = len(items)
    bias1 = 1.0 - ADAM_B1 ** ADAM_STEP
    bias2 = 1.0 - ADAM_B2 ** ADAM_STEP

    def body(*refs):
        ins, outs = refs[:4 * n], refs[4 * n:]
        for k in range(n):
            w_ref, g_ref, m_ref, v_ref = ins[4 * k:4 * k + 4]
            d_ref, mo_ref, vo_ref = outs[3 * k:3 * k + 3]
            gg = g_ref[...]
            m2 = ADAM_B1 * m_ref[...] + (1.0 - ADAM_B1) * gg
            v2 = ADAM_B2 * v_ref[...] + (1.0 - ADAM_B2) * (gg * gg)
            mo_ref[...] = m2
            vo_ref[...] = v2
            d_ref[...] = -ADAM_LR * ((m2 / bias1) / (jnp.sqrt(v2 / bias2) + ADAM_EPS) + ADAM_WD * w_ref[...])

    in_specs, out_specs, out_shape, args = [], [], [], []
    for w, g, m, v in items:
        r, c = w.shape
        steps = ADAMW_STEPS if r % (8 * ADAMW_STEPS) == 0 else 1
        assert steps == ADAMW_STEPS or n == 1
        spec = pl.BlockSpec((r // steps, c), lambda i: (i, 0))
        in_specs += [spec] * 4
        out_specs += [spec] * 3
        out_shape += [jax.ShapeDtypeStruct((r, c), F32)] * 3
        args += [w, g, m, v]
    out, got = _call(body, args, name=name, grid=(steps,), in_specs=in_specs, out_specs=out_specs, out_shape=out_shape,
                     exchange=exchange, steps=(0, steps - 1))
    return [tuple(out[3 * k:3 * k + 3]) for k in range(n)], got


def kernel(x, p, ffn1_norm, ffn1_w_in, ffn1_w_out, mix_norm, w_mix_in, gmlp_v_norm, gmlp_w_s, gmlp_b, w_mix_out, ffn2_norm, ffn2_w_in, ffn2_w_out, ple_norm, ple_w_gate, ple_w_proj, final_norm, loss_target, m_ffn1_norm, m_ffn1_w_in, m_ffn1_w_out, m_mix_norm, m_w_mix_in, m_gmlp_v_norm, m_gmlp_w_s, m_gmlp_b, m_w_mix_out, m_ffn2_norm, m_ffn2_w_in, m_ffn2_w_out, m_ple_norm, m_ple_w_gate, m_ple_w_proj, m_final_norm, v_ffn1_norm, v_ffn1_w_in, v_ffn1_w_out, v_mix_norm, v_w_mix_in, v_gmlp_v_norm, v_gmlp_w_s, v_gmlp_b, v_w_mix_out, v_ffn2_norm, v_ffn2_w_in, v_ffn2_w_out, v_ple_norm, v_ple_w_gate, v_ple_w_proj, v_final_norm):
    args = dict(locals())
    w = {n: args[n] for n in _ALL}
    m = {n: args["m_" + n] for n in _ALL}
    v = {n: args["v_" + n] for n in _ALL}
    xi, yi, ci = _mesh_pos()
    pos = jnp.stack([2 * xi + yi, ci]).astype(jnp.int32)
    shard = {n: w[n][0] for n in _BIG}
    cast = {n: shard[n].astype(BF16) for n in _BIG}
    small = {n: (w[n][0] if w[n].ndim > 2 else w[n].reshape(1, -1)) for n in _SMALL}
    bt = small["gmlp_b"].T
    g_small, pair, from_chips = {}, {}, {}

    def pair_reduce(partials, tag):
        names = list(partials)
        blocks = lambda a, n: a.reshape(N_CHIPS, *shard[n].shape)
        got = _pair_exchange([blocks(partials[n][1], n) for n in names], "grad_pair_exchange_" + tag)
        pair.update(zip(names, _pair_sums([blocks(partials[n][0], n) for n in names], got, pos, "pair_sums_" + tag)))
        return names

    w1in, w1out = _run_exchange(_WeightGather([cast["ffn1_w_in"], cast["ffn1_w_out"]]), "gather_ffn1")
    w1out = w1out.reshape(D_FF, D_MODEL)
    (h1, gu1), (wmix, wmo) = _ffn_fwd(x[0], small["ffn1_norm"], w1in, w1out, "ffn1_fwd",
                                      _WeightGather([cast["w_mix_in"], cast["w_mix_out"]]))
    wmo = wmo.reshape(D_MODEL, D_MODEL)
    zg, qkv = _mix_in_fwd(h1, small["mix_norm"], wmix)
    gm = _gmlp_fwd(zg, small["gmlp_v_norm"], small["gmlp_w_s"], bt)
    (att, carries), (w2in, w2out, wg, wproj) = _attn_fwd(
        qkv, _WeightGather([cast["ffn2_w_in"], cast["ffn2_w_out"], cast["ple_w_gate"], cast["ple_w_proj"]]))
    w2out = w2out.reshape(D_FF, D_MODEL)
    wg = wg.reshape(D_MODEL, D_MODEL)
    mixed = jnp.concatenate([gm, att], axis=1)
    h2 = _matmul_residual(h1, mixed, wmo, "mix_out_fwd")
    (h3, gu2), _ = _ffn_fwd(h2, small["ffn2_norm"], w2in, w2out, "ffn2_fwd")
    loss_part, g_small["final_norm"], g_small["ple_norm"], dh3, dgp, dpp, n4, pb = _head(
        h3, p[0, 0], loss_target[0], small["ple_norm"], small["final_norm"], wg, wproj)

    part = {"ple_w_gate": _wgrad_rows(n4, dgp, N_CHIPS, "wgrad_ple_gate"),
            "ple_w_proj": _wgrad_cols(pb, dpp, N_CHIPS, "wgrad_ple_proj")}
    (dh2, dgu2, n3, act2, dhh3, g_small["ffn2_norm"]), _ = _ffn_bwd(dh3, h2, small["ffn2_norm"], gu2, w2in, w2out,
                                                                   "ffn2_bwd")
    part["ffn2_w_in"] = _wgrad_cols(n3, dgu2, N_CHIPS, "wgrad_ffn2_in")
    part["ffn2_w_out"] = _wgrad_rows(act2, dhh3, 2, "wgrad_ffn2_out")
    dmixed, dh2b = _matmul_nt_cast(dh2, wmo, "mix_out_bwd")
    part["w_mix_out"] = _wgrad_rows(mixed, dh2b, 2, "wgrad_mix_out")
    group = pair_reduce(part, "late")
    dzg, g_small["gmlp_w_s"], dbt, g_small["gmlp_v_norm"] = _gmlp_bwd(zg, dmixed, small["gmlp_v_norm"],
                                                                      small["gmlp_w_s"], bt)
    g_small["gmlp_b"] = dbt.T
    (dq, dk, dv), got = _attn_bwd(qkv, dmixed, carries, _ChipExchange([pair[n] for n in group]))
    from_chips.update(zip(group, got))

    dzmix = jnp.concatenate([dzg, dq, dk, dv], axis=1)
    dh1, n2, g_small["mix_norm"] = _norm_input_bwd(dh2, dzmix, wmix, h1, small["mix_norm"], "mix_in_bwd")
    group = pair_reduce({"w_mix_in": _wgrad_cols(n2, dzmix, N_CHIPS, "wgrad_mix_in")}, "mix")
    (dx, dgu1, n1, act1, dhh1, g_small["ffn1_norm"]), _ = _ffn_bwd(dh1, x[0], small["ffn1_norm"], gu1, w1in, w1out,
                                                                   "ffn1_bwd")

    g_out, got = _wgrad_rows(act1, dhh1, 2, "wgrad_ffn1_out", _ChipExchange([pair[n] for n in group]))
    from_chips.update(zip(group, got))
    pair_reduce({"ffn1_w_out": g_out}, "out")
    g_in, got = _wgrad_cols(n1, dgu1, N_CHIPS, "wgrad_ffn1_in", _ChipExchange([pair["ffn1_w_out"]]))
    from_chips["ffn1_w_out"] = got[0]
    pair_reduce({"ffn1_w_in": g_in}, "in")

    def finish(names, tag, exchange=None):
        halves = _chip_sums([pair[n] for n in names], [from_chips[n] for n in names], pos, "chip_sums_" + tag)
        full = _pair_share(halves, "grad_pair_share_" + tag)
        out, got = _adamw([(shard[n], g, m[n][0], v[n][0]) for n, g in zip(names, full)], "adamw_" + tag, exchange)
        for n, g, (d2, m2, v2) in zip(names, full, out):
            grads[n], delta[n], new_m[n], new_v[n] = g[None], d2[None], m2[None], v2[None]
        return got

    grads, delta, new_m, new_v = {}, {}, {}, {}
    got = finish([n for n in _BIG if n != "ffn1_w_in"], "most", _ChipExchange([pair["ffn1_w_in"]]))
    from_chips["ffn1_w_in"] = got[0]
    finish(["ffn1_w_in"], "last")

    rows = [n for n in _SMALL if g_small[n].shape[0] == 1]
    mats = [n for n in _SMALL if n not in rows]
    summed = _all_reduce_small([g_small[n] for n in rows] + [loss_part], [g_small[n] for n in mats])
    loss = summed[len(rows)][0, 0]
    g_sum = dict(zip(rows + mats, summed[:len(rows)] + summed[len(rows) + 1:]))
    like = lambda a, n: a[n].reshape(small[n].shape)
    out = _adamw_small([(small[n], g_sum[n], like(m, n), like(v, n)) for n in _SMALL])
    for n, (d2, m2, v2) in zip(_SMALL, out):
        grads[n], delta[n], new_m[n], new_v[n] = (a.reshape(w[n].shape) for a in (g_sum[n], d2, m2, v2))

    return (loss, dx[None], *[grads[n] for n in _ALL], *[delta[n] for n in _ALL], *[new_m[n] for n in _ALL],
            *[new_v[n] for n in _ALL])
```

```python
import functools
import math

import jax
import jax.numpy as jnp
from jax import lax
from jax.experimental import pallas as pl
from jax.experimental.pallas import tpu as pltpu

F32, BF16 = jnp.float32, jnp.bfloat16

D_MODEL = 1024
D_FF = 2816
FF_BLOCK = 2 * D_FF // 4
PLE_DIM = 256
CHUNK = 128
GM_HEADS = 4
GM_WIDTH = 512
SB_HEAD_DIM = 64
SB_WIDTH = 512
MIX_IN_WIDTH = 2 * GM_WIDTH + 3 * SB_WIDTH
MIX_BLOCK = MIX_IN_WIDTH // 4
EPS = 1e-6
N_CHIPS = 4
LANES = 128
ATT_BLOCK = 128
ATT_Q = 512
VMEM_LIMIT = 56 * 1024 * 1024

ADAM_LR, ADAM_B1, ADAM_B2, ADAM_EPS, ADAM_WD, ADAM_STEP = 0.001, 0.9, 0.999, 1e-08, 0.01, 10


def _dot(a, b):
    return jnp.dot(a, b, preferred_element_type=F32)


def _dot_nt(a, b):
    return lax.dot_general(a, b, (((1,), (1,)), ((), ())), preferred_element_type=F32)


def _dot_tn(a, b):
    return lax.dot_general(a, b, (((0,), (0,)), ((), ())), preferred_element_type=F32)


def _resident(shape):
    nd = len(shape)
    return pl.BlockSpec(shape, lambda *_: (0,) * nd, pipeline_mode=pl.Buffered(1))


def _rows(tm, width):
    return pl.BlockSpec((tm, width), lambda i: (i, 0))


def _params(n_axes=1):
    return pltpu.CompilerParams(dimension_semantics=("arbitrary",) * n_axes, vmem_limit_bytes=VMEM_LIMIT)


def _rstd(h):
    return lax.rsqrt(jnp.mean(h * h, axis=-1, keepdims=True) + EPS)


def _rms_bwd(dy, h, r, g):
    dyg = dy * g
    dh = r * dyg - h * (r * r * r) * jnp.mean(dyg * h, axis=-1, keepdims=True)
    return dh, dy * h * r


def _gelu(x):
    return 0.5 * x * (1.0 + lax.erf(x * (2.0 ** -0.5)))


def _gelu_grad(x):
    return 0.5 * (1.0 + lax.erf(x * (2.0 ** -0.5))) + x * jnp.exp(-0.5 * x * x) * ((2.0 * jnp.pi) ** -0.5)


def _token_tile(t, wide=False):
    return min(512 if wide else 256, t)


def _ffn_fwd(h, g, win, wout, name, exchange=None):
    t = h.shape[0]
    tm = _token_tile(t)

    def body(h_ref, g_ref, win_ref, wout_ref, ho_ref, gu_ref):
        hh = h_ref[...]
        n = (hh * _rstd(hh) * g_ref[...]).astype(BF16)
        acc = jnp.zeros((tm, D_MODEL), F32)
        for jb in range(2):
            gate = _dot(n, win_ref[jb])
            up = _dot(n, win_ref[2 + jb])
            gu_ref[:, jb * FF_BLOCK:(jb + 1) * FF_BLOCK] = gate.astype(BF16)
            gu_ref[:, D_FF + jb * FF_BLOCK:D_FF + (jb + 1) * FF_BLOCK] = up.astype(BF16)
            act = (gate * jax.nn.sigmoid(gate) * up).astype(BF16)
            acc = acc + _dot(act, wout_ref[jb * FF_BLOCK:(jb + 1) * FF_BLOCK, :])
        ho_ref[...] = hh + 0.5 * acc

    n = t // tm
    return _call(
        body, (h, g, win, wout), name=name, grid=(n,),
        in_specs=[_rows(tm, D_MODEL), _resident((1, D_MODEL)), _resident(win.shape), _resident(wout.shape)],
        out_specs=[_rows(tm, D_MODEL), _rows(tm, 2 * D_FF)],
        out_shape=[jax.ShapeDtypeStruct((t, D_MODEL), F32), jax.ShapeDtypeStruct((t, 2 * D_FF), BF16)],
        exchange=exchange, steps=(0, (2 * n) // 3, n - 1))


def _ffn_bwd(dho, h, g, gu, win, wout, name, exchange=None):
    t = h.shape[0]
    tm = _token_tile(t)

    def body(dho_ref, h_ref, g_ref, gu_ref, win_ref, wout_ref, dh_ref, dgu_ref, n_ref, act_ref, dhh_ref, dg_ref):
        i = pl.program_id(0)
        hh = h_ref[...]
        gg = g_ref[...]
        r = _rstd(hh)
        n_ref[...] = (hh * r * gg).astype(BF16)
        dho = dho_ref[...]
        dhh = (0.5 * dho).astype(BF16)
        dhh_ref[...] = dhh
        dn = jnp.zeros((tm, D_MODEL), F32)
        for jb in range(2):
            cg = slice(jb * FF_BLOCK, (jb + 1) * FF_BLOCK)
            cu = slice(D_FF + jb * FF_BLOCK, D_FF + (jb + 1) * FF_BLOCK)
            dact = _dot_nt(dhh, wout_ref[cg, :])
            gate = gu_ref[:, cg].astype(F32)
            up = gu_ref[:, cu].astype(F32)
            sg = jax.nn.sigmoid(gate)
            silu = gate * sg
            act_ref[:, cg] = (silu * up).astype(BF16)
            dgate = (dact * up * (sg * (1.0 + gate * (1.0 - sg)))).astype(BF16)
            dup = (dact * silu).astype(BF16)
            dgu_ref[:, cg] = dgate
            dgu_ref[:, cu] = dup
            dn = dn + _dot_nt(dgate, win_ref[jb]) + _dot_nt(dup, win_ref[2 + jb])
        dh, dg_rows = _rms_bwd(dn, hh, r, gg)
        dh_ref[...] = dho + dh

        @pl.when(i == 0)
        def _():
            dg_ref[...] = jnp.zeros_like(dg_ref)

        dg_ref[...] += jnp.sum(dg_rows, axis=0, keepdims=True)

    n = t // tm
    return _call(
        body, (dho, h, g, gu, win, wout), name=name, grid=(n,),
        in_specs=[_rows(tm, D_MODEL), _rows(tm, D_MODEL), _resident((1, D_MODEL)), _rows(tm, 2 * D_FF),
                  _resident(win.shape), _resident(wout.shape)],
        out_specs=[_rows(tm, D_MODEL), _rows(tm, 2 * D_FF), _rows(tm, D_MODEL), _rows(tm, D_FF), _rows(tm, D_MODEL),
                   pl.BlockSpec((1, D_MODEL), lambda i: (0, 0))],
        out_shape=[jax.ShapeDtypeStruct((t, D_MODEL), F32), jax.ShapeDtypeStruct((t, 2 * D_FF), BF16),
                   jax.ShapeDtypeStruct((t, D_MODEL), BF16), jax.ShapeDtypeStruct((t, D_FF), BF16),
                   jax.ShapeDtypeStruct((t, D_MODEL), BF16), jax.ShapeDtypeStruct((1, D_MODEL), F32)],
        exchange=exchange, steps=(0, n - 1))


def _wgrad(a, b, out_shape, out_block, out_index, a_width, b_width, grid_ij, name, exchange=None):
    t = a.shape[0]
    tk = min(2048, t)
    nk = t // tk

    def body(a_ref, b_ref, o_ref, ob_ref):
        k = pl.program_id(2)
        prod = _dot_tn(a_ref[...], b_ref[...]).reshape(o_ref.shape)

        @pl.when(k == 0)
        def _():
            o_ref[...] = prod

        @pl.when(k > 0)
        def _():
            o_ref[...] += prod

        @pl.when(k == nk - 1)
        def _():
            ob_ref[...] = o_ref[...].astype(BF16)

    grid = (*grid_ij, nk)
    out_spec = pl.BlockSpec(out_block, lambda i, j, k: out_index(i, j))
    out, got = _call(
        body, (a, b), name=name, grid=grid,
        in_specs=[pl.BlockSpec((tk, a_width), lambda i, j, k: (k, i)), pl.BlockSpec((tk, b_width), lambda i, j, k: (k, j))],
        out_specs=[out_spec, out_spec],
        out_shape=[jax.ShapeDtypeStruct(out_shape, F32), jax.ShapeDtypeStruct(out_shape, BF16)],
        exchange=exchange, steps=(0, grid[0] * grid[1] * grid[2] - 1))
    return tuple(out) if exchange is None else (tuple(out), got)


def _wgrad_cols(a, b, n_blocks, name, exchange=None):
    ka, nb = a.shape[1], b.shape[1] // n_blocks
    return _wgrad(a, b, (n_blocks, ka, nb), (1, ka, nb), lambda i, j: (j, 0, 0), ka, nb, (1, n_blocks), name, exchange)


def _wgrad_rows(a, b, n_blocks, name, exchange=None):
    ka, nb = a.shape[1] // n_blocks, b.shape[1]
    return _wgrad(a, b, (a.shape[1], nb), (ka, nb), lambda i, j: (i, 0), ka, nb, (n_blocks, 1), name, exchange)


def _mix_in_fwd(h, g, wmix):
    t = h.shape[0]
    tm = _token_tile(t, wide=True)
    gw2 = 2 * GM_WIDTH

    def body(h_ref, g_ref, w_ref, zg_ref, qkv_ref):
        hh = h_ref[...]
        n = (hh * _rstd(hh) * g_ref[...]).astype(BF16)
        for b in range(N_CHIPS):
            z = _dot(n, w_ref[b])
            lo, hi = b * MIX_BLOCK, (b + 1) * MIX_BLOCK
            if hi <= gw2:
                zg_ref[:, lo:hi] = z
            elif lo >= gw2:
                qkv_ref[:, lo - gw2:hi - gw2] = z.astype(BF16)
            else:
                zg_ref[:, lo:gw2] = z[:, :gw2 - lo]
                qkv_ref[:, 0:hi - gw2] = z[:, gw2 - lo:].astype(BF16)

    return pl.pallas_call(
        body, name="mix_in_fwd", grid=(t // tm,),
        in_specs=[_rows(tm, D_MODEL), _resident((1, D_MODEL)), _resident(wmix.shape)],
        out_specs=[_rows(tm, gw2), _rows(tm, 3 * SB_WIDTH)],
        out_shape=[jax.ShapeDtypeStruct((t, gw2), F32), jax.ShapeDtypeStruct((t, 3 * SB_WIDTH), BF16)],
        compiler_params=_params(),
    )(h, g, wmix)


def _causal_chunk_mask():
    row = lax.broadcasted_iota(jnp.int32, (CHUNK, CHUNK), 0)
    col = lax.broadcasted_iota(jnp.int32, (CHUNK, CHUNK), 1)
    return row >= col


def _gmlp_tile(t):
    return min(512, t)


def _gmlp_fwd(zg, gv, ws, bt):
    t = zg.shape[0]
    tm = _gmlp_tile(t)

    def body(zg_ref, gv_ref, ws_ref, bt_ref, o_ref):
        u = _gelu(zg_ref[:, :GM_WIDTH])
        v = _gelu(zg_ref[:, GM_WIDTH:])
        vn = (v * _rstd(v) * gv_ref[...]).astype(BF16)
        mask = _causal_chunk_mask()
        for hd in range(GM_HEADS):
            wm = jnp.where(mask, ws_ref[hd], 0.0).astype(BF16)
            cols = slice(hd * CHUNK, (hd + 1) * CHUNK)
            for c in range(tm // CHUNK):
                rows = slice(c * CHUNK, (c + 1) * CHUNK)
                sv = _dot(wm, vn[rows, cols]) + bt_ref[:, hd:hd + 1]
                o_ref[rows, cols] = (u[rows, cols] * sv).astype(BF16)

    return pl.pallas_call(
        body, name="gmlp_fwd", grid=(t // tm,),
        in_specs=[_rows(tm, 2 * GM_WIDTH), _resident((1, GM_WIDTH)), _resident(ws.shape), _resident(bt.shape)],
        out_specs=_rows(tm, GM_WIDTH),
        out_shape=jax.ShapeDtypeStruct((t, GM_WIDTH), BF16),
        compiler_params=_params(),
    )(zg, gv, ws, bt)


def _gmlp_bwd(zg, dmixed, gv, ws, bt):
    t = zg.shape[0]
    tm = _gmlp_tile(t)

    def body(zg_ref, dgm_ref, gv_ref, ws_ref, bt_ref, dzg_ref, dws_ref, dbt_ref, dgv_ref):
        i = pl.program_id(0)

        @pl.when(i == 0)
        def _():
            dws_ref[...] = jnp.zeros_like(dws_ref)
            dbt_ref[...] = jnp.zeros_like(dbt_ref)
            dgv_ref[...] = jnp.zeros_like(dgv_ref)

        zu = zg_ref[:, :GM_WIDTH]
        zv = zg_ref[:, GM_WIDTH:]
        u = _gelu(zu)
        v = _gelu(zv)
        r = _rstd(v)
        gvv = gv_ref[...]
        vn = (v * r * gvv).astype(BF16)
        dgm = dgm_ref[...].astype(F32)
        dsv = (dgm * u).astype(BF16)
        mask = _causal_chunk_mask()
        du_cols, dvn_cols = [], []
        for hd in range(GM_HEADS):
            wm = jnp.where(mask, ws_ref[hd], 0.0).astype(BF16)
            cols = slice(hd * CHUNK, (hd + 1) * CHUNK)
            dw = jnp.zeros((CHUNK, CHUNK), F32)
            db = jnp.zeros((CHUNK, 1), F32)
            du_rows, dvn_rows = [], []
            for c in range(tm // CHUNK):
                rows = slice(c * CHUNK, (c + 1) * CHUNK)
                sv = _dot(wm, vn[rows, cols]) + bt_ref[:, hd:hd + 1]
                du_rows.append(dgm[rows, cols] * sv)
                dvn_rows.append(_dot_tn(wm, dsv[rows, cols]))
                dw = dw + _dot_nt(dsv[rows, cols], vn[rows, cols])
                db = db + jnp.sum(dsv[rows, cols].astype(F32), axis=1, keepdims=True)
            dws_ref[hd] += jnp.where(mask, dw, 0.0)
            dbt_ref[:, hd:hd + 1] += db
            du_cols.append(jnp.concatenate(du_rows, axis=0))
            dvn_cols.append(jnp.concatenate(dvn_rows, axis=0))
        du = jnp.concatenate(du_cols, axis=1)
        dvn = jnp.concatenate(dvn_cols, axis=1)
        dv, dgv_rows = _rms_bwd(dvn, v, r, gvv)
        dgv_ref[...] += jnp.sum(dgv_rows, axis=0, keepdims=True)
        dzg_ref[:, :GM_WIDTH] = (du * _gelu_grad(zu)).astype(BF16)
        dzg_ref[:, GM_WIDTH:] = (dv * _gelu_grad(zv)).astype(BF16)

    const = lambda nd: (lambda i: (0,) * nd)
    return pl.pallas_call(
        body, name="gmlp_bwd", grid=(t // tm,),
        in_specs=[_rows(tm, 2 * GM_WIDTH), _rows(tm, GM_WIDTH), _resident((1, GM_WIDTH)), _resident(ws.shape),
                  _resident(bt.shape)],
        out_specs=[_rows(tm, 2 * GM_WIDTH), pl.BlockSpec(ws.shape, const(3)), pl.BlockSpec(bt.shape, const(2)),
                   pl.BlockSpec((1, GM_WIDTH), const(2))],
        out_shape=[jax.ShapeDtypeStruct((t, 2 * GM_WIDTH), BF16), jax.ShapeDtypeStruct(ws.shape, F32),
                   jax.ShapeDtypeStruct(bt.shape, F32), jax.ShapeDtypeStruct((1, GM_WIDTH), F32)],
        compiler_params=_params(),
    )(zg, dmixed, gv, ws, bt)


def _att_masks():
    tb = ATT_BLOCK
    lane = lax.broadcasted_iota(jnp.int32, (1, LANES), 1)
    rj = lax.broadcasted_iota(jnp.int32, (2 * tb, 2 * tb), 0)
    cs = lax.broadcasted_iota(jnp.int32, (2 * tb, 2 * tb), 1)
    same_head = ((rj < tb) & (cs < tb)) | ((rj >= tb) & (cs >= tb))
    suffix = jnp.where(same_head & (rj >= cs), 1.0, 0.0).astype(BF16)
    prefix = jnp.where(same_head & (rj <= cs), 1.0, 0.0).astype(BF16)
    left = lax.broadcasted_iota(jnp.int32, (1, 2 * tb), 1) < tb
    tq = lax.broadcasted_iota(jnp.int32, (ATT_Q, 4 * tb), 0)
    ts = lax.broadcasted_iota(jnp.int32, (ATT_Q, 4 * tb), 1)
    key = jnp.where(ts < 2 * tb, ts & (tb - 1), (ts & (tb - 1)) + tb)
    return lane, suffix, prefix, left, key, tq


def _att_fill(k_ref, v_ref, kcat, vcat, n_blocks, lane):
    tb = ATT_BLOCK
    first = lane < SB_HEAD_DIM

    def fill(jb, carry):
        rows = pl.ds(pl.multiple_of(jb * tb, tb), tb)
        top = pl.ds(pl.multiple_of(jb * 2 * tb, tb), tb)
        bot = pl.ds(pl.multiple_of(jb * 2 * tb + tb, tb), tb)
        kb = k_ref[rows, :]
        vb = v_ref[rows, :]
        zero = jnp.zeros_like(kb)
        kcat[top, :] = jnp.where(first, kb, zero)
        kcat[bot, :] = jnp.where(first, zero, kb)
        vcat[top, :] = jnp.where(first, vb, zero)
        vcat[bot, :] = jnp.where(first, zero, vb)
        return carry

    lax.fori_loop(0, n_blocks, fill, 0)


def _block_sums(x, m):
    return _dot(x.astype(BF16), m)


def _softplus2(z2):
    return jnp.maximum(z2, 0.0) + jnp.log2(1.0 + jnp.exp2(-jnp.abs(z2)))


def _scaled_queries(q_ref, base2):
    scale = SB_HEAD_DIM ** -0.5
    return (q_ref[...].astype(F32) * (scale * math.log2(math.e) if base2 else scale)).astype(BF16)


def _att_specs(t):
    n_pairs = SB_WIDTH // LANES
    q_spec = pl.BlockSpec((ATT_Q, LANES), lambda p, i: (i, p))
    k_spec = pl.BlockSpec((t, LANES), lambda p, i: (0, n_pairs + p))
    v_spec = pl.BlockSpec((t, LANES), lambda p, i: (0, 2 * n_pairs + p))
    return n_pairs, q_spec, k_spec, v_spec


def _attn_fwd(qkv, exchange=None):
    t = qkv.shape[0]
    tb = ATT_BLOCK
    nkb = t // tb
    assert 2 * nkb <= LANES and t % ATT_Q == 0 and ATT_Q == 4 * tb
    n_pairs, q_spec, k_spec, v_spec = _att_specs(t)

    def body(q_ref, k_ref, v_ref, o_ref, ct_ref, kcat, vcat, acc, carry, z0, r0, z1, r1):
        i = pl.program_id(1)
        lane, suffix, _, left, key, tq = _att_masks()

        @pl.when(i == 0)
        def _():
            _att_fill(k_ref, v_ref, kcat, vcat, nkb, lane)

        q = _scaled_queries(q_ref, True)
        acc[...] = jnp.zeros_like(acc)
        carry[...] = jnp.zeros_like(carry)
        ct_ref[0] = jnp.zeros((ATT_Q, LANES), F32)

        def key_rows(m):
            return pl.ds(pl.multiple_of(m * 4 * tb, 4 * tb), 4 * tb)

        def scores(m, zb, rb, causal=None, rs=slice(None)):
            z = _dot_nt(q[rs], kcat[key_rows(m), :])
            zb[rs, :] = z
            sp = _softplus2(z)
            if causal is not None:
                sp = jnp.where(causal[rs], sp, 0.0)
            for g in (1, 0):
                cols = slice(g * 2 * tb, (g + 1) * 2 * tb)
                rb[rs, cols] = _block_sums(sp[:, cols], suffix)

        def weigh(m, zb, rb, causal=None, rs=slice(None)):
            probs = [None, None]
            for g in (1, 0):
                cols = slice(g * 2 * tb, (g + 1) * 2 * tb)
                j = 2 * m + g
                r = rb[rs, cols]
                c = carry[rs, :]
                ct_ref[0, rs, :] = jnp.where(lane == j, c[:, :tb], jnp.where(lane == nkb + j, c[:, tb:], ct_ref[0, rs, :]))
                a = jnp.exp2(zb[rs, cols] - (r + c))
                if causal is not None:
                    a = jnp.where(causal[rs, cols], a, 0.0)
                probs[g] = a.astype(BF16)
                carry[rs, :] = c + jnp.where(left, r[:, 0:1], r[:, tb:tb + 1])
            acc[rs, :] += _dot(jnp.concatenate(probs, axis=1), vcat[key_rows(m), :])

        sooner, later, late_rows = key < tq, key + 2 * tb < tq, slice(2 * tb, 4 * tb)
        scores(2 * i + 1, z1, r1, later, late_rows)
        scores(2 * i, z0, r0, sooner)
        weigh(2 * i + 1, z1, r1, later, late_rows)
        weigh(2 * i, z0, r0, sooner)

        @pl.when(i > 0)
        def _():
            scores(2 * i - 1, z1, r1)

            def loop(k, c):
                u = i - 1 - k
                scores(2 * u, z0, r0)
                weigh(2 * u + 1, z1, r1)
                scores(2 * u - 1, z1, r1)
                weigh(2 * u, z0, r0)
                return c

            lax.fori_loop(0, i - 1, loop, 0)
            scores(0, z0, r0)
            weigh(1, z1, r1)
            weigh(0, z0, r0)

        o_ref[...] = acc[...].astype(BF16)

    tile = pltpu.VMEM((ATT_Q, 4 * tb), F32)

    nq = t // ATT_Q
    return _call(
        body, (qkv, qkv, qkv), name="attn_fwd", grid=(n_pairs, nq),
        in_specs=[q_spec, k_spec, v_spec],
        out_specs=[pl.BlockSpec((ATT_Q, LANES), lambda p, i: (i, p)), pl.BlockSpec((1, ATT_Q, LANES), lambda p, i: (p, i, 0))],
        out_shape=[jax.ShapeDtypeStruct((t, SB_WIDTH), BF16), jax.ShapeDtypeStruct((n_pairs, t, LANES), F32)],
        scratch_shapes=[pltpu.VMEM((2 * t, LANES), BF16), pltpu.VMEM((2 * t, LANES), BF16),
                        pltpu.VMEM((ATT_Q, LANES), F32), pltpu.VMEM((ATT_Q, 2 * tb), F32), tile, tile, tile, tile],
        exchange=exchange, steps=(0, (n_pairs - 1) * nq - 1, n_pairs * nq - 1))


def _attn_bwd(qkv, dmixed, carries, exchange=None):
    t = qkv.shape[0]
    tb = ATT_BLOCK
    nkb = t // tb
    nq = t // ATT_Q
    scale = SB_HEAD_DIM ** -0.5
    n_pairs, q_spec, k_spec, v_spec = _att_specs(t)
    gm_blocks = GM_WIDTH // LANES

    def body(q_ref, k_ref, v_ref, do_ref, ct_ref, dq_ref, dk_ref, dv_ref, kcat, vcat, dkacc, dvacc, dqacc, carry,
             z0, r0, s0, a0, z1, r1, s1, a1):
        i = pl.program_id(1)
        lane, suffix, prefix, left, key, tq = _att_masks()
        first = lane < SB_HEAD_DIM

        @pl.when(i == 0)
        def _():
            _att_fill(k_ref, v_ref, kcat, vcat, nkb, lane)
            dkacc[...] = jnp.zeros_like(dkacc)
            dvacc[...] = jnp.zeros_like(dvacc)

        q2 = _scaled_queries(q_ref, True)
        q = _scaled_queries(q_ref, False)
        do = do_ref[...]
        dqacc[...] = jnp.zeros_like(dqacc)
        carry[...] = jnp.zeros_like(carry)

        def key_rows(m):
            return pl.ds(pl.multiple_of(m * 4 * tb, 4 * tb), 4 * tb)

        def front(m, bufs, causal=None, rs=slice(None)):
            zb, rb, sb, ab = bufs
            z = _dot_nt(q2[rs], kcat[key_rows(m), :])
            zb[rs, :] = z
            sp = _softplus2(z)
            sb[rs, :] = jnp.exp2(z - sp)
            if causal is not None:
                sp = jnp.where(causal[rs], sp, 0.0)
            for g in (0, 1):
                cols = slice(g * 2 * tb, (g + 1) * 2 * tb)
                rb[rs, cols] = _block_sums(sp[:, cols], suffix)
            ab[rs, :] = _dot_nt(do[rs], vcat[key_rows(m), :])

        def back(m, bufs, causal=None, rs=slice(None)):
            zb, rb, sb, ab = bufs
            dzs, probs = [None, None], [None, None]
            for g in (0, 1):
                cols = slice(g * 2 * tb, (g + 1) * 2 * tb)
                j = 2 * m + g
                ct = ct_ref[0, rs, :]
                ca = jnp.sum(jnp.where(lane == j, ct, 0.0), axis=1, keepdims=True)
                cb = jnp.sum(jnp.where(lane == nkb + j, ct, 0.0), axis=1, keepdims=True)
                a = jnp.exp2(zb[rs, cols] - (rb[rs, cols] + jnp.where(left, ca, cb)))
                if causal is not None:
                    a = jnp.where(causal[rs, cols], a, 0.0)
                de = ab[rs, cols] * a
                cl = _block_sums(de, prefix)
                pre = carry[rs, :]
                dz = de - sb[rs, cols] * (cl + pre)
                if causal is not None:
                    dz = jnp.where(causal[rs, cols], dz, 0.0)
                carry[rs, :] = pre + jnp.where(left, cl[:, tb - 1:tb], cl[:, 2 * tb - 1:2 * tb])
                dzs[g] = dz.astype(BF16)
                probs[g] = a.astype(BF16)
            dzb = jnp.concatenate(dzs, axis=1)
            dqacc[rs, :] += _dot(dzb, kcat[key_rows(m), :])
            dkc = _dot_tn(dzb, q[rs])
            dvc = _dot_tn(jnp.concatenate(probs, axis=1), do[rs])
            out_rows = pl.ds(pl.multiple_of(m * 2 * tb, 2 * tb), 2 * tb)
            pick = lambda x: jnp.concatenate([jnp.where(first, x[0:tb], x[tb:2 * tb]),
                                              jnp.where(first, x[2 * tb:3 * tb], x[3 * tb:4 * tb])], axis=0)
            dkacc[out_rows, :] += pick(dkc)
            dvacc[out_rows, :] += pick(dvc)

        b0, b1 = (z0, r0, s0, a0), (z1, r1, s1, a1)

        @pl.when(i > 0)
        def _():
            front(0, b0)

            def loop(u, c):
                front(2 * u + 1, b1)
                back(2 * u, b0)
                front(2 * u + 2, b0)
                back(2 * u + 1, b1)
                return c

            lax.fori_loop(0, i - 1, loop, 0)
            front(2 * i - 1, b1)
            back(2 * i - 2, b0)
            back(2 * i - 1, b1)

        sooner, later, late_rows = key < tq, key + 2 * tb < tq, slice(2 * tb, 4 * tb)
        front(2 * i, b0, sooner)
        front(2 * i + 1, b1, later, late_rows)
        back(2 * i, b0, sooner)
        back(2 * i + 1, b1, later, late_rows)

        dq_ref[...] = (dqacc[...] * scale).astype(BF16)

        @pl.when(i == nq - 1)
        def _():
            dk_ref[...] = dkacc[...].astype(BF16)
            dv_ref[...] = dvacc[...].astype(BF16)

    col = pl.BlockSpec((t, LANES), lambda p, i: (0, p))
    out = jax.ShapeDtypeStruct((t, SB_WIDTH), BF16)
    tile = pltpu.VMEM((ATT_Q, 4 * tb), F32)
    return _call(
        body, (qkv, qkv, qkv, dmixed, carries), name="attn_bwd", grid=(n_pairs, nq),
        in_specs=[q_spec, k_spec, v_spec, pl.BlockSpec((ATT_Q, LANES), lambda p, i: (i, gm_blocks + p)),
                  pl.BlockSpec((1, ATT_Q, LANES), lambda p, i: (p, i, 0))],
        out_specs=[pl.BlockSpec((ATT_Q, LANES), lambda p, i: (i, p)), col, col],
        out_shape=[out, out, out],
        scratch_shapes=[pltpu.VMEM((2 * t, LANES), BF16), pltpu.VMEM((2 * t, LANES), BF16),
                        pltpu.VMEM((t, LANES), F32), pltpu.VMEM((t, LANES), F32),
                        pltpu.VMEM((ATT_Q, LANES), F32), pltpu.VMEM((ATT_Q, 2 * tb), F32)] + [tile] * 8,
        exchange=exchange, steps=(0, n_pairs * nq - 1))


def _matmul_residual(res, a, w, name):
    t = a.shape[0]
    tm = _token_tile(t, wide=True)

    def body(res_ref, a_ref, w_ref, o_ref):
        o_ref[...] = res_ref[...] + _dot(a_ref[...], w_ref[...])

    return pl.pallas_call(
        body, name=name, grid=(t // tm,),
        in_specs=[_rows(tm, res.shape[1]), _rows(tm, a.shape[1]), _resident(w.shape)],
        out_specs=_rows(tm, res.shape[1]),
        out_shape=jax.ShapeDtypeStruct(res.shape, F32),
        compiler_params=_params(),
    )(res, a, w)


def _matmul_nt_cast(dy, w, name):
    t = dy.shape[0]
    tm = _token_tile(t, wide=True)

    def body(dy_ref, w_ref, o_ref, dyb_ref):
        dyb = dy_ref[...].astype(BF16)
        dyb_ref[...] = dyb
        o_ref[...] = _dot_nt(dyb, w_ref[...]).astype(BF16)

    return pl.pallas_call(
        body, name=name, grid=(t // tm,),
        in_specs=[_rows(tm, dy.shape[1]), _resident(w.shape)],
        out_specs=[_rows(tm, w.shape[0]), _rows(tm, dy.shape[1])],
        out_shape=[jax.ShapeDtypeStruct((t, w.shape[0]), BF16), jax.ShapeDtypeStruct(dy.shape, BF16)],
        compiler_params=_params(),
    )(dy, w)


def _norm_input_bwd(dres, dz, w, h, g, name):
    t = h.shape[0]
    tm = _token_tile(t, wide=True)
    nb, _, width = w.shape

    def body(dres_ref, dz_ref, w_ref, h_ref, g_ref, dh_ref, n_ref, dg_ref):
        i = pl.program_id(0)
        hh = h_ref[...]
        gg = g_ref[...]
        r = _rstd(hh)
        n_ref[...] = (hh * r * gg).astype(BF16)
        dn = jnp.zeros((tm, D_MODEL), F32)
        for b in range(nb):
            dn = dn + _dot_nt(dz_ref[:, b * width:(b + 1) * width], w_ref[b])
        dh, dg_rows = _rms_bwd(dn, hh, r, gg)
        dh_ref[...] = dres_ref[...] + dh

        @pl.when(i == 0)
        def _():
            dg_ref[...] = jnp.zeros_like(dg_ref)

        dg_ref[...] += jnp.sum(dg_rows, axis=0, keepdims=True)

    return pl.pallas_call(
        body, name=name, grid=(t // tm,),
        in_specs=[_rows(tm, D_MODEL), _rows(tm, nb * width), _resident(w.shape), _rows(tm, D_MODEL),
                  _resident((1, D_MODEL))],
        out_specs=[_rows(tm, D_MODEL), _rows(tm, D_MODEL), pl.BlockSpec((1, D_MODEL), lambda i: (0, 0))],
        out_shape=[jax.ShapeDtypeStruct((t, D_MODEL), F32), jax.ShapeDtypeStruct((t, D_MODEL), BF16),
                   jax.ShapeDtypeStruct((1, D_MODEL), F32)],
        compiler_params=_params(),
    )(dres, dz, w, h, g)


def _head(h, p, target, gple, gfin, wg, wproj):
    t = h.shape[0]
    tm = _token_tile(t)
    pw = D_MODEL // N_CHIPS

    def body(h_ref, p_ref, tgt_ref, gple_ref, gfin_ref, wg_ref, wproj_ref,
             loss_ref, dgf_ref, dgple_ref, dh_ref, dgp_ref, dpp_ref, n_ref, pb_ref):
        i = pl.program_id(0)
        hh = h_ref[...]
        r_in = _rstd(hh)
        gp = gple_ref[...]
        n = (hh * r_in * gp).astype(BF16)
        n_ref[...] = n
        gate = jax.nn.sigmoid(_dot(n, wg_ref[...]))
        pb = p_ref[...].astype(BF16)
        pb_ref[...] = pb
        pp = jnp.concatenate([_dot(pb, wproj_ref[b]) for b in range(N_CHIPS)], axis=1)
        h4 = hh + gate * pp
        r = _rstd(h4)
        gf = gfin_ref[...]
        err = h4 * r * gf - tgt_ref[...]
        dy = err * (1.0 / D_MODEL)
        dh4, dgf_rows = _rms_bwd(dy, h4, r, gf)
        dgp = (dh4 * pp * gate * (1.0 - gate)).astype(BF16)
        dgp_ref[...] = dgp
        dpp_ref[...] = (dh4 * gate).astype(BF16)
        dh, dgple_rows = _rms_bwd(_dot_nt(dgp, wg_ref[...]), hh, r_in, gp)
        dh_ref[...] = dh4 + dh

        @pl.when(i == 0)
        def _():
            loss_ref[...] = jnp.zeros_like(loss_ref)
            dgf_ref[...] = jnp.zeros_like(dgf_ref)
            dgple_ref[...] = jnp.zeros_like(dgple_ref)

        loss_ref[...] += (0.5 / D_MODEL) * jnp.sum(err * err)
        dgf_ref[...] += jnp.sum(dgf_rows, axis=0, keepdims=True)
        dgple_ref[...] += jnp.sum(dgple_rows, axis=0, keepdims=True)

    bf = lambda w: jax.ShapeDtypeStruct((t, w), BF16)
    const = lambda i: (0, 0)
    return pl.pallas_call(
        body, name="head", grid=(t // tm,),
        in_specs=[_rows(tm, D_MODEL), _rows(tm, PLE_DIM), _rows(tm, D_MODEL), _resident((1, D_MODEL)),
                  _resident((1, D_MODEL)), _resident(wg.shape), _resident(wproj.shape)],
        out_specs=[pl.BlockSpec((1, LANES), const), pl.BlockSpec((1, D_MODEL), const), pl.BlockSpec((1, D_MODEL), const),
                   _rows(tm, D_MODEL), _rows(tm, D_MODEL), _rows(tm, D_MODEL), _rows(tm, D_MODEL), _rows(tm, PLE_DIM)],
        out_shape=[jax.ShapeDtypeStruct((1, LANES), F32), jax.ShapeDtypeStruct((1, D_MODEL), F32),
                   jax.ShapeDtypeStruct((1, D_MODEL), F32), jax.ShapeDtypeStruct((t, D_MODEL), F32), bf(D_MODEL),
                   bf(D_MODEL), bf(D_MODEL), bf(PLE_DIM)],
        compiler_params=_params(),
    )(h, p, target, gple, gfin, wg, wproj)


_BIG = ("ffn1_w_in", "ffn1_w_out", "w_mix_in", "w_mix_out", "ffn2_w_in", "ffn2_w_out", "ple_w_gate", "ple_w_proj")
_SMALL = ("ffn1_norm", "mix_norm", "gmlp_v_norm", "gmlp_w_s", "gmlp_b", "ffn2_norm", "ple_norm", "final_norm")
_ALL = ("ffn1_norm", "ffn1_w_in", "ffn1_w_out", "mix_norm", "w_mix_in", "gmlp_v_norm", "gmlp_w_s", "gmlp_b", "w_mix_out",
        "ffn2_norm", "ffn2_w_in", "ffn2_w_out", "ple_norm", "ple_w_gate", "ple_w_proj", "final_norm")
_ANY = pl.BlockSpec(memory_space=pl.ANY)
_MESH = pl.DeviceIdType.MESH


def _mesh_pos():
    return lax.axis_index("x"), lax.axis_index("y"), lax.axis_index("c")


def _other_chips(x, y):
    return [((x, 1 - y), 2 * x + 1 - y), ((1 - x, y), 2 * (1 - x) + y), ((1 - x, 1 - y), 2 * (1 - x) + 1 - y)]


def _remote(src, dst, send_sem, recv_sem, device):
    return pltpu.make_async_remote_copy(src_ref=src, dst_ref=dst, send_sem=send_sem, recv_sem=recv_sem,
                                        device_id=device, device_id_type=_MESH)


class _WeightGather:
    def __init__(self, shards):
        self.shapes = [s.shape for s in shards]
        self.operands = list(shards)
        self.out_shape = [jax.ShapeDtypeStruct((N_CHIPS, *s.shape), s.dtype) for s in shards]
        n = len(shards)
        self.per = 2 * (N_CHIPS - 1)
        self.scratch = [pltpu.SemaphoreType.DMA((self.per * n,)), pltpu.SemaphoreType.DMA((self.per * n,)),
                        pltpu.SemaphoreType.DMA((n,))]
        self.phases = [self.send, self.forward, self.finish]

    def _copies(self, ins, outs, sems):
        send_sems, recv_sems, local_sems = sems
        x, y, c = _mesh_pos()
        sibling = (x, y, 1 - c)
        mine = 2 * x + y
        local, first, landing, passed, arriving = [], [], [], [], []
        for w, shape in enumerate(self.shapes):
            hr = shape[0] // 2
            half = lambda blk, cc, w=w, hr=hr: outs[w].at[blk, pl.ds(cc * hr, hr), :]
            local.append(pltpu.make_async_copy(ins[w], outs[w].at[mine], local_sems.at[w]))
            for k, (chip, blk) in enumerate(_other_chips(x, y)):
                s = self.per * w + k
                first.append(_remote(ins[w].at[pl.ds(c * hr, hr), :], half(mine, c), send_sems.at[s], recv_sems.at[s],
                                     (*chip, c)))
                landing.append(_remote(half(blk, c), half(blk, c), send_sems.at[s], recv_sems.at[s], sibling))
                s = self.per * w + N_CHIPS - 1 + k
                passed.append(_remote(half(blk, c), half(blk, c), send_sems.at[s], recv_sems.at[s], sibling))
                arriving.append(_remote(half(blk, 1 - c), half(blk, 1 - c), send_sems.at[s], recv_sems.at[s], sibling))
        return local, first, landing, passed, arriving

    def send(self, ins, outs, sems):
        local, first, _, _, _ = self._copies(ins, outs, sems)
        for cp in local + first:
            cp.start()

    def forward(self, ins, outs, sems):
        _, _, landing, passed, _ = self._copies(ins, outs, sems)
        for landed, cp in zip(landing, passed):
            landed.wait_recv()
            cp.start()

    def finish(self, ins, outs, sems):
        local, first, _, passed, arriving = self._copies(ins, outs, sems)
        for cp in arriving:
            cp.wait_recv()
        for cp in first + passed:
            cp.wait_send()
        for cp in local:
            cp.wait()


class _ChipExchange:
    def __init__(self, sums):
        n = len(sums)
        self.n = n
        self.per = N_CHIPS - 1
        self.operands = list(sums)
        self.out_shape = [jax.ShapeDtypeStruct((self.per, *s.shape[1:]), s.dtype) for s in sums]
        self.scratch = [pltpu.SemaphoreType.DMA((self.per * n,)), pltpu.SemaphoreType.DMA((self.per * n,))]
        self.phases = [self.send, self.finish]

    def _copies(self, ins, outs, sems):
        send_sems, recv_sems = sems
        x, y, c = _mesh_pos()
        cps = []
        for w in range(self.n):
            for k, (chip, _) in enumerate(_other_chips(x, y)):
                s = self.per * w + k
                cps.append(_remote(ins[w].at[k + 1], outs[w].at[k], send_sems.at[s], recv_sems.at[s], (*chip, c)))
        return cps

    def send(self, ins, outs, sems):
        for cp in self._copies(ins, outs, sems):
            cp.start()

    def finish(self, ins, outs, sems):
        for cp in self._copies(ins, outs, sems):
            cp.wait()


def _run_exchange(ex, name):
    n_in, n_out = len(ex.operands), len(ex.out_shape)

    def body(*refs):
        ins, outs, sems = refs[:n_in], refs[n_in:n_in + n_out], refs[n_in + n_out:]
        for phase in ex.phases:
            phase(ins, outs, sems)

    return pl.pallas_call(body, name=name, in_specs=[_ANY] * n_in, out_specs=[_ANY] * n_out, out_shape=ex.out_shape,
                          scratch_shapes=ex.scratch)(*ex.operands)


def _call(body, args, *, name, grid, in_specs, out_specs, out_shape, scratch_shapes=(), exchange=None, steps=None):
    params = _params(len(grid))
    if exchange is None:
        out = pl.pallas_call(body, name=name, grid=grid, in_specs=in_specs, out_specs=out_specs, out_shape=out_shape,
                             scratch_shapes=list(scratch_shapes), compiler_params=params)(*args)
        return out, None
    n_in, n_out, n_scr = len(in_specs), len(out_specs), len(scratch_shapes)
    n_xin, n_xout = len(exchange.operands), len(exchange.out_shape)
    assert len(steps) == len(exchange.phases)

    def hosting(*refs):
        cuts = [n_in, n_xin, n_out, n_xout, n_scr]
        parts, at = [], 0
        for size in cuts:
            parts.append(refs[at:at + size])
            at += size
        ins, xins, outs, xouts, scr = parts
        sems = refs[at:]
        step = 0
        for axis, size in enumerate(grid):
            step = step * size + pl.program_id(axis)
        pl.when(step == steps[0])(lambda: exchange.phases[0](xins, xouts, sems))
        body(*ins, *outs, *scr)
        for at_step, phase in zip(steps[1:], exchange.phases[1:]):
            pl.when(step == at_step)(functools.partial(phase, xins, xouts, sems))

    out = pl.pallas_call(
        hosting, name=name, grid=grid,
        in_specs=list(in_specs) + [_ANY] * n_xin, out_specs=list(out_specs) + [_ANY] * n_xout,
        out_shape=list(out_shape) + list(exchange.out_shape),
        scratch_shapes=list(scratch_shapes) + list(exchange.scratch), compiler_params=params,
    )(*args, *exchange.operands)
    return out[:n_out], out[n_out:]


def _pair_exchange(grads, name):
    n = len(grads)

    def body(*refs):
        ins, outs = refs[:n], refs[n:2 * n]
        send_sems, recv_sems = refs[2 * n:]
        x, y, c = _mesh_pos()
        cps = []
        for w in range(n):
            hr = grads[w].shape[1] // 2
            cp = _remote(ins[w].at[:, pl.ds((1 - c) * hr, hr), :], outs[w], send_sems.at[w], recv_sems.at[w], (x, y, 1 - c))
            cp.start()
            cps.append(cp)
        for cp in cps:
            cp.wait()

    return pl.pallas_call(
        body, name=name,
        in_specs=[_ANY] * n, out_specs=[_ANY] * n,
        out_shape=[jax.ShapeDtypeStruct((g.shape[0], g.shape[1] // 2, g.shape[2]), g.dtype) for g in grads],
        scratch_shapes=[pltpu.SemaphoreType.DMA((n,)), pltpu.SemaphoreType.DMA((n,))],
    )(*grads)


SUM_STEPS = 2


def _pair_sums(gs, sibling, pos, name):
    n = len(gs)

    def body(pos_ref, *refs):
        for k in range(n):
            refs[2 * n + k][...] = (refs[k][...] + refs[n + k][...].astype(F32)).astype(BF16)

    g_specs, a_specs, o_specs, out_shape = [], [], [], []
    for g in gs:
        nb, r, c = g.shape
        tr = r // 2 // SUM_STEPS
        g_specs.append(pl.BlockSpec((1, tr, c), lambda k, i, pos: (k ^ pos[0], pos[1] * SUM_STEPS + i, 0)))
        a_specs.append(pl.BlockSpec((1, tr, c), lambda k, i, pos: (k ^ pos[0], i, 0)))
        o_specs.append(pl.BlockSpec((1, tr, c), lambda k, i, pos: (k, i, 0)))
        out_shape.append(jax.ShapeDtypeStruct((nb, r // 2, c), BF16))
    return pl.pallas_call(
        body, name=name,
        grid_spec=pltpu.PrefetchScalarGridSpec(num_scalar_prefetch=1, grid=(N_CHIPS, SUM_STEPS),
                                               in_specs=g_specs + a_specs, out_specs=o_specs),
        out_shape=out_shape, compiler_params=_params(2),
    )(pos, *gs, *sibling)


def _chip_sums(pairs, others, pos, name):
    n = len(pairs)

    def body(pos_ref, *refs):
        for k in range(n):
            s_ref, b_ref = refs[k], refs[n + k]
            refs[2 * n + k][...] = ((s_ref[0].astype(F32) + b_ref[0].astype(F32))
                                    + (b_ref[1].astype(F32) + b_ref[2].astype(F32)))

    s_specs, b_specs, o_specs, out_shape = [], [], [], []
    for s in pairs:
        _, hr, c = s.shape
        tr = hr // SUM_STEPS
        s_specs.append(pl.BlockSpec((1, tr, c), lambda i, pos: (0, i, 0)))
        b_specs.append(pl.BlockSpec((N_CHIPS - 1, tr, c), lambda i, pos: (0, i, 0)))
        o_specs.append(pl.BlockSpec((tr, c), lambda i, pos: (pos[1] * SUM_STEPS + i, 0)))
        out_shape.append(jax.ShapeDtypeStruct((2 * hr, c), F32))
    return pl.pallas_call(
        body, name=name,
        grid_spec=pltpu.PrefetchScalarGridSpec(num_scalar_prefetch=1, grid=(SUM_STEPS,),
                                               in_specs=s_specs + b_specs, out_specs=o_specs),
        out_shape=out_shape, compiler_params=_params(),
    )(pos, *pairs, *others)


def _pair_share(grads, name):
    n = len(grads)

    def body(*refs):
        outs = refs[n:2 * n]
        send_sems, recv_sems = refs[2 * n:]
        x, y, c = _mesh_pos()
        cps = []
        for w in range(n):
            hr = grads[w].shape[0] // 2
            rows = outs[w].at[pl.ds(c * hr, hr), :]
            cp = _remote(rows, rows, send_sems.at[w], recv_sems.at[w], (x, y, 1 - c))
            cp.start()
            cps.append(cp)
        for w, cp in enumerate(cps):
            cp.wait_send()
            hr = grads[w].shape[0] // 2
            other = outs[w].at[pl.ds((1 - c) * hr, hr), :]
            _remote(other, other, send_sems.at[w], recv_sems.at[w], (x, y, 1 - c)).wait_recv()

    return pl.pallas_call(
        body, name=name,
        in_specs=[_ANY] * n, out_specs=[_ANY] * n,
        out_shape=[jax.ShapeDtypeStruct(g.shape, g.dtype) for g in grads],
        input_output_aliases={w: w for w in range(n)},
        scratch_shapes=[pltpu.SemaphoreType.DMA((n,)), pltpu.SemaphoreType.DMA((n,))],
    )(*grads)


def _all_reduce_small(rows, mats):
    n_sems = 2 * N_CHIPS
    n_rows = -(-len(rows) // 8) * 8
    heights = [math.prod(a.shape[:-1]) for a in mats]
    n_tall = -(-sum(heights) // 8) * 8
    arrays = list(rows) + list(mats)

    def body(*refs):
        ins, outs = refs[:len(arrays)], refs[len(arrays):2 * len(arrays)]
        wide, tall, wide_pair, tall_pair, wide_slots, tall_slots, send_sems, recv_sems = refs[2 * len(arrays):]
        x, y, c = _mesh_pos()
        wide[...] = jnp.zeros_like(wide)
        tall[...] = jnp.zeros_like(tall)
        for k, a in enumerate(rows):
            wide[k:k + 1, 0:a.shape[1]] = ins[k][...]
        at = 0
        for k, h in enumerate(heights):
            tall[at:at + h, :] = ins[len(rows) + k][...].reshape(h, LANES)
            at += h
        wide_pair[c] = wide[...]
        tall_pair[c] = tall[...]
        cps = [_remote(wide, wide_pair.at[c], send_sems.at[0], recv_sems.at[0], (x, y, 1 - c)),
               _remote(tall, tall_pair.at[c], send_sems.at[1], recv_sems.at[1], (x, y, 1 - c))]
        for cp in cps:
            cp.start()
        for cp in cps:
            cp.wait()
        wide[...] = wide_pair[0] + wide_pair[1]
        tall[...] = tall_pair[0] + tall_pair[1]
        mine = 2 * x + y
        wide_slots[mine] = wide[...]
        tall_slots[mine] = tall[...]
        cps = []
        for k, (chip, _) in enumerate(_other_chips(x, y)):
            for j, (buf, slots) in enumerate(((wide, wide_slots), (tall, tall_slots))):
                s = 2 + 2 * k + j
                cps.append(_remote(buf, slots.at[mine], send_sems.at[s], recv_sems.at[s], (*chip, c)))
        for cp in cps:
            cp.start()
        for cp in cps:
            cp.wait()
        wide_sum, tall_sum = wide_slots[0], tall_slots[0]
        for d in range(1, N_CHIPS):
            wide_sum = wide_sum + wide_slots[d]
            tall_sum = tall_sum + tall_slots[d]
        for k, a in enumerate(rows):
            outs[k][...] = wide_sum[k:k + 1, 0:a.shape[1]]
        at = 0
        for k, h in enumerate(heights):
            outs[len(rows) + k][...] = tall_sum[at:at + h, :].reshape(mats[k].shape)
            at += h

    vmem = pl.BlockSpec(memory_space=pltpu.VMEM)
    return pl.pallas_call(
        body, name="all_reduce_small",
        in_specs=[vmem] * len(arrays), out_specs=[vmem] * len(arrays),
        out_shape=[jax.ShapeDtypeStruct(a.shape, F32) for a in arrays],
        scratch_shapes=[pltpu.VMEM((n_rows, D_MODEL), F32), pltpu.VMEM((n_tall, LANES), F32),
                        pltpu.VMEM((2, n_rows, D_MODEL), F32), pltpu.VMEM((2, n_tall, LANES), F32),
                        pltpu.VMEM((N_CHIPS, n_rows, D_MODEL), F32), pltpu.VMEM((N_CHIPS, n_tall, LANES), F32),
                        pltpu.SemaphoreType.DMA((n_sems,)), pltpu.SemaphoreType.DMA((n_sems,))],
    )(*arrays)


def _adamw_small(items):
    n = len(items)
    bias1 = 1.0 - ADAM_B1 ** ADAM_STEP
    bias2 = 1.0 - ADAM_B2 ** ADAM_STEP

    def body(*refs):
        ins, outs = refs[:4 * n], refs[4 * n:]
        for k in range(n):
            w_ref, g_ref, m_ref, v_ref = ins[4 * k:4 * k + 4]
            gg = g_ref[...]
            m2 = ADAM_B1 * m_ref[...] + (1.0 - ADAM_B1) * gg
            v2 = ADAM_B2 * v_ref[...] + (1.0 - ADAM_B2) * (gg * gg)
            outs[3 * k + 1][...] = m2
            outs[3 * k + 2][...] = v2
            outs[3 * k][...] = -ADAM_LR * ((m2 / bias1) / (jnp.sqrt(v2 / bias2) + ADAM_EPS) + ADAM_WD * w_ref[...])

    vmem = pl.BlockSpec(memory_space=pltpu.VMEM)
    out = pl.pallas_call(
        body, name="adamw_small", in_specs=[vmem] * (4 * n), out_specs=[vmem] * (3 * n),
        out_shape=[jax.ShapeDtypeStruct(w.shape, F32) for w, _, _, _ in items for _ in range(3)],
    )(*[a for item in items for a in item])
    return [tuple(out[3 * k:3 * k + 3]) for k in range(n)]


ADAMW_STEPS = 8


def _adamw(items, name, exchange=None):
    n = len(items)
    bias1 = 1.0 - ADAM_B1 ** ADAM_STEP
    bias2 = 1.0 - ADAM_B2 ** ADAM_STEP

    def body(*refs):
        ins, outs = refs[:4 * n], refs[4 * n:]
        for k in range(n):
            w_ref, g_ref, m_ref, v_ref = ins[4 * k:4 * k + 4]
            d_ref, mo_ref, vo_ref = outs[3 * k:3 * k + 3]
            gg = g_ref[...]
            m2 = ADAM_B1 * m_ref[...] + (1.0 - ADAM_B1) * gg
            v2 = ADAM_B2 * v_ref[...] + (1.0 - ADAM_B2) * (gg * gg)
            mo_ref[...] = m2
            vo_ref[...] = v2
            d_ref[...] = -ADAM_LR * ((m2 / bias1) / (jnp.sqrt(v2 / bias2) + ADAM_EPS) + ADAM_WD * w_ref[...])

    in_specs, out_specs, out_shape, args = [], [], [], []
    for w, g, m, v in items:
        r, c = w.shape
        steps = ADAMW_STEPS if r % (8 * ADAMW_STEPS) == 0 else 1
        assert steps == ADAMW_STEPS or n == 1
        spec = pl.BlockSpec((r // steps, c), lambda i: (i, 0))
        in_specs += [spec] * 4
        out_specs += [spec] * 3
        out_shape += [jax.ShapeDtypeStruct((r, c), F32)] * 3
        args += [w, g, m, v]
    out, got = _call(body, args, name=name, grid=(steps,), in_specs=in_specs, out_specs=out_specs, out_shape=out_shape,
                     exchange=exchange, steps=(0, steps - 1))
    return [tuple(out[3 * k:3 * k + 3]) for k in range(n)], got


def kernel(x, p, ffn1_norm, ffn1_w_in, ffn1_w_out, mix_norm, w_mix_in, gmlp_v_norm, gmlp_w_s, gmlp_b, w_mix_out, ffn2_norm, ffn2_w_in, ffn2_w_out, ple_norm, ple_w_gate, ple_w_proj, final_norm, loss_target, m_ffn1_norm, m_ffn1_w_in, m_ffn1_w_out, m_mix_norm, m_w_mix_in, m_gmlp_v_norm, m_gmlp_w_s, m_gmlp_b, m_w_mix_out, m_ffn2_norm, m_ffn2_w_in, m_ffn2_w_out, m_ple_norm, m_ple_w_gate, m_ple_w_proj, m_final_norm, v_ffn1_norm, v_ffn1_w_in, v_ffn1_w_out, v_mix_norm, v_w_mix_in, v_gmlp_v_norm, v_gmlp_w_s, v_gmlp_b, v_w_mix_out, v_ffn2_norm, v_ffn2_w_in, v_ffn2_w_out, v_ple_norm, v_ple_w_gate, v_ple_w_proj, v_final_norm):
    args = dict(locals())
    w = {n: args[n] for n in _ALL}
    m = {n: args["m_" + n] for n in _ALL}
    v = {n: args["v_" + n] for n in _ALL}
    xi, yi, ci = _mesh_pos()
    pos = jnp.stack([2 * xi + yi, ci]).astype(jnp.int32)
    shard = {n: w[n][0] for n in _BIG}
    cast = {n: shard[n].astype(BF16) for n in _BIG}
    small = {n: (w[n][0] if w[n].ndim > 2 else w[n].reshape(1, -1)) for n in _SMALL}
    bt = small["gmlp_b"].T
    g_small, pair, from_chips = {}, {}, {}

    def pair_reduce(partials, tag):
        names = list(partials)
        blocks = lambda a, n: a.reshape(N_CHIPS, *shard[n].shape)
        got = _pair_exchange([blocks(partials[n][1], n) for n in names], "grad_pair_exchange_" + tag)
        pair.update(zip(names, _pair_sums([blocks(partials[n][0], n) for n in names], got, pos, "pair_sums_" + tag)))
        return names

    w1in, w1out = _run_exchange(_WeightGather([cast["ffn1_w_in"], cast["ffn1_w_out"]]), "gather_ffn1")
    w1out = w1out.reshape(D_FF, D_MODEL)
    (h1, gu1), (wmix, wmo) = _ffn_fwd(x[0], small["ffn1_norm"], w1in, w1out, "ffn1_fwd",
                                      _WeightGather([cast["w_mix_in"], cast["w_mix_out"]]))
    wmo = wmo.reshape(D_MODEL, D_MODEL)
    zg, qkv = _mix_in_fwd(h1, small["mix_norm"], wmix)
    gm = _gmlp_fwd(zg, small["gmlp_v_norm"], small["gmlp_w_s"], bt)
    (att, carries), (w2in, w2out, wg, wproj) = _attn_fwd(
        qkv, _WeightGather([cast["ffn2_w_in"], cast["ffn2_w_out"], cast["ple_w_gate"], cast["ple_w_proj"]]))
    w2out = w2out.reshape(D_FF, D_MODEL)
    wg = wg.reshape(D_MODEL, D_MODEL)
    mixed = jnp.concatenate([gm, att], axis=1)
    h2 = _matmul_residual(h1, mixed, wmo, "mix_out_fwd")
    (h3, gu2), _ = _ffn_fwd(h2, small["ffn2_norm"], w2in, w2out, "ffn2_fwd")
    loss_part, g_small["final_norm"], g_small["ple_norm"], dh3, dgp, dpp, n4, pb = _head(
        h3, p[0, 0], loss_target[0], small["ple_norm"], small["final_norm"], wg, wproj)

    part = {"ple_w_gate": _wgrad_rows(n4, dgp, N_CHIPS, "wgrad_ple_gate"),
            "ple_w_proj": _wgrad_cols(pb, dpp, N_CHIPS, "wgrad_ple_proj")}
    (dh2, dgu2, n3, act2, dhh3, g_small["ffn2_norm"]), _ = _ffn_bwd(dh3, h2, small["ffn2_norm"], gu2, w2in, w2out,
                                                                   "ffn2_bwd")
    part["ffn2_w_in"] = _wgrad_cols(n3, dgu2, N_CHIPS, "wgrad_ffn2_in")
    part["ffn2_w_out"] = _wgrad_rows(act2, dhh3, 2, "wgrad_ffn2_out")
    dmixed, dh2b = _matmul_nt_cast(dh2, wmo, "mix_out_bwd")
    part["w_mix_out"] = _wgrad_rows(mixed, dh2b, 2, "wgrad_mix_out")
    group = pair_reduce(part, "late")
    dzg, g_small["gmlp_w_s"], dbt, g_small["gmlp_v_norm"] = _gmlp_bwd(zg, dmixed, small["gmlp_v_norm"],
                                                                      small["gmlp_w_s"], bt)
    g_small["gmlp_b"] = dbt.T
    (dq, dk, dv), got = _attn_bwd(qkv, dmixed, carries, _ChipExchange([pair[n] for n in group]))
    from_chips.update(zip(group, got))

    dzmix = jnp.concatenate([dzg, dq, dk, dv], axis=1)
    dh1, n2, g_small["mix_norm"] = _norm_input_bwd(dh2, dzmix, wmix, h1, small["mix_norm"], "mix_in_bwd")
    group = pair_reduce({"w_mix_in": _wgrad_cols(n2, dzmix, N_CHIPS, "wgrad_mix_in")}, "mix")
    (dx, dgu1, n1, act1, dhh1, g_small["ffn1_norm"]), _ = _ffn_bwd(dh1, x[0], small["ffn1_norm"], gu1, w1in, w1out,
                                                                   "ffn1_bwd")

    g_out, got = _wgrad_rows(act1, dhh1, 2, "wgrad_ffn1_out", _ChipExchange([pair[n] for n in group]))
    from_chips.update(zip(group, got))
    pair_reduce({"ffn1_w_out": g_out}, "out")
    g_in, got = _wgrad_cols(n1, dgu1, N_CHIPS, "wgrad_ffn1_in", _ChipExchange([pair["ffn1_w_out"]]))
    from_chips["ffn1_w_out"] = got[0]
    pair_reduce({"ffn1_w_in": g_in}, "in")

    def finish(names, tag, exchange=None):
        halves = _chip_sums([pair[n] for n in names], [from_chips[n] for n in names], pos, "chip_sums_" + tag)
        full = _pair_share(halves, "grad_pair_share_" + tag)
        out, got = _adamw([(shard[n], g, m[n][0], v[n][0]) for n, g in zip(names, full)], "adamw_" + tag, exchange)
        for n, g, (d2, m2, v2) in zip(names, full, out):
            grads[n], delta[n], new_m[n], new_v[n] = g[None], d2[None], m2[None], v2[None]
        return got

    grads, delta, new_m, new_v = {}, {}, {}, {}
    got = finish([n for n in _BIG if n != "ffn1_w_in"], "most", _ChipExchange([pair["ffn1_w_in"]]))
    from_chips["ffn1_w_in"] = got[0]
    finish(["ffn1_w_in"], "last")

    rows = [n for n in _SMALL if g_small[n].shape[0] == 1]
    mats = [n for n in _SMALL if n not in rows]
    summed = _all_reduce_small([g_small[n] for n in rows] + [loss_part], [g_small[n] for n in mats])
    loss = summed[len(rows)][0, 0]
    g_sum = dict(zip(rows + mats, summed[:len(rows)] + summed[len(rows) + 1:]))
    like = lambda a, n: a[n].reshape(small[n].shape)
    out = _adamw_small([(small[n], g_sum[n], like(m, n), like(v, n)) for n in _SMALL])
    for n, (d2, m2, v2) in zip(_SMALL, out):
        grads[n], delta[n], new_m[n], new_v[n] = (a.reshape(w[n].shape) for a in (g_sum[n], d2, m2, v2))

    return (loss, dx[None], *[grads[n] for n in _ALL], *[delta[n] for n in _ALL], *[new_m[n] for n in _ALL],
            *[new_v[n] for n in _ALL])
```

```python
import functools
import math

import jax
import jax.numpy as jnp
from jax import lax
from jax.experimental import pallas as pl
from jax.experimental.pallas import tpu as pltpu

F32, BF16 = jnp.float32, jnp.bfloat16

D_MODEL = 1024
D_FF = 2816
FF_BLOCK = 2 * D_FF // 4
PLE_DIM = 256
CHUNK = 128
GM_HEADS = 4
GM_WIDTH = 512
SB_HEAD_DIM = 64
SB_WIDTH = 512
MIX_IN_WIDTH = 2 * GM_WIDTH + 3 * SB_WIDTH
MIX_BLOCK = MIX_IN_WIDTH // 4
EPS = 1e-6
N_CHIPS = 4
LANES = 128
ATT_BLOCK = 128
ATT_Q = 512
VMEM_LIMIT = 56 * 1024 * 1024

ADAM_LR, ADAM_B1, ADAM_B2, ADAM_EPS, ADAM_WD, ADAM_STEP = 0.001, 0.9, 0.999, 1e-08, 0.01, 10


def _dot(a, b):
    return jnp.dot(a, b, preferred_element_type=F32)


def _dot_nt(a, b):
    return lax.dot_general(a, b, (((1,), (1,)), ((), ())), preferred_element_type=F32)


def _dot_tn(a, b):
    return lax.dot_general(a, b, (((0,), (0,)), ((), ())), preferred_element_type=F32)


def _resident(shape):
    nd = len(shape)
    return pl.BlockSpec(shape, lambda *_: (0,) * nd, pipeline_mode=pl.Buffered(1))


def _rows(tm, width):
    return pl.BlockSpec((tm, width), lambda i: (i, 0))


def _params(n_axes=1):
    return pltpu.CompilerParams(dimension_semantics=("arbitrary",) * n_axes, vmem_limit_bytes=VMEM_LIMIT)


def _rstd(h):
    return lax.rsqrt(jnp.mean(h * h, axis=-1, keepdims=True) + EPS)


def _rms_bwd(dy, h, r, g):
    dyg = dy * g
    dh = r * dyg - h * (r * r * r) * jnp.mean(dyg * h, axis=-1, keepdims=True)
    return dh, dy * h * r


def _gelu(x):
    return 0.5 * x * (1.0 + lax.erf(x * (2.0 ** -0.5)))


def _gelu_grad(x):
    return 0.5 * (1.0 + lax.erf(x * (2.0 ** -0.5))) + x * jnp.exp(-0.5 * x * x) * ((2.0 * jnp.pi) ** -0.5)


def _token_tile(t, wide=False):
    return min(512 if wide else 256, t)


def _ffn_fwd(h, g, win, wout, name, exchange=None):
    t = h.shape[0]
    tm = _token_tile(t)

    def body(h_ref, g_ref, win_ref, wout_ref, ho_ref, gu_ref):
        hh = h_ref[...]
        n = (hh * _rstd(hh) * g_ref[...]).astype(BF16)
        acc = jnp.zeros((tm, D_MODEL), F32)
        for jb in range(2):
            gate = _dot(n, win_ref[jb])
            up = _dot(n, win_ref[2 + jb])
            gu_ref[:, jb * FF_BLOCK:(jb + 1) * FF_BLOCK] = gate.astype(BF16)
            gu_ref[:, D_FF + jb * FF_BLOCK:D_FF + (jb + 1) * FF_BLOCK] = up.astype(BF16)
            act = (gate * jax.nn.sigmoid(gate) * up).astype(BF16)
            acc = acc + _dot(act, wout_ref[jb * FF_BLOCK:(jb + 1) * FF_BLOCK, :])
        ho_ref[...] = hh + 0.5 * acc

    n = t // tm
    return _call(
        body, (h, g, win, wout), name=name, grid=(n,),
        in_specs=[_rows(tm, D_MODEL), _resident((1, D_MODEL)), _resident(win.shape), _resident(wout.shape)],
        out_specs=[_rows(tm, D_MODEL), _rows(tm, 2 * D_FF)],
        out_shape=[jax.ShapeDtypeStruct((t, D_MODEL), F32), jax.ShapeDtypeStruct((t, 2 * D_FF), BF16)],
        exchange=exchange, steps=(0, (2 * n) // 3, n - 1))


def _ffn_bwd(dho, h, g, gu, win, wout, name, exchange=None):
    t = h.shape[0]
    tm = _token_tile(t)

    def body(dho_ref, h_ref, g_ref, gu_ref, win_ref, wout_ref, dh_ref, dgu_ref, n_ref, act_ref, dhh_ref, dg_ref):
        i = pl.program_id(0)
        hh = h_ref[...]
        gg = g_ref[...]
        r = _rstd(hh)
        n_ref[...] = (hh * r * gg).astype(BF16)
        dho = dho_ref[...]
        dhh = (0.5 * dho).astype(BF16)
        dhh_ref[...] = dhh
        dn = jnp.zeros((tm, D_MODEL), F32)
        for jb in range(2):
            cg = slice(jb * FF_BLOCK, (jb + 1) * FF_BLOCK)
            cu = slice(D_FF + jb * FF_BLOCK, D_FF + (jb + 1) * FF_BLOCK)
            dact = _dot_nt(dhh, wout_ref[cg, :])
            gate = gu_ref[:, cg].astype(F32)
            up = gu_ref[:, cu].astype(F32)
            sg = jax.nn.sigmoid(gate)
            silu = gate * sg
            act_ref[:, cg] = (silu * up).astype(BF16)
            dgate = (dact * up * (sg * (1.0 + gate * (1.0 - sg)))).astype(BF16)
            dup = (dact * silu).astype(BF16)
            dgu_ref[:, cg] = dgate
            dgu_ref[:, cu] = dup
            dn = dn + _dot_nt(dgate, win_ref[jb]) + _dot_nt(dup, win_ref[2 + jb])
        dh, dg_rows = _rms_bwd(dn, hh, r, gg)
        dh_ref[...] = dho + dh

        @pl.when(i == 0)
        def _():
            dg_ref[...] = jnp.zeros_like(dg_ref)

        dg_ref[...] += jnp.sum(dg_rows, axis=0, keepdims=True)

    n = t // tm
    return _call(
        body, (dho, h, g, gu, win, wout), name=name, grid=(n,),
        in_specs=[_rows(tm, D_MODEL), _rows(tm, D_MODEL), _resident((1, D_MODEL)), _rows(tm, 2 * D_FF),
                  _resident(win.shape), _resident(wout.shape)],
        out_specs=[_rows(tm, D_MODEL), _rows(tm, 2 * D_FF), _rows(tm, D_MODEL), _rows(tm, D_FF), _rows(tm, D_MODEL),
                   pl.BlockSpec((1, D_MODEL), lambda i: (0, 0))],
        out_shape=[jax.ShapeDtypeStruct((t, D_MODEL), F32), jax.ShapeDtypeStruct((t, 2 * D_FF), BF16),
                   jax.ShapeDtypeStruct((t, D_MODEL), BF16), jax.ShapeDtypeStruct((t, D_FF), BF16),
                   jax.ShapeDtypeStruct((t, D_MODEL), BF16), jax.ShapeDtypeStruct((1, D_MODEL), F32)],
        exchange=exchange, steps=(0, n - 1))


def _wgrad(a, b, out_shape, out_block, out_index, a_width, b_width, grid_ij, name, exchange=None):
    t = a.shape[0]
    tk = t if 4 * t * (a_width + b_width) <= 26 * 2 ** 20 else min(2048, t)
    nk = t // tk

    def body(a_ref, b_ref, o_ref, ob_ref):
        k = pl.program_id(2)
        prod = _dot_tn(a_ref[...], b_ref[...]).reshape(o_ref.shape)

        @pl.when(k == 0)
        def _():
            o_ref[...] = prod

        @pl.when(k > 0)
        def _():
            o_ref[...] += prod

        @pl.when(k == nk - 1)
        def _():
            ob_ref[...] = o_ref[...].astype(BF16)

    grid = (*grid_ij, nk)
    out_spec = pl.BlockSpec(out_block, lambda i, j, k: out_index(i, j))
    out, got = _call(
        body, (a, b), name=name, grid=grid,
        in_specs=[pl.BlockSpec((tk, a_width), lambda i, j, k: (k, i)), pl.BlockSpec((tk, b_width), lambda i, j, k: (k, j))],
        out_specs=[out_spec, out_spec],
        out_shape=[jax.ShapeDtypeStruct(out_shape, F32), jax.ShapeDtypeStruct(out_shape, BF16)],
        exchange=exchange, steps=(0, grid[0] * grid[1] * grid[2] - 1))
    return tuple(out) if exchange is None else (tuple(out), got)


def _wgrad_cols(a, b, n_blocks, name, exchange=None):
    ka, nb = a.shape[1], b.shape[1] // n_blocks
    return _wgrad(a, b, (n_blocks, ka, nb), (1, ka, nb), lambda i, j: (j, 0, 0), ka, nb, (1, n_blocks), name, exchange)


def _wgrad_rows(a, b, n_blocks, name, exchange=None):
    ka, nb = a.shape[1] // n_blocks, b.shape[1]
    return _wgrad(a, b, (a.shape[1], nb), (ka, nb), lambda i, j: (i, 0), ka, nb, (n_blocks, 1), name, exchange)


def _mix_in_fwd(h, g, wmix):
    t = h.shape[0]
    tm = _token_tile(t, wide=True)
    gw2 = 2 * GM_WIDTH

    def body(h_ref, g_ref, w_ref, zg_ref, qkv_ref):
        hh = h_ref[...]
        n = (hh * _rstd(hh) * g_ref[...]).astype(BF16)
        for b in range(N_CHIPS):
            z = _dot(n, w_ref[b])
            lo, hi = b * MIX_BLOCK, (b + 1) * MIX_BLOCK
            if hi <= gw2:
                zg_ref[:, lo:hi] = z
            elif lo >= gw2:
                qkv_ref[:, lo - gw2:hi - gw2] = z.astype(BF16)
            else:
                zg_ref[:, lo:gw2] = z[:, :gw2 - lo]
                qkv_ref[:, 0:hi - gw2] = z[:, gw2 - lo:].astype(BF16)

    return pl.pallas_call(
        body, name="mix_in_fwd", grid=(t // tm,),
        in_specs=[_rows(tm, D_MODEL), _resident((1, D_MODEL)), _resident(wmix.shape)],
        out_specs=[_rows(tm, gw2), _rows(tm, 3 * SB_WIDTH)],
        out_shape=[jax.ShapeDtypeStruct((t, gw2), F32), jax.ShapeDtypeStruct((t, 3 * SB_WIDTH), BF16)],
        compiler_params=_params(),
    )(h, g, wmix)


def _causal_chunk_mask():
    row = lax.broadcasted_iota(jnp.int32, (CHUNK, CHUNK), 0)
    col = lax.broadcasted_iota(jnp.int32, (CHUNK, CHUNK), 1)
    return row >= col


def _gmlp_tile(t):
    return min(512, t)


def _gmlp_fwd(zg, gv, ws, bt):
    t = zg.shape[0]
    tm = _gmlp_tile(t)

    def body(zg_ref, gv_ref, ws_ref, bt_ref, o_ref):
        u = _gelu(zg_ref[:, :GM_WIDTH])
        v = _gelu(zg_ref[:, GM_WIDTH:])
        vn = (v * _rstd(v) * gv_ref[...]).astype(BF16)
        mask = _causal_chunk_mask()
        for hd in range(GM_HEADS):
            wm = jnp.where(mask, ws_ref[hd], 0.0).astype(BF16)
            cols = slice(hd * CHUNK, (hd + 1) * CHUNK)
            for c in range(tm // CHUNK):
                rows = slice(c * CHUNK, (c + 1) * CHUNK)
                sv = _dot(wm, vn[rows, cols]) + bt_ref[:, hd:hd + 1]
                o_ref[rows, cols] = (u[rows, cols] * sv).astype(BF16)

    return pl.pallas_call(
        body, name="gmlp_fwd", grid=(t // tm,),
        in_specs=[_rows(tm, 2 * GM_WIDTH), _resident((1, GM_WIDTH)), _resident(ws.shape), _resident(bt.shape)],
        out_specs=_rows(tm, GM_WIDTH),
        out_shape=jax.ShapeDtypeStruct((t, D_MODEL), BF16),
        compiler_params=_params(),
    )(zg, gv, ws, bt)


def _gmlp_bwd(zg, dmixed, gv, ws, bt):
    t = zg.shape[0]
    tm = _gmlp_tile(t)

    def body(zg_ref, dgm_ref, gv_ref, ws_ref, bt_ref, dzg_ref, dws_ref, dbt_ref, dgv_ref):
        i = pl.program_id(0)

        @pl.when(i == 0)
        def _():
            dws_ref[...] = jnp.zeros_like(dws_ref)
            dbt_ref[...] = jnp.zeros_like(dbt_ref)
            dgv_ref[...] = jnp.zeros_like(dgv_ref)

        zu = zg_ref[:, :GM_WIDTH]
        zv = zg_ref[:, GM_WIDTH:]
        u = _gelu(zu)
        v = _gelu(zv)
        r = _rstd(v)
        gvv = gv_ref[...]
        vn = (v * r * gvv).astype(BF16)
        dgm = dgm_ref[...].astype(F32)
        dsv = (dgm * u).astype(BF16)
        mask = _causal_chunk_mask()
        du_cols, dvn_cols = [], []
        for hd in range(GM_HEADS):
            wm = jnp.where(mask, ws_ref[hd], 0.0).astype(BF16)
            cols = slice(hd * CHUNK, (hd + 1) * CHUNK)
            dw = jnp.zeros((CHUNK, CHUNK), F32)
            db = jnp.zeros((CHUNK, 1), F32)
            du_rows, dvn_rows = [], []
            for c in range(tm // CHUNK):
                rows = slice(c * CHUNK, (c + 1) * CHUNK)
                sv = _dot(wm, vn[rows, cols]) + bt_ref[:, hd:hd + 1]
                du_rows.append(dgm[rows, cols] * sv)
                dvn_rows.append(_dot_tn(wm, dsv[rows, cols]))
                dw = dw + _dot_nt(dsv[rows, cols], vn[rows, cols])
                db = db + jnp.sum(dsv[rows, cols].astype(F32), axis=1, keepdims=True)
            dws_ref[hd] += jnp.where(mask, dw, 0.0)
            dbt_ref[:, hd:hd + 1] += db
            du_cols.append(jnp.concatenate(du_rows, axis=0))
            dvn_cols.append(jnp.concatenate(dvn_rows, axis=0))
        du = jnp.concatenate(du_cols, axis=1)
        dvn = jnp.concatenate(dvn_cols, axis=1)
        dv, dgv_rows = _rms_bwd(dvn, v, r, gvv)
        dgv_ref[...] += jnp.sum(dgv_rows, axis=0, keepdims=True)
        dzg_ref[:, :GM_WIDTH] = (du * _gelu_grad(zu)).astype(BF16)
        dzg_ref[:, GM_WIDTH:] = (dv * _gelu_grad(zv)).astype(BF16)

    const = lambda nd: (lambda i: (0,) * nd)
    return pl.pallas_call(
        body, name="gmlp_bwd", grid=(t // tm,),
        in_specs=[_rows(tm, 2 * GM_WIDTH), _rows(tm, GM_WIDTH), _resident((1, GM_WIDTH)), _resident(ws.shape),
                  _resident(bt.shape)],
        out_specs=[_rows(tm, 2 * GM_WIDTH), pl.BlockSpec(ws.shape, const(3)), pl.BlockSpec(bt.shape, const(2)),
                   pl.BlockSpec((1, GM_WIDTH), const(2))],
        out_shape=[jax.ShapeDtypeStruct((t, 2 * GM_WIDTH), BF16), jax.ShapeDtypeStruct(ws.shape, F32),
                   jax.ShapeDtypeStruct(bt.shape, F32), jax.ShapeDtypeStruct((1, GM_WIDTH), F32)],
        compiler_params=_params(),
    )(zg, dmixed, gv, ws, bt)


def _att_masks():
    tb = ATT_BLOCK
    lane = lax.broadcasted_iota(jnp.int32, (1, LANES), 1)
    rj = lax.broadcasted_iota(jnp.int32, (2 * tb, 2 * tb), 0)
    cs = lax.broadcasted_iota(jnp.int32, (2 * tb, 2 * tb), 1)
    same_head = ((rj < tb) & (cs < tb)) | ((rj >= tb) & (cs >= tb))
    suffix = jnp.where(same_head & (rj >= cs), 1.0, 0.0).astype(BF16)
    prefix = jnp.where(same_head & (rj <= cs), 1.0, 0.0).astype(BF16)
    left = lax.broadcasted_iota(jnp.int32, (1, 2 * tb), 1) < tb
    tq = lax.broadcasted_iota(jnp.int32, (ATT_Q, 4 * tb), 0)
    ts = lax.broadcasted_iota(jnp.int32, (ATT_Q, 4 * tb), 1)
    key = jnp.where(ts < 2 * tb, ts & (tb - 1), (ts & (tb - 1)) + tb)
    return lane, suffix, prefix, left, key, tq


def _att_fill(k_ref, v_ref, kcat, vcat, n_blocks, lane):
    tb = ATT_BLOCK
    first = lane < SB_HEAD_DIM

    def fill(jb, carry):
        rows = pl.ds(pl.multiple_of(jb * tb, tb), tb)
        top = pl.ds(pl.multiple_of(jb * 2 * tb, tb), tb)
        bot = pl.ds(pl.multiple_of(jb * 2 * tb + tb, tb), tb)
        kb = k_ref[rows, :]
        vb = v_ref[rows, :]
        zero = jnp.zeros_like(kb)
        kcat[top, :] = jnp.where(first, kb, zero)
        kcat[bot, :] = jnp.where(first, zero, kb)
        vcat[top, :] = jnp.where(first, vb, zero)
        vcat[bot, :] = jnp.where(first, zero, vb)
        return carry

    lax.fori_loop(0, n_blocks, fill, 0)


def _block_sums(x, m):
    return _dot(x.astype(BF16), m)


def _softplus2(z2):
    return jnp.maximum(z2, 0.0) + jnp.log2(1.0 + jnp.exp2(-jnp.abs(z2)))


def _scaled_queries(q_ref, base2):
    scale = SB_HEAD_DIM ** -0.5
    return (q_ref[...].astype(F32) * (scale * math.log2(math.e) if base2 else scale)).astype(BF16)


def _att_specs(t):
    n_pairs = SB_WIDTH // LANES
    q_spec = pl.BlockSpec((ATT_Q, LANES), lambda p, i: (i, p))
    k_spec = pl.BlockSpec((t, LANES), lambda p, i: (0, n_pairs + p))
    v_spec = pl.BlockSpec((t, LANES), lambda p, i: (0, 2 * n_pairs + p))
    return n_pairs, q_spec, k_spec, v_spec


def _attn_fwd(qkv, mixed, exchange=None):
    t = qkv.shape[0]
    tb = ATT_BLOCK
    nkb = t // tb
    assert 2 * nkb <= LANES and t % ATT_Q == 0 and ATT_Q == 4 * tb
    n_pairs, q_spec, k_spec, v_spec = _att_specs(t)

    def body(q_ref, k_ref, v_ref, mixed_ref, o_ref, ct_ref, kcat, vcat, acc, carry, z0, r0, z1, r1):
        i = pl.program_id(1)
        lane, suffix, _, left, key, tq = _att_masks()

        @pl.when(i == 0)
        def _():
            _att_fill(k_ref, v_ref, kcat, vcat, nkb, lane)

        q = _scaled_queries(q_ref, True)
        acc[...] = jnp.zeros_like(acc)
        carry[...] = jnp.zeros_like(carry)
        ct_ref[0] = jnp.zeros((ATT_Q, LANES), F32)

        def key_rows(m):
            return pl.ds(pl.multiple_of(m * 4 * tb, 4 * tb), 4 * tb)

        def scores(m, zb, rb, causal=None, rs=slice(None)):
            z = _dot_nt(q[rs], kcat[key_rows(m), :])
            zb[rs, :] = z
            sp = _softplus2(z)
            if causal is not None:
                sp = jnp.where(causal[rs], sp, 0.0)
            for g in (1, 0):
                cols = slice(g * 2 * tb, (g + 1) * 2 * tb)
                rb[rs, cols] = _block_sums(sp[:, cols], suffix)

        def weigh(m, zb, rb, causal=None, rs=slice(None)):
            probs = [None, None]
            for g in (1, 0):
                cols = slice(g * 2 * tb, (g + 1) * 2 * tb)
                j = 2 * m + g
                r = rb[rs, cols]
                c = carry[rs, :]
                ct_ref[0, rs, :] = jnp.where(lane == j, c[:, :tb], jnp.where(lane == nkb + j, c[:, tb:], ct_ref[0, rs, :]))
                a = jnp.exp2(zb[rs, cols] - (r + c))
                if causal is not None:
                    a = jnp.where(causal[rs, cols], a, 0.0)
                probs[g] = a.astype(BF16)
                carry[rs, :] = c + jnp.where(left, r[:, 0:1], r[:, tb:tb + 1])
            acc[rs, :] += _dot(jnp.concatenate(probs, axis=1), vcat[key_rows(m), :])

        sooner, later, late_rows = key < tq, key + 2 * tb < tq, slice(2 * tb, 4 * tb)
        scores(2 * i + 1, z1, r1, later, late_rows)
        scores(2 * i, z0, r0, sooner)
        weigh(2 * i + 1, z1, r1, later, late_rows)
        weigh(2 * i, z0, r0, sooner)

        @pl.when(i > 0)
        def _():
            scores(2 * i - 1, z1, r1)

            def loop(k, c):
                u = i - 1 - k
                scores(2 * u, z0, r0)
                weigh(2 * u + 1, z1, r1)
                scores(2 * u - 1, z1, r1)
                weigh(2 * u, z0, r0)
                return c

            lax.fori_loop(0, i - 1, loop, 0)
            scores(0, z0, r0)
            weigh(1, z1, r1)
            weigh(0, z0, r0)

        o_ref[...] = acc[...].astype(BF16)

    tile = pltpu.VMEM((ATT_Q, 4 * tb), F32)

    nq = t // ATT_Q
    return _call(
        body, (qkv, qkv, qkv, mixed), name="attn_fwd", grid=(n_pairs, nq),
        in_specs=[q_spec, k_spec, v_spec, _ANY],
        out_specs=[pl.BlockSpec((ATT_Q, LANES), lambda p, i: (i, GM_WIDTH // LANES + p)),
                   pl.BlockSpec((1, ATT_Q, LANES), lambda p, i: (p, i, 0))],
        out_shape=[jax.ShapeDtypeStruct(mixed.shape, BF16), jax.ShapeDtypeStruct((n_pairs, t, LANES), F32)],
        aliases={3: 0},
        scratch_shapes=[pltpu.VMEM((2 * t, LANES), BF16), pltpu.VMEM((2 * t, LANES), BF16),
                        pltpu.VMEM((ATT_Q, LANES), F32), pltpu.VMEM((ATT_Q, 2 * tb), F32), tile, tile, tile, tile],
        exchange=exchange, steps=(0, (n_pairs - 1) * nq - 1, n_pairs * nq - 1))


def _attn_bwd(qkv, dmixed, carries, exchange=None):
    t = qkv.shape[0]
    tb = ATT_BLOCK
    nkb = t // tb
    nq = t // ATT_Q
    scale = SB_HEAD_DIM ** -0.5
    n_pairs, q_spec, k_spec, v_spec = _att_specs(t)
    gm_blocks = GM_WIDTH // LANES

    def body(q_ref, k_ref, v_ref, do_ref, ct_ref, dq_ref, dk_ref, dv_ref, kcat, vcat, dkacc, dvacc, dqacc, carry,
             z0, r0, s0, a0, z1, r1, s1, a1):
        i = pl.program_id(1)
        lane, suffix, prefix, left, key, tq = _att_masks()
        first = lane < SB_HEAD_DIM

        @pl.when(i == 0)
        def _():
            _att_fill(k_ref, v_ref, kcat, vcat, nkb, lane)
            dkacc[...] = jnp.zeros_like(dkacc)
            dvacc[...] = jnp.zeros_like(dvacc)

        q2 = _scaled_queries(q_ref, True)
        q = _scaled_queries(q_ref, False)
        do = do_ref[...]
        dqacc[...] = jnp.zeros_like(dqacc)
        carry[...] = jnp.zeros_like(carry)

        def key_rows(m):
            return pl.ds(pl.multiple_of(m * 4 * tb, 4 * tb), 4 * tb)

        def front(m, bufs, causal=None, rs=slice(None)):
            zb, rb, sb, ab = bufs
            z = _dot_nt(q2[rs], kcat[key_rows(m), :])
            zb[rs, :] = z
            sp = _softplus2(z)
            sb[rs, :] = jnp.exp2(z - sp)
            if causal is not None:
                sp = jnp.where(causal[rs], sp, 0.0)
            for g in (0, 1):
                cols = slice(g * 2 * tb, (g + 1) * 2 * tb)
                rb[rs, cols] = _block_sums(sp[:, cols], suffix)
            ab[rs, :] = _dot_nt(do[rs], vcat[key_rows(m), :])

        def back(m, bufs, causal=None, rs=slice(None)):
            zb, rb, sb, ab = bufs
            dzs, probs = [None, None], [None, None]
            for g in (0, 1):
                cols = slice(g * 2 * tb, (g + 1) * 2 * tb)
                j = 2 * m + g
                ct = ct_ref[0, rs, :]
                ca = jnp.sum(jnp.where(lane == j, ct, 0.0), axis=1, keepdims=True)
                cb = jnp.sum(jnp.where(lane == nkb + j, ct, 0.0), axis=1, keepdims=True)
                a = jnp.exp2(zb[rs, cols] - (rb[rs, cols] + jnp.where(left, ca, cb)))
                if causal is not None:
                    a = jnp.where(causal[rs, cols], a, 0.0)
                de = ab[rs, cols] * a
                cl = _block_sums(de, prefix)
                pre = carry[rs, :]
                dz = de - sb[rs, cols] * (cl + pre)
                if causal is not None:
                    dz = jnp.where(causal[rs, cols], dz, 0.0)
                carry[rs, :] = pre + jnp.where(left, cl[:, tb - 1:tb], cl[:, 2 * tb - 1:2 * tb])
                dzs[g] = dz.astype(BF16)
                probs[g] = a.astype(BF16)
            dzb = jnp.concatenate(dzs, axis=1)
            dqacc[rs, :] += _dot(dzb, kcat[key_rows(m), :])
            dkc = _dot_tn(dzb, q[rs])
            dvc = _dot_tn(jnp.concatenate(probs, axis=1), do[rs])
            out_rows = pl.ds(pl.multiple_of(m * 2 * tb, 2 * tb), 2 * tb)
            pick = lambda x: jnp.concatenate([jnp.where(first, x[0:tb], x[tb:2 * tb]),
                                              jnp.where(first, x[2 * tb:3 * tb], x[3 * tb:4 * tb])], axis=0)
            dkacc[out_rows, :] += pick(dkc)
            dvacc[out_rows, :] += pick(dvc)

        b0, b1 = (z0, r0, s0, a0), (z1, r1, s1, a1)

        @pl.when(i > 0)
        def _():
            front(0, b0)

            def loop(u, c):
                front(2 * u + 1, b1)
                back(2 * u, b0)
                front(2 * u + 2, b0)
                back(2 * u + 1, b1)
                return c

            lax.fori_loop(0, i - 1, loop, 0)
            front(2 * i - 1, b1)
            back(2 * i - 2, b0)
            back(2 * i - 1, b1)

        sooner, later, late_rows = key < tq, key + 2 * tb < tq, slice(2 * tb, 4 * tb)
        front(2 * i, b0, sooner)
        front(2 * i + 1, b1, later, late_rows)
        back(2 * i, b0, sooner)
        back(2 * i + 1, b1, later, late_rows)

        dq_ref[...] = (dqacc[...] * scale).astype(BF16)

        @pl.when(i == nq - 1)
        def _():
            dk_ref[...] = dkacc[...].astype(BF16)
            dv_ref[...] = dvacc[...].astype(BF16)

    col = pl.BlockSpec((t, LANES), lambda p, i: (0, p))
    out = jax.ShapeDtypeStruct((t, SB_WIDTH), BF16)
    tile = pltpu.VMEM((ATT_Q, 4 * tb), F32)
    return _call(
        body, (qkv, qkv, qkv, dmixed, carries), name="attn_bwd", grid=(n_pairs, nq),
        in_specs=[q_spec, k_spec, v_spec, pl.BlockSpec((ATT_Q, LANES), lambda p, i: (i, gm_blocks + p)),
                  pl.BlockSpec((1, ATT_Q, LANES), lambda p, i: (p, i, 0))],
        out_specs=[pl.BlockSpec((ATT_Q, LANES), lambda p, i: (i, p)), col, col],
        out_shape=[out, out, out],
        scratch_shapes=[pltpu.VMEM((2 * t, LANES), BF16), pltpu.VMEM((2 * t, LANES), BF16),
                        pltpu.VMEM((t, LANES), F32), pltpu.VMEM((t, LANES), F32),
                        pltpu.VMEM((ATT_Q, LANES), F32), pltpu.VMEM((ATT_Q, 2 * tb), F32)] + [tile] * 8,
        exchange=exchange, steps=(0, n_pairs * nq - 1))


def _matmul_residual(res, a, w, name):
    t = a.shape[0]
    tm = _token_tile(t, wide=True)

    def body(res_ref, a_ref, w_ref, o_ref):
        o_ref[...] = res_ref[...] + _dot(a_ref[...], w_ref[...])

    return pl.pallas_call(
        body, name=name, grid=(t // tm,),
        in_specs=[_rows(tm, res.shape[1]), _rows(tm, a.shape[1]), _resident(w.shape)],
        out_specs=_rows(tm, res.shape[1]),
        out_shape=jax.ShapeDtypeStruct(res.shape, F32),
        compiler_params=_params(),
    )(res, a, w)


def _matmul_nt_cast(dy, w, name):
    t = dy.shape[0]
    tm = _token_tile(t, wide=True)

    def body(dy_ref, w_ref, o_ref, dyb_ref):
        dyb = dy_ref[...].astype(BF16)
        dyb_ref[...] = dyb
        o_ref[...] = _dot_nt(dyb, w_ref[...]).astype(BF16)

    return pl.pallas_call(
        body, name=name, grid=(t // tm,),
        in_specs=[_rows(tm, dy.shape[1]), _resident(w.shape)],
        out_specs=[_rows(tm, w.shape[0]), _rows(tm, dy.shape[1])],
        out_shape=[jax.ShapeDtypeStruct((t, w.shape[0]), BF16), jax.ShapeDtypeStruct(dy.shape, BF16)],
        compiler_params=_params(),
    )(dy, w)


def _norm_input_bwd(dres, dz, w, h, g, name):
    t = h.shape[0]
    tm = _token_tile(t, wide=True)
    nb, _, width = w.shape

    def body(dres_ref, dz_ref, w_ref, h_ref, g_ref, dh_ref, n_ref, dg_ref):
        i = pl.program_id(0)
        hh = h_ref[...]
        gg = g_ref[...]
        r = _rstd(hh)
        n_ref[...] = (hh * r * gg).astype(BF16)
        dn = jnp.zeros((tm, D_MODEL), F32)
        for b in range(nb):
            dn = dn + _dot_nt(dz_ref[:, b * width:(b + 1) * width], w_ref[b])
        dh, dg_rows = _rms_bwd(dn, hh, r, gg)
        dh_ref[...] = dres_ref[...] + dh

        @pl.when(i == 0)
        def _():
            dg_ref[...] = jnp.zeros_like(dg_ref)

        dg_ref[...] += jnp.sum(dg_rows, axis=0, keepdims=True)

    return pl.pallas_call(
        body, name=name, grid=(t // tm,),
        in_specs=[_rows(tm, D_MODEL), _rows(tm, nb * width), _resident(w.shape), _rows(tm, D_MODEL),
                  _resident((1, D_MODEL))],
        out_specs=[_rows(tm, D_MODEL), _rows(tm, D_MODEL), pl.BlockSpec((1, D_MODEL), lambda i: (0, 0))],
        out_shape=[jax.ShapeDtypeStruct((t, D_MODEL), F32), jax.ShapeDtypeStruct((t, D_MODEL), BF16),
                   jax.ShapeDtypeStruct((1, D_MODEL), F32)],
        compiler_params=_params(),
    )(dres, dz, w, h, g)


def _head(h, p, target, gple, gfin, wg, wproj):
    t = h.shape[0]
    tm = _token_tile(t)
    pw = D_MODEL // N_CHIPS

    def body(h_ref, p_ref, tgt_ref, gple_ref, gfin_ref, wg_ref, wproj_ref,
             loss_ref, dgf_ref, dgple_ref, dh_ref, dgp_ref, dpp_ref, n_ref, pb_ref):
        i = pl.program_id(0)
        hh = h_ref[...]
        r_in = _rstd(hh)
        gp = gple_ref[...]
        n = (hh * r_in * gp).astype(BF16)
        n_ref[...] = n
        gate = jax.nn.sigmoid(_dot(n, wg_ref[...]))
        pb = p_ref[...].astype(BF16)
        pb_ref[...] = pb
        pp = jnp.concatenate([_dot(pb, wproj_ref[b]) for b in range(N_CHIPS)], axis=1)
        h4 = hh + gate * pp
        r = _rstd(h4)
        gf = gfin_ref[...]
        err = h4 * r * gf - tgt_ref[...]
        dy = err * (1.0 / D_MODEL)
        dh4, dgf_rows = _rms_bwd(dy, h4, r, gf)
        dgp = (dh4 * pp * gate * (1.0 - gate)).astype(BF16)
        dgp_ref[...] = dgp
        dpp_ref[...] = (dh4 * gate).astype(BF16)
        dh, dgple_rows = _rms_bwd(_dot_nt(dgp, wg_ref[...]), hh, r_in, gp)
        dh_ref[...] = dh4 + dh

        @pl.when(i == 0)
        def _():
            loss_ref[...] = jnp.zeros_like(loss_ref)
            dgf_ref[...] = jnp.zeros_like(dgf_ref)
            dgple_ref[...] = jnp.zeros_like(dgple_ref)

        loss_ref[...] += (0.5 / D_MODEL) * jnp.sum(err * err)
        dgf_ref[...] += jnp.sum(dgf_rows, axis=0, keepdims=True)
        dgple_ref[...] += jnp.sum(dgple_rows, axis=0, keepdims=True)

    bf = lambda w: jax.ShapeDtypeStruct((t, w), BF16)
    const = lambda i: (0, 0)
    return pl.pallas_call(
        body, name="head", grid=(t // tm,),
        in_specs=[_rows(tm, D_MODEL), _rows(tm, PLE_DIM), _rows(tm, D_MODEL), _resident((1, D_MODEL)),
                  _resident((1, D_MODEL)), _resident(wg.shape), _resident(wproj.shape)],
        out_specs=[pl.BlockSpec((1, LANES), const), pl.BlockSpec((1, D_MODEL), const), pl.BlockSpec((1, D_MODEL), const),
                   _rows(tm, D_MODEL), _rows(tm, D_MODEL), _rows(tm, D_MODEL), _rows(tm, D_MODEL), _rows(tm, PLE_DIM)],
        out_shape=[jax.ShapeDtypeStruct((1, LANES), F32), jax.ShapeDtypeStruct((1, D_MODEL), F32),
                   jax.ShapeDtypeStruct((1, D_MODEL), F32), jax.ShapeDtypeStruct((t, D_MODEL), F32), bf(D_MODEL),
                   bf(D_MODEL), bf(D_MODEL), bf(PLE_DIM)],
        compiler_params=_params(),
    )(h, p, target, gple, gfin, wg, wproj)


_BIG = ("ffn1_w_in", "ffn1_w_out", "w_mix_in", "w_mix_out", "ffn2_w_in", "ffn2_w_out", "ple_w_gate", "ple_w_proj")
_SMALL = ("ffn1_norm", "mix_norm", "gmlp_v_norm", "gmlp_w_s", "gmlp_b", "ffn2_norm", "ple_norm", "final_norm")
_ALL = ("ffn1_norm", "ffn1_w_in", "ffn1_w_out", "mix_norm", "w_mix_in", "gmlp_v_norm", "gmlp_w_s", "gmlp_b", "w_mix_out",
        "ffn2_norm", "ffn2_w_in", "ffn2_w_out", "ple_norm", "ple_w_gate", "ple_w_proj", "final_norm")
_ANY = pl.BlockSpec(memory_space=pl.ANY)
_MESH = pl.DeviceIdType.MESH


def _mesh_pos():
    return lax.axis_index("x"), lax.axis_index("y"), lax.axis_index("c")


def _other_chips(x, y):
    return [((x, 1 - y), 2 * x + 1 - y), ((1 - x, y), 2 * (1 - x) + y), ((1 - x, 1 - y), 2 * (1 - x) + 1 - y)]


def _remote(src, dst, send_sem, recv_sem, device):
    return pltpu.make_async_remote_copy(src_ref=src, dst_ref=dst, send_sem=send_sem, recv_sem=recv_sem,
                                        device_id=device, device_id_type=_MESH)


class _WeightGather:
    def __init__(self, shards):
        self.shapes = [s.shape for s in shards]
        self.operands = list(shards)
        self.out_shape = [jax.ShapeDtypeStruct((N_CHIPS, *s.shape), s.dtype) for s in shards]
        n = len(shards)
        self.per = 2 * (N_CHIPS - 1)
        self.scratch = [pltpu.SemaphoreType.DMA((self.per * n,)), pltpu.SemaphoreType.DMA((self.per * n,)),
                        pltpu.SemaphoreType.DMA((n,))]
        self.phases = [self.send, self.forward, self.finish]

    def _copies(self, ins, outs, sems):
        send_sems, recv_sems, local_sems = sems
        x, y, c = _mesh_pos()
        sibling = (x, y, 1 - c)
        mine = 2 * x + y
        local, first, landing, passed, arriving = [], [], [], [], []
        for w, shape in enumerate(self.shapes):
            hr = shape[0] // 2
            half = lambda blk, cc, w=w, hr=hr: outs[w].at[blk, pl.ds(cc * hr, hr), :]
            local.append(pltpu.make_async_copy(ins[w], outs[w].at[mine], local_sems.at[w]))
            for k, (chip, blk) in enumerate(_other_chips(x, y)):
                s = self.per * w + k
                first.append(_remote(ins[w].at[pl.ds(c * hr, hr), :], half(mine, c), send_sems.at[s], recv_sems.at[s],
                                     (*chip, c)))
                landing.append(_remote(half(blk, c), half(blk, c), send_sems.at[s], recv_sems.at[s], sibling))
                s = self.per * w + N_CHIPS - 1 + k
                passed.append(_remote(half(blk, c), half(blk, c), send_sems.at[s], recv_sems.at[s], sibling))
                arriving.append(_remote(half(blk, 1 - c), half(blk, 1 - c), send_sems.at[s], recv_sems.at[s], sibling))
        return local, first, landing, passed, arriving

    def send(self, ins, outs, sems):
        local, first, _, _, _ = self._copies(ins, outs, sems)
        for cp in local + first:
            cp.start()

    def forward(self, ins, outs, sems):
        _, _, landing, passed, _ = self._copies(ins, outs, sems)
        for landed, cp in zip(landing, passed):
            landed.wait_recv()
            cp.start()

    def finish(self, ins, outs, sems):
        local, first, _, passed, arriving = self._copies(ins, outs, sems)
        for cp in arriving:
            cp.wait_recv()
        for cp in first + passed:
            cp.wait_send()
        for cp in local:
            cp.wait()


class _ChipExchange:
    def __init__(self, sums):
        n = len(sums)
        self.n = n
        self.per = N_CHIPS - 1
        self.operands = list(sums)
        self.out_shape = [jax.ShapeDtypeStruct((self.per, *s.shape[1:]), s.dtype) for s in sums]
        self.scratch = [pltpu.SemaphoreType.DMA((self.per * n,)), pltpu.SemaphoreType.DMA((self.per * n,))]
        self.phases = [self.send, self.finish]

    def _copies(self, ins, outs, sems):
        send_sems, recv_sems = sems
        x, y, c = _mesh_pos()
        cps = []
        for w in range(self.n):
            for k, (chip, _) in enumerate(_other_chips(x, y)):
                s = self.per * w + k
                cps.append(_remote(ins[w].at[k + 1], outs[w].at[k], send_sems.at[s], recv_sems.at[s], (*chip, c)))
        return cps

    def send(self, ins, outs, sems):
        for cp in self._copies(ins, outs, sems):
            cp.start()

    def finish(self, ins, outs, sems):
        for cp in self._copies(ins, outs, sems):
            cp.wait()


def _run_exchange(ex, name):
    n_in, n_out = len(ex.operands), len(ex.out_shape)

    def body(*refs):
        ins, outs, sems = refs[:n_in], refs[n_in:n_in + n_out], refs[n_in + n_out:]
        for phase in ex.phases:
            phase(ins, outs, sems)

    return pl.pallas_call(body, name=name, in_specs=[_ANY] * n_in, out_specs=[_ANY] * n_out, out_shape=ex.out_shape,
                          scratch_shapes=ex.scratch)(*ex.operands)


def _call(body, args, *, name, grid, in_specs, out_specs, out_shape, scratch_shapes=(), exchange=None, steps=None,
          aliases=None):
    params = _params(len(grid))
    if exchange is None:
        out = pl.pallas_call(body, name=name, grid=grid, in_specs=in_specs, out_specs=out_specs, out_shape=out_shape,
                             scratch_shapes=list(scratch_shapes), input_output_aliases=aliases or {},
                             compiler_params=params)(*args)
        return out, None
    n_in, n_out, n_scr = len(in_specs), len(out_specs), len(scratch_shapes)
    n_xin, n_xout = len(exchange.operands), len(exchange.out_shape)
    assert len(steps) == len(exchange.phases)

    def hosting(*refs):
        cuts = [n_in, n_xin, n_out, n_xout, n_scr]
        parts, at = [], 0
        for size in cuts:
            parts.append(refs[at:at + size])
            at += size
        ins, xins, outs, xouts, scr = parts
        sems = refs[at:]
        step = 0
        for axis, size in enumerate(grid):
            step = step * size + pl.program_id(axis)
        pl.when(step == steps[0])(lambda: exchange.phases[0](xins, xouts, sems))
        body(*ins, *outs, *scr)
        for at_step, phase in zip(steps[1:], exchange.phases[1:]):
            pl.when(step == at_step)(functools.partial(phase, xins, xouts, sems))

    out = pl.pallas_call(
        hosting, name=name, grid=grid,
        in_specs=list(in_specs) + [_ANY] * n_xin, out_specs=list(out_specs) + [_ANY] * n_xout,
        out_shape=list(out_shape) + list(exchange.out_shape),
        scratch_shapes=list(scratch_shapes) + list(exchange.scratch), input_output_aliases=aliases or {},
        compiler_params=params,
    )(*args, *exchange.operands)
    return out[:n_out], out[n_out:]


def _pair_exchange(grads, name):
    n = len(grads)

    def body(*refs):
        ins, outs = refs[:n], refs[n:2 * n]
        send_sems, recv_sems = refs[2 * n:]
        x, y, c = _mesh_pos()
        cps = []
        for w in range(n):
            hr = grads[w].shape[1] // 2
            cp = _remote(ins[w].at[:, pl.ds((1 - c) * hr, hr), :], outs[w], send_sems.at[w], recv_sems.at[w], (x, y, 1 - c))
            cp.start()
            cps.append(cp)
        for cp in cps:
            cp.wait()

    return pl.pallas_call(
        body, name=name,
        in_specs=[_ANY] * n, out_specs=[_ANY] * n,
        out_shape=[jax.ShapeDtypeStruct((g.shape[0], g.shape[1] // 2, g.shape[2]), g.dtype) for g in grads],
        scratch_shapes=[pltpu.SemaphoreType.DMA((n,)), pltpu.SemaphoreType.DMA((n,))],
    )(*grads)


SUM_STEPS = 2


def _pair_sums(gs, sibling, pos, name):
    n = len(gs)

    def body(pos_ref, *refs):
        for k in range(n):
            refs[2 * n + k][...] = (refs[k][...] + refs[n + k][...].astype(F32)).astype(BF16)

    g_specs, a_specs, o_specs, out_shape = [], [], [], []
    for g in gs:
        nb, r, c = g.shape
        tr = r // 2 // SUM_STEPS
        g_specs.append(pl.BlockSpec((1, tr, c), lambda k, i, pos: (k ^ pos[0], pos[1] * SUM_STEPS + i, 0)))
        a_specs.append(pl.BlockSpec((1, tr, c), lambda k, i, pos: (k ^ pos[0], i, 0)))
        o_specs.append(pl.BlockSpec((1, tr, c), lambda k, i, pos: (k, i, 0)))
        out_shape.append(jax.ShapeDtypeStruct((nb, r // 2, c), BF16))
    return pl.pallas_call(
        body, name=name,
        grid_spec=pltpu.PrefetchScalarGridSpec(num_scalar_prefetch=1, grid=(N_CHIPS, SUM_STEPS),
                                               in_specs=g_specs + a_specs, out_specs=o_specs),
        out_shape=out_shape, compiler_params=_params(2),
    )(pos, *gs, *sibling)


def _chip_sums(pairs, others, pos, name):
    n = len(pairs)

    def body(pos_ref, *refs):
        for k in range(n):
            s_ref, b_ref = refs[k], refs[n + k]
            refs[2 * n + k][...] = ((s_ref[0].astype(F32) + b_ref[0].astype(F32))
                                    + (b_ref[1].astype(F32) + b_ref[2].astype(F32)))

    s_specs, b_specs, o_specs, out_shape = [], [], [], []
    for s in pairs:
        _, hr, c = s.shape
        tr = hr // SUM_STEPS
        s_specs.append(pl.BlockSpec((1, tr, c), lambda i, pos: (0, i, 0)))
        b_specs.append(pl.BlockSpec((N_CHIPS - 1, tr, c), lambda i, pos: (0, i, 0)))
        o_specs.append(pl.BlockSpec((tr, c), lambda i, pos: (pos[1] * SUM_STEPS + i, 0)))
        out_shape.append(jax.ShapeDtypeStruct((2 * hr, c), F32))
    return pl.pallas_call(
        body, name=name,
        grid_spec=pltpu.PrefetchScalarGridSpec(num_scalar_prefetch=1, grid=(SUM_STEPS,),
                                               in_specs=s_specs + b_specs, out_specs=o_specs),
        out_shape=out_shape, compiler_params=_params(),
    )(pos, *pairs, *others)


def _pair_share(grads, name):
    n = len(grads)

    def body(*refs):
        outs = refs[n:2 * n]
        send_sems, recv_sems = refs[2 * n:]
        x, y, c = _mesh_pos()
        cps = []
        for w in range(n):
            hr = grads[w].shape[0] // 2
            rows = outs[w].at[pl.ds(c * hr, hr), :]
            cp = _remote(rows, rows, send_sems.at[w], recv_sems.at[w], (x, y, 1 - c))
            cp.start()
            cps.append(cp)
        for w, cp in enumerate(cps):
            cp.wait_send()
            hr = grads[w].shape[0] // 2
            other = outs[w].at[pl.ds((1 - c) * hr, hr), :]
            _remote(other, other, send_sems.at[w], recv_sems.at[w], (x, y, 1 - c)).wait_recv()

    return pl.pallas_call(
        body, name=name,
        in_specs=[_ANY] * n, out_specs=[_ANY] * n,
        out_shape=[jax.ShapeDtypeStruct(g.shape, g.dtype) for g in grads],
        input_output_aliases={w: w for w in range(n)},
        scratch_shapes=[pltpu.SemaphoreType.DMA((n,)), pltpu.SemaphoreType.DMA((n,))],
    )(*grads)


def _all_reduce_small(rows, mats):
    n_sems = 2 * N_CHIPS
    n_rows = -(-len(rows) // 8) * 8
    heights = [math.prod(a.shape[:-1]) for a in mats]
    n_tall = -(-sum(heights) // 8) * 8
    arrays = list(rows) + list(mats)

    def body(*refs):
        ins, outs = refs[:len(arrays)], refs[len(arrays):2 * len(arrays)]
        wide, tall, wide_pair, tall_pair, wide_slots, tall_slots, send_sems, recv_sems = refs[2 * len(arrays):]
        x, y, c = _mesh_pos()
        wide[...] = jnp.zeros_like(wide)
        tall[...] = jnp.zeros_like(tall)
        for k, a in enumerate(rows):
            wide[k:k + 1, 0:a.shape[1]] = ins[k][...]
        at = 0
        for k, h in enumerate(heights):
            tall[at:at + h, :] = ins[len(rows) + k][...].reshape(h, LANES)
            at += h
        wide_pair[c] = wide[...]
        tall_pair[c] = tall[...]
        cps = [_remote(wide, wide_pair.at[c], send_sems.at[0], recv_sems.at[0], (x, y, 1 - c)),
               _remote(tall, tall_pair.at[c], send_sems.at[1], recv_sems.at[1], (x, y, 1 - c))]
        for cp in cps:
            cp.start()
        for cp in cps:
            cp.wait()
        wide[...] = wide_pair[0] + wide_pair[1]
        tall[...] = tall_pair[0] + tall_pair[1]
        mine = 2 * x + y
        wide_slots[mine] = wide[...]
        tall_slots[mine] = tall[...]
        cps = []
        for k, (chip, _) in enumerate(_other_chips(x, y)):
            for j, (buf, slots) in enumerate(((wide, wide_slots), (tall, tall_slots))):
                s = 2 + 2 * k + j
                cps.append(_remote(buf, slots.at[mine], send_sems.at[s], recv_sems.at[s], (*chip, c)))
        for cp in cps:
            cp.start()
        for cp in cps:
            cp.wait()
        wide_sum, tall_sum = wide_slots[0], tall_slots[0]
        for d in range(1, N_CHIPS):
            wide_sum = wide_sum + wide_slots[d]
            tall_sum = tall_sum + tall_slots[d]
        for k, a in enumerate(rows):
            outs[k][...] = wide_sum[k:k + 1, 0:a.shape[1]]
        at = 0
        for k, h in enumerate(heights):
            outs[len(rows) + k][...] = tall_sum[at:at + h, :].reshape(mats[k].shape)
            at += h

    vmem = pl.BlockSpec(memory_space=pltpu.VMEM)
    return pl.pallas_call(
        body, name="all_reduce_small",
        in_specs=[vmem] * len(arrays), out_specs=[vmem] * len(arrays),
        out_shape=[jax.ShapeDtypeStruct(a.shape, F32) for a in arrays],
        scratch_shapes=[pltpu.VMEM((n_rows, D_MODEL), F32), pltpu.VMEM((n_tall, LANES), F32),
                        pltpu.VMEM((2, n_rows, D_MODEL), F32), pltpu.VMEM((2, n_tall, LANES), F32),
                        pltpu.VMEM((N_CHIPS, n_rows, D_MODEL), F32), pltpu.VMEM((N_CHIPS, n_tall, LANES), F32),
                        pltpu.SemaphoreType.DMA((n_sems,)), pltpu.SemaphoreType.DMA((n_sems,))],
    )(*arrays)


def _adamw_small(items):
    n = len(items)
    bias1 = 1.0 - ADAM_B1 ** ADAM_STEP
    bias2 = 1.0 - ADAM_B2 ** ADAM_STEP

    def body(*refs):
        ins, outs = refs[:4 * n], refs[4 * n:]
        for k in range(n):
            w_ref, g_ref, m_ref, v_ref = ins[4 * k:4 * k + 4]
            gg = g_ref[...]
            m2 = ADAM_B1 * m_ref[...] + (1.0 - ADAM_B1) * gg
            v2 = ADAM_B2 * v_ref[...] + (1.0 - ADAM_B2) * (gg * gg)
            outs[3 * k + 1][...] = m2
            outs[3 * k + 2][...] = v2
            outs[3 * k][...] = -ADAM_LR * ((m2 / bias1) / (jnp.sqrt(v2 / bias2) + ADAM_EPS) + ADAM_WD * w_ref[...])

    vmem = pl.BlockSpec(memory_space=pltpu.VMEM)
    out = pl.pallas_call(
        body, name="adamw_small", in_specs=[vmem] * (4 * n), out_specs=[vmem] * (3 * n),
        out_shape=[jax.ShapeDtypeStruct(w.shape, F32) for w, _, _, _ in items for _ in range(3)],
    )(*[a for item in items for a in item])
    return [tuple(out[3 * k:3 * k + 3]) for k in range(n)]


ADAMW_STEPS = 8


def _adamw(items, name, exchange=None):
    n = len(items)
    bias1 = 1.0 - ADAM_B1 ** ADAM_STEP
    bias2 = 1.0 - ADAM_B2 ** ADAM_STEP

    def body(*refs):
        ins, outs = refs[:4 * n], refs[4 * n:]
        for k in range(n):
            w_ref, g_ref, m_ref, v_ref = ins[4 * k:4 * k + 4]
            d_ref, mo_ref, vo_ref = outs[3 * k:3 * k + 3]
            gg = g_ref[...]
            m2 = ADAM_B1 * m_ref[...] + (1.0 - ADAM_B1) * gg
            v2 = ADAM_B2 * v_ref[...] + (1.0 - ADAM_B2) * (gg * gg)
            mo_ref[...] = m2
            vo_ref[...] = v2
            d_ref[...] = -ADAM_LR * ((m2 / bias1) / (jnp.sqrt(v2 / bias2) + ADAM_EPS) + ADAM_WD * w_ref[...])

    in_specs, out_specs, out_shape, args = [], [], [], []
    for w, g, m, v in items:
        r, c = w.shape
        steps = ADAMW_STEPS if r % (8 * ADAMW_STEPS) == 0 else 1
        assert steps == ADAMW_STEPS or n == 1
        spec = pl.BlockSpec((r // steps, c), lambda i: (i, 0))
        in_specs += [spec] * 4
        out_specs += [spec] * 3
        out_shape += [jax.ShapeDtypeStruct((r, c), F32)] * 3
        args += [w, g, m, v]
    out, got = _call(body, args, name=name, grid=(steps,), in_specs=in_specs, out_specs=out_specs, out_shape=out_shape,
                     exchange=exchange, steps=(0, steps - 1))
    return [tuple(out[3 * k:3 * k + 3]) for k in range(n)], got


def kernel(x, p, ffn1_norm, ffn1_w_in, ffn1_w_out, mix_norm, w_mix_in, gmlp_v_norm, gmlp_w_s, gmlp_b, w_mix_out, ffn2_norm, ffn2_w_in, ffn2_w_out, ple_norm, ple_w_gate, ple_w_proj, final_norm, loss_target, m_ffn1_norm, m_ffn1_w_in, m_ffn1_w_out, m_mix_norm, m_w_mix_in, m_gmlp_v_norm, m_gmlp_w_s, m_gmlp_b, m_w_mix_out, m_ffn2_norm, m_ffn2_w_in, m_ffn2_w_out, m_ple_norm, m_ple_w_gate, m_ple_w_proj, m_final_norm, v_ffn1_norm, v_ffn1_w_in, v_ffn1_w_out, v_mix_norm, v_w_mix_in, v_gmlp_v_norm, v_gmlp_w_s, v_gmlp_b, v_w_mix_out, v_ffn2_norm, v_ffn2_w_in, v_ffn2_w_out, v_ple_norm, v_ple_w_gate, v_ple_w_proj, v_final_norm):
    args = dict(locals())
    w = {n: args[n] for n in _ALL}
    m = {n: args["m_" + n] for n in _ALL}
    v = {n: args["v_" + n] for n in _ALL}
    xi, yi, ci = _mesh_pos()
    pos = jnp.stack([2 * xi + yi, ci]).astype(jnp.int32)
    shard = {n: w[n][0] for n in _BIG}
    cast = {n: shard[n].astype(BF16) for n in _BIG}
    small = {n: (w[n][0] if w[n].ndim > 2 else w[n].reshape(1, -1)) for n in _SMALL}
    bt = small["gmlp_b"].T
    g_small, pair, from_chips = {}, {}, {}

    def pair_reduce(partials, tag):
        names = list(partials)
        blocks = lambda a, n: a.reshape(N_CHIPS, *shard[n].shape)
        got = _pair_exchange([blocks(partials[n][1], n) for n in names], "grad_pair_exchange_" + tag)
        pair.update(zip(names, _pair_sums([blocks(partials[n][0], n) for n in names], got, pos, "pair_sums_" + tag)))
        return names

    w1in, w1out = _run_exchange(_WeightGather([cast["ffn1_w_in"], cast["ffn1_w_out"]]), "gather_ffn1")
    w1out = w1out.reshape(D_FF, D_MODEL)
    (h1, gu1), (wmix, wmo) = _ffn_fwd(x[0], small["ffn1_norm"], w1in, w1out, "ffn1_fwd",
                                      _WeightGather([cast["w_mix_in"], cast["w_mix_out"]]))
    wmo = wmo.reshape(D_MODEL, D_MODEL)
    zg, qkv = _mix_in_fwd(h1, small["mix_norm"], wmix)
    gm = _gmlp_fwd(zg, small["gmlp_v_norm"], small["gmlp_w_s"], bt)
    (mixed, carries), (w2in, w2out, wg, wproj) = _attn_fwd(
        qkv, gm, _WeightGather([cast["ffn2_w_in"], cast["ffn2_w_out"], cast["ple_w_gate"], cast["ple_w_proj"]]))
    w2out = w2out.reshape(D_FF, D_MODEL)
    wg = wg.reshape(D_MODEL, D_MODEL)
    h2 = _matmul_residual(h1, mixed, wmo, "mix_out_fwd")
    (h3, gu2), _ = _ffn_fwd(h2, small["ffn2_norm"], w2in, w2out, "ffn2_fwd")
    loss_part, g_small["final_norm"], g_small["ple_norm"], dh3, dgp, dpp, n4, pb = _head(
        h3, p[0, 0], loss_target[0], small["ple_norm"], small["final_norm"], wg, wproj)

    part = {"ple_w_gate": _wgrad_rows(n4, dgp, N_CHIPS, "wgrad_ple_gate"),
            "ple_w_proj": _wgrad_cols(pb, dpp, N_CHIPS, "wgrad_ple_proj")}
    (dh2, dgu2, n3, act2, dhh3, g_small["ffn2_norm"]), _ = _ffn_bwd(dh3, h2, small["ffn2_norm"], gu2, w2in, w2out,
                                                                   "ffn2_bwd")
    part["ffn2_w_in"] = _wgrad_cols(n3, dgu2, N_CHIPS, "wgrad_ffn2_in")
    part["ffn2_w_out"] = _wgrad_rows(act2, dhh3, 2, "wgrad_ffn2_out")
    dmixed, dh2b = _matmul_nt_cast(dh2, wmo, "mix_out_bwd")
    part["w_mix_out"] = _wgrad_rows(mixed, dh2b, 2, "wgrad_mix_out")
    group = pair_reduce(part, "late")
    dzg, g_small["gmlp_w_s"], dbt, g_small["gmlp_v_norm"] = _gmlp_bwd(zg, dmixed, small["gmlp_v_norm"],
                                                                      small["gmlp_w_s"], bt)
    g_small["gmlp_b"] = dbt.T
    (dq, dk, dv), got = _attn_bwd(qkv, dmixed, carries, _ChipExchange([pair[n] for n in group]))
    from_chips.update(zip(group, got))

    dzmix = jnp.concatenate([dzg, dq, dk, dv], axis=1)
    dh1, n2, g_small["mix_norm"] = _norm_input_bwd(dh2, dzmix, wmix, h1, small["mix_norm"], "mix_in_bwd")
    group = pair_reduce({"w_mix_in": _wgrad_cols(n2, dzmix, N_CHIPS, "wgrad_mix_in")}, "mix")
    (dx, dgu1, n1, act1, dhh1, g_small["ffn1_norm"]), _ = _ffn_bwd(dh1, x[0], small["ffn1_norm"], gu1, w1in, w1out,
                                                                   "ffn1_bwd")

    g_out, got = _wgrad_rows(act1, dhh1, 2, "wgrad_ffn1_out", _ChipExchange([pair[n] for n in group]))
    from_chips.update(zip(group, got))
    pair_reduce({"ffn1_w_out": g_out}, "out")
    g_in, got = _wgrad_cols(n1, dgu1, N_CHIPS, "wgrad_ffn1_in", _ChipExchange([pair["ffn1_w_out"]]))
    from_chips["ffn1_w_out"] = got[0]
    pair_reduce({"ffn1_w_in": g_in}, "in")

    def finish(names, tag, exchange=None):
        halves = _chip_sums([pair[n] for n in names], [from_chips[n] for n in names], pos, "chip_sums_" + tag)
        full = _pair_share(halves, "grad_pair_share_" + tag)
        out, got = _adamw([(shard[n], g, m[n][0], v[n][0]) for n, g in zip(names, full)], "adamw_" + tag, exchange)
        for n, g, (d2, m2, v2) in zip(names, full, out):
            grads[n], delta[n], new_m[n], new_v[n] = g[None], d2[None], m2[None], v2[None]
        return got

    grads, delta, new_m, new_v = {}, {}, {}, {}
    got = finish([n for n in _BIG if n != "ffn1_w_in"], "most", _ChipExchange([pair["ffn1_w_in"]]))
    from_chips["ffn1_w_in"] = got[0]
    finish(["ffn1_w_in"], "last")

    rows = [n for n in _SMALL if g_small[n].shape[0] == 1]
    mats = [n for n in _SMALL if n not in rows]
    summed = _all_reduce_small([g_small[n] for n in rows] + [loss_part], [g_small[n] for n in mats])
    loss = summed[len(rows)][0, 0]
    g_sum = dict(zip(rows + mats, summed[:len(rows)] + summed[len(rows) + 1:]))
    like = lambda a, n: a[n].reshape(small[n].shape)
    out = _adamw_small([(small[n], g_sum[n], like(m, n), like(v, n)) for n in _SMALL])
    for n, (d2, m2, v2) in zip(_SMALL, out):
        grads[n], delta[n], new_m[n], new_v[n] = (a.reshape(w[n].shape) for a in (g_sum[n], d2, m2, v2))

    return (loss, dx[None], *[grads[n] for n in _ALL], *[delta[n] for n in _ALL], *[new_m[n] for n in _ALL],
            *[new_v[n] for n in _ALL])
```

```python
import functools
import math

import jax
import jax.numpy as jnp
from jax import lax
from jax.experimental import pallas as pl
from jax.experimental.pallas import tpu as pltpu

F32, BF16 = jnp.float32, jnp.bfloat16

D_MODEL = 1024
D_FF = 2816
FF_BLOCK = 2 * D_FF // 4
PLE_DIM = 256
CHUNK = 128
GM_HEADS = 4
GM_WIDTH = 512
SB_HEAD_DIM = 64
SB_WIDTH = 512
MIX_IN_WIDTH = 2 * GM_WIDTH + 3 * SB_WIDTH
MIX_BLOCK = MIX_IN_WIDTH // 4
EPS = 1e-6
N_CHIPS = 4
LANES = 128
ATT_BLOCK = 128
ATT_Q = 512
VMEM_LIMIT = 56 * 1024 * 1024

ADAM_LR, ADAM_B1, ADAM_B2, ADAM_EPS, ADAM_WD, ADAM_STEP = 0.001, 0.9, 0.999, 1e-08, 0.01, 10


def _dot(a, b):
    return jnp.dot(a, b, preferred_element_type=F32)


def _dot_nt(a, b):
    return lax.dot_general(a, b, (((1,), (1,)), ((), ())), preferred_element_type=F32)


def _dot_tn(a, b):
    return lax.dot_general(a, b, (((0,), (0,)), ((), ())), preferred_element_type=F32)


def _resident(shape):
    nd = len(shape)
    return pl.BlockSpec(shape, lambda *_: (0,) * nd, pipeline_mode=pl.Buffered(1))


def _rows(tm, width):
    return pl.BlockSpec((tm, width), lambda i: (i, 0))


def _params(n_axes=1):
    return pltpu.CompilerParams(dimension_semantics=("arbitrary",) * n_axes, vmem_limit_bytes=VMEM_LIMIT)


def _rstd(h):
    return lax.rsqrt(jnp.mean(h * h, axis=-1, keepdims=True) + EPS)


def _rms_bwd(dy, h, r, g):
    dyg = dy * g
    dh = r * dyg - h * (r * r * r) * jnp.mean(dyg * h, axis=-1, keepdims=True)
    return dh, dy * h * r


def _gelu(x):
    return 0.5 * x * (1.0 + lax.erf(x * (2.0 ** -0.5)))


def _gelu_grad(x):
    return 0.5 * (1.0 + lax.erf(x * (2.0 ** -0.5))) + x * jnp.exp(-0.5 * x * x) * ((2.0 * jnp.pi) ** -0.5)


def _token_tile(t, wide=False):
    return min(512 if wide else 256, t)


def _ffn_fwd(h, g, win, wout, name, exchange=None):
    t = h.shape[0]
    tm = _token_tile(t)

    def body(h_ref, g_ref, win_ref, wout_ref, ho_ref, gu_ref):
        hh = h_ref[...]
        n = (hh * _rstd(hh) * g_ref[...]).astype(BF16)
        acc = jnp.zeros((tm, D_MODEL), F32)
        for jb in range(2):
            gate = _dot(n, win_ref[jb])
            up = _dot(n, win_ref[2 + jb])
            gu_ref[:, jb * FF_BLOCK:(jb + 1) * FF_BLOCK] = gate.astype(BF16)
            gu_ref[:, D_FF + jb * FF_BLOCK:D_FF + (jb + 1) * FF_BLOCK] = up.astype(BF16)
            act = (gate * jax.nn.sigmoid(gate) * up).astype(BF16)
            acc = acc + _dot(act, wout_ref[jb * FF_BLOCK:(jb + 1) * FF_BLOCK, :])
        ho_ref[...] = hh + 0.5 * acc

    n = t // tm
    return _call(
        body, (h, g, win, wout), name=name, grid=(n,),
        in_specs=[_rows(tm, D_MODEL), _resident((1, D_MODEL)), _resident(win.shape), _resident(wout.shape)],
        out_specs=[_rows(tm, D_MODEL), _rows(tm, 2 * D_FF)],
        out_shape=[jax.ShapeDtypeStruct((t, D_MODEL), F32), jax.ShapeDtypeStruct((t, 2 * D_FF), BF16)],
        exchange=exchange, steps=(0, (2 * n) // 3, n - 1))


def _ffn_bwd(dho, h, g, gu, win, wout, name, exchange=None):
    t = h.shape[0]
    tm = _token_tile(t)

    def body(dho_ref, h_ref, g_ref, gu_ref, win_ref, wout_ref, dh_ref, dgu_ref, n_ref, act_ref, dhh_ref, dg_ref):
        i = pl.program_id(0)
        hh = h_ref[...]
        gg = g_ref[...]
        r = _rstd(hh)
        n_ref[...] = (hh * r * gg).astype(BF16)
        dho = dho_ref[...]
        dhh = (0.5 * dho).astype(BF16)
        dhh_ref[...] = dhh
        dn = jnp.zeros((tm, D_MODEL), F32)
        for jb in range(2):
            cg = slice(jb * FF_BLOCK, (jb + 1) * FF_BLOCK)
            cu = slice(D_FF + jb * FF_BLOCK, D_FF + (jb + 1) * FF_BLOCK)
            dact = _dot_nt(dhh, wout_ref[cg, :])
            gate = gu_ref[:, cg].astype(F32)
            up = gu_ref[:, cu].astype(F32)
            sg = jax.nn.sigmoid(gate)
            silu = gate * sg
            act_ref[:, cg] = (silu * up).astype(BF16)
            dgate = (dact * up * (sg * (1.0 + gate * (1.0 - sg)))).astype(BF16)
            dup = (dact * silu).astype(BF16)
            dgu_ref[:, cg] = dgate
            dgu_ref[:, cu] = dup
            dn = dn + _dot_nt(dgate, win_ref[jb]) + _dot_nt(dup, win_ref[2 + jb])
        dh, dg_rows = _rms_bwd(dn, hh, r, gg)
        dh_ref[...] = dho + dh

        @pl.when(i == 0)
        def _():
            dg_ref[...] = jnp.zeros_like(dg_ref)

        dg_ref[...] += jnp.sum(dg_rows, axis=0, keepdims=True)

    n = t // tm
    return _call(
        body, (dho, h, g, gu, win, wout), name=name, grid=(n,),
        in_specs=[_rows(tm, D_MODEL), _rows(tm, D_MODEL), _resident((1, D_MODEL)), _rows(tm, 2 * D_FF),
                  _resident(win.shape), _resident(wout.shape)],
        out_specs=[_rows(tm, D_MODEL), _rows(tm, 2 * D_FF), _rows(tm, D_MODEL), _rows(tm, D_FF), _rows(tm, D_MODEL),
                   pl.BlockSpec((1, D_MODEL), lambda i: (0, 0))],
        out_shape=[jax.ShapeDtypeStruct((t, D_MODEL), F32), jax.ShapeDtypeStruct((t, 2 * D_FF), BF16),
                   jax.ShapeDtypeStruct((t, D_MODEL), BF16), jax.ShapeDtypeStruct((t, D_FF), BF16),
                   jax.ShapeDtypeStruct((t, D_MODEL), BF16), jax.ShapeDtypeStruct((1, D_MODEL), F32)],
        exchange=exchange, steps=(0, n - 1))


def _wgrad(a, b, out_shape, out_block, out_index, a_width, b_width, grid_ij, name, exchange=None):
    t = a.shape[0]
    tk = t if 4 * t * (a_width + b_width) <= 26 * 2 ** 20 else min(2048, t)
    nk = t // tk

    def body(a_ref, b_ref, o_ref, ob_ref):
        k = pl.program_id(2)
        prod = _dot_tn(a_ref[...], b_ref[...]).reshape(o_ref.shape)

        @pl.when(k == 0)
        def _():
            o_ref[...] = prod

        @pl.when(k > 0)
        def _():
            o_ref[...] += prod

        @pl.when(k == nk - 1)
        def _():
            ob_ref[...] = o_ref[...].astype(BF16)

    grid = (*grid_ij, nk)
    out_spec = pl.BlockSpec(out_block, lambda i, j, k: out_index(i, j))
    out, got = _call(
        body, (a, b), name=name, grid=grid,
        in_specs=[pl.BlockSpec((tk, a_width), lambda i, j, k: (k, i)), pl.BlockSpec((tk, b_width), lambda i, j, k: (k, j))],
        out_specs=[out_spec, out_spec],
        out_shape=[jax.ShapeDtypeStruct(out_shape, F32), jax.ShapeDtypeStruct(out_shape, BF16)],
        exchange=exchange, steps=(0, grid[0] * grid[1] * grid[2] - 1))
    return tuple(out) if exchange is None else (tuple(out), got)


def _wgrad_cols(a, b, n_blocks, name, exchange=None):
    ka, nb = a.shape[1], b.shape[1] // n_blocks
    return _wgrad(a, b, (n_blocks, ka, nb), (1, ka, nb), lambda i, j: (j, 0, 0), ka, nb, (1, n_blocks), name, exchange)


def _wgrad_rows(a, b, n_blocks, name, exchange=None):
    ka, nb = a.shape[1] // n_blocks, b.shape[1]
    return _wgrad(a, b, (a.shape[1], nb), (ka, nb), lambda i, j: (i, 0), ka, nb, (n_blocks, 1), name, exchange)


def _mix_in_fwd(h, g, wmix):
    t = h.shape[0]
    tm = _token_tile(t, wide=True)
    gw2 = 2 * GM_WIDTH

    def body(h_ref, g_ref, w_ref, zg_ref, qkv_ref):
        hh = h_ref[...]
        n = (hh * _rstd(hh) * g_ref[...]).astype(BF16)
        for b in range(N_CHIPS):
            z = _dot(n, w_ref[b])
            lo, hi = b * MIX_BLOCK, (b + 1) * MIX_BLOCK
            if hi <= gw2:
                zg_ref[:, lo:hi] = z
            elif lo >= gw2:
                qkv_ref[:, lo - gw2:hi - gw2] = z.astype(BF16)
            else:
                zg_ref[:, lo:gw2] = z[:, :gw2 - lo]
                qkv_ref[:, 0:hi - gw2] = z[:, gw2 - lo:].astype(BF16)

    return pl.pallas_call(
        body, name="mix_in_fwd", grid=(t // tm,),
        in_specs=[_rows(tm, D_MODEL), _resident((1, D_MODEL)), _resident(wmix.shape)],
        out_specs=[_rows(tm, gw2), _rows(tm, 3 * SB_WIDTH)],
        out_shape=[jax.ShapeDtypeStruct((t, gw2), F32), jax.ShapeDtypeStruct((t, 3 * SB_WIDTH), BF16)],
        compiler_params=_params(),
    )(h, g, wmix)


def _causal_chunk_mask():
    row = lax.broadcasted_iota(jnp.int32, (CHUNK, CHUNK), 0)
    col = lax.broadcasted_iota(jnp.int32, (CHUNK, CHUNK), 1)
    return row >= col


def _gmlp_tile(t):
    return min(512, t)


def _gmlp_fwd(zg, gv, ws, bt):
    t = zg.shape[0]
    tm = _gmlp_tile(t)

    def body(zg_ref, gv_ref, ws_ref, bt_ref, o_ref):
        u = _gelu(zg_ref[:, :GM_WIDTH])
        v = _gelu(zg_ref[:, GM_WIDTH:])
        vn = (v * _rstd(v) * gv_ref[...]).astype(BF16)
        mask = _causal_chunk_mask()
        for hd in range(GM_HEADS):
            wm = jnp.where(mask, ws_ref[hd], 0.0).astype(BF16)
            cols = slice(hd * CHUNK, (hd + 1) * CHUNK)
            for c in range(tm // CHUNK):
                rows = slice(c * CHUNK, (c + 1) * CHUNK)
                sv = _dot(wm, vn[rows, cols]) + bt_ref[:, hd:hd + 1]
                o_ref[rows, cols] = (u[rows, cols] * sv).astype(BF16)

    return pl.pallas_call(
        body, name="gmlp_fwd", grid=(t // tm,),
        in_specs=[_rows(tm, 2 * GM_WIDTH), _resident((1, GM_WIDTH)), _resident(ws.shape), _resident(bt.shape)],
        out_specs=_rows(tm, GM_WIDTH),
        out_shape=jax.ShapeDtypeStruct((t, D_MODEL), BF16),
        compiler_params=_params(),
    )(zg, gv, ws, bt)


def _gmlp_bwd(zg, dmixed, gv, ws, bt, exchange=None):
    t = zg.shape[0]
    tm = _gmlp_tile(t)

    def body(zg_ref, dgm_ref, gv_ref, ws_ref, bt_ref, dzg_ref, dws_ref, dbt_ref, dgv_ref):
        i = pl.program_id(0)

        @pl.when(i == 0)
        def _():
            dws_ref[...] = jnp.zeros_like(dws_ref)
            dbt_ref[...] = jnp.zeros_like(dbt_ref)
            dgv_ref[...] = jnp.zeros_like(dgv_ref)

        zu = zg_ref[:, :GM_WIDTH]
        zv = zg_ref[:, GM_WIDTH:]
        u = _gelu(zu)
        v = _gelu(zv)
        r = _rstd(v)
        gvv = gv_ref[...]
        vn = (v * r * gvv).astype(BF16)
        dgm = dgm_ref[...].astype(F32)
        dsv = (dgm * u).astype(BF16)
        mask = _causal_chunk_mask()
        du_cols, dvn_cols = [], []
        for hd in range(GM_HEADS):
            wm = jnp.where(mask, ws_ref[hd], 0.0).astype(BF16)
            cols = slice(hd * CHUNK, (hd + 1) * CHUNK)
            dw = jnp.zeros((CHUNK, CHUNK), F32)
            db = jnp.zeros((CHUNK, 1), F32)
            du_rows, dvn_rows = [], []
            for c in range(tm // CHUNK):
                rows = slice(c * CHUNK, (c + 1) * CHUNK)
                sv = _dot(wm, vn[rows, cols]) + bt_ref[:, hd:hd + 1]
                du_rows.append(dgm[rows, cols] * sv)
                dvn_rows.append(_dot_tn(wm, dsv[rows, cols]))
                dw = dw + _dot_nt(dsv[rows, cols], vn[rows, cols])
                db = db + jnp.sum(dsv[rows, cols].astype(F32), axis=1, keepdims=True)
            dws_ref[hd] += jnp.where(mask, dw, 0.0)
            dbt_ref[:, hd:hd + 1] += db
            du_cols.append(jnp.concatenate(du_rows, axis=0))
            dvn_cols.append(jnp.concatenate(dvn_rows, axis=0))
        du = jnp.concatenate(du_cols, axis=1)
        dvn = jnp.concatenate(dvn_cols, axis=1)
        dv, dgv_rows = _rms_bwd(dvn, v, r, gvv)
        dgv_ref[...] += jnp.sum(dgv_rows, axis=0, keepdims=True)
        dzg_ref[:, :GM_WIDTH] = (du * _gelu_grad(zu)).astype(BF16)
        dzg_ref[:, GM_WIDTH:] = (dv * _gelu_grad(zv)).astype(BF16)

    const = lambda nd: (lambda i: (0,) * nd)
    n = t // tm
    return _call(
        body, (zg, dmixed, gv, ws, bt), name="gmlp_bwd", grid=(n,),
        in_specs=[_rows(tm, 2 * GM_WIDTH), _rows(tm, GM_WIDTH), _resident((1, GM_WIDTH)), _resident(ws.shape),
                  _resident(bt.shape)],
        out_specs=[_rows(tm, 2 * GM_WIDTH), pl.BlockSpec(ws.shape, const(3)), pl.BlockSpec(bt.shape, const(2)),
                   pl.BlockSpec((1, GM_WIDTH), const(2))],
        out_shape=[jax.ShapeDtypeStruct((t, 2 * GM_WIDTH), BF16), jax.ShapeDtypeStruct(ws.shape, F32),
                   jax.ShapeDtypeStruct(bt.shape, F32), jax.ShapeDtypeStruct((1, GM_WIDTH), F32)],
        exchange=exchange, steps=(0, n - 1))


def _att_masks():
    tb = ATT_BLOCK
    lane = lax.broadcasted_iota(jnp.int32, (1, LANES), 1)
    rj = lax.broadcasted_iota(jnp.int32, (2 * tb, 2 * tb), 0)
    cs = lax.broadcasted_iota(jnp.int32, (2 * tb, 2 * tb), 1)
    same_head = ((rj < tb) & (cs < tb)) | ((rj >= tb) & (cs >= tb))
    suffix = jnp.where(same_head & (rj >= cs), 1.0, 0.0).astype(BF16)
    prefix = jnp.where(same_head & (rj <= cs), 1.0, 0.0).astype(BF16)
    left = lax.broadcasted_iota(jnp.int32, (1, 2 * tb), 1) < tb
    tq = lax.broadcasted_iota(jnp.int32, (ATT_Q, 4 * tb), 0)
    ts = lax.broadcasted_iota(jnp.int32, (ATT_Q, 4 * tb), 1)
    key = jnp.where(ts < 2 * tb, ts & (tb - 1), (ts & (tb - 1)) + tb)
    return lane, suffix, prefix, left, key, tq


def _att_fill(k_ref, v_ref, kcat, vcat, n_blocks, lane):
    tb = ATT_BLOCK
    first = lane < SB_HEAD_DIM

    def fill(jb, carry):
        rows = pl.ds(pl.multiple_of(jb * tb, tb), tb)
        top = pl.ds(pl.multiple_of(jb * 2 * tb, tb), tb)
        bot = pl.ds(pl.multiple_of(jb * 2 * tb + tb, tb), tb)
        kb = k_ref[rows, :]
        vb = v_ref[rows, :]
        zero = jnp.zeros_like(kb)
        kcat[top, :] = jnp.where(first, kb, zero)
        kcat[bot, :] = jnp.where(first, zero, kb)
        vcat[top, :] = jnp.where(first, vb, zero)
        vcat[bot, :] = jnp.where(first, zero, vb)
        return carry

    lax.fori_loop(0, n_blocks, fill, 0)


def _block_sums(x, m):
    return _dot(x.astype(BF16), m)


def _softplus2(z2):
    return jnp.maximum(z2, 0.0) + jnp.log2(1.0 + jnp.exp2(-jnp.abs(z2)))


def _scaled_queries(q_ref, base2):
    scale = SB_HEAD_DIM ** -0.5
    return (q_ref[...].astype(F32) * (scale * math.log2(math.e) if base2 else scale)).astype(BF16)


def _att_specs(t):
    n_pairs = SB_WIDTH // LANES
    q_spec = pl.BlockSpec((ATT_Q, LANES), lambda p, i: (i, p))
    k_spec = pl.BlockSpec((t, LANES), lambda p, i: (0, n_pairs + p))
    v_spec = pl.BlockSpec((t, LANES), lambda p, i: (0, 2 * n_pairs + p))
    return n_pairs, q_spec, k_spec, v_spec


def _attn_fwd(qkv, mixed, exchange=None):
    t = qkv.shape[0]
    tb = ATT_BLOCK
    nkb = t // tb
    assert 2 * nkb <= LANES and t % ATT_Q == 0 and ATT_Q == 4 * tb
    n_pairs, q_spec, k_spec, v_spec = _att_specs(t)

    def body(q_ref, k_ref, v_ref, mixed_ref, o_ref, ct_ref, kcat, vcat, acc, carry, z0, r0, z1, r1):
        i = pl.program_id(1)
        lane, suffix, _, left, key, tq = _att_masks()

        @pl.when(i == 0)
        def _():
            _att_fill(k_ref, v_ref, kcat, vcat, nkb, lane)

        q = _scaled_queries(q_ref, True)
        acc[...] = jnp.zeros_like(acc)
        carry[...] = jnp.zeros_like(carry)
        ct_ref[0] = jnp.zeros((ATT_Q, LANES), F32)

        def key_rows(m):
            return pl.ds(pl.multiple_of(m * 4 * tb, 4 * tb), 4 * tb)

        def scores(m, zb, rb, causal=None, rs=slice(None)):
            z = _dot_nt(q[rs], kcat[key_rows(m), :])
            zb[rs, :] = z
            sp = _softplus2(z)
            if causal is not None:
                sp = jnp.where(causal[rs], sp, 0.0)
            for g in (1, 0):
                cols = slice(g * 2 * tb, (g + 1) * 2 * tb)
                rb[rs, cols] = _block_sums(sp[:, cols], suffix)

        def weigh(m, zb, rb, causal=None, rs=slice(None)):
            probs = [None, None]
            for g in (1, 0):
                cols = slice(g * 2 * tb, (g + 1) * 2 * tb)
                j = 2 * m + g
                r = rb[rs, cols]
                c = carry[rs, :]
                ct_ref[0, rs, :] = jnp.where(lane == j, c[:, :tb], jnp.where(lane == nkb + j, c[:, tb:], ct_ref[0, rs, :]))
                a = jnp.exp2(zb[rs, cols] - (r + c))
                if causal is not None:
                    a = jnp.where(causal[rs, cols], a, 0.0)
                probs[g] = a.astype(BF16)
                carry[rs, :] = c + jnp.where(left, r[:, 0:1], r[:, tb:tb + 1])
            acc[rs, :] += _dot(jnp.concatenate(probs, axis=1), vcat[key_rows(m), :])

        sooner, later, late_rows = key < tq, key + 2 * tb < tq, slice(2 * tb, 4 * tb)
        scores(2 * i + 1, z1, r1, later, late_rows)
        scores(2 * i, z0, r0, sooner)
        weigh(2 * i + 1, z1, r1, later, late_rows)
        weigh(2 * i, z0, r0, sooner)

        @pl.when(i > 0)
        def _():
            scores(2 * i - 1, z1, r1)

            def loop(k, c):
                u = i - 1 - k
                scores(2 * u, z0, r0)
                weigh(2 * u + 1, z1, r1)
                scores(2 * u - 1, z1, r1)
                weigh(2 * u, z0, r0)
                return c

            lax.fori_loop(0, i - 1, loop, 0)
            scores(0, z0, r0)
            weigh(1, z1, r1)
            weigh(0, z0, r0)

        o_ref[...] = acc[...].astype(BF16)

    tile = pltpu.VMEM((ATT_Q, 4 * tb), F32)

    nq = t // ATT_Q
    return _call(
        body, (qkv, qkv, qkv, mixed), name="attn_fwd", grid=(n_pairs, nq),
        in_specs=[q_spec, k_spec, v_spec, _ANY],
        out_specs=[pl.BlockSpec((ATT_Q, LANES), lambda p, i: (i, GM_WIDTH // LANES + p)),
                   pl.BlockSpec((1, ATT_Q, LANES), lambda p, i: (p, i, 0))],
        out_shape=[jax.ShapeDtypeStruct(mixed.shape, BF16), jax.ShapeDtypeStruct((n_pairs, t, LANES), F32)],
        aliases={3: 0},
        scratch_shapes=[pltpu.VMEM((2 * t, LANES), BF16), pltpu.VMEM((2 * t, LANES), BF16),
                        pltpu.VMEM((ATT_Q, LANES), F32), pltpu.VMEM((ATT_Q, 2 * tb), F32), tile, tile, tile, tile],
        exchange=exchange, steps=(0, (n_pairs - 1) * nq - 1, n_pairs * nq - 1))


def _attn_bwd(qkv, dmixed, carries, exchange=None):
    t = qkv.shape[0]
    tb = ATT_BLOCK
    nkb = t // tb
    nq = t // ATT_Q
    scale = SB_HEAD_DIM ** -0.5
    n_pairs, q_spec, k_spec, v_spec = _att_specs(t)
    gm_blocks = GM_WIDTH // LANES

    def body(q_ref, k_ref, v_ref, do_ref, ct_ref, dq_ref, dk_ref, dv_ref, kcat, vcat, dkacc, dvacc, dqacc, carry,
             z0, r0, s0, a0, z1, r1, s1, a1):
        i = pl.program_id(1)
        lane, suffix, prefix, left, key, tq = _att_masks()
        first = lane < SB_HEAD_DIM

        @pl.when(i == 0)
        def _():
            _att_fill(k_ref, v_ref, kcat, vcat, nkb, lane)
            dkacc[...] = jnp.zeros_like(dkacc)
            dvacc[...] = jnp.zeros_like(dvacc)

        q2 = _scaled_queries(q_ref, True)
        q = _scaled_queries(q_ref, False)
        do = do_ref[...]
        dqacc[...] = jnp.zeros_like(dqacc)
        carry[...] = jnp.zeros_like(carry)

        def key_rows(m):
            return pl.ds(pl.multiple_of(m * 4 * tb, 4 * tb), 4 * tb)

        def front(m, bufs, causal=None, rs=slice(None)):
            zb, rb, sb, ab = bufs
            z = _dot_nt(q2[rs], kcat[key_rows(m), :])
            zb[rs, :] = z
            sp = _softplus2(z)
            sb[rs, :] = jnp.exp2(z - sp)
            if causal is not None:
                sp = jnp.where(causal[rs], sp, 0.0)
            for g in (0, 1):
                cols = slice(g * 2 * tb, (g + 1) * 2 * tb)
                rb[rs, cols] = _block_sums(sp[:, cols], suffix)
            ab[rs, :] = _dot_nt(do[rs], vcat[key_rows(m), :])

        def back(m, bufs, causal=None, rs=slice(None)):
            zb, rb, sb, ab = bufs
            dzs, probs = [None, None], [None, None]
            for g in (0, 1):
                cols = slice(g * 2 * tb, (g + 1) * 2 * tb)
                j = 2 * m + g
                ct = ct_ref[0, rs, :]
                ca = jnp.sum(jnp.where(lane == j, ct, 0.0), axis=1, keepdims=True)
                cb = jnp.sum(jnp.where(lane == nkb + j, ct, 0.0), axis=1, keepdims=True)
                a = jnp.exp2(zb[rs, cols] - (rb[rs, cols] + jnp.where(left, ca, cb)))
                if causal is not None:
                    a = jnp.where(causal[rs, cols], a, 0.0)
                de = ab[rs, cols] * a
                cl = _block_sums(de, prefix)
                pre = carry[rs, :]
                dz = de - sb[rs, cols] * (cl + pre)
                if causal is not None:
                    dz = jnp.where(causal[rs, cols], dz, 0.0)
                carry[rs, :] = pre + jnp.where(left, cl[:, tb - 1:tb], cl[:, 2 * tb - 1:2 * tb])
                dzs[g] = dz.astype(BF16)
                probs[g] = a.astype(BF16)
            dzb = jnp.concatenate(dzs, axis=1)
            dqacc[rs, :] += _dot(dzb, kcat[key_rows(m), :])
            dkc = _dot_tn(dzb, q[rs])
            dvc = _dot_tn(jnp.concatenate(probs, axis=1), do[rs])
            out_rows = pl.ds(pl.multiple_of(m * 2 * tb, 2 * tb), 2 * tb)
            pick = lambda x: jnp.concatenate([jnp.where(first, x[0:tb], x[tb:2 * tb]),
                                              jnp.where(first, x[2 * tb:3 * tb], x[3 * tb:4 * tb])], axis=0)
            dkacc[out_rows, :] += pick(dkc)
            dvacc[out_rows, :] += pick(dvc)

        b0, b1 = (z0, r0, s0, a0), (z1, r1, s1, a1)

        @pl.when(i > 0)
        def _():
            front(0, b0)

            def loop(u, c):
                front(2 * u + 1, b1)
                back(2 * u, b0)
                front(2 * u + 2, b0)
                back(2 * u + 1, b1)
                return c

            lax.fori_loop(0, i - 1, loop, 0)
            front(2 * i - 1, b1)
            back(2 * i - 2, b0)
            back(2 * i - 1, b1)

        sooner, later, late_rows = key < tq, key + 2 * tb < tq, slice(2 * tb, 4 * tb)
        front(2 * i, b0, sooner)
        front(2 * i + 1, b1, later, late_rows)
        back(2 * i, b0, sooner)
        back(2 * i + 1, b1, later, late_rows)

        dq_ref[...] = (dqacc[...] * scale).astype(BF16)

        @pl.when(i == nq - 1)
        def _():
            dk_ref[...] = dkacc[...].astype(BF16)
            dv_ref[...] = dvacc[...].astype(BF16)

    col = pl.BlockSpec((t, LANES), lambda p, i: (0, p))
    out = jax.ShapeDtypeStruct((t, SB_WIDTH), BF16)
    tile = pltpu.VMEM((ATT_Q, 4 * tb), F32)
    return _call(
        body, (qkv, qkv, qkv, dmixed, carries), name="attn_bwd", grid=(n_pairs, nq),
        in_specs=[q_spec, k_spec, v_spec, pl.BlockSpec((ATT_Q, LANES), lambda p, i: (i, gm_blocks + p)),
                  pl.BlockSpec((1, ATT_Q, LANES), lambda p, i: (p, i, 0))],
        out_specs=[pl.BlockSpec((ATT_Q, LANES), lambda p, i: (i, p)), col, col],
        out_shape=[out, out, out],
        scratch_shapes=[pltpu.VMEM((2 * t, LANES), BF16), pltpu.VMEM((2 * t, LANES), BF16),
                        pltpu.VMEM((t, LANES), F32), pltpu.VMEM((t, LANES), F32),
                        pltpu.VMEM((ATT_Q, LANES), F32), pltpu.VMEM((ATT_Q, 2 * tb), F32)] + [tile] * 8,
        exchange=exchange, steps=(0, n_pairs * nq - 1))


def _matmul_residual(res, a, w, name):
    t = a.shape[0]
    tm = _token_tile(t, wide=True)

    def body(res_ref, a_ref, w_ref, o_ref):
        o_ref[...] = res_ref[...] + _dot(a_ref[...], w_ref[...])

    return pl.pallas_call(
        body, name=name, grid=(t // tm,),
        in_specs=[_rows(tm, res.shape[1]), _rows(tm, a.shape[1]), _resident(w.shape)],
        out_specs=_rows(tm, res.shape[1]),
        out_shape=jax.ShapeDtypeStruct(res.shape, F32),
        compiler_params=_params(),
    )(res, a, w)


def _matmul_nt_cast(dy, w, name):
    t = dy.shape[0]
    tm = _token_tile(t, wide=True)

    def body(dy_ref, w_ref, o_ref, dyb_ref):
        dyb = dy_ref[...].astype(BF16)
        dyb_ref[...] = dyb
        o_ref[...] = _dot_nt(dyb, w_ref[...]).astype(BF16)

    return pl.pallas_call(
        body, name=name, grid=(t // tm,),
        in_specs=[_rows(tm, dy.shape[1]), _resident(w.shape)],
        out_specs=[_rows(tm, w.shape[0]), _rows(tm, dy.shape[1])],
        out_shape=[jax.ShapeDtypeStruct((t, w.shape[0]), BF16), jax.ShapeDtypeStruct(dy.shape, BF16)],
        compiler_params=_params(),
    )(dy, w)


def _norm_input_bwd(dres, dz, w, h, g, name):
    t = h.shape[0]
    tm = _token_tile(t, wide=True)
    nb, _, width = w.shape

    def body(dres_ref, dz_ref, w_ref, h_ref, g_ref, dh_ref, n_ref, dg_ref):
        i = pl.program_id(0)
        hh = h_ref[...]
        gg = g_ref[...]
        r = _rstd(hh)
        n_ref[...] = (hh * r * gg).astype(BF16)
        dn = jnp.zeros((tm, D_MODEL), F32)
        for b in range(nb):
            dn = dn + _dot_nt(dz_ref[:, b * width:(b + 1) * width], w_ref[b])
        dh, dg_rows = _rms_bwd(dn, hh, r, gg)
        dh_ref[...] = dres_ref[...] + dh

        @pl.when(i == 0)
        def _():
            dg_ref[...] = jnp.zeros_like(dg_ref)

        dg_ref[...] += jnp.sum(dg_rows, axis=0, keepdims=True)

    return pl.pallas_call(
        body, name=name, grid=(t // tm,),
        in_specs=[_rows(tm, D_MODEL), _rows(tm, nb * width), _resident(w.shape), _rows(tm, D_MODEL),
                  _resident((1, D_MODEL))],
        out_specs=[_rows(tm, D_MODEL), _rows(tm, D_MODEL), pl.BlockSpec((1, D_MODEL), lambda i: (0, 0))],
        out_shape=[jax.ShapeDtypeStruct((t, D_MODEL), F32), jax.ShapeDtypeStruct((t, D_MODEL), BF16),
                   jax.ShapeDtypeStruct((1, D_MODEL), F32)],
        compiler_params=_params(),
    )(dres, dz, w, h, g)


def _head(h, p, target, gple, gfin, wg, wproj):
    t = h.shape[0]
    tm = _token_tile(t)
    pw = D_MODEL // N_CHIPS

    def body(h_ref, p_ref, tgt_ref, gple_ref, gfin_ref, wg_ref, wproj_ref,
             loss_ref, dgf_ref, dgple_ref, dh_ref, dgp_ref, dpp_ref, n_ref, pb_ref):
        i = pl.program_id(0)
        hh = h_ref[...]
        r_in = _rstd(hh)
        gp = gple_ref[...]
        n = (hh * r_in * gp).astype(BF16)
        n_ref[...] = n
        gate = jax.nn.sigmoid(_dot(n, wg_ref[...]))
        pb = p_ref[...].astype(BF16)
        pb_ref[...] = pb
        pp = jnp.concatenate([_dot(pb, wproj_ref[b]) for b in range(N_CHIPS)], axis=1)
        h4 = hh + gate * pp
        r = _rstd(h4)
        gf = gfin_ref[...]
        err = h4 * r * gf - tgt_ref[...]
        dy = err * (1.0 / D_MODEL)
        dh4, dgf_rows = _rms_bwd(dy, h4, r, gf)
        dgp = (dh4 * pp * gate * (1.0 - gate)).astype(BF16)
        dgp_ref[...] = dgp
        dpp_ref[...] = (dh4 * gate).astype(BF16)
        dh, dgple_rows = _rms_bwd(_dot_nt(dgp, wg_ref[...]), hh, r_in, gp)
        dh_ref[...] = dh4 + dh

        @pl.when(i == 0)
        def _():
            loss_ref[...] = jnp.zeros_like(loss_ref)
            dgf_ref[...] = jnp.zeros_like(dgf_ref)
            dgple_ref[...] = jnp.zeros_like(dgple_ref)

        loss_ref[...] += (0.5 / D_MODEL) * jnp.sum(err * err)
        dgf_ref[...] += jnp.sum(dgf_rows, axis=0, keepdims=True)
        dgple_ref[...] += jnp.sum(dgple_rows, axis=0, keepdims=True)

    bf = lambda w: jax.ShapeDtypeStruct((t, w), BF16)
    const = lambda i: (0, 0)
    return pl.pallas_call(
        body, name="head", grid=(t // tm,),
        in_specs=[_rows(tm, D_MODEL), _rows(tm, PLE_DIM), _rows(tm, D_MODEL), _resident((1, D_MODEL)),
                  _resident((1, D_MODEL)), _resident(wg.shape), _resident(wproj.shape)],
        out_specs=[pl.BlockSpec((1, LANES), const), pl.BlockSpec((1, D_MODEL), const), pl.BlockSpec((1, D_MODEL), const),
                   _rows(tm, D_MODEL), _rows(tm, D_MODEL), _rows(tm, D_MODEL), _rows(tm, D_MODEL), _rows(tm, PLE_DIM)],
        out_shape=[jax.ShapeDtypeStruct((1, LANES), F32), jax.ShapeDtypeStruct((1, D_MODEL), F32),
                   jax.ShapeDtypeStruct((1, D_MODEL), F32), jax.ShapeDtypeStruct((t, D_MODEL), F32), bf(D_MODEL),
                   bf(D_MODEL), bf(D_MODEL), bf(PLE_DIM)],
        compiler_params=_params(),
    )(h, p, target, gple, gfin, wg, wproj)


_BIG = ("ffn1_w_in", "ffn1_w_out", "w_mix_in", "w_mix_out", "ffn2_w_in", "ffn2_w_out", "ple_w_gate", "ple_w_proj")
_SMALL = ("ffn1_norm", "mix_norm", "gmlp_v_norm", "gmlp_w_s", "gmlp_b", "ffn2_norm", "ple_norm", "final_norm")
_ALL = ("ffn1_norm", "ffn1_w_in", "ffn1_w_out", "mix_norm", "w_mix_in", "gmlp_v_norm", "gmlp_w_s", "gmlp_b", "w_mix_out",
        "ffn2_norm", "ffn2_w_in", "ffn2_w_out", "ple_norm", "ple_w_gate", "ple_w_proj", "final_norm")
_ANY = pl.BlockSpec(memory_space=pl.ANY)
_MESH = pl.DeviceIdType.MESH


def _mesh_pos():
    return lax.axis_index("x"), lax.axis_index("y"), lax.axis_index("c")


def _other_chips(x, y):
    return [((x, 1 - y), 2 * x + 1 - y), ((1 - x, y), 2 * (1 - x) + y), ((1 - x, 1 - y), 2 * (1 - x) + 1 - y)]


def _remote(src, dst, send_sem, recv_sem, device):
    return pltpu.make_async_remote_copy(src_ref=src, dst_ref=dst, send_sem=send_sem, recv_sem=recv_sem,
                                        device_id=device, device_id_type=_MESH)


class _WeightGather:
    def __init__(self, shards):
        self.shapes = [s.shape for s in shards]
        self.operands = list(shards)
        self.out_shape = [jax.ShapeDtypeStruct((N_CHIPS, *s.shape), s.dtype) for s in shards]
        n = len(shards)
        self.per = 2 * (N_CHIPS - 1)
        self.scratch = [pltpu.SemaphoreType.DMA((self.per * n,)), pltpu.SemaphoreType.DMA((self.per * n,)),
                        pltpu.SemaphoreType.DMA((n,))]
        self.phases = [self.send, self.forward, self.finish]

    def _copies(self, ins, outs, sems):
        send_sems, recv_sems, local_sems = sems
        x, y, c = _mesh_pos()
        sibling = (x, y, 1 - c)
        mine = 2 * x + y
        local, first, landing, passed, arriving = [], [], [], [], []
        for w, shape in enumerate(self.shapes):
            hr = shape[0] // 2
            half = lambda blk, cc, w=w, hr=hr: outs[w].at[blk, pl.ds(cc * hr, hr), :]
            local.append(pltpu.make_async_copy(ins[w], outs[w].at[mine], local_sems.at[w]))
            for k, (chip, blk) in enumerate(_other_chips(x, y)):
                s = self.per * w + k
                first.append(_remote(ins[w].at[pl.ds(c * hr, hr), :], half(mine, c), send_sems.at[s], recv_sems.at[s],
                                     (*chip, c)))
                landing.append(_remote(half(blk, c), half(blk, c), send_sems.at[s], recv_sems.at[s], sibling))
                s = self.per * w + N_CHIPS - 1 + k
                passed.append(_remote(half(blk, c), half(blk, c), send_sems.at[s], recv_sems.at[s], sibling))
                arriving.append(_remote(half(blk, 1 - c), half(blk, 1 - c), send_sems.at[s], recv_sems.at[s], sibling))
        return local, first, landing, passed, arriving

    def send(self, ins, outs, sems):
        local, first, _, _, _ = self._copies(ins, outs, sems)
        for cp in local + first:
            cp.start()

    def forward(self, ins, outs, sems):
        _, _, landing, passed, _ = self._copies(ins, outs, sems)
        for landed, cp in zip(landing, passed):
            landed.wait_recv()
            cp.start()

    def finish(self, ins, outs, sems):
        local, first, _, passed, arriving = self._copies(ins, outs, sems)
        for cp in arriving:
            cp.wait_recv()
        for cp in first + passed:
            cp.wait_send()
        for cp in local:
            cp.wait()


class _ChipExchange:
    def __init__(self, sums):
        n = len(sums)
        self.n = n
        self.per = N_CHIPS - 1
        self.operands = list(sums)
        self.out_shape = [jax.ShapeDtypeStruct((self.per, *s.shape[1:]), s.dtype) for s in sums]
        self.scratch = [pltpu.SemaphoreType.DMA((self.per * n,)), pltpu.SemaphoreType.DMA((self.per * n,))]
        self.phases = [self.send, self.finish]

    def _copies(self, ins, outs, sems):
        send_sems, recv_sems = sems
        x, y, c = _mesh_pos()
        cps = []
        for w in range(self.n):
            for k, (chip, _) in enumerate(_other_chips(x, y)):
                s = self.per * w + k
                cps.append(_remote(ins[w].at[k + 1], outs[w].at[k], send_sems.at[s], recv_sems.at[s], (*chip, c)))
        return cps

    def send(self, ins, outs, sems):
        for cp in self._copies(ins, outs, sems):
            cp.start()

    def finish(self, ins, outs, sems):
        for cp in self._copies(ins, outs, sems):
            cp.wait()


def _run_exchange(ex, name):
    n_in, n_out = len(ex.operands), len(ex.out_shape)

    def body(*refs):
        ins, outs, sems = refs[:n_in], refs[n_in:n_in + n_out], refs[n_in + n_out:]
        for phase in ex.phases:
            phase(ins, outs, sems)

    return pl.pallas_call(body, name=name, in_specs=[_ANY] * n_in, out_specs=[_ANY] * n_out, out_shape=ex.out_shape,
                          scratch_shapes=ex.scratch)(*ex.operands)


def _call(body, args, *, name, grid, in_specs, out_specs, out_shape, scratch_shapes=(), exchange=None, steps=None,
          aliases=None):
    params = _params(len(grid))
    if exchange is None:
        out = pl.pallas_call(body, name=name, grid=grid, in_specs=in_specs, out_specs=out_specs, out_shape=out_shape,
                             scratch_shapes=list(scratch_shapes), input_output_aliases=aliases or {},
                             compiler_params=params)(*args)
        return out, None
    n_in, n_out, n_scr = len(in_specs), len(out_specs), len(scratch_shapes)
    n_xin, n_xout = len(exchange.operands), len(exchange.out_shape)
    assert len(steps) == len(exchange.phases)

    def hosting(*refs):
        cuts = [n_in, n_xin, n_out, n_xout, n_scr]
        parts, at = [], 0
        for size in cuts:
            parts.append(refs[at:at + size])
            at += size
        ins, xins, outs, xouts, scr = parts
        sems = refs[at:]
        step = 0
        for axis, size in enumerate(grid):
            step = step * size + pl.program_id(axis)
        pl.when(step == steps[0])(lambda: exchange.phases[0](xins, xouts, sems))
        body(*ins, *outs, *scr)
        for at_step, phase in zip(steps[1:], exchange.phases[1:]):
            pl.when(step == at_step)(functools.partial(phase, xins, xouts, sems))

    out = pl.pallas_call(
        hosting, name=name, grid=grid,
        in_specs=list(in_specs) + [_ANY] * n_xin, out_specs=list(out_specs) + [_ANY] * n_xout,
        out_shape=list(out_shape) + list(exchange.out_shape),
        scratch_shapes=list(scratch_shapes) + list(exchange.scratch), input_output_aliases=aliases or {},
        compiler_params=params,
    )(*args, *exchange.operands)
    return out[:n_out], out[n_out:]


class _PairExchange:
    def __init__(self, parts):
        self.halves = [g.shape[1] // 2 for g in parts]
        self.operands = list(parts)
        self.out_shape = [jax.ShapeDtypeStruct((g.shape[0], g.shape[1] // 2, g.shape[2]), g.dtype) for g in parts]
        n = len(parts)
        self.scratch = [pltpu.SemaphoreType.DMA((n,)), pltpu.SemaphoreType.DMA((n,))]
        self.phases = [self.send, self.finish]

    def _copies(self, ins, outs, sems):
        send_sems, recv_sems = sems
        x, y, c = _mesh_pos()
        return [_remote(ins[w].at[:, pl.ds((1 - c) * hr, hr), :], outs[w], send_sems.at[w], recv_sems.at[w], (x, y, 1 - c))
                for w, hr in enumerate(self.halves)]

    def send(self, ins, outs, sems):
        for cp in self._copies(ins, outs, sems):
            cp.start()

    def finish(self, ins, outs, sems):
        for cp in self._copies(ins, outs, sems):
            cp.wait()


SUM_STEPS = 2


def _pair_sums(gs, sibling, pos, name):
    n = len(gs)

    def body(pos_ref, *refs):
        for k in range(n):
            refs[2 * n + k][...] = (refs[k][...] + refs[n + k][...].astype(F32)).astype(BF16)

    g_specs, a_specs, o_specs, out_shape = [], [], [], []
    for g in gs:
        nb, r, c = g.shape
        tr = r // 2 // SUM_STEPS
        g_specs.append(pl.BlockSpec((1, tr, c), lambda k, i, pos: (k ^ pos[0], pos[1] * SUM_STEPS + i, 0)))
        a_specs.append(pl.BlockSpec((1, tr, c), lambda k, i, pos: (k ^ pos[0], i, 0)))
        o_specs.append(pl.BlockSpec((1, tr, c), lambda k, i, pos: (k, i, 0)))
        out_shape.append(jax.ShapeDtypeStruct((nb, r // 2, c), BF16))
    return pl.pallas_call(
        body, name=name,
        grid_spec=pltpu.PrefetchScalarGridSpec(num_scalar_prefetch=1, grid=(N_CHIPS, SUM_STEPS),
                                               in_specs=g_specs + a_specs, out_specs=o_specs),
        out_shape=out_shape, compiler_params=_params(2),
    )(pos, *gs, *sibling)


def _chip_sums(pairs, others, pos, name):
    n = len(pairs)

    def body(pos_ref, *refs):
        for k in range(n):
            s_ref, b_ref = refs[k], refs[n + k]
            refs[2 * n + k][...] = ((s_ref[0].astype(F32) + b_ref[0].astype(F32))
                                    + (b_ref[1].astype(F32) + b_ref[2].astype(F32)))

    s_specs, b_specs, o_specs, out_shape = [], [], [], []
    for s in pairs:
        _, hr, c = s.shape
        tr = hr // SUM_STEPS
        s_specs.append(pl.BlockSpec((1, tr, c), lambda i, pos: (0, i, 0)))
        b_specs.append(pl.BlockSpec((N_CHIPS - 1, tr, c), lambda i, pos: (0, i, 0)))
        o_specs.append(pl.BlockSpec((tr, c), lambda i, pos: (pos[1] * SUM_STEPS + i, 0)))
        out_shape.append(jax.ShapeDtypeStruct((2 * hr, c), F32))
    return pl.pallas_call(
        body, name=name,
        grid_spec=pltpu.PrefetchScalarGridSpec(num_scalar_prefetch=1, grid=(SUM_STEPS,),
                                               in_specs=s_specs + b_specs, out_specs=o_specs),
        out_shape=out_shape, compiler_params=_params(),
    )(pos, *pairs, *others)


def _pair_share(grads, name):
    n = len(grads)

    def body(*refs):
        outs = refs[n:2 * n]
        send_sems, recv_sems = refs[2 * n:]
        x, y, c = _mesh_pos()
        cps = []
        for w in range(n):
            hr = grads[w].shape[0] // 2
            rows = outs[w].at[pl.ds(c * hr, hr), :]
            cp = _remote(rows, rows, send_sems.at[w], recv_sems.at[w], (x, y, 1 - c))
            cp.start()
            cps.append(cp)
        for w, cp in enumerate(cps):
            cp.wait_send()
            hr = grads[w].shape[0] // 2
            other = outs[w].at[pl.ds((1 - c) * hr, hr), :]
            _remote(other, other, send_sems.at[w], recv_sems.at[w], (x, y, 1 - c)).wait_recv()

    return pl.pallas_call(
        body, name=name,
        in_specs=[_ANY] * n, out_specs=[_ANY] * n,
        out_shape=[jax.ShapeDtypeStruct(g.shape, g.dtype) for g in grads],
        input_output_aliases={w: w for w in range(n)},
        scratch_shapes=[pltpu.SemaphoreType.DMA((n,)), pltpu.SemaphoreType.DMA((n,))],
    )(*grads)


def _all_reduce_small(rows, mats):
    n_sems = 2 * N_CHIPS
    n_rows = -(-len(rows) // 8) * 8
    heights = [math.prod(a.shape[:-1]) for a in mats]
    n_tall = -(-sum(heights) // 8) * 8
    arrays = list(rows) + list(mats)

    def body(*refs):
        ins, outs = refs[:len(arrays)], refs[len(arrays):2 * len(arrays)]
        wide, tall, wide_pair, tall_pair, wide_slots, tall_slots, send_sems, recv_sems = refs[2 * len(arrays):]
        x, y, c = _mesh_pos()
        wide[...] = jnp.zeros_like(wide)
        tall[...] = jnp.zeros_like(tall)
        for k, a in enumerate(rows):
            wide[k:k + 1, 0:a.shape[1]] = ins[k][...]
        at = 0
        for k, h in enumerate(heights):
            tall[at:at + h, :] = ins[len(rows) + k][...].reshape(h, LANES)
            at += h
        wide_pair[c] = wide[...]
        tall_pair[c] = tall[...]
        cps = [_remote(wide, wide_pair.at[c], send_sems.at[0], recv_sems.at[0], (x, y, 1 - c)),
               _remote(tall, tall_pair.at[c], send_sems.at[1], recv_sems.at[1], (x, y, 1 - c))]
        for cp in cps:
            cp.start()
        for cp in cps:
            cp.wait()
        wide[...] = wide_pair[0] + wide_pair[1]
        tall[...] = tall_pair[0] + tall_pair[1]
        mine = 2 * x + y
        wide_slots[mine] = wide[...]
        tall_slots[mine] = tall[...]
        cps = []
        for k, (chip, _) in enumerate(_other_chips(x, y)):
            for j, (buf, slots) in enumerate(((wide, wide_slots), (tall, tall_slots))):
                s = 2 + 2 * k + j
                cps.append(_remote(buf, slots.at[mine], send_sems.at[s], recv_sems.at[s], (*chip, c)))
        for cp in cps:
            cp.start()
        for cp in cps:
            cp.wait()
        wide_sum, tall_sum = wide_slots[0], tall_slots[0]
        for d in range(1, N_CHIPS):
            wide_sum = wide_sum + wide_slots[d]
            tall_sum = tall_sum + tall_slots[d]
        for k, a in enumerate(rows):
            outs[k][...] = wide_sum[k:k + 1, 0:a.shape[1]]
        at = 0
        for k, h in enumerate(heights):
            outs[len(rows) + k][...] = tall_sum[at:at + h, :].reshape(mats[k].shape)
            at += h

    vmem = pl.BlockSpec(memory_space=pltpu.VMEM)
    return pl.pallas_call(
        body, name="all_reduce_small",
        in_specs=[vmem] * len(arrays), out_specs=[vmem] * len(arrays),
        out_shape=[jax.ShapeDtypeStruct(a.shape, F32) for a in arrays],
        scratch_shapes=[pltpu.VMEM((n_rows, D_MODEL), F32), pltpu.VMEM((n_tall, LANES), F32),
                        pltpu.VMEM((2, n_rows, D_MODEL), F32), pltpu.VMEM((2, n_tall, LANES), F32),
                        pltpu.VMEM((N_CHIPS, n_rows, D_MODEL), F32), pltpu.VMEM((N_CHIPS, n_tall, LANES), F32),
                        pltpu.SemaphoreType.DMA((n_sems,)), pltpu.SemaphoreType.DMA((n_sems,))],
    )(*arrays)


def _adamw_small(items):
    n = len(items)
    bias1 = 1.0 - ADAM_B1 ** ADAM_STEP
    bias2 = 1.0 - ADAM_B2 ** ADAM_STEP

    def body(*refs):
        ins, outs = refs[:4 * n], refs[4 * n:]
        for k in range(n):
            w_ref, g_ref, m_ref, v_ref = ins[4 * k:4 * k + 4]
            gg = g_ref[...]
            m2 = ADAM_B1 * m_ref[...] + (1.0 - ADAM_B1) * gg
            v2 = ADAM_B2 * v_ref[...] + (1.0 - ADAM_B2) * (gg * gg)
            outs[3 * k + 1][...] = m2
            outs[3 * k + 2][...] = v2
            outs[3 * k][...] = -ADAM_LR * ((m2 / bias1) / (jnp.sqrt(v2 / bias2) + ADAM_EPS) + ADAM_WD * w_ref[...])

    vmem = pl.BlockSpec(memory_space=pltpu.VMEM)
    out = pl.pallas_call(
        body, name="adamw_small", in_specs=[vmem] * (4 * n), out_specs=[vmem] * (3 * n),
        out_shape=[jax.ShapeDtypeStruct(w.shape, F32) for w, _, _, _ in items for _ in range(3)],
    )(*[a for item in items for a in item])
    return [tuple(out[3 * k:3 * k + 3]) for k in range(n)]


ADAMW_STEPS = 8


def _adamw(items, name, exchange=None):
    n = len(items)
    bias1 = 1.0 - ADAM_B1 ** ADAM_STEP
    bias2 = 1.0 - ADAM_B2 ** ADAM_STEP

    def body(*refs):
        ins, outs = refs[:4 * n], refs[4 * n:]
        for k in range(n):
            w_ref, g_ref, m_ref, v_ref = ins[4 * k:4 * k + 4]
            d_ref, mo_ref, vo_ref = outs[3 * k:3 * k + 3]
            gg = g_ref[...]
            m2 = ADAM_B1 * m_ref[...] + (1.0 - ADAM_B1) * gg
            v2 = ADAM_B2 * v_ref[...] + (1.0 - ADAM_B2) * (gg * gg)
            mo_ref[...] = m2
            vo_ref[...] = v2
            d_ref[...] = -ADAM_LR * ((m2 / bias1) / (jnp.sqrt(v2 / bias2) + ADAM_EPS) + ADAM_WD * w_ref[...])

    in_specs, out_specs, out_shape, args = [], [], [], []
    for w, g, m, v in items:
        r, c = w.shape
        steps = ADAMW_STEPS if r % (8 * ADAMW_STEPS) == 0 else 1
        assert steps == ADAMW_STEPS or n == 1
        spec = pl.BlockSpec((r // steps, c), lambda i: (i, 0))
        in_specs += [spec] * 4
        out_specs += [spec] * 3
        out_shape += [jax.ShapeDtypeStruct((r, c), F32)] * 3
        args += [w, g, m, v]
    out, got = _call(body, args, name=name, grid=(steps,), in_specs=in_specs, out_specs=out_specs, out_shape=out_shape,
                     exchange=exchange, steps=(0, steps - 1))
    return [tuple(out[3 * k:3 * k + 3]) for k in range(n)], got


def kernel(x, p, ffn1_norm, ffn1_w_in, ffn1_w_out, mix_norm, w_mix_in, gmlp_v_norm, gmlp_w_s, gmlp_b, w_mix_out, ffn2_norm, ffn2_w_in, ffn2_w_out, ple_norm, ple_w_gate, ple_w_proj, final_norm, loss_target, m_ffn1_norm, m_ffn1_w_in, m_ffn1_w_out, m_mix_norm, m_w_mix_in, m_gmlp_v_norm, m_gmlp_w_s, m_gmlp_b, m_w_mix_out, m_ffn2_norm, m_ffn2_w_in, m_ffn2_w_out, m_ple_norm, m_ple_w_gate, m_ple_w_proj, m_final_norm, v_ffn1_norm, v_ffn1_w_in, v_ffn1_w_out, v_mix_norm, v_w_mix_in, v_gmlp_v_norm, v_gmlp_w_s, v_gmlp_b, v_w_mix_out, v_ffn2_norm, v_ffn2_w_in, v_ffn2_w_out, v_ple_norm, v_ple_w_gate, v_ple_w_proj, v_final_norm):
    args = dict(locals())
    w = {n: args[n] for n in _ALL}
    m = {n: args["m_" + n] for n in _ALL}
    v = {n: args["v_" + n] for n in _ALL}
    xi, yi, ci = _mesh_pos()
    pos = jnp.stack([2 * xi + yi, ci]).astype(jnp.int32)
    shard = {n: w[n][0] for n in _BIG}
    cast = {n: shard[n].astype(BF16) for n in _BIG}
    small = {n: (w[n][0] if w[n].ndim > 2 else w[n].reshape(1, -1)) for n in _SMALL}
    bt = small["gmlp_b"].T
    g_small, pair, from_chips = {}, {}, {}

    def pair_reduce(partials, tag, host=None):
        names = list(partials)
        blocks = lambda a, n: a.reshape(N_CHIPS, *shard[n].shape)
        exchange = _PairExchange([blocks(partials[n][1], n) for n in names])
        result, got = host(exchange) if host else (None, _run_exchange(exchange, "grad_pair_exchange_" + tag))
        pair.update(zip(names, _pair_sums([blocks(partials[n][0], n) for n in names], got, pos, "pair_sums_" + tag)))
        return names, result

    w1in, w1out = _run_exchange(_WeightGather([cast["ffn1_w_in"], cast["ffn1_w_out"]]), "gather_ffn1")
    w1out = w1out.reshape(D_FF, D_MODEL)
    (h1, gu1), (wmix, wmo) = _ffn_fwd(x[0], small["ffn1_norm"], w1in, w1out, "ffn1_fwd",
                                      _WeightGather([cast["w_mix_in"], cast["w_mix_out"]]))
    wmo = wmo.reshape(D_MODEL, D_MODEL)
    zg, qkv = _mix_in_fwd(h1, small["mix_norm"], wmix)
    gm = _gmlp_fwd(zg, small["gmlp_v_norm"], small["gmlp_w_s"], bt)
    (mixed, carries), (w2in, w2out, wg, wproj) = _attn_fwd(
        qkv, gm, _WeightGather([cast["ffn2_w_in"], cast["ffn2_w_out"], cast["ple_w_gate"], cast["ple_w_proj"]]))
    w2out = w2out.reshape(D_FF, D_MODEL)
    wg = wg.reshape(D_MODEL, D_MODEL)
    h2 = _matmul_residual(h1, mixed, wmo, "mix_out_fwd")
    (h3, gu2), _ = _ffn_fwd(h2, small["ffn2_norm"], w2in, w2out, "ffn2_fwd")
    loss_part, g_small["final_norm"], g_small["ple_norm"], dh3, dgp, dpp, n4, pb = _head(
        h3, p[0, 0], loss_target[0], small["ple_norm"], small["final_norm"], wg, wproj)

    part = {"ple_w_gate": _wgrad_rows(n4, dgp, N_CHIPS, "wgrad_ple_gate"),
            "ple_w_proj": _wgrad_cols(pb, dpp, N_CHIPS, "wgrad_ple_proj")}
    (dh2, dgu2, n3, act2, dhh3, g_small["ffn2_norm"]), _ = _ffn_bwd(dh3, h2, small["ffn2_norm"], gu2, w2in, w2out,
                                                                   "ffn2_bwd")
    part["ffn2_w_in"] = _wgrad_cols(n3, dgu2, N_CHIPS, "wgrad_ffn2_in")
    part["ffn2_w_out"] = _wgrad_rows(act2, dhh3, 2, "wgrad_ffn2_out")
    dmixed, dh2b = _matmul_nt_cast(dh2, wmo, "mix_out_bwd")
    part["w_mix_out"] = _wgrad_rows(mixed, dh2b, 2, "wgrad_mix_out")
    group, (dzg, g_small["gmlp_w_s"], dbt, g_small["gmlp_v_norm"]) = pair_reduce(
        part, "late", lambda ex: _gmlp_bwd(zg, dmixed, small["gmlp_v_norm"], small["gmlp_w_s"], bt, ex))
    g_small["gmlp_b"] = dbt.T
    (dq, dk, dv), got = _attn_bwd(qkv, dmixed, carries, _ChipExchange([pair[n] for n in group]))
    from_chips.update(zip(group, got))

    dzmix = jnp.concatenate([dzg, dq, dk, dv], axis=1)
    dh1, n2, g_small["mix_norm"] = _norm_input_bwd(dh2, dzmix, wmix, h1, small["mix_norm"], "mix_in_bwd")
    group, _ = pair_reduce({"w_mix_in": _wgrad_cols(n2, dzmix, N_CHIPS, "wgrad_mix_in")}, "mix")
    (dx, dgu1, n1, act1, dhh1, g_small["ffn1_norm"]), _ = _ffn_bwd(dh1, x[0], small["ffn1_norm"], gu1, w1in, w1out,
                                                                   "ffn1_bwd")

    g_out, got = _wgrad_rows(act1, dhh1, 2, "wgrad_ffn1_out", _ChipExchange([pair[n] for n in group]))
    from_chips.update(zip(group, got))
    pair_reduce({"ffn1_w_out": g_out}, "out")
    g_in, got = _wgrad_cols(n1, dgu1, N_CHIPS, "wgrad_ffn1_in", _ChipExchange([pair["ffn1_w_out"]]))
    from_chips["ffn1_w_out"] = got[0]
    pair_reduce({"ffn1_w_in": g_in}, "in")

    def finish(names, tag, exchange=None):
        halves = _chip_sums([pair[n] for n in names], [from_chips[n] for n in names], pos, "chip_sums_" + tag)
        full = _pair_share(halves, "grad_pair_share_" + tag)
        out, got = _adamw([(shard[n], g, m[n][0], v[n][0]) for n, g in zip(names, full)], "adamw_" + tag, exchange)
        for n, g, (d2, m2, v2) in zip(names, full, out):
            grads[n], delta[n], new_m[n], new_v[n] = g[None], d2[None], m2[None], v2[None]
        return got

    grads, delta, new_m, new_v = {}, {}, {}, {}
    got = finish([n for n in _BIG if n != "ffn1_w_in"], "most", _ChipExchange([pair["ffn1_w_in"]]))
    from_chips["ffn1_w_in"] = got[0]
    finish(["ffn1_w_in"], "last")

    rows = [n for n in _SMALL if g_small[n].shape[0] == 1]
    mats = [n for n in _SMALL if n not in rows]
    summed = _all_reduce_small([g_small[n] for n in rows] + [loss_part], [g_small[n] for n in mats])
    loss = summed[len(rows)][0, 0]
    g_sum = dict(zip(rows + mats, summed[:len(rows)] + summed[len(rows) + 1:]))
    like = lambda a, n: a[n].reshape(small[n].shape)
    out = _adamw_small([(small[n], g_sum[n], like(m, n), like(v, n)) for n in _SMALL])
    for n, (d2, m2, v2) in zip(_SMALL, out):
        grads[n], delta[n], new_m[n], new_v[n] = (a.reshape(w[n].shape) for a in (g_sum[n], d2, m2, v2))

    return (loss, dx[None], *[grads[n] for n in _ALL], *[delta[n] for n in _ALL], *[new_m[n] for n in _ALL],
            *[new_v[n] for n in _ALL])
```

```python
import functools
import math

import jax
import jax.numpy as jnp
from jax import lax
from jax.experimental import pallas as pl
from jax.experimental.pallas import tpu as pltpu

F32, BF16 = jnp.float32, jnp.bfloat16

D_MODEL = 1024
D_FF = 2816
FF_BLOCK = 2 * D_FF // 4
PLE_DIM = 256
CHUNK = 128
GM_HEADS = 4
GM_WIDTH = 512
SB_HEAD_DIM = 64
SB_WIDTH = 512
MIX_IN_WIDTH = 2 * GM_WIDTH + 3 * SB_WIDTH
MIX_BLOCK = MIX_IN_WIDTH // 4
EPS = 1e-6
N_CHIPS = 4
LANES = 128
ATT_BLOCK = 128
ATT_Q = 512
VMEM_LIMIT = 56 * 1024 * 1024

ADAM_LR, ADAM_B1, ADAM_B2, ADAM_EPS, ADAM_WD, ADAM_STEP = 0.001, 0.9, 0.999, 1e-08, 0.01, 10


def _dot(a, b):
    return jnp.dot(a, b, preferred_element_type=F32)


def _dot_nt(a, b):
    return lax.dot_general(a, b, (((1,), (1,)), ((), ())), preferred_element_type=F32)


def _dot_tn(a, b):
    return lax.dot_general(a, b, (((0,), (0,)), ((), ())), preferred_element_type=F32)


def _resident(shape):
    nd = len(shape)
    return pl.BlockSpec(shape, lambda *_: (0,) * nd, pipeline_mode=pl.Buffered(1))


def _rows(tm, width):
    return pl.BlockSpec((tm, width), lambda i: (i, 0))


def _params(n_axes=1):
    return pltpu.CompilerParams(dimension_semantics=("arbitrary",) * n_axes, vmem_limit_bytes=VMEM_LIMIT)


def _rstd(h):
    return lax.rsqrt(jnp.mean(h * h, axis=-1, keepdims=True) + EPS)


def _rms_bwd(dy, h, r, g):
    dyg = dy * g
    dh = r * dyg - h * (r * r * r) * jnp.mean(dyg * h, axis=-1, keepdims=True)
    return dh, dy * h * r


def _gelu(x):
    return 0.5 * x * (1.0 + lax.erf(x * (2.0 ** -0.5)))


def _gelu_grad(x):
    return 0.5 * (1.0 + lax.erf(x * (2.0 ** -0.5))) + x * jnp.exp(-0.5 * x * x) * ((2.0 * jnp.pi) ** -0.5)


def _token_tile(t, wide=False):
    return min(512 if wide else 256, t)


def _ffn_fwd(h, g, win, wout, name, exchange=None):
    t = h.shape[0]
    tm = _token_tile(t)

    def body(h_ref, g_ref, win_ref, wout_ref, ho_ref, gu_ref):
        hh = h_ref[...]
        n = (hh * _rstd(hh) * g_ref[...]).astype(BF16)
        acc = jnp.zeros((tm, D_MODEL), F32)
        for jb in range(2):
            gate = _dot(n, win_ref[jb])
            up = _dot(n, win_ref[2 + jb])
            gu_ref[:, jb * FF_BLOCK:(jb + 1) * FF_BLOCK] = gate.astype(BF16)
            gu_ref[:, D_FF + jb * FF_BLOCK:D_FF + (jb + 1) * FF_BLOCK] = up.astype(BF16)
            act = (gate * jax.nn.sigmoid(gate) * up).astype(BF16)
            acc = acc + _dot(act, wout_ref[jb * FF_BLOCK:(jb + 1) * FF_BLOCK, :])
        ho_ref[...] = hh + 0.5 * acc

    n = t // tm
    return _call(
        body, (h, g, win, wout), name=name, grid=(n,),
        in_specs=[_rows(tm, D_MODEL), _resident((1, D_MODEL)), _resident(win.shape), _resident(wout.shape)],
        out_specs=[_rows(tm, D_MODEL), _rows(tm, 2 * D_FF)],
        out_shape=[jax.ShapeDtypeStruct((t, D_MODEL), F32), jax.ShapeDtypeStruct((t, 2 * D_FF), BF16)],
        exchange=exchange, steps=(0, (2 * n) // 3, n - 1))


def _ffn_bwd(dho, h, g, gu, win, wout, name, exchange=None):
    t = h.shape[0]
    tm = _token_tile(t)

    def body(dho_ref, h_ref, g_ref, gu_ref, win_ref, wout_ref, dh_ref, dgu_ref, n_ref, act_ref, dhh_ref, dg_ref):
        i = pl.program_id(0)
        hh = h_ref[...]
        gg = g_ref[...]
        r = _rstd(hh)
        n_ref[...] = (hh * r * gg).astype(BF16)
        dho = dho_ref[...]
        dhh = (0.5 * dho).astype(BF16)
        dhh_ref[...] = dhh
        dn = jnp.zeros((tm, D_MODEL), F32)
        for jb in range(2):
            cg = slice(jb * FF_BLOCK, (jb + 1) * FF_BLOCK)
            cu = slice(D_FF + jb * FF_BLOCK, D_FF + (jb + 1) * FF_BLOCK)
            dact = _dot_nt(dhh, wout_ref[cg, :])
            gate = gu_ref[:, cg].astype(F32)
            up = gu_ref[:, cu].astype(F32)
            sg = jax.nn.sigmoid(gate)
            silu = gate * sg
            act_ref[:, cg] = (silu * up).astype(BF16)
            dgate = (dact * up * (sg * (1.0 + gate * (1.0 - sg)))).astype(BF16)
            dup = (dact * silu).astype(BF16)
            dgu_ref[:, cg] = dgate
            dgu_ref[:, cu] = dup
            dn = dn + _dot_nt(dgate, win_ref[jb]) + _dot_nt(dup, win_ref[2 + jb])
        dh, dg_rows = _rms_bwd(dn, hh, r, gg)
        dh_ref[...] = dho + dh

        @pl.when(i == 0)
        def _():
            dg_ref[...] = jnp.zeros_like(dg_ref)

        dg_ref[...] += jnp.sum(dg_rows, axis=0, keepdims=True)

    n = t // tm
    return _call(
        body, (dho, h, g, gu, win, wout), name=name, grid=(n,),
        in_specs=[_rows(tm, D_MODEL), _rows(tm, D_MODEL), _resident((1, D_MODEL)), _rows(tm, 2 * D_FF),
                  _resident(win.shape), _resident(wout.shape)],
        out_specs=[_rows(tm, D_MODEL), _rows(tm, 2 * D_FF), _rows(tm, D_MODEL), _rows(tm, D_FF), _rows(tm, D_MODEL),
                   pl.BlockSpec((1, D_MODEL), lambda i: (0, 0))],
        out_shape=[jax.ShapeDtypeStruct((t, D_MODEL), F32), jax.ShapeDtypeStruct((t, 2 * D_FF), BF16),
                   jax.ShapeDtypeStruct((t, D_MODEL), BF16), jax.ShapeDtypeStruct((t, D_FF), BF16),
                   jax.ShapeDtypeStruct((t, D_MODEL), BF16), jax.ShapeDtypeStruct((1, D_MODEL), F32)],
        exchange=exchange, steps=(0, n - 1))


def _wgrad(a, b, out_shape, out_block, out_index, a_width, b_width, grid_ij, name, exchange=None):
    t = a.shape[0]
    tk = t if 4 * t * (a_width + b_width) <= 26 * 2 ** 20 else min(2048, t)
    nk = t // tk

    def body(a_ref, b_ref, o_ref, ob_ref):
        k = pl.program_id(2)
        prod = _dot_tn(a_ref[...], b_ref[...]).reshape(o_ref.shape)

        @pl.when(k == 0)
        def _():
            o_ref[...] = prod

        @pl.when(k > 0)
        def _():
            o_ref[...] += prod

        @pl.when(k == nk - 1)
        def _():
            ob_ref[...] = o_ref[...].astype(BF16)

    grid = (*grid_ij, nk)
    out_spec = pl.BlockSpec(out_block, lambda i, j, k: out_index(i, j))
    out, got = _call(
        body, (a, b), name=name, grid=grid,
        in_specs=[pl.BlockSpec((tk, a_width), lambda i, j, k: (k, i)), pl.BlockSpec((tk, b_width), lambda i, j, k: (k, j))],
        out_specs=[out_spec, out_spec],
        out_shape=[jax.ShapeDtypeStruct(out_shape, F32), jax.ShapeDtypeStruct(out_shape, BF16)],
        exchange=exchange, steps=(0, grid[0] * grid[1] * grid[2] - 1))
    return tuple(out) if exchange is None else (tuple(out), got)


def _wgrad_cols(a, b, n_blocks, name, exchange=None):
    ka, nb = a.shape[1], b.shape[1] // n_blocks
    return _wgrad(a, b, (n_blocks, ka, nb), (1, ka, nb), lambda i, j: (j, 0, 0), ka, nb, (1, n_blocks), name, exchange)


def _wgrad_rows(a, b, n_blocks, name, exchange=None):
    ka, nb = a.shape[1] // n_blocks, b.shape[1]
    return _wgrad(a, b, (a.shape[1], nb), (ka, nb), lambda i, j: (i, 0), ka, nb, (n_blocks, 1), name, exchange)


def _mix_in_fwd(h, g, wmix):
    t = h.shape[0]
    tm = _token_tile(t, wide=True)
    gw2 = 2 * GM_WIDTH

    def body(h_ref, g_ref, w_ref, zg_ref, qkv_ref):
        hh = h_ref[...]
        n = (hh * _rstd(hh) * g_ref[...]).astype(BF16)
        for b in range(N_CHIPS):
            z = _dot(n, w_ref[b])
            lo, hi = b * MIX_BLOCK, (b + 1) * MIX_BLOCK
            if hi <= gw2:
                zg_ref[:, lo:hi] = z
            elif lo >= gw2:
                qkv_ref[:, lo - gw2:hi - gw2] = z.astype(BF16)
            else:
                zg_ref[:, lo:gw2] = z[:, :gw2 - lo]
                qkv_ref[:, 0:hi - gw2] = z[:, gw2 - lo:].astype(BF16)

    return pl.pallas_call(
        body, name="mix_in_fwd", grid=(t // tm,),
        in_specs=[_rows(tm, D_MODEL), _resident((1, D_MODEL)), _resident(wmix.shape)],
        out_specs=[_rows(tm, gw2), _rows(tm, 3 * SB_WIDTH)],
        out_shape=[jax.ShapeDtypeStruct((t, gw2), F32), jax.ShapeDtypeStruct((t, 3 * SB_WIDTH), BF16)],
        compiler_params=_params(),
    )(h, g, wmix)


def _causal_chunk_mask():
    row = lax.broadcasted_iota(jnp.int32, (CHUNK, CHUNK), 0)
    col = lax.broadcasted_iota(jnp.int32, (CHUNK, CHUNK), 1)
    return row >= col


def _gmlp_tile(t):
    return min(512, t)


def _gmlp_fwd(zg, gv, ws, bt):
    t = zg.shape[0]
    tm = _gmlp_tile(t)

    def body(zg_ref, gv_ref, ws_ref, bt_ref, o_ref):
        u = _gelu(zg_ref[:, :GM_WIDTH])
        v = _gelu(zg_ref[:, GM_WIDTH:])
        vn = (v * _rstd(v) * gv_ref[...]).astype(BF16)
        mask = _causal_chunk_mask()
        for hd in range(GM_HEADS):
            wm = jnp.where(mask, ws_ref[hd], 0.0).astype(BF16)
            cols = slice(hd * CHUNK, (hd + 1) * CHUNK)
            for c in range(tm // CHUNK):
                rows = slice(c * CHUNK, (c + 1) * CHUNK)
                sv = _dot(wm, vn[rows, cols]) + bt_ref[:, hd:hd + 1]
                o_ref[rows, cols] = (u[rows, cols] * sv).astype(BF16)

    return pl.pallas_call(
        body, name="gmlp_fwd", grid=(t // tm,),
        in_specs=[_rows(tm, 2 * GM_WIDTH), _resident((1, GM_WIDTH)), _resident(ws.shape), _resident(bt.shape)],
        out_specs=_rows(tm, GM_WIDTH),
        out_shape=jax.ShapeDtypeStruct((t, D_MODEL), BF16),
        compiler_params=_params(),
    )(zg, gv, ws, bt)


def _gmlp_bwd(zg, dmixed, gv, ws, bt, exchange=None):
    t = zg.shape[0]
    tm = _gmlp_tile(t)

    def body(zg_ref, dgm_ref, gv_ref, ws_ref, bt_ref, dzg_ref, dws_ref, dbt_ref, dgv_ref):
        i = pl.program_id(0)

        @pl.when(i == 0)
        def _():
            dws_ref[...] = jnp.zeros_like(dws_ref)
            dbt_ref[...] = jnp.zeros_like(dbt_ref)
            dgv_ref[...] = jnp.zeros_like(dgv_ref)

        zu = zg_ref[:, :GM_WIDTH]
        zv = zg_ref[:, GM_WIDTH:]
        u = _gelu(zu)
        v = _gelu(zv)
        r = _rstd(v)
        gvv = gv_ref[...]
        vn = (v * r * gvv).astype(BF16)
        dgm = dgm_ref[...].astype(F32)
        dsv = (dgm * u).astype(BF16)
        mask = _causal_chunk_mask()
        du_cols, dvn_cols = [], []
        for hd in range(GM_HEADS):
            wm = jnp.where(mask, ws_ref[hd], 0.0).astype(BF16)
            cols = slice(hd * CHUNK, (hd + 1) * CHUNK)
            dw = jnp.zeros((CHUNK, CHUNK), F32)
            db = jnp.zeros((CHUNK, 1), F32)
            du_rows, dvn_rows = [], []
            for c in range(tm // CHUNK):
                rows = slice(c * CHUNK, (c + 1) * CHUNK)
                sv = _dot(wm, vn[rows, cols]) + bt_ref[:, hd:hd + 1]
                du_rows.append(dgm[rows, cols] * sv)
                dvn_rows.append(_dot_tn(wm, dsv[rows, cols]))
                dw = dw + _dot_nt(dsv[rows, cols], vn[rows, cols])
                db = db + jnp.sum(dsv[rows, cols].astype(F32), axis=1, keepdims=True)
            dws_ref[hd] += jnp.where(mask, dw, 0.0)
            dbt_ref[:, hd:hd + 1] += db
            du_cols.append(jnp.concatenate(du_rows, axis=0))
            dvn_cols.append(jnp.concatenate(dvn_rows, axis=0))
        du = jnp.concatenate(du_cols, axis=1)
        dvn = jnp.concatenate(dvn_cols, axis=1)
        dv, dgv_rows = _rms_bwd(dvn, v, r, gvv)
        dgv_ref[...] += jnp.sum(dgv_rows, axis=0, keepdims=True)
        dzg_ref[:, :GM_WIDTH] = (du * _gelu_grad(zu)).astype(BF16)
        dzg_ref[:, GM_WIDTH:] = (dv * _gelu_grad(zv)).astype(BF16)

    const = lambda nd: (lambda i: (0,) * nd)
    n = t // tm
    return _call(
        body, (zg, dmixed, gv, ws, bt), name="gmlp_bwd", grid=(n,),
        in_specs=[_rows(tm, 2 * GM_WIDTH), _rows(tm, GM_WIDTH), _resident((1, GM_WIDTH)), _resident(ws.shape),
                  _resident(bt.shape)],
        out_specs=[_rows(tm, 2 * GM_WIDTH), pl.BlockSpec(ws.shape, const(3)), pl.BlockSpec(bt.shape, const(2)),
                   pl.BlockSpec((1, GM_WIDTH), const(2))],
        out_shape=[jax.ShapeDtypeStruct((t, 2 * GM_WIDTH), BF16), jax.ShapeDtypeStruct(ws.shape, F32),
                   jax.ShapeDtypeStruct(bt.shape, F32), jax.ShapeDtypeStruct((1, GM_WIDTH), F32)],
        exchange=exchange, steps=(0, n - 1))


def _att_masks():
    tb = ATT_BLOCK
    lane = lax.broadcasted_iota(jnp.int32, (1, LANES), 1)
    rj = lax.broadcasted_iota(jnp.int32, (2 * tb, 2 * tb), 0)
    cs = lax.broadcasted_iota(jnp.int32, (2 * tb, 2 * tb), 1)
    same_head = ((rj < tb) & (cs < tb)) | ((rj >= tb) & (cs >= tb))
    suffix = jnp.where(same_head & (rj >= cs), 1.0, 0.0).astype(BF16)
    prefix = jnp.where(same_head & (rj <= cs), 1.0, 0.0).astype(BF16)
    left = lax.broadcasted_iota(jnp.int32, (1, 2 * tb), 1) < tb
    tq = lax.broadcasted_iota(jnp.int32, (ATT_Q, 4 * tb), 0)
    ts = lax.broadcasted_iota(jnp.int32, (ATT_Q, 4 * tb), 1)
    key = jnp.where(ts < 2 * tb, ts & (tb - 1), (ts & (tb - 1)) + tb)
    return lane, suffix, prefix, left, key, tq


def _att_fill(k_ref, v_ref, kcat, vcat, n_blocks, lane):
    tb = ATT_BLOCK
    first = lane < SB_HEAD_DIM

    def fill(jb, carry):
        rows = pl.ds(pl.multiple_of(jb * tb, tb), tb)
        top = pl.ds(pl.multiple_of(jb * 2 * tb, tb), tb)
        bot = pl.ds(pl.multiple_of(jb * 2 * tb + tb, tb), tb)
        kb = k_ref[rows, :]
        vb = v_ref[rows, :]
        zero = jnp.zeros_like(kb)
        kcat[top, :] = jnp.where(first, kb, zero)
        kcat[bot, :] = jnp.where(first, zero, kb)
        vcat[top, :] = jnp.where(first, vb, zero)
        vcat[bot, :] = jnp.where(first, zero, vb)
        return carry

    lax.fori_loop(0, n_blocks, fill, 0)


def _block_sums(x, m):
    return _dot(x.astype(BF16), m)


def _softplus2(z2):
    return jnp.maximum(z2, 0.0) + jnp.log2(1.0 + jnp.exp2(-jnp.abs(z2)))


def _scaled_queries(q_ref, base2):
    scale = SB_HEAD_DIM ** -0.5
    return (q_ref[...].astype(F32) * (scale * math.log2(math.e) if base2 else scale)).astype(BF16)


def _att_specs(t):
    n_pairs = SB_WIDTH // LANES
    q_spec = pl.BlockSpec((ATT_Q, LANES), lambda p, i: (i, p))
    k_spec = pl.BlockSpec((t, LANES), lambda p, i: (0, n_pairs + p))
    v_spec = pl.BlockSpec((t, LANES), lambda p, i: (0, 2 * n_pairs + p))
    return n_pairs, q_spec, k_spec, v_spec


def _attn_fwd(qkv, mixed, exchange=None):
    t = qkv.shape[0]
    tb = ATT_BLOCK
    nkb = t // tb
    assert 2 * nkb <= LANES and t % ATT_Q == 0 and ATT_Q == 4 * tb
    n_pairs, q_spec, k_spec, v_spec = _att_specs(t)

    def body(q_ref, k_ref, v_ref, mixed_ref, o_ref, ct_ref, kcat, vcat, acc, carry, z0, r0, z1, r1):
        i = pl.program_id(1)
        lane, suffix, _, left, key, tq = _att_masks()

        @pl.when(i == 0)
        def _():
            _att_fill(k_ref, v_ref, kcat, vcat, nkb, lane)

        q = _scaled_queries(q_ref, True)
        acc[...] = jnp.zeros_like(acc)
        carry[...] = jnp.zeros_like(carry)
        ct_ref[0] = jnp.zeros((ATT_Q, LANES), F32)

        def key_rows(m):
            return pl.ds(pl.multiple_of(m * 4 * tb, 4 * tb), 4 * tb)

        def scores(m, zb, rb, causal=None, rs=slice(None)):
            z = _dot_nt(q[rs], kcat[key_rows(m), :])
            zb[rs, :] = z
            sp = _softplus2(z)
            if causal is not None:
                sp = jnp.where(causal[rs], sp, 0.0)
            for g in (1, 0):
                cols = slice(g * 2 * tb, (g + 1) * 2 * tb)
                rb[rs, cols] = _block_sums(sp[:, cols], suffix)

        def weigh(m, zb, rb, causal=None, rs=slice(None)):
            probs = [None, None]
            for g in (1, 0):
                cols = slice(g * 2 * tb, (g + 1) * 2 * tb)
                j = 2 * m + g
                r = rb[rs, cols]
                c = carry[rs, :]
                ct_ref[0, rs, :] = jnp.where(lane == j, c[:, :tb], jnp.where(lane == nkb + j, c[:, tb:], ct_ref[0, rs, :]))
                a = jnp.exp2(zb[rs, cols] - (r + c))
                if causal is not None:
                    a = jnp.where(causal[rs, cols], a, 0.0)
                probs[g] = a.astype(BF16)
                carry[rs, :] = c + jnp.where(left, r[:, 0:1], r[:, tb:tb + 1])
            acc[rs, :] += _dot(jnp.concatenate(probs, axis=1), vcat[key_rows(m), :])

        sooner, later, late_rows = key < tq, key + 2 * tb < tq, slice(2 * tb, 4 * tb)
        scores(2 * i + 1, z1, r1, later, late_rows)
        scores(2 * i, z0, r0, sooner)
        weigh(2 * i + 1, z1, r1, later, late_rows)
        weigh(2 * i, z0, r0, sooner)

        @pl.when(i > 0)
        def _():
            scores(2 * i - 1, z1, r1)

            def loop(k, c):
                u = i - 1 - k
                scores(2 * u, z0, r0)
                weigh(2 * u + 1, z1, r1)
                scores(2 * u - 1, z1, r1)
                weigh(2 * u, z0, r0)
                return c

            lax.fori_loop(0, i - 1, loop, 0)
            scores(0, z0, r0)
            weigh(1, z1, r1)
            weigh(0, z0, r0)

        o_ref[...] = acc[...].astype(BF16)

    tile = pltpu.VMEM((ATT_Q, 4 * tb), F32)

    nq = t // ATT_Q
    return _call(
        body, (qkv, qkv, qkv, mixed), name="attn_fwd", grid=(n_pairs, nq),
        in_specs=[q_spec, k_spec, v_spec, _ANY],
        out_specs=[pl.BlockSpec((ATT_Q, LANES), lambda p, i: (i, GM_WIDTH // LANES + p)),
                   pl.BlockSpec((1, ATT_Q, LANES), lambda p, i: (p, i, 0))],
        out_shape=[jax.ShapeDtypeStruct(mixed.shape, BF16), jax.ShapeDtypeStruct((n_pairs, t, LANES), F32)],
        aliases={3: 0},
        scratch_shapes=[pltpu.VMEM((2 * t, LANES), BF16), pltpu.VMEM((2 * t, LANES), BF16),
                        pltpu.VMEM((ATT_Q, LANES), F32), pltpu.VMEM((ATT_Q, 2 * tb), F32), tile, tile, tile, tile],
        exchange=exchange, steps=(0, (n_pairs - 1) * nq - 1, n_pairs * nq - 1))


def _attn_bwd(qkv, dmixed, carries, exchange=None):
    t = qkv.shape[0]
    tb = ATT_BLOCK
    nkb = t // tb
    nq = t // ATT_Q
    scale = SB_HEAD_DIM ** -0.5
    n_pairs, q_spec, k_spec, v_spec = _att_specs(t)
    gm_blocks = GM_WIDTH // LANES

    def body(q_ref, k_ref, v_ref, do_ref, ct_ref, dq_ref, dk_ref, dv_ref, kcat, vcat, dkacc, dvacc, dqacc, carry,
             z0, r0, s0, a0, z1, r1, s1, a1):
        i = pl.program_id(1)
        lane, suffix, prefix, left, key, tq = _att_masks()
        first = lane < SB_HEAD_DIM

        @pl.when(i == 0)
        def _():
            _att_fill(k_ref, v_ref, kcat, vcat, nkb, lane)
            dkacc[...] = jnp.zeros_like(dkacc)
            dvacc[...] = jnp.zeros_like(dvacc)

        q2 = _scaled_queries(q_ref, True)
        q = _scaled_queries(q_ref, False)
        do = do_ref[...]
        dqacc[...] = jnp.zeros_like(dqacc)
        carry[...] = jnp.zeros_like(carry)

        def key_rows(m):
            return pl.ds(pl.multiple_of(m * 4 * tb, 4 * tb), 4 * tb)

        def front(m, bufs, causal=None, rs=slice(None)):
            zb, rb, sb, ab = bufs
            z = _dot_nt(q2[rs], kcat[key_rows(m), :])
            zb[rs, :] = z
            sp = _softplus2(z)
            sb[rs, :] = jnp.exp2(z - sp)
            if causal is not None:
                sp = jnp.where(causal[rs], sp, 0.0)
            for g in (0, 1):
                cols = slice(g * 2 * tb, (g + 1) * 2 * tb)
                rb[rs, cols] = _block_sums(sp[:, cols], suffix)
            ab[rs, :] = _dot_nt(do[rs], vcat[key_rows(m), :])

        def back(m, bufs, causal=None, rs=slice(None)):
            zb, rb, sb, ab = bufs
            dzs, probs = [None, None], [None, None]
            for g in (0, 1):
                cols = slice(g * 2 * tb, (g + 1) * 2 * tb)
                j = 2 * m + g
                ct = ct_ref[0, rs, :]
                ca = jnp.sum(jnp.where(lane == j, ct, 0.0), axis=1, keepdims=True)
                cb = jnp.sum(jnp.where(lane == nkb + j, ct, 0.0), axis=1, keepdims=True)
                a = jnp.exp2(zb[rs, cols] - (rb[rs, cols] + jnp.where(left, ca, cb)))
                if causal is not None:
                    a = jnp.where(causal[rs, cols], a, 0.0)
                de = ab[rs, cols] * a
                cl = _block_sums(de, prefix)
                pre = carry[rs, :]
                dz = de - sb[rs, cols] * (cl + pre)
                if causal is not None:
                    dz = jnp.where(causal[rs, cols], dz, 0.0)
                carry[rs, :] = pre + jnp.where(left, cl[:, tb - 1:tb], cl[:, 2 * tb - 1:2 * tb])
                dzs[g] = dz.astype(BF16)
                probs[g] = a.astype(BF16)
            dzb = jnp.concatenate(dzs, axis=1)
            dqacc[rs, :] += _dot(dzb, kcat[key_rows(m), :])
            dkc = _dot_tn(dzb, q[rs])
            dvc = _dot_tn(jnp.concatenate(probs, axis=1), do[rs])
            out_rows = pl.ds(pl.multiple_of(m * 2 * tb, 2 * tb), 2 * tb)
            pick = lambda x: jnp.concatenate([jnp.where(first, x[0:tb], x[tb:2 * tb]),
                                              jnp.where(first, x[2 * tb:3 * tb], x[3 * tb:4 * tb])], axis=0)
            dkacc[out_rows, :] += pick(dkc)
            dvacc[out_rows, :] += pick(dvc)

        b0, b1 = (z0, r0, s0, a0), (z1, r1, s1, a1)

        @pl.when(i > 0)
        def _():
            front(0, b0)

            def loop(u, c):
                front(2 * u + 1, b1)
                back(2 * u, b0)
                front(2 * u + 2, b0)
                back(2 * u + 1, b1)
                return c

            lax.fori_loop(0, i - 1, loop, 0)
            front(2 * i - 1, b1)
            back(2 * i - 2, b0)
            back(2 * i - 1, b1)

        sooner, later, late_rows = key < tq, key + 2 * tb < tq, slice(2 * tb, 4 * tb)
        front(2 * i, b0, sooner)
        front(2 * i + 1, b1, later, late_rows)
        back(2 * i, b0, sooner)
        back(2 * i + 1, b1, later, late_rows)

        dq_ref[...] = (dqacc[...] * scale).astype(BF16)

        @pl.when(i == nq - 1)
        def _():
            dk_ref[...] = dkacc[...].astype(BF16)
            dv_ref[...] = dvacc[...].astype(BF16)

    col = pl.BlockSpec((t, LANES), lambda p, i: (0, p))
    out = jax.ShapeDtypeStruct((t, SB_WIDTH), BF16)
    tile = pltpu.VMEM((ATT_Q, 4 * tb), F32)
    return _call(
        body, (qkv, qkv, qkv, dmixed, carries), name="attn_bwd", grid=(n_pairs, nq),
        in_specs=[q_spec, k_spec, v_spec, pl.BlockSpec((ATT_Q, LANES), lambda p, i: (i, gm_blocks + p)),
                  pl.BlockSpec((1, ATT_Q, LANES), lambda p, i: (p, i, 0))],
        out_specs=[pl.BlockSpec((ATT_Q, LANES), lambda p, i: (i, p)), col, col],
        out_shape=[out, out, out],
        scratch_shapes=[pltpu.VMEM((2 * t, LANES), BF16), pltpu.VMEM((2 * t, LANES), BF16),
                        pltpu.VMEM((t, LANES), F32), pltpu.VMEM((t, LANES), F32),
                        pltpu.VMEM((ATT_Q, LANES), F32), pltpu.VMEM((ATT_Q, 2 * tb), F32)] + [tile] * 8,
        exchange=exchange, steps=(0, n_pairs * nq - 1))


def _matmul_residual(res, a, w, name):
    t = a.shape[0]
    tm = _token_tile(t, wide=True)

    def body(res_ref, a_ref, w_ref, o_ref):
        o_ref[...] = res_ref[...] + _dot(a_ref[...], w_ref[...])

    return pl.pallas_call(
        body, name=name, grid=(t // tm,),
        in_specs=[_rows(tm, res.shape[1]), _rows(tm, a.shape[1]), _resident(w.shape)],
        out_specs=_rows(tm, res.shape[1]),
        out_shape=jax.ShapeDtypeStruct(res.shape, F32),
        compiler_params=_params(),
    )(res, a, w)


def _matmul_nt_cast(dy, w, name):
    t = dy.shape[0]
    tm = _token_tile(t, wide=True)

    def body(dy_ref, w_ref, o_ref, dyb_ref):
        dyb = dy_ref[...].astype(BF16)
        dyb_ref[...] = dyb
        o_ref[...] = _dot_nt(dyb, w_ref[...]).astype(BF16)

    return pl.pallas_call(
        body, name=name, grid=(t // tm,),
        in_specs=[_rows(tm, dy.shape[1]), _resident(w.shape)],
        out_specs=[_rows(tm, w.shape[0]), _rows(tm, dy.shape[1])],
        out_shape=[jax.ShapeDtypeStruct((t, w.shape[0]), BF16), jax.ShapeDtypeStruct(dy.shape, BF16)],
        compiler_params=_params(),
    )(dy, w)


def _norm_input_bwd(dres, dz, w, h, g, name):
    t = h.shape[0]
    tm = _token_tile(t, wide=True)
    nb, _, width = w.shape

    def body(dres_ref, dz_ref, w_ref, h_ref, g_ref, dh_ref, n_ref, dg_ref):
        i = pl.program_id(0)
        hh = h_ref[...]
        gg = g_ref[...]
        r = _rstd(hh)
        n_ref[...] = (hh * r * gg).astype(BF16)
        dn = jnp.zeros((tm, D_MODEL), F32)
        for b in range(nb):
            dn = dn + _dot_nt(dz_ref[:, b * width:(b + 1) * width], w_ref[b])
        dh, dg_rows = _rms_bwd(dn, hh, r, gg)
        dh_ref[...] = dres_ref[...] + dh

        @pl.when(i == 0)
        def _():
            dg_ref[...] = jnp.zeros_like(dg_ref)

        dg_ref[...] += jnp.sum(dg_rows, axis=0, keepdims=True)

    return pl.pallas_call(
        body, name=name, grid=(t // tm,),
        in_specs=[_rows(tm, D_MODEL), _rows(tm, nb * width), _resident(w.shape), _rows(tm, D_MODEL),
                  _resident((1, D_MODEL))],
        out_specs=[_rows(tm, D_MODEL), _rows(tm, D_MODEL), pl.BlockSpec((1, D_MODEL), lambda i: (0, 0))],
        out_shape=[jax.ShapeDtypeStruct((t, D_MODEL), F32), jax.ShapeDtypeStruct((t, D_MODEL), BF16),
                   jax.ShapeDtypeStruct((1, D_MODEL), F32)],
        compiler_params=_params(),
    )(dres, dz, w, h, g)


def _head(h, p, target, gple, gfin, wg, wproj):
    t = h.shape[0]
    tm = _token_tile(t, wide=True)
    pw = D_MODEL // N_CHIPS

    def body(h_ref, p_ref, tgt_ref, gple_ref, gfin_ref, wg_ref, wproj_ref,
             loss_ref, dgf_ref, dgple_ref, dh_ref, dgp_ref, dpp_ref, n_ref, pb_ref):
        i = pl.program_id(0)
        hh = h_ref[...]
        r_in = _rstd(hh)
        gp = gple_ref[...]
        n = (hh * r_in * gp).astype(BF16)
        n_ref[...] = n
        gate = jax.nn.sigmoid(_dot(n, wg_ref[...]))
        pb = p_ref[...].astype(BF16)
        pb_ref[...] = pb
        pp = jnp.concatenate([_dot(pb, wproj_ref[b]) for b in range(N_CHIPS)], axis=1)
        h4 = hh + gate * pp
        r = _rstd(h4)
        gf = gfin_ref[...]
        err = h4 * r * gf - tgt_ref[...]
        dy = err * (1.0 / D_MODEL)
        dh4, dgf_rows = _rms_bwd(dy, h4, r, gf)
        dgp = (dh4 * pp * gate * (1.0 - gate)).astype(BF16)
        dgp_ref[...] = dgp
        dpp_ref[...] = (dh4 * gate).astype(BF16)
        dh, dgple_rows = _rms_bwd(_dot_nt(dgp, wg_ref[...]), hh, r_in, gp)
        dh_ref[...] = dh4 + dh

        @pl.when(i == 0)
        def _():
            loss_ref[...] = jnp.zeros_like(loss_ref)
            dgf_ref[...] = jnp.zeros_like(dgf_ref)
            dgple_ref[...] = jnp.zeros_like(dgple_ref)

        loss_ref[...] += (0.5 / D_MODEL) * jnp.sum(err * err)
        dgf_ref[...] += jnp.sum(dgf_rows, axis=0, keepdims=True)
        dgple_ref[...] += jnp.sum(dgple_rows, axis=0, keepdims=True)

    bf = lambda w: jax.ShapeDtypeStruct((t, w), BF16)
    const = lambda i: (0, 0)
    return pl.pallas_call(
        body, name="head", grid=(t // tm,),
        in_specs=[_rows(tm, D_MODEL), _rows(tm, PLE_DIM), _rows(tm, D_MODEL), _resident((1, D_MODEL)),
                  _resident((1, D_MODEL)), _resident(wg.shape), _resident(wproj.shape)],
        out_specs=[pl.BlockSpec((1, LANES), const), pl.BlockSpec((1, D_MODEL), const), pl.BlockSpec((1, D_MODEL), const),
                   _rows(tm, D_MODEL), _rows(tm, D_MODEL), _rows(tm, D_MODEL), _rows(tm, D_MODEL), _rows(tm, PLE_DIM)],
        out_shape=[jax.ShapeDtypeStruct((1, LANES), F32), jax.ShapeDtypeStruct((1, D_MODEL), F32),
                   jax.ShapeDtypeStruct((1, D_MODEL), F32), jax.ShapeDtypeStruct((t, D_MODEL), F32), bf(D_MODEL),
                   bf(D_MODEL), bf(D_MODEL), bf(PLE_DIM)],
        compiler_params=_params(),
    )(h, p, target, gple, gfin, wg, wproj)


_BIG = ("ffn1_w_in", "ffn1_w_out", "w_mix_in", "w_mix_out", "ffn2_w_in", "ffn2_w_out", "ple_w_gate", "ple_w_proj")
_SMALL = ("ffn1_norm", "mix_norm", "gmlp_v_norm", "gmlp_w_s", "gmlp_b", "ffn2_norm", "ple_norm", "final_norm")
_ALL = ("ffn1_norm", "ffn1_w_in", "ffn1_w_out", "mix_norm", "w_mix_in", "gmlp_v_norm", "gmlp_w_s", "gmlp_b", "w_mix_out",
        "ffn2_norm", "ffn2_w_in", "ffn2_w_out", "ple_norm", "ple_w_gate", "ple_w_proj", "final_norm")
_ANY = pl.BlockSpec(memory_space=pl.ANY)
_MESH = pl.DeviceIdType.MESH


def _mesh_pos():
    return lax.axis_index("x"), lax.axis_index("y"), lax.axis_index("c")


def _other_chips(x, y):
    return [((x, 1 - y), 2 * x + 1 - y), ((1 - x, y), 2 * (1 - x) + y), ((1 - x, 1 - y), 2 * (1 - x) + 1 - y)]


def _remote(src, dst, send_sem, recv_sem, device):
    return pltpu.make_async_remote_copy(src_ref=src, dst_ref=dst, send_sem=send_sem, recv_sem=recv_sem,
                                        device_id=device, device_id_type=_MESH)


class _WeightGather:
    def __init__(self, shards):
        self.shapes = [s.shape for s in shards]
        self.operands = list(shards)
        self.out_shape = [jax.ShapeDtypeStruct((N_CHIPS, *s.shape), s.dtype) for s in shards]
        n = len(shards)
        self.per = 2 * (N_CHIPS - 1)
        self.scratch = [pltpu.SemaphoreType.DMA((self.per * n,)), pltpu.SemaphoreType.DMA((self.per * n,)),
                        pltpu.SemaphoreType.DMA((n,))]
        self.phases = [self.send, self.forward, self.finish]

    def _copies(self, ins, outs, sems):
        send_sems, recv_sems, local_sems = sems
        x, y, c = _mesh_pos()
        sibling = (x, y, 1 - c)
        mine = 2 * x + y
        local, first, landing, passed, arriving = [], [], [], [], []
        for w, shape in enumerate(self.shapes):
            hr = shape[0] // 2
            half = lambda blk, cc, w=w, hr=hr: outs[w].at[blk, pl.ds(cc * hr, hr), :]
            local.append(pltpu.make_async_copy(ins[w], outs[w].at[mine], local_sems.at[w]))
            for k, (chip, blk) in enumerate(_other_chips(x, y)):
                s = self.per * w + k
                first.append(_remote(ins[w].at[pl.ds(c * hr, hr), :], half(mine, c), send_sems.at[s], recv_sems.at[s],
                                     (*chip, c)))
                landing.append(_remote(half(blk, c), half(blk, c), send_sems.at[s], recv_sems.at[s], sibling))
                s = self.per * w + N_CHIPS - 1 + k
                passed.append(_remote(half(blk, c), half(blk, c), send_sems.at[s], recv_sems.at[s], sibling))
                arriving.append(_remote(half(blk, 1 - c), half(blk, 1 - c), send_sems.at[s], recv_sems.at[s], sibling))
        return local, first, landing, passed, arriving

    def send(self, ins, outs, sems):
        local, first, _, _, _ = self._copies(ins, outs, sems)
        for cp in local + first:
            cp.start()

    def forward(self, ins, outs, sems):
        _, _, landing, passed, _ = self._copies(ins, outs, sems)
        for landed, cp in zip(landing, passed):
            landed.wait_recv()
            cp.start()

    def finish(self, ins, outs, sems):
        local, first, _, passed, arriving = self._copies(ins, outs, sems)
        for cp in arriving:
            cp.wait_recv()
        for cp in first + passed:
            cp.wait_send()
        for cp in local:
            cp.wait()


class _ChipExchange:
    def __init__(self, sums):
        n = len(sums)
        self.n = n
        self.per = N_CHIPS - 1
        self.operands = list(sums)
        self.out_shape = [jax.ShapeDtypeStruct((self.per, *s.shape[1:]), s.dtype) for s in sums]
        self.scratch = [pltpu.SemaphoreType.DMA((self.per * n,)), pltpu.SemaphoreType.DMA((self.per * n,))]
        self.phases = [self.send, self.finish]

    def _copies(self, ins, outs, sems):
        send_sems, recv_sems = sems
        x, y, c = _mesh_pos()
        cps = []
        for w in range(self.n):
            for k, (chip, _) in enumerate(_other_chips(x, y)):
                s = self.per * w + k
                cps.append(_remote(ins[w].at[k + 1], outs[w].at[k], send_sems.at[s], recv_sems.at[s], (*chip, c)))
        return cps

    def send(self, ins, outs, sems):
        for cp in self._copies(ins, outs, sems):
            cp.start()

    def finish(self, ins, outs, sems):
        for cp in self._copies(ins, outs, sems):
            cp.wait()


def _run_exchange(ex, name):
    n_in, n_out = len(ex.operands), len(ex.out_shape)

    def body(*refs):
        ins, outs, sems = refs[:n_in], refs[n_in:n_in + n_out], refs[n_in + n_out:]
        for phase in ex.phases:
            phase(ins, outs, sems)

    return pl.pallas_call(body, name=name, in_specs=[_ANY] * n_in, out_specs=[_ANY] * n_out, out_shape=ex.out_shape,
                          scratch_shapes=ex.scratch)(*ex.operands)


def _call(body, args, *, name, grid, in_specs, out_specs, out_shape, scratch_shapes=(), exchange=None, steps=None,
          aliases=None):
    params = _params(len(grid))
    if exchange is None:
        out = pl.pallas_call(body, name=name, grid=grid, in_specs=in_specs, out_specs=out_specs, out_shape=out_shape,
                             scratch_shapes=list(scratch_shapes), input_output_aliases=aliases or {},
                             compiler_params=params)(*args)
        return out, None
    n_in, n_out, n_scr = len(in_specs), len(out_specs), len(scratch_shapes)
    n_xin, n_xout = len(exchange.operands), len(exchange.out_shape)
    assert len(steps) == len(exchange.phases)

    def hosting(*refs):
        cuts = [n_in, n_xin, n_out, n_xout, n_scr]
        parts, at = [], 0
        for size in cuts:
            parts.append(refs[at:at + size])
            at += size
        ins, xins, outs, xouts, scr = parts
        sems = refs[at:]
        step = 0
        for axis, size in enumerate(grid):
            step = step * size + pl.program_id(axis)
        pl.when(step == steps[0])(lambda: exchange.phases[0](xins, xouts, sems))
        body(*ins, *outs, *scr)
        for at_step, phase in zip(steps[1:], exchange.phases[1:]):
            pl.when(step == at_step)(functools.partial(phase, xins, xouts, sems))

    out = pl.pallas_call(
        hosting, name=name, grid=grid,
        in_specs=list(in_specs) + [_ANY] * n_xin, out_specs=list(out_specs) + [_ANY] * n_xout,
        out_shape=list(out_shape) + list(exchange.out_shape),
        scratch_shapes=list(scratch_shapes) + list(exchange.scratch), input_output_aliases=aliases or {},
        compiler_params=params,
    )(*args, *exchange.operands)
    return out[:n_out], out[n_out:]


class _PairExchange:
    def __init__(self, parts):
        self.halves = [g.shape[1] // 2 for g in parts]
        self.operands = list(parts)
        self.out_shape = [jax.ShapeDtypeStruct((g.shape[0], g.shape[1] // 2, g.shape[2]), g.dtype) for g in parts]
        n = len(parts)
        self.scratch = [pltpu.SemaphoreType.DMA((n,)), pltpu.SemaphoreType.DMA((n,))]
        self.phases = [self.send, self.finish]

    def _copies(self, ins, outs, sems):
        send_sems, recv_sems = sems
        x, y, c = _mesh_pos()
        return [_remote(ins[w].at[:, pl.ds((1 - c) * hr, hr), :], outs[w], send_sems.at[w], recv_sems.at[w], (x, y, 1 - c))
                for w, hr in enumerate(self.halves)]

    def send(self, ins, outs, sems):
        for cp in self._copies(ins, outs, sems):
            cp.start()

    def finish(self, ins, outs, sems):
        for cp in self._copies(ins, outs, sems):
            cp.wait()


SUM_STEPS = 2


def _pair_sums(gs, sibling, pos, name):
    n = len(gs)

    def body(pos_ref, *refs):
        for k in range(n):
            refs[2 * n + k][...] = (refs[k][...] + refs[n + k][...].astype(F32)).astype(BF16)

    g_specs, a_specs, o_specs, out_shape = [], [], [], []
    for g in gs:
        nb, r, c = g.shape
        tr = r // 2 // SUM_STEPS
        g_specs.append(pl.BlockSpec((1, tr, c), lambda k, i, pos: (k ^ pos[0], pos[1] * SUM_STEPS + i, 0)))
        a_specs.append(pl.BlockSpec((1, tr, c), lambda k, i, pos: (k ^ pos[0], i, 0)))
        o_specs.append(pl.BlockSpec((1, tr, c), lambda k, i, pos: (k, i, 0)))
        out_shape.append(jax.ShapeDtypeStruct((nb, r // 2, c), BF16))
    return pl.pallas_call(
        body, name=name,
        grid_spec=pltpu.PrefetchScalarGridSpec(num_scalar_prefetch=1, grid=(N_CHIPS, SUM_STEPS),
                                               in_specs=g_specs + a_specs, out_specs=o_specs),
        out_shape=out_shape, compiler_params=_params(2),
    )(pos, *gs, *sibling)


def _chip_sums(pairs, others, pos, name):
    n = len(pairs)

    def body(pos_ref, *refs):
        for k in range(n):
            s_ref, b_ref = refs[k], refs[n + k]
            refs[2 * n + k][...] = ((s_ref[0].astype(F32) + b_ref[0].astype(F32))
                                    + (b_ref[1].astype(F32) + b_ref[2].astype(F32)))

    s_specs, b_specs, o_specs, out_shape = [], [], [], []
    for s in pairs:
        _, hr, c = s.shape
        tr = hr // SUM_STEPS
        s_specs.append(pl.BlockSpec((1, tr, c), lambda i, pos: (0, i, 0)))
        b_specs.append(pl.BlockSpec((N_CHIPS - 1, tr, c), lambda i, pos: (0, i, 0)))
        o_specs.append(pl.BlockSpec((tr, c), lambda i, pos: (pos[1] * SUM_STEPS + i, 0)))
        out_shape.append(jax.ShapeDtypeStruct((2 * hr, c), F32))
    return pl.pallas_call(
        body, name=name,
        grid_spec=pltpu.PrefetchScalarGridSpec(num_scalar_prefetch=1, grid=(SUM_STEPS,),
                                               in_specs=s_specs + b_specs, out_specs=o_specs),
        out_shape=out_shape, compiler_params=_params(),
    )(pos, *pairs, *others)


def _pair_share(grads, name):
    n = len(grads)

    def body(*refs):
        outs = refs[n:2 * n]
        send_sems, recv_sems = refs[2 * n:]
        x, y, c = _mesh_pos()
        cps = []
        for w in range(n):
            hr = grads[w].shape[0] // 2
            rows = outs[w].at[pl.ds(c * hr, hr), :]
            cp = _remote(rows, rows, send_sems.at[w], recv_sems.at[w], (x, y, 1 - c))
            cp.start()
            cps.append(cp)
        for w, cp in enumerate(cps):
            cp.wait_send()
            hr = grads[w].shape[0] // 2
            other = outs[w].at[pl.ds((1 - c) * hr, hr), :]
            _remote(other, other, send_sems.at[w], recv_sems.at[w], (x, y, 1 - c)).wait_recv()

    return pl.pallas_call(
        body, name=name,
        in_specs=[_ANY] * n, out_specs=[_ANY] * n,
        out_shape=[jax.ShapeDtypeStruct(g.shape, g.dtype) for g in grads],
        input_output_aliases={w: w for w in range(n)},
        scratch_shapes=[pltpu.SemaphoreType.DMA((n,)), pltpu.SemaphoreType.DMA((n,))],
    )(*grads)


def _all_reduce_small(rows, mats):
    n_sems = 2 * N_CHIPS
    n_rows = -(-len(rows) // 8) * 8
    heights = [math.prod(a.shape[:-1]) for a in mats]
    n_tall = -(-sum(heights) // 8) * 8
    arrays = list(rows) + list(mats)

    def body(*refs):
        ins, outs = refs[:len(arrays)], refs[len(arrays):2 * len(arrays)]
        wide, tall, wide_pair, tall_pair, wide_slots, tall_slots, send_sems, recv_sems = refs[2 * len(arrays):]
        x, y, c = _mesh_pos()
        wide[...] = jnp.zeros_like(wide)
        tall[...] = jnp.zeros_like(tall)
        for k, a in enumerate(rows):
            wide[k:k + 1, 0:a.shape[1]] = ins[k][...]
        at = 0
        for k, h in enumerate(heights):
            tall[at:at + h, :] = ins[len(rows) + k][...].reshape(h, LANES)
            at += h
        wide_pair[c] = wide[...]
        tall_pair[c] = tall[...]
        cps = [_remote(wide, wide_pair.at[c], send_sems.at[0], recv_sems.at[0], (x, y, 1 - c)),
               _remote(tall, tall_pair.at[c], send_sems.at[1], recv_sems.at[1], (x, y, 1 - c))]
        for cp in cps:
            cp.start()
        for cp in cps:
            cp.wait()
        wide[...] = wide_pair[0] + wide_pair[1]
        tall[...] = tall_pair[0] + tall_pair[1]
        mine = 2 * x + y
        wide_slots[mine] = wide[...]
        tall_slots[mine] = tall[...]
        cps = []
        for k, (chip, _) in enumerate(_other_chips(x, y)):
            for j, (buf, slots) in enumerate(((wide, wide_slots), (tall, tall_slots))):
                s = 2 + 2 * k + j
                cps.append(_remote(buf, slots.at[mine], send_sems.at[s], recv_sems.at[s], (*chip, c)))
        for cp in cps:
            cp.start()
        for cp in cps:
            cp.wait()
        wide_sum, tall_sum = wide_slots[0], tall_slots[0]
        for d in range(1, N_CHIPS):
            wide_sum = wide_sum + wide_slots[d]
            tall_sum = tall_sum + tall_slots[d]
        for k, a in enumerate(rows):
            outs[k][...] = wide_sum[k:k + 1, 0:a.shape[1]]
        at = 0
        for k, h in enumerate(heights):
            outs[len(rows) + k][...] = tall_sum[at:at + h, :].reshape(mats[k].shape)
            at += h

    vmem = pl.BlockSpec(memory_space=pltpu.VMEM)
    return pl.pallas_call(
        body, name="all_reduce_small",
        in_specs=[vmem] * len(arrays), out_specs=[vmem] * len(arrays),
        out_shape=[jax.ShapeDtypeStruct(a.shape, F32) for a in arrays],
        scratch_shapes=[pltpu.VMEM((n_rows, D_MODEL), F32), pltpu.VMEM((n_tall, LANES), F32),
                        pltpu.VMEM((2, n_rows, D_MODEL), F32), pltpu.VMEM((2, n_tall, LANES), F32),
                        pltpu.VMEM((N_CHIPS, n_rows, D_MODEL), F32), pltpu.VMEM((N_CHIPS, n_tall, LANES), F32),
                        pltpu.SemaphoreType.DMA((n_sems,)), pltpu.SemaphoreType.DMA((n_sems,))],
    )(*arrays)


def _adamw_small(items):
    n = len(items)
    bias1 = 1.0 - ADAM_B1 ** ADAM_STEP
    bias2 = 1.0 - ADAM_B2 ** ADAM_STEP

    def body(*refs):
        ins, outs = refs[:4 * n], refs[4 * n:]
        for k in range(n):
            w_ref, g_ref, m_ref, v_ref = ins[4 * k:4 * k + 4]
            gg = g_ref[...]
            m2 = ADAM_B1 * m_ref[...] + (1.0 - ADAM_B1) * gg
            v2 = ADAM_B2 * v_ref[...] + (1.0 - ADAM_B2) * (gg * gg)
            outs[3 * k + 1][...] = m2
            outs[3 * k + 2][...] = v2
            outs[3 * k][...] = -ADAM_LR * ((m2 / bias1) / (jnp.sqrt(v2 / bias2) + ADAM_EPS) + ADAM_WD * w_ref[...])

    vmem = pl.BlockSpec(memory_space=pltpu.VMEM)
    out = pl.pallas_call(
        body, name="adamw_small", in_specs=[vmem] * (4 * n), out_specs=[vmem] * (3 * n),
        out_shape=[jax.ShapeDtypeStruct(w.shape, F32) for w, _, _, _ in items for _ in range(3)],
    )(*[a for item in items for a in item])
    return [tuple(out[3 * k:3 * k + 3]) for k in range(n)]


ADAMW_STEPS = 8


def _adamw(items, name, exchange=None):
    n = len(items)
    bias1 = 1.0 - ADAM_B1 ** ADAM_STEP
    bias2 = 1.0 - ADAM_B2 ** ADAM_STEP

    def body(*refs):
        ins, outs = refs[:4 * n], refs[4 * n:]
        for k in range(n):
            w_ref, g_ref, m_ref, v_ref = ins[4 * k:4 * k + 4]
            d_ref, mo_ref, vo_ref = outs[3 * k:3 * k + 3]
            gg = g_ref[...]
            m2 = ADAM_B1 * m_ref[...] + (1.0 - ADAM_B1) * gg
            v2 = ADAM_B2 * v_ref[...] + (1.0 - ADAM_B2) * (gg * gg)
            mo_ref[...] = m2
            vo_ref[...] = v2
            d_ref[...] = -ADAM_LR * ((m2 / bias1) / (jnp.sqrt(v2 / bias2) + ADAM_EPS) + ADAM_WD * w_ref[...])

    in_specs, out_specs, out_shape, args = [], [], [], []
    for w, g, m, v in items:
        r, c = w.shape
        steps = ADAMW_STEPS if r % (8 * ADAMW_STEPS) == 0 else 1
        assert steps == ADAMW_STEPS or n == 1
        spec = pl.BlockSpec((r // steps, c), lambda i: (i, 0))
        in_specs += [spec] * 4
        out_specs += [spec] * 3
        out_shape += [jax.ShapeDtypeStruct((r, c), F32)] * 3
        args += [w, g, m, v]
    out, got = _call(body, args, name=name, grid=(steps,), in_specs=in_specs, out_specs=out_specs, out_shape=out_shape,
                     exchange=exchange, steps=(0, steps - 1))
    return [tuple(out[3 * k:3 * k + 3]) for k in range(n)], got


def kernel(x, p, ffn1_norm, ffn1_w_in, ffn1_w_out, mix_norm, w_mix_in, gmlp_v_norm, gmlp_w_s, gmlp_b, w_mix_out, ffn2_norm, ffn2_w_in, ffn2_w_out, ple_norm, ple_w_gate, ple_w_proj, final_norm, loss_target, m_ffn1_norm, m_ffn1_w_in, m_ffn1_w_out, m_mix_norm, m_w_mix_in, m_gmlp_v_norm, m_gmlp_w_s, m_gmlp_b, m_w_mix_out, m_ffn2_norm, m_ffn2_w_in, m_ffn2_w_out, m_ple_norm, m_ple_w_gate, m_ple_w_proj, m_final_norm, v_ffn1_norm, v_ffn1_w_in, v_ffn1_w_out, v_mix_norm, v_w_mix_in, v_gmlp_v_norm, v_gmlp_w_s, v_gmlp_b, v_w_mix_out, v_ffn2_norm, v_ffn2_w_in, v_ffn2_w_out, v_ple_norm, v_ple_w_gate, v_ple_w_proj, v_final_norm):
    args = dict(locals())
    w = {n: args[n] for n in _ALL}
    m = {n: args["m_" + n] for n in _ALL}
    v = {n: args["v_" + n] for n in _ALL}
    xi, yi, ci = _mesh_pos()
    pos = jnp.stack([2 * xi + yi, ci]).astype(jnp.int32)
    shard = {n: w[n][0] for n in _BIG}
    cast = {n: shard[n].astype(BF16) for n in _BIG}
    small = {n: (w[n][0] if w[n].ndim > 2 else w[n].reshape(1, -1)) for n in _SMALL}
    bt = small["gmlp_b"].T
    g_small, pair, from_chips = {}, {}, {}

    def pair_reduce(partials, tag, host=None):
        names = list(partials)
        blocks = lambda a, n: a.reshape(N_CHIPS, *shard[n].shape)
        exchange = _PairExchange([blocks(partials[n][1], n) for n in names])
        result, got = host(exchange) if host else (None, _run_exchange(exchange, "grad_pair_exchange_" + tag))
        pair.update(zip(names, _pair_sums([blocks(partials[n][0], n) for n in names], got, pos, "pair_sums_" + tag)))
        return names, result

    w1in, w1out = _run_exchange(_WeightGather([cast["ffn1_w_in"], cast["ffn1_w_out"]]), "gather_ffn1")
    w1out = w1out.reshape(D_FF, D_MODEL)
    (h1, gu1), (wmix, wmo) = _ffn_fwd(x[0], small["ffn1_norm"], w1in, w1out, "ffn1_fwd",
                                      _WeightGather([cast["w_mix_in"], cast["w_mix_out"]]))
    wmo = wmo.reshape(D_MODEL, D_MODEL)
    zg, qkv = _mix_in_fwd(h1, small["mix_norm"], wmix)
    gm = _gmlp_fwd(zg, small["gmlp_v_norm"], small["gmlp_w_s"], bt)
    (mixed, carries), (w2in, w2out, wg, wproj) = _attn_fwd(
        qkv, gm, _WeightGather([cast["ffn2_w_in"], cast["ffn2_w_out"], cast["ple_w_gate"], cast["ple_w_proj"]]))
    w2out = w2out.reshape(D_FF, D_MODEL)
    wg = wg.reshape(D_MODEL, D_MODEL)
    h2 = _matmul_residual(h1, mixed, wmo, "mix_out_fwd")
    (h3, gu2), _ = _ffn_fwd(h2, small["ffn2_norm"], w2in, w2out, "ffn2_fwd")
    loss_part, g_small["final_norm"], g_small["ple_norm"], dh3, dgp, dpp, n4, pb = _head(
        h3, p[0, 0], loss_target[0], small["ple_norm"], small["final_norm"], wg, wproj)

    part = {"ple_w_gate": _wgrad_rows(n4, dgp, N_CHIPS, "wgrad_ple_gate"),
            "ple_w_proj": _wgrad_cols(pb, dpp, N_CHIPS, "wgrad_ple_proj")}
    (dh2, dgu2, n3, act2, dhh3, g_small["ffn2_norm"]), _ = _ffn_bwd(dh3, h2, small["ffn2_norm"], gu2, w2in, w2out,
                                                                   "ffn2_bwd")
    part["ffn2_w_in"] = _wgrad_cols(n3, dgu2, N_CHIPS, "wgrad_ffn2_in")
    part["ffn2_w_out"] = _wgrad_rows(act2, dhh3, 2, "wgrad_ffn2_out")
    dmixed, dh2b = _matmul_nt_cast(dh2, wmo, "mix_out_bwd")
    part["w_mix_out"] = _wgrad_rows(mixed, dh2b, 2, "wgrad_mix_out")
    group, (dzg, g_small["gmlp_w_s"], dbt, g_small["gmlp_v_norm"]) = pair_reduce(
        part, "late", lambda ex: _gmlp_bwd(zg, dmixed, small["gmlp_v_norm"], small["gmlp_w_s"], bt, ex))
    g_small["gmlp_b"] = dbt.T
    (dq, dk, dv), got = _attn_bwd(qkv, dmixed, carries, _ChipExchange([pair[n] for n in group]))
    from_chips.update(zip(group, got))

    dzmix = jnp.concatenate([dzg, dq, dk, dv], axis=1)
    dh1, n2, g_small["mix_norm"] = _norm_input_bwd(dh2, dzmix, wmix, h1, small["mix_norm"], "mix_in_bwd")
    group, _ = pair_reduce({"w_mix_in": _wgrad_cols(n2, dzmix, N_CHIPS, "wgrad_mix_in")}, "mix")
    (dx, dgu1, n1, act1, dhh1, g_small["ffn1_norm"]), _ = _ffn_bwd(dh1, x[0], small["ffn1_norm"], gu1, w1in, w1out,
                                                                   "ffn1_bwd")

    g_out, got = _wgrad_rows(act1, dhh1, 2, "wgrad_ffn1_out", _ChipExchange([pair[n] for n in group]))
    from_chips.update(zip(group, got))
    pair_reduce({"ffn1_w_out": g_out}, "out")
    g_in, got = _wgrad_cols(n1, dgu1, N_CHIPS, "wgrad_ffn1_in", _ChipExchange([pair["ffn1_w_out"]]))
    from_chips["ffn1_w_out"] = got[0]
    pair_reduce({"ffn1_w_in": g_in}, "in")

    def finish(names, tag, exchange=None):
        halves = _chip_sums([pair[n] for n in names], [from_chips[n] for n in names], pos, "chip_sums_" + tag)
        full = _pair_share(halves, "grad_pair_share_" + tag)
        out, got = _adamw([(shard[n], g, m[n][0], v[n][0]) for n, g in zip(names, full)], "adamw_" + tag, exchange)
        for n, g, (d2, m2, v2) in zip(names, full, out):
            grads[n], delta[n], new_m[n], new_v[n] = g[None], d2[None], m2[None], v2[None]
        return got

    grads, delta, new_m, new_v = {}, {}, {}, {}
    got = finish([n for n in _BIG if n != "ffn1_w_in"], "most", _ChipExchange([pair["ffn1_w_in"]]))
    from_chips["ffn1_w_in"] = got[0]
    finish(["ffn1_w_in"], "last")

    rows = [n for n in _SMALL if g_small[n].shape[0] == 1]
    mats = [n for n in _SMALL if n not in rows]
    summed = _all_reduce_small([g_small[n] for n in rows] + [loss_part], [g_small[n] for n in mats])
    loss = summed[len(rows)][0, 0]
    g_sum = dict(zip(rows + mats, summed[:len(rows)] + summed[len(rows) + 1:]))
    like = lambda a, n: a[n].reshape(small[n].shape)
    out = _adamw_small([(small[n], g_sum[n], like(m, n), like(v, n)) for n in _SMALL])
    for n, (d2, m2, v2) in zip(_SMALL, out):
        grads[n], delta[n], new_m[n], new_v[n] = (a.reshape(w[n].shape) for a in (g_sum[n], d2, m2, v2))

    return (loss, dx[None], *[grads[n] for n in _ALL], *[delta[n] for n in _ALL], *[new_m[n] for n in _ALL],
            *[new_v[n] for n in _ALL])
```

```python
import functools
import math

import jax
import jax.numpy as jnp
from jax import lax
from jax.experimental import pallas as pl
from jax.experimental.pallas import tpu as pltpu

F32, BF16 = jnp.float32, jnp.bfloat16

D_MODEL = 1024
D_FF = 2816
FF_BLOCK = 2 * D_FF // 4
PLE_DIM = 256
CHUNK = 128
GM_HEADS = 4
GM_WIDTH = 512
SB_HEAD_DIM = 64
SB_WIDTH = 512
MIX_IN_WIDTH = 2 * GM_WIDTH + 3 * SB_WIDTH
MIX_BLOCK = MIX_IN_WIDTH // 4
EPS = 1e-6
N_CHIPS = 4
LANES = 128
ATT_BLOCK = 128
ATT_Q = 512
VMEM_LIMIT = 56 * 1024 * 1024

ADAM_LR, ADAM_B1, ADAM_B2, ADAM_EPS, ADAM_WD, ADAM_STEP = 0.001, 0.9, 0.999, 1e-08, 0.01, 10


def _dot(a, b):
    return jnp.dot(a, b, preferred_element_type=F32)


def _dot_nt(a, b):
    return lax.dot_general(a, b, (((1,), (1,)), ((), ())), preferred_element_type=F32)


def _dot_tn(a, b):
    return lax.dot_general(a, b, (((0,), (0,)), ((), ())), preferred_element_type=F32)


def _resident(shape):
    nd = len(shape)
    return pl.BlockSpec(shape, lambda *_: (0,) * nd, pipeline_mode=pl.Buffered(1))


def _rows(tm, width):
    return pl.BlockSpec((tm, width), lambda i: (i, 0))


def _params(n_axes=1):
    return pltpu.CompilerParams(dimension_semantics=("arbitrary",) * n_axes, vmem_limit_bytes=VMEM_LIMIT)


def _rstd(h):
    return lax.rsqrt(jnp.mean(h * h, axis=-1, keepdims=True) + EPS)


def _rms_bwd(dy, h, r, g):
    dyg = dy * g
    dh = r * dyg - h * (r * r * r) * jnp.mean(dyg * h, axis=-1, keepdims=True)
    return dh, dy * h * r


def _gelu(x):
    return 0.5 * x * (1.0 + lax.erf(x * (2.0 ** -0.5)))


def _gelu_grad(x):
    return 0.5 * (1.0 + lax.erf(x * (2.0 ** -0.5))) + x * jnp.exp(-0.5 * x * x) * ((2.0 * jnp.pi) ** -0.5)


def _token_tile(t, wide=False):
    return min(512 if wide else 256, t)


def _ffn_fwd(h, g, win, wout, name, exchange=None):
    t = h.shape[0]
    tm = _token_tile(t, wide=True)

    def body(h_ref, g_ref, win_ref, wout_ref, ho_ref, gu_ref):
        hh = h_ref[...]
        n = (hh * _rstd(hh) * g_ref[...]).astype(BF16)
        acc = jnp.zeros((tm, D_MODEL), F32)
        for jb in range(2):
            gate = _dot(n, win_ref[jb])
            up = _dot(n, win_ref[2 + jb])
            gu_ref[:, jb * FF_BLOCK:(jb + 1) * FF_BLOCK] = gate.astype(BF16)
            gu_ref[:, D_FF + jb * FF_BLOCK:D_FF + (jb + 1) * FF_BLOCK] = up.astype(BF16)
            act = (gate * jax.nn.sigmoid(gate) * up).astype(BF16)
            acc = acc + _dot(act, wout_ref[jb * FF_BLOCK:(jb + 1) * FF_BLOCK, :])
        ho_ref[...] = hh + 0.5 * acc

    n = t // tm
    return _call(
        body, (h, g, win, wout), name=name, grid=(n,),
        in_specs=[_rows(tm, D_MODEL), _resident((1, D_MODEL)), _resident(win.shape), _resident(wout.shape)],
        out_specs=[_rows(tm, D_MODEL), _rows(tm, 2 * D_FF)],
        out_shape=[jax.ShapeDtypeStruct((t, D_MODEL), F32), jax.ShapeDtypeStruct((t, 2 * D_FF), BF16)],
        exchange=exchange, steps=(0, (2 * n) // 3, n - 1))


def _ffn_bwd(dho, h, g, gu, win, wout, name, exchange=None):
    t = h.shape[0]
    tm = _token_tile(t)

    def body(dho_ref, h_ref, g_ref, gu_ref, win_ref, wout_ref, dh_ref, dgu_ref, n_ref, act_ref, dhh_ref, dg_ref):
        i = pl.program_id(0)
        hh = h_ref[...]
        gg = g_ref[...]
        r = _rstd(hh)
        n_ref[...] = (hh * r * gg).astype(BF16)
        dho = dho_ref[...]
        dhh = (0.5 * dho).astype(BF16)
        dhh_ref[...] = dhh
        dn = jnp.zeros((tm, D_MODEL), F32)
        for jb in range(2):
            cg = slice(jb * FF_BLOCK, (jb + 1) * FF_BLOCK)
            cu = slice(D_FF + jb * FF_BLOCK, D_FF + (jb + 1) * FF_BLOCK)
            dact = _dot_nt(dhh, wout_ref[cg, :])
            gate = gu_ref[:, cg].astype(F32)
            up = gu_ref[:, cu].astype(F32)
            sg = jax.nn.sigmoid(gate)
            silu = gate * sg
            act_ref[:, cg] = (silu * up).astype(BF16)
            dgate = (dact * up * (sg * (1.0 + gate * (1.0 - sg)))).astype(BF16)
            dup = (dact * silu).astype(BF16)
            dgu_ref[:, cg] = dgate
            dgu_ref[:, cu] = dup
            dn = dn + _dot_nt(dgate, win_ref[jb]) + _dot_nt(dup, win_ref[2 + jb])
        dh, dg_rows = _rms_bwd(dn, hh, r, gg)
        dh_ref[...] = dho + dh

        @pl.when(i == 0)
        def _():
            dg_ref[...] = jnp.zeros_like(dg_ref)

        dg_ref[...] += jnp.sum(dg_rows, axis=0, keepdims=True)

    n = t // tm
    return _call(
        body, (dho, h, g, gu, win, wout), name=name, grid=(n,),
        in_specs=[_rows(tm, D_MODEL), _rows(tm, D_MODEL), _resident((1, D_MODEL)), _rows(tm, 2 * D_FF),
                  _resident(win.shape), _resident(wout.shape)],
        out_specs=[_rows(tm, D_MODEL), _rows(tm, 2 * D_FF), _rows(tm, D_MODEL), _rows(tm, D_FF), _rows(tm, D_MODEL),
                   pl.BlockSpec((1, D_MODEL), lambda i: (0, 0))],
        out_shape=[jax.ShapeDtypeStruct((t, D_MODEL), F32), jax.ShapeDtypeStruct((t, 2 * D_FF), BF16),
                   jax.ShapeDtypeStruct((t, D_MODEL), BF16), jax.ShapeDtypeStruct((t, D_FF), BF16),
                   jax.ShapeDtypeStruct((t, D_MODEL), BF16), jax.ShapeDtypeStruct((1, D_MODEL), F32)],
        exchange=exchange, steps=(0, n - 1))


def _wgrad(a, b, out_shape, out_block, out_index, a_width, b_width, grid_ij, name, exchange=None):
    t = a.shape[0]
    tk = t if 4 * t * (a_width + b_width) <= 26 * 2 ** 20 else min(2048, t)
    nk = t // tk

    def body(a_ref, b_ref, o_ref, ob_ref):
        k = pl.program_id(2)
        prod = _dot_tn(a_ref[...], b_ref[...]).reshape(o_ref.shape)

        @pl.when(k == 0)
        def _():
            o_ref[...] = prod

        @pl.when(k > 0)
        def _():
            o_ref[...] += prod

        @pl.when(k == nk - 1)
        def _():
            ob_ref[...] = o_ref[...].astype(BF16)

    grid = (*grid_ij, nk)
    out_spec = pl.BlockSpec(out_block, lambda i, j, k: out_index(i, j))
    out, got = _call(
        body, (a, b), name=name, grid=grid,
        in_specs=[pl.BlockSpec((tk, a_width), lambda i, j, k: (k, i)), pl.BlockSpec((tk, b_width), lambda i, j, k: (k, j))],
        out_specs=[out_spec, out_spec],
        out_shape=[jax.ShapeDtypeStruct(out_shape, F32), jax.ShapeDtypeStruct(out_shape, BF16)],
        exchange=exchange, steps=(0, grid[0] * grid[1] * grid[2] - 1))
    return tuple(out) if exchange is None else (tuple(out), got)


def _wgrad_cols(a, b, n_blocks, name, exchange=None):
    ka, nb = a.shape[1], b.shape[1] // n_blocks
    return _wgrad(a, b, (n_blocks, ka, nb), (1, ka, nb), lambda i, j: (j, 0, 0), ka, nb, (1, n_blocks), name, exchange)


def _wgrad_rows(a, b, n_blocks, name, exchange=None):
    ka, nb = a.shape[1] // n_blocks, b.shape[1]
    return _wgrad(a, b, (a.shape[1], nb), (ka, nb), lambda i, j: (i, 0), ka, nb, (n_blocks, 1), name, exchange)


def _mix_in_fwd(h, g, wmix):
    t = h.shape[0]
    tm = _token_tile(t, wide=True)
    gw2 = 2 * GM_WIDTH

    def body(h_ref, g_ref, w_ref, zg_ref, qkv_ref):
        hh = h_ref[...]
        n = (hh * _rstd(hh) * g_ref[...]).astype(BF16)
        for b in range(N_CHIPS):
            z = _dot(n, w_ref[b])
            lo, hi = b * MIX_BLOCK, (b + 1) * MIX_BLOCK
            if hi <= gw2:
                zg_ref[:, lo:hi] = z
            elif lo >= gw2:
                qkv_ref[:, lo - gw2:hi - gw2] = z.astype(BF16)
            else:
                zg_ref[:, lo:gw2] = z[:, :gw2 - lo]
                qkv_ref[:, 0:hi - gw2] = z[:, gw2 - lo:].astype(BF16)

    return pl.pallas_call(
        body, name="mix_in_fwd", grid=(t // tm,),
        in_specs=[_rows(tm, D_MODEL), _resident((1, D_MODEL)), _resident(wmix.shape)],
        out_specs=[_rows(tm, gw2), _rows(tm, 3 * SB_WIDTH)],
        out_shape=[jax.ShapeDtypeStruct((t, gw2), F32), jax.ShapeDtypeStruct((t, 3 * SB_WIDTH), BF16)],
        compiler_params=_params(),
    )(h, g, wmix)


def _causal_chunk_mask():
    row = lax.broadcasted_iota(jnp.int32, (CHUNK, CHUNK), 0)
    col = lax.broadcasted_iota(jnp.int32, (CHUNK, CHUNK), 1)
    return row >= col


def _gmlp_tile(t):
    return min(512, t)


def _gmlp_fwd(zg, gv, ws, bt):
    t = zg.shape[0]
    tm = _gmlp_tile(t)

    def body(zg_ref, gv_ref, ws_ref, bt_ref, o_ref):
        u = _gelu(zg_ref[:, :GM_WIDTH])
        v = _gelu(zg_ref[:, GM_WIDTH:])
        vn = (v * _rstd(v) * gv_ref[...]).astype(BF16)
        mask = _causal_chunk_mask()
        for hd in range(GM_HEADS):
            wm = jnp.where(mask, ws_ref[hd], 0.0).astype(BF16)
            cols = slice(hd * CHUNK, (hd + 1) * CHUNK)
            for c in range(tm // CHUNK):
                rows = slice(c * CHUNK, (c + 1) * CHUNK)
                sv = _dot(wm, vn[rows, cols]) + bt_ref[:, hd:hd + 1]
                o_ref[rows, cols] = (u[rows, cols] * sv).astype(BF16)

    return pl.pallas_call(
        body, name="gmlp_fwd", grid=(t // tm,),
        in_specs=[_rows(tm, 2 * GM_WIDTH), _resident((1, GM_WIDTH)), _resident(ws.shape), _resident(bt.shape)],
        out_specs=_rows(tm, GM_WIDTH),
        out_shape=jax.ShapeDtypeStruct((t, D_MODEL), BF16),
        compiler_params=_params(),
    )(zg, gv, ws, bt)


def _gmlp_bwd(zg, dmixed, gv, ws, bt, exchange=None):
    t = zg.shape[0]
    tm = _gmlp_tile(t)

    def body(zg_ref, dgm_ref, gv_ref, ws_ref, bt_ref, dzg_ref, dws_ref, dbt_ref, dgv_ref):
        i = pl.program_id(0)

        @pl.when(i == 0)
        def _():
            dws_ref[...] = jnp.zeros_like(dws_ref)
            dbt_ref[...] = jnp.zeros_like(dbt_ref)
            dgv_ref[...] = jnp.zeros_like(dgv_ref)

        zu = zg_ref[:, :GM_WIDTH]
        zv = zg_ref[:, GM_WIDTH:]
        u = _gelu(zu)
        v = _gelu(zv)
        r = _rstd(v)
        gvv = gv_ref[...]
        vn = (v * r * gvv).astype(BF16)
        dgm = dgm_ref[...].astype(F32)
        dsv = (dgm * u).astype(BF16)
        mask = _causal_chunk_mask()
        du_cols, dvn_cols = [], []
        for hd in range(GM_HEADS):
            wm = jnp.where(mask, ws_ref[hd], 0.0).astype(BF16)
            cols = slice(hd * CHUNK, (hd + 1) * CHUNK)
            dw = jnp.zeros((CHUNK, CHUNK), F32)
            db = jnp.zeros((CHUNK, 1), F32)
            du_rows, dvn_rows = [], []
            for c in range(tm // CHUNK):
                rows = slice(c * CHUNK, (c + 1) * CHUNK)
                sv = _dot(wm, vn[rows, cols]) + bt_ref[:, hd:hd + 1]
                du_rows.append(dgm[rows, cols] * sv)
                dvn_rows.append(_dot_tn(wm, dsv[rows, cols]))
                dw = dw + _dot_nt(dsv[rows, cols], vn[rows, cols])
                db = db + jnp.sum(dsv[rows, cols].astype(F32), axis=1, keepdims=True)
            dws_ref[hd] += jnp.where(mask, dw, 0.0)
            dbt_ref[:, hd:hd + 1] += db
            du_cols.append(jnp.concatenate(du_rows, axis=0))
            dvn_cols.append(jnp.concatenate(dvn_rows, axis=0))
        du = jnp.concatenate(du_cols, axis=1)
        dvn = jnp.concatenate(dvn_cols, axis=1)
        dv, dgv_rows = _rms_bwd(dvn, v, r, gvv)
        dgv_ref[...] += jnp.sum(dgv_rows, axis=0, keepdims=True)
        dzg_ref[:, :GM_WIDTH] = (du * _gelu_grad(zu)).astype(BF16)
        dzg_ref[:, GM_WIDTH:] = (dv * _gelu_grad(zv)).astype(BF16)

    const = lambda nd: (lambda i: (0,) * nd)
    n = t // tm
    return _call(
        body, (zg, dmixed, gv, ws, bt), name="gmlp_bwd", grid=(n,),
        in_specs=[_rows(tm, 2 * GM_WIDTH), _rows(tm, GM_WIDTH), _resident((1, GM_WIDTH)), _resident(ws.shape),
                  _resident(bt.shape)],
        out_specs=[_rows(tm, 2 * GM_WIDTH), pl.BlockSpec(ws.shape, const(3)), pl.BlockSpec(bt.shape, const(2)),
                   pl.BlockSpec((1, GM_WIDTH), const(2))],
        out_shape=[jax.ShapeDtypeStruct((t, 2 * GM_WIDTH), BF16), jax.ShapeDtypeStruct(ws.shape, F32),
                   jax.ShapeDtypeStruct(bt.shape, F32), jax.ShapeDtypeStruct((1, GM_WIDTH), F32)],
        exchange=exchange, steps=(0, n - 1))


def _att_masks():
    tb = ATT_BLOCK
    lane = lax.broadcasted_iota(jnp.int32, (1, LANES), 1)
    rj = lax.broadcasted_iota(jnp.int32, (2 * tb, 2 * tb), 0)
    cs = lax.broadcasted_iota(jnp.int32, (2 * tb, 2 * tb), 1)
    same_head = ((rj < tb) & (cs < tb)) | ((rj >= tb) & (cs >= tb))
    suffix = jnp.where(same_head & (rj >= cs), 1.0, 0.0).astype(BF16)
    prefix = jnp.where(same_head & (rj <= cs), 1.0, 0.0).astype(BF16)
    left = lax.broadcasted_iota(jnp.int32, (1, 2 * tb), 1) < tb
    tq = lax.broadcasted_iota(jnp.int32, (ATT_Q, 4 * tb), 0)
    ts = lax.broadcasted_iota(jnp.int32, (ATT_Q, 4 * tb), 1)
    key = jnp.where(ts < 2 * tb, ts & (tb - 1), (ts & (tb - 1)) + tb)
    return lane, suffix, prefix, left, key, tq


def _att_fill(k_ref, v_ref, kcat, vcat, n_blocks, lane):
    tb = ATT_BLOCK
    first = lane < SB_HEAD_DIM

    def fill(jb, carry):
        rows = pl.ds(pl.multiple_of(jb * tb, tb), tb)
        top = pl.ds(pl.multiple_of(jb * 2 * tb, tb), tb)
        bot = pl.ds(pl.multiple_of(jb * 2 * tb + tb, tb), tb)
        kb = k_ref[rows, :]
        vb = v_ref[rows, :]
        zero = jnp.zeros_like(kb)
        kcat[top, :] = jnp.where(first, kb, zero)
        kcat[bot, :] = jnp.where(first, zero, kb)
        vcat[top, :] = jnp.where(first, vb, zero)
        vcat[bot, :] = jnp.where(first, zero, vb)
        return carry

    lax.fori_loop(0, n_blocks, fill, 0)


def _block_sums(x, m):
    return _dot(x.astype(BF16), m)


def _softplus2(z2):
    return jnp.maximum(z2, 0.0) + jnp.log2(1.0 + jnp.exp2(-jnp.abs(z2)))


def _scaled_queries(q_ref, base2):
    scale = SB_HEAD_DIM ** -0.5
    return (q_ref[...].astype(F32) * (scale * math.log2(math.e) if base2 else scale)).astype(BF16)


def _att_specs(t):
    n_pairs = SB_WIDTH // LANES
    q_spec = pl.BlockSpec((ATT_Q, LANES), lambda p, i: (i, p))
    k_spec = pl.BlockSpec((t, LANES), lambda p, i: (0, n_pairs + p))
    v_spec = pl.BlockSpec((t, LANES), lambda p, i: (0, 2 * n_pairs + p))
    return n_pairs, q_spec, k_spec, v_spec


def _attn_fwd(qkv, mixed, exchange=None):
    t = qkv.shape[0]
    tb = ATT_BLOCK
    nkb = t // tb
    assert 2 * nkb <= LANES and t % ATT_Q == 0 and ATT_Q == 4 * tb
    n_pairs, q_spec, k_spec, v_spec = _att_specs(t)

    def body(q_ref, k_ref, v_ref, mixed_ref, o_ref, ct_ref, kcat, vcat, acc, carry, z0, r0, z1, r1):
        i = pl.program_id(1)
        lane, suffix, _, left, key, tq = _att_masks()

        @pl.when(i == 0)
        def _():
            _att_fill(k_ref, v_ref, kcat, vcat, nkb, lane)

        q = _scaled_queries(q_ref, True)
        acc[...] = jnp.zeros_like(acc)
        carry[...] = jnp.zeros_like(carry)
        ct_ref[0] = jnp.zeros((ATT_Q, LANES), F32)

        def key_rows(m):
            return pl.ds(pl.multiple_of(m * 4 * tb, 4 * tb), 4 * tb)

        def scores(m, zb, rb, causal=None, rs=slice(None)):
            z = _dot_nt(q[rs], kcat[key_rows(m), :])
            zb[rs, :] = z
            sp = _softplus2(z)
            if causal is not None:
                sp = jnp.where(causal[rs], sp, 0.0)
            for g in (1, 0):
                cols = slice(g * 2 * tb, (g + 1) * 2 * tb)
                rb[rs, cols] = _block_sums(sp[:, cols], suffix)

        def weigh(m, zb, rb, causal=None, rs=slice(None)):
            probs = [None, None]
            for g in (1, 0):
                cols = slice(g * 2 * tb, (g + 1) * 2 * tb)
                j = 2 * m + g
                r = rb[rs, cols]
                c = carry[rs, :]
                ct_ref[0, rs, :] = jnp.where(lane == j, c[:, :tb], jnp.where(lane == nkb + j, c[:, tb:], ct_ref[0, rs, :]))
                a = jnp.exp2(zb[rs, cols] - (r + c))
                if causal is not None:
                    a = jnp.where(causal[rs, cols], a, 0.0)
                probs[g] = a.astype(BF16)
                carry[rs, :] = c + jnp.where(left, r[:, 0:1], r[:, tb:tb + 1])
            acc[rs, :] += _dot(jnp.concatenate(probs, axis=1), vcat[key_rows(m), :])

        sooner, later, late_rows = key < tq, key + 2 * tb < tq, slice(2 * tb, 4 * tb)
        scores(2 * i + 1, z1, r1, later, late_rows)
        scores(2 * i, z0, r0, sooner)
        weigh(2 * i + 1, z1, r1, later, late_rows)
        weigh(2 * i, z0, r0, sooner)

        @pl.when(i > 0)
        def _():
            scores(2 * i - 1, z1, r1)

            def loop(k, c):
                u = i - 1 - k
                scores(2 * u, z0, r0)
                weigh(2 * u + 1, z1, r1)
                scores(2 * u - 1, z1, r1)
                weigh(2 * u, z0, r0)
                return c

            lax.fori_loop(0, i - 1, loop, 0)
            scores(0, z0, r0)
            weigh(1, z1, r1)
            weigh(0, z0, r0)

        o_ref[...] = acc[...].astype(BF16)

    tile = pltpu.VMEM((ATT_Q, 4 * tb), F32)

    nq = t // ATT_Q
    return _call(
        body, (qkv, qkv, qkv, mixed), name="attn_fwd", grid=(n_pairs, nq),
        in_specs=[q_spec, k_spec, v_spec, _ANY],
        out_specs=[pl.BlockSpec((ATT_Q, LANES), lambda p, i: (i, GM_WIDTH // LANES + p)),
                   pl.BlockSpec((1, ATT_Q, LANES), lambda p, i: (p, i, 0))],
        out_shape=[jax.ShapeDtypeStruct(mixed.shape, BF16), jax.ShapeDtypeStruct((n_pairs, t, LANES), F32)],
        aliases={3: 0},
        scratch_shapes=[pltpu.VMEM((2 * t, LANES), BF16), pltpu.VMEM((2 * t, LANES), BF16),
                        pltpu.VMEM((ATT_Q, LANES), F32), pltpu.VMEM((ATT_Q, 2 * tb), F32), tile, tile, tile, tile],
        exchange=exchange, steps=(0, (n_pairs - 1) * nq - 1, n_pairs * nq - 1))


def _attn_bwd(qkv, dmixed, carries, exchange=None):
    t = qkv.shape[0]
    tb = ATT_BLOCK
    nkb = t // tb
    nq = t // ATT_Q
    scale = SB_HEAD_DIM ** -0.5
    n_pairs, q_spec, k_spec, v_spec = _att_specs(t)
    gm_blocks = GM_WIDTH // LANES

    def body(q_ref, k_ref, v_ref, do_ref, ct_ref, dq_ref, dk_ref, dv_ref, kcat, vcat, dkacc, dvacc, dqacc, carry,
             z0, r0, s0, a0, z1, r1, s1, a1):
        i = pl.program_id(1)
        lane, suffix, prefix, left, key, tq = _att_masks()
        first = lane < SB_HEAD_DIM

        @pl.when(i == 0)
        def _():
            _att_fill(k_ref, v_ref, kcat, vcat, nkb, lane)
            dkacc[...] = jnp.zeros_like(dkacc)
            dvacc[...] = jnp.zeros_like(dvacc)

        q2 = _scaled_queries(q_ref, True)
        q = _scaled_queries(q_ref, False)
        do = do_ref[...]
        dqacc[...] = jnp.zeros_like(dqacc)
        carry[...] = jnp.zeros_like(carry)

        def key_rows(m):
            return pl.ds(pl.multiple_of(m * 4 * tb, 4 * tb), 4 * tb)

        def front(m, bufs, causal=None, rs=slice(None)):
            zb, rb, sb, ab = bufs
            z = _dot_nt(q2[rs], kcat[key_rows(m), :])
            zb[rs, :] = z
            sp = _softplus2(z)
            sb[rs, :] = jnp.exp2(z - sp)
            if causal is not None:
                sp = jnp.where(causal[rs], sp, 0.0)
            for g in (0, 1):
                cols = slice(g * 2 * tb, (g + 1) * 2 * tb)
                rb[rs, cols] = _block_sums(sp[:, cols], suffix)
            ab[rs, :] = _dot_nt(do[rs], vcat[key_rows(m), :])

        def back(m, bufs, causal=None, rs=slice(None)):
            zb, rb, sb, ab = bufs
            dzs, probs = [None, None], [None, None]
            for g in (0, 1):
                cols = slice(g * 2 * tb, (g + 1) * 2 * tb)
                j = 2 * m + g
                ct = ct_ref[0, rs, :]
                ca = jnp.sum(jnp.where(lane == j, ct, 0.0), axis=1, keepdims=True)
                cb = jnp.sum(jnp.where(lane == nkb + j, ct, 0.0), axis=1, keepdims=True)
                a = jnp.exp2(zb[rs, cols] - (rb[rs, cols] + jnp.where(left, ca, cb)))
                if causal is not None:
                    a = jnp.where(causal[rs, cols], a, 0.0)
                de = ab[rs, cols] * a
                cl = _block_sums(de, prefix)
                pre = carry[rs, :]
                dz = de - sb[rs, cols] * (cl + pre)
                if causal is not None:
                    dz = jnp.where(causal[rs, cols], dz, 0.0)
                carry[rs, :] = pre + jnp.where(left, cl[:, tb - 1:tb], cl[:, 2 * tb - 1:2 * tb])
                dzs[g] = dz.astype(BF16)
                probs[g] = a.astype(BF16)
            dzb = jnp.concatenate(dzs, axis=1)
            dqacc[rs, :] += _dot(dzb, kcat[key_rows(m), :])
            dkc = _dot_tn(dzb, q[rs])
            dvc = _dot_tn(jnp.concatenate(probs, axis=1), do[rs])
            out_rows = pl.ds(pl.multiple_of(m * 2 * tb, 2 * tb), 2 * tb)
            pick = lambda x: jnp.concatenate([jnp.where(first, x[0:tb], x[tb:2 * tb]),
                                              jnp.where(first, x[2 * tb:3 * tb], x[3 * tb:4 * tb])], axis=0)
            dkacc[out_rows, :] += pick(dkc)
            dvacc[out_rows, :] += pick(dvc)

        b0, b1 = (z0, r0, s0, a0), (z1, r1, s1, a1)

        @pl.when(i > 0)
        def _():
            front(0, b0)

            def loop(u, c):
                front(2 * u + 1, b1)
                back(2 * u, b0)
                front(2 * u + 2, b0)
                back(2 * u + 1, b1)
                return c

            lax.fori_loop(0, i - 1, loop, 0)
            front(2 * i - 1, b1)
            back(2 * i - 2, b0)
            back(2 * i - 1, b1)

        sooner, later, late_rows = key < tq, key + 2 * tb < tq, slice(2 * tb, 4 * tb)
        front(2 * i, b0, sooner)
        front(2 * i + 1, b1, later, late_rows)
        back(2 * i, b0, sooner)
        back(2 * i + 1, b1, later, late_rows)

        dq_ref[...] = (dqacc[...] * scale).astype(BF16)

        @pl.when(i == nq - 1)
        def _():
            dk_ref[...] = dkacc[...].astype(BF16)
            dv_ref[...] = dvacc[...].astype(BF16)

    col = pl.BlockSpec((t, LANES), lambda p, i: (0, p))
    out = jax.ShapeDtypeStruct((t, SB_WIDTH), BF16)
    tile = pltpu.VMEM((ATT_Q, 4 * tb), F32)
    return _call(
        body, (qkv, qkv, qkv, dmixed, carries), name="attn_bwd", grid=(n_pairs, nq),
        in_specs=[q_spec, k_spec, v_spec, pl.BlockSpec((ATT_Q, LANES), lambda p, i: (i, gm_blocks + p)),
                  pl.BlockSpec((1, ATT_Q, LANES), lambda p, i: (p, i, 0))],
        out_specs=[pl.BlockSpec((ATT_Q, LANES), lambda p, i: (i, p)), col, col],
        out_shape=[out, out, out],
        scratch_shapes=[pltpu.VMEM((2 * t, LANES), BF16), pltpu.VMEM((2 * t, LANES), BF16),
                        pltpu.VMEM((t, LANES), F32), pltpu.VMEM((t, LANES), F32),
                        pltpu.VMEM((ATT_Q, LANES), F32), pltpu.VMEM((ATT_Q, 2 * tb), F32)] + [tile] * 8,
        exchange=exchange, steps=(0, n_pairs * nq - 1))


def _matmul_residual(res, a, w, name):
    t = a.shape[0]
    tm = _token_tile(t, wide=True)

    def body(res_ref, a_ref, w_ref, o_ref):
        o_ref[...] = res_ref[...] + _dot(a_ref[...], w_ref[...])

    return pl.pallas_call(
        body, name=name, grid=(t // tm,),
        in_specs=[_rows(tm, res.shape[1]), _rows(tm, a.shape[1]), _resident(w.shape)],
        out_specs=_rows(tm, res.shape[1]),
        out_shape=jax.ShapeDtypeStruct(res.shape, F32),
        compiler_params=_params(),
    )(res, a, w)


def _matmul_nt_cast(dy, w, name):
    t = dy.shape[0]
    tm = _token_tile(t, wide=True)

    def body(dy_ref, w_ref, o_ref, dyb_ref):
        dyb = dy_ref[...].astype(BF16)
        dyb_ref[...] = dyb
        o_ref[...] = _dot_nt(dyb, w_ref[...]).astype(BF16)

    return pl.pallas_call(
        body, name=name, grid=(t // tm,),
        in_specs=[_rows(tm, dy.shape[1]), _resident(w.shape)],
        out_specs=[_rows(tm, w.shape[0]), _rows(tm, dy.shape[1])],
        out_shape=[jax.ShapeDtypeStruct((t, w.shape[0]), BF16), jax.ShapeDtypeStruct(dy.shape, BF16)],
        compiler_params=_params(),
    )(dy, w)


def _norm_input_bwd(dres, dz, w, h, g, name):
    t = h.shape[0]
    tm = _token_tile(t, wide=True)
    nb, _, width = w.shape

    def body(dres_ref, dz_ref, w_ref, h_ref, g_ref, dh_ref, n_ref, dg_ref):
        i = pl.program_id(0)
        hh = h_ref[...]
        gg = g_ref[...]
        r = _rstd(hh)
        n_ref[...] = (hh * r * gg).astype(BF16)
        dn = jnp.zeros((tm, D_MODEL), F32)
        for b in range(nb):
            dn = dn + _dot_nt(dz_ref[:, b * width:(b + 1) * width], w_ref[b])
        dh, dg_rows = _rms_bwd(dn, hh, r, gg)
        dh_ref[...] = dres_ref[...] + dh

        @pl.when(i == 0)
        def _():
            dg_ref[...] = jnp.zeros_like(dg_ref)

        dg_ref[...] += jnp.sum(dg_rows, axis=0, keepdims=True)

    return pl.pallas_call(
        body, name=name, grid=(t // tm,),
        in_specs=[_rows(tm, D_MODEL), _rows(tm, nb * width), _resident(w.shape), _rows(tm, D_MODEL),
                  _resident((1, D_MODEL))],
        out_specs=[_rows(tm, D_MODEL), _rows(tm, D_MODEL), pl.BlockSpec((1, D_MODEL), lambda i: (0, 0))],
        out_shape=[jax.ShapeDtypeStruct((t, D_MODEL), F32), jax.ShapeDtypeStruct((t, D_MODEL), BF16),
                   jax.ShapeDtypeStruct((1, D_MODEL), F32)],
        compiler_params=_params(),
    )(dres, dz, w, h, g)


def _head(h, p, target, gple, gfin, wg, wproj):
    t = h.shape[0]
    tm = _token_tile(t, wide=True)
    pw = D_MODEL // N_CHIPS

    def body(h_ref, p_ref, tgt_ref, gple_ref, gfin_ref, wg_ref, wproj_ref,
             loss_ref, dgf_ref, dgple_ref, dh_ref, dgp_ref, dpp_ref, n_ref, pb_ref):
        i = pl.program_id(0)
        hh = h_ref[...]
        r_in = _rstd(hh)
        gp = gple_ref[...]
        n = (hh * r_in * gp).astype(BF16)
        n_ref[...] = n
        gate = jax.nn.sigmoid(_dot(n, wg_ref[...]))
        pb = p_ref[...].astype(BF16)
        pb_ref[...] = pb
        pp = jnp.concatenate([_dot(pb, wproj_ref[b]) for b in range(N_CHIPS)], axis=1)
        h4 = hh + gate * pp
        r = _rstd(h4)
        gf = gfin_ref[...]
        err = h4 * r * gf - tgt_ref[...]
        dy = err * (1.0 / D_MODEL)
        dh4, dgf_rows = _rms_bwd(dy, h4, r, gf)
        dgp = (dh4 * pp * gate * (1.0 - gate)).astype(BF16)
        dgp_ref[...] = dgp
        dpp_ref[...] = (dh4 * gate).astype(BF16)
        dh, dgple_rows = _rms_bwd(_dot_nt(dgp, wg_ref[...]), hh, r_in, gp)
        dh_ref[...] = dh4 + dh

        @pl.when(i == 0)
        def _():
            loss_ref[...] = jnp.zeros_like(loss_ref)
            dgf_ref[...] = jnp.zeros_like(dgf_ref)
            dgple_ref[...] = jnp.zeros_like(dgple_ref)

        loss_ref[...] += (0.5 / D_MODEL) * jnp.sum(err * err)
        dgf_ref[...] += jnp.sum(dgf_rows, axis=0, keepdims=True)
        dgple_ref[...] += jnp.sum(dgple_rows, axis=0, keepdims=True)

    bf = lambda w: jax.ShapeDtypeStruct((t, w), BF16)
    const = lambda i: (0, 0)
    return pl.pallas_call(
        body, name="head", grid=(t // tm,),
        in_specs=[_rows(tm, D_MODEL), _rows(tm, PLE_DIM), _rows(tm, D_MODEL), _resident((1, D_MODEL)),
                  _resident((1, D_MODEL)), _resident(wg.shape), _resident(wproj.shape)],
        out_specs=[pl.BlockSpec((1, LANES), const), pl.BlockSpec((1, D_MODEL), const), pl.BlockSpec((1, D_MODEL), const),
                   _rows(tm, D_MODEL), _rows(tm, D_MODEL), _rows(tm, D_MODEL), _rows(tm, D_MODEL), _rows(tm, PLE_DIM)],
        out_shape=[jax.ShapeDtypeStruct((1, LANES), F32), jax.ShapeDtypeStruct((1, D_MODEL), F32),
                   jax.ShapeDtypeStruct((1, D_MODEL), F32), jax.ShapeDtypeStruct((t, D_MODEL), F32), bf(D_MODEL),
                   bf(D_MODEL), bf(D_MODEL), bf(PLE_DIM)],
        compiler_params=_params(),
    )(h, p, target, gple, gfin, wg, wproj)


_BIG = ("ffn1_w_in", "ffn1_w_out", "w_mix_in", "w_mix_out", "ffn2_w_in", "ffn2_w_out", "ple_w_gate", "ple_w_proj")
_SMALL = ("ffn1_norm", "mix_norm", "gmlp_v_norm", "gmlp_w_s", "gmlp_b", "ffn2_norm", "ple_norm", "final_norm")
_ALL = ("ffn1_norm", "ffn1_w_in", "ffn1_w_out", "mix_norm", "w_mix_in", "gmlp_v_norm", "gmlp_w_s", "gmlp_b", "w_mix_out",
        "ffn2_norm", "ffn2_w_in", "ffn2_w_out", "ple_norm", "ple_w_gate", "ple_w_proj", "final_norm")
_ANY = pl.BlockSpec(memory_space=pl.ANY)
_MESH = pl.DeviceIdType.MESH


def _mesh_pos():
    return lax.axis_index("x"), lax.axis_index("y"), lax.axis_index("c")


def _other_chips(x, y):
    return [((x, 1 - y), 2 * x + 1 - y), ((1 - x, y), 2 * (1 - x) + y), ((1 - x, 1 - y), 2 * (1 - x) + 1 - y)]


def _remote(src, dst, send_sem, recv_sem, device):
    return pltpu.make_async_remote_copy(src_ref=src, dst_ref=dst, send_sem=send_sem, recv_sem=recv_sem,
                                        device_id=device, device_id_type=_MESH)


class _WeightGather:
    def __init__(self, shards):
        self.shapes = [s.shape for s in shards]
        self.operands = list(shards)
        self.out_shape = [jax.ShapeDtypeStruct((N_CHIPS, *s.shape), s.dtype) for s in shards]
        n = len(shards)
        self.per = 2 * (N_CHIPS - 1)
        self.scratch = [pltpu.SemaphoreType.DMA((self.per * n,)), pltpu.SemaphoreType.DMA((self.per * n,)),
                        pltpu.SemaphoreType.DMA((n,))]
        self.phases = [self.send, self.forward, self.finish]

    def _copies(self, ins, outs, sems):
        send_sems, recv_sems, local_sems = sems
        x, y, c = _mesh_pos()
        sibling = (x, y, 1 - c)
        mine = 2 * x + y
        local, first, landing, passed, arriving = [], [], [], [], []
        for w, shape in enumerate(self.shapes):
            hr = shape[0] // 2
            half = lambda blk, cc, w=w, hr=hr: outs[w].at[blk, pl.ds(cc * hr, hr), :]
            local.append(pltpu.make_async_copy(ins[w], outs[w].at[mine], local_sems.at[w]))
            for k, (chip, blk) in enumerate(_other_chips(x, y)):
                s = self.per * w + k
                first.append(_remote(ins[w].at[pl.ds(c * hr, hr), :], half(mine, c), send_sems.at[s], recv_sems.at[s],
                                     (*chip, c)))
                landing.append(_remote(half(blk, c), half(blk, c), send_sems.at[s], recv_sems.at[s], sibling))
                s = self.per * w + N_CHIPS - 1 + k
                passed.append(_remote(half(blk, c), half(blk, c), send_sems.at[s], recv_sems.at[s], sibling))
                arriving.append(_remote(half(blk, 1 - c), half(blk, 1 - c), send_sems.at[s], recv_sems.at[s], sibling))
        return local, first, landing, passed, arriving

    def send(self, ins, outs, sems):
        local, first, _, _, _ = self._copies(ins, outs, sems)
        for cp in local + first:
            cp.start()

    def forward(self, ins, outs, sems):
        _, _, landing, passed, _ = self._copies(ins, outs, sems)
        for landed, cp in zip(landing, passed):
            landed.wait_recv()
            cp.start()

    def finish(self, ins, outs, sems):
        local, first, _, passed, arriving = self._copies(ins, outs, sems)
        for cp in arriving:
            cp.wait_recv()
        for cp in first + passed:
            cp.wait_send()
        for cp in local:
            cp.wait()


class _ChipExchange:
    def __init__(self, sums):
        n = len(sums)
        self.n = n
        self.per = N_CHIPS - 1
        self.operands = list(sums)
        self.out_shape = [jax.ShapeDtypeStruct((self.per, *s.shape[1:]), s.dtype) for s in sums]
        self.scratch = [pltpu.SemaphoreType.DMA((self.per * n,)), pltpu.SemaphoreType.DMA((self.per * n,))]
        self.phases = [self.send, self.finish]

    def _copies(self, ins, outs, sems):
        send_sems, recv_sems = sems
        x, y, c = _mesh_pos()
        cps = []
        for w in range(self.n):
            for k, (chip, _) in enumerate(_other_chips(x, y)):
                s = self.per * w + k
                cps.append(_remote(ins[w].at[k + 1], outs[w].at[k], send_sems.at[s], recv_sems.at[s], (*chip, c)))
        return cps

    def send(self, ins, outs, sems):
        for cp in self._copies(ins, outs, sems):
            cp.start()

    def finish(self, ins, outs, sems):
        for cp in self._copies(ins, outs, sems):
            cp.wait()


def _run_exchange(ex, name):
    n_in, n_out = len(ex.operands), len(ex.out_shape)

    def body(*refs):
        ins, outs, sems = refs[:n_in], refs[n_in:n_in + n_out], refs[n_in + n_out:]
        for phase in ex.phases:
            phase(ins, outs, sems)

    return pl.pallas_call(body, name=name, in_specs=[_ANY] * n_in, out_specs=[_ANY] * n_out, out_shape=ex.out_shape,
                          scratch_shapes=ex.scratch)(*ex.operands)


def _call(body, args, *, name, grid, in_specs, out_specs, out_shape, scratch_shapes=(), exchange=None, steps=None,
          aliases=None):
    params = _params(len(grid))
    if exchange is None:
        out = pl.pallas_call(body, name=name, grid=grid, in_specs=in_specs, out_specs=out_specs, out_shape=out_shape,
                             scratch_shapes=list(scratch_shapes), input_output_aliases=aliases or {},
                             compiler_params=params)(*args)
        return out, None
    n_in, n_out, n_scr = len(in_specs), len(out_specs), len(scratch_shapes)
    n_xin, n_xout = len(exchange.operands), len(exchange.out_shape)
    assert len(steps) == len(exchange.phases)

    def hosting(*refs):
        cuts = [n_in, n_xin, n_out, n_xout, n_scr]
        parts, at = [], 0
        for size in cuts:
            parts.append(refs[at:at + size])
            at += size
        ins, xins, outs, xouts, scr = parts
        sems = refs[at:]
        step = 0
        for axis, size in enumerate(grid):
            step = step * size + pl.program_id(axis)
        pl.when(step == steps[0])(lambda: exchange.phases[0](xins, xouts, sems))
        body(*ins, *outs, *scr)
        for at_step, phase in zip(steps[1:], exchange.phases[1:]):
            pl.when(step == at_step)(functools.partial(phase, xins, xouts, sems))

    out = pl.pallas_call(
        hosting, name=name, grid=grid,
        in_specs=list(in_specs) + [_ANY] * n_xin, out_specs=list(out_specs) + [_ANY] * n_xout,
        out_shape=list(out_shape) + list(exchange.out_shape),
        scratch_shapes=list(scratch_shapes) + list(exchange.scratch), input_output_aliases=aliases or {},
        compiler_params=params,
    )(*args, *exchange.operands)
    return out[:n_out], out[n_out:]


class _PairExchange:
    def __init__(self, parts):
        self.halves = [g.shape[1] // 2 for g in parts]
        self.operands = list(parts)
        self.out_shape = [jax.ShapeDtypeStruct((g.shape[0], g.shape[1] // 2, g.shape[2]), g.dtype) for g in parts]
        n = len(parts)
        self.scratch = [pltpu.SemaphoreType.DMA((n,)), pltpu.SemaphoreType.DMA((n,))]
        self.phases = [self.send, self.finish]

    def _copies(self, ins, outs, sems):
        send_sems, recv_sems = sems
        x, y, c = _mesh_pos()
        return [_remote(ins[w].at[:, pl.ds((1 - c) * hr, hr), :], outs[w], send_sems.at[w], recv_sems.at[w], (x, y, 1 - c))
                for w, hr in enumerate(self.halves)]

    def send(self, ins, outs, sems):
        for cp in self._copies(ins, outs, sems):
            cp.start()

    def finish(self, ins, outs, sems):
        for cp in self._copies(ins, outs, sems):
            cp.wait()


SUM_STEPS = 2


def _pair_sums(gs, sibling, pos, name):
    n = len(gs)

    def body(pos_ref, *refs):
        for k in range(n):
            refs[2 * n + k][...] = (refs[k][...] + refs[n + k][...].astype(F32)).astype(BF16)

    g_specs, a_specs, o_specs, out_shape = [], [], [], []
    for g in gs:
        nb, r, c = g.shape
        tr = r // 2 // SUM_STEPS
        g_specs.append(pl.BlockSpec((1, tr, c), lambda k, i, pos: (k ^ pos[0], pos[1] * SUM_STEPS + i, 0)))
        a_specs.append(pl.BlockSpec((1, tr, c), lambda k, i, pos: (k ^ pos[0], i, 0)))
        o_specs.append(pl.BlockSpec((1, tr, c), lambda k, i, pos: (k, i, 0)))
        out_shape.append(jax.ShapeDtypeStruct((nb, r // 2, c), BF16))
    return pl.pallas_call(
        body, name=name,
        grid_spec=pltpu.PrefetchScalarGridSpec(num_scalar_prefetch=1, grid=(N_CHIPS, SUM_STEPS),
                                               in_specs=g_specs + a_specs, out_specs=o_specs),
        out_shape=out_shape, compiler_params=_params(2),
    )(pos, *gs, *sibling)


def _chip_sums(pairs, others, pos, name):
    n = len(pairs)

    def body(pos_ref, *refs):
        for k in range(n):
            s_ref, b_ref = refs[k], refs[n + k]
            refs[2 * n + k][...] = ((s_ref[0].astype(F32) + b_ref[0].astype(F32))
                                    + (b_ref[1].astype(F32) + b_ref[2].astype(F32)))

    s_specs, b_specs, o_specs, out_shape = [], [], [], []
    for s in pairs:
        _, hr, c = s.shape
        tr = hr // SUM_STEPS
        s_specs.append(pl.BlockSpec((1, tr, c), lambda i, pos: (0, i, 0)))
        b_specs.append(pl.BlockSpec((N_CHIPS - 1, tr, c), lambda i, pos: (0, i, 0)))
        o_specs.append(pl.BlockSpec((tr, c), lambda i, pos: (pos[1] * SUM_STEPS + i, 0)))
        out_shape.append(jax.ShapeDtypeStruct((2 * hr, c), F32))
    return pl.pallas_call(
        body, name=name,
        grid_spec=pltpu.PrefetchScalarGridSpec(num_scalar_prefetch=1, grid=(SUM_STEPS,),
                                               in_specs=s_specs + b_specs, out_specs=o_specs),
        out_shape=out_shape, compiler_params=_params(),
    )(pos, *pairs, *others)


def _pair_share(grads, name):
    n = len(grads)

    def body(*refs):
        outs = refs[n:2 * n]
        send_sems, recv_sems = refs[2 * n:]
        x, y, c = _mesh_pos()
        cps = []
        for w in range(n):
            hr = grads[w].shape[0] // 2
            rows = outs[w].at[pl.ds(c * hr, hr), :]
            cp = _remote(rows, rows, send_sems.at[w], recv_sems.at[w], (x, y, 1 - c))
            cp.start()
            cps.append(cp)
        for w, cp in enumerate(cps):
            cp.wait_send()
            hr = grads[w].shape[0] // 2
            other = outs[w].at[pl.ds((1 - c) * hr, hr), :]
            _remote(other, other, send_sems.at[w], recv_sems.at[w], (x, y, 1 - c)).wait_recv()

    return pl.pallas_call(
        body, name=name,
        in_specs=[_ANY] * n, out_specs=[_ANY] * n,
        out_shape=[jax.ShapeDtypeStruct(g.shape, g.dtype) for g in grads],
        input_output_aliases={w: w for w in range(n)},
        scratch_shapes=[pltpu.SemaphoreType.DMA((n,)), pltpu.SemaphoreType.DMA((n,))],
    )(*grads)


def _all_reduce_small(rows, mats):
    n_sems = 2 * N_CHIPS
    n_rows = -(-len(rows) // 8) * 8
    heights = [math.prod(a.shape[:-1]) for a in mats]
    n_tall = -(-sum(heights) // 8) * 8
    arrays = list(rows) + list(mats)

    def body(*refs):
        ins, outs = refs[:len(arrays)], refs[len(arrays):2 * len(arrays)]
        wide, tall, wide_pair, tall_pair, wide_slots, tall_slots, send_sems, recv_sems = refs[2 * len(arrays):]
        x, y, c = _mesh_pos()
        wide[...] = jnp.zeros_like(wide)
        tall[...] = jnp.zeros_like(tall)
        for k, a in enumerate(rows):
            wide[k:k + 1, 0:a.shape[1]] = ins[k][...]
        at = 0
        for k, h in enumerate(heights):
            tall[at:at + h, :] = ins[len(rows) + k][...].reshape(h, LANES)
            at += h
        wide_pair[c] = wide[...]
        tall_pair[c] = tall[...]
        cps = [_remote(wide, wide_pair.at[c], send_sems.at[0], recv_sems.at[0], (x, y, 1 - c)),
               _remote(tall, tall_pair.at[c], send_sems.at[1], recv_sems.at[1], (x, y, 1 - c))]
        for cp in cps:
            cp.start()
        for cp in cps:
            cp.wait()
        wide[...] = wide_pair[0] + wide_pair[1]
        tall[...] = tall_pair[0] + tall_pair[1]
        mine = 2 * x + y
        wide_slots[mine] = wide[...]
        tall_slots[mine] = tall[...]
        cps = []
        for k, (chip, _) in enumerate(_other_chips(x, y)):
            for j, (buf, slots) in enumerate(((wide, wide_slots), (tall, tall_slots))):
                s = 2 + 2 * k + j
                cps.append(_remote(buf, slots.at[mine], send_sems.at[s], recv_sems.at[s], (*chip, c)))
        for cp in cps:
            cp.start()
        for cp in cps:
            cp.wait()
        wide_sum, tall_sum = wide_slots[0], tall_slots[0]
        for d in range(1, N_CHIPS):
            wide_sum = wide_sum + wide_slots[d]
            tall_sum = tall_sum + tall_slots[d]
        for k, a in enumerate(rows):
            outs[k][...] = wide_sum[k:k + 1, 0:a.shape[1]]
        at = 0
        for k, h in enumerate(heights):
            outs[len(rows) + k][...] = tall_sum[at:at + h, :].reshape(mats[k].shape)
            at += h

    vmem = pl.BlockSpec(memory_space=pltpu.VMEM)
    return pl.pallas_call(
        body, name="all_reduce_small",
        in_specs=[vmem] * len(arrays), out_specs=[vmem] * len(arrays),
        out_shape=[jax.ShapeDtypeStruct(a.shape, F32) for a in arrays],
        scratch_shapes=[pltpu.VMEM((n_rows, D_MODEL), F32), pltpu.VMEM((n_tall, LANES), F32),
                        pltpu.VMEM((2, n_rows, D_MODEL), F32), pltpu.VMEM((2, n_tall, LANES), F32),
                        pltpu.VMEM((N_CHIPS, n_rows, D_MODEL), F32), pltpu.VMEM((N_CHIPS, n_tall, LANES), F32),
                        pltpu.SemaphoreType.DMA((n_sems,)), pltpu.SemaphoreType.DMA((n_sems,))],
    )(*arrays)


def _adamw_small(items):
    n = len(items)
    bias1 = 1.0 - ADAM_B1 ** ADAM_STEP
    bias2 = 1.0 - ADAM_B2 ** ADAM_STEP

    def body(*refs):
        ins, outs = refs[:4 * n], refs[4 * n:]
        for k in range(n):
            w_ref, g_ref, m_ref, v_ref = ins[4 * k:4 * k + 4]
            gg = g_ref[...]
            m2 = ADAM_B1 * m_ref[...] + (1.0 - ADAM_B1) * gg
            v2 = ADAM_B2 * v_ref[...] + (1.0 - ADAM_B2) * (gg * gg)
            outs[3 * k + 1][...] = m2
            outs[3 * k + 2][...] = v2
            outs[3 * k][...] = -ADAM_LR * ((m2 / bias1) / (jnp.sqrt(v2 / bias2) + ADAM_EPS) + ADAM_WD * w_ref[...])

    vmem = pl.BlockSpec(memory_space=pltpu.VMEM)
    out = pl.pallas_call(
        body, name="adamw_small", in_specs=[vmem] * (4 * n), out_specs=[vmem] * (3 * n),
        out_shape=[jax.ShapeDtypeStruct(w.shape, F32) for w, _, _, _ in items for _ in range(3)],
    )(*[a for item in items for a in item])
    return [tuple(out[3 * k:3 * k + 3]) for k in range(n)]


ADAMW_STEPS = 8


def _adamw(items, name, exchange=None):
    n = len(items)
    bias1 = 1.0 - ADAM_B1 ** ADAM_STEP
    bias2 = 1.0 - ADAM_B2 ** ADAM_STEP

    def body(*refs):
        ins, outs = refs[:4 * n], refs[4 * n:]
        for k in range(n):
            w_ref, g_ref, m_ref, v_ref = ins[4 * k:4 * k + 4]
            d_ref, mo_ref, vo_ref = outs[3 * k:3 * k + 3]
            gg = g_ref[...]
            m2 = ADAM_B1 * m_ref[...] + (1.0 - ADAM_B1) * gg
            v2 = ADAM_B2 * v_ref[...] + (1.0 - ADAM_B2) * (gg * gg)
            mo_ref[...] = m2
            vo_ref[...] = v2
            d_ref[...] = -ADAM_LR * ((m2 / bias1) / (jnp.sqrt(v2 / bias2) + ADAM_EPS) + ADAM_WD * w_ref[...])

    in_specs, out_specs, out_shape, args = [], [], [], []
    for w, g, m, v in items:
        r, c = w.shape
        steps = ADAMW_STEPS if r % (8 * ADAMW_STEPS) == 0 else 1
        assert steps == ADAMW_STEPS or n == 1
        spec = pl.BlockSpec((r // steps, c), lambda i: (i, 0))
        in_specs += [spec] * 4
        out_specs += [spec] * 3
        out_shape += [jax.ShapeDtypeStruct((r, c), F32)] * 3
        args += [w, g, m, v]
    out, got = _call(body, args, name=name, grid=(steps,), in_specs=in_specs, out_specs=out_specs, out_shape=out_shape,
                     exchange=exchange, steps=(0, steps - 1))
    return [tuple(out[3 * k:3 * k + 3]) for k in range(n)], got


def kernel(x, p, ffn1_norm, ffn1_w_in, ffn1_w_out, mix_norm, w_mix_in, gmlp_v_norm, gmlp_w_s, gmlp_b, w_mix_out, ffn2_norm, ffn2_w_in, ffn2_w_out, ple_norm, ple_w_gate, ple_w_proj, final_norm, loss_target, m_ffn1_norm, m_ffn1_w_in, m_ffn1_w_out, m_mix_norm, m_w_mix_in, m_gmlp_v_norm, m_gmlp_w_s, m_gmlp_b, m_w_mix_out, m_ffn2_norm, m_ffn2_w_in, m_ffn2_w_out, m_ple_norm, m_ple_w_gate, m_ple_w_proj, m_final_norm, v_ffn1_norm, v_ffn1_w_in, v_ffn1_w_out, v_mix_norm, v_w_mix_in, v_gmlp_v_norm, v_gmlp_w_s, v_gmlp_b, v_w_mix_out, v_ffn2_norm, v_ffn2_w_in, v_ffn2_w_out, v_ple_norm, v_ple_w_gate, v_ple_w_proj, v_final_norm):
    args = dict(locals())
    w = {n: args[n] for n in _ALL}
    m = {n: args["m_" + n] for n in _ALL}
    v = {n: args["v_" + n] for n in _ALL}
    xi, yi, ci = _mesh_pos()
    pos = jnp.stack([2 * xi + yi, ci]).astype(jnp.int32)
    shard = {n: w[n][0] for n in _BIG}
    cast = {n: shard[n].astype(BF16) for n in _BIG}
    small = {n: (w[n][0] if w[n].ndim > 2 else w[n].reshape(1, -1)) for n in _SMALL}
    bt = small["gmlp_b"].T
    g_small, pair, from_chips = {}, {}, {}

    def pair_reduce(partials, tag, host=None):
        names = list(partials)
        blocks = lambda a, n: a.reshape(N_CHIPS, *shard[n].shape)
        exchange = _PairExchange([blocks(partials[n][1], n) for n in names])
        result, got = host(exchange) if host else (None, _run_exchange(exchange, "grad_pair_exchange_" + tag))
        pair.update(zip(names, _pair_sums([blocks(partials[n][0], n) for n in names], got, pos, "pair_sums_" + tag)))
        return names, result

    w1in, w1out = _run_exchange(_WeightGather([cast["ffn1_w_in"], cast["ffn1_w_out"]]), "gather_ffn1")
    w1out = w1out.reshape(D_FF, D_MODEL)
    (h1, gu1), (wmix, wmo) = _ffn_fwd(x[0], small["ffn1_norm"], w1in, w1out, "ffn1_fwd",
                                      _WeightGather([cast["w_mix_in"], cast["w_mix_out"]]))
    wmo = wmo.reshape(D_MODEL, D_MODEL)
    zg, qkv = _mix_in_fwd(h1, small["mix_norm"], wmix)
    gm = _gmlp_fwd(zg, small["gmlp_v_norm"], small["gmlp_w_s"], bt)
    (mixed, carries), (w2in, w2out, wg, wproj) = _attn_fwd(
        qkv, gm, _WeightGather([cast["ffn2_w_in"], cast["ffn2_w_out"], cast["ple_w_gate"], cast["ple_w_proj"]]))
    w2out = w2out.reshape(D_FF, D_MODEL)
    wg = wg.reshape(D_MODEL, D_MODEL)
    h2 = _matmul_residual(h1, mixed, wmo, "mix_out_fwd")
    (h3, gu2), _ = _ffn_fwd(h2, small["ffn2_norm"], w2in, w2out, "ffn2_fwd")
    loss_part, g_small["final_norm"], g_small["ple_norm"], dh3, dgp, dpp, n4, pb = _head(
        h3, p[0, 0], loss_target[0], small["ple_norm"], small["final_norm"], wg, wproj)

    part = {"ple_w_gate": _wgrad_rows(n4, dgp, N_CHIPS, "wgrad_ple_gate"),
            "ple_w_proj": _wgrad_cols(pb, dpp, N_CHIPS, "wgrad_ple_proj")}
    (dh2, dgu2, n3, act2, dhh3, g_small["ffn2_norm"]), _ = _ffn_bwd(dh3, h2, small["ffn2_norm"], gu2, w2in, w2out,
                                                                   "ffn2_bwd")
    part["ffn2_w_in"] = _wgrad_cols(n3, dgu2, N_CHIPS, "wgrad_ffn2_in")
    part["ffn2_w_out"] = _wgrad_rows(act2, dhh3, 2, "wgrad_ffn2_out")
    dmixed, dh2b = _matmul_nt_cast(dh2, wmo, "mix_out_bwd")
    part["w_mix_out"] = _wgrad_rows(mixed, dh2b, 2, "wgrad_mix_out")
    group, (dzg, g_small["gmlp_w_s"], dbt, g_small["gmlp_v_norm"]) = pair_reduce(
        part, "late", lambda ex: _gmlp_bwd(zg, dmixed, small["gmlp_v_norm"], small["gmlp_w_s"], bt, ex))
    g_small["gmlp_b"] = dbt.T
    (dq, dk, dv), got = _attn_bwd(qkv, dmixed, carries, _ChipExchange([pair[n] for n in group]))
    from_chips.update(zip(group, got))

    dzmix = jnp.concatenate([dzg, dq, dk, dv], axis=1)
    dh1, n2, g_small["mix_norm"] = _norm_input_bwd(dh2, dzmix, wmix, h1, small["mix_norm"], "mix_in_bwd")
    group, _ = pair_reduce({"w_mix_in": _wgrad_cols(n2, dzmix, N_CHIPS, "wgrad_mix_in")}, "mix")
    (dx, dgu1, n1, act1, dhh1, g_small["ffn1_norm"]), _ = _ffn_bwd(dh1, x[0], small["ffn1_norm"], gu1, w1in, w1out,
                                                                   "ffn1_bwd")

    g_out, got = _wgrad_rows(act1, dhh1, 2, "wgrad_ffn1_out", _ChipExchange([pair[n] for n in group]))
    from_chips.update(zip(group, got))
    pair_reduce({"ffn1_w_out": g_out}, "out")
    g_in, got = _wgrad_cols(n1, dgu1, N_CHIPS, "wgrad_ffn1_in", _ChipExchange([pair["ffn1_w_out"]]))
    from_chips["ffn1_w_out"] = got[0]
    pair_reduce({"ffn1_w_in": g_in}, "in")

    def finish(names, tag, exchange=None):
        halves = _chip_sums([pair[n] for n in names], [from_chips[n] for n in names], pos, "chip_sums_" + tag)
        full = _pair_share(halves, "grad_pair_share_" + tag)
        out, got = _adamw([(shard[n], g, m[n][0], v[n][0]) for n, g in zip(names, full)], "adamw_" + tag, exchange)
        for n, g, (d2, m2, v2) in zip(names, full, out):
            grads[n], delta[n], new_m[n], new_v[n] = g[None], d2[None], m2[None], v2[None]
        return got

    grads, delta, new_m, new_v = {}, {}, {}, {}
    got = finish([n for n in _BIG if n != "ffn1_w_in"], "most", _ChipExchange([pair["ffn1_w_in"]]))
    from_chips["ffn1_w_in"] = got[0]
    finish(["ffn1_w_in"], "last")

    rows = [n for n in _SMALL if g_small[n].shape[0] == 1]
    mats = [n for n in _SMALL if n not in rows]
    summed = _all_reduce_small([g_small[n] for n in rows] + [loss_part], [g_small[n] for n in mats])
    loss = summed[len(rows)][0, 0]
    g_sum = dict(zip(rows + mats, summed[:len(rows)] + summed[len(rows) + 1:]))
    like = lambda a, n: a[n].reshape(small[n].shape)
    out = _adamw_small([(small[n], g_sum[n], like(m, n), like(v, n)) for n in _SMALL])
    for n, (d2, m2, v2) in zip(_SMALL, out):
        grads[n], delta[n], new_m[n], new_v[n] = (a.reshape(w[n].shape) for a in (g_sum[n], d2, m2, v2))

    return (loss, dx[None], *[grads[n] for n in _ALL], *[delta[n] for n in _ALL], *[new_m[n] for n in _ALL],
            *[new_v[n] for n in _ALL])
```
